```python
import jax, jax.numpy as jnp
from jax import lax
import numpy as np

D_MODEL = 1024
BATCH = 8
SEQ = 4096
DEPTH = 1

D_MIX = D_MODEL
SWA_WIDTH = D_MIX // 2
SWA_HEAD_DIM = 64
SWA_Q_HEADS = SWA_WIDTH // SWA_HEAD_DIM
SWA_KV_HEADS = 2
SWA_GROUP = SWA_Q_HEADS // SWA_KV_HEADS
WINDOW = 128
BLOCK = 128
ROPE_THETA = 500000.0
ROT_DIM = SWA_HEAD_DIM // 4
GLA_WIDTH = D_MIX - SWA_WIDTH
GLA_HEADS = 4
GLA_DK = GLA_WIDTH // 2 // GLA_HEADS
GLA_DV = GLA_WIDTH // GLA_HEADS
GLA_RANK = 16
GLA_TAU = 16.0
GLA_CHUNK = 64
IN_SPLITS = (
    SWA_Q_HEADS * SWA_HEAD_DIM,
    SWA_KV_HEADS * SWA_HEAD_DIM,
    SWA_KV_HEADS * SWA_HEAD_DIM,
    SWA_WIDTH,
    GLA_HEADS * GLA_DK,
    GLA_HEADS * GLA_DK,
    GLA_HEADS * GLA_DV,
    GLA_WIDTH,
    GLA_RANK,
)
D_IN_PROJ = sum(IN_SPLITS)
EPS = 1e-5
ALPHA = (2 * DEPTH) ** 0.25
BETA = (8 * DEPTH) ** -0.25

kernel_name = 'hymba_swa_sink_gla_deepnorm'


def split_cols(t, sizes):
    out, start = [], 0
    for s in sizes:
        out.append(t[..., start:start + s])
        start += s
    return out


def partial_rope(t, pos):
    half = ROT_DIM // 2
    inv_freq = ROPE_THETA ** (-jnp.arange(half, dtype=jnp.float32) / half)
    ang = pos.astype(jnp.float32)[..., None] * inv_freq
    cos = jnp.cos(ang)[:, :, None, :].astype(t.dtype)
    sin = jnp.sin(ang)[:, :, None, :].astype(t.dtype)
    t1 = t[..., :half]
    t2 = t[..., half:ROT_DIM]
    return jnp.concatenate([t1 * cos - t2 * sin, t2 * cos + t1 * sin, t[..., ROT_DIM:]], axis=-1)


def sliding_window_attention(q, k, v, sinks):
    B, S = q.shape[0], q.shape[1]
    nb = S // BLOCK
    qb = q.reshape(B, nb, BLOCK, SWA_KV_HEADS, SWA_GROUP, SWA_HEAD_DIM)

    def with_prev(t):
        tb = t.reshape(B, nb, BLOCK, SWA_KV_HEADS, SWA_HEAD_DIM)
        prev = jnp.concatenate([jnp.zeros_like(tb[:, :1]), tb[:, :-1]], axis=1)
        return jnp.concatenate([prev, tb], axis=2)

    kb = with_prev(k)
    vb = with_prev(v)
    scale = SWA_HEAD_DIM ** -0.5
    scores = jnp.einsum('bnqhgd,bnkhd->bnhgqk', qb, kb).astype(jnp.float32) * scale
    qi = jnp.arange(BLOCK)[:, None]
    ki = jnp.arange(2 * BLOCK)[None, :]
    dist = qi + BLOCK - ki
    in_window = (dist >= 0) & (dist < WINDOW)
    has_prev = (jnp.arange(nb)[:, None, None] > 0) | (ki >= BLOCK)[None]
    mask = in_window[None] & has_prev
    scores = jnp.where(mask[None, :, None, None], scores, -jnp.inf)
    sink = sinks.astype(jnp.float32).reshape(SWA_KV_HEADS, SWA_GROUP)[None, None, :, :, None, None]
    m = jnp.maximum(scores.max(axis=-1, keepdims=True), sink)
    p = jnp.exp(scores - m)
    denom = p.sum(axis=-1, keepdims=True) + jnp.exp(sink - m)
    probs = (p / denom).astype(v.dtype)
    out = jnp.einsum('bnhgqk,bnkhd->bnqhgd', probs, vb)
    return out.reshape(B, S, SWA_Q_HEADS * SWA_HEAD_DIM)


def gla_chunked(q, k, v, log_a):
    B, S = q.shape[0], q.shape[1]
    C = GLA_CHUNK
    nc = S // C

    def chunks(t):
        return t.reshape(B, nc, C, t.shape[2], t.shape[3]).astype(jnp.float32)

    qc = chunks(q) * (GLA_DK ** -0.5)
    kc = chunks(k)
    vc = chunks(v)
    b = jnp.cumsum(chunks(log_a), axis=2)
    b_last = b[:, :, -1:]
    q_dec = qc * jnp.exp(b)
    k_inv = kc * jnp.exp(-b)
    k_to_end = kc * jnp.exp(b_last - b)
    causal = jnp.tril(jnp.ones((C, C), dtype=bool))
    attn = jnp.einsum('bnihd,bnjhd->bnhij', q_dec, k_inv)
    attn = jnp.where(causal, attn, 0.0)
    o_intra = jnp.einsum('bnhij,bnjhv->bnihv', attn, vc)
    upd = jnp.einsum('bnjhd,bnjhv->bnhdv', k_to_end, vc)
    decay = jnp.exp(b_last[:, :, 0])

    def step(state, inp):
        dec, u = inp
        return state * dec[..., None] + u, state

    init = jnp.zeros((B, GLA_HEADS, GLA_DK, GLA_DV), jnp.float32)
    _, s_prev = lax.scan(step, init, (jnp.swapaxes(decay, 0, 1), jnp.swapaxes(upd, 0, 1)))
    s_prev = jnp.swapaxes(s_prev, 0, 1)
    o_inter = jnp.einsum('bnihd,bnhdv->bnihv', q_dec, s_prev)
    return (o_intra + o_inter).reshape(B, S, GLA_HEADS, GLA_DV)


def layer_norm(t, g, b):
    tf = t.astype(jnp.float32)
    mu = tf.mean(axis=-1, keepdims=True)
    var = jnp.square(tf - mu).mean(axis=-1, keepdims=True)
    return ((tf - mu) * lax.rsqrt(var + EPS) * g.astype(jnp.float32) + b.astype(jnp.float32)).astype(t.dtype)


def _fwd_setup_inputs(seed: int = 0) -> dict:
    key = jax.random.key(seed)
    ks = jax.random.split(key, 12)
    x = jax.random.normal(ks[0], (BATCH, SEQ, D_MODEL), jnp.float32)
    offset = jax.random.randint(ks[1], (BATCH, 1), 0, 1024, dtype=jnp.int32)
    positions = offset + jnp.arange(SEQ, dtype=jnp.int32)[None, :]
    w_in = jax.random.normal(ks[2], (DEPTH, D_MODEL, D_IN_PROJ), jnp.float32) * D_MODEL ** -0.5
    starts = np.cumsum((0,) + IN_SPLITS)
    col_scale = np.ones((D_IN_PROJ,), np.float32)
    col_scale[starts[2]:starts[3]] = BETA
    col_scale[starts[6]:starts[7]] = BETA
    w_in = w_in * jnp.asarray(col_scale)
    gla_w_gate_up = jax.random.normal(ks[3], (DEPTH, GLA_RANK, GLA_HEADS * GLA_DK), jnp.float32) * GLA_RANK ** -0.5
    gla_b_gate = 0.01 * jax.random.normal(ks[4], (DEPTH, GLA_HEADS * GLA_DK), jnp.float32)
    attn_sinks = 0.5 * jax.random.normal(ks[5], (DEPTH, SWA_Q_HEADS), jnp.float32)
    gla_norm_w = 1.0 + 0.01 * jax.random.normal(ks[6], (DEPTH, GLA_DV), jnp.float32)
    w_out = jax.random.normal(ks[7], (DEPTH, D_MIX, D_MODEL), jnp.float32) * (D_MIX ** -0.5) * BETA
    ln_g = 1.0 + 0.01 * jax.random.normal(ks[8], (DEPTH, D_MODEL), jnp.float32)
    ln_b = 0.01 * jax.random.normal(ks[9], (DEPTH, D_MODEL), jnp.float32)
    return {'x': x, 'positions': positions, 'w_in': w_in, 'gla_w_gate_up': gla_w_gate_up,
            'gla_b_gate': gla_b_gate, 'attn_sinks': attn_sinks, 'gla_norm_w': gla_norm_w,
            'w_out': w_out, 'ln_g': ln_g, 'ln_b': ln_b}


def _fwd_reference(x, positions, w_in, gla_w_gate_up, gla_b_gate, attn_sinks, gla_norm_w, w_out, ln_g, ln_b):
    B, S = x.shape[0], x.shape[1]
    for layer in range(DEPTH):
        proj = jnp.einsum('bsd,de->bse', x, w_in[layer])
        q_a, k_a, v_a, g_a, q_b, k_b, v_b, g_b, r_b = split_cols(proj, IN_SPLITS)
        q_a = partial_rope(q_a.reshape(B, S, SWA_Q_HEADS, SWA_HEAD_DIM), positions)
        k_a = partial_rope(k_a.reshape(B, S, SWA_KV_HEADS, SWA_HEAD_DIM), positions)
        v_a = v_a.reshape(B, S, SWA_KV_HEADS, SWA_HEAD_DIM)
        out_a = sliding_window_attention(q_a, k_a, v_a, attn_sinks[layer]) * jax.nn.silu(g_a)
        gate_logit = jnp.einsum('bsr,re->bse', r_b, gla_w_gate_up[layer]) + gla_b_gate[layer]
        log_a = jax.nn.log_sigmoid(gate_logit.astype(jnp.float32)) / GLA_TAU
        o_b = gla_chunked(q_b.reshape(B, S, GLA_HEADS, GLA_DK),
                          k_b.reshape(B, S, GLA_HEADS, GLA_DK),
                          v_b.reshape(B, S, GLA_HEADS, GLA_DV),
                          log_a.reshape(B, S, GLA_HEADS, GLA_DK))
        o_b = o_b * lax.rsqrt(jnp.mean(jnp.square(o_b), axis=-1, keepdims=True) + EPS) * gla_norm_w[layer].astype(jnp.float32)
        out_b = o_b.reshape(B, S, GLA_WIDTH).astype(x.dtype) * jax.nn.silu(g_b)
        mix = jnp.einsum('bse,ed->bsd', jnp.concatenate([out_a, out_b], axis=-1), w_out[layer])
        x = layer_norm(ALPHA * x + mix, ln_g[layer], ln_b[layer])
    return x


import jax as _jax
import jax.numpy as _jnp

TWIN_FORMAT = 'train_step'
FWD_PARAMS = ['x', 'positions', 'w_in', 'gla_w_gate_up', 'gla_b_gate', 'attn_sinks', 'gla_norm_w', 'w_out', 'ln_g', 'ln_b']
TWIN_WEIGHTS = ['w_in', 'gla_w_gate_up', 'gla_b_gate', 'attn_sinks', 'gla_norm_w', 'w_out', 'ln_g', 'ln_b']
TWIN_DIFF_INPUT = 'x'
TWIN_INPUTS = ['x', 'positions', 'w_in', 'gla_w_gate_up', 'gla_b_gate', 'attn_sinks', 'gla_norm_w', 'w_out', 'ln_g', 'ln_b', 'loss_target', 'm_w_in', 'm_gla_w_gate_up', 'm_gla_b_gate', 'm_attn_sinks', 'm_gla_norm_w', 'm_w_out', 'm_ln_g', 'm_ln_b', 'v_w_in', 'v_gla_w_gate_up', 'v_gla_b_gate', 'v_attn_sinks', 'v_gla_norm_w', 'v_w_out', 'v_ln_g', 'v_ln_b']
TWIN_OUTPUTS = ['loss', 'grad_x', 'grad_w_in', 'grad_gla_w_gate_up', 'grad_gla_b_gate', 'grad_attn_sinks', 'grad_gla_norm_w', 'grad_w_out', 'grad_ln_g', 'grad_ln_b', 'delta_w_in', 'delta_gla_w_gate_up', 'delta_gla_b_gate', 'delta_attn_sinks', 'delta_gla_norm_w', 'delta_w_out', 'delta_ln_g', 'delta_ln_b', 'new_m_w_in', 'new_m_gla_w_gate_up', 'new_m_gla_b_gate', 'new_m_attn_sinks', 'new_m_gla_norm_w', 'new_m_w_out', 'new_m_ln_g', 'new_m_ln_b', 'new_v_w_in', 'new_v_gla_w_gate_up', 'new_v_gla_b_gate', 'new_v_attn_sinks', 'new_v_gla_norm_w', 'new_v_w_out', 'new_v_ln_g', 'new_v_ln_b']
TWIN_LEAF_KINDS = {'loss': 'loss', 'grad_x': 'grad_x', 'grad_w_in': 'grad_w', 'grad_gla_w_gate_up': 'grad_w', 'grad_gla_b_gate': 'grad_w', 'grad_attn_sinks': 'grad_w', 'grad_gla_norm_w': 'grad_w', 'grad_w_out': 'grad_w', 'grad_ln_g': 'grad_w', 'grad_ln_b': 'grad_w', 'delta_w_in': 'delta_w', 'delta_gla_w_gate_up': 'delta_w', 'delta_gla_b_gate': 'delta_w', 'delta_attn_sinks': 'delta_w', 'delta_gla_norm_w': 'delta_w', 'delta_w_out': 'delta_w', 'delta_ln_g': 'delta_w', 'delta_ln_b': 'delta_w', 'new_m_w_in': 'new_m', 'new_m_gla_w_gate_up': 'new_m', 'new_m_gla_b_gate': 'new_m', 'new_m_attn_sinks': 'new_m', 'new_m_gla_norm_w': 'new_m', 'new_m_w_out': 'new_m', 'new_m_ln_g': 'new_m', 'new_m_ln_b': 'new_m', 'new_v_w_in': 'new_v', 'new_v_gla_w_gate_up': 'new_v', 'new_v_gla_b_gate': 'new_v', 'new_v_attn_sinks': 'new_v', 'new_v_gla_norm_w': 'new_v', 'new_v_w_out': 'new_v', 'new_v_ln_g': 'new_v', 'new_v_ln_b': 'new_v'}


def _forward(args):
    return _fwd_reference(*[args[k] for k in FWD_PARAMS])


def _output_shape():
    def fwd():
        inp = _fwd_setup_inputs(0)
        return _fwd_reference(*[inp[k] for k in FWD_PARAMS])
    out = _jax.eval_shape(fwd)
    return out.shape, out.dtype

N_MICROBATCH = 1
ADAM_LR = 0.001
ADAM_B1 = 0.9
ADAM_B2 = 0.999
ADAM_EPS = 1e-08
ADAM_WD = 0.01
ADAM_STEP = 10
PER_EXAMPLE_BATCH_AXIS = {'x': 0, 'positions': 0, 'loss_target': 0}
SHARED_INPUTS = []
_WEIGHT_DTYPES = {'w_in': _jnp.float32, 'gla_w_gate_up': _jnp.float32, 'gla_b_gate': _jnp.float32, 'attn_sinks': _jnp.float32, 'gla_norm_w': _jnp.float32, 'w_out': _jnp.float32, 'ln_g': _jnp.float32, 'ln_b': _jnp.float32}
MOMENT_SCALE = {'w_in': 5.532365e-02, 'gla_w_gate_up': 8.293123e-03, 'gla_b_gate': 3.108072e-02, 'attn_sinks': 3.789245e-03, 'gla_norm_w': 9.922946e-02, 'w_out': 6.166514e-02, 'ln_g': 3.200548e+01, 'ln_b': 3.488280e-01}


def _to_microbatches(a, axis):
    t = _jnp.moveaxis(a, axis, 0)
    t = t.reshape((N_MICROBATCH, t.shape[0] // N_MICROBATCH) + t.shape[1:])
    return _jnp.moveaxis(t, 1, axis + 1)


def setup_inputs(seed: int = 0) -> dict:
    inp = _fwd_setup_inputs(seed)
    key = _jax.random.fold_in(_jax.random.key(seed), 7919)
    shape, _ = _output_shape()
    out = dict(inp)
    out["loss_target"] = _jax.random.normal(_jax.random.fold_in(key, 0), shape, _jnp.float32)
    for i, name in enumerate(TWIN_WEIGHTS):
        w = inp[name].astype(_jnp.float32)
        if MOMENT_SCALE is None:
            s = _jnp.sqrt(_jnp.mean(_jnp.square(w)) + 1e-30)
        else:
            s = MOMENT_SCALE[name]
        km, kv = _jax.random.split(_jax.random.fold_in(key, i + 1))
        out[name] = w
        out["m_" + name] = s * _jax.random.normal(km, w.shape, _jnp.float32)
        out["v_" + name] = (s * s) * _jax.random.uniform(kv, w.shape, _jnp.float32, 0.5, 1.5)
    if N_MICROBATCH > 1:
        for name, axis in PER_EXAMPLE_BATCH_AXIS.items():
            out[name] = _to_microbatches(out[name], axis)
    return {'x': out['x'], 'positions': out['positions'], 'w_in': out['w_in'], 'gla_w_gate_up': out['gla_w_gate_up'], 'gla_b_gate': out['gla_b_gate'], 'attn_sinks': out['attn_sinks'], 'gla_norm_w': out['gla_norm_w'], 'w_out': out['w_out'], 'ln_g': out['ln_g'], 'ln_b': out['ln_b'], 'loss_target': out['loss_target'], 'm_w_in': out['m_w_in'], 'm_gla_w_gate_up': out['m_gla_w_gate_up'], 'm_gla_b_gate': out['m_gla_b_gate'], 'm_attn_sinks': out['m_attn_sinks'], 'm_gla_norm_w': out['m_gla_norm_w'], 'm_w_out': out['m_w_out'], 'm_ln_g': out['m_ln_g'], 'm_ln_b': out['m_ln_b'], 'v_w_in': out['v_w_in'], 'v_gla_w_gate_up': out['v_gla_w_gate_up'], 'v_gla_b_gate': out['v_gla_b_gate'], 'v_attn_sinks': out['v_attn_sinks'], 'v_gla_norm_w': out['v_gla_norm_w'], 'v_w_out': out['v_w_out'], 'v_ln_g': out['v_ln_g'], 'v_ln_b': out['v_ln_b']}


def _loss(weights, diff, rest, loss_target):
    with _jax.named_scope("forward"):
        args = {**rest, TWIN_DIFF_INPUT: diff, **{k: w.astype(_WEIGHT_DTYPES[k]) for k, w in weights.items()}}
        y = _forward(args)
    with _jax.named_scope("loss_head"):
        err = _jnp.square(y.astype(_jnp.float32) - loss_target)
        return 0.5 * _jnp.sum(_jnp.mean(err, axis=-1)) if err.ndim else 0.5 * err


def _adamw(w, g, m, v):
    m = ADAM_B1 * m + (1.0 - ADAM_B1) * g
    v = ADAM_B2 * v + (1.0 - ADAM_B2) * _jnp.square(g)
    m_hat = m / (1.0 - ADAM_B1 ** ADAM_STEP)
    v_hat = v / (1.0 - ADAM_B2 ** ADAM_STEP)
    delta = -ADAM_LR * (m_hat / (_jnp.sqrt(v_hat) + ADAM_EPS) + ADAM_WD * w)
    return delta, m, v


def reference(x, positions, w_in, gla_w_gate_up, gla_b_gate, attn_sinks, gla_norm_w, w_out, ln_g, ln_b, loss_target, m_w_in, m_gla_w_gate_up, m_gla_b_gate, m_attn_sinks, m_gla_norm_w, m_w_out, m_ln_g, m_ln_b, v_w_in, v_gla_w_gate_up, v_gla_b_gate, v_attn_sinks, v_gla_norm_w, v_w_out, v_ln_g, v_ln_b):
    given = dict(x=x, positions=positions, w_in=w_in, gla_w_gate_up=gla_w_gate_up, gla_b_gate=gla_b_gate, attn_sinks=attn_sinks, gla_norm_w=gla_norm_w, w_out=w_out, ln_g=ln_g, ln_b=ln_b, loss_target=loss_target, m_w_in=m_w_in, m_gla_w_gate_up=m_gla_w_gate_up, m_gla_b_gate=m_gla_b_gate, m_attn_sinks=m_attn_sinks, m_gla_norm_w=m_gla_norm_w, m_w_out=m_w_out, m_ln_g=m_ln_g, m_ln_b=m_ln_b, v_w_in=v_w_in, v_gla_w_gate_up=v_gla_w_gate_up, v_gla_b_gate=v_gla_b_gate, v_attn_sinks=v_attn_sinks, v_gla_norm_w=v_gla_norm_w, v_w_out=v_w_out, v_ln_g=v_ln_g, v_ln_b=v_ln_b)
    weights = {n: given[n] for n in TWIN_WEIGHTS}
    shared = {n: given[n] for n in SHARED_INPUTS}
    per_example = {n: given[n] for n in ['x', 'positions']}
    grad_fn = _jax.value_and_grad(_loss, argnums=(0, 1))

    def one_microbatch(ex, loss_target):
        ex = dict(ex)
        diff = ex.pop(TWIN_DIFF_INPUT)
        return grad_fn(weights, diff, {**shared, **ex}, loss_target)

    if N_MICROBATCH == 1:
        loss, (grad_w, grad_x) = one_microbatch(per_example, given["loss_target"])
    else:
        def body(carry, xs):
            loss_sum, grad_sum = carry
            l_k, (gw_k, gx_k) = one_microbatch(xs[0], xs[1])
            with _jax.named_scope("update"):
                return (loss_sum + l_k, _jax.tree.map(_jnp.add, grad_sum, gw_k)), gx_k

        init = (_jnp.zeros((), _jnp.float32), _jax.tree.map(_jnp.zeros_like, weights))
        (loss, grad_w), grad_x = _jax.lax.scan(body, init, (per_example, given["loss_target"]))
    with _jax.named_scope("update"):
        delta_w, new_m, new_v = {}, {}, {}
        for n in TWIN_WEIGHTS:
            delta_w[n], new_m[n], new_v[n] = _adamw(weights[n], grad_w[n], given["m_" + n], given["v_" + n])
    return (loss, grad_x, *[grad_w[n] for n in TWIN_WEIGHTS], *[delta_w[n] for n in TWIN_WEIGHTS],
            *[new_m[n] for n in TWIN_WEIGHTS], *[new_v[n] for n in TWIN_WEIGHTS])
```

```python
import functools

import jax
import jax.numpy as jnp
from jax import lax
from jax.experimental import pallas as pl
from jax.experimental.pallas import tpu as pltpu

F32 = jnp.float32
BF16 = jnp.bfloat16
MESH = pl.DeviceIdType.MESH

D_MODEL = 1024
N_CHIPS = 4
W_QA, W_KA, W_VA, W_GA, W_QB, W_KB, W_VB, W_GB, W_R = 512, 128, 128, 512, 256, 256, 512, 512, 16
O_QA = 0
O_KA = O_QA + W_QA
O_VA = O_KA + W_KA
O_GA = O_VA + W_VA
O_QB = O_GA + W_GA
O_KB = O_QB + W_QB
O_VB = O_KB + W_KB
O_GB = O_VB + W_VB
O_R = O_GB + W_GB
D_PROJ = O_R + W_R
SHARD_IN = D_PROJ // N_CHIPS
SHARD_OUT = D_MODEL // N_CHIPS

HEAD_A = 64
Q_HEADS = 8
KV_HEADS = 2
GROUP = 4
BLOCK = 128
GLA_HEADS = 4
GLA_DK = 64
GLA_DV = 128
CHUNK = 64
GLA_TAU = 16.0
EPS = 1e-5
ALPHA = 2.0 ** 0.25
ATT_SCALE = HEAD_A ** -0.5
GLA_SCALE = GLA_DK ** -0.5

ADAM_LR = 0.001
ADAM_B1 = 0.9
ADAM_B2 = 0.999
ADAM_EPS = 1e-08
ADAM_WD = 0.01
ADAM_STEP = 10

TM = 256
VMEM_LIMIT = 56 * 1024 * 1024

P_LNG, P_LNB, P_BG, P_NW, P_SINK, P_LOSS, P_GU = 0, 8, 16, 18, 19, 20, 24
PACK_ROWS = P_GU + N_CHIPS * 16
PACK2_ROWS = P_GU + 16


def _mm(a, b):
    return jnp.dot(a, b, preferred_element_type=F32)


def _mm_nt(a, b):
    return lax.dot_general(a, b, (((1,), (1,)), ((), ())), preferred_element_type=F32)


def _mm_tn(a, b):
    return lax.dot_general(a, b, (((0,), (0,)), ((), ())), preferred_element_type=F32)


def _split3(a):
    hi = a.astype(BF16)
    r1 = a - hi.astype(F32)
    mid = r1.astype(BF16)
    lo = (r1 - mid.astype(F32)).astype(BF16)
    return hi, mid, lo


def _tri_mm(tri, a):
    hi, mid, lo = _split3(a)
    return _mm(tri, hi) + _mm(tri, mid) + _mm(tri, lo)


def _chunk_tri(n, upper):
    r = lax.broadcasted_iota(jnp.int32, (n, n), 0)
    c = lax.broadcasted_iota(jnp.int32, (n, n), 1)
    same = (r >> 6) == (c >> 6)
    order = (c >= r) if upper else (c <= r)
    return jnp.where(same & order, 1.0, 0.0).astype(BF16)


def _rope(t, cos, sa, sb):
    w = t.shape[1]
    return t * cos + pltpu.roll(t, w - 8, 1) * sa + pltpu.roll(t, 8, 1) * sb


def _rope_bwd(d, cos, sa, sb):
    w = d.shape[1]
    return d * cos + pltpu.roll(d * sa, 8, 1) + pltpu.roll(d * sb, w - 8, 1)


def _log_sigmoid(z):
    return jnp.minimum(z, 0.0) - jnp.log1p(jnp.exp(-jnp.abs(z)))


def _sigmoid(z):
    return 1.0 / (1.0 + jnp.exp(-z))


def _attn_mask(has_prev):
    r = lax.broadcasted_iota(jnp.int32, (GROUP * BLOCK, 2 * BLOCK), 0) & (BLOCK - 1)
    k = lax.broadcasted_iota(jnp.int32, (GROUP * BLOCK, 2 * BLOCK), 1)
    first_key = jnp.where(has_prev, 0, BLOCK)
    return (k > r) & (k <= r + BLOCK) & (k >= first_key)


def _sink_col(sinks_ref, j):
    r = lax.broadcasted_iota(jnp.int32, (GROUP * BLOCK, 1), 0) >> 7
    col = jnp.full((GROUP * BLOCK, 1), sinks_ref[GROUP * j], F32)
    for g in range(1, GROUP):
        col = jnp.where(r == g, sinks_ref[GROUP * j + g], col)
    return col


def _stack_heads(t, j):
    return jnp.concatenate([t[:, (GROUP * j + g) * HEAD_A:(GROUP * j + g + 1) * HEAD_A] for g in range(GROUP)], axis=0)


def _unstack_heads(parts):
    return jnp.concatenate([parts[j][g * BLOCK:(g + 1) * BLOCK] for j in range(KV_HEADS) for g in range(GROUP)], axis=1)


def _softmax_block(qs, kc, mask, sink):
    s = _mm_nt(qs, kc) * ATT_SCALE
    s = jnp.where(mask, s, -jnp.inf)
    m = jnp.maximum(jnp.max(s, axis=1, keepdims=True), sink)
    p = jnp.exp(s - m)
    e_sink = jnp.exp(sink - m)
    inv = 1.0 / (jnp.sum(p, axis=1, keepdims=True) + e_sink)
    return p * inv, e_sink * inv


def _fwd_call(x, tgt, cos, sa, sb, win, wout, wgu, bg, sinks, nw, lng, lnb):
    s_len = x.shape[0]
    nt = s_len // TM
    nblk = TM // BLOCK
    nch = TM // CHUNK

    def body(x_ref, t_ref, cos_ref, sa_ref, sb_ref, win_ref, wout_ref, wgu_ref, bg_ref, sinks_ref, nw_ref,
             lng_ref, lnb_ref,
             qa_ref, ka_ref, va_ref, qb_ref, kb_ref, vb_ref, r_ref, dattn_ref, dga_ref, dob_ref, dgb_ref, dh_ref,
             st_ref, dwout_ref, glng_ref, glnb_ref, gnw_ref, loss_ref,
             kprev, vprev, state, attn_s, ga_s, ob_s, gb_s, cat_s):
        i = pl.program_id(0)

        @pl.when(i == 0)
        def _():
            kprev[...] = jnp.zeros_like(kprev)
            vprev[...] = jnp.zeros_like(vprev)
            state[...] = jnp.zeros_like(state)
            dwout_ref[...] = jnp.zeros_like(dwout_ref)
            glng_ref[...] = jnp.zeros_like(glng_ref)
            glnb_ref[...] = jnp.zeros_like(glnb_ref)
            gnw_ref[...] = jnp.zeros_like(gnw_ref)
            loss_ref[...] = jnp.zeros_like(loss_ref)

        x = x_ref[...]
        xb = x.astype(BF16)

        def proj(off, width):
            return _mm(xb, win_ref[:, off:off + width])

        cos = cos_ref[...]
        sa = sa_ref[...]
        sb = sb_ref[...]
        cos4, sa4, sb4 = (jnp.concatenate([t] * 4, axis=1) for t in (cos, sa, sb))
        qa = _rope(proj(O_QA, W_QA), cos4, sa4, sb4).astype(BF16)
        ka = _rope(proj(O_KA, W_KA), cos, sa, sb).astype(BF16)
        va = proj(O_VA, W_VA).astype(BF16)
        qa_ref[...] = qa
        ka_ref[...] = ka
        va_ref[...] = va
        ga_s[...] = proj(O_GA, W_GA)
        gb_s[...] = proj(O_GB, W_GB)

        for b in range(nblk):
            rows = slice(b * BLOCK, (b + 1) * BLOCK)
            has_prev = (i * nblk + b) > 0
            mask = _attn_mask(has_prev)
            k_cur = ka[rows]
            v_cur = va[rows]
            k_old = kprev[...] if b == 0 else ka[(b - 1) * BLOCK:b * BLOCK]
            v_old = vprev[...] if b == 0 else va[(b - 1) * BLOCK:b * BLOCK]
            outs = []
            for j in range(KV_HEADS):
                hs = slice(j * HEAD_A, (j + 1) * HEAD_A)
                kc = jnp.concatenate([k_old[:, hs], k_cur[:, hs]], axis=0)
                vc = jnp.concatenate([v_old[:, hs], v_cur[:, hs]], axis=0)
                probs, _ = _softmax_block(_stack_heads(qa[rows], j), kc, mask, _sink_col(sinks_ref, j))
                outs.append(_mm(probs.astype(BF16), vc))
            attn_s[rows, :] = _unstack_heads(outs)
        kprev[...] = ka[(nblk - 1) * BLOCK:]
        vprev[...] = va[(nblk - 1) * BLOCK:]

        r = proj(O_R, W_R)
        r_ref[...] = r
        z = _mm(r.astype(BF16), wgu_ref[...].astype(BF16)) + bg_ref[...]
        log_a = _log_sigmoid(z) / GLA_TAU
        bcum = _tri_mm(_chunk_tri(TM, False), log_a)
        qb = proj(O_QB, W_QB)
        kb = proj(O_KB, W_KB)
        vb = proj(O_VB, W_VB).astype(BF16)
        qb_ref[...] = qb
        kb_ref[...] = kb
        vb_ref[...] = vb
        qd_all = (qb * GLA_SCALE * jnp.exp(bcum)).astype(BF16)
        ki_all = (kb * jnp.exp(-bcum)).astype(BF16)
        tril = lax.broadcasted_iota(jnp.int32, (CHUNK, CHUNK), 0) >= lax.broadcasted_iota(jnp.int32, (CHUNK, CHUNK), 1)
        for c in range(nch):
            rows = slice(c * CHUNK, (c + 1) * CHUNK)
            b_c = bcum[rows]
            b_last = b_c[CHUNK - 1:CHUNK]
            ke = (kb[rows] * jnp.exp(b_last - b_c)).astype(BF16)
            st = state[...]
            st_ref[c] = st
            st16 = st.astype(BF16)
            o_parts, u_parts = [], []
            for h in range(GLA_HEADS):
                ks = slice(h * GLA_DK, (h + 1) * GLA_DK)
                vs = slice(h * GLA_DV, (h + 1) * GLA_DV)
                qd = qd_all[rows, ks]
                v_h = vb[rows, vs]
                a = jnp.where(tril, _mm_nt(qd, ki_all[rows, ks]), 0.0)
                o_parts.append(_mm(a.astype(BF16), v_h) + _mm_nt(qd, st16[:, ks]))
                u_parts.append(_mm_tn(v_h, ke[:, ks]))
            ob_s[rows, :] = jnp.concatenate(o_parts, axis=1)
            state[...] = st * jnp.exp(b_last) + jnp.concatenate(u_parts, axis=1)

        ga = ga_s[...]
        sg_a = _sigmoid(ga)
        silu_a = ga * sg_a
        attn = attn_s[...]
        cat_s[:, :W_GA] = (attn * silu_a).astype(BF16)
        gb = gb_s[...]
        sg_b = _sigmoid(gb)
        silu_b = gb * sg_b
        nw = nw_ref[...]
        on_parts = []
        for h in range(GLA_HEADS):
            vs = slice(h * GLA_DV, (h + 1) * GLA_DV)
            o_h = ob_s[:, vs]
            rs = lax.rsqrt(jnp.mean(o_h * o_h, axis=1, keepdims=True) + EPS)
            on_parts.append(o_h * rs * nw)
        on = jnp.concatenate(on_parts, axis=1)
        cat_s[:, W_GA:] = (on * silu_b).astype(BF16)
        cat = cat_s[...]
        hres = ALPHA * x + _mm(cat, wout_ref[...])
        mu = jnp.mean(hres, axis=1, keepdims=True)
        hc = hres - mu
        rstd = lax.rsqrt(jnp.mean(hc * hc, axis=1, keepdims=True) + EPS)
        xhat = hc * rstd
        g_ln = lng_ref[...]
        err = xhat * g_ln + lnb_ref[...] - t_ref[...]
        loss_ref[...] += jnp.sum(err * err) * (0.5 / D_MODEL)
        dy = err * (1.0 / D_MODEL)
        glng_ref[...] += jnp.sum(dy * xhat, axis=0, keepdims=True)
        glnb_ref[...] += jnp.sum(dy, axis=0, keepdims=True)
        dxh = dy * g_ln
        dh = rstd * (dxh - jnp.mean(dxh, axis=1, keepdims=True) - xhat * jnp.mean(dxh * xhat, axis=1, keepdims=True))
        dh_ref[...] = dh
        dh16 = dh.astype(BF16)
        dwout_ref[...] += _mm_tn(cat, dh16)
        dcat = _mm_nt(dh16, wout_ref[...])

        d_a = dcat[:, :W_GA]
        dattn_ref[...] = (d_a * silu_a).astype(BF16)
        dga_ref[...] = (d_a * attn * (sg_a * (1.0 + ga * (1.0 - sg_a)))).astype(BF16)
        d_b = dcat[:, W_GA:]
        dgb_ref[...] = (d_b * on * (sg_b * (1.0 + gb * (1.0 - sg_b)))).astype(BF16)
        d_on = d_b * silu_b
        gnw = jnp.zeros((1, GLA_DV), F32)
        do_parts = []
        for h in range(GLA_HEADS):
            vs = slice(h * GLA_DV, (h + 1) * GLA_DV)
            o_h = ob_s[:, vs]
            rs = lax.rsqrt(jnp.mean(o_h * o_h, axis=1, keepdims=True) + EPS)
            d_on_h = d_on[:, vs]
            gnw = gnw + jnp.sum(d_on_h * o_h * rs, axis=0, keepdims=True)
            gg = d_on_h * nw
            do_parts.append(rs * gg - o_h * (rs * rs * rs) * jnp.mean(gg * o_h, axis=1, keepdims=True))
        gnw_ref[...] += gnw
        dob_ref[...] = jnp.concatenate(do_parts, axis=1).astype(BF16)

    tile = lambda w: pl.BlockSpec((TM, w), lambda i: (i, 0))
    whole = lambda shape: pl.BlockSpec(shape, lambda i: tuple(0 for _ in shape), pipeline_mode=pl.Buffered(1))
    out_shape = (
        jax.ShapeDtypeStruct((s_len, W_QA), BF16),
        jax.ShapeDtypeStruct((s_len, W_KA), BF16),
        jax.ShapeDtypeStruct((s_len, W_VA), BF16),
        jax.ShapeDtypeStruct((s_len, W_QB), F32),
        jax.ShapeDtypeStruct((s_len, W_KB), F32),
        jax.ShapeDtypeStruct((s_len, W_VB), BF16),
        jax.ShapeDtypeStruct((s_len, W_R), F32),
        jax.ShapeDtypeStruct((s_len, W_GA), BF16),
        jax.ShapeDtypeStruct((s_len, W_GA), BF16),
        jax.ShapeDtypeStruct((s_len, W_GB), BF16),
        jax.ShapeDtypeStruct((s_len, W_GB), BF16),
        jax.ShapeDtypeStruct((s_len, D_MODEL), F32),
        jax.ShapeDtypeStruct((s_len // CHUNK, GLA_DV, GLA_HEADS * GLA_DK), F32),
        jax.ShapeDtypeStruct((D_MODEL, D_MODEL), F32),
        jax.ShapeDtypeStruct((1, D_MODEL), F32),
        jax.ShapeDtypeStruct((1, D_MODEL), F32),
        jax.ShapeDtypeStruct((1, GLA_DV), F32),
        jax.ShapeDtypeStruct((1, 128), F32),
    )
    out_specs = (
        tile(W_QA), tile(W_KA), tile(W_VA), tile(W_QB), tile(W_KB), tile(W_VB), tile(W_R),
        tile(W_GA), tile(W_GA), tile(W_GB), tile(W_GB), tile(D_MODEL),
        pl.BlockSpec((nch, GLA_DV, GLA_HEADS * GLA_DK), lambda i: (i, 0, 0)),
        whole((D_MODEL, D_MODEL)), whole((1, D_MODEL)), whole((1, D_MODEL)), whole((1, GLA_DV)), whole((1, 128)),
    )
    in_specs = [
        tile(D_MODEL), tile(D_MODEL), tile(128), tile(128), tile(128),
        whole((D_MODEL, D_PROJ)), whole((D_MODEL, D_MODEL)), whole((W_R, W_KB)), whole((1, W_KB)),
        pl.BlockSpec(memory_space=pltpu.SMEM), whole((1, GLA_DV)), whole((1, D_MODEL)), whole((1, D_MODEL)),
    ]
    scratch = [
        pltpu.VMEM((BLOCK, W_KA), BF16), pltpu.VMEM((BLOCK, W_VA), BF16),
        pltpu.VMEM((GLA_DV, GLA_HEADS * GLA_DK), F32),
        pltpu.VMEM((TM, W_GA), F32), pltpu.VMEM((TM, W_GA), F32), pltpu.VMEM((TM, W_GB), F32),
        pltpu.VMEM((TM, W_GB), F32), pltpu.VMEM((TM, D_MODEL), BF16),
    ]
    return pl.pallas_call(
        body, name="fwd_head", grid=(nt,), in_specs=in_specs, out_specs=out_specs, out_shape=out_shape,
        scratch_shapes=scratch,
        compiler_params=pltpu.CompilerParams(dimension_semantics=("arbitrary",), vmem_limit_bytes=VMEM_LIMIT),
    )(x, tgt, cos, sa, sb, win, wout, wgu, bg, sinks, nw, lng, lnb)


def _bwd_call(x, dh, qa, ka, va, dattn, dga, dob, dgb, qb, kb, vb, r, st, cos, sa, sb, win, wgu, bg, sinks):
    s_len = x.shape[0]
    nt = s_len // TM
    nblk = TM // BLOCK
    nch = TM // CHUNK

    def body(x_ref, dh_ref, qa_ref, ka_ref, va_ref, kap_ref, vap_ref, dattn_ref, dga_ref, dob_ref, dgb_ref,
             qb_ref, kb_ref, vb_ref, r_ref, st_ref, cos_ref, sa_ref, sb_ref, win_ref, wgu_ref, bg_ref, sinks_ref,
             gx_ref, dwin_ref, gsink_ref, gbg_ref, gwgu_ref,
             dproj, dk_carry, dv_carry, ds_carry, db_s):
        i = pl.program_id(0)
        t = nt - 1 - i

        @pl.when(i == 0)
        def _():
            dk_carry[...] = jnp.zeros_like(dk_carry)
            dv_carry[...] = jnp.zeros_like(dv_carry)
            ds_carry[...] = jnp.zeros_like(ds_carry)
            dwin_ref[...] = jnp.zeros_like(dwin_ref)
            gsink_ref[...] = jnp.zeros_like(gsink_ref)
            gbg_ref[...] = jnp.zeros_like(gbg_ref)
            gwgu_ref[...] = jnp.zeros_like(gwgu_ref)

        cos = cos_ref[...]
        sa = sa_ref[...]
        sb = sb_ref[...]
        cos4, sa4, sb4 = (jnp.concatenate([v] * 4, axis=1) for v in (cos, sa, sb))

        qa = qa_ref[...]
        ka = ka_ref[...]
        va = va_ref[...]
        dattn = dattn_ref[...]
        gsink_rows = [jnp.zeros((1, 1), F32) for _ in range(Q_HEADS)]
        for b in reversed(range(nblk)):
            rows = slice(b * BLOCK, (b + 1) * BLOCK)
            has_prev = (t * nblk + b) > 0
            mask = _attn_mask(has_prev)
            k_cur = ka[rows]
            v_cur = va[rows]
            k_old = kap_ref[...] if b == 0 else ka[(b - 1) * BLOCK:b * BLOCK]
            v_old = vap_ref[...] if b == 0 else va[(b - 1) * BLOCK:b * BLOCK]
            dq_parts, dk_parts, dv_parts = [], [], []
            for j in range(KV_HEADS):
                hs = slice(j * HEAD_A, (j + 1) * HEAD_A)
                kc = jnp.concatenate([k_old[:, hs], k_cur[:, hs]], axis=0)
                vc = jnp.concatenate([v_old[:, hs], v_cur[:, hs]], axis=0)
                qs = _stack_heads(qa[rows], j)
                do_s = _stack_heads(dattn[rows], j)
                probs, p_sink = _softmax_block(qs, kc, mask, _sink_col(sinks_ref, j))
                dp = _mm_nt(do_s, vc)
                d_row = jnp.sum(probs * dp, axis=1, keepdims=True)
                ds16 = (probs * (dp - d_row) * ATT_SCALE).astype(BF16)
                dq_parts.append(_mm(ds16, kc))
                dk_parts.append(_mm_tn(ds16, qs))
                dv_parts.append(_mm_tn(probs.astype(BF16), do_s))
                t_sink = d_row * p_sink
                for g in range(GROUP):
                    gsink_rows[GROUP * j + g] = gsink_rows[GROUP * j + g] - jnp.sum(
                        t_sink[g * BLOCK:(g + 1) * BLOCK], axis=0, keepdims=True)
            dq = _rope_bwd(_unstack_heads(dq_parts), cos4[rows], sa4[rows], sb4[rows])
            dproj[rows, O_QA:O_QA + W_QA] = dq.astype(BF16)
            dk_cur = dk_carry[...] + jnp.concatenate([p[BLOCK:] for p in dk_parts], axis=1)
            dv_cur = dv_carry[...] + jnp.concatenate([p[BLOCK:] for p in dv_parts], axis=1)
            dproj[rows, O_KA:O_KA + W_KA] = _rope_bwd(dk_cur, cos[rows], sa[rows], sb[rows]).astype(BF16)
            dproj[rows, O_VA:O_VA + W_VA] = dv_cur.astype(BF16)
            dk_carry[...] = jnp.concatenate([p[:BLOCK] for p in dk_parts], axis=1)
            dv_carry[...] = jnp.concatenate([p[:BLOCK] for p in dv_parts], axis=1)
        for hq in range(Q_HEADS):
            gsink_ref[hq:hq + 1, :] += jnp.broadcast_to(gsink_rows[hq], (1, 128))

        dproj[:, O_GA:O_GA + W_GA] = dga_ref[...]
        dproj[:, O_GB:O_GB + W_GB] = dgb_ref[...]

        r16 = r_ref[...].astype(BF16)
        wgu16 = wgu_ref[...].astype(BF16)
        z = _mm(r16, wgu16) + bg_ref[...]
        log_a = _log_sigmoid(z) / GLA_TAU
        bcum = _tri_mm(_chunk_tri(TM, False), log_a)
        qb = qb_ref[...]
        kb = kb_ref[...]
        vb = vb_ref[...]
        dob = dob_ref[...]
        e_b = jnp.exp(bcum)
        e_nb = jnp.exp(-bcum)
        qd_f = qb * GLA_SCALE * e_b
        ki_f = kb * e_nb
        qd_all = qd_f.astype(BF16)
        ki_all = ki_f.astype(BF16)
        tril = lax.broadcasted_iota(jnp.int32, (CHUNK, CHUNK), 0) >= lax.broadcasted_iota(jnp.int32, (CHUNK, CHUNK), 1)
        last_row = lax.broadcasted_iota(jnp.int32, (CHUNK, 1), 0) == CHUNK - 1
        for c in reversed(range(nch)):
            rows = slice(c * CHUNK, (c + 1) * CHUNK)
            b_c = bcum[rows]
            b_last = b_c[CHUNK - 1:CHUNK]
            e_e = jnp.exp(b_last - b_c)
            dec = jnp.exp(b_last)
            ke_f = kb[rows] * e_e
            ke = ke_f.astype(BF16)
            sp = st_ref[c]
            sp16 = sp.astype(BF16)
            dsn = ds_carry[...]
            dsn16 = dsn.astype(BF16)
            dqd_p, dki_p, dke_p, dv_p, dsp_p = [], [], [], [], []
            for h in range(GLA_HEADS):
                ks = slice(h * GLA_DK, (h + 1) * GLA_DK)
                vs = slice(h * GLA_DV, (h + 1) * GLA_DV)
                qd = qd_all[rows, ks]
                ki = ki_all[rows, ks]
                v_h = vb[rows, vs]
                do_h = dob[rows, vs]
                a16 = jnp.where(tril, _mm_nt(qd, ki), 0.0).astype(BF16)
                da16 = jnp.where(tril, _mm_nt(do_h, v_h), 0.0).astype(BF16)
                dv_p.append(_mm_tn(a16, do_h) + _mm_nt(ke[:, ks], dsn16[:, ks]))
                dqd_p.append(_mm(da16, ki) + _mm(do_h, sp16[:, ks]))
                dki_p.append(_mm_tn(da16, qd))
                dke_p.append(_mm(v_h, dsn16[:, ks]))
                dsp_p.append(_mm_tn(do_h, qd))
            dqd = jnp.concatenate(dqd_p, axis=1)
            dki = jnp.concatenate(dki_p, axis=1)
            dke = jnp.concatenate(dke_p, axis=1)
            ddec = jnp.sum(dsn * sp, axis=0, keepdims=True)
            ds_carry[...] = dsn * dec + jnp.concatenate(dsp_p, axis=1)
            dproj[rows, O_QB:O_QB + W_QB] = (dqd * e_b[rows] * GLA_SCALE).astype(BF16)
            dproj[rows, O_KB:O_KB + W_KB] = (dki * e_nb[rows] + dke * e_e).astype(BF16)
            dproj[rows, O_VB:O_VB + W_VB] = jnp.concatenate(dv_p, axis=1).astype(BF16)
            dke_ke = dke * ke_f
            d_b = dqd * qd_f[rows] - dki * ki_f[rows] - dke_ke
            d_bl = jnp.sum(dke_ke, axis=0, keepdims=True) + ddec * dec
            db_s[rows, :] = d_b + jnp.where(last_row, d_bl, 0.0)
        dlog_a = _tri_mm(_chunk_tri(TM, True), db_s[...])
        dz = dlog_a * (1.0 / GLA_TAU) * _sigmoid(-z)
        dz16 = dz.astype(BF16)
        gbg_ref[...] += jnp.sum(dz, axis=0, keepdims=True)
        gwgu_ref[...] += _mm_tn(r16, dz16)
        dproj[:, O_R:O_R + W_R] = _mm_nt(dz16, wgu16).astype(BF16)

        dp16 = dproj[...]
        gx_ref[...] = ALPHA * dh_ref[...] + _mm_nt(dp16, win_ref[...])
        dwin_ref[...] += _mm_tn(x_ref[...].astype(BF16), dp16)

    tile = lambda w: pl.BlockSpec((TM, w), lambda i: (nt - 1 - i, 0))
    whole = lambda shape: pl.BlockSpec(shape, lambda i: tuple(0 for _ in shape), pipeline_mode=pl.Buffered(1))
    prev_blk = pl.BlockSpec((BLOCK, W_KA), lambda i: (jnp.maximum((nt - 1 - i) * nblk - 1, 0), 0))
    in_specs = [
        tile(D_MODEL), tile(D_MODEL), tile(W_QA), tile(W_KA), tile(W_VA), prev_blk, prev_blk,
        tile(W_GA), tile(W_GA), tile(W_GB), tile(W_GB), tile(W_QB), tile(W_KB), tile(W_VB), tile(W_R),
        pl.BlockSpec((nch, GLA_DV, GLA_HEADS * GLA_DK), lambda i: (nt - 1 - i, 0, 0)),
        tile(128), tile(128), tile(128),
        whole((D_MODEL, D_PROJ)), whole((W_R, W_KB)), whole((1, W_KB)), pl.BlockSpec(memory_space=pltpu.SMEM),
    ]
    out_shape = (
        jax.ShapeDtypeStruct((s_len, D_MODEL), F32),
        jax.ShapeDtypeStruct((D_MODEL, D_PROJ), F32),
        jax.ShapeDtypeStruct((Q_HEADS, 128), F32),
        jax.ShapeDtypeStruct((1, W_KB), F32),
        jax.ShapeDtypeStruct((W_R, W_KB), F32),
    )
    out_specs = (tile(D_MODEL), whole((D_MODEL, D_PROJ)), whole((Q_HEADS, 128)), whole((1, W_KB)), whole((W_R, W_KB)))
    scratch = [
        pltpu.VMEM((TM, D_PROJ), BF16), pltpu.VMEM((BLOCK, W_KA), F32), pltpu.VMEM((BLOCK, W_VA), F32),
        pltpu.VMEM((GLA_DV, GLA_HEADS * GLA_DK), F32), pltpu.VMEM((TM, W_KB), F32),
    ]
    return pl.pallas_call(
        body, name="bwd_mix", grid=(nt,), in_specs=in_specs, out_specs=out_specs, out_shape=out_shape,
        scratch_shapes=scratch,
        compiler_params=pltpu.CompilerParams(dimension_semantics=("arbitrary",), vmem_limit_bytes=VMEM_LIMIT),
    )(x, dh, qa, ka, va, ka, va, dattn, dga, dob, dgb, qb, kb, vb, r, st, cos, sa, sb, win, wgu, bg, sinks)


def _mesh_place():
    x, y, c = lax.axis_index("x"), lax.axis_index("y"), lax.axis_index("c")
    chips = [(1 - x, y), (x, 1 - y), (1 - x, 1 - y)]
    return x, y, c, chips


def _gather_weights_call(w_in, w_out, wgu):
    half_in = D_MODEL // 2
    half_out = SHARD_OUT // 2

    def body(win_ref, wout_ref, wgu_ref, win_all, wout_all, wgu_all, send_sems, recv_sems):
        x, y, c, chips = _mesh_place()
        k_me = 2 * x + y
        win_all[k_me] = win_ref[...].astype(BF16)
        wout_all[k_me] = wout_ref[...].astype(BF16)
        wgu_all[k_me] = wgu_ref[...]

        def blocks(k, hc):
            return (win_all.at[k, pl.ds(hc * half_in, half_in), :], wout_all.at[k, pl.ds(hc * half_out, half_out), :])

        def copies(k, hc, sem0, to):
            return [pltpu.make_async_remote_copy(src_ref=blk, dst_ref=blk, send_sem=send_sems.at[sem0 + n],
                                                 recv_sem=recv_sems.at[sem0 + n], device_id=to, device_id_type=MESH)
                    for n, blk in enumerate(blocks(k, hc))]

        def gu_copy(k, r, to):
            return pltpu.make_async_remote_copy(src_ref=wgu_all.at[k], dst_ref=wgu_all.at[k], send_sem=send_sems.at[12 + r],
                                                recv_sem=recv_sems.at[12 + r], device_id=to, device_id_type=MESH)

        started = []
        for r, chip in enumerate(chips):
            started += copies(k_me, c, 2 * r, (*chip, c))
            started.append(gu_copy(k_me, r, (*chip, c)))
        for cp in started:
            cp.start()
        for r, chip in enumerate(chips):
            k_r = 2 * chip[0] + chip[1]
            for cp in copies(k_r, c, 2 * r, (x, y, c)):
                cp.wait_recv()
            passed = copies(k_r, c, 6 + 2 * r, (x, y, 1 - c))
            for cp in passed:
                cp.start()
            started += passed
        for r, chip in enumerate(chips):
            k_r = 2 * chip[0] + chip[1]
            for cp in copies(k_r, 1 - c, 6 + 2 * r, (x, y, c)):
                cp.wait_recv()
            gu_copy(k_r, r, (x, y, c)).wait_recv()
        for cp in started:
            cp.wait_send()

    vmem = pl.BlockSpec(memory_space=pltpu.VMEM)
    return pl.pallas_call(
        body, name="gather_weights",
        out_shape=(jax.ShapeDtypeStruct((N_CHIPS, D_MODEL, SHARD_IN), BF16),
                   jax.ShapeDtypeStruct((N_CHIPS, SHARD_OUT, D_MODEL), BF16),
                   jax.ShapeDtypeStruct((N_CHIPS, W_R, W_KB // N_CHIPS), F32)),
        in_specs=[vmem, vmem, vmem], out_specs=(vmem, vmem, vmem),
        scratch_shapes=[pltpu.SemaphoreType.DMA((15,)), pltpu.SemaphoreType.DMA((15,))],
        compiler_params=pltpu.CompilerParams(vmem_limit_bytes=VMEM_LIMIT),
    )(w_in, w_out, wgu)


def _adamw(w, g, m, v):
    m = ADAM_B1 * m + (1.0 - ADAM_B1) * g
    v = ADAM_B2 * v + (1.0 - ADAM_B2) * (g * g)
    m_hat = m / (1.0 - ADAM_B1 ** ADAM_STEP)
    v_hat = v / (1.0 - ADAM_B2 ** ADAM_STEP)
    delta = -ADAM_LR * (m_hat / (jnp.sqrt(v_hat) + ADAM_EPS) + ADAM_WD * w)
    return delta, m, v


def _reduce_grads_call(g_in, g_out, pack, w_small, m_small, v_small):
    half_in = D_MODEL // 2
    half_out = SHARD_OUT // 2

    def body(gin_hbm, gout_hbm, pack_ref, ws_ref, ms_ref, vs_ref,
             fin_in, fin_out, gs_ref, ds_ref, nms_ref, nvs_ref,
             a_in, a_out, b_in, b_out, r_in, r_out, pack_all, send_sems, recv_sems, local_sems):
        x, y, c, chips = _mesh_place()
        k_me = 2 * x + y
        me = 4 * x + 2 * y + c
        sibling = (x, y, 1 - c)

        pack_all[me] = pack_ref[...]
        small = []
        for mask in range(1, 8):
            peer = (x ^ (mask >> 2), y ^ ((mask >> 1) & 1), c ^ (mask & 1))
            small.append(pltpu.make_async_remote_copy(
                src_ref=pack_ref, dst_ref=pack_all.at[me], send_sem=send_sems.at[mask], recv_sem=recv_sems.at[mask],
                device_id=peer, device_id_type=MESH))
        for cp in small:
            cp.start()

        mine = [pltpu.make_async_copy(gin_hbm.at[:, pl.ds(c * half_in, half_in), :], a_in, local_sems.at[0]),
                pltpu.make_async_copy(gout_hbm.at[:, pl.ds(c * half_out, half_out), :], a_out, local_sems.at[1])]
        to_sib = [pltpu.make_async_remote_copy(
                      src_ref=gin_hbm.at[:, pl.ds((1 - c) * half_in, half_in), :], dst_ref=b_in,
                      send_sem=send_sems.at[8], recv_sem=recv_sems.at[8], device_id=sibling, device_id_type=MESH),
                  pltpu.make_async_remote_copy(
                      src_ref=gout_hbm.at[:, pl.ds((1 - c) * half_out, half_out), :], dst_ref=b_out,
                      send_sem=send_sems.at[9], recv_sem=recv_sems.at[9], device_id=sibling, device_id_type=MESH)]
        for cp in mine + to_sib:
            cp.start()
        for cp in mine:
            cp.wait()
        for cp in to_sib:
            cp.wait_recv()
        for k in range(N_CHIPS):
            a_in[k] = a_in[k] + b_in[k]
            a_out[k] = a_out[k] + b_out[k]

        sent = []
        for r, chip in enumerate(chips):
            k_r = 2 * chip[0] + chip[1]
            sent.append(pltpu.make_async_remote_copy(
                src_ref=a_in.at[k_r], dst_ref=r_in.at[r], send_sem=send_sems.at[10 + 2 * r],
                recv_sem=recv_sems.at[10 + 2 * r], device_id=(*chip, c), device_id_type=MESH))
            sent.append(pltpu.make_async_remote_copy(
                src_ref=a_out.at[k_r], dst_ref=r_out.at[r], send_sem=send_sems.at[11 + 2 * r],
                recv_sem=recv_sems.at[11 + 2 * r], device_id=(*chip, c), device_id_type=MESH))
        for cp in sent:
            cp.start()
        for cp in sent:
            cp.wait_recv()
        rows_in = pl.ds(pl.multiple_of(c * half_in, half_in), half_in)
        rows_out = pl.ds(pl.multiple_of(c * half_out, half_out), half_out)
        fin_in[rows_in, :] = a_in[k_me] + r_in[0] + r_in[1] + r_in[2]
        fin_out[rows_out, :] = a_out[k_me] + r_out[0] + r_out[1] + r_out[2]

        swap = [pltpu.make_async_remote_copy(
                    src_ref=fin_in.at[pl.ds(c * half_in, half_in), :], dst_ref=fin_in.at[pl.ds(c * half_in, half_in), :],
                    send_sem=send_sems.at[16], recv_sem=recv_sems.at[16], device_id=sibling, device_id_type=MESH),
                pltpu.make_async_remote_copy(
                    src_ref=fin_out.at[pl.ds(c * half_out, half_out), :], dst_ref=fin_out.at[pl.ds(c * half_out, half_out), :],
                    send_sem=send_sems.at[17], recv_sem=recv_sems.at[17], device_id=sibling, device_id_type=MESH)]
        for cp in swap:
            cp.start()

        for cp in small:
            cp.wait_recv()
        total = pack_all[0]
        for d in range(1, 8):
            total = total + pack_all[d]
        gu_rows = pl.ds(pl.multiple_of(P_GU + 16 * k_me, 8), 16)
        pack_all[0] = total
        g_small = jnp.concatenate([total[:P_GU], pack_all[0, gu_rows, :]], axis=0)
        delta, new_m, new_v = _adamw(ws_ref[...], g_small, ms_ref[...], vs_ref[...])
        gs_ref[...] = g_small
        ds_ref[...] = delta
        nms_ref[...] = new_m
        nvs_ref[...] = new_v

        other_in = pltpu.make_async_remote_copy(
            src_ref=fin_in.at[pl.ds((1 - c) * half_in, half_in), :], dst_ref=fin_in.at[pl.ds((1 - c) * half_in, half_in), :],
            send_sem=send_sems.at[16], recv_sem=recv_sems.at[16], device_id=sibling, device_id_type=MESH)
        other_out = pltpu.make_async_remote_copy(
            src_ref=fin_out.at[pl.ds((1 - c) * half_out, half_out), :], dst_ref=fin_out.at[pl.ds((1 - c) * half_out, half_out), :],
            send_sem=send_sems.at[17], recv_sem=recv_sems.at[17], device_id=sibling, device_id_type=MESH)
        other_in.wait_recv()
        other_out.wait_recv()
        for cp in small + to_sib + sent + swap:
            cp.wait_send()

    vmem = pl.BlockSpec(memory_space=pltpu.VMEM)
    hbm = pl.BlockSpec(memory_space=pl.ANY)
    small_shape = jax.ShapeDtypeStruct((PACK2_ROWS, 128), F32)
    return pl.pallas_call(
        body, name="reduce_grads",
        out_shape=(jax.ShapeDtypeStruct((D_MODEL, SHARD_IN), F32), jax.ShapeDtypeStruct((SHARD_OUT, D_MODEL), F32),
                   small_shape, small_shape, small_shape, small_shape),
        in_specs=[hbm, hbm, vmem, vmem, vmem, vmem], out_specs=(vmem,) * 6,
        scratch_shapes=[
            pltpu.VMEM((N_CHIPS, half_in, SHARD_IN), F32), pltpu.VMEM((N_CHIPS, half_out, D_MODEL), F32),
            pltpu.VMEM((N_CHIPS, half_in, SHARD_IN), F32), pltpu.VMEM((N_CHIPS, half_out, D_MODEL), F32),
            pltpu.VMEM((3, half_in, SHARD_IN), F32), pltpu.VMEM((3, half_out, D_MODEL), F32),
            pltpu.VMEM((8, PACK_ROWS, 128), F32),
            pltpu.SemaphoreType.DMA((18,)), pltpu.SemaphoreType.DMA((18,)), pltpu.SemaphoreType.DMA((2,)),
        ],
        compiler_params=pltpu.CompilerParams(vmem_limit_bytes=VMEM_LIMIT),
    )(g_in, g_out, pack, w_small, m_small, v_small)


def _adamw_call(g_in, w_in, m_in, v_in, g_out, w_out, m_out, v_out):
    steps = 8
    rows_in = D_MODEL // steps
    rows_out = SHARD_OUT // steps

    def body(gi, wi, mi, vi, go, wo, mo, vo, di, nmi, nvi, do, nmo, nvo):
        di[...], nmi[...], nvi[...] = _adamw(wi[...], gi[...], mi[...], vi[...])
        do[...], nmo[...], nvo[...] = _adamw(wo[...], go[...], mo[...], vo[...])

    t_in = pl.BlockSpec((rows_in, SHARD_IN), lambda i: (i, 0))
    t_out = pl.BlockSpec((rows_out, D_MODEL), lambda i: (i, 0))
    s_in = jax.ShapeDtypeStruct((D_MODEL, SHARD_IN), F32)
    s_out = jax.ShapeDtypeStruct((SHARD_OUT, D_MODEL), F32)
    return pl.pallas_call(
        body, name="adamw", grid=(steps,), in_specs=[t_in] * 4 + [t_out] * 4, out_specs=(t_in,) * 3 + (t_out,) * 3,
        out_shape=(s_in,) * 3 + (s_out,) * 3,
        compiler_params=pltpu.CompilerParams(dimension_semantics=("arbitrary",)),
    )(g_in, w_in, m_in, v_in, g_out, w_out, m_out, v_out)


def _rope_tables(positions):
    half = 8
    inv_freq = 500000.0 ** (-jnp.arange(half, dtype=F32) / half)
    ang = positions.astype(F32)[:, None] * inv_freq
    cos, sin = jnp.cos(ang), jnp.sin(ang)
    s_len = positions.shape[0]
    zeros = jnp.zeros((s_len, HEAD_A - 2 * half), F32)
    zeros8 = jnp.zeros((s_len, half), F32)
    cos_h = jnp.concatenate([cos, cos, jnp.ones((s_len, HEAD_A - 2 * half), F32)], axis=1)
    sa_h = jnp.concatenate([-sin, zeros8, zeros], axis=1)
    sb_h = jnp.concatenate([zeros8, sin, zeros], axis=1)
    return tuple(jnp.concatenate([t, t], axis=1) for t in (cos_h, sa_h, sb_h))


def _pad_lanes(a):
    return jnp.pad(a, ((0, 0), (0, 128 - a.shape[1])))


def _pack_small(ln_g, ln_b, b_gate, norm_w, sinks, gate_up_rows, extra_rows):
    return jnp.concatenate([
        ln_g.reshape(8, 128), ln_b.reshape(8, 128), b_gate.reshape(2, 128), norm_w.reshape(1, 128),
        _pad_lanes(sinks.reshape(1, Q_HEADS)), extra_rows, gate_up_rows], axis=0)


def kernel(x, positions, w_in, gla_w_gate_up, gla_b_gate, attn_sinks, gla_norm_w, w_out, ln_g, ln_b, loss_target, m_w_in, m_gla_w_gate_up, m_gla_b_gate, m_attn_sinks, m_gla_norm_w, m_w_out, m_ln_g, m_ln_b, v_w_in, v_gla_w_gate_up, v_gla_b_gate, v_attn_sinks, v_gla_norm_w, v_w_out, v_ln_g, v_ln_b):
    win_all, wout_all, wgu_all = _gather_weights_call(w_in[0], w_out[0], gla_w_gate_up[0])
    win = jnp.transpose(win_all, (1, 0, 2)).reshape(D_MODEL, D_PROJ)
    wout = wout_all.reshape(D_MODEL, D_MODEL)
    wgu = jnp.transpose(wgu_all, (1, 0, 2)).reshape(W_R, W_KB)
    cos, sa, sb = _rope_tables(positions[0])
    sinks = attn_sinks[0]

    (qa, ka, va, qb, kb, vb, r, dattn, dga, dob, dgb, dh, st, g_wout, g_lng, g_lnb, g_nw, loss) = _fwd_call(
        x[0], loss_target[0], cos, sa, sb, win, wout, wgu, gla_b_gate, sinks, gla_norm_w, ln_g, ln_b)
    gx, g_win, g_sink, g_bg, g_wgu = _bwd_call(
        x[0], dh, qa, ka, va, dattn, dga, dob, dgb, qb, kb, vb, r, st, cos, sa, sb, win, wgu, gla_b_gate, sinks)

    g_win_by_chip = jnp.transpose(g_win.reshape(D_MODEL, N_CHIPS, SHARD_IN), (1, 0, 2))
    g_wout_by_chip = g_wout.reshape(N_CHIPS, SHARD_OUT, D_MODEL)
    gu_rows = _pad_lanes(jnp.transpose(g_wgu.reshape(W_R, N_CHIPS, W_KB // N_CHIPS), (1, 0, 2)).reshape(N_CHIPS * W_R, -1))
    loss_rows = jnp.concatenate([loss, jnp.zeros((3, 128), F32)], axis=0)
    pack = _pack_small(g_lng, g_lnb, g_bg, g_nw, g_sink[:, 0], gu_rows, loss_rows)

    def small(ln_g_, ln_b_, b_gate_, norm_w_, sinks_, gate_up_):
        return _pack_small(ln_g_, ln_b_, b_gate_, norm_w_, sinks_[0], _pad_lanes(gate_up_[0]), jnp.zeros((4, 128), F32))

    w_small = small(ln_g, ln_b, gla_b_gate, gla_norm_w, attn_sinks, gla_w_gate_up)
    m_small = small(m_ln_g, m_ln_b, m_gla_b_gate, m_gla_norm_w, m_attn_sinks, m_gla_w_gate_up)
    v_small = small(v_ln_g, v_ln_b, v_gla_b_gate, v_gla_norm_w, v_attn_sinks, v_gla_w_gate_up)

    fin_in, fin_out, g_s, d_s, nm_s, nv_s = _reduce_grads_call(g_win_by_chip, g_wout_by_chip, pack, w_small, m_small, v_small)
    d_in, nm_in, nv_in, d_out, nm_out, nv_out = _adamw_call(
        fin_in, w_in[0], m_w_in[0], v_w_in[0], fin_out, w_out[0], m_w_out[0], v_w_out[0])

    def unpack(p, big_in, big_out):
        return (big_in[None], p[P_GU:P_GU + W_R, :W_KB // N_CHIPS][None], p[P_BG:P_BG + 2].reshape(1, W_KB),
                p[P_SINK:P_SINK + 1, :Q_HEADS], p[P_NW:P_NW + 1], big_out[None],
                p[P_LNG:P_LNG + 8].reshape(1, D_MODEL), p[P_LNB:P_LNB + 8].reshape(1, D_MODEL))

    loss_total = g_s[P_LOSS, 0]
    return (loss_total, gx[None], *unpack(g_s, fin_in, fin_out), *unpack(d_s, d_in, d_out),
            *unpack(nm_s, nm_in, nm_out), *unpack(nv_s, nv_in, nv_out))
```

```python
import functools

import jax
import jax.numpy as jnp
import numpy as np
from jax import lax
from jax.experimental import pallas as pl
from jax.experimental.pallas import tpu as pltpu

F32 = jnp.float32
BF16 = jnp.bfloat16
MESH = pl.DeviceIdType.MESH

D_MODEL = 1024
N_CHIPS = 4
W_QA, W_KA, W_VA, W_GA, W_QB, W_KB, W_VB, W_GB, W_R = 512, 128, 128, 512, 256, 256, 512, 512, 16
O_QA = 0
O_KA = O_QA + W_QA
O_VA = O_KA + W_KA
O_GA = O_VA + W_VA
O_QB = O_GA + W_GA
O_KB = O_QB + W_QB
O_VB = O_KB + W_KB
O_GB = O_VB + W_VB
O_R = O_GB + W_GB
D_PROJ = O_R + W_R
SHARD_IN = D_PROJ // N_CHIPS
SHARD_OUT = D_MODEL // N_CHIPS
SHARD_PAD = 720
ACC_ROWS = -(-((N_CHIPS - 1) * SHARD_IN + SHARD_PAD) // 8) * 8
LIN_ROWS = SHARD_IN * D_MODEL // 128
HALF = D_MODEL // 2

HEAD_A = 64
Q_HEADS = 8
KV_HEADS = 2
GROUP = 4
BLOCK = 128
GLA_HEADS = 4
GLA_DK = 64
GLA_DV = 128
CHUNK = 64
GLA_TAU = 16.0
EPS = 1e-5
ALPHA = 2.0 ** 0.25
ATT_SCALE = HEAD_A ** -0.5
GLA_SCALE = GLA_DK ** -0.5

ADAM_LR = 0.001
ADAM_B1 = 0.9
ADAM_B2 = 0.999
ADAM_EPS = 1e-08
ADAM_WD = 0.01
ADAM_STEP = 10

TM = 256
VMEM_LIMIT = 56 * 1024 * 1024

P_LNG, P_LNB, P_BG, P_NW, P_SINK, P_LOSS, P_GU = 0, 8, 16, 18, 19, 20, 24
PACK_ROWS = P_GU + N_CHIPS * 16
PACK2_ROWS = P_GU + 16


def _mm(a, b):
    return jnp.dot(a, b, preferred_element_type=F32)


def _mm_nt(a, b):
    return lax.dot_general(a, b, (((1,), (1,)), ((), ())), preferred_element_type=F32)


def _mm_tn(a, b):
    return lax.dot_general(a, b, (((0,), (0,)), ((), ())), preferred_element_type=F32)


def _split3(a):
    hi = a.astype(BF16)
    r1 = a - hi.astype(F32)
    mid = r1.astype(BF16)
    lo = (r1 - mid.astype(F32)).astype(BF16)
    return hi, mid, lo


def _tri_mm(tri, a):
    hi, mid, lo = _split3(a)
    return _mm(tri, hi) + _mm(tri, mid) + _mm(tri, lo)


def _chunk_tri(n, upper):
    r = lax.broadcasted_iota(jnp.int32, (n, n), 0)
    c = lax.broadcasted_iota(jnp.int32, (n, n), 1)
    same = (r >> 6) == (c >> 6)
    order = (c >= r) if upper else (c <= r)
    return jnp.where(same & order, 1.0, 0.0).astype(BF16)


def _rope(t, cos, sa, sb):
    w = t.shape[1]
    return t * cos + pltpu.roll(t, w - 8, 1) * sa + pltpu.roll(t, 8, 1) * sb


def _rope_bwd(d, cos, sa, sb):
    w = d.shape[1]
    return d * cos + pltpu.roll(d * sa, 8, 1) + pltpu.roll(d * sb, w - 8, 1)


def _log_sigmoid(z):
    return jnp.minimum(z, 0.0) - jnp.log1p(jnp.exp(-jnp.abs(z)))


def _sigmoid(z):
    return 1.0 / (1.0 + jnp.exp(-z))


def _attn_mask(has_prev):
    r = lax.broadcasted_iota(jnp.int32, (GROUP * BLOCK, 2 * BLOCK), 0) & (BLOCK - 1)
    k = lax.broadcasted_iota(jnp.int32, (GROUP * BLOCK, 2 * BLOCK), 1)
    first_key = jnp.where(has_prev, 0, BLOCK)
    return (k > r) & (k <= r + BLOCK) & (k >= first_key)


def _sink_col(sinks_ref, j):
    r = lax.broadcasted_iota(jnp.int32, (GROUP * BLOCK, 1), 0) >> 7
    col = jnp.full((GROUP * BLOCK, 1), sinks_ref[GROUP * j], F32)
    for g in range(1, GROUP):
        col = jnp.where(r == g, sinks_ref[GROUP * j + g], col)
    return col


def _stack_heads(t, j):
    return jnp.concatenate([t[:, (GROUP * j + g) * HEAD_A:(GROUP * j + g + 1) * HEAD_A] for g in range(GROUP)], axis=0)


def _unstack_heads(parts):
    return jnp.concatenate([parts[j][g * BLOCK:(g + 1) * BLOCK] for j in range(KV_HEADS) for g in range(GROUP)], axis=1)


def _softmax_block(qs, kc, mask, sink):
    s = _mm_nt(qs, kc) * ATT_SCALE
    s = jnp.where(mask, s, -jnp.inf)
    m = jnp.maximum(jnp.max(s, axis=1, keepdims=True), sink)
    p = jnp.exp(s - m)
    e_sink = jnp.exp(sink - m)
    inv = 1.0 / (jnp.sum(p, axis=1, keepdims=True) + e_sink)
    return p * inv, e_sink * inv


def _fwd_call(x, tgt, cos, sa, sb, win, wout, wgu, bg, sinks, nw, lng, lnb):
    s_len = x.shape[0]
    nt = s_len // TM
    nblk = TM // BLOCK
    nch = TM // CHUNK

    def body(x_ref, t_ref, cos_ref, sa_ref, sb_ref, win_ref, wout_ref, wgu_ref, bg_ref, sinks_ref, nw_ref,
             lng_ref, lnb_ref,
             qa_ref, ka_ref, va_ref, qb_ref, kb_ref, vb_ref, r_ref, dattn_ref, dga_ref, dob_ref, dgb_ref, dh_ref,
             st_ref, dwout_ref, glng_ref, glnb_ref, gnw_ref, loss_ref,
             kprev, vprev, state, attn_s, ga_s, ob_s, gb_s, cat_s):
        i = pl.program_id(0)

        @pl.when(i == 0)
        def _():
            kprev[...] = jnp.zeros_like(kprev)
            vprev[...] = jnp.zeros_like(vprev)
            state[...] = jnp.zeros_like(state)
            dwout_ref[...] = jnp.zeros_like(dwout_ref)
            glng_ref[...] = jnp.zeros_like(glng_ref)
            glnb_ref[...] = jnp.zeros_like(glnb_ref)
            gnw_ref[...] = jnp.zeros_like(gnw_ref)
            loss_ref[...] = jnp.zeros_like(loss_ref)

        x = x_ref[...]
        xb = x.astype(BF16)

        def proj(off, width):
            return _mm_nt(xb, win_ref[off:off + width, :])

        cos = cos_ref[...]
        sa = sa_ref[...]
        sb = sb_ref[...]
        cos4, sa4, sb4 = (jnp.concatenate([t] * 4, axis=1) for t in (cos, sa, sb))
        qa = _rope(proj(O_QA, W_QA), cos4, sa4, sb4).astype(BF16)
        ka = _rope(proj(O_KA, W_KA), cos, sa, sb).astype(BF16)
        va = proj(O_VA, W_VA).astype(BF16)
        qa_ref[...] = qa
        ka_ref[...] = ka
        va_ref[...] = va
        ga_s[...] = proj(O_GA, W_GA)
        gb_s[...] = proj(O_GB, W_GB)

        for b in range(nblk):
            rows = slice(b * BLOCK, (b + 1) * BLOCK)
            has_prev = (i * nblk + b) > 0
            mask = _attn_mask(has_prev)
            k_cur = ka[rows]
            v_cur = va[rows]
            k_old = kprev[...] if b == 0 else ka[(b - 1) * BLOCK:b * BLOCK]
            v_old = vprev[...] if b == 0 else va[(b - 1) * BLOCK:b * BLOCK]
            outs = []
            for j in range(KV_HEADS):
                hs = slice(j * HEAD_A, (j + 1) * HEAD_A)
                kc = jnp.concatenate([k_old[:, hs], k_cur[:, hs]], axis=0)
                vc = jnp.concatenate([v_old[:, hs], v_cur[:, hs]], axis=0)
                probs, _ = _softmax_block(_stack_heads(qa[rows], j), kc, mask, _sink_col(sinks_ref, j))
                outs.append(_mm(probs.astype(BF16), vc))
            attn_s[rows, :] = _unstack_heads(outs)
        kprev[...] = ka[(nblk - 1) * BLOCK:]
        vprev[...] = va[(nblk - 1) * BLOCK:]

        r = proj(O_R, W_R)
        r_ref[...] = r
        z = _mm(r.astype(BF16), wgu_ref[...].astype(BF16)) + bg_ref[...]
        log_a = _log_sigmoid(z) / GLA_TAU
        bcum = _tri_mm(_chunk_tri(TM, False), log_a)
        qb = proj(O_QB, W_QB)
        kb = proj(O_KB, W_KB)
        vb = proj(O_VB, W_VB).astype(BF16)
        qb_ref[...] = qb
        kb_ref[...] = kb
        vb_ref[...] = vb
        qd_all = (qb * GLA_SCALE * jnp.exp(bcum)).astype(BF16)
        ki_all = (kb * jnp.exp(-bcum)).astype(BF16)
        tril = lax.broadcasted_iota(jnp.int32, (CHUNK, CHUNK), 0) >= lax.broadcasted_iota(jnp.int32, (CHUNK, CHUNK), 1)
        for c in range(nch):
            rows = slice(c * CHUNK, (c + 1) * CHUNK)
            b_c = bcum[rows]
            b_last = b_c[CHUNK - 1:CHUNK]
            ke = (kb[rows] * jnp.exp(b_last - b_c)).astype(BF16)
            st = state[...]
            st_ref[c] = st
            st16 = st.astype(BF16)
            o_parts, u_parts = [], []
            for h in range(GLA_HEADS):
                ks = slice(h * GLA_DK, (h + 1) * GLA_DK)
                vs = slice(h * GLA_DV, (h + 1) * GLA_DV)
                qd = qd_all[rows, ks]
                v_h = vb[rows, vs]
                a = jnp.where(tril, _mm_nt(qd, ki_all[rows, ks]), 0.0)
                o_parts.append(_mm(a.astype(BF16), v_h) + _mm_nt(qd, st16[:, ks]))
                u_parts.append(_mm_tn(v_h, ke[:, ks]))
            ob_s[rows, :] = jnp.concatenate(o_parts, axis=1)
            state[...] = st * jnp.exp(b_last) + jnp.concatenate(u_parts, axis=1)

        ga = ga_s[...]
        sg_a = _sigmoid(ga)
        silu_a = ga * sg_a
        attn = attn_s[...]
        cat_s[:, :W_GA] = (attn * silu_a).astype(BF16)
        gb = gb_s[...]
        sg_b = _sigmoid(gb)
        silu_b = gb * sg_b
        nw = nw_ref[...]
        on_parts = []
        for h in range(GLA_HEADS):
            vs = slice(h * GLA_DV, (h + 1) * GLA_DV)
            o_h = ob_s[:, vs]
            rs = lax.rsqrt(jnp.mean(o_h * o_h, axis=1, keepdims=True) + EPS)
            on_parts.append(o_h * rs * nw)
        on = jnp.concatenate(on_parts, axis=1)
        cat_s[:, W_GA:] = (on * silu_b).astype(BF16)
        cat = cat_s[...]
        hres = ALPHA * x + _mm(cat, wout_ref[...])
        mu = jnp.mean(hres, axis=1, keepdims=True)
        hc = hres - mu
        rstd = lax.rsqrt(jnp.mean(hc * hc, axis=1, keepdims=True) + EPS)
        xhat = hc * rstd
        g_ln = lng_ref[...]
        err = xhat * g_ln + lnb_ref[...] - t_ref[...]
        loss_ref[...] += jnp.sum(err * err) * (0.5 / D_MODEL)
        dy = err * (1.0 / D_MODEL)
        glng_ref[...] += jnp.sum(dy * xhat, axis=0, keepdims=True)
        glnb_ref[...] += jnp.sum(dy, axis=0, keepdims=True)
        dxh = dy * g_ln
        dh = rstd * (dxh - jnp.mean(dxh, axis=1, keepdims=True) - xhat * jnp.mean(dxh * xhat, axis=1, keepdims=True))
        dh_ref[...] = dh
        dh16 = dh.astype(BF16)
        for h in range(2):
            dwout_ref[h] += _mm_tn(cat, dh16[:, h * HALF:(h + 1) * HALF])
        dcat = _mm_nt(dh16, wout_ref[...])

        d_a = dcat[:, :W_GA]
        dattn_ref[...] = (d_a * silu_a).astype(BF16)
        dga_ref[...] = (d_a * attn * (sg_a * (1.0 + ga * (1.0 - sg_a)))).astype(BF16)
        d_b = dcat[:, W_GA:]
        dgb_ref[...] = (d_b * on * (sg_b * (1.0 + gb * (1.0 - sg_b)))).astype(BF16)
        d_on = d_b * silu_b
        gnw = jnp.zeros((1, GLA_DV), F32)
        do_parts = []
        for h in range(GLA_HEADS):
            vs = slice(h * GLA_DV, (h + 1) * GLA_DV)
            o_h = ob_s[:, vs]
            rs = lax.rsqrt(jnp.mean(o_h * o_h, axis=1, keepdims=True) + EPS)
            d_on_h = d_on[:, vs]
            gnw = gnw + jnp.sum(d_on_h * o_h * rs, axis=0, keepdims=True)
            gg = d_on_h * nw
            do_parts.append(rs * gg - o_h * (rs * rs * rs) * jnp.mean(gg * o_h, axis=1, keepdims=True))
        gnw_ref[...] += gnw
        dob_ref[...] = jnp.concatenate(do_parts, axis=1).astype(BF16)

    tile = lambda w: pl.BlockSpec((TM, w), lambda i: (i, 0))
    whole = lambda shape: pl.BlockSpec(shape, lambda i: tuple(0 for _ in shape), pipeline_mode=pl.Buffered(1))
    out_shape = (
        jax.ShapeDtypeStruct((s_len, W_QA), BF16),
        jax.ShapeDtypeStruct((s_len, W_KA), BF16),
        jax.ShapeDtypeStruct((s_len, W_VA), BF16),
        jax.ShapeDtypeStruct((s_len, W_QB), F32),
        jax.ShapeDtypeStruct((s_len, W_KB), F32),
        jax.ShapeDtypeStruct((s_len, W_VB), BF16),
        jax.ShapeDtypeStruct((s_len, W_R), F32),
        jax.ShapeDtypeStruct((s_len, W_GA), BF16),
        jax.ShapeDtypeStruct((s_len, W_GA), BF16),
        jax.ShapeDtypeStruct((s_len, W_GB), BF16),
        jax.ShapeDtypeStruct((s_len, W_GB), BF16),
        jax.ShapeDtypeStruct((s_len, D_MODEL), F32),
        jax.ShapeDtypeStruct((s_len // CHUNK, GLA_DV, GLA_HEADS * GLA_DK), F32),
        jax.ShapeDtypeStruct((2, D_MODEL, HALF), F32),
        jax.ShapeDtypeStruct((1, D_MODEL), F32),
        jax.ShapeDtypeStruct((1, D_MODEL), F32),
        jax.ShapeDtypeStruct((1, GLA_DV), F32),
        jax.ShapeDtypeStruct((1, 128), F32),
    )
    out_specs = (
        tile(W_QA), tile(W_KA), tile(W_VA), tile(W_QB), tile(W_KB), tile(W_VB), tile(W_R),
        tile(W_GA), tile(W_GA), tile(W_GB), tile(W_GB), tile(D_MODEL),
        pl.BlockSpec((nch, GLA_DV, GLA_HEADS * GLA_DK), lambda i: (i, 0, 0)),
        whole((2, D_MODEL, HALF)), whole((1, D_MODEL)), whole((1, D_MODEL)), whole((1, GLA_DV)), whole((1, 128)),
    )
    in_specs = [
        tile(D_MODEL), tile(D_MODEL), tile(128), tile(128), tile(128),
        whole((D_PROJ, D_MODEL)), whole((D_MODEL, D_MODEL)), whole((W_R, W_KB)), whole((1, W_KB)),
        pl.BlockSpec(memory_space=pltpu.SMEM), whole((1, GLA_DV)), whole((1, D_MODEL)), whole((1, D_MODEL)),
    ]
    scratch = [
        pltpu.VMEM((BLOCK, W_KA), BF16), pltpu.VMEM((BLOCK, W_VA), BF16),
        pltpu.VMEM((GLA_DV, GLA_HEADS * GLA_DK), F32),
        pltpu.VMEM((TM, W_GA), F32), pltpu.VMEM((TM, W_GA), F32), pltpu.VMEM((TM, W_GB), F32),
        pltpu.VMEM((TM, W_GB), F32), pltpu.VMEM((TM, D_MODEL), BF16),
    ]
    return pl.pallas_call(
        body, name="fwd_head", grid=(nt,), in_specs=in_specs, out_specs=out_specs, out_shape=out_shape,
        scratch_shapes=scratch,
        compiler_params=pltpu.CompilerParams(dimension_semantics=("arbitrary",), vmem_limit_bytes=VMEM_LIMIT),
    )(x, tgt, cos, sa, sb, win, wout, wgu, bg, sinks, nw, lng, lnb)


def _bwd_call(x, dh, qa, ka, va, dattn, dga, dob, dgb, qb, kb, vb, r, st, cos, sa, sb, win, wgu, bg, sinks):
    s_len = x.shape[0]
    nt = s_len // TM
    nblk = TM // BLOCK
    nch = TM // CHUNK

    def body(x_ref, dh_ref, qa_ref, ka_ref, va_ref, kap_ref, vap_ref, dattn_ref, dga_ref, dob_ref, dgb_ref,
             qb_ref, kb_ref, vb_ref, r_ref, st_ref, cos_ref, sa_ref, sb_ref, win_ref, wgu_ref, bg_ref, sinks_ref,
             gx_ref, dwin_ref, gsink_ref, gbg_ref, gwgu_ref,
             dproj, dk_carry, dv_carry, ds_carry, db_s, dwt):
        i = pl.program_id(0)
        t = nt - 1 - i

        @pl.when(i == 0)
        def _():
            dk_carry[...] = jnp.zeros_like(dk_carry)
            dv_carry[...] = jnp.zeros_like(dv_carry)
            ds_carry[...] = jnp.zeros_like(ds_carry)
            dwt[...] = jnp.zeros_like(dwt)
            gsink_ref[...] = jnp.zeros_like(gsink_ref)
            gbg_ref[...] = jnp.zeros_like(gbg_ref)
            gwgu_ref[...] = jnp.zeros_like(gwgu_ref)

        cos = cos_ref[...]
        sa = sa_ref[...]
        sb = sb_ref[...]
        cos4, sa4, sb4 = (jnp.concatenate([v] * 4, axis=1) for v in (cos, sa, sb))

        qa = qa_ref[...]
        ka = ka_ref[...]
        va = va_ref[...]
        dattn = dattn_ref[...]
        gsink_rows = [jnp.zeros((1, 1), F32) for _ in range(Q_HEADS)]
        for b in reversed(range(nblk)):
            rows = slice(b * BLOCK, (b + 1) * BLOCK)
            has_prev = (t * nblk + b) > 0
            mask = _attn_mask(has_prev)
            k_cur = ka[rows]
            v_cur = va[rows]
            k_old = kap_ref[...] if b == 0 else ka[(b - 1) * BLOCK:b * BLOCK]
            v_old = vap_ref[...] if b == 0 else va[(b - 1) * BLOCK:b * BLOCK]
            dq_parts, dk_parts, dv_parts = [], [], []
            for j in range(KV_HEADS):
                hs = slice(j * HEAD_A, (j + 1) * HEAD_A)
                kc = jnp.concatenate([k_old[:, hs], k_cur[:, hs]], axis=0)
                vc = jnp.concatenate([v_old[:, hs], v_cur[:, hs]], axis=0)
                qs = _stack_heads(qa[rows], j)
                do_s = _stack_heads(dattn[rows], j)
                probs, p_sink = _softmax_block(qs, kc, mask, _sink_col(sinks_ref, j))
                dp = _mm_nt(do_s, vc)
                d_row = jnp.sum(probs * dp, axis=1, keepdims=True)
                ds16 = (probs * (dp - d_row) * ATT_SCALE).astype(BF16)
                dq_parts.append(_mm(ds16, kc))
                dk_parts.append(_mm_tn(ds16, qs))
                dv_parts.append(_mm_tn(probs.astype(BF16), do_s))
                t_sink = d_row * p_sink
                for g in range(GROUP):
                    gsink_rows[GROUP * j + g] = gsink_rows[GROUP * j + g] - jnp.sum(
                        t_sink[g * BLOCK:(g + 1) * BLOCK], axis=0, keepdims=True)
            dq = _rope_bwd(_unstack_heads(dq_parts), cos4[rows], sa4[rows], sb4[rows])
            dproj[rows, O_QA:O_QA + W_QA] = dq.astype(BF16)
            dk_cur = dk_carry[...] + jnp.concatenate([p[BLOCK:] for p in dk_parts], axis=1)
            dv_cur = dv_carry[...] + jnp.concatenate([p[BLOCK:] for p in dv_parts], axis=1)
            dproj[rows, O_KA:O_KA + W_KA] = _rope_bwd(dk_cur, cos[rows], sa[rows], sb[rows]).astype(BF16)
            dproj[rows, O_VA:O_VA + W_VA] = dv_cur.astype(BF16)
            dk_carry[...] = jnp.concatenate([p[:BLOCK] for p in dk_parts], axis=1)
            dv_carry[...] = jnp.concatenate([p[:BLOCK] for p in dv_parts], axis=1)
        for hq in range(Q_HEADS):
            gsink_ref[hq:hq + 1, :] += jnp.broadcast_to(gsink_rows[hq], (1, 128))

        dproj[:, O_GA:O_GA + W_GA] = dga_ref[...]
        dproj[:, O_GB:O_GB + W_GB] = dgb_ref[...]

        r16 = r_ref[...].astype(BF16)
        wgu16 = wgu_ref[...].astype(BF16)
        z = _mm(r16, wgu16) + bg_ref[...]
        log_a = _log_sigmoid(z) / GLA_TAU
        bcum = _tri_mm(_chunk_tri(TM, False), log_a)
        qb = qb_ref[...]
        kb = kb_ref[...]
        vb = vb_ref[...]
        dob = dob_ref[...]
        e_b = jnp.exp(bcum)
        e_nb = jnp.exp(-bcum)
        qd_f = qb * GLA_SCALE * e_b
        ki_f = kb * e_nb
        qd_all = qd_f.astype(BF16)
        ki_all = ki_f.astype(BF16)
        tril = lax.broadcasted_iota(jnp.int32, (CHUNK, CHUNK), 0) >= lax.broadcasted_iota(jnp.int32, (CHUNK, CHUNK), 1)
        last_row = lax.broadcasted_iota(jnp.int32, (CHUNK, 1), 0) == CHUNK - 1
        for c in reversed(range(nch)):
            rows = slice(c * CHUNK, (c + 1) * CHUNK)
            b_c = bcum[rows]
            b_last = b_c[CHUNK - 1:CHUNK]
            e_e = jnp.exp(b_last - b_c)
            dec = jnp.exp(b_last)
            ke_f = kb[rows] * e_e
            ke = ke_f.astype(BF16)
            sp = st_ref[c]
            sp16 = sp.astype(BF16)
            dsn = ds_carry[...]
            dsn16 = dsn.astype(BF16)
            dqd_p, dki_p, dke_p, dv_p, dsp_p = [], [], [], [], []
            for h in range(GLA_HEADS):
                ks = slice(h * GLA_DK, (h + 1) * GLA_DK)
                vs = slice(h * GLA_DV, (h + 1) * GLA_DV)
                qd = qd_all[rows, ks]
                ki = ki_all[rows, ks]
                v_h = vb[rows, vs]
                do_h = dob[rows, vs]
                a16 = jnp.where(tril, _mm_nt(qd, ki), 0.0).astype(BF16)
                da16 = jnp.where(tril, _mm_nt(do_h, v_h), 0.0).astype(BF16)
                dv_p.append(_mm_tn(a16, do_h) + _mm_nt(ke[:, ks], dsn16[:, ks]))
                dqd_p.append(_mm(da16, ki) + _mm(do_h, sp16[:, ks]))
                dki_p.append(_mm_tn(da16, qd))
                dke_p.append(_mm(v_h, dsn16[:, ks]))
                dsp_p.append(_mm_tn(do_h, qd))
            dqd = jnp.concatenate(dqd_p, axis=1)
            dki = jnp.concatenate(dki_p, axis=1)
            dke = jnp.concatenate(dke_p, axis=1)
            ddec = jnp.sum(dsn * sp, axis=0, keepdims=True)
            ds_carry[...] = dsn * dec + jnp.concatenate(dsp_p, axis=1)
            dproj[rows, O_QB:O_QB + W_QB] = (dqd * e_b[rows] * GLA_SCALE).astype(BF16)
            dproj[rows, O_KB:O_KB + W_KB] = (dki * e_nb[rows] + dke * e_e).astype(BF16)
            dproj[rows, O_VB:O_VB + W_VB] = jnp.concatenate(dv_p, axis=1).astype(BF16)
            dke_ke = dke * ke_f
            d_b = dqd * qd_f[rows] - dki * ki_f[rows] - dke_ke
            d_bl = jnp.sum(dke_ke, axis=0, keepdims=True) + ddec * dec
            db_s[rows, :] = d_b + jnp.where(last_row, d_bl, 0.0)
        dlog_a = _tri_mm(_chunk_tri(TM, True), db_s[...])
        dz = dlog_a * (1.0 / GLA_TAU) * _sigmoid(-z)
        dz16 = dz.astype(BF16)
        gbg_ref[...] += jnp.sum(dz, axis=0, keepdims=True)
        gwgu_ref[...] += _mm_tn(r16, dz16)
        dproj[:, O_R:O_R + W_R] = _mm_nt(dz16, wgu16).astype(BF16)

        dp16 = dproj[...]
        gx_ref[...] = ALPHA * dh_ref[...] + _mm(dp16, win_ref[...])
        dwt[0:D_PROJ, :] += _mm_tn(dp16, x_ref[...].astype(BF16))

        @pl.when(i == nt - 1)
        def _():
            for k in range(N_CHIPS):
                for h in range(2):
                    dwin_ref[h, k] = dwt[k * SHARD_IN:k * SHARD_IN + SHARD_PAD, h * HALF:(h + 1) * HALF]

    tile = lambda w: pl.BlockSpec((TM, w), lambda i: (nt - 1 - i, 0))
    whole = lambda shape: pl.BlockSpec(shape, lambda i: tuple(0 for _ in shape), pipeline_mode=pl.Buffered(1))
    prev_blk = pl.BlockSpec((BLOCK, W_KA), lambda i: (jnp.maximum((nt - 1 - i) * nblk - 1, 0), 0))
    in_specs = [
        tile(D_MODEL), tile(D_MODEL), tile(W_QA), tile(W_KA), tile(W_VA), prev_blk, prev_blk,
        tile(W_GA), tile(W_GA), tile(W_GB), tile(W_GB), tile(W_QB), tile(W_KB), tile(W_VB), tile(W_R),
        pl.BlockSpec((nch, GLA_DV, GLA_HEADS * GLA_DK), lambda i: (nt - 1 - i, 0, 0)),
        tile(128), tile(128), tile(128),
        whole((D_PROJ, D_MODEL)), whole((W_R, W_KB)), whole((1, W_KB)), pl.BlockSpec(memory_space=pltpu.SMEM),
    ]
    out_shape = (
        jax.ShapeDtypeStruct((s_len, D_MODEL), F32),
        jax.ShapeDtypeStruct((2, N_CHIPS, SHARD_PAD, HALF), F32),
        jax.ShapeDtypeStruct((Q_HEADS, 128), F32),
        jax.ShapeDtypeStruct((1, W_KB), F32),
        jax.ShapeDtypeStruct((W_R, W_KB), F32),
    )
    out_specs = (tile(D_MODEL), whole((2, N_CHIPS, SHARD_PAD, HALF)), whole((Q_HEADS, 128)), whole((1, W_KB)),
                 whole((W_R, W_KB)))
    scratch = [
        pltpu.VMEM((TM, D_PROJ), BF16), pltpu.VMEM((BLOCK, W_KA), F32), pltpu.VMEM((BLOCK, W_VA), F32),
        pltpu.VMEM((GLA_DV, GLA_HEADS * GLA_DK), F32), pltpu.VMEM((TM, W_KB), F32),
        pltpu.VMEM((ACC_ROWS, D_MODEL), F32),
    ]
    return pl.pallas_call(
        body, name="bwd_mix", grid=(nt,), in_specs=in_specs, out_specs=out_specs, out_shape=out_shape,
        scratch_shapes=scratch,
        compiler_params=pltpu.CompilerParams(dimension_semantics=("arbitrary",), vmem_limit_bytes=VMEM_LIMIT),
    )(x, dh, qa, ka, va, ka, va, dattn, dga, dob, dgb, qb, kb, vb, r, st, cos, sa, sb, win, wgu, bg, sinks)


def _mesh_place():
    x, y, c = lax.axis_index("x"), lax.axis_index("y"), lax.axis_index("c")
    chips = [(1 - x, y), (x, 1 - y), (1 - x, 1 - y)]
    return x, y, c, chips


def _gather_weights_call(w_lin, w_out, wgu):
    def body(wlin_ref, wout_ref, wgu_ref, wt_ref, wout_full, wgu_all, blk, oblk, asm, send_sems, recv_sems):
        x, y, c, chips = _mesh_place()
        k_me = 2 * x + y
        asm[SHARD_IN - 4:SHARD_PAD, :] = jnp.zeros((SHARD_PAD - SHARD_IN + 4, D_MODEL), F32)
        for a in range(8):
            asm[0:SHARD_IN, a * 128:(a + 1) * 128] = wlin_ref[pl.ds(a, SHARD_IN, stride=8), :]
        for h in range(2):
            blk[k_me, h] = asm[0:SHARD_PAD, h * HALF:(h + 1) * HALF].astype(BF16)
            oblk[k_me, h] = wout_ref[:, h * HALF:(h + 1) * HALF].astype(BF16)
        wgu_all[k_me] = wgu_ref[...]

        def blocks(k, hc):
            return (blk.at[k, hc], oblk.at[k, hc])

        def copies(k, hc, sem0, to):
            return [pltpu.make_async_remote_copy(src_ref=blk, dst_ref=blk, send_sem=send_sems.at[sem0 + n],
                                                 recv_sem=recv_sems.at[sem0 + n], device_id=to, device_id_type=MESH)
                    for n, blk in enumerate(blocks(k, hc))]

        def gu_copy(k, r, to):
            return pltpu.make_async_remote_copy(src_ref=wgu_all.at[k], dst_ref=wgu_all.at[k], send_sem=send_sems.at[12 + r],
                                                recv_sem=recv_sems.at[12 + r], device_id=to, device_id_type=MESH)

        started = []
        for r, chip in enumerate(chips):
            started += copies(k_me, c, 2 * r, (*chip, c))
            started.append(gu_copy(k_me, r, (*chip, c)))
        for cp in started:
            cp.start()
        for r, chip in enumerate(chips):
            k_r = 2 * chip[0] + chip[1]
            for cp in copies(k_r, c, 2 * r, (x, y, c)):
                cp.wait_recv()
            passed = copies(k_r, c, 6 + 2 * r, (x, y, 1 - c))
            for cp in passed:
                cp.start()
            started += passed
        for r, chip in enumerate(chips):
            k_r = 2 * chip[0] + chip[1]
            for cp in copies(k_r, 1 - c, 6 + 2 * r, (x, y, c)):
                cp.wait_recv()
            gu_copy(k_r, r, (x, y, c)).wait_recv()
        for cp in started:
            cp.wait_send()

        for k in range(N_CHIPS):
            for h in range(2):
                asm[k * SHARD_IN:k * SHARD_IN + SHARD_PAD, h * HALF:(h + 1) * HALF] = blk[k, h].astype(F32)
                wout_full[k * SHARD_OUT:(k + 1) * SHARD_OUT, h * HALF:(h + 1) * HALF] = oblk[k, h]
        wt_ref[...] = asm[0:D_PROJ, :].astype(BF16)

    vmem = pl.BlockSpec(memory_space=pltpu.VMEM)
    return pl.pallas_call(
        body, name="gather_weights",
        out_shape=(jax.ShapeDtypeStruct((D_PROJ, D_MODEL), BF16),
                   jax.ShapeDtypeStruct((D_MODEL, D_MODEL), BF16),
                   jax.ShapeDtypeStruct((N_CHIPS, W_R, W_KB // N_CHIPS), F32)),
        in_specs=[vmem, vmem, vmem], out_specs=(vmem, vmem, vmem),
        scratch_shapes=[pltpu.VMEM((N_CHIPS, 2, SHARD_PAD, HALF), BF16), pltpu.VMEM((N_CHIPS, 2, SHARD_OUT, HALF), BF16),
                        pltpu.VMEM((ACC_ROWS, D_MODEL), F32),
                        pltpu.SemaphoreType.DMA((15,)), pltpu.SemaphoreType.DMA((15,))],
        compiler_params=pltpu.CompilerParams(vmem_limit_bytes=VMEM_LIMIT),
    )(w_lin, w_out, wgu)


def _adamw(w, g, m, v):
    m = ADAM_B1 * m + (1.0 - ADAM_B1) * g
    v = ADAM_B2 * v + (1.0 - ADAM_B2) * (g * g)
    m_hat = m / (1.0 - ADAM_B1 ** ADAM_STEP)
    v_hat = v / (1.0 - ADAM_B2 ** ADAM_STEP)
    delta = -ADAM_LR * (m_hat / (jnp.sqrt(v_hat) + ADAM_EPS) + ADAM_WD * w)
    return delta, m, v


def _reduce_grads_call(g_in, g_out, pack, w_small, m_small, v_small):
    def body(gin_hbm, gout_hbm, pack_ref, ws_ref, ms_ref, vs_ref,
             lin_in, fin_out, gs_ref, ds_ref, nms_ref, nvs_ref,
             a_in, a_out, b_in, b_out, s_in, s_out, r_in, r_out, f_in, f_out, pack_all,
             send_sems, recv_sems, local_sems):
        x, y, c, chips = _mesh_place()
        k_me = 2 * x + y
        me = 4 * x + 2 * y + c
        sibling = (x, y, 1 - c)

        pack_all[me] = pack_ref[...]
        small = []
        for mask in range(1, 8):
            peer = (x ^ (mask >> 2), y ^ ((mask >> 1) & 1), c ^ (mask & 1))
            small.append(pltpu.make_async_remote_copy(
                src_ref=pack_ref, dst_ref=pack_all.at[me], send_sem=send_sems.at[mask], recv_sem=recv_sems.at[mask],
                device_id=peer, device_id_type=MESH))
        for cp in small:
            cp.start()

        mine = [pltpu.make_async_copy(gin_hbm.at[c], a_in, local_sems.at[0]),
                pltpu.make_async_copy(gout_hbm.at[c], a_out, local_sems.at[1])]
        to_sib = [pltpu.make_async_remote_copy(
                      src_ref=gin_hbm.at[1 - c], dst_ref=b_in,
                      send_sem=send_sems.at[8], recv_sem=recv_sems.at[8], device_id=sibling, device_id_type=MESH),
                  pltpu.make_async_remote_copy(
                      src_ref=gout_hbm.at[1 - c], dst_ref=b_out,
                      send_sem=send_sems.at[9], recv_sem=recv_sems.at[9], device_id=sibling, device_id_type=MESH)]
        for cp in mine + to_sib:
            cp.start()
        for cp in mine:
            cp.wait()
        for cp in to_sib:
            cp.wait_recv()

        sent = []
        for r, chip in enumerate(chips):
            k_r = 2 * chip[0] + chip[1]
            s_in[r] = (a_in[k_r] + b_in[k_r]).astype(BF16)
            s_out[r] = (a_out[k_r] + b_out[k_r]).astype(BF16)
            sent.append(pltpu.make_async_remote_copy(
                src_ref=s_in.at[r], dst_ref=r_in.at[r], send_sem=send_sems.at[10 + 2 * r],
                recv_sem=recv_sems.at[10 + 2 * r], device_id=(*chip, c), device_id_type=MESH))
            sent.append(pltpu.make_async_remote_copy(
                src_ref=s_out.at[r], dst_ref=r_out.at[r], send_sem=send_sems.at[11 + 2 * r],
                recv_sem=recv_sems.at[11 + 2 * r], device_id=(*chip, c), device_id_type=MESH))
            sent[-2].start()
            sent[-1].start()
        own_in = a_in[k_me] + b_in[k_me]
        own_out = a_out[k_me] + b_out[k_me]
        for cp in sent:
            cp.wait_recv()
        f_in[c] = own_in + r_in[0].astype(F32) + r_in[1].astype(F32) + r_in[2].astype(F32)
        f_out[c] = own_out + r_out[0].astype(F32) + r_out[1].astype(F32) + r_out[2].astype(F32)

        swap = [pltpu.make_async_remote_copy(
                    src_ref=f_in.at[c], dst_ref=f_in.at[c],
                    send_sem=send_sems.at[16], recv_sem=recv_sems.at[16], device_id=sibling, device_id_type=MESH),
                pltpu.make_async_remote_copy(
                    src_ref=f_out.at[c], dst_ref=f_out.at[c],
                    send_sem=send_sems.at[17], recv_sem=recv_sems.at[17], device_id=sibling, device_id_type=MESH)]
        for cp in swap:
            cp.start()

        for cp in small:
            cp.wait_recv()
        total = pack_all[0]
        for d in range(1, 8):
            total = total + pack_all[d]
        gu_rows = pl.ds(pl.multiple_of(P_GU + 16 * k_me, 8), 16)
        pack_all[0] = total
        g_small = jnp.concatenate([total[:P_GU], pack_all[0, gu_rows, :]], axis=0)
        delta, new_m, new_v = _adamw(ws_ref[...], g_small, ms_ref[...], vs_ref[...])
        gs_ref[...] = g_small
        ds_ref[...] = delta
        nms_ref[...] = new_m
        nvs_ref[...] = new_v

        other_in = pltpu.make_async_remote_copy(
            src_ref=f_in.at[1 - c], dst_ref=f_in.at[1 - c],
            send_sem=send_sems.at[16], recv_sem=recv_sems.at[16], device_id=sibling, device_id_type=MESH)
        other_out = pltpu.make_async_remote_copy(
            src_ref=f_out.at[1 - c], dst_ref=f_out.at[1 - c],
            send_sem=send_sems.at[17], recv_sem=recv_sems.at[17], device_id=sibling, device_id_type=MESH)
        other_in.wait_recv()
        other_out.wait_recv()
        for cp in small + to_sib + sent + swap:
            cp.wait_send()

        for a in range(8):
            lin_in[pl.ds(a, SHARD_IN, stride=8), :] = f_in[a // 4, 0:SHARD_IN, (a % 4) * 128:(a % 4 + 1) * 128]
        for h in range(2):
            fin_out[:, h * HALF:(h + 1) * HALF] = f_out[h]

    vmem = pl.BlockSpec(memory_space=pltpu.VMEM)
    hbm = pl.BlockSpec(memory_space=pl.ANY)
    small_shape = jax.ShapeDtypeStruct((PACK2_ROWS, 128), F32)
    return pl.pallas_call(
        body, name="reduce_grads",
        out_shape=(jax.ShapeDtypeStruct((LIN_ROWS, 128), F32), jax.ShapeDtypeStruct((SHARD_OUT, D_MODEL), F32),
                   small_shape, small_shape, small_shape, small_shape),
        in_specs=[hbm, hbm, vmem, vmem, vmem, vmem], out_specs=(vmem,) * 6,
        scratch_shapes=[
            pltpu.VMEM((N_CHIPS, SHARD_PAD, HALF), F32), pltpu.VMEM((N_CHIPS, SHARD_OUT, HALF), F32),
            pltpu.VMEM((N_CHIPS, SHARD_PAD, HALF), F32), pltpu.VMEM((N_CHIPS, SHARD_OUT, HALF), F32),
            pltpu.VMEM((3, SHARD_PAD, HALF), BF16), pltpu.VMEM((3, SHARD_OUT, HALF), BF16),
            pltpu.VMEM((3, SHARD_PAD, HALF), BF16), pltpu.VMEM((3, SHARD_OUT, HALF), BF16),
            pltpu.VMEM((2, SHARD_PAD, HALF), F32), pltpu.VMEM((2, SHARD_OUT, HALF), F32),
            pltpu.VMEM((8, PACK_ROWS, 128), F32),
            pltpu.SemaphoreType.DMA((18,)), pltpu.SemaphoreType.DMA((18,)), pltpu.SemaphoreType.DMA((2,)),
        ],
        compiler_params=pltpu.CompilerParams(vmem_limit_bytes=VMEM_LIMIT),
    )(g_in, g_out, pack, w_small, m_small, v_small)


def _adamw_call(g_in, w_in, m_in, v_in, g_out, w_out, m_out, v_out):
    steps = 4
    rows_in = LIN_ROWS // steps
    rows_out = SHARD_OUT // steps

    def body(gi, wi, mi, vi, go, wo, mo, vo, di, nmi, nvi, do, nmo, nvo):
        di[...], nmi[...], nvi[...] = _adamw(wi[...], gi[...], mi[...], vi[...])
        do[...], nmo[...], nvo[...] = _adamw(wo[...], go[...], mo[...], vo[...])

    t_in = pl.BlockSpec((rows_in, 128), lambda i: (i, 0))
    t_out = pl.BlockSpec((rows_out, D_MODEL), lambda i: (i, 0))
    s_in = jax.ShapeDtypeStruct((LIN_ROWS, 128), F32)
    s_out = jax.ShapeDtypeStruct((SHARD_OUT, D_MODEL), F32)
    return pl.pallas_call(
        body, name="adamw", grid=(steps,), in_specs=[t_in] * 4 + [t_out] * 4, out_specs=(t_in,) * 3 + (t_out,) * 3,
        out_shape=(s_in,) * 3 + (s_out,) * 3,
        compiler_params=pltpu.CompilerParams(dimension_semantics=("arbitrary",)),
    )(g_in, w_in, m_in, v_in, g_out, w_out, m_out, v_out)


def _rope_tables(positions):
    half = 8
    inv_freq = 500000.0 ** (-jnp.arange(half, dtype=F32) / half)
    d = np.arange(128) % HEAD_A
    rotated = d < 2 * half
    freq = jnp.where(rotated, inv_freq[d % half], 0.0)
    ang = positions.astype(F32)[:, None] * freq[None, :]
    sin = jnp.sin(ang)
    sa = sin * np.where(d < half, -1.0, 0.0).astype(np.float32)
    sb = sin * np.where(rotated & (d >= half), 1.0, 0.0).astype(np.float32)
    return jnp.cos(ang), sa, sb


def _pad_lanes(a):
    return jnp.pad(a, ((0, 0), (0, 128 - a.shape[1])))


def _pack_small(ln_g, ln_b, b_gate, norm_w, sinks, gate_up_rows, extra_rows):
    return jnp.concatenate([
        ln_g.reshape(8, 128), ln_b.reshape(8, 128), b_gate.reshape(2, 128), norm_w.reshape(1, 128),
        _pad_lanes(sinks.reshape(1, Q_HEADS)), extra_rows, gate_up_rows], axis=0)


def kernel(x, positions, w_in, gla_w_gate_up, gla_b_gate, attn_sinks, gla_norm_w, w_out, ln_g, ln_b, loss_target, m_w_in, m_gla_w_gate_up, m_gla_b_gate, m_attn_sinks, m_gla_norm_w, m_w_out, m_ln_g, m_ln_b, v_w_in, v_gla_w_gate_up, v_gla_b_gate, v_attn_sinks, v_gla_norm_w, v_w_out, v_ln_g, v_ln_b):
    def lin(w):
        return jnp.transpose(w[0]).reshape(LIN_ROWS, 128)

    def unlin(w):
        return jnp.transpose(w.reshape(SHARD_IN, D_MODEL))[None]

    win, wout, wgu_all = _gather_weights_call(lin(w_in), w_out[0], gla_w_gate_up[0])
    wgu = jnp.transpose(wgu_all, (1, 0, 2)).reshape(W_R, W_KB)
    cos, sa, sb = _rope_tables(positions[0])
    sinks = attn_sinks[0]

    (qa, ka, va, qb, kb, vb, r, dattn, dga, dob, dgb, dh, st, g_wout, g_lng, g_lnb, g_nw, loss) = _fwd_call(
        x[0], loss_target[0], cos, sa, sb, win, wout, wgu, gla_b_gate, sinks, gla_norm_w, ln_g, ln_b)
    gx, g_win, g_sink, g_bg, g_wgu = _bwd_call(
        x[0], dh, qa, ka, va, dattn, dga, dob, dgb, qb, kb, vb, r, st, cos, sa, sb, win, wgu, gla_b_gate, sinks)

    g_wout_by_chip = g_wout.reshape(2, N_CHIPS, SHARD_OUT, HALF)
    gu_rows = _pad_lanes(jnp.transpose(g_wgu.reshape(W_R, N_CHIPS, W_KB // N_CHIPS), (1, 0, 2)).reshape(N_CHIPS * W_R, -1))
    loss_rows = jnp.concatenate([loss, jnp.zeros((3, 128), F32)], axis=0)
    pack = _pack_small(g_lng, g_lnb, g_bg, g_nw, g_sink[:, 0], gu_rows, loss_rows)

    def small(ln_g_, ln_b_, b_gate_, norm_w_, sinks_, gate_up_):
        return _pack_small(ln_g_, ln_b_, b_gate_, norm_w_, sinks_[0], _pad_lanes(gate_up_[0]), jnp.zeros((4, 128), F32))

    w_small = small(ln_g, ln_b, gla_b_gate, gla_norm_w, attn_sinks, gla_w_gate_up)
    m_small = small(m_ln_g, m_ln_b, m_gla_b_gate, m_gla_norm_w, m_attn_sinks, m_gla_w_gate_up)
    v_small = small(v_ln_g, v_ln_b, v_gla_b_gate, v_gla_norm_w, v_attn_sinks, v_gla_w_gate_up)

    fin_in, fin_out, g_s, d_s, nm_s, nv_s = _reduce_grads_call(g_win, g_wout_by_chip, pack, w_small, m_small, v_small)
    d_in, nm_in, nv_in, d_out, nm_out, nv_out = _adamw_call(
        fin_in, lin(w_in), lin(m_w_in), lin(v_w_in), fin_out, w_out[0], m_w_out[0], v_w_out[0])

    def unpack(p, big_in, big_out):
        return (unlin(big_in), p[P_GU:P_GU + W_R, :W_KB // N_CHIPS][None], p[P_BG:P_BG + 2].reshape(1, W_KB),
                p[P_SINK:P_SINK + 1, :Q_HEADS], p[P_NW:P_NW + 1], big_out[None],
                p[P_LNG:P_LNG + 8].reshape(1, D_MODEL), p[P_LNB:P_LNB + 8].reshape(1, D_MODEL))

    loss_total = g_s[P_LOSS, 0]
    return (loss_total, gx[None], *unpack(g_s, fin_in, fin_out), *unpack(d_s, d_in, d_out),
            *unpack(nm_s, nm_in, nm_out), *unpack(nv_s, nv_in, nv_out))
```

```python
import functools

import jax
import jax.numpy as jnp
import numpy as np
from jax import lax
from jax.experimental import pallas as pl
from jax.experimental.pallas import tpu as pltpu

F32 = jnp.float32
BF16 = jnp.bfloat16
MESH = pl.DeviceIdType.MESH

D_MODEL = 1024
N_CHIPS = 4
W_QA, W_KA, W_VA, W_GA, W_QB, W_KB, W_VB, W_GB, W_R = 512, 128, 128, 512, 256, 256, 512, 512, 16
O_QA = 0
O_KA = O_QA + W_QA
O_VA = O_KA + W_KA
O_GA = O_VA + W_VA
O_QB = O_GA + W_GA
O_KB = O_QB + W_QB
O_VB = O_KB + W_KB
O_GB = O_VB + W_VB
O_R = O_GB + W_GB
D_PROJ = O_R + W_R
SHARD_IN = D_PROJ // N_CHIPS
SHARD_OUT = D_MODEL // N_CHIPS
SHARD_PAD = 720
ACC_ROWS = -(-((N_CHIPS - 1) * SHARD_IN + SHARD_PAD) // 8) * 8
LIN_ROWS = SHARD_IN * D_MODEL // 128
HALF = D_MODEL // 2

HEAD_A = 64
Q_HEADS = 8
KV_HEADS = 2
GROUP = 4
BLOCK = 128
GLA_HEADS = 4
GLA_DK = 64
GLA_DV = 128
CHUNK = 64
GLA_TAU = 16.0
EPS = 1e-5
ALPHA = 2.0 ** 0.25
ATT_SCALE = HEAD_A ** -0.5
GLA_SCALE = GLA_DK ** -0.5

ADAM_LR = 0.001
ADAM_B1 = 0.9
ADAM_B2 = 0.999
ADAM_EPS = 1e-08
ADAM_WD = 0.01
ADAM_STEP = 10

TM = 512
TRI_SLAB = 128
VMEM_LIMIT = 60 * 1024 * 1024

P_LNG, P_LNB, P_BG, P_NW, P_SINK, P_LOSS, P_GU = 0, 8, 16, 18, 19, 20, 24
PACK_ROWS = P_GU + N_CHIPS * 16
PACK2_ROWS = P_GU + 16


def _mm(a, b):
    return jnp.dot(a, b, preferred_element_type=F32)


def _mm_nt(a, b):
    return lax.dot_general(a, b, (((1,), (1,)), ((), ())), preferred_element_type=F32)


def _mm_tn(a, b):
    return lax.dot_general(a, b, (((0,), (0,)), ((), ())), preferred_element_type=F32)


def _split3(a):
    hi = a.astype(BF16)
    r1 = a - hi.astype(F32)
    mid = r1.astype(BF16)
    lo = (r1 - mid.astype(F32)).astype(BF16)
    return hi, mid, lo


def _tri_mm(tri, a):
    slab = tri.shape[0]
    hi, mid, lo = _split3(a)
    return jnp.concatenate(
        [_mm(tri, hi[s:s + slab]) + _mm(tri, mid[s:s + slab]) + _mm(tri, lo[s:s + slab])
         for s in range(0, a.shape[0], slab)], axis=0)


def _chunk_tri(n, upper):
    r = lax.broadcasted_iota(jnp.int32, (n, n), 0)
    c = lax.broadcasted_iota(jnp.int32, (n, n), 1)
    same = (r >> 6) == (c >> 6)
    order = (c >= r) if upper else (c <= r)
    return jnp.where(same & order, 1.0, 0.0).astype(BF16)


def _rope(t, cos, sa, sb):
    w = t.shape[1]
    return t * cos + pltpu.roll(t, w - 8, 1) * sa + pltpu.roll(t, 8, 1) * sb


def _rope_bwd(d, cos, sa, sb):
    w = d.shape[1]
    return d * cos + pltpu.roll(d * sa, 8, 1) + pltpu.roll(d * sb, w - 8, 1)


def _log_sigmoid(z):
    return jnp.minimum(z, 0.0) - jnp.log1p(jnp.exp(-jnp.abs(z)))


def _sigmoid(z):
    return 1.0 / (1.0 + jnp.exp(-z))


def _attn_mask(has_prev):
    r = lax.broadcasted_iota(jnp.int32, (GROUP * BLOCK, 2 * BLOCK), 0) & (BLOCK - 1)
    k = lax.broadcasted_iota(jnp.int32, (GROUP * BLOCK, 2 * BLOCK), 1)
    first_key = jnp.where(has_prev, 0, BLOCK)
    return (k > r) & (k <= r + BLOCK) & (k >= first_key)


def _sink_col(sinks_ref, j):
    r = lax.broadcasted_iota(jnp.int32, (GROUP * BLOCK, 1), 0) >> 7
    col = jnp.full((GROUP * BLOCK, 1), sinks_ref[GROUP * j], F32)
    for g in range(1, GROUP):
        col = jnp.where(r == g, sinks_ref[GROUP * j + g], col)
    return col


def _stack_heads(t, j):
    return jnp.concatenate([t[:, (GROUP * j + g) * HEAD_A:(GROUP * j + g + 1) * HEAD_A] for g in range(GROUP)], axis=0)


def _unstack_heads(parts):
    return jnp.concatenate([parts[j][g * BLOCK:(g + 1) * BLOCK] for j in range(KV_HEADS) for g in range(GROUP)], axis=1)


def _softmax_block(qs, kc, mask, sink):
    s = _mm_nt(qs, kc) * ATT_SCALE
    s = jnp.where(mask, s, -jnp.inf)
    m = jnp.maximum(jnp.max(s, axis=1, keepdims=True), sink)
    p = jnp.exp(s - m)
    e_sink = jnp.exp(sink - m)
    inv = 1.0 / (jnp.sum(p, axis=1, keepdims=True) + e_sink)
    return p * inv, e_sink * inv


def _fwd_call(x, tgt, cos, sa, sb, win, wout, wgu, bg, sinks, nw, lng, lnb):
    s_len = x.shape[0]
    nt = s_len // TM
    nblk = TM // BLOCK
    nch = TM // CHUNK

    def body(x_ref, t_ref, cos_ref, sa_ref, sb_ref, win_ref, wout_ref, wgu_ref, bg_ref, sinks_ref, nw_ref,
             lng_ref, lnb_ref,
             qa_ref, ka_ref, va_ref, qb_ref, kb_ref, vb_ref, r_ref, dattn_ref, dga_ref, dob_ref, dgb_ref, dh_ref,
             st_ref, dwout_ref, glng_ref, glnb_ref, gnw_ref, loss_ref,
             kprev, vprev, state, attn_s, ga_s, ob_s, gb_s, cat_s):
        i = pl.program_id(0)

        @pl.when(i == 0)
        def _():
            kprev[...] = jnp.zeros_like(kprev)
            vprev[...] = jnp.zeros_like(vprev)
            state[...] = jnp.zeros_like(state)
            dwout_ref[...] = jnp.zeros_like(dwout_ref)
            glng_ref[...] = jnp.zeros_like(glng_ref)
            glnb_ref[...] = jnp.zeros_like(glnb_ref)
            gnw_ref[...] = jnp.zeros_like(gnw_ref)
            loss_ref[...] = jnp.zeros_like(loss_ref)

        x = x_ref[...]
        xb = x.astype(BF16)

        def proj(off, width):
            return _mm_nt(xb, win_ref[off:off + width, :])

        cos = cos_ref[...]
        sa = sa_ref[...]
        sb = sb_ref[...]
        cos4, sa4, sb4 = (jnp.concatenate([t] * 4, axis=1) for t in (cos, sa, sb))
        qa = _rope(proj(O_QA, W_QA), cos4, sa4, sb4).astype(BF16)
        ka = _rope(proj(O_KA, W_KA), cos, sa, sb).astype(BF16)
        va = proj(O_VA, W_VA).astype(BF16)
        qa_ref[...] = qa
        ka_ref[...] = ka
        va_ref[...] = va
        ga_s[...] = proj(O_GA, W_GA)
        gb_s[...] = proj(O_GB, W_GB)

        for b in range(nblk):
            rows = slice(b * BLOCK, (b + 1) * BLOCK)
            has_prev = (i * nblk + b) > 0
            mask = _attn_mask(has_prev)
            k_cur = ka[rows]
            v_cur = va[rows]
            k_old = kprev[...] if b == 0 else ka[(b - 1) * BLOCK:b * BLOCK]
            v_old = vprev[...] if b == 0 else va[(b - 1) * BLOCK:b * BLOCK]
            outs = []
            for j in range(KV_HEADS):
                hs = slice(j * HEAD_A, (j + 1) * HEAD_A)
                kc = jnp.concatenate([k_old[:, hs], k_cur[:, hs]], axis=0)
                vc = jnp.concatenate([v_old[:, hs], v_cur[:, hs]], axis=0)
                probs, _ = _softmax_block(_stack_heads(qa[rows], j), kc, mask, _sink_col(sinks_ref, j))
                outs.append(_mm(probs.astype(BF16), vc))
            attn_s[rows, :] = _unstack_heads(outs)
        kprev[...] = ka[(nblk - 1) * BLOCK:]
        vprev[...] = va[(nblk - 1) * BLOCK:]

        r = proj(O_R, W_R)
        r_ref[...] = r
        z = _mm(r.astype(BF16), wgu_ref[...].astype(BF16)) + bg_ref[...]
        log_a = _log_sigmoid(z) / GLA_TAU
        bcum = _tri_mm(_chunk_tri(TRI_SLAB, False), log_a)
        qb = proj(O_QB, W_QB)
        kb = proj(O_KB, W_KB)
        vb = proj(O_VB, W_VB).astype(BF16)
        qb_ref[...] = qb
        kb_ref[...] = kb
        vb_ref[...] = vb
        qd_all = (qb * GLA_SCALE * jnp.exp(bcum)).astype(BF16)
        ki_all = (kb * jnp.exp(-bcum)).astype(BF16)
        tril = lax.broadcasted_iota(jnp.int32, (CHUNK, CHUNK), 0) >= lax.broadcasted_iota(jnp.int32, (CHUNK, CHUNK), 1)
        for c in range(nch):
            rows = slice(c * CHUNK, (c + 1) * CHUNK)
            b_c = bcum[rows]
            b_last = b_c[CHUNK - 1:CHUNK]
            ke = (kb[rows] * jnp.exp(b_last - b_c)).astype(BF16)
            st = state[...]
            st_ref[c] = st
            st16 = st.astype(BF16)
            o_parts, u_parts = [], []
            for h in range(GLA_HEADS):
                ks = slice(h * GLA_DK, (h + 1) * GLA_DK)
                vs = slice(h * GLA_DV, (h + 1) * GLA_DV)
                qd = qd_all[rows, ks]
                v_h = vb[rows, vs]
                a = jnp.where(tril, _mm_nt(qd, ki_all[rows, ks]), 0.0)
                o_parts.append(_mm(a.astype(BF16), v_h) + _mm_nt(qd, st16[:, ks]))
                u_parts.append(_mm_tn(v_h, ke[:, ks]))
            ob_s[rows, :] = jnp.concatenate(o_parts, axis=1)
            state[...] = st * jnp.exp(b_last) + jnp.concatenate(u_parts, axis=1)

        ga = ga_s[...]
        sg_a = _sigmoid(ga)
        silu_a = ga * sg_a
        attn = attn_s[...]
        cat_s[:, :W_GA] = (attn * silu_a).astype(BF16)
        gb = gb_s[...]
        sg_b = _sigmoid(gb)
        silu_b = gb * sg_b
        nw = nw_ref[...]
        on_parts = []
        for h in range(GLA_HEADS):
            vs = slice(h * GLA_DV, (h + 1) * GLA_DV)
            o_h = ob_s[:, vs]
            rs = lax.rsqrt(jnp.mean(o_h * o_h, axis=1, keepdims=True) + EPS)
            on_parts.append(o_h * rs * nw)
        on = jnp.concatenate(on_parts, axis=1)
        cat_s[:, W_GA:] = (on * silu_b).astype(BF16)
        cat = cat_s[...]
        hres = ALPHA * x + _mm(cat, wout_ref[...])
        mu = jnp.mean(hres, axis=1, keepdims=True)
        hc = hres - mu
        rstd = lax.rsqrt(jnp.mean(hc * hc, axis=1, keepdims=True) + EPS)
        xhat = hc * rstd
        g_ln = lng_ref[...]
        err = xhat * g_ln + lnb_ref[...] - t_ref[...]
        loss_ref[...] += jnp.sum(err * err) * (0.5 / D_MODEL)
        dy = err * (1.0 / D_MODEL)
        glng_ref[...] += jnp.sum(dy * xhat, axis=0, keepdims=True)
        glnb_ref[...] += jnp.sum(dy, axis=0, keepdims=True)
        dxh = dy * g_ln
        dh = rstd * (dxh - jnp.mean(dxh, axis=1, keepdims=True) - xhat * jnp.mean(dxh * xhat, axis=1, keepdims=True))
        dh_ref[...] = dh
        dh16 = dh.astype(BF16)
        for h in range(2):
            dwout_ref[h] += _mm_tn(cat, dh16[:, h * HALF:(h + 1) * HALF])
        dcat = _mm_nt(dh16, wout_ref[...])

        d_a = dcat[:, :W_GA]
        dattn_ref[...] = (d_a * silu_a).astype(BF16)
        dga_ref[...] = (d_a * attn * (sg_a * (1.0 + ga * (1.0 - sg_a)))).astype(BF16)
        d_b = dcat[:, W_GA:]
        dgb_ref[...] = (d_b * on * (sg_b * (1.0 + gb * (1.0 - sg_b)))).astype(BF16)
        d_on = d_b * silu_b
        gnw = jnp.zeros((1, GLA_DV), F32)
        do_parts = []
        for h in range(GLA_HEADS):
            vs = slice(h * GLA_DV, (h + 1) * GLA_DV)
            o_h = ob_s[:, vs]
            rs = lax.rsqrt(jnp.mean(o_h * o_h, axis=1, keepdims=True) + EPS)
            d_on_h = d_on[:, vs]
            gnw = gnw + jnp.sum(d_on_h * o_h * rs, axis=0, keepdims=True)
            gg = d_on_h * nw
            do_parts.append(rs * gg - o_h * (rs * rs * rs) * jnp.mean(gg * o_h, axis=1, keepdims=True))
        gnw_ref[...] += gnw
        dob_ref[...] = jnp.concatenate(do_parts, axis=1).astype(BF16)

    tile = lambda w: pl.BlockSpec((TM, w), lambda i: (i, 0))
    whole = lambda shape: pl.BlockSpec(shape, lambda i: tuple(0 for _ in shape), pipeline_mode=pl.Buffered(1))
    out_shape = (
        jax.ShapeDtypeStruct((s_len, W_QA), BF16),
        jax.ShapeDtypeStruct((s_len, W_KA), BF16),
        jax.ShapeDtypeStruct((s_len, W_VA), BF16),
        jax.ShapeDtypeStruct((s_len, W_QB), F32),
        jax.ShapeDtypeStruct((s_len, W_KB), F32),
        jax.ShapeDtypeStruct((s_len, W_VB), BF16),
        jax.ShapeDtypeStruct((s_len, W_R), F32),
        jax.ShapeDtypeStruct((s_len, W_GA), BF16),
        jax.ShapeDtypeStruct((s_len, W_GA), BF16),
        jax.ShapeDtypeStruct((s_len, W_GB), BF16),
        jax.ShapeDtypeStruct((s_len, W_GB), BF16),
        jax.ShapeDtypeStruct((s_len, D_MODEL), F32),
        jax.ShapeDtypeStruct((s_len // CHUNK, GLA_DV, GLA_HEADS * GLA_DK), F32),
        jax.ShapeDtypeStruct((2, D_MODEL, HALF), F32),
        jax.ShapeDtypeStruct((1, D_MODEL), F32),
        jax.ShapeDtypeStruct((1, D_MODEL), F32),
        jax.ShapeDtypeStruct((1, GLA_DV), F32),
        jax.ShapeDtypeStruct((1, 128), F32),
    )
    out_specs = (
        tile(W_QA), tile(W_KA), tile(W_VA), tile(W_QB), tile(W_KB), tile(W_VB), tile(W_R),
        tile(W_GA), tile(W_GA), tile(W_GB), tile(W_GB), tile(D_MODEL),
        pl.BlockSpec((nch, GLA_DV, GLA_HEADS * GLA_DK), lambda i: (i, 0, 0)),
        whole((2, D_MODEL, HALF)), whole((1, D_MODEL)), whole((1, D_MODEL)), whole((1, GLA_DV)), whole((1, 128)),
    )
    in_specs = [
        tile(D_MODEL), tile(D_MODEL), tile(128), tile(128), tile(128),
        whole((D_PROJ, D_MODEL)), whole((D_MODEL, D_MODEL)), whole((W_R, W_KB)), whole((1, W_KB)),
        pl.BlockSpec(memory_space=pltpu.SMEM), whole((1, GLA_DV)), whole((1, D_MODEL)), whole((1, D_MODEL)),
    ]
    scratch = [
        pltpu.VMEM((BLOCK, W_KA), BF16), pltpu.VMEM((BLOCK, W_VA), BF16),
        pltpu.VMEM((GLA_DV, GLA_HEADS * GLA_DK), F32),
        pltpu.VMEM((TM, W_GA), F32), pltpu.VMEM((TM, W_GA), F32), pltpu.VMEM((TM, W_GB), F32),
        pltpu.VMEM((TM, W_GB), F32), pltpu.VMEM((TM, D_MODEL), BF16),
    ]
    return pl.pallas_call(
        body, name="fwd_head", grid=(nt,), in_specs=in_specs, out_specs=out_specs, out_shape=out_shape,
        scratch_shapes=scratch,
        compiler_params=pltpu.CompilerParams(dimension_semantics=("arbitrary",), vmem_limit_bytes=VMEM_LIMIT),
    )(x, tgt, cos, sa, sb, win, wout, wgu, bg, sinks, nw, lng, lnb)


def _bwd_call(x, dh, qa, ka, va, dattn, dga, dob, dgb, qb, kb, vb, r, st, cos, sa, sb, win, wgu, bg, sinks):
    s_len = x.shape[0]
    nt = s_len // TM
    nblk = TM // BLOCK
    nch = TM // CHUNK

    def body(x_ref, dh_ref, qa_ref, ka_ref, va_ref, kap_ref, vap_ref, dattn_ref, dga_ref, dob_ref, dgb_ref,
             qb_ref, kb_ref, vb_ref, r_ref, st_ref, cos_ref, sa_ref, sb_ref, win_ref, wgu_ref, bg_ref, sinks_ref,
             gx_ref, dwin_ref, gsink_ref, gbg_ref, gwgu_ref,
             dproj, dk_carry, dv_carry, ds_carry, db_s):
        i = pl.program_id(0)
        t = nt - 1 - i

        @pl.when(i == 0)
        def _():
            dk_carry[...] = jnp.zeros_like(dk_carry)
            dv_carry[...] = jnp.zeros_like(dv_carry)
            ds_carry[...] = jnp.zeros_like(ds_carry)
            dwin_ref[...] = jnp.zeros_like(dwin_ref)
            gsink_ref[...] = jnp.zeros_like(gsink_ref)
            gbg_ref[...] = jnp.zeros_like(gbg_ref)
            gwgu_ref[...] = jnp.zeros_like(gwgu_ref)

        cos = cos_ref[...]
        sa = sa_ref[...]
        sb = sb_ref[...]
        cos4, sa4, sb4 = (jnp.concatenate([v] * 4, axis=1) for v in (cos, sa, sb))

        qa = qa_ref[...]
        ka = ka_ref[...]
        va = va_ref[...]
        dattn = dattn_ref[...]
        gsink_rows = [jnp.zeros((1, 1), F32) for _ in range(Q_HEADS)]
        for b in reversed(range(nblk)):
            rows = slice(b * BLOCK, (b + 1) * BLOCK)
            has_prev = (t * nblk + b) > 0
            mask = _attn_mask(has_prev)
            k_cur = ka[rows]
            v_cur = va[rows]
            k_old = kap_ref[...] if b == 0 else ka[(b - 1) * BLOCK:b * BLOCK]
            v_old = vap_ref[...] if b == 0 else va[(b - 1) * BLOCK:b * BLOCK]
            dq_parts, dk_parts, dv_parts = [], [], []
            for j in range(KV_HEADS):
                hs = slice(j * HEAD_A, (j + 1) * HEAD_A)
                kc = jnp.concatenate([k_old[:, hs], k_cur[:, hs]], axis=0)
                vc = jnp.concatenate([v_old[:, hs], v_cur[:, hs]], axis=0)
                qs = _stack_heads(qa[rows], j)
                do_s = _stack_heads(dattn[rows], j)
                probs, p_sink = _softmax_block(qs, kc, mask, _sink_col(sinks_ref, j))
                dp = _mm_nt(do_s, vc)
                d_row = jnp.sum(probs * dp, axis=1, keepdims=True)
                ds16 = (probs * (dp - d_row) * ATT_SCALE).astype(BF16)
                dq_parts.append(_mm(ds16, kc))
                dk_parts.append(_mm_tn(ds16, qs))
                dv_parts.append(_mm_tn(probs.astype(BF16), do_s))
                t_sink = d_row * p_sink
                for g in range(GROUP):
                    gsink_rows[GROUP * j + g] = gsink_rows[GROUP * j + g] - jnp.sum(
                        t_sink[g * BLOCK:(g + 1) * BLOCK], axis=0, keepdims=True)
            dq = _rope_bwd(_unstack_heads(dq_parts), cos4[rows], sa4[rows], sb4[rows])
            dproj[rows, O_QA:O_QA + W_QA] = dq.astype(BF16)
            dk_cur = dk_carry[...] + jnp.concatenate([p[BLOCK:] for p in dk_parts], axis=1)
            dv_cur = dv_carry[...] + jnp.concatenate([p[BLOCK:] for p in dv_parts], axis=1)
            dproj[rows, O_KA:O_KA + W_KA] = _rope_bwd(dk_cur, cos[rows], sa[rows], sb[rows]).astype(BF16)
            dproj[rows, O_VA:O_VA + W_VA] = dv_cur.astype(BF16)
            dk_carry[...] = jnp.concatenate([p[:BLOCK] for p in dk_parts], axis=1)
            dv_carry[...] = jnp.concatenate([p[:BLOCK] for p in dv_parts], axis=1)
        for hq in range(Q_HEADS):
            gsink_ref[hq:hq + 1, :] += jnp.broadcast_to(gsink_rows[hq], (1, 128))

        dproj[:, O_GA:O_GA + W_GA] = dga_ref[...]
        dproj[:, O_GB:O_GB + W_GB] = dgb_ref[...]

        r16 = r_ref[...].astype(BF16)
        wgu16 = wgu_ref[...].astype(BF16)
        z = _mm(r16, wgu16) + bg_ref[...]
        log_a = _log_sigmoid(z) / GLA_TAU
        bcum = _tri_mm(_chunk_tri(TRI_SLAB, False), log_a)
        qb = qb_ref[...]
        kb = kb_ref[...]
        vb = vb_ref[...]
        dob = dob_ref[...]
        e_b = jnp.exp(bcum)
        e_nb = jnp.exp(-bcum)
        qd_f = qb * GLA_SCALE * e_b
        ki_f = kb * e_nb
        qd_all = qd_f.astype(BF16)
        ki_all = ki_f.astype(BF16)
        tril = lax.broadcasted_iota(jnp.int32, (CHUNK, CHUNK), 0) >= lax.broadcasted_iota(jnp.int32, (CHUNK, CHUNK), 1)
        last_row = lax.broadcasted_iota(jnp.int32, (CHUNK, 1), 0) == CHUNK - 1
        for c in reversed(range(nch)):
            rows = slice(c * CHUNK, (c + 1) * CHUNK)
            b_c = bcum[rows]
            b_last = b_c[CHUNK - 1:CHUNK]
            e_e = jnp.exp(b_last - b_c)
            dec = jnp.exp(b_last)
            ke_f = kb[rows] * e_e
            ke = ke_f.astype(BF16)
            sp = st_ref[c]
            sp16 = sp.astype(BF16)
            dsn = ds_carry[...]
            dsn16 = dsn.astype(BF16)
            dqd_p, dki_p, dke_p, dv_p, dsp_p = [], [], [], [], []
            for h in range(GLA_HEADS):
                ks = slice(h * GLA_DK, (h + 1) * GLA_DK)
                vs = slice(h * GLA_DV, (h + 1) * GLA_DV)
                qd = qd_all[rows, ks]
                ki = ki_all[rows, ks]
                v_h = vb[rows, vs]
                do_h = dob[rows, vs]
                a16 = jnp.where(tril, _mm_nt(qd, ki), 0.0).astype(BF16)
                da16 = jnp.where(tril, _mm_nt(do_h, v_h), 0.0).astype(BF16)
                dv_p.append(_mm_tn(a16, do_h) + _mm_nt(ke[:, ks], dsn16[:, ks]))
                dqd_p.append(_mm(da16, ki) + _mm(do_h, sp16[:, ks]))
                dki_p.append(_mm_tn(da16, qd))
                dke_p.append(_mm(v_h, dsn16[:, ks]))
                dsp_p.append(_mm_tn(do_h, qd))
            dqd = jnp.concatenate(dqd_p, axis=1)
            dki = jnp.concatenate(dki_p, axis=1)
            dke = jnp.concatenate(dke_p, axis=1)
            ddec = jnp.sum(dsn * sp, axis=0, keepdims=True)
            ds_carry[...] = dsn * dec + jnp.concatenate(dsp_p, axis=1)
            dproj[rows, O_QB:O_QB + W_QB] = (dqd * e_b[rows] * GLA_SCALE).astype(BF16)
            dproj[rows, O_KB:O_KB + W_KB] = (dki * e_nb[rows] + dke * e_e).astype(BF16)
            dproj[rows, O_VB:O_VB + W_VB] = jnp.concatenate(dv_p, axis=1).astype(BF16)
            dke_ke = dke * ke_f
            d_b = dqd * qd_f[rows] - dki * ki_f[rows] - dke_ke
            d_bl = jnp.sum(dke_ke, axis=0, keepdims=True) + ddec * dec
            db_s[rows, :] = d_b + jnp.where(last_row, d_bl, 0.0)
        dlog_a = _tri_mm(_chunk_tri(TRI_SLAB, True), db_s[...])
        dz = dlog_a * (1.0 / GLA_TAU) * _sigmoid(-z)
        dz16 = dz.astype(BF16)
        gbg_ref[...] += jnp.sum(dz, axis=0, keepdims=True)
        gwgu_ref[...] += _mm_tn(r16, dz16)
        dproj[:, O_R:O_R + W_R] = _mm_nt(dz16, wgu16).astype(BF16)

        dp16 = dproj[...]
        gx_ref[...] = ALPHA * dh_ref[...] + _mm(dp16, win_ref[...])
        x16 = x_ref[...].astype(BF16)
        for h in range(2):
            dwin_ref[h, 0:D_PROJ, :] += _mm_tn(dp16, x16[:, h * HALF:(h + 1) * HALF])

    tile = lambda w: pl.BlockSpec((TM, w), lambda i: (nt - 1 - i, 0))
    whole = lambda shape: pl.BlockSpec(shape, lambda i: tuple(0 for _ in shape), pipeline_mode=pl.Buffered(1))
    prev_blk = pl.BlockSpec((BLOCK, W_KA), lambda i: (jnp.maximum((nt - 1 - i) * nblk - 1, 0), 0))
    in_specs = [
        tile(D_MODEL), tile(D_MODEL), tile(W_QA), tile(W_KA), tile(W_VA), prev_blk, prev_blk,
        tile(W_GA), tile(W_GA), tile(W_GB), tile(W_GB), tile(W_QB), tile(W_KB), tile(W_VB), tile(W_R),
        pl.BlockSpec((nch, GLA_DV, GLA_HEADS * GLA_DK), lambda i: (nt - 1 - i, 0, 0)),
        tile(128), tile(128), tile(128),
        whole((D_PROJ, D_MODEL)), whole((W_R, W_KB)), whole((1, W_KB)), pl.BlockSpec(memory_space=pltpu.SMEM),
    ]
    out_shape = (
        jax.ShapeDtypeStruct((s_len, D_MODEL), F32),
        jax.ShapeDtypeStruct((2, ACC_ROWS, HALF), F32),
        jax.ShapeDtypeStruct((Q_HEADS, 128), F32),
        jax.ShapeDtypeStruct((1, W_KB), F32),
        jax.ShapeDtypeStruct((W_R, W_KB), F32),
    )
    out_specs = (tile(D_MODEL), whole((2, ACC_ROWS, HALF)), whole((Q_HEADS, 128)), whole((1, W_KB)),
                 whole((W_R, W_KB)))
    scratch = [
        pltpu.VMEM((TM, D_PROJ), BF16), pltpu.VMEM((BLOCK, W_KA), F32), pltpu.VMEM((BLOCK, W_VA), F32),
        pltpu.VMEM((GLA_DV, GLA_HEADS * GLA_DK), F32), pltpu.VMEM((TM, W_KB), F32),
    ]
    return pl.pallas_call(
        body, name="bwd_mix", grid=(nt,), in_specs=in_specs, out_specs=out_specs, out_shape=out_shape,
        scratch_shapes=scratch,
        compiler_params=pltpu.CompilerParams(dimension_semantics=("arbitrary",), vmem_limit_bytes=VMEM_LIMIT),
    )(x, dh, qa, ka, va, ka, va, dattn, dga, dob, dgb, qb, kb, vb, r, st, cos, sa, sb, win, wgu, bg, sinks)


def _mesh_place():
    x, y, c = lax.axis_index("x"), lax.axis_index("y"), lax.axis_index("c")
    chips = [(1 - x, y), (x, 1 - y), (1 - x, 1 - y)]
    return x, y, c, chips


def _gather_weights_call(w_lin, w_out, wgu):
    def body(wlin_ref, wout_ref, wgu_ref, wt_ref, wout_full, wgu_all, blk, oblk, asm, send_sems, recv_sems):
        x, y, c, chips = _mesh_place()
        k_me = 2 * x + y
        asm[SHARD_IN - 4:SHARD_PAD, :] = jnp.zeros((SHARD_PAD - SHARD_IN + 4, D_MODEL), F32)
        for a in range(8):
            asm[0:SHARD_IN, a * 128:(a + 1) * 128] = wlin_ref[pl.ds(a, SHARD_IN, stride=8), :]
        for h in range(2):
            blk[k_me, h] = asm[0:SHARD_PAD, h * HALF:(h + 1) * HALF].astype(BF16)
            oblk[k_me, h] = wout_ref[:, h * HALF:(h + 1) * HALF].astype(BF16)
        wgu_all[k_me] = wgu_ref[...]

        def blocks(k, hc):
            return (blk.at[k, hc], oblk.at[k, hc])

        def copies(k, hc, sem0, to):
            return [pltpu.make_async_remote_copy(src_ref=blk, dst_ref=blk, send_sem=send_sems.at[sem0 + n],
                                                 recv_sem=recv_sems.at[sem0 + n], device_id=to, device_id_type=MESH)
                    for n, blk in enumerate(blocks(k, hc))]

        def gu_copy(k, r, to):
            return pltpu.make_async_remote_copy(src_ref=wgu_all.at[k], dst_ref=wgu_all.at[k], send_sem=send_sems.at[12 + r],
                                                recv_sem=recv_sems.at[12 + r], device_id=to, device_id_type=MESH)

        started = []
        for r, chip in enumerate(chips):
            started += copies(k_me, c, 2 * r, (*chip, c))
            started.append(gu_copy(k_me, r, (*chip, c)))
        for cp in started:
            cp.start()
        for r, chip in enumerate(chips):
            k_r = 2 * chip[0] + chip[1]
            for cp in copies(k_r, c, 2 * r, (x, y, c)):
                cp.wait_recv()
            passed = copies(k_r, c, 6 + 2 * r, (x, y, 1 - c))
            for cp in passed:
                cp.start()
            started += passed
        for r, chip in enumerate(chips):
            k_r = 2 * chip[0] + chip[1]
            for cp in copies(k_r, 1 - c, 6 + 2 * r, (x, y, c)):
                cp.wait_recv()
            gu_copy(k_r, r, (x, y, c)).wait_recv()
        for cp in started:
            cp.wait_send()

        for k in range(N_CHIPS):
            for h in range(2):
                asm[k * SHARD_IN:k * SHARD_IN + SHARD_PAD, h * HALF:(h + 1) * HALF] = blk[k, h].astype(F32)
                wout_full[k * SHARD_OUT:(k + 1) * SHARD_OUT, h * HALF:(h + 1) * HALF] = oblk[k, h]
        wt_ref[...] = asm[0:D_PROJ, :].astype(BF16)

    vmem = pl.BlockSpec(memory_space=pltpu.VMEM)
    return pl.pallas_call(
        body, name="gather_weights",
        out_shape=(jax.ShapeDtypeStruct((D_PROJ, D_MODEL), BF16),
                   jax.ShapeDtypeStruct((D_MODEL, D_MODEL), BF16),
                   jax.ShapeDtypeStruct((N_CHIPS, W_R, W_KB // N_CHIPS), F32)),
        in_specs=[vmem, vmem, vmem], out_specs=(vmem, vmem, vmem),
        scratch_shapes=[pltpu.VMEM((N_CHIPS, 2, SHARD_PAD, HALF), BF16), pltpu.VMEM((N_CHIPS, 2, SHARD_OUT, HALF), BF16),
                        pltpu.VMEM((ACC_ROWS, D_MODEL), F32),
                        pltpu.SemaphoreType.DMA((15,)), pltpu.SemaphoreType.DMA((15,))],
        compiler_params=pltpu.CompilerParams(vmem_limit_bytes=VMEM_LIMIT),
    )(w_lin, w_out, wgu)


def _adamw(w, g, m, v):
    m = ADAM_B1 * m + (1.0 - ADAM_B1) * g
    v = ADAM_B2 * v + (1.0 - ADAM_B2) * (g * g)
    m_hat = m / (1.0 - ADAM_B1 ** ADAM_STEP)
    v_hat = v / (1.0 - ADAM_B2 ** ADAM_STEP)
    delta = -ADAM_LR * (m_hat / (jnp.sqrt(v_hat) + ADAM_EPS) + ADAM_WD * w)
    return delta, m, v


def _reduce_grads_call(g_in, g_out, pack, w_small, m_small, v_small):
    def body(gin_hbm, gout_hbm, pack_ref, ws_ref, ms_ref, vs_ref,
             lin_in, fin_out, gs_ref, ds_ref, nms_ref, nvs_ref,
             a_in, a_out, b_in, b_out, c_in, s_in, s_out, r_in, r_out, f_in, f_out, pack_all,
             send_sems, recv_sems, local_sems):
        x, y, c, chips = _mesh_place()
        k_me = 2 * x + y
        me = 4 * x + 2 * y + c
        sibling = (x, y, 1 - c)

        pack_all[me] = pack_ref[...]
        small = []
        for mask in range(1, 8):
            peer = (x ^ (mask >> 2), y ^ ((mask >> 1) & 1), c ^ (mask & 1))
            small.append(pltpu.make_async_remote_copy(
                src_ref=pack_ref, dst_ref=pack_all.at[me], send_sem=send_sems.at[mask], recv_sem=recv_sems.at[mask],
                device_id=peer, device_id_type=MESH))
        for cp in small:
            cp.start()

        mine = [pltpu.make_async_copy(gin_hbm.at[c], a_in, local_sems.at[0]),
                pltpu.make_async_copy(gout_hbm.at[c], a_out, local_sems.at[1])]
        to_sib = [pltpu.make_async_remote_copy(
                      src_ref=gin_hbm.at[1 - c], dst_ref=b_in,
                      send_sem=send_sems.at[8], recv_sem=recv_sems.at[8], device_id=sibling, device_id_type=MESH),
                  pltpu.make_async_remote_copy(
                      src_ref=gout_hbm.at[1 - c], dst_ref=b_out,
                      send_sem=send_sems.at[9], recv_sem=recv_sems.at[9], device_id=sibling, device_id_type=MESH)]
        for cp in mine + to_sib:
            cp.start()
        for cp in mine:
            cp.wait()
        for cp in to_sib:
            cp.wait_recv()
        for k in range(N_CHIPS):
            rows = slice(k * SHARD_IN, k * SHARD_IN + SHARD_PAD)
            c_in[k] = a_in[rows, :] + b_in[rows, :]

        sent = []
        for r, chip in enumerate(chips):
            k_r = 2 * chip[0] + chip[1]
            s_in[r] = c_in[k_r].astype(BF16)
            s_out[r] = (a_out[k_r] + b_out[k_r]).astype(BF16)
            sent.append(pltpu.make_async_remote_copy(
                src_ref=s_in.at[r], dst_ref=r_in.at[r], send_sem=send_sems.at[10 + 2 * r],
                recv_sem=recv_sems.at[10 + 2 * r], device_id=(*chip, c), device_id_type=MESH))
            sent.append(pltpu.make_async_remote_copy(
                src_ref=s_out.at[r], dst_ref=r_out.at[r], send_sem=send_sems.at[11 + 2 * r],
                recv_sem=recv_sems.at[11 + 2 * r], device_id=(*chip, c), device_id_type=MESH))
            sent[-2].start()
            sent[-1].start()
        own_out = a_out[k_me] + b_out[k_me]
        for cp in sent:
            cp.wait_recv()
        f_in[c] = c_in[k_me] + r_in[0].astype(F32) + r_in[1].astype(F32) + r_in[2].astype(F32)
        f_out[c] = own_out + r_out[0].astype(F32) + r_out[1].astype(F32) + r_out[2].astype(F32)

        swap = [pltpu.make_async_remote_copy(
                    src_ref=f_in.at[c], dst_ref=f_in.at[c],
                    send_sem=send_sems.at[16], recv_sem=recv_sems.at[16], device_id=sibling, device_id_type=MESH),
                pltpu.make_async_remote_copy(
                    src_ref=f_out.at[c], dst_ref=f_out.at[c],
                    send_sem=send_sems.at[17], recv_sem=recv_sems.at[17], device_id=sibling, device_id_type=MESH)]
        for cp in swap:
            cp.start()

        for cp in small:
            cp.wait_recv()
        total = pack_all[0]
        for d in range(1, 8):
            total = total + pack_all[d]
        gu_rows = pl.ds(pl.multiple_of(P_GU + 16 * k_me, 8), 16)
        pack_all[0] = total
        g_small = jnp.concatenate([total[:P_GU], pack_all[0, gu_rows, :]], axis=0)
        delta, new_m, new_v = _adamw(ws_ref[...], g_small, ms_ref[...], vs_ref[...])
        gs_ref[...] = g_small
        ds_ref[...] = delta
        nms_ref[...] = new_m
        nvs_ref[...] = new_v

        other_in = pltpu.make_async_remote_copy(
            src_ref=f_in.at[1 - c], dst_ref=f_in.at[1 - c],
            send_sem=send_sems.at[16], recv_sem=recv_sems.at[16], device_id=sibling, device_id_type=MESH)
        other_out = pltpu.make_async_remote_copy(
            src_ref=f_out.at[1 - c], dst_ref=f_out.at[1 - c],
            send_sem=send_sems.at[17], recv_sem=recv_sems.at[17], device_id=sibling, device_id_type=MESH)
        other_in.wait_recv()
        other_out.wait_recv()
        for cp in small + to_sib + sent + swap:
            cp.wait_send()

        for a in range(8):
            lin_in[pl.ds(a, SHARD_IN, stride=8), :] = f_in[a // 4, 0:SHARD_IN, (a % 4) * 128:(a % 4 + 1) * 128]
        for h in range(2):
            fin_out[:, h * HALF:(h + 1) * HALF] = f_out[h]

    vmem = pl.BlockSpec(memory_space=pltpu.VMEM)
    hbm = pl.BlockSpec(memory_space=pl.ANY)
    small_shape = jax.ShapeDtypeStruct((PACK2_ROWS, 128), F32)
    return pl.pallas_call(
        body, name="reduce_grads",
        out_shape=(jax.ShapeDtypeStruct((LIN_ROWS, 128), F32), jax.ShapeDtypeStruct((SHARD_OUT, D_MODEL), F32),
                   small_shape, small_shape, small_shape, small_shape),
        in_specs=[hbm, hbm, vmem, vmem, vmem, vmem], out_specs=(vmem,) * 6,
        scratch_shapes=[
            pltpu.VMEM((ACC_ROWS, HALF), F32), pltpu.VMEM((N_CHIPS, SHARD_OUT, HALF), F32),
            pltpu.VMEM((ACC_ROWS, HALF), F32), pltpu.VMEM((N_CHIPS, SHARD_OUT, HALF), F32),
            pltpu.VMEM((N_CHIPS, SHARD_PAD, HALF), F32),
            pltpu.VMEM((3, SHARD_PAD, HALF), BF16), pltpu.VMEM((3, SHARD_OUT, HALF), BF16),
            pltpu.VMEM((3, SHARD_PAD, HALF), BF16), pltpu.VMEM((3, SHARD_OUT, HALF), BF16),
            pltpu.VMEM((2, SHARD_PAD, HALF), F32), pltpu.VMEM((2, SHARD_OUT, HALF), F32),
            pltpu.VMEM((8, PACK_ROWS, 128), F32),
            pltpu.SemaphoreType.DMA((18,)), pltpu.SemaphoreType.DMA((18,)), pltpu.SemaphoreType.DMA((2,)),
        ],
        compiler_params=pltpu.CompilerParams(vmem_limit_bytes=VMEM_LIMIT),
    )(g_in, g_out, pack, w_small, m_small, v_small)


def _adamw_call(g_in, w_in, m_in, v_in, g_out, w_out, m_out, v_out):
    steps = 4
    rows_in = LIN_ROWS // steps
    rows_out = SHARD_OUT // steps

    def body(gi, wi, mi, vi, go, wo, mo, vo, di, nmi, nvi, do, nmo, nvo):
        di[...], nmi[...], nvi[...] = _adamw(wi[...], gi[...], mi[...], vi[...])
        do[...], nmo[...], nvo[...] = _adamw(wo[...], go[...], mo[...], vo[...])

    t_in = pl.BlockSpec((rows_in, 128), lambda i: (i, 0))
    t_out = pl.BlockSpec((rows_out, D_MODEL), lambda i: (i, 0))
    s_in = jax.ShapeDtypeStruct((LIN_ROWS, 128), F32)
    s_out = jax.ShapeDtypeStruct((SHARD_OUT, D_MODEL), F32)
    return pl.pallas_call(
        body, name="adamw", grid=(steps,), in_specs=[t_in] * 4 + [t_out] * 4, out_specs=(t_in,) * 3 + (t_out,) * 3,
        out_shape=(s_in,) * 3 + (s_out,) * 3,
        compiler_params=pltpu.CompilerParams(dimension_semantics=("arbitrary",)),
    )(g_in, w_in, m_in, v_in, g_out, w_out, m_out, v_out)


def _rope_tables(positions):
    half = 8
    inv_freq = 500000.0 ** (-jnp.arange(half, dtype=F32) / half)
    d = np.arange(128) % HEAD_A
    rotated = d < 2 * half
    freq = jnp.where(rotated, inv_freq[d % half], 0.0)
    ang = positions.astype(F32)[:, None] * freq[None, :]
    sin = jnp.sin(ang)
    sa = sin * np.where(d < half, -1.0, 0.0).astype(np.float32)
    sb = sin * np.where(rotated & (d >= half), 1.0, 0.0).astype(np.float32)
    return jnp.cos(ang), sa, sb


def _pad_lanes(a):
    return jnp.pad(a, ((0, 0), (0, 128 - a.shape[1])))


def _pack_small(ln_g, ln_b, b_gate, norm_w, sinks, gate_up_rows, extra_rows):
    return jnp.concatenate([
        ln_g.reshape(8, 128), ln_b.reshape(8, 128), b_gate.reshape(2, 128), norm_w.reshape(1, 128),
        _pad_lanes(sinks.reshape(1, Q_HEADS)), extra_rows, gate_up_rows], axis=0)


def kernel(x, positions, w_in, gla_w_gate_up, gla_b_gate, attn_sinks, gla_norm_w, w_out, ln_g, ln_b, loss_target, m_w_in, m_gla_w_gate_up, m_gla_b_gate, m_attn_sinks, m_gla_norm_w, m_w_out, m_ln_g, m_ln_b, v_w_in, v_gla_w_gate_up, v_gla_b_gate, v_attn_sinks, v_gla_norm_w, v_w_out, v_ln_g, v_ln_b):
    def lin(w):
        return jnp.transpose(w[0]).reshape(LIN_ROWS, 128)

    def unlin(w):
        return jnp.transpose(w.reshape(SHARD_IN, D_MODEL))[None]

    win, wout, wgu_all = _gather_weights_call(lin(w_in), w_out[0], gla_w_gate_up[0])
    wgu = jnp.transpose(wgu_all, (1, 0, 2)).reshape(W_R, W_KB)
    cos, sa, sb = _rope_tables(positions[0])
    sinks = attn_sinks[0]

    (qa, ka, va, qb, kb, vb, r, dattn, dga, dob, dgb, dh, st, g_wout, g_lng, g_lnb, g_nw, loss) = _fwd_call(
        x[0], loss_target[0], cos, sa, sb, win, wout, wgu, gla_b_gate, sinks, gla_norm_w, ln_g, ln_b)
    gx, g_win, g_sink, g_bg, g_wgu = _bwd_call(
        x[0], dh, qa, ka, va, dattn, dga, dob, dgb, qb, kb, vb, r, st, cos, sa, sb, win, wgu, gla_b_gate, sinks)

    g_wout_by_chip = g_wout.reshape(2, N_CHIPS, SHARD_OUT, HALF)
    gu_rows = _pad_lanes(jnp.transpose(g_wgu.reshape(W_R, N_CHIPS, W_KB // N_CHIPS), (1, 0, 2)).reshape(N_CHIPS * W_R, -1))
    loss_rows = jnp.concatenate([loss, jnp.zeros((3, 128), F32)], axis=0)
    pack = _pack_small(g_lng, g_lnb, g_bg, g_nw, g_sink[:, 0], gu_rows, loss_rows)

    def small(ln_g_, ln_b_, b_gate_, norm_w_, sinks_, gate_up_):
        return _pack_small(ln_g_, ln_b_, b_gate_, norm_w_, sinks_[0], _pad_lanes(gate_up_[0]), jnp.zeros((4, 128), F32))

    w_small = small(ln_g, ln_b, gla_b_gate, gla_norm_w, attn_sinks, gla_w_gate_up)
    m_small = small(m_ln_g, m_ln_b, m_gla_b_gate, m_gla_norm_w, m_attn_sinks, m_gla_w_gate_up)
    v_small = small(v_ln_g, v_ln_b, v_gla_b_gate, v_gla_norm_w, v_attn_sinks, v_gla_w_gate_up)

    fin_in, fin_out, g_s, d_s, nm_s, nv_s = _reduce_grads_call(g_win, g_wout_by_chip, pack, w_small, m_small, v_small)
    d_in, nm_in, nv_in, d_out, nm_out, nv_out = _adamw_call(
        fin_in, lin(w_in), lin(m_w_in), lin(v_w_in), fin_out, w_out[0], m_w_out[0], v_w_out[0])

    def unpack(p, big_in, big_out):
        return (unlin(big_in), p[P_GU:P_GU + W_R, :W_KB // N_CHIPS][None], p[P_BG:P_BG + 2].reshape(1, W_KB),
                p[P_SINK:P_SINK + 1, :Q_HEADS], p[P_NW:P_NW + 1], big_out[None],
                p[P_LNG:P_LNG + 8].reshape(1, D_MODEL), p[P_LNB:P_LNB + 8].reshape(1, D_MODEL))

    loss_total = g_s[P_LOSS, 0]
    return (loss_total, gx[None], *unpack(g_s, fin_in, fin_out), *unpack(d_s, d_in, d_out),
            *unpack(nm_s, nm_in, nm_out), *unpack(nv_s, nv_in, nv_out))
```

```python
import functools

import jax
import jax.numpy as jnp
import numpy as np
from jax import lax
from jax.experimental import pallas as pl
from jax.experimental.pallas import tpu as pltpu

F32 = jnp.float32
BF16 = jnp.bfloat16
MESH = pl.DeviceIdType.MESH

D_MODEL = 1024
N_CHIPS = 4
W_QA, W_KA, W_VA, W_GA, W_QB, W_KB, W_VB, W_GB, W_R = 512, 128, 128, 512, 256, 256, 512, 512, 16
O_QA = 0
O_KA = O_QA + W_QA
O_VA = O_KA + W_KA
O_GA = O_VA + W_VA
O_QB = O_GA + W_GA
O_KB = O_QB + W_QB
O_VB = O_KB + W_KB
O_GB = O_VB + W_VB
O_R = O_GB + W_GB
D_PROJ = O_R + W_R
SHARD_IN = D_PROJ // N_CHIPS
SHARD_OUT = D_MODEL // N_CHIPS
SHARD_PAD = 720
ACC_ROWS = -(-((N_CHIPS - 1) * SHARD_IN + SHARD_PAD) // 8) * 8
HALF = D_MODEL // 2

HEAD_A = 64
Q_HEADS = 8
KV_HEADS = 2
GROUP = 4
BLOCK = 128
GLA_HEADS = 4
GLA_DK = 64
GLA_DV = 128
CHUNK = 64
GLA_TAU = 16.0
EPS = 1e-5
ALPHA = 2.0 ** 0.25
ATT_SCALE = HEAD_A ** -0.5
GLA_SCALE = GLA_DK ** -0.5

ADAM_LR = 0.001
ADAM_B1 = 0.9
ADAM_B2 = 0.999
ADAM_EPS = 1e-08
ADAM_WD = 0.01
ADAM_STEP = 10

TM = 256
TRI_SLAB = 128
VMEM_LIMIT = 56 * 1024 * 1024

P_LNG, P_LNB, P_BG, P_NW, P_SINK, P_LOSS, P_GU = 0, 8, 16, 18, 19, 20, 24
PACK_ROWS = P_GU + N_CHIPS * 16


def _mm(a, b):
    return jnp.dot(a, b, preferred_element_type=F32)


def _mm_nt(a, b):
    return lax.dot_general(a, b, (((1,), (1,)), ((), ())), preferred_element_type=F32)


def _mm_tn(a, b):
    return lax.dot_general(a, b, (((0,), (0,)), ((), ())), preferred_element_type=F32)


def _split3(a):
    hi = a.astype(BF16)
    r1 = a - hi.astype(F32)
    mid = r1.astype(BF16)
    lo = (r1 - mid.astype(F32)).astype(BF16)
    return hi, mid, lo


def _tri_mm(tri, a):
    slab = tri.shape[0]
    hi, mid, lo = _split3(a)
    return jnp.concatenate(
        [_mm(tri, hi[s:s + slab]) + _mm(tri, mid[s:s + slab]) + _mm(tri, lo[s:s + slab])
         for s in range(0, a.shape[0], slab)], axis=0)


def _chunk_tri(n, upper):
    r = lax.broadcasted_iota(jnp.int32, (n, n), 0)
    c = lax.broadcasted_iota(jnp.int32, (n, n), 1)
    same = (r >> 6) == (c >> 6)
    order = (c >= r) if upper else (c <= r)
    return jnp.where(same & order, 1.0, 0.0).astype(BF16)


def _rope(t, cos, sa, sb):
    w = t.shape[1]
    return t * cos + pltpu.roll(t, w - 8, 1) * sa + pltpu.roll(t, 8, 1) * sb


def _rope_bwd(d, cos, sa, sb):
    w = d.shape[1]
    return d * cos + pltpu.roll(d * sa, 8, 1) + pltpu.roll(d * sb, w - 8, 1)


def _log_sigmoid(z):
    return jnp.minimum(z, 0.0) - jnp.log1p(jnp.exp(-jnp.abs(z)))


def _sigmoid(z):
    return 1.0 / (1.0 + jnp.exp(-z))


def _attn_mask(has_prev):
    r = lax.broadcasted_iota(jnp.int32, (GROUP * BLOCK, 2 * BLOCK), 0) & (BLOCK - 1)
    k = lax.broadcasted_iota(jnp.int32, (GROUP * BLOCK, 2 * BLOCK), 1)
    first_key = jnp.where(has_prev, 0, BLOCK)
    return (k > r) & (k <= r + BLOCK) & (k >= first_key)


def _sink_col(sinks_ref, j):
    r = lax.broadcasted_iota(jnp.int32, (GROUP * BLOCK, 1), 0) >> 7
    col = jnp.full((GROUP * BLOCK, 1), sinks_ref[GROUP * j], F32)
    for g in range(1, GROUP):
        col = jnp.where(r == g, sinks_ref[GROUP * j + g], col)
    return col


def _stack_heads(t, j):
    return jnp.concatenate([t[:, (GROUP * j + g) * HEAD_A:(GROUP * j + g + 1) * HEAD_A] for g in range(GROUP)], axis=0)


def _unstack_heads(parts):
    return jnp.concatenate([parts[j][g * BLOCK:(g + 1) * BLOCK] for j in range(KV_HEADS) for g in range(GROUP)], axis=1)


def _softmax_block(qs, kc, mask, sink):
    s = _mm_nt(qs, kc) * ATT_SCALE
    s = jnp.where(mask, s, -jnp.inf)
    m = jnp.maximum(jnp.max(s, axis=1, keepdims=True), sink)
    p = jnp.exp(s - m)
    e_sink = jnp.exp(sink - m)
    inv = 1.0 / (jnp.sum(p, axis=1, keepdims=True) + e_sink)
    return p * inv, e_sink * inv


def _fwd_call(x, tgt, cos, sa, sb, win, wout, wgu, bg, sinks, nw, lng, lnb):
    s_len = x.shape[0]
    nt = s_len // TM
    nblk = TM // BLOCK
    nch = TM // CHUNK

    def body(x_ref, t_ref, cos_ref, sa_ref, sb_ref, win_ref, wout_ref, wgu_ref, bg_ref, sinks_ref, nw_ref,
             lng_ref, lnb_ref,
             qa_ref, ka_ref, va_ref, qb_ref, kb_ref, vb_ref, r_ref, dattn_ref, dga_ref, dob_ref, dgb_ref, dh_ref,
             st_ref, dwout_ref, glng_ref, glnb_ref, gnw_ref, loss_ref,
             kprev, vprev, state, attn_s, ga_s, ob_s, gb_s, cat_s):
        i = pl.program_id(0)

        @pl.when(i == 0)
        def _():
            kprev[...] = jnp.zeros_like(kprev)
            vprev[...] = jnp.zeros_like(vprev)
            state[...] = jnp.zeros_like(state)
            dwout_ref[...] = jnp.zeros_like(dwout_ref)
            glng_ref[...] = jnp.zeros_like(glng_ref)
            glnb_ref[...] = jnp.zeros_like(glnb_ref)
            gnw_ref[...] = jnp.zeros_like(gnw_ref)
            loss_ref[...] = jnp.zeros_like(loss_ref)

        x = x_ref[...]
        xb = x.astype(BF16)

        def proj(off, width):
            return _mm_nt(xb, win_ref[off:off + width, :])

        cos = cos_ref[...]
        sa = sa_ref[...]
        sb = sb_ref[...]
        cos4, sa4, sb4 = (jnp.concatenate([t] * 4, axis=1) for t in (cos, sa, sb))
        qa = _rope(proj(O_QA, W_QA), cos4, sa4, sb4).astype(BF16)
        ka = _rope(proj(O_KA, W_KA), cos, sa, sb).astype(BF16)
        va = proj(O_VA, W_VA).astype(BF16)
        qa_ref[...] = qa
        ka_ref[...] = ka
        va_ref[...] = va
        ga_s[...] = proj(O_GA, W_GA)
        gb_s[...] = proj(O_GB, W_GB)

        for b in range(nblk):
            rows = slice(b * BLOCK, (b + 1) * BLOCK)
            has_prev = (i * nblk + b) > 0
            mask = _attn_mask(has_prev)
            k_cur = ka[rows]
            v_cur = va[rows]
            k_old = kprev[...] if b == 0 else ka[(b - 1) * BLOCK:b * BLOCK]
            v_old = vprev[...] if b == 0 else va[(b - 1) * BLOCK:b * BLOCK]
            outs = []
            for j in range(KV_HEADS):
                hs = slice(j * HEAD_A, (j + 1) * HEAD_A)
                kc = jnp.concatenate([k_old[:, hs], k_cur[:, hs]], axis=0)
                vc = jnp.concatenate([v_old[:, hs], v_cur[:, hs]], axis=0)
                probs, _ = _softmax_block(_stack_heads(qa[rows], j), kc, mask, _sink_col(sinks_ref, j))
                outs.append(_mm(probs.astype(BF16), vc))
            attn_s[rows, :] = _unstack_heads(outs)
        kprev[...] = ka[(nblk - 1) * BLOCK:]
        vprev[...] = va[(nblk - 1) * BLOCK:]

        r = proj(O_R, W_R)
        r_ref[...] = r
        z = _mm(r.astype(BF16), wgu_ref[...].astype(BF16)) + bg_ref[...]
        log_a = _log_sigmoid(z) / GLA_TAU
        bcum = _tri_mm(_chunk_tri(TRI_SLAB, False), log_a)
        qb = proj(O_QB, W_QB)
        kb = proj(O_KB, W_KB)
        vb = proj(O_VB, W_VB).astype(BF16)
        qb_ref[...] = qb
        kb_ref[...] = kb
        vb_ref[...] = vb
        qd_all = (qb * GLA_SCALE * jnp.exp(bcum)).astype(BF16)
        ki_all = (kb * jnp.exp(-bcum)).astype(BF16)
        tril = lax.broadcasted_iota(jnp.int32, (CHUNK, CHUNK), 0) >= lax.broadcasted_iota(jnp.int32, (CHUNK, CHUNK), 1)
        for c in range(nch):
            rows = slice(c * CHUNK, (c + 1) * CHUNK)
            b_c = bcum[rows]
            b_last = b_c[CHUNK - 1:CHUNK]
            ke = (kb[rows] * jnp.exp(b_last - b_c)).astype(BF16)
            st = state[...]
            st_ref[c] = st
            st16 = st.astype(BF16)
            o_parts, u_parts = [], []
            for h in range(GLA_HEADS):
                ks = slice(h * GLA_DK, (h + 1) * GLA_DK)
                vs = slice(h * GLA_DV, (h + 1) * GLA_DV)
                qd = qd_all[rows, ks]
                v_h = vb[rows, vs]
                a = jnp.where(tril, _mm_nt(qd, ki_all[rows, ks]), 0.0)
                o_parts.append(_mm(a.astype(BF16), v_h) + _mm_nt(qd, st16[:, ks]))
                u_parts.append(_mm_tn(v_h, ke[:, ks]))
            ob_s[rows, :] = jnp.concatenate(o_parts, axis=1)
            state[...] = st * jnp.exp(b_last) + jnp.concatenate(u_parts, axis=1)

        ga = ga_s[...]
        sg_a = _sigmoid(ga)
        silu_a = ga * sg_a
        attn = attn_s[...]
        cat_s[:, :W_GA] = (attn * silu_a).astype(BF16)
        gb = gb_s[...]
        sg_b = _sigmoid(gb)
        silu_b = gb * sg_b
        nw = nw_ref[...]
        on_parts = []
        for h in range(GLA_HEADS):
            vs = slice(h * GLA_DV, (h + 1) * GLA_DV)
            o_h = ob_s[:, vs]
            rs = lax.rsqrt(jnp.mean(o_h * o_h, axis=1, keepdims=True) + EPS)
            on_parts.append(o_h * rs * nw)
        on = jnp.concatenate(on_parts, axis=1)
        cat_s[:, W_GA:] = (on * silu_b).astype(BF16)
        cat = cat_s[...]
        hres = ALPHA * x + _mm(cat, wout_ref[...])
        mu = jnp.mean(hres, axis=1, keepdims=True)
        hc = hres - mu
        rstd = lax.rsqrt(jnp.mean(hc * hc, axis=1, keepdims=True) + EPS)
        xhat = hc * rstd
        g_ln = lng_ref[...]
        err = xhat * g_ln + lnb_ref[...] - t_ref[...]
        loss_ref[...] += jnp.sum(err * err) * (0.5 / D_MODEL)
        dy = err * (1.0 / D_MODEL)
        glng_ref[...] += jnp.sum(dy * xhat, axis=0, keepdims=True)
        glnb_ref[...] += jnp.sum(dy, axis=0, keepdims=True)
        dxh = dy * g_ln
        dh = rstd * (dxh - jnp.mean(dxh, axis=1, keepdims=True) - xhat * jnp.mean(dxh * xhat, axis=1, keepdims=True))
        dh_ref[...] = dh
        dh16 = dh.astype(BF16)
        for h in range(2):
            dwout_ref[h] += _mm_tn(cat, dh16[:, h * HALF:(h + 1) * HALF])
        dcat = _mm_nt(dh16, wout_ref[...])

        d_a = dcat[:, :W_GA]
        dattn_ref[...] = (d_a * silu_a).astype(BF16)
        dga_ref[...] = (d_a * attn * (sg_a * (1.0 + ga * (1.0 - sg_a)))).astype(BF16)
        d_b = dcat[:, W_GA:]
        dgb_ref[...] = (d_b * on * (sg_b * (1.0 + gb * (1.0 - sg_b)))).astype(BF16)
        d_on = d_b * silu_b
        gnw = jnp.zeros((1, GLA_DV), F32)
        do_parts = []
        for h in range(GLA_HEADS):
            vs = slice(h * GLA_DV, (h + 1) * GLA_DV)
            o_h = ob_s[:, vs]
            rs = lax.rsqrt(jnp.mean(o_h * o_h, axis=1, keepdims=True) + EPS)
            d_on_h = d_on[:, vs]
            gnw = gnw + jnp.sum(d_on_h * o_h * rs, axis=0, keepdims=True)
            gg = d_on_h * nw
            do_parts.append(rs * gg - o_h * (rs * rs * rs) * jnp.mean(gg * o_h, axis=1, keepdims=True))
        gnw_ref[...] += gnw
        dob_ref[...] = jnp.concatenate(do_parts, axis=1).astype(BF16)

    tile = lambda w: pl.BlockSpec((TM, w), lambda i: (i, 0))
    whole = lambda shape: pl.BlockSpec(shape, lambda i: tuple(0 for _ in shape), pipeline_mode=pl.Buffered(1))
    out_shape = (
        jax.ShapeDtypeStruct((s_len, W_QA), BF16),
        jax.ShapeDtypeStruct((s_len, W_KA), BF16),
        jax.ShapeDtypeStruct((s_len, W_VA), BF16),
        jax.ShapeDtypeStruct((s_len, W_QB), F32),
        jax.ShapeDtypeStruct((s_len, W_KB), F32),
        jax.ShapeDtypeStruct((s_len, W_VB), BF16),
        jax.ShapeDtypeStruct((s_len, W_R), F32),
        jax.ShapeDtypeStruct((s_len, W_GA), BF16),
        jax.ShapeDtypeStruct((s_len, W_GA), BF16),
        jax.ShapeDtypeStruct((s_len, W_GB), BF16),
        jax.ShapeDtypeStruct((s_len, W_GB), BF16),
        jax.ShapeDtypeStruct((s_len, D_MODEL), F32),
        jax.ShapeDtypeStruct((s_len // CHUNK, GLA_DV, GLA_HEADS * GLA_DK), F32),
        jax.ShapeDtypeStruct((2, D_MODEL, HALF), F32),
        jax.ShapeDtypeStruct((1, D_MODEL), F32),
        jax.ShapeDtypeStruct((1, D_MODEL), F32),
        jax.ShapeDtypeStruct((1, GLA_DV), F32),
        jax.ShapeDtypeStruct((1, 128), F32),
    )
    out_specs = (
        tile(W_QA), tile(W_KA), tile(W_VA), tile(W_QB), tile(W_KB), tile(W_VB), tile(W_R),
        tile(W_GA), tile(W_GA), tile(W_GB), tile(W_GB), tile(D_MODEL),
        pl.BlockSpec((nch, GLA_DV, GLA_HEADS * GLA_DK), lambda i: (i, 0, 0)),
        whole((2, D_MODEL, HALF)), whole((1, D_MODEL)), whole((1, D_MODEL)), whole((1, GLA_DV)), whole((1, 128)),
    )
    in_specs = [
        tile(D_MODEL), tile(D_MODEL), tile(128), tile(128), tile(128),
        whole((D_PROJ, D_MODEL)), whole((D_MODEL, D_MODEL)), whole((W_R, W_KB)), whole((1, W_KB)),
        pl.BlockSpec(memory_space=pltpu.SMEM), whole((1, GLA_DV)), whole((1, D_MODEL)), whole((1, D_MODEL)),
    ]
    scratch = [
        pltpu.VMEM((BLOCK, W_KA), BF16), pltpu.VMEM((BLOCK, W_VA), BF16),
        pltpu.VMEM((GLA_DV, GLA_HEADS * GLA_DK), F32),
        pltpu.VMEM((TM, W_GA), F32), pltpu.VMEM((TM, W_GA), F32), pltpu.VMEM((TM, W_GB), F32),
        pltpu.VMEM((TM, W_GB), F32), pltpu.VMEM((TM, D_MODEL), BF16),
    ]
    return pl.pallas_call(
        body, name="fwd_head", grid=(nt,), in_specs=in_specs, out_specs=out_specs, out_shape=out_shape,
        scratch_shapes=scratch,
        compiler_params=pltpu.CompilerParams(dimension_semantics=("arbitrary",), vmem_limit_bytes=VMEM_LIMIT),
    )(x, tgt, cos, sa, sb, win, wout, wgu, bg, sinks, nw, lng, lnb)


def _bwd_call(x, dh, qa, ka, va, dattn, dga, dob, dgb, qb, kb, vb, r, st, cos, sa, sb, win, wgu, bg, sinks):
    s_len = x.shape[0]
    nt = s_len // TM
    nblk = TM // BLOCK
    nch = TM // CHUNK

    def body(x_ref, dh_ref, qa_ref, ka_ref, va_ref, kap_ref, vap_ref, dattn_ref, dga_ref, dob_ref, dgb_ref,
             qb_ref, kb_ref, vb_ref, r_ref, st_ref, cos_ref, sa_ref, sb_ref, win_ref, wgu_ref, bg_ref, sinks_ref,
             gx_ref, dwin_ref, gsink_ref, gbg_ref, gwgu_ref,
             dproj, dk_carry, dv_carry, ds_carry, db_s):
        i = pl.program_id(0)
        t = nt - 1 - i

        @pl.when(i == 0)
        def _():
            dk_carry[...] = jnp.zeros_like(dk_carry)
            dv_carry[...] = jnp.zeros_like(dv_carry)
            ds_carry[...] = jnp.zeros_like(ds_carry)
            dwin_ref[...] = jnp.zeros_like(dwin_ref)
            gsink_ref[...] = jnp.zeros_like(gsink_ref)
            gbg_ref[...] = jnp.zeros_like(gbg_ref)
            gwgu_ref[...] = jnp.zeros_like(gwgu_ref)

        cos = cos_ref[...]
        sa = sa_ref[...]
        sb = sb_ref[...]
        cos4, sa4, sb4 = (jnp.concatenate([v] * 4, axis=1) for v in (cos, sa, sb))

        qa = qa_ref[...]
        ka = ka_ref[...]
        va = va_ref[...]
        dattn = dattn_ref[...]
        gsink_rows = [jnp.zeros((1, 1), F32) for _ in range(Q_HEADS)]
        for b in reversed(range(nblk)):
            rows = slice(b * BLOCK, (b + 1) * BLOCK)
            has_prev = (t * nblk + b) > 0
            mask = _attn_mask(has_prev)
            k_cur = ka[rows]
            v_cur = va[rows]
            k_old = kap_ref[...] if b == 0 else ka[(b - 1) * BLOCK:b * BLOCK]
            v_old = vap_ref[...] if b == 0 else va[(b - 1) * BLOCK:b * BLOCK]
            dq_parts, dk_parts, dv_parts = [], [], []
            for j in range(KV_HEADS):
                hs = slice(j * HEAD_A, (j + 1) * HEAD_A)
                kc = jnp.concatenate([k_old[:, hs], k_cur[:, hs]], axis=0)
                vc = jnp.concatenate([v_old[:, hs], v_cur[:, hs]], axis=0)
                qs = _stack_heads(qa[rows], j)
                do_s = _stack_heads(dattn[rows], j)
                probs, p_sink = _softmax_block(qs, kc, mask, _sink_col(sinks_ref, j))
                dp = _mm_nt(do_s, vc)
                d_row = jnp.sum(probs * dp, axis=1, keepdims=True)
                ds16 = (probs * (dp - d_row) * ATT_SCALE).astype(BF16)
                dq_parts.append(_mm(ds16, kc))
                dk_parts.append(_mm_tn(ds16, qs))
                dv_parts.append(_mm_tn(probs.astype(BF16), do_s))
                t_sink = d_row * p_sink
                for g in range(GROUP):
                    gsink_rows[GROUP * j + g] = gsink_rows[GROUP * j + g] - jnp.sum(
                        t_sink[g * BLOCK:(g + 1) * BLOCK], axis=0, keepdims=True)
            dq = _rope_bwd(_unstack_heads(dq_parts), cos4[rows], sa4[rows], sb4[rows])
            dproj[rows, O_QA:O_QA + W_QA] = dq.astype(BF16)
            dk_cur = dk_carry[...] + jnp.concatenate([p[BLOCK:] for p in dk_parts], axis=1)
            dv_cur = dv_carry[...] + jnp.concatenate([p[BLOCK:] for p in dv_parts], axis=1)
            dproj[rows, O_KA:O_KA + W_KA] = _rope_bwd(dk_cur, cos[rows], sa[rows], sb[rows]).astype(BF16)
            dproj[rows, O_VA:O_VA + W_VA] = dv_cur.astype(BF16)
            dk_carry[...] = jnp.concatenate([p[:BLOCK] for p in dk_parts], axis=1)
            dv_carry[...] = jnp.concatenate([p[:BLOCK] for p in dv_parts], axis=1)
        for hq in range(Q_HEADS):
            gsink_ref[hq:hq + 1, :] += jnp.broadcast_to(gsink_rows[hq], (1, 128))

        dproj[:, O_GA:O_GA + W_GA] = dga_ref[...]
        dproj[:, O_GB:O_GB + W_GB] = dgb_ref[...]

        r16 = r_ref[...].astype(BF16)
        wgu16 = wgu_ref[...].astype(BF16)
        z = _mm(r16, wgu16) + bg_ref[...]
        log_a = _log_sigmoid(z) / GLA_TAU
        bcum = _tri_mm(_chunk_tri(TRI_SLAB, False), log_a)
        qb = qb_ref[...]
        kb = kb_ref[...]
        vb = vb_ref[...]
        dob = dob_ref[...]
        e_b = jnp.exp(bcum)
        e_nb = jnp.exp(-bcum)
        qd_f = qb * GLA_SCALE * e_b
        ki_f = kb * e_nb
        qd_all = qd_f.astype(BF16)
        ki_all = ki_f.astype(BF16)
        tril = lax.broadcasted_iota(jnp.int32, (CHUNK, CHUNK), 0) >= lax.broadcasted_iota(jnp.int32, (CHUNK, CHUNK), 1)
        last_row = lax.broadcasted_iota(jnp.int32, (CHUNK, 1), 0) == CHUNK - 1
        for c in reversed(range(nch)):
            rows = slice(c * CHUNK, (c + 1) * CHUNK)
            b_c = bcum[rows]
            b_last = b_c[CHUNK - 1:CHUNK]
            e_e = jnp.exp(b_last - b_c)
            dec = jnp.exp(b_last)
            ke_f = kb[rows] * e_e
            ke = ke_f.astype(BF16)
            sp = st_ref[c]
            sp16 = sp.astype(BF16)
            dsn = ds_carry[...]
            dsn16 = dsn.astype(BF16)
            dqd_p, dki_p, dke_p, dv_p, dsp_p = [], [], [], [], []
            for h in range(GLA_HEADS):
                ks = slice(h * GLA_DK, (h + 1) * GLA_DK)
                vs = slice(h * GLA_DV, (h + 1) * GLA_DV)
                qd = qd_all[rows, ks]
                ki = ki_all[rows, ks]
                v_h = vb[rows, vs]
                do_h = dob[rows, vs]
                a16 = jnp.where(tril, _mm_nt(qd, ki), 0.0).astype(BF16)
                da16 = jnp.where(tril, _mm_nt(do_h, v_h), 0.0).astype(BF16)
                dv_p.append(_mm_tn(a16, do_h) + _mm_nt(ke[:, ks], dsn16[:, ks]))
                dqd_p.append(_mm(da16, ki) + _mm(do_h, sp16[:, ks]))
                dki_p.append(_mm_tn(da16, qd))
                dke_p.append(_mm(v_h, dsn16[:, ks]))
                dsp_p.append(_mm_tn(do_h, qd))
            dqd = jnp.concatenate(dqd_p, axis=1)
            dki = jnp.concatenate(dki_p, axis=1)
            dke = jnp.concatenate(dke_p, axis=1)
            ddec = jnp.sum(dsn * sp, axis=0, keepdims=True)
            ds_carry[...] = dsn * dec + jnp.concatenate(dsp_p, axis=1)
            dproj[rows, O_QB:O_QB + W_QB] = (dqd * e_b[rows] * GLA_SCALE).astype(BF16)
            dproj[rows, O_KB:O_KB + W_KB] = (dki * e_nb[rows] + dke * e_e).astype(BF16)
            dproj[rows, O_VB:O_VB + W_VB] = jnp.concatenate(dv_p, axis=1).astype(BF16)
            dke_ke = dke * ke_f
            d_b = dqd * qd_f[rows] - dki * ki_f[rows] - dke_ke
            d_bl = jnp.sum(dke_ke, axis=0, keepdims=True) + ddec * dec
            db_s[rows, :] = d_b + jnp.where(last_row, d_bl, 0.0)
        dlog_a = _tri_mm(_chunk_tri(TRI_SLAB, True), db_s[...])
        dz = dlog_a * (1.0 / GLA_TAU) * _sigmoid(-z)
        dz16 = dz.astype(BF16)
        gbg_ref[...] += jnp.sum(dz, axis=0, keepdims=True)
        gwgu_ref[...] += _mm_tn(r16, dz16)
        dproj[:, O_R:O_R + W_R] = _mm_nt(dz16, wgu16).astype(BF16)

        dp16 = dproj[...]
        gx_ref[...] = ALPHA * dh_ref[...] + _mm(dp16, win_ref[...])
        x16 = x_ref[...].astype(BF16)
        for h in range(2):
            dwin_ref[h, 0:D_PROJ, :] += _mm_tn(dp16, x16[:, h * HALF:(h + 1) * HALF])

    tile = lambda w: pl.BlockSpec((TM, w), lambda i: (nt - 1 - i, 0))
    whole = lambda shape: pl.BlockSpec(shape, lambda i: tuple(0 for _ in shape), pipeline_mode=pl.Buffered(1))
    prev_blk = pl.BlockSpec((BLOCK, W_KA), lambda i: (jnp.maximum((nt - 1 - i) * nblk - 1, 0), 0))
    in_specs = [
        tile(D_MODEL), tile(D_MODEL), tile(W_QA), tile(W_KA), tile(W_VA), prev_blk, prev_blk,
        tile(W_GA), tile(W_GA), tile(W_GB), tile(W_GB), tile(W_QB), tile(W_KB), tile(W_VB), tile(W_R),
        pl.BlockSpec((nch, GLA_DV, GLA_HEADS * GLA_DK), lambda i: (nt - 1 - i, 0, 0)),
        tile(128), tile(128), tile(128),
        whole((D_PROJ, D_MODEL)), whole((W_R, W_KB)), whole((1, W_KB)), pl.BlockSpec(memory_space=pltpu.SMEM),
    ]
    out_shape = (
        jax.ShapeDtypeStruct((s_len, D_MODEL), F32),
        jax.ShapeDtypeStruct((2, ACC_ROWS, HALF), F32),
        jax.ShapeDtypeStruct((Q_HEADS, 128), F32),
        jax.ShapeDtypeStruct((1, W_KB), F32),
        jax.ShapeDtypeStruct((W_R, W_KB), F32),
    )
    out_specs = (tile(D_MODEL), whole((2, ACC_ROWS, HALF)), whole((Q_HEADS, 128)), whole((1, W_KB)),
                 whole((W_R, W_KB)))
    scratch = [
        pltpu.VMEM((TM, D_PROJ), BF16), pltpu.VMEM((BLOCK, W_KA), F32), pltpu.VMEM((BLOCK, W_VA), F32),
        pltpu.VMEM((GLA_DV, GLA_HEADS * GLA_DK), F32), pltpu.VMEM((TM, W_KB), F32),
    ]
    return pl.pallas_call(
        body, name="bwd_mix", grid=(nt,), in_specs=in_specs, out_specs=out_specs, out_shape=out_shape,
        scratch_shapes=scratch,
        compiler_params=pltpu.CompilerParams(dimension_semantics=("arbitrary",), vmem_limit_bytes=VMEM_LIMIT),
    )(x, dh, qa, ka, va, ka, va, dattn, dga, dob, dgb, qb, kb, vb, r, st, cos, sa, sb, win, wgu, bg, sinks)


def _mesh_place():
    x, y, c = lax.axis_index("x"), lax.axis_index("y"), lax.axis_index("c")
    chips = [(1 - x, y), (x, 1 - y), (1 - x, 1 - y)]
    return x, y, c, chips


def _gather_weights_call(w_lin, w_out, wgu):
    def body(wlin_ref, wout_ref, wgu_ref, wt_ref, wout_full, wgu_all, blk, oblk, asm, send_sems, recv_sems):
        x, y, c, chips = _mesh_place()
        k_me = 2 * x + y
        asm[SHARD_IN - 4:SHARD_PAD, :] = jnp.zeros((SHARD_PAD - SHARD_IN + 4, D_MODEL), F32)
        asm[0:SHARD_IN, :] = wlin_ref[...]
        for h in range(2):
            blk[k_me, h] = asm[0:SHARD_PAD, h * HALF:(h + 1) * HALF].astype(BF16)
            oblk[k_me, h] = wout_ref[:, h * HALF:(h + 1) * HALF].astype(BF16)
        wgu_all[k_me] = wgu_ref[...]

        def blocks(k, hc):
            return (blk.at[k, hc], oblk.at[k, hc])

        def copies(k, hc, sem0, to):
            return [pltpu.make_async_remote_copy(src_ref=blk, dst_ref=blk, send_sem=send_sems.at[sem0 + n],
                                                 recv_sem=recv_sems.at[sem0 + n], device_id=to, device_id_type=MESH)
                    for n, blk in enumerate(blocks(k, hc))]

        def gu_copy(k, r, to):
            return pltpu.make_async_remote_copy(src_ref=wgu_all.at[k], dst_ref=wgu_all.at[k], send_sem=send_sems.at[12 + r],
                                                recv_sem=recv_sems.at[12 + r], device_id=to, device_id_type=MESH)

        started = []
        for r, chip in enumerate(chips):
            started += copies(k_me, c, 2 * r, (*chip, c))
            started.append(gu_copy(k_me, r, (*chip, c)))
        for cp in started:
            cp.start()
        for r, chip in enumerate(chips):
            k_r = 2 * chip[0] + chip[1]
            for cp in copies(k_r, c, 2 * r, (x, y, c)):
                cp.wait_recv()
            passed = copies(k_r, c, 6 + 2 * r, (x, y, 1 - c))
            for cp in passed:
                cp.start()
            started += passed
        for r, chip in enumerate(chips):
            k_r = 2 * chip[0] + chip[1]
            for cp in copies(k_r, 1 - c, 6 + 2 * r, (x, y, c)):
                cp.wait_recv()
            gu_copy(k_r, r, (x, y, c)).wait_recv()
        for cp in started:
            cp.wait_send()

        for k in range(N_CHIPS):
            for h in range(2):
                asm[k * SHARD_IN:k * SHARD_IN + SHARD_PAD, h * HALF:(h + 1) * HALF] = blk[k, h].astype(F32)
                wout_full[k * SHARD_OUT:(k + 1) * SHARD_OUT, h * HALF:(h + 1) * HALF] = oblk[k, h]
        wt_ref[...] = asm[0:D_PROJ, :].astype(BF16)

    vmem = pl.BlockSpec(memory_space=pltpu.VMEM)
    return pl.pallas_call(
        body, name="gather_weights",
        out_shape=(jax.ShapeDtypeStruct((D_PROJ, D_MODEL), BF16),
                   jax.ShapeDtypeStruct((D_MODEL, D_MODEL), BF16),
                   jax.ShapeDtypeStruct((N_CHIPS, W_R, W_KB // N_CHIPS), F32)),
        in_specs=[vmem, vmem, vmem], out_specs=(vmem, vmem, vmem),
        scratch_shapes=[pltpu.VMEM((N_CHIPS, 2, SHARD_PAD, HALF), BF16), pltpu.VMEM((N_CHIPS, 2, SHARD_OUT, HALF), BF16),
                        pltpu.VMEM((ACC_ROWS, D_MODEL), F32),
                        pltpu.SemaphoreType.DMA((15,)), pltpu.SemaphoreType.DMA((15,))],
        compiler_params=pltpu.CompilerParams(vmem_limit_bytes=VMEM_LIMIT),
    )(w_lin, w_out, wgu)


def _adamw(w, g, m, v):
    m = ADAM_B1 * m + (1.0 - ADAM_B1) * g
    v = ADAM_B2 * v + (1.0 - ADAM_B2) * (g * g)
    m_hat = m / (1.0 - ADAM_B1 ** ADAM_STEP)
    v_hat = v / (1.0 - ADAM_B2 ** ADAM_STEP)
    delta = -ADAM_LR * (m_hat / (jnp.sqrt(v_hat) + ADAM_EPS) + ADAM_WD * w)
    return delta, m, v


N_SMALL = 6


def _reduce_grads_call(g_in, g_out, small_grads, small_params):
    def body(gin_hbm, gout_hbm, g_lng, g_lnb, g_bg, g_nw, g_sink, g_wgu, loss_in, *rest):
        params = rest[:3 * N_SMALL]
        lin_in, fin_out, loss_out = rest[3 * N_SMALL:3 * N_SMALL + 3]
        small_out = rest[3 * N_SMALL + 3:7 * N_SMALL + 3]
        (a_in, a_out, b_in, b_out, c_in, s_in, s_out, r_in, r_out, f_in, f_out, pack_ref, tot_ref, pack_all,
         send_sems, recv_sems, local_sems) = rest[7 * N_SMALL + 3:]
        x, y, c, chips = _mesh_place()
        k_me = 2 * x + y
        me = 4 * x + 2 * y + c
        sibling = (x, y, 1 - c)

        pack_ref[...] = jnp.zeros_like(pack_ref)
        for a in range(8):
            pack_ref[P_LNG + a:P_LNG + a + 1, :] = g_lng[:, a * 128:(a + 1) * 128]
            pack_ref[P_LNB + a:P_LNB + a + 1, :] = g_lnb[:, a * 128:(a + 1) * 128]
        for a in range(2):
            pack_ref[P_BG + a:P_BG + a + 1, :] = g_bg[:, a * 128:(a + 1) * 128]
        pack_ref[P_NW:P_NW + 1, :] = g_nw[...]
        lane = lax.broadcasted_iota(jnp.int32, (1, 128), 1)
        sink_row = jnp.zeros((1, 128), F32)
        for hq in range(Q_HEADS):
            sink_row = jnp.where(lane == hq, g_sink[hq:hq + 1, :], sink_row)
        pack_ref[P_SINK:P_SINK + 1, :] = sink_row
        pack_ref[P_LOSS:P_LOSS + 1, :] = loss_in[...]
        gu_w = W_KB // N_CHIPS
        for k in range(N_CHIPS):
            pack_ref[P_GU + W_R * k:P_GU + W_R * (k + 1), 0:gu_w] = g_wgu[:, k * gu_w:(k + 1) * gu_w]
        pack_all[me] = pack_ref[...]
        small = []
        for mask in range(1, 8):
            peer = (x ^ (mask >> 2), y ^ ((mask >> 1) & 1), c ^ (mask & 1))
            small.append(pltpu.make_async_remote_copy(
                src_ref=pack_ref, dst_ref=pack_all.at[me], send_sem=send_sems.at[mask], recv_sem=recv_sems.at[mask],
                device_id=peer, device_id_type=MESH))
        for cp in small:
            cp.start()

        mine = [pltpu.make_async_copy(gin_hbm.at[c], a_in, local_sems.at[0]),
                pltpu.make_async_copy(gout_hbm.at[c], a_out, local_sems.at[1])]
        to_sib = [pltpu.make_async_remote_copy(
                      src_ref=gin_hbm.at[1 - c], dst_ref=b_in,
                      send_sem=send_sems.at[8], recv_sem=recv_sems.at[8], device_id=sibling, device_id_type=MESH),
                  pltpu.make_async_remote_copy(
                      src_ref=gout_hbm.at[1 - c], dst_ref=b_out,
                      send_sem=send_sems.at[9], recv_sem=recv_sems.at[9], device_id=sibling, device_id_type=MESH)]
        for cp in mine + to_sib:
            cp.start()
        for cp in mine:
            cp.wait()
        for cp in to_sib:
            cp.wait_recv()
        for k in range(N_CHIPS):
            rows = slice(k * SHARD_IN, k * SHARD_IN + SHARD_PAD)
            c_in[k] = a_in[rows, :] + b_in[rows, :]

        sent = []
        for r, chip in enumerate(chips):
            k_r = 2 * chip[0] + chip[1]
            s_in[r] = c_in[k_r].astype(BF16)
            s_out[r] = (a_out[k_r] + b_out[k_r]).astype(BF16)
            sent.append(pltpu.make_async_remote_copy(
                src_ref=s_in.at[r], dst_ref=r_in.at[r], send_sem=send_sems.at[10 + 2 * r],
                recv_sem=recv_sems.at[10 + 2 * r], device_id=(*chip, c), device_id_type=MESH))
            sent.append(pltpu.make_async_remote_copy(
                src_ref=s_out.at[r], dst_ref=r_out.at[r], send_sem=send_sems.at[11 + 2 * r],
                recv_sem=recv_sems.at[11 + 2 * r], device_id=(*chip, c), device_id_type=MESH))
            sent[-2].start()
            sent[-1].start()
        own_out = a_out[k_me] + b_out[k_me]
        for cp in sent:
            cp.wait_recv()
        f_in[c] = c_in[k_me] + r_in[0].astype(F32) + r_in[1].astype(F32) + r_in[2].astype(F32)
        f_out[c] = own_out + r_out[0].astype(F32) + r_out[1].astype(F32) + r_out[2].astype(F32)

        swap = [pltpu.make_async_remote_copy(
                    src_ref=f_in.at[c], dst_ref=f_in.at[c],
                    send_sem=send_sems.at[16], recv_sem=recv_sems.at[16], device_id=sibling, device_id_type=MESH),
                pltpu.make_async_remote_copy(
                    src_ref=f_out.at[c], dst_ref=f_out.at[c],
                    send_sem=send_sems.at[17], recv_sem=recv_sems.at[17], device_id=sibling, device_id_type=MESH)]
        for cp in swap:
            cp.start()

        for cp in small:
            cp.wait_recv()
        total = pack_all[0]
        for d in range(1, 8):
            total = total + pack_all[d]
        tot_ref[...] = total
        loss_out[...] = total[P_LOSS:P_LOSS + 1, :]
        g_outs = small_out[0:N_SMALL]
        for a in range(8):
            g_outs[0][:, a * 128:(a + 1) * 128] = total[P_LNG + a:P_LNG + a + 1, :]
            g_outs[1][:, a * 128:(a + 1) * 128] = total[P_LNB + a:P_LNB + a + 1, :]
        for a in range(2):
            g_outs[2][:, a * 128:(a + 1) * 128] = total[P_BG + a:P_BG + a + 1, :]
        g_outs[3][...] = total[P_NW:P_NW + 1, :]
        g_outs[4][...] = total[P_SINK:P_SINK + 1, 0:Q_HEADS]
        gu_rows = pl.ds(pl.multiple_of(P_GU + W_R * k_me, 8), W_R)
        g_outs[5][...] = tot_ref[gu_rows, 0:gu_w]
        for n in range(N_SMALL):
            w_ref, m_ref, v_ref = params[3 * n:3 * n + 3]
            delta, new_m, new_v = _adamw(w_ref[...], g_outs[n][...], m_ref[...], v_ref[...])
            small_out[N_SMALL + n][...] = delta
            small_out[2 * N_SMALL + n][...] = new_m
            small_out[3 * N_SMALL + n][...] = new_v

        other_in = pltpu.make_async_remote_copy(
            src_ref=f_in.at[1 - c], dst_ref=f_in.at[1 - c],
            send_sem=send_sems.at[16], recv_sem=recv_sems.at[16], device_id=sibling, device_id_type=MESH)
        other_out = pltpu.make_async_remote_copy(
            src_ref=f_out.at[1 - c], dst_ref=f_out.at[1 - c],
            send_sem=send_sems.at[17], recv_sem=recv_sems.at[17], device_id=sibling, device_id_type=MESH)
        other_in.wait_recv()
        other_out.wait_recv()
        for cp in small + to_sib + sent + swap:
            cp.wait_send()

        for h in range(2):
            lin_in[:, h * HALF:(h + 1) * HALF] = f_in[h, 0:SHARD_IN, :]
            fin_out[:, h * HALF:(h + 1) * HALF] = f_out[h]

    vmem = pl.BlockSpec(memory_space=pltpu.VMEM)
    hbm = pl.BlockSpec(memory_space=pl.ANY)
    small_shapes = [jax.ShapeDtypeStruct(p.shape, F32) for p in small_params[0::3]]
    return pl.pallas_call(
        body, name="reduce_grads",
        out_shape=(jax.ShapeDtypeStruct((SHARD_IN, D_MODEL), F32), jax.ShapeDtypeStruct((SHARD_OUT, D_MODEL), F32),
                   jax.ShapeDtypeStruct((1, 128), F32), *(small_shapes * 4)),
        in_specs=[hbm, hbm] + [vmem] * (7 + 3 * N_SMALL), out_specs=(vmem,) * (3 + 4 * N_SMALL),
        scratch_shapes=[
            pltpu.VMEM((ACC_ROWS, HALF), F32), pltpu.VMEM((N_CHIPS, SHARD_OUT, HALF), F32),
            pltpu.VMEM((ACC_ROWS, HALF), F32), pltpu.VMEM((N_CHIPS, SHARD_OUT, HALF), F32),
            pltpu.VMEM((N_CHIPS, SHARD_PAD, HALF), F32),
            pltpu.VMEM((3, SHARD_PAD, HALF), BF16), pltpu.VMEM((3, SHARD_OUT, HALF), BF16),
            pltpu.VMEM((3, SHARD_PAD, HALF), BF16), pltpu.VMEM((3, SHARD_OUT, HALF), BF16),
            pltpu.VMEM((2, SHARD_PAD, HALF), F32), pltpu.VMEM((2, SHARD_OUT, HALF), F32),
            pltpu.VMEM((PACK_ROWS, 128), F32), pltpu.VMEM((PACK_ROWS, 128), F32), pltpu.VMEM((8, PACK_ROWS, 128), F32),
            pltpu.SemaphoreType.DMA((18,)), pltpu.SemaphoreType.DMA((18,)), pltpu.SemaphoreType.DMA((2,)),
        ],
        compiler_params=pltpu.CompilerParams(vmem_limit_bytes=VMEM_LIMIT),
    )(g_in, g_out, *small_grads, *small_params)


def _adamw_call(g_in, w_in, m_in, v_in, g_out, w_out, m_out, v_out):
    steps = 4
    rows_out = SHARD_OUT // steps

    def body(gi, wi, mi, vi, go, wo, mo, vo, di, nmi, nvi, do, nmo, nvo):
        di[...], nmi[...], nvi[...] = _adamw(wi[...], gi[...], mi[...], vi[...])
        do[...], nmo[...], nvo[...] = _adamw(wo[...], go[...], mo[...], vo[...])

    t_in = pl.BlockSpec((SHARD_IN, D_MODEL // steps), lambda i: (0, i))
    t_out = pl.BlockSpec((rows_out, D_MODEL), lambda i: (i, 0))
    s_in = jax.ShapeDtypeStruct((SHARD_IN, D_MODEL), F32)
    s_out = jax.ShapeDtypeStruct((SHARD_OUT, D_MODEL), F32)
    return pl.pallas_call(
        body, name="adamw", grid=(steps,), in_specs=[t_in] * 4 + [t_out] * 4, out_specs=(t_in,) * 3 + (t_out,) * 3,
        out_shape=(s_in,) * 3 + (s_out,) * 3,
        compiler_params=pltpu.CompilerParams(dimension_semantics=("arbitrary",)),
    )(g_in, w_in, m_in, v_in, g_out, w_out, m_out, v_out)


def _rope_tables(positions):
    half = 8
    inv_freq = 500000.0 ** (-jnp.arange(half, dtype=F32) / half)
    d = np.arange(128) % HEAD_A
    rotated = d < 2 * half
    freq = jnp.where(rotated, inv_freq[d % half], 0.0)
    ang = positions.astype(F32)[:, None] * freq[None, :]
    sin = jnp.sin(ang)
    sa = sin * np.where(d < half, -1.0, 0.0).astype(np.float32)
    sb = sin * np.where(rotated & (d >= half), 1.0, 0.0).astype(np.float32)
    return jnp.cos(ang), sa, sb


def kernel(x, positions, w_in, gla_w_gate_up, gla_b_gate, attn_sinks, gla_norm_w, w_out, ln_g, ln_b, loss_target, m_w_in, m_gla_w_gate_up, m_gla_b_gate, m_attn_sinks, m_gla_norm_w, m_w_out, m_ln_g, m_ln_b, v_w_in, v_gla_w_gate_up, v_gla_b_gate, v_attn_sinks, v_gla_norm_w, v_w_out, v_ln_g, v_ln_b):
    def lin(w):
        return jnp.transpose(w[0])

    def unlin(w):
        return jnp.transpose(w)[None]

    win, wout, wgu_all = _gather_weights_call(lin(w_in), w_out[0], gla_w_gate_up[0])
    wgu = jnp.transpose(wgu_all, (1, 0, 2)).reshape(W_R, W_KB)
    cos, sa, sb = _rope_tables(positions[0])
    sinks = attn_sinks[0]

    (qa, ka, va, qb, kb, vb, r, dattn, dga, dob, dgb, dh, st, g_wout, g_lng, g_lnb, g_nw, loss) = _fwd_call(
        x[0], loss_target[0], cos, sa, sb, win, wout, wgu, gla_b_gate, sinks, gla_norm_w, ln_g, ln_b)
    gx, g_win, g_sink, g_bg, g_wgu = _bwd_call(
        x[0], dh, qa, ka, va, dattn, dga, dob, dgb, qb, kb, vb, r, st, cos, sa, sb, win, wgu, gla_b_gate, sinks)

    g_wout_by_chip = g_wout.reshape(2, N_CHIPS, SHARD_OUT, HALF)
    small_params = []
    for group in ((ln_g, m_ln_g, v_ln_g), (ln_b, m_ln_b, v_ln_b), (gla_b_gate, m_gla_b_gate, v_gla_b_gate),
                  (gla_norm_w, m_gla_norm_w, v_gla_norm_w), (attn_sinks, m_attn_sinks, v_attn_sinks)):
        small_params += list(group)
    small_params += [gla_w_gate_up[0], m_gla_w_gate_up[0], v_gla_w_gate_up[0]]
    fin_in, fin_out, loss_sum, *small_out = _reduce_grads_call(
        g_win, g_wout_by_chip, (g_lng, g_lnb, g_bg, g_nw, g_sink, g_wgu, loss), small_params)
    d_in, nm_in, nv_in, d_out, nm_out, nv_out = _adamw_call(
        fin_in, lin(w_in), lin(m_w_in), lin(v_w_in), fin_out, w_out[0], m_w_out[0], v_w_out[0])

    def unpack(kind, big_in, big_out):
        lng_, lnb_, bg_, nw_, sink_, gu_ = small_out[kind * N_SMALL:(kind + 1) * N_SMALL]
        return (unlin(big_in), gu_[None], bg_, sink_, nw_, big_out[None], lng_, lnb_)

    loss_total = loss_sum[0, 0]
    g_s, d_s, nm_s, nv_s = 0, 1, 2, 3
    return (loss_total, gx[None], *unpack(g_s, fin_in, fin_out), *unpack(d_s, d_in, d_out),
            *unpack(nm_s, nm_in, nm_out), *unpack(nv_s, nv_in, nv_out))
```

```python
import functools

import jax
import jax.numpy as jnp
import numpy as np
from jax import lax
from jax.experimental import pallas as pl
from jax.experimental.pallas import tpu as pltpu

F32 = jnp.float32
BF16 = jnp.bfloat16
MESH = pl.DeviceIdType.MESH

D_MODEL = 1024
N_CHIPS = 4
W_QA, W_KA, W_VA, W_GA, W_QB, W_KB, W_VB, W_GB, W_R = 512, 128, 128, 512, 256, 256, 512, 512, 16
O_QA = 0
O_KA = O_QA + W_QA
O_VA = O_KA + W_KA
O_GA = O_VA + W_VA
O_QB = O_GA + W_GA
O_KB = O_QB + W_QB
O_VB = O_KB + W_KB
O_GB = O_VB + W_VB
O_R = O_GB + W_GB
D_PROJ = O_R + W_R
SHARD_IN = D_PROJ // N_CHIPS
SHARD_OUT = D_MODEL // N_CHIPS
SHARD_PAD = 720
ACC_ROWS = -(-((N_CHIPS - 1) * SHARD_IN + SHARD_PAD) // 8) * 8
HALF = D_MODEL // 2

HEAD_A = 64
Q_HEADS = 8
KV_HEADS = 2
GROUP = 4
BLOCK = 128
GLA_HEADS = 4
GLA_DK = 64
GLA_DV = 128
CHUNK = 64
GLA_TAU = 16.0
EPS = 1e-5
ALPHA = 2.0 ** 0.25
ATT_SCALE = HEAD_A ** -0.5
GLA_SCALE = GLA_DK ** -0.5

ADAM_LR = 0.001
ADAM_B1 = 0.9
ADAM_B2 = 0.999
ADAM_EPS = 1e-08
ADAM_WD = 0.01
ADAM_STEP = 10

TM = 256
TRI_SLAB = 128
VMEM_LIMIT = 56 * 1024 * 1024

P_LNG, P_LNB, P_BG, P_NW, P_SINK, P_LOSS, P_GU = 0, 8, 16, 18, 19, 20, 24
PACK_ROWS = P_GU + N_CHIPS * 16


def _mm(a, b):
    return jnp.dot(a, b, preferred_element_type=F32)


def _mm_nt(a, b):
    return lax.dot_general(a, b, (((1,), (1,)), ((), ())), preferred_element_type=F32)


def _mm_tn(a, b):
    return lax.dot_general(a, b, (((0,), (0,)), ((), ())), preferred_element_type=F32)


def _split3(a):
    hi = a.astype(BF16)
    r1 = a - hi.astype(F32)
    mid = r1.astype(BF16)
    lo = (r1 - mid.astype(F32)).astype(BF16)
    return hi, mid, lo


def _tri_mm(tri, a):
    slab = tri.shape[0]
    hi, mid, lo = _split3(a)
    return jnp.concatenate(
        [_mm(tri, hi[s:s + slab]) + _mm(tri, mid[s:s + slab]) + _mm(tri, lo[s:s + slab])
         for s in range(0, a.shape[0], slab)], axis=0)


def _chunk_tri(n, upper):
    r = lax.broadcasted_iota(jnp.int32, (n, n), 0)
    c = lax.broadcasted_iota(jnp.int32, (n, n), 1)
    same = (r >> 6) == (c >> 6)
    order = (c >= r) if upper else (c <= r)
    return jnp.where(same & order, 1.0, 0.0).astype(BF16)


def _rope(t, cos, sa, sb):
    w = t.shape[1]
    return t * cos + pltpu.roll(t, w - 8, 1) * sa + pltpu.roll(t, 8, 1) * sb


def _rope_bwd(d, cos, sa, sb):
    w = d.shape[1]
    return d * cos + pltpu.roll(d * sa, 8, 1) + pltpu.roll(d * sb, w - 8, 1)


def _log_sigmoid(z):
    return jnp.minimum(z, 0.0) - jnp.log1p(jnp.exp(-jnp.abs(z)))


def _sigmoid(z):
    return 1.0 / (1.0 + jnp.exp(-z))


def _attn_mask(has_prev):
    r = lax.broadcasted_iota(jnp.int32, (GROUP * BLOCK, 2 * BLOCK), 0) & (BLOCK - 1)
    k = lax.broadcasted_iota(jnp.int32, (GROUP * BLOCK, 2 * BLOCK), 1)
    first_key = jnp.where(has_prev, 0, BLOCK)
    return (k > r) & (k <= r + BLOCK) & (k >= first_key)


def _sink_col(sinks_ref, j):
    r = lax.broadcasted_iota(jnp.int32, (GROUP * BLOCK, 1), 0) >> 7
    col = jnp.full((GROUP * BLOCK, 1), sinks_ref[GROUP * j], F32)
    for g in range(1, GROUP):
        col = jnp.where(r == g, sinks_ref[GROUP * j + g], col)
    return col


def _stack_heads(t, j):
    return jnp.concatenate([t[:, (GROUP * j + g) * HEAD_A:(GROUP * j + g + 1) * HEAD_A] for g in range(GROUP)], axis=0)


def _unstack_heads(parts):
    return jnp.concatenate([parts[j][g * BLOCK:(g + 1) * BLOCK] for j in range(KV_HEADS) for g in range(GROUP)], axis=1)


def _softmax_block(qs, kc, mask, sink):
    s = _mm_nt(qs, kc) * ATT_SCALE
    s = jnp.where(mask, s, -jnp.inf)
    m = jnp.maximum(jnp.max(s, axis=1, keepdims=True), sink)
    p = jnp.exp(s - m)
    e_sink = jnp.exp(sink - m)
    inv = 1.0 / (jnp.sum(p, axis=1, keepdims=True) + e_sink)
    return p * inv, e_sink * inv


def _fwd_call(x, tgt, cos, sa, sb, win, wout, wgu, bg, sinks, nw, lng, lnb):
    s_len = x.shape[0]
    nt = s_len // TM
    nblk = TM // BLOCK
    nch = TM // CHUNK

    def body(x_ref, t_ref, cos_ref, sa_ref, sb_ref, win_ref, wout_ref, wgu_ref, bg_ref, sinks_ref, nw_ref,
             lng_ref, lnb_ref,
             qa_ref, ka_ref, va_ref, qb_ref, kb_ref, vb_ref, r_ref, dattn_ref, dga_ref, dob_ref, dgb_ref, dh_ref,
             st_ref, dwout_ref, glng_ref, glnb_ref, gnw_ref, loss_ref,
             kprev, vprev, state, attn_s, ga_s, ob_s, gb_s, cat_s):
        i = pl.program_id(0)

        @pl.when(i == 0)
        def _():
            kprev[...] = jnp.zeros_like(kprev)
            vprev[...] = jnp.zeros_like(vprev)
            state[...] = jnp.zeros_like(state)
            dwout_ref[...] = jnp.zeros_like(dwout_ref)
            glng_ref[...] = jnp.zeros_like(glng_ref)
            glnb_ref[...] = jnp.zeros_like(glnb_ref)
            gnw_ref[...] = jnp.zeros_like(gnw_ref)
            loss_ref[...] = jnp.zeros_like(loss_ref)

        x = x_ref[...]
        xb = x.astype(BF16)

        def proj(off, width):
            return _mm_nt(xb, win_ref[off:off + width, :])

        cos = cos_ref[...]
        sa = sa_ref[...]
        sb = sb_ref[...]
        cos4, sa4, sb4 = (jnp.concatenate([t] * 4, axis=1) for t in (cos, sa, sb))
        qa = _rope(proj(O_QA, W_QA), cos4, sa4, sb4).astype(BF16)
        ka = _rope(proj(O_KA, W_KA), cos, sa, sb).astype(BF16)
        va = proj(O_VA, W_VA).astype(BF16)
        qa_ref[...] = qa
        ka_ref[...] = ka
        va_ref[...] = va
        ga_s[...] = proj(O_GA, W_GA)
        gb_s[...] = proj(O_GB, W_GB)

        for b in range(nblk):
            rows = slice(b * BLOCK, (b + 1) * BLOCK)
            has_prev = (i * nblk + b) > 0
            mask = _attn_mask(has_prev)
            k_cur = ka[rows]
            v_cur = va[rows]
            k_old = kprev[...] if b == 0 else ka[(b - 1) * BLOCK:b * BLOCK]
            v_old = vprev[...] if b == 0 else va[(b - 1) * BLOCK:b * BLOCK]
            outs = []
            for j in range(KV_HEADS):
                hs = slice(j * HEAD_A, (j + 1) * HEAD_A)
                kc = jnp.concatenate([k_old[:, hs], k_cur[:, hs]], axis=0)
                vc = jnp.concatenate([v_old[:, hs], v_cur[:, hs]], axis=0)
                probs, _ = _softmax_block(_stack_heads(qa[rows], j), kc, mask, _sink_col(sinks_ref, j))
                outs.append(_mm(probs.astype(BF16), vc))
            attn_s[rows, :] = _unstack_heads(outs)
        kprev[...] = ka[(nblk - 1) * BLOCK:]
        vprev[...] = va[(nblk - 1) * BLOCK:]

        r = proj(O_R, W_R)
        r_ref[...] = r
        z = _mm(r.astype(BF16), wgu_ref[...].astype(BF16)) + bg_ref[...]
        log_a = _log_sigmoid(z) / GLA_TAU
        bcum = _tri_mm(_chunk_tri(TRI_SLAB, False), log_a)
        qb = proj(O_QB, W_QB)
        kb = proj(O_KB, W_KB)
        vb = proj(O_VB, W_VB).astype(BF16)
        qb_ref[...] = qb
        kb_ref[...] = kb
        vb_ref[...] = vb
        qd_all = (qb * GLA_SCALE * jnp.exp(bcum)).astype(BF16)
        ki_all = (kb * jnp.exp(-bcum)).astype(BF16)
        tril = lax.broadcasted_iota(jnp.int32, (CHUNK, CHUNK), 0) >= lax.broadcasted_iota(jnp.int32, (CHUNK, CHUNK), 1)
        for c in range(nch):
            rows = slice(c * CHUNK, (c + 1) * CHUNK)
            b_c = bcum[rows]
            b_last = b_c[CHUNK - 1:CHUNK]
            ke = (kb[rows] * jnp.exp(b_last - b_c)).astype(BF16)
            st = state[...]
            st_ref[c] = st
            st16 = st.astype(BF16)
            o_parts, u_parts = [], []
            for h in range(GLA_HEADS):
                ks = slice(h * GLA_DK, (h + 1) * GLA_DK)
                vs = slice(h * GLA_DV, (h + 1) * GLA_DV)
                qd = qd_all[rows, ks]
                v_h = vb[rows, vs]
                a = jnp.where(tril, _mm_nt(qd, ki_all[rows, ks]), 0.0)
                o_parts.append(_mm(a.astype(BF16), v_h) + _mm_nt(qd, st16[:, ks]))
                u_parts.append(_mm_tn(v_h, ke[:, ks]))
            ob_s[rows, :] = jnp.concatenate(o_parts, axis=1)
            state[...] = st * jnp.exp(b_last) + jnp.concatenate(u_parts, axis=1)

        ga = ga_s[...]
        sg_a = _sigmoid(ga)
        silu_a = ga * sg_a
        attn = attn_s[...]
        cat_s[:, :W_GA] = (attn * silu_a).astype(BF16)
        gb = gb_s[...]
        sg_b = _sigmoid(gb)
        silu_b = gb * sg_b
        nw = nw_ref[...]
        on_parts = []
        for h in range(GLA_HEADS):
            vs = slice(h * GLA_DV, (h + 1) * GLA_DV)
            o_h = ob_s[:, vs]
            rs = lax.rsqrt(jnp.mean(o_h * o_h, axis=1, keepdims=True) + EPS)
            on_parts.append(o_h * rs * nw)
        on = jnp.concatenate(on_parts, axis=1)
        cat_s[:, W_GA:] = (on * silu_b).astype(BF16)
        cat = cat_s[...]
        hres = ALPHA * x + _mm(cat, wout_ref[...])
        mu = jnp.mean(hres, axis=1, keepdims=True)
        hc = hres - mu
        rstd = lax.rsqrt(jnp.mean(hc * hc, axis=1, keepdims=True) + EPS)
        xhat = hc * rstd
        g_ln = lng_ref[...]
        err = xhat * g_ln + lnb_ref[...] - t_ref[...]
        loss_ref[...] += jnp.sum(err * err) * (0.5 / D_MODEL)
        dy = err * (1.0 / D_MODEL)
        glng_ref[...] += jnp.sum(dy * xhat, axis=0, keepdims=True)
        glnb_ref[...] += jnp.sum(dy, axis=0, keepdims=True)
        dxh = dy * g_ln
        dh = rstd * (dxh - jnp.mean(dxh, axis=1, keepdims=True) - xhat * jnp.mean(dxh * xhat, axis=1, keepdims=True))
        dh_ref[...] = dh
        dh16 = dh.astype(BF16)
        for h in range(2):
            dwout_ref[h] += _mm_tn(cat, dh16[:, h * HALF:(h + 1) * HALF])
        dcat = _mm_nt(dh16, wout_ref[...])

        d_a = dcat[:, :W_GA]
        dattn_ref[...] = (d_a * silu_a).astype(BF16)
        dga_ref[...] = (d_a * attn * (sg_a * (1.0 + ga * (1.0 - sg_a)))).astype(BF16)
        d_b = dcat[:, W_GA:]
        dgb_ref[...] = (d_b * on * (sg_b * (1.0 + gb * (1.0 - sg_b)))).astype(BF16)
        d_on = d_b * silu_b
        gnw = jnp.zeros((1, GLA_DV), F32)
        do_parts = []
        for h in range(GLA_HEADS):
            vs = slice(h * GLA_DV, (h + 1) * GLA_DV)
            o_h = ob_s[:, vs]
            rs = lax.rsqrt(jnp.mean(o_h * o_h, axis=1, keepdims=True) + EPS)
            d_on_h = d_on[:, vs]
            gnw = gnw + jnp.sum(d_on_h * o_h * rs, axis=0, keepdims=True)
            gg = d_on_h * nw
            do_parts.append(rs * gg - o_h * (rs * rs * rs) * jnp.mean(gg * o_h, axis=1, keepdims=True))
        gnw_ref[...] += gnw
        dob_ref[...] = jnp.concatenate(do_parts, axis=1).astype(BF16)

    tile = lambda w: pl.BlockSpec((TM, w), lambda i: (i, 0))
    whole = lambda shape: pl.BlockSpec(shape, lambda i: tuple(0 for _ in shape), pipeline_mode=pl.Buffered(1))
    out_shape = (
        jax.ShapeDtypeStruct((s_len, W_QA), BF16),
        jax.ShapeDtypeStruct((s_len, W_KA), BF16),
        jax.ShapeDtypeStruct((s_len, W_VA), BF16),
        jax.ShapeDtypeStruct((s_len, W_QB), F32),
        jax.ShapeDtypeStruct((s_len, W_KB), F32),
        jax.ShapeDtypeStruct((s_len, W_VB), BF16),
        jax.ShapeDtypeStruct((s_len, W_R), F32),
        jax.ShapeDtypeStruct((s_len, W_GA), BF16),
        jax.ShapeDtypeStruct((s_len, W_GA), BF16),
        jax.ShapeDtypeStruct((s_len, W_GB), BF16),
        jax.ShapeDtypeStruct((s_len, W_GB), BF16),
        jax.ShapeDtypeStruct((s_len, D_MODEL), F32),
        jax.ShapeDtypeStruct((s_len // CHUNK, GLA_DV, GLA_HEADS * GLA_DK), F32),
        jax.ShapeDtypeStruct((2, D_MODEL, HALF), F32),
        jax.ShapeDtypeStruct((1, D_MODEL), F32),
        jax.ShapeDtypeStruct((1, D_MODEL), F32),
        jax.ShapeDtypeStruct((1, GLA_DV), F32),
        jax.ShapeDtypeStruct((1, 128), F32),
    )
    out_specs = (
        tile(W_QA), tile(W_KA), tile(W_VA), tile(W_QB), tile(W_KB), tile(W_VB), tile(W_R),
        tile(W_GA), tile(W_GA), tile(W_GB), tile(W_GB), tile(D_MODEL),
        pl.BlockSpec((nch, GLA_DV, GLA_HEADS * GLA_DK), lambda i: (i, 0, 0)),
        whole((2, D_MODEL, HALF)), whole((1, D_MODEL)), whole((1, D_MODEL)), whole((1, GLA_DV)), whole((1, 128)),
    )
    in_specs = [
        tile(D_MODEL), tile(D_MODEL), tile(128), tile(128), tile(128),
        whole((D_PROJ, D_MODEL)), whole((D_MODEL, D_MODEL)), whole((W_R, W_KB)), whole((1, W_KB)),
        pl.BlockSpec(memory_space=pltpu.SMEM), whole((1, GLA_DV)), whole((1, D_MODEL)), whole((1, D_MODEL)),
    ]
    scratch = [
        pltpu.VMEM((BLOCK, W_KA), BF16), pltpu.VMEM((BLOCK, W_VA), BF16),
        pltpu.VMEM((GLA_DV, GLA_HEADS * GLA_DK), F32),
        pltpu.VMEM((TM, W_GA), F32), pltpu.VMEM((TM, W_GA), F32), pltpu.VMEM((TM, W_GB), F32),
        pltpu.VMEM((TM, W_GB), F32), pltpu.VMEM((TM, D_MODEL), BF16),
    ]
    return pl.pallas_call(
        body, name="fwd_head", grid=(nt,), in_specs=in_specs, out_specs=out_specs, out_shape=out_shape,
        scratch_shapes=scratch,
        compiler_params=pltpu.CompilerParams(dimension_semantics=("arbitrary",), vmem_limit_bytes=VMEM_LIMIT),
    )(x, tgt, cos, sa, sb, win, wout, wgu, bg, sinks, nw, lng, lnb)


def _bwd_call(x, dh, qa, ka, va, dattn, dga, dob, dgb, qb, kb, vb, r, st, cos, sa, sb, win, wgu, bg, sinks):
    s_len = x.shape[0]
    nt = s_len // TM
    nblk = TM // BLOCK
    nch = TM // CHUNK

    def body(x_ref, dh_ref, qa_ref, ka_ref, va_ref, kap_ref, vap_ref, dattn_ref, dga_ref, dob_ref, dgb_ref,
             qb_ref, kb_ref, vb_ref, r_ref, st_ref, cos_ref, sa_ref, sb_ref, win_ref, wgu_ref, bg_ref, sinks_ref,
             gx_ref, dwin_ref, gsink_ref, gbg_ref, gwgu_ref,
             dproj, dk_carry, dv_carry, ds_carry, db_s):
        i = pl.program_id(0)
        t = nt - 1 - i

        @pl.when(i == 0)
        def _():
            dk_carry[...] = jnp.zeros_like(dk_carry)
            dv_carry[...] = jnp.zeros_like(dv_carry)
            ds_carry[...] = jnp.zeros_like(ds_carry)
            dwin_ref[...] = jnp.zeros_like(dwin_ref)
            gsink_ref[...] = jnp.zeros_like(gsink_ref)
            gbg_ref[...] = jnp.zeros_like(gbg_ref)
            gwgu_ref[...] = jnp.zeros_like(gwgu_ref)

        cos = cos_ref[...]
        sa = sa_ref[...]
        sb = sb_ref[...]
        cos4, sa4, sb4 = (jnp.concatenate([v] * 4, axis=1) for v in (cos, sa, sb))

        qa = qa_ref[...]
        ka = ka_ref[...]
        va = va_ref[...]
        dattn = dattn_ref[...]
        gsink_rows = [jnp.zeros((1, 1), F32) for _ in range(Q_HEADS)]
        for b in reversed(range(nblk)):
            rows = slice(b * BLOCK, (b + 1) * BLOCK)
            has_prev = (t * nblk + b) > 0
            mask = _attn_mask(has_prev)
            k_cur = ka[rows]
            v_cur = va[rows]
            k_old = kap_ref[...] if b == 0 else ka[(b - 1) * BLOCK:b * BLOCK]
            v_old = vap_ref[...] if b == 0 else va[(b - 1) * BLOCK:b * BLOCK]
            dq_parts, dk_parts, dv_parts = [], [], []
            for j in range(KV_HEADS):
                hs = slice(j * HEAD_A, (j + 1) * HEAD_A)
                kc = jnp.concatenate([k_old[:, hs], k_cur[:, hs]], axis=0)
                vc = jnp.concatenate([v_old[:, hs], v_cur[:, hs]], axis=0)
                qs = _stack_heads(qa[rows], j)
                do_s = _stack_heads(dattn[rows], j)
                probs, p_sink = _softmax_block(qs, kc, mask, _sink_col(sinks_ref, j))
                dp = _mm_nt(do_s, vc)
                d_row = jnp.sum(probs * dp, axis=1, keepdims=True)
                ds16 = (probs * (dp - d_row) * ATT_SCALE).astype(BF16)
                dq_parts.append(_mm(ds16, kc))
                dk_parts.append(_mm_tn(ds16, qs))
                dv_parts.append(_mm_tn(probs.astype(BF16), do_s))
                t_sink = d_row * p_sink
                for g in range(GROUP):
                    gsink_rows[GROUP * j + g] = gsink_rows[GROUP * j + g] - jnp.sum(
                        t_sink[g * BLOCK:(g + 1) * BLOCK], axis=0, keepdims=True)
            dq = _rope_bwd(_unstack_heads(dq_parts), cos4[rows], sa4[rows], sb4[rows])
            dproj[rows, O_QA:O_QA + W_QA] = dq.astype(BF16)
            dk_cur = dk_carry[...] + jnp.concatenate([p[BLOCK:] for p in dk_parts], axis=1)
            dv_cur = dv_carry[...] + jnp.concatenate([p[BLOCK:] for p in dv_parts], axis=1)
            dproj[rows, O_KA:O_KA + W_KA] = _rope_bwd(dk_cur, cos[rows], sa[rows], sb[rows]).astype(BF16)
            dproj[rows, O_VA:O_VA + W_VA] = dv_cur.astype(BF16)
            dk_carry[...] = jnp.concatenate([p[:BLOCK] for p in dk_parts], axis=1)
            dv_carry[...] = jnp.concatenate([p[:BLOCK] for p in dv_parts], axis=1)
        for hq in range(Q_HEADS):
            gsink_ref[hq:hq + 1, :] += jnp.broadcast_to(gsink_rows[hq], (1, 128))

        dproj[:, O_GA:O_GA + W_GA] = dga_ref[...]
        dproj[:, O_GB:O_GB + W_GB] = dgb_ref[...]

        r16 = r_ref[...].astype(BF16)
        wgu16 = wgu_ref[...].astype(BF16)
        z = _mm(r16, wgu16) + bg_ref[...]
        log_a = _log_sigmoid(z) / GLA_TAU
        bcum = _tri_mm(_chunk_tri(TRI_SLAB, False), log_a)
        qb = qb_ref[...]
        kb = kb_ref[...]
        vb = vb_ref[...]
        dob = dob_ref[...]
        e_b = jnp.exp(bcum)
        e_nb = jnp.exp(-bcum)
        qd_f = qb * GLA_SCALE * e_b
        ki_f = kb * e_nb
        qd_all = qd_f.astype(BF16)
        ki_all = ki_f.astype(BF16)
        tril = lax.broadcasted_iota(jnp.int32, (CHUNK, CHUNK), 0) >= lax.broadcasted_iota(jnp.int32, (CHUNK, CHUNK), 1)
        last_row = lax.broadcasted_iota(jnp.int32, (CHUNK, 1), 0) == CHUNK - 1
        for c in reversed(range(nch)):
            rows = slice(c * CHUNK, (c + 1) * CHUNK)
            b_c = bcum[rows]
            b_last = b_c[CHUNK - 1:CHUNK]
            e_e = jnp.exp(b_last - b_c)
            dec = jnp.exp(b_last)
            ke_f = kb[rows] * e_e
            ke = ke_f.astype(BF16)
            sp = st_ref[c]
            sp16 = sp.astype(BF16)
            dsn = ds_carry[...]
            dsn16 = dsn.astype(BF16)
            dqd_p, dki_p, dke_p, dv_p, dsp_p = [], [], [], [], []
            for h in range(GLA_HEADS):
                ks = slice(h * GLA_DK, (h + 1) * GLA_DK)
                vs = slice(h * GLA_DV, (h + 1) * GLA_DV)
                qd = qd_all[rows, ks]
                ki = ki_all[rows, ks]
                v_h = vb[rows, vs]
                do_h = dob[rows, vs]
                a16 = jnp.where(tril, _mm_nt(qd, ki), 0.0).astype(BF16)
                da16 = jnp.where(tril, _mm_nt(do_h, v_h), 0.0).astype(BF16)
                dv_p.append(_mm_tn(a16, do_h) + _mm_nt(ke[:, ks], dsn16[:, ks]))
                dqd_p.append(_mm(da16, ki) + _mm(do_h, sp16[:, ks]))
                dki_p.append(_mm_tn(da16, qd))
                dke_p.append(_mm(v_h, dsn16[:, ks]))
                dsp_p.append(_mm_tn(do_h, qd))
            dqd = jnp.concatenate(dqd_p, axis=1)
            dki = jnp.concatenate(dki_p, axis=1)
            dke = jnp.concatenate(dke_p, axis=1)
            ddec = jnp.sum(dsn * sp, axis=0, keepdims=True)
            ds_carry[...] = dsn * dec + jnp.concatenate(dsp_p, axis=1)
            dproj[rows, O_QB:O_QB + W_QB] = (dqd * e_b[rows] * GLA_SCALE).astype(BF16)
            dproj[rows, O_KB:O_KB + W_KB] = (dki * e_nb[rows] + dke * e_e).astype(BF16)
            dproj[rows, O_VB:O_VB + W_VB] = jnp.concatenate(dv_p, axis=1).astype(BF16)
            dke_ke = dke * ke_f
            d_b = dqd * qd_f[rows] - dki * ki_f[rows] - dke_ke
            d_bl = jnp.sum(dke_ke, axis=0, keepdims=True) + ddec * dec
            db_s[rows, :] = d_b + jnp.where(last_row, d_bl, 0.0)
        dlog_a = _tri_mm(_chunk_tri(TRI_SLAB, True), db_s[...])
        dz = dlog_a * (1.0 / GLA_TAU) * _sigmoid(-z)
        dz16 = dz.astype(BF16)
        gbg_ref[...] += jnp.sum(dz, axis=0, keepdims=True)
        gwgu_ref[...] += _mm_tn(r16, dz16)
        dproj[:, O_R:O_R + W_R] = _mm_nt(dz16, wgu16).astype(BF16)

        dp16 = dproj[...]
        gx_ref[...] = ALPHA * dh_ref[...] + _mm(dp16, win_ref[...])
        x16 = x_ref[...].astype(BF16)
        for h in range(2):
            dwin_ref[h, 0:D_PROJ, :] += _mm_tn(dp16, x16[:, h * HALF:(h + 1) * HALF])

    tile = lambda w: pl.BlockSpec((TM, w), lambda i: (nt - 1 - i, 0))
    whole = lambda shape: pl.BlockSpec(shape, lambda i: tuple(0 for _ in shape), pipeline_mode=pl.Buffered(1))
    prev_blk = pl.BlockSpec((BLOCK, W_KA), lambda i: (jnp.maximum((nt - 1 - i) * nblk - 1, 0), 0))
    in_specs = [
        tile(D_MODEL), tile(D_MODEL), tile(W_QA), tile(W_KA), tile(W_VA), prev_blk, prev_blk,
        tile(W_GA), tile(W_GA), tile(W_GB), tile(W_GB), tile(W_QB), tile(W_KB), tile(W_VB), tile(W_R),
        pl.BlockSpec((nch, GLA_DV, GLA_HEADS * GLA_DK), lambda i: (nt - 1 - i, 0, 0)),
        tile(128), tile(128), tile(128),
        whole((D_PROJ, D_MODEL)), whole((W_R, W_KB)), whole((1, W_KB)), pl.BlockSpec(memory_space=pltpu.SMEM),
    ]
    out_shape = (
        jax.ShapeDtypeStruct((s_len, D_MODEL), F32),
        jax.ShapeDtypeStruct((2, ACC_ROWS, HALF), F32),
        jax.ShapeDtypeStruct((Q_HEADS, 128), F32),
        jax.ShapeDtypeStruct((1, W_KB), F32),
        jax.ShapeDtypeStruct((W_R, W_KB), F32),
    )
    out_specs = (tile(D_MODEL), whole((2, ACC_ROWS, HALF)), whole((Q_HEADS, 128)), whole((1, W_KB)),
                 whole((W_R, W_KB)))
    scratch = [
        pltpu.VMEM((TM, D_PROJ), BF16), pltpu.VMEM((BLOCK, W_KA), F32), pltpu.VMEM((BLOCK, W_VA), F32),
        pltpu.VMEM((GLA_DV, GLA_HEADS * GLA_DK), F32), pltpu.VMEM((TM, W_KB), F32),
    ]
    return pl.pallas_call(
        body, name="bwd_mix", grid=(nt,), in_specs=in_specs, out_specs=out_specs, out_shape=out_shape,
        scratch_shapes=scratch,
        compiler_params=pltpu.CompilerParams(dimension_semantics=("arbitrary",), vmem_limit_bytes=VMEM_LIMIT),
    )(x, dh, qa, ka, va, ka, va, dattn, dga, dob, dgb, qb, kb, vb, r, st, cos, sa, sb, win, wgu, bg, sinks)


def _mesh_place():
    x, y, c = lax.axis_index("x"), lax.axis_index("y"), lax.axis_index("c")
    chips = [(1 - x, y), (x, 1 - y), (1 - x, 1 - y)]
    return x, y, c, chips


def _gather_weights_call(w_lin, w_out, wgu):
    def body(wlin_ref, wout_ref, wgu_ref, wt_ref, wout_full, wgu_all, blk, oblk, asm, send_sems, recv_sems):
        x, y, c, chips = _mesh_place()
        k_me = 2 * x + y
        asm[SHARD_IN - 4:SHARD_PAD, :] = jnp.zeros((SHARD_PAD - SHARD_IN + 4, D_MODEL), F32)
        asm[0:SHARD_IN, :] = wlin_ref[...]
        for h in range(2):
            blk[k_me, h] = asm[0:SHARD_PAD, h * HALF:(h + 1) * HALF].astype(BF16)
            oblk[k_me, h] = wout_ref[:, h * HALF:(h + 1) * HALF].astype(BF16)
        wgu_all[k_me] = wgu_ref[...]

        def blocks(k, hc):
            return (blk.at[k, hc], oblk.at[k, hc])

        def copies(k, hc, sem0, to):
            return [pltpu.make_async_remote_copy(src_ref=blk, dst_ref=blk, send_sem=send_sems.at[sem0 + n],
                                                 recv_sem=recv_sems.at[sem0 + n], device_id=to, device_id_type=MESH)
                    for n, blk in enumerate(blocks(k, hc))]

        def gu_copy(k, r, to):
            return pltpu.make_async_remote_copy(src_ref=wgu_all.at[k], dst_ref=wgu_all.at[k], send_sem=send_sems.at[12 + r],
                                                recv_sem=recv_sems.at[12 + r], device_id=to, device_id_type=MESH)

        started = []
        for r, chip in enumerate(chips):
            started += copies(k_me, c, 2 * r, (*chip, c))
            started.append(gu_copy(k_me, r, (*chip, c)))
        for cp in started:
            cp.start()
        for r, chip in enumerate(chips):
            k_r = 2 * chip[0] + chip[1]
            for cp in copies(k_r, c, 2 * r, (x, y, c)):
                cp.wait_recv()
            passed = copies(k_r, c, 6 + 2 * r, (x, y, 1 - c))
            for cp in passed:
                cp.start()
            started += passed
        for r, chip in enumerate(chips):
            k_r = 2 * chip[0] + chip[1]
            for cp in copies(k_r, 1 - c, 6 + 2 * r, (x, y, c)):
                cp.wait_recv()
            gu_copy(k_r, r, (x, y, c)).wait_recv()
        for cp in started:
            cp.wait_send()

        for k in range(N_CHIPS):
            for h in range(2):
                asm[k * SHARD_IN:k * SHARD_IN + SHARD_PAD, h * HALF:(h + 1) * HALF] = blk[k, h].astype(F32)
                wout_full[k * SHARD_OUT:(k + 1) * SHARD_OUT, h * HALF:(h + 1) * HALF] = oblk[k, h]
        wt_ref[...] = asm[0:D_PROJ, :].astype(BF16)

    vmem = pl.BlockSpec(memory_space=pltpu.VMEM)
    return pl.pallas_call(
        body, name="gather_weights",
        out_shape=(jax.ShapeDtypeStruct((D_PROJ, D_MODEL), BF16),
                   jax.ShapeDtypeStruct((D_MODEL, D_MODEL), BF16),
                   jax.ShapeDtypeStruct((N_CHIPS, W_R, W_KB // N_CHIPS), F32)),
        in_specs=[vmem, vmem, vmem], out_specs=(vmem, vmem, vmem),
        scratch_shapes=[pltpu.VMEM((N_CHIPS, 2, SHARD_PAD, HALF), BF16), pltpu.VMEM((N_CHIPS, 2, SHARD_OUT, HALF), BF16),
                        pltpu.VMEM((ACC_ROWS, D_MODEL), F32),
                        pltpu.SemaphoreType.DMA((15,)), pltpu.SemaphoreType.DMA((15,))],
        compiler_params=pltpu.CompilerParams(vmem_limit_bytes=VMEM_LIMIT),
    )(w_lin, w_out, wgu)


def _adamw(w, g, m, v):
    m = ADAM_B1 * m + (1.0 - ADAM_B1) * g
    v = ADAM_B2 * v + (1.0 - ADAM_B2) * (g * g)
    m_hat = m / (1.0 - ADAM_B1 ** ADAM_STEP)
    v_hat = v / (1.0 - ADAM_B2 ** ADAM_STEP)
    delta = -ADAM_LR * (m_hat / (jnp.sqrt(v_hat) + ADAM_EPS) + ADAM_WD * w)
    return delta, m, v


N_SMALL = 6


def _reduce_grads_call(g_in, g_out, small_grads, small_params):
    def body(gin_hbm, gout_hbm, g_lng, g_lnb, g_bg, g_nw, g_sink, g_wgu, loss_in, *rest):
        params = rest[:3 * N_SMALL]
        lin_in, fin_out, loss_out = rest[3 * N_SMALL:3 * N_SMALL + 3]
        small_out = rest[3 * N_SMALL + 3:7 * N_SMALL + 3]
        (a_in, a_out, b_in, b_out, c_in, s_in, s_out, r_in, r_out, f_in, f_out, pack_ref, tot_ref, pack_all,
         send_sems, recv_sems, local_sems) = rest[7 * N_SMALL + 3:]
        x, y, c, chips = _mesh_place()
        k_me = 2 * x + y
        me = 4 * x + 2 * y + c
        sibling = (x, y, 1 - c)

        pack_ref[...] = jnp.zeros_like(pack_ref)
        for a in range(8):
            pack_ref[P_LNG + a:P_LNG + a + 1, :] = g_lng[:, a * 128:(a + 1) * 128]
            pack_ref[P_LNB + a:P_LNB + a + 1, :] = g_lnb[:, a * 128:(a + 1) * 128]
        for a in range(2):
            pack_ref[P_BG + a:P_BG + a + 1, :] = g_bg[:, a * 128:(a + 1) * 128]
        pack_ref[P_NW:P_NW + 1, :] = g_nw[...]
        lane = lax.broadcasted_iota(jnp.int32, (1, 128), 1)
        sink_row = jnp.zeros((1, 128), F32)
        for hq in range(Q_HEADS):
            sink_row = jnp.where(lane == hq, g_sink[hq:hq + 1, :], sink_row)
        pack_ref[P_SINK:P_SINK + 1, :] = sink_row
        pack_ref[P_LOSS:P_LOSS + 1, :] = loss_in[...]
        gu_w = W_KB // N_CHIPS
        for k in range(N_CHIPS):
            pack_ref[P_GU + W_R * k:P_GU + W_R * (k + 1), 0:gu_w] = g_wgu[:, k * gu_w:(k + 1) * gu_w]
        pack_all[me] = pack_ref[...]
        small = []
        for mask in range(1, 8):
            peer = (x ^ (mask >> 2), y ^ ((mask >> 1) & 1), c ^ (mask & 1))
            small.append(pltpu.make_async_remote_copy(
                src_ref=pack_ref, dst_ref=pack_all.at[me], send_sem=send_sems.at[mask], recv_sem=recv_sems.at[mask],
                device_id=peer, device_id_type=MESH))
        for cp in small:
            cp.start()

        mine = [pltpu.make_async_copy(gin_hbm.at[c], a_in, local_sems.at[0]),
                pltpu.make_async_copy(gout_hbm.at[c], a_out, local_sems.at[1])]
        to_sib = [pltpu.make_async_remote_copy(
                      src_ref=gin_hbm.at[1 - c], dst_ref=b_in,
                      send_sem=send_sems.at[8], recv_sem=recv_sems.at[8], device_id=sibling, device_id_type=MESH),
                  pltpu.make_async_remote_copy(
                      src_ref=gout_hbm.at[1 - c], dst_ref=b_out,
                      send_sem=send_sems.at[9], recv_sem=recv_sems.at[9], device_id=sibling, device_id_type=MESH)]
        for cp in mine + to_sib:
            cp.start()
        for cp in mine:
            cp.wait()
        for cp in to_sib:
            cp.wait_recv()
        for k in range(N_CHIPS):
            rows = slice(k * SHARD_IN, k * SHARD_IN + SHARD_PAD)
            c_in[k] = a_in[rows, :] + b_in[rows, :]

        sent = []
        for r, chip in enumerate(chips):
            k_r = 2 * chip[0] + chip[1]
            s_in[r] = c_in[k_r].astype(BF16)
            s_out[r] = (a_out[k_r] + b_out[k_r]).astype(BF16)
            sent.append(pltpu.make_async_remote_copy(
                src_ref=s_in.at[r], dst_ref=r_in.at[r], send_sem=send_sems.at[10 + 2 * r],
                recv_sem=recv_sems.at[10 + 2 * r], device_id=(*chip, c), device_id_type=MESH))
            sent.append(pltpu.make_async_remote_copy(
                src_ref=s_out.at[r], dst_ref=r_out.at[r], send_sem=send_sems.at[11 + 2 * r],
                recv_sem=recv_sems.at[11 + 2 * r], device_id=(*chip, c), device_id_type=MESH))
            sent[-2].start()
            sent[-1].start()
        own_out = a_out[k_me] + b_out[k_me]
        for cp in sent:
            cp.wait_recv()
        f_in[c] = c_in[k_me] + r_in[0].astype(F32) + r_in[1].astype(F32) + r_in[2].astype(F32)
        f_out[c] = own_out + r_out[0].astype(F32) + r_out[1].astype(F32) + r_out[2].astype(F32)

        swap = [pltpu.make_async_remote_copy(
                    src_ref=f_in.at[c], dst_ref=f_in.at[c],
                    send_sem=send_sems.at[16], recv_sem=recv_sems.at[16], device_id=sibling, device_id_type=MESH),
                pltpu.make_async_remote_copy(
                    src_ref=f_out.at[c], dst_ref=f_out.at[c],
                    send_sem=send_sems.at[17], recv_sem=recv_sems.at[17], device_id=sibling, device_id_type=MESH)]
        for cp in swap:
            cp.start()

        for cp in small:
            cp.wait_recv()
        total = pack_all[0]
        for d in range(1, 8):
            total = total + pack_all[d]
        tot_ref[...] = total
        loss_out[...] = total[P_LOSS:P_LOSS + 1, :]
        g_outs = small_out[0:N_SMALL]
        for a in range(8):
            g_outs[0][:, a * 128:(a + 1) * 128] = total[P_LNG + a:P_LNG + a + 1, :]
            g_outs[1][:, a * 128:(a + 1) * 128] = total[P_LNB + a:P_LNB + a + 1, :]
        for a in range(2):
            g_outs[2][:, a * 128:(a + 1) * 128] = total[P_BG + a:P_BG + a + 1, :]
        g_outs[3][...] = total[P_NW:P_NW + 1, :]
        g_outs[4][...] = total[P_SINK:P_SINK + 1, 0:Q_HEADS]
        gu_rows = pl.ds(pl.multiple_of(P_GU + W_R * k_me, 8), W_R)
        g_outs[5][...] = tot_ref[gu_rows, 0:gu_w]
        for n in range(N_SMALL):
            w_ref, m_ref, v_ref = params[3 * n:3 * n + 3]
            delta, new_m, new_v = _adamw(w_ref[...], g_outs[n][...], m_ref[...], v_ref[...])
            small_out[N_SMALL + n][...] = delta
            small_out[2 * N_SMALL + n][...] = new_m
            small_out[3 * N_SMALL + n][...] = new_v

        other_in = pltpu.make_async_remote_copy(
            src_ref=f_in.at[1 - c], dst_ref=f_in.at[1 - c],
            send_sem=send_sems.at[16], recv_sem=recv_sems.at[16], device_id=sibling, device_id_type=MESH)
        other_out = pltpu.make_async_remote_copy(
            src_ref=f_out.at[1 - c], dst_ref=f_out.at[1 - c],
            send_sem=send_sems.at[17], recv_sem=recv_sems.at[17], device_id=sibling, device_id_type=MESH)
        other_in.wait_recv()
        other_out.wait_recv()
        for cp in small + to_sib + sent + swap:
            cp.wait_send()

        for h in range(2):
            lin_in[:, h * HALF:(h + 1) * HALF] = f_in[h, 0:SHARD_IN, :]
            fin_out[:, h * HALF:(h + 1) * HALF] = f_out[h]

    vmem = pl.BlockSpec(memory_space=pltpu.VMEM)
    hbm = pl.BlockSpec(memory_space=pl.ANY)
    small_shapes = [jax.ShapeDtypeStruct(p.shape, F32) for p in small_params[0::3]]
    return pl.pallas_call(
        body, name="reduce_grads",
        out_shape=(jax.ShapeDtypeStruct((SHARD_IN, D_MODEL), F32), jax.ShapeDtypeStruct((SHARD_OUT, D_MODEL), F32),
                   jax.ShapeDtypeStruct((1, 128), F32), *(small_shapes * 4)),
        in_specs=[hbm, hbm] + [vmem] * (7 + 3 * N_SMALL), out_specs=(vmem,) * (3 + 4 * N_SMALL),
        scratch_shapes=[
            pltpu.VMEM((ACC_ROWS, HALF), F32), pltpu.VMEM((N_CHIPS, SHARD_OUT, HALF), F32),
            pltpu.VMEM((ACC_ROWS, HALF), F32), pltpu.VMEM((N_CHIPS, SHARD_OUT, HALF), F32),
            pltpu.VMEM((N_CHIPS, SHARD_PAD, HALF), F32),
            pltpu.VMEM((3, SHARD_PAD, HALF), BF16), pltpu.VMEM((3, SHARD_OUT, HALF), BF16),
            pltpu.VMEM((3, SHARD_PAD, HALF), BF16), pltpu.VMEM((3, SHARD_OUT, HALF), BF16),
            pltpu.VMEM((2, SHARD_PAD, HALF), F32), pltpu.VMEM((2, SHARD_OUT, HALF), F32),
            pltpu.VMEM((PACK_ROWS, 128), F32), pltpu.VMEM((PACK_ROWS, 128), F32), pltpu.VMEM((8, PACK_ROWS, 128), F32),
            pltpu.SemaphoreType.DMA((18,)), pltpu.SemaphoreType.DMA((18,)), pltpu.SemaphoreType.DMA((2,)),
        ],
        compiler_params=pltpu.CompilerParams(vmem_limit_bytes=VMEM_LIMIT),
    )(g_in, g_out, *small_grads, *small_params)


def _adamw_call(g_in, w_in, m_in, v_in, g_out, w_out, m_out, v_out):
    steps = 4
    rows_out = SHARD_OUT // steps
    cols = D_MODEL // steps

    def body(gi, wi, mi, vi, go, wo, mo, vo, gi_o, di, nmi, nvi, go_o, do, nmo, nvo):
        g = gi[...]
        delta, new_m, new_v = _adamw(wi[:, 0, :], g, mi[:, 0, :], vi[:, 0, :])
        gi_o[:, 0, :] = g
        di[:, 0, :] = delta
        nmi[:, 0, :] = new_m
        nvi[:, 0, :] = new_v
        g = go[...]
        go_o[...] = g
        do[...], nmo[...], nvo[...] = _adamw(wo[...], g, mo[...], vo[...])

    t_g = pl.BlockSpec((SHARD_IN, cols), lambda i: (0, i))
    t_in = pl.BlockSpec((SHARD_IN, 1, cols), lambda i: (0, 0, i))
    t_out = pl.BlockSpec((rows_out, D_MODEL), lambda i: (i, 0))
    s_in = jax.ShapeDtypeStruct((SHARD_IN, 1, D_MODEL), F32)
    s_out = jax.ShapeDtypeStruct((SHARD_OUT, D_MODEL), F32)
    return pl.pallas_call(
        body, name="adamw", grid=(steps,), in_specs=[t_g] + [t_in] * 3 + [t_out] * 4,
        out_specs=(t_in,) * 4 + (t_out,) * 4, out_shape=(s_in,) * 4 + (s_out,) * 4,
        compiler_params=pltpu.CompilerParams(dimension_semantics=("arbitrary",)),
    )(g_in, w_in, m_in, v_in, g_out, w_out, m_out, v_out)


def _rope_tables(positions):
    half = 8
    inv_freq = 500000.0 ** (-jnp.arange(half, dtype=F32) / half)
    d = np.arange(128) % HEAD_A
    rotated = d < 2 * half
    freq = jnp.where(rotated, inv_freq[d % half], 0.0)
    ang = positions.astype(F32)[:, None] * freq[None, :]
    sin = jnp.sin(ang)
    sa = sin * np.where(d < half, -1.0, 0.0).astype(np.float32)
    sb = sin * np.where(rotated & (d >= half), 1.0, 0.0).astype(np.float32)
    return jnp.cos(ang), sa, sb


def kernel(x, positions, w_in, gla_w_gate_up, gla_b_gate, attn_sinks, gla_norm_w, w_out, ln_g, ln_b, loss_target, m_w_in, m_gla_w_gate_up, m_gla_b_gate, m_attn_sinks, m_gla_norm_w, m_w_out, m_ln_g, m_ln_b, v_w_in, v_gla_w_gate_up, v_gla_b_gate, v_attn_sinks, v_gla_norm_w, v_w_out, v_ln_g, v_ln_b):
    def lin(w):
        return jnp.transpose(w[0])

    def lin3(w):
        return jnp.transpose(w, (2, 0, 1))

    def unlin(w):
        return jnp.transpose(w, (1, 2, 0))

    win, wout, wgu_all = _gather_weights_call(lin(w_in), w_out[0], gla_w_gate_up[0])
    wgu = jnp.transpose(wgu_all, (1, 0, 2)).reshape(W_R, W_KB)
    cos, sa, sb = _rope_tables(positions[0])
    sinks = attn_sinks[0]

    (qa, ka, va, qb, kb, vb, r, dattn, dga, dob, dgb, dh, st, g_wout, g_lng, g_lnb, g_nw, loss) = _fwd_call(
        x[0], loss_target[0], cos, sa, sb, win, wout, wgu, gla_b_gate, sinks, gla_norm_w, ln_g, ln_b)
    gx, g_win, g_sink, g_bg, g_wgu = _bwd_call(
        x[0], dh, qa, ka, va, dattn, dga, dob, dgb, qb, kb, vb, r, st, cos, sa, sb, win, wgu, gla_b_gate, sinks)

    g_wout_by_chip = g_wout.reshape(2, N_CHIPS, SHARD_OUT, HALF)
    small_params = []
    for group in ((ln_g, m_ln_g, v_ln_g), (ln_b, m_ln_b, v_ln_b), (gla_b_gate, m_gla_b_gate, v_gla_b_gate),
                  (gla_norm_w, m_gla_norm_w, v_gla_norm_w), (attn_sinks, m_attn_sinks, v_attn_sinks)):
        small_params += list(group)
    small_params += [gla_w_gate_up[0], m_gla_w_gate_up[0], v_gla_w_gate_up[0]]
    fin_in, fin_out, loss_sum, *small_out = _reduce_grads_call(
        g_win, g_wout_by_chip, (g_lng, g_lnb, g_bg, g_nw, g_sink, g_wgu, loss), small_params)
    fin_in, d_in, nm_in, nv_in, fin_out, d_out, nm_out, nv_out = _adamw_call(
        fin_in, lin3(w_in), lin3(m_w_in), lin3(v_w_in), fin_out, w_out[0], m_w_out[0], v_w_out[0])

    def unpack(kind, big_in, big_out):
        lng_, lnb_, bg_, nw_, sink_, gu_ = small_out[kind * N_SMALL:(kind + 1) * N_SMALL]
        return (unlin(big_in), gu_[None], bg_, sink_, nw_, big_out[None], lng_, lnb_)

    loss_total = loss_sum[0, 0]
    g_s, d_s, nm_s, nv_s = 0, 1, 2, 3
    return (loss_total, gx[None], *unpack(g_s, fin_in, fin_out), *unpack(d_s, d_in, d_out),
            *unpack(nm_s, nm_in, nm_out), *unpack(nv_s, nv_in, nv_out))
```

```python
import functools

import jax
import jax.numpy as jnp
import numpy as np
from jax import lax
from jax.experimental import pallas as pl
from jax.experimental.pallas import tpu as pltpu

F32 = jnp.float32
BF16 = jnp.bfloat16
MESH = pl.DeviceIdType.MESH

D_MODEL = 1024
N_CHIPS = 4
W_QA, W_KA, W_VA, W_GA, W_QB, W_KB, W_VB, W_GB, W_R = 512, 128, 128, 512, 256, 256, 512, 512, 16
O_QA = 0
O_KA = O_QA + W_QA
O_VA = O_KA + W_KA
O_GA = O_VA + W_VA
O_QB = O_GA + W_GA
O_KB = O_QB + W_QB
O_VB = O_KB + W_KB
O_GB = O_VB + W_VB
O_R = O_GB + W_GB
D_PROJ = O_R + W_R
SHARD_IN = D_PROJ // N_CHIPS
SHARD_OUT = D_MODEL // N_CHIPS
SHARD_PAD = 720
ACC_ROWS = -(-((N_CHIPS - 1) * SHARD_IN + SHARD_PAD) // 8) * 8
HALF = D_MODEL // 2

HEAD_A = 64
Q_HEADS = 8
KV_HEADS = 2
GROUP = 4
BLOCK = 128
GLA_HEADS = 4
GLA_DK = 64
GLA_DV = 128
CHUNK = 64
GLA_TAU = 16.0
EPS = 1e-5
ALPHA = 2.0 ** 0.25
ATT_SCALE = HEAD_A ** -0.5
GLA_SCALE = GLA_DK ** -0.5

ADAM_LR = 0.001
ADAM_B1 = 0.9
ADAM_B2 = 0.999
ADAM_EPS = 1e-08
ADAM_WD = 0.01
ADAM_STEP = 10

TM = 256
TRI_SLAB = 128
VMEM_LIMIT = 56 * 1024 * 1024

P_LNG, P_LNB, P_BG, P_NW, P_SINK, P_LOSS, P_GU = 0, 8, 16, 18, 19, 20, 24
PACK_ROWS = P_GU + N_CHIPS * 16
PACK_OWN_ROWS = P_GU + 16


def _mm(a, b):
    return jnp.dot(a, b, preferred_element_type=F32)


def _mm_nt(a, b):
    return lax.dot_general(a, b, (((1,), (1,)), ((), ())), preferred_element_type=F32)


def _mm_tn(a, b):
    return lax.dot_general(a, b, (((0,), (0,)), ((), ())), preferred_element_type=F32)


def _split3(a):
    hi = a.astype(BF16)
    r1 = a - hi.astype(F32)
    mid = r1.astype(BF16)
    lo = (r1 - mid.astype(F32)).astype(BF16)
    return hi, mid, lo


def _tri_mm(tri, a):
    slab = tri.shape[0]
    hi, mid, lo = _split3(a)
    return jnp.concatenate(
        [_mm(tri, hi[s:s + slab]) + _mm(tri, mid[s:s + slab]) + _mm(tri, lo[s:s + slab])
         for s in range(0, a.shape[0], slab)], axis=0)


def _chunk_tri(n, upper):
    r = lax.broadcasted_iota(jnp.int32, (n, n), 0)
    c = lax.broadcasted_iota(jnp.int32, (n, n), 1)
    same = (r >> 6) == (c >> 6)
    order = (c >= r) if upper else (c <= r)
    return jnp.where(same & order, 1.0, 0.0).astype(BF16)


def _rope(t, cos, sa, sb):
    w = t.shape[1]
    return t * cos + pltpu.roll(t, w - 8, 1) * sa + pltpu.roll(t, 8, 1) * sb


def _rope_tile(cs):
    row = lax.broadcasted_iota(jnp.int32, (16, 128), 0)
    d = lax.broadcasted_iota(jnp.int32, (16, 128), 1) & (HEAD_A - 1)
    hit = (d & 7) == (row & 7)
    is_cos = row < 8
    lo = d < 8
    hi = (d >= 8) & (d < 16)
    pick_cos = jnp.where(hit & is_cos & (lo | hi), 1.0, 0.0).astype(BF16)
    pick_sa = jnp.where(hit & ~is_cos & lo, -1.0, 0.0).astype(BF16)
    pick_sb = jnp.where(hit & ~is_cos & hi, 1.0, 0.0).astype(BF16)
    pieces = _split3(cs)

    def spread(pick):
        return _mm_tn(pieces[0], pick) + _mm_tn(pieces[1], pick) + _mm_tn(pieces[2], pick)

    d1 = lax.broadcasted_iota(jnp.int32, (1, 128), 1) & (HEAD_A - 1)
    return spread(pick_cos) + jnp.where(d1 < 16, 0.0, 1.0), spread(pick_sa), spread(pick_sb)


def _rope_bwd(d, cos, sa, sb):
    w = d.shape[1]
    return d * cos + pltpu.roll(d * sa, 8, 1) + pltpu.roll(d * sb, w - 8, 1)


def _log_sigmoid(z):
    return jnp.minimum(z, 0.0) - jnp.log1p(jnp.exp(-jnp.abs(z)))


def _sigmoid(z):
    return 1.0 / (1.0 + jnp.exp(-z))


def _attn_mask(has_prev):
    r = lax.broadcasted_iota(jnp.int32, (GROUP * BLOCK, 2 * BLOCK), 0) & (BLOCK - 1)
    k = lax.broadcasted_iota(jnp.int32, (GROUP * BLOCK, 2 * BLOCK), 1)
    first_key = jnp.where(has_prev, 0, BLOCK)
    return (k > r) & (k <= r + BLOCK) & (k >= first_key)


def _sink_col(sinks_ref, j):
    r = lax.broadcasted_iota(jnp.int32, (GROUP * BLOCK, 1), 0) >> 7
    col = jnp.full((GROUP * BLOCK, 1), sinks_ref[GROUP * j], F32)
    for g in range(1, GROUP):
        col = jnp.where(r == g, sinks_ref[GROUP * j + g], col)
    return col


def _stack_heads(t, j):
    return jnp.concatenate([t[:, (GROUP * j + g) * HEAD_A:(GROUP * j + g + 1) * HEAD_A] for g in range(GROUP)], axis=0)


def _unstack_heads(parts):
    return jnp.concatenate([parts[j][g * BLOCK:(g + 1) * BLOCK] for j in range(KV_HEADS) for g in range(GROUP)], axis=1)


def _softmax_block(qs, kc, mask, sink):
    s = _mm_nt(qs, kc) * ATT_SCALE
    s = jnp.where(mask, s, -jnp.inf)
    m = jnp.maximum(jnp.max(s, axis=1, keepdims=True), sink)
    p = jnp.exp(s - m)
    e_sink = jnp.exp(sink - m)
    inv = 1.0 / (jnp.sum(p, axis=1, keepdims=True) + e_sink)
    return p * inv, e_sink * inv


def _fwd_call(x, tgt, cs, win, wout, wgu, bg, sinks, nw, lng, lnb):
    s_len = x.shape[0]
    nt = s_len // TM
    nblk = TM // BLOCK
    nch = TM // CHUNK

    def body(x_ref, t_ref, cs_ref, win_ref, wout_ref, wgu_ref, bg_ref, sinks_ref, nw_ref,
             lng_ref, lnb_ref,
             qa_ref, ka_ref, va_ref, qb_ref, kb_ref, vb_ref, r_ref, dattn_ref, dga_ref, dob_ref, dgb_ref, dh_ref,
             st_ref, dwout_ref, glng_ref, glnb_ref, gnw_ref, loss_ref,
             kprev, vprev, state, attn_s, ga_s, ob_s, gb_s, cat_s):
        i = pl.program_id(0)

        @pl.when(i == 0)
        def _():
            kprev[...] = jnp.zeros_like(kprev)
            vprev[...] = jnp.zeros_like(vprev)
            state[...] = jnp.zeros_like(state)
            dwout_ref[...] = jnp.zeros_like(dwout_ref)
            glng_ref[...] = jnp.zeros_like(glng_ref)
            glnb_ref[...] = jnp.zeros_like(glnb_ref)
            gnw_ref[...] = jnp.zeros_like(gnw_ref)
            loss_ref[...] = jnp.zeros_like(loss_ref)

        x = x_ref[...]
        xb = x.astype(BF16)

        def proj(off, width):
            return _mm_nt(xb, win_ref[off:off + width, :])

        cos, sa, sb = _rope_tile(cs_ref[...])
        cos4, sa4, sb4 = (jnp.concatenate([t] * 4, axis=1) for t in (cos, sa, sb))
        qa = _rope(proj(O_QA, W_QA), cos4, sa4, sb4).astype(BF16)
        ka = _rope(proj(O_KA, W_KA), cos, sa, sb).astype(BF16)
        va = proj(O_VA, W_VA).astype(BF16)
        qa_ref[...] = qa
        ka_ref[...] = ka
        va_ref[...] = va
        ga_s[...] = proj(O_GA, W_GA)
        gb_s[...] = proj(O_GB, W_GB)

        for b in range(nblk):
            rows = slice(b * BLOCK, (b + 1) * BLOCK)
            has_prev = (i * nblk + b) > 0
            mask = _attn_mask(has_prev)
            k_cur = ka[rows]
            v_cur = va[rows]
            k_old = kprev[...] if b == 0 else ka[(b - 1) * BLOCK:b * BLOCK]
            v_old = vprev[...] if b == 0 else va[(b - 1) * BLOCK:b * BLOCK]
            outs = []
            for j in range(KV_HEADS):
                hs = slice(j * HEAD_A, (j + 1) * HEAD_A)
                kc = jnp.concatenate([k_old[:, hs], k_cur[:, hs]], axis=0)
                vc = jnp.concatenate([v_old[:, hs], v_cur[:, hs]], axis=0)
                probs, _ = _softmax_block(_stack_heads(qa[rows], j), kc, mask, _sink_col(sinks_ref, j))
                outs.append(_mm(probs.astype(BF16), vc))
            attn_s[rows, :] = _unstack_heads(outs)
        kprev[...] = ka[(nblk - 1) * BLOCK:]
        vprev[...] = va[(nblk - 1) * BLOCK:]

        r = proj(O_R, W_R)
        r_ref[...] = r
        z = _mm(r.astype(BF16), wgu_ref[...].astype(BF16)) + bg_ref[...]
        log_a = _log_sigmoid(z) / GLA_TAU
        bcum = _tri_mm(_chunk_tri(TRI_SLAB, False), log_a)
        qb = proj(O_QB, W_QB)
        kb = proj(O_KB, W_KB)
        vb = proj(O_VB, W_VB).astype(BF16)
        qb_ref[...] = qb
        kb_ref[...] = kb
        vb_ref[...] = vb
        qd_all = (qb * GLA_SCALE * jnp.exp(bcum)).astype(BF16)
        ki_all = (kb * jnp.exp(-bcum)).astype(BF16)
        tril = lax.broadcasted_iota(jnp.int32, (CHUNK, CHUNK), 0) >= lax.broadcasted_iota(jnp.int32, (CHUNK, CHUNK), 1)
        for c in range(nch):
            rows = slice(c * CHUNK, (c + 1) * CHUNK)
            b_c = bcum[rows]
            b_last = b_c[CHUNK - 1:CHUNK]
            ke = (kb[rows] * jnp.exp(b_last - b_c)).astype(BF16)
            st = state[...]
            st_ref[c] = st
            st16 = st.astype(BF16)
            o_parts, u_parts = [], []
            for h in range(GLA_HEADS):
                ks = slice(h * GLA_DK, (h + 1) * GLA_DK)
                vs = slice(h * GLA_DV, (h + 1) * GLA_DV)
                qd = qd_all[rows, ks]
                v_h = vb[rows, vs]
                a = jnp.where(tril, _mm_nt(qd, ki_all[rows, ks]), 0.0)
                o_parts.append(_mm(a.astype(BF16), v_h) + _mm_nt(qd, st16[:, ks]))
                u_parts.append(_mm_tn(v_h, ke[:, ks]))
            ob_s[rows, :] = jnp.concatenate(o_parts, axis=1)
            state[...] = st * jnp.exp(b_last) + jnp.concatenate(u_parts, axis=1)

        ga = ga_s[...]
        sg_a = _sigmoid(ga)
        silu_a = ga * sg_a
        attn = attn_s[...]
        cat_s[:, :W_GA] = (attn * silu_a).astype(BF16)
        gb = gb_s[...]
        sg_b = _sigmoid(gb)
        silu_b = gb * sg_b
        nw = nw_ref[...]
        on_parts = []
        for h in range(GLA_HEADS):
            vs = slice(h * GLA_DV, (h + 1) * GLA_DV)
            o_h = ob_s[:, vs]
            rs = lax.rsqrt(jnp.mean(o_h * o_h, axis=1, keepdims=True) + EPS)
            on_parts.append(o_h * rs * nw)
        on = jnp.concatenate(on_parts, axis=1)
        cat_s[:, W_GA:] = (on * silu_b).astype(BF16)
        cat = cat_s[...]
        hres = ALPHA * x + _mm(cat, wout_ref[...])
        mu = jnp.mean(hres, axis=1, keepdims=True)
        hc = hres - mu
        rstd = lax.rsqrt(jnp.mean(hc * hc, axis=1, keepdims=True) + EPS)
        xhat = hc * rstd
        g_ln = lng_ref[...]
        err = xhat * g_ln + lnb_ref[...] - t_ref[...]
        loss_ref[...] += jnp.sum(err * err) * (0.5 / D_MODEL)
        dy = err * (1.0 / D_MODEL)
        glng_ref[...] += jnp.sum(dy * xhat, axis=0, keepdims=True)
        glnb_ref[...] += jnp.sum(dy, axis=0, keepdims=True)
        dxh = dy * g_ln
        dh = rstd * (dxh - jnp.mean(dxh, axis=1, keepdims=True) - xhat * jnp.mean(dxh * xhat, axis=1, keepdims=True))
        dh_ref[...] = dh
        dh16 = dh.astype(BF16)
        for h in range(2):
            dwout_ref[h] += _mm_tn(cat, dh16[:, h * HALF:(h + 1) * HALF])
        dcat = _mm_nt(dh16, wout_ref[...])

        d_a = dcat[:, :W_GA]
        dattn_ref[...] = (d_a * silu_a).astype(BF16)
        dga_ref[...] = (d_a * attn * (sg_a * (1.0 + ga * (1.0 - sg_a)))).astype(BF16)
        d_b = dcat[:, W_GA:]
        dgb_ref[...] = (d_b * on * (sg_b * (1.0 + gb * (1.0 - sg_b)))).astype(BF16)
        d_on = d_b * silu_b
        gnw = jnp.zeros((1, GLA_DV), F32)
        do_parts = []
        for h in range(GLA_HEADS):
            vs = slice(h * GLA_DV, (h + 1) * GLA_DV)
            o_h = ob_s[:, vs]
            rs = lax.rsqrt(jnp.mean(o_h * o_h, axis=1, keepdims=True) + EPS)
            d_on_h = d_on[:, vs]
            gnw = gnw + jnp.sum(d_on_h * o_h * rs, axis=0, keepdims=True)
            gg = d_on_h * nw
            do_parts.append(rs * gg - o_h * (rs * rs * rs) * jnp.mean(gg * o_h, axis=1, keepdims=True))
        gnw_ref[...] += gnw
        dob_ref[...] = jnp.concatenate(do_parts, axis=1).astype(BF16)

    tile = lambda w: pl.BlockSpec((TM, w), lambda i: (i, 0))
    whole = lambda shape: pl.BlockSpec(shape, lambda i: tuple(0 for _ in shape), pipeline_mode=pl.Buffered(1))
    out_shape = (
        jax.ShapeDtypeStruct((s_len, W_QA), BF16),
        jax.ShapeDtypeStruct((s_len, W_KA), BF16),
        jax.ShapeDtypeStruct((s_len, W_VA), BF16),
        jax.ShapeDtypeStruct((s_len, W_QB), F32),
        jax.ShapeDtypeStruct((s_len, W_KB), F32),
        jax.ShapeDtypeStruct((s_len, W_VB), BF16),
        jax.ShapeDtypeStruct((s_len, W_R), F32),
        jax.ShapeDtypeStruct((s_len, W_GA), BF16),
        jax.ShapeDtypeStruct((s_len, W_GA), BF16),
        jax.ShapeDtypeStruct((s_len, W_GB), BF16),
        jax.ShapeDtypeStruct((s_len, W_GB), BF16),
        jax.ShapeDtypeStruct((s_len, D_MODEL), F32),
        jax.ShapeDtypeStruct((s_len // CHUNK, GLA_DV, GLA_HEADS * GLA_DK), F32),
        jax.ShapeDtypeStruct((2, D_MODEL, HALF), F32),
        jax.ShapeDtypeStruct((1, D_MODEL), F32),
        jax.ShapeDtypeStruct((1, D_MODEL), F32),
        jax.ShapeDtypeStruct((1, GLA_DV), F32),
        jax.ShapeDtypeStruct((1, 128), F32),
    )
    out_specs = (
        tile(W_QA), tile(W_KA), tile(W_VA), tile(W_QB), tile(W_KB), tile(W_VB), tile(W_R),
        tile(W_GA), tile(W_GA), tile(W_GB), tile(W_GB), tile(D_MODEL),
        pl.BlockSpec((nch, GLA_DV, GLA_HEADS * GLA_DK), lambda i: (i, 0, 0)),
        whole((2, D_MODEL, HALF)), whole((1, D_MODEL)), whole((1, D_MODEL)), whole((1, GLA_DV)), whole((1, 128)),
    )
    in_specs = [
        tile(D_MODEL), tile(D_MODEL), pl.BlockSpec((16, TM), lambda i: (0, i)),
        whole((D_PROJ, D_MODEL)), whole((D_MODEL, D_MODEL)), whole((W_R, W_KB)), whole((1, W_KB)),
        pl.BlockSpec(memory_space=pltpu.SMEM), whole((1, GLA_DV)), whole((1, D_MODEL)), whole((1, D_MODEL)),
    ]
    scratch = [
        pltpu.VMEM((BLOCK, W_KA), BF16), pltpu.VMEM((BLOCK, W_VA), BF16),
        pltpu.VMEM((GLA_DV, GLA_HEADS * GLA_DK), F32),
        pltpu.VMEM((TM, W_GA), F32), pltpu.VMEM((TM, W_GA), F32), pltpu.VMEM((TM, W_GB), F32),
        pltpu.VMEM((TM, W_GB), F32), pltpu.VMEM((TM, D_MODEL), BF16),
    ]
    return pl.pallas_call(
        body, name="fwd_head", grid=(nt,), in_specs=in_specs, out_specs=out_specs, out_shape=out_shape,
        scratch_shapes=scratch,
        compiler_params=pltpu.CompilerParams(dimension_semantics=("arbitrary",), vmem_limit_bytes=VMEM_LIMIT),
    )(x, tgt, cs, win, wout, wgu, bg, sinks, nw, lng, lnb)


def _bwd_call(x, dh, qa, ka, va, dattn, dga, dob, dgb, qb, kb, vb, r, st, cs, win, wgu, bg, sinks):
    s_len = x.shape[0]
    nt = s_len // TM
    nblk = TM // BLOCK
    nch = TM // CHUNK

    def body(x_ref, dh_ref, qa_ref, ka_ref, va_ref, kap_ref, vap_ref, dattn_ref, dga_ref, dob_ref, dgb_ref,
             qb_ref, kb_ref, vb_ref, r_ref, st_ref, cs_ref, win_ref, wgu_ref, bg_ref, sinks_ref,
             gx_ref, dwin_ref, gsink_ref, gbg_ref, gwgu_ref,
             dproj, dk_carry, dv_carry, ds_carry, db_s):
        i = pl.program_id(0)
        t = nt - 1 - i

        @pl.when(i == 0)
        def _():
            dk_carry[...] = jnp.zeros_like(dk_carry)
            dv_carry[...] = jnp.zeros_like(dv_carry)
            ds_carry[...] = jnp.zeros_like(ds_carry)
            dwin_ref[...] = jnp.zeros_like(dwin_ref)
            gsink_ref[...] = jnp.zeros_like(gsink_ref)
            gbg_ref[...] = jnp.zeros_like(gbg_ref)
            gwgu_ref[...] = jnp.zeros_like(gwgu_ref)

        cos, sa, sb = _rope_tile(cs_ref[...])
        cos4, sa4, sb4 = (jnp.concatenate([v] * 4, axis=1) for v in (cos, sa, sb))

        qa = qa_ref[...]
        ka = ka_ref[...]
        va = va_ref[...]
        dattn = dattn_ref[...]
        gsink_rows = [jnp.zeros((1, 1), F32) for _ in range(Q_HEADS)]
        for b in reversed(range(nblk)):
            rows = slice(b * BLOCK, (b + 1) * BLOCK)
            has_prev = (t * nblk + b) > 0
            mask = _attn_mask(has_prev)
            k_cur = ka[rows]
            v_cur = va[rows]
            k_old = kap_ref[...] if b == 0 else ka[(b - 1) * BLOCK:b * BLOCK]
            v_old = vap_ref[...] if b == 0 else va[(b - 1) * BLOCK:b * BLOCK]
            dq_parts, dk_parts, dv_parts = [], [], []
            for j in range(KV_HEADS):
                hs = slice(j * HEAD_A, (j + 1) * HEAD_A)
                kc = jnp.concatenate([k_old[:, hs], k_cur[:, hs]], axis=0)
                vc = jnp.concatenate([v_old[:, hs], v_cur[:, hs]], axis=0)
                qs = _stack_heads(qa[rows], j)
                do_s = _stack_heads(dattn[rows], j)
                probs, p_sink = _softmax_block(qs, kc, mask, _sink_col(sinks_ref, j))
                dp = _mm_nt(do_s, vc)
                d_row = jnp.sum(probs * dp, axis=1, keepdims=True)
                ds16 = (probs * (dp - d_row) * ATT_SCALE).astype(BF16)
                dq_parts.append(_mm(ds16, kc))
                dk_parts.append(_mm_tn(ds16, qs))
                dv_parts.append(_mm_tn(probs.astype(BF16), do_s))
                t_sink = d_row * p_sink
                for g in range(GROUP):
                    gsink_rows[GROUP * j + g] = gsink_rows[GROUP * j + g] - jnp.sum(
                        t_sink[g * BLOCK:(g + 1) * BLOCK], axis=0, keepdims=True)
            dq = _rope_bwd(_unstack_heads(dq_parts), cos4[rows], sa4[rows], sb4[rows])
            dproj[rows, O_QA:O_QA + W_QA] = dq.astype(BF16)
            dk_cur = dk_carry[...] + jnp.concatenate([p[BLOCK:] for p in dk_parts], axis=1)
            dv_cur = dv_carry[...] + jnp.concatenate([p[BLOCK:] for p in dv_parts], axis=1)
            dproj[rows, O_KA:O_KA + W_KA] = _rope_bwd(dk_cur, cos[rows], sa[rows], sb[rows]).astype(BF16)
            dproj[rows, O_VA:O_VA + W_VA] = dv_cur.astype(BF16)
            dk_carry[...] = jnp.concatenate([p[:BLOCK] for p in dk_parts], axis=1)
            dv_carry[...] = jnp.concatenate([p[:BLOCK] for p in dv_parts], axis=1)
        for hq in range(Q_HEADS):
            gsink_ref[hq:hq + 1, :] += jnp.broadcast_to(gsink_rows[hq], (1, 128))

        dproj[:, O_GA:O_GA + W_GA] = dga_ref[...]
        dproj[:, O_GB:O_GB + W_GB] = dgb_ref[...]

        r16 = r_ref[...].astype(BF16)
        wgu16 = wgu_ref[...].astype(BF16)
        z = _mm(r16, wgu16) + bg_ref[...]
        log_a = _log_sigmoid(z) / GLA_TAU
        bcum = _tri_mm(_chunk_tri(TRI_SLAB, False), log_a)
        qb = qb_ref[...]
        kb = kb_ref[...]
        vb = vb_ref[...]
        dob = dob_ref[...]
        e_b = jnp.exp(bcum)
        e_nb = jnp.exp(-bcum)
        qd_f = qb * GLA_SCALE * e_b
        ki_f = kb * e_nb
        qd_all = qd_f.astype(BF16)
        ki_all = ki_f.astype(BF16)
        tril = lax.broadcasted_iota(jnp.int32, (CHUNK, CHUNK), 0) >= lax.broadcasted_iota(jnp.int32, (CHUNK, CHUNK), 1)
        last_row = lax.broadcasted_iota(jnp.int32, (CHUNK, 1), 0) == CHUNK - 1
        for c in reversed(range(nch)):
            rows = slice(c * CHUNK, (c + 1) * CHUNK)
            b_c = bcum[rows]
            b_last = b_c[CHUNK - 1:CHUNK]
            e_e = jnp.exp(b_last - b_c)
            dec = jnp.exp(b_last)
            ke_f = kb[rows] * e_e
            ke = ke_f.astype(BF16)
            sp = st_ref[c]
            sp16 = sp.astype(BF16)
            dsn = ds_carry[...]
            dsn16 = dsn.astype(BF16)
            dqd_p, dki_p, dke_p, dv_p, dsp_p = [], [], [], [], []
            for h in range(GLA_HEADS):
                ks = slice(h * GLA_DK, (h + 1) * GLA_DK)
                vs = slice(h * GLA_DV, (h + 1) * GLA_DV)
                qd = qd_all[rows, ks]
                ki = ki_all[rows, ks]
                v_h = vb[rows, vs]
                do_h = dob[rows, vs]
                a16 = jnp.where(tril, _mm_nt(qd, ki), 0.0).astype(BF16)
                da16 = jnp.where(tril, _mm_nt(do_h, v_h), 0.0).astype(BF16)
                dv_p.append(_mm_tn(a16, do_h) + _mm_nt(ke[:, ks], dsn16[:, ks]))
                dqd_p.append(_mm(da16, ki) + _mm(do_h, sp16[:, ks]))
                dki_p.append(_mm_tn(da16, qd))
                dke_p.append(_mm(v_h, dsn16[:, ks]))
                dsp_p.append(_mm_tn(do_h, qd))
            dqd = jnp.concatenate(dqd_p, axis=1)
            dki = jnp.concatenate(dki_p, axis=1)
            dke = jnp.concatenate(dke_p, axis=1)
            ddec = jnp.sum(dsn * sp, axis=0, keepdims=True)
            ds_carry[...] = dsn * dec + jnp.concatenate(dsp_p, axis=1)
            dproj[rows, O_QB:O_QB + W_QB] = (dqd * e_b[rows] * GLA_SCALE).astype(BF16)
            dproj[rows, O_KB:O_KB + W_KB] = (dki * e_nb[rows] + dke * e_e).astype(BF16)
            dproj[rows, O_VB:O_VB + W_VB] = jnp.concatenate(dv_p, axis=1).astype(BF16)
            dke_ke = dke * ke_f
            d_b = dqd * qd_f[rows] - dki * ki_f[rows] - dke_ke
            d_bl = jnp.sum(dke_ke, axis=0, keepdims=True) + ddec * dec
            db_s[rows, :] = d_b + jnp.where(last_row, d_bl, 0.0)
        dlog_a = _tri_mm(_chunk_tri(TRI_SLAB, True), db_s[...])
        dz = dlog_a * (1.0 / GLA_TAU) * _sigmoid(-z)
        dz16 = dz.astype(BF16)
        gbg_ref[...] += jnp.sum(dz, axis=0, keepdims=True)
        gwgu_ref[...] += _mm_tn(r16, dz16)
        dproj[:, O_R:O_R + W_R] = _mm_nt(dz16, wgu16).astype(BF16)

        dp16 = dproj[...]
        gx_ref[...] = ALPHA * dh_ref[...] + _mm(dp16, win_ref[...])
        x16 = x_ref[...].astype(BF16)
        for h in range(2):
            dwin_ref[h, 0:D_PROJ, :] += _mm_tn(dp16, x16[:, h * HALF:(h + 1) * HALF])

    tile = lambda w: pl.BlockSpec((TM, w), lambda i: (nt - 1 - i, 0))
    whole = lambda shape: pl.BlockSpec(shape, lambda i: tuple(0 for _ in shape), pipeline_mode=pl.Buffered(1))
    prev_blk = pl.BlockSpec((BLOCK, W_KA), lambda i: (jnp.maximum((nt - 1 - i) * nblk - 1, 0), 0))
    in_specs = [
        tile(D_MODEL), tile(D_MODEL), tile(W_QA), tile(W_KA), tile(W_VA), prev_blk, prev_blk,
        tile(W_GA), tile(W_GA), tile(W_GB), tile(W_GB), tile(W_QB), tile(W_KB), tile(W_VB), tile(W_R),
        pl.BlockSpec((nch, GLA_DV, GLA_HEADS * GLA_DK), lambda i: (nt - 1 - i, 0, 0)),
        pl.BlockSpec((16, TM), lambda i: (0, nt - 1 - i)),
        whole((D_PROJ, D_MODEL)), whole((W_R, W_KB)), whole((1, W_KB)), pl.BlockSpec(memory_space=pltpu.SMEM),
    ]
    out_shape = (
        jax.ShapeDtypeStruct((s_len, D_MODEL), F32),
        jax.ShapeDtypeStruct((2, ACC_ROWS, HALF), F32),
        jax.ShapeDtypeStruct((Q_HEADS, 128), F32),
        jax.ShapeDtypeStruct((1, W_KB), F32),
        jax.ShapeDtypeStruct((W_R, W_KB), F32),
    )
    out_specs = (tile(D_MODEL), whole((2, ACC_ROWS, HALF)), whole((Q_HEADS, 128)), whole((1, W_KB)),
                 whole((W_R, W_KB)))
    scratch = [
        pltpu.VMEM((TM, D_PROJ), BF16), pltpu.VMEM((BLOCK, W_KA), F32), pltpu.VMEM((BLOCK, W_VA), F32),
        pltpu.VMEM((GLA_DV, GLA_HEADS * GLA_DK), F32), pltpu.VMEM((TM, W_KB), F32),
    ]
    return pl.pallas_call(
        body, name="bwd_mix", grid=(nt,), in_specs=in_specs, out_specs=out_specs, out_shape=out_shape,
        scratch_shapes=scratch,
        compiler_params=pltpu.CompilerParams(dimension_semantics=("arbitrary",), vmem_limit_bytes=VMEM_LIMIT),
    )(x, dh, qa, ka, va, ka, va, dattn, dga, dob, dgb, qb, kb, vb, r, st, cs, win, wgu, bg, sinks)


def _mesh_place():
    x, y, c = lax.axis_index("x"), lax.axis_index("y"), lax.axis_index("c")
    chips = [(1 - x, y), (x, 1 - y), (1 - x, 1 - y)]
    return x, y, c, chips


def _gather_weights_call(w_lin, w_out, wgu):
    def body(wlin_ref, wout_ref, wgu_ref, wt_ref, wout_full, wgu_all, blk, oblk, asm, send_sems, recv_sems):
        x, y, c, chips = _mesh_place()
        k_me = 2 * x + y
        asm[SHARD_IN - 4:SHARD_PAD, :] = jnp.zeros((SHARD_PAD - SHARD_IN + 4, D_MODEL), F32)
        asm[0:SHARD_IN, :] = wlin_ref[:, 0, :]
        for h in range(2):
            blk[k_me, h] = asm[0:SHARD_PAD, h * HALF:(h + 1) * HALF].astype(BF16)
            oblk[k_me, h] = wout_ref[:, h * HALF:(h + 1) * HALF].astype(BF16)
        wgu_all[k_me] = wgu_ref[...]

        def blocks(k, hc):
            return (blk.at[k, hc], oblk.at[k, hc])

        def copies(k, hc, sem0, to):
            return [pltpu.make_async_remote_copy(src_ref=blk, dst_ref=blk, send_sem=send_sems.at[sem0 + n],
                                                 recv_sem=recv_sems.at[sem0 + n], device_id=to, device_id_type=MESH)
                    for n, blk in enumerate(blocks(k, hc))]

        def gu_copy(k, r, to):
            return pltpu.make_async_remote_copy(src_ref=wgu_all.at[k], dst_ref=wgu_all.at[k], send_sem=send_sems.at[12 + r],
                                                recv_sem=recv_sems.at[12 + r], device_id=to, device_id_type=MESH)

        started = []
        for r, chip in enumerate(chips):
            started += copies(k_me, c, 2 * r, (*chip, c))
            started.append(gu_copy(k_me, r, (*chip, c)))
        for cp in started:
            cp.start()
        for r, chip in enumerate(chips):
            k_r = 2 * chip[0] + chip[1]
            for cp in copies(k_r, c, 2 * r, (x, y, c)):
                cp.wait_recv()
            passed = copies(k_r, c, 6 + 2 * r, (x, y, 1 - c))
            for cp in passed:
                cp.start()
            started += passed
        for r, chip in enumerate(chips):
            k_r = 2 * chip[0] + chip[1]
            for cp in copies(k_r, 1 - c, 6 + 2 * r, (x, y, c)):
                cp.wait_recv()
            gu_copy(k_r, r, (x, y, c)).wait_recv()
        for cp in started:
            cp.wait_send()

        for k in range(N_CHIPS):
            for h in range(2):
                asm[k * SHARD_IN:k * SHARD_IN + SHARD_PAD, h * HALF:(h + 1) * HALF] = blk[k, h].astype(F32)
                wout_full[k * SHARD_OUT:(k + 1) * SHARD_OUT, h * HALF:(h + 1) * HALF] = oblk[k, h]
        wt_ref[...] = asm[0:D_PROJ, :].astype(BF16)

    vmem = pl.BlockSpec(memory_space=pltpu.VMEM)
    return pl.pallas_call(
        body, name="gather_weights",
        out_shape=(jax.ShapeDtypeStruct((D_PROJ, D_MODEL), BF16),
                   jax.ShapeDtypeStruct((D_MODEL, D_MODEL), BF16),
                   jax.ShapeDtypeStruct((N_CHIPS, W_R, W_KB // N_CHIPS), F32)),
        in_specs=[vmem, vmem, vmem], out_specs=(vmem, vmem, vmem),
        scratch_shapes=[pltpu.VMEM((N_CHIPS, 2, SHARD_PAD, HALF), BF16), pltpu.VMEM((N_CHIPS, 2, SHARD_OUT, HALF), BF16),
                        pltpu.VMEM((ACC_ROWS, D_MODEL), F32),
                        pltpu.SemaphoreType.DMA((15,)), pltpu.SemaphoreType.DMA((15,))],
        compiler_params=pltpu.CompilerParams(vmem_limit_bytes=VMEM_LIMIT),
    )(w_lin, w_out, wgu)


def _adamw(w, g, m, v):
    m = ADAM_B1 * m + (1.0 - ADAM_B1) * g
    v = ADAM_B2 * v + (1.0 - ADAM_B2) * (g * g)
    m_hat = m / (1.0 - ADAM_B1 ** ADAM_STEP)
    v_hat = v / (1.0 - ADAM_B2 ** ADAM_STEP)
    delta = -ADAM_LR * (m_hat / (jnp.sqrt(v_hat) + ADAM_EPS) + ADAM_WD * w)
    return delta, m, v


N_SMALL = 6


def _reduce_grads_call(g_in, g_out, small_grads):
    def body(gin_hbm, gout_hbm, g_lng, g_lnb, g_bg, g_nw, g_sink, g_wgu, loss_in, lin_in, fin_out, tot_out,
             a_in, a_out, b_in, b_out, c_in, s_in, s_out, r_in, r_out, f_in, f_out, pack_ref, tot_ref, pack_all,
             send_sems, recv_sems, local_sems):
        x, y, c, chips = _mesh_place()
        k_me = 2 * x + y
        me = 4 * x + 2 * y + c
        sibling = (x, y, 1 - c)

        pack_ref[...] = jnp.zeros_like(pack_ref)
        for a in range(8):
            pack_ref[P_LNG + a:P_LNG + a + 1, :] = g_lng[:, a * 128:(a + 1) * 128]
            pack_ref[P_LNB + a:P_LNB + a + 1, :] = g_lnb[:, a * 128:(a + 1) * 128]
        for a in range(2):
            pack_ref[P_BG + a:P_BG + a + 1, :] = g_bg[:, a * 128:(a + 1) * 128]
        pack_ref[P_NW:P_NW + 1, :] = g_nw[...]
        lane = lax.broadcasted_iota(jnp.int32, (1, 128), 1)
        sink_row = jnp.zeros((1, 128), F32)
        for hq in range(Q_HEADS):
            sink_row = jnp.where(lane == hq, g_sink[hq:hq + 1, :], sink_row)
        pack_ref[P_SINK:P_SINK + 1, :] = sink_row
        pack_ref[P_LOSS:P_LOSS + 1, :] = loss_in[...]
        gu_w = W_KB // N_CHIPS
        for k in range(N_CHIPS):
            pack_ref[P_GU + W_R * k:P_GU + W_R * (k + 1), 0:gu_w] = g_wgu[:, k * gu_w:(k + 1) * gu_w]
        pack_all[me] = pack_ref[...]
        small = []
        for mask in range(1, 8):
            peer = (x ^ (mask >> 2), y ^ ((mask >> 1) & 1), c ^ (mask & 1))
            small.append(pltpu.make_async_remote_copy(
                src_ref=pack_ref, dst_ref=pack_all.at[me], send_sem=send_sems.at[mask], recv_sem=recv_sems.at[mask],
                device_id=peer, device_id_type=MESH))
        for cp in small:
            cp.start()

        mine = [pltpu.make_async_copy(gin_hbm.at[c], a_in, local_sems.at[0]),
                pltpu.make_async_copy(gout_hbm.at[c], a_out, local_sems.at[1])]
        to_sib = [pltpu.make_async_remote_copy(
                      src_ref=gin_hbm.at[1 - c], dst_ref=b_in,
                      send_sem=send_sems.at[8], recv_sem=recv_sems.at[8], device_id=sibling, device_id_type=MESH),
                  pltpu.make_async_remote_copy(
                      src_ref=gout_hbm.at[1 - c], dst_ref=b_out,
                      send_sem=send_sems.at[9], recv_sem=recv_sems.at[9], device_id=sibling, device_id_type=MESH)]
        for cp in mine + to_sib:
            cp.start()
        for cp in mine:
            cp.wait()
        for cp in to_sib:
            cp.wait_recv()
        for k in range(N_CHIPS):
            rows = slice(k * SHARD_IN, k * SHARD_IN + SHARD_PAD)
            c_in[k] = a_in[rows, :] + b_in[rows, :]

        sent = []
        for r, chip in enumerate(chips):
            k_r = 2 * chip[0] + chip[1]
            s_in[r] = c_in[k_r].astype(BF16)
            s_out[r] = (a_out[k_r] + b_out[k_r]).astype(BF16)
            sent.append(pltpu.make_async_remote_copy(
                src_ref=s_in.at[r], dst_ref=r_in.at[r], send_sem=send_sems.at[10 + 2 * r],
                recv_sem=recv_sems.at[10 + 2 * r], device_id=(*chip, c), device_id_type=MESH))
            sent.append(pltpu.make_async_remote_copy(
                src_ref=s_out.at[r], dst_ref=r_out.at[r], send_sem=send_sems.at[11 + 2 * r],
                recv_sem=recv_sems.at[11 + 2 * r], device_id=(*chip, c), device_id_type=MESH))
            sent[-2].start()
            sent[-1].start()
        own_out = a_out[k_me] + b_out[k_me]
        for cp in sent:
            cp.wait_recv()
        f_in[c] = c_in[k_me] + r_in[0].astype(F32) + r_in[1].astype(F32) + r_in[2].astype(F32)
        f_out[c] = own_out + r_out[0].astype(F32) + r_out[1].astype(F32) + r_out[2].astype(F32)

        swap = [pltpu.make_async_remote_copy(
                    src_ref=f_in.at[c], dst_ref=f_in.at[c],
                    send_sem=send_sems.at[16], recv_sem=recv_sems.at[16], device_id=sibling, device_id_type=MESH),
                pltpu.make_async_remote_copy(
                    src_ref=f_out.at[c], dst_ref=f_out.at[c],
                    send_sem=send_sems.at[17], recv_sem=recv_sems.at[17], device_id=sibling, device_id_type=MESH)]
        for cp in swap:
            cp.start()

        for cp in small:
            cp.wait_recv()
        total = pack_all[0]
        for d in range(1, 8):
            total = total + pack_all[d]
        tot_ref[...] = total
        tot_out[0:P_GU, :] = total[0:P_GU]
        tot_out[P_GU:PACK_OWN_ROWS, :] = tot_ref[pl.ds(pl.multiple_of(P_GU + W_R * k_me, 8), W_R), :]

        other_in = pltpu.make_async_remote_copy(
            src_ref=f_in.at[1 - c], dst_ref=f_in.at[1 - c],
            send_sem=send_sems.at[16], recv_sem=recv_sems.at[16], device_id=sibling, device_id_type=MESH)
        other_out = pltpu.make_async_remote_copy(
            src_ref=f_out.at[1 - c], dst_ref=f_out.at[1 - c],
            send_sem=send_sems.at[17], recv_sem=recv_sems.at[17], device_id=sibling, device_id_type=MESH)
        other_in.wait_recv()
        other_out.wait_recv()
        for cp in small + to_sib + sent + swap:
            cp.wait_send()

        for h in range(2):
            lin_in[:, h * HALF:(h + 1) * HALF] = f_in[h, 0:SHARD_IN, :]
            fin_out[:, h * HALF:(h + 1) * HALF] = f_out[h]

    vmem = pl.BlockSpec(memory_space=pltpu.VMEM)
    hbm = pl.BlockSpec(memory_space=pl.ANY)
    return pl.pallas_call(
        body, name="reduce_grads",
        out_shape=(jax.ShapeDtypeStruct((SHARD_IN, D_MODEL), F32), jax.ShapeDtypeStruct((SHARD_OUT, D_MODEL), F32),
                   jax.ShapeDtypeStruct((PACK_OWN_ROWS, 128), F32)),
        in_specs=[hbm, hbm] + [vmem] * 7, out_specs=(vmem,) * 3,
        scratch_shapes=[
            pltpu.VMEM((ACC_ROWS, HALF), F32), pltpu.VMEM((N_CHIPS, SHARD_OUT, HALF), F32),
            pltpu.VMEM((ACC_ROWS, HALF), F32), pltpu.VMEM((N_CHIPS, SHARD_OUT, HALF), F32),
            pltpu.VMEM((N_CHIPS, SHARD_PAD, HALF), F32),
            pltpu.VMEM((3, SHARD_PAD, HALF), BF16), pltpu.VMEM((3, SHARD_OUT, HALF), BF16),
            pltpu.VMEM((3, SHARD_PAD, HALF), BF16), pltpu.VMEM((3, SHARD_OUT, HALF), BF16),
            pltpu.VMEM((2, SHARD_PAD, HALF), F32), pltpu.VMEM((2, SHARD_OUT, HALF), F32),
            pltpu.VMEM((PACK_ROWS, 128), F32), pltpu.VMEM((PACK_ROWS, 128), F32), pltpu.VMEM((8, PACK_ROWS, 128), F32),
            pltpu.SemaphoreType.DMA((18,)), pltpu.SemaphoreType.DMA((18,)), pltpu.SemaphoreType.DMA((2,)),
        ],
        compiler_params=pltpu.CompilerParams(vmem_limit_bytes=VMEM_LIMIT),
    )(g_in, g_out, *small_grads)


def _adamw_call(g_in, w_in, m_in, v_in, g_out, w_out, m_out, v_out, tot, small_params):
    steps = 4
    rows_out = SHARD_OUT // steps
    cols = D_MODEL // steps
    gu_w = W_KB // N_CHIPS

    def body(gi, wi, mi, vi, go, wo, mo, vo, tot, *rest):
        params = rest[:3 * N_SMALL]
        gi_o, di, nmi, nvi, go_o, do, nmo, nvo, loss_out = rest[3 * N_SMALL:3 * N_SMALL + 9]
        small_out = rest[3 * N_SMALL + 9:]

        @pl.when(pl.program_id(0) == 0)
        def _():
            loss_out[...] = tot[P_LOSS:P_LOSS + 1, :]
            g_outs = small_out[0:N_SMALL]
            for a in range(8):
                g_outs[0][:, a * 128:(a + 1) * 128] = tot[P_LNG + a:P_LNG + a + 1, :]
                g_outs[1][:, a * 128:(a + 1) * 128] = tot[P_LNB + a:P_LNB + a + 1, :]
            for a in range(2):
                g_outs[2][:, a * 128:(a + 1) * 128] = tot[P_BG + a:P_BG + a + 1, :]
            g_outs[3][...] = tot[P_NW:P_NW + 1, :]
            g_outs[4][...] = tot[P_SINK:P_SINK + 1, 0:Q_HEADS]
            g_outs[5][...] = tot[P_GU:PACK_OWN_ROWS, 0:gu_w]
            for n in range(N_SMALL):
                w_ref, m_ref, v_ref = params[3 * n:3 * n + 3]
                delta, new_m, new_v = _adamw(w_ref[...], g_outs[n][...], m_ref[...], v_ref[...])
                small_out[N_SMALL + n][...] = delta
                small_out[2 * N_SMALL + n][...] = new_m
                small_out[3 * N_SMALL + n][...] = new_v

        g = gi[...]
        delta, new_m, new_v = _adamw(wi[:, 0, :], g, mi[:, 0, :], vi[:, 0, :])
        gi_o[:, 0, :] = g
        di[:, 0, :] = delta
        nmi[:, 0, :] = new_m
        nvi[:, 0, :] = new_v
        g = go[...]
        go_o[...] = g
        do[...], nmo[...], nvo[...] = _adamw(wo[...], g, mo[...], vo[...])

    t_g = pl.BlockSpec((SHARD_IN, cols), lambda i: (0, i))
    t_in = pl.BlockSpec((SHARD_IN, 1, cols), lambda i: (0, 0, i))
    t_out = pl.BlockSpec((rows_out, D_MODEL), lambda i: (i, 0))
    s_in = jax.ShapeDtypeStruct((SHARD_IN, 1, D_MODEL), F32)
    s_out = jax.ShapeDtypeStruct((SHARD_OUT, D_MODEL), F32)
    whole = lambda shape: pl.BlockSpec(shape, lambda i: tuple(0 for _ in shape))
    small_specs = [whole(p.shape) for p in small_params]
    small_shapes = [jax.ShapeDtypeStruct(p.shape, F32) for p in small_params[0::3]] * 4
    return pl.pallas_call(
        body, name="adamw", grid=(steps,),
        in_specs=[t_g] + [t_in] * 3 + [t_out] * 4 + [whole(tot.shape)] + small_specs,
        out_specs=(t_in,) * 4 + (t_out,) * 4 + (whole((1, 128)),) + tuple(small_specs[0::3] * 4),
        out_shape=(s_in,) * 4 + (s_out,) * 4 + (jax.ShapeDtypeStruct((1, 128), F32),) + tuple(small_shapes),
        compiler_params=pltpu.CompilerParams(dimension_semantics=("arbitrary",)),
    )(g_in, w_in, m_in, v_in, g_out, w_out, m_out, v_out, tot, *small_params)


def _rope_tables(positions):
    half = 8
    inv_freq = 500000.0 ** (-jnp.arange(half, dtype=F32) / half)
    ang = inv_freq[:, None] * positions.astype(F32)[None, :]
    return jnp.concatenate([jnp.cos(ang), jnp.sin(ang)], axis=0)


def kernel(x, positions, w_in, gla_w_gate_up, gla_b_gate, attn_sinks, gla_norm_w, w_out, ln_g, ln_b, loss_target, m_w_in, m_gla_w_gate_up, m_gla_b_gate, m_attn_sinks, m_gla_norm_w, m_w_out, m_ln_g, m_ln_b, v_w_in, v_gla_w_gate_up, v_gla_b_gate, v_attn_sinks, v_gla_norm_w, v_w_out, v_ln_g, v_ln_b):
    def lin3(w):
        return jnp.transpose(w, (2, 0, 1))

    def unlin(w):
        return jnp.transpose(w, (1, 2, 0))

    win, wout, wgu_all = _gather_weights_call(lin3(w_in), w_out[0], gla_w_gate_up[0])
    wgu = jnp.transpose(wgu_all, (1, 0, 2)).reshape(W_R, W_KB)
    cs = _rope_tables(positions[0])
    sinks = attn_sinks[0]

    (qa, ka, va, qb, kb, vb, r, dattn, dga, dob, dgb, dh, st, g_wout, g_lng, g_lnb, g_nw, loss) = _fwd_call(
        x[0], loss_target[0], cs, win, wout, wgu, gla_b_gate, sinks, gla_norm_w, ln_g, ln_b)
    gx, g_win, g_sink, g_bg, g_wgu = _bwd_call(
        x[0], dh, qa, ka, va, dattn, dga, dob, dgb, qb, kb, vb, r, st, cs, win, wgu, gla_b_gate, sinks)

    g_wout_by_chip = g_wout.reshape(2, N_CHIPS, SHARD_OUT, HALF)
    small_params = []
    for group in ((ln_g, m_ln_g, v_ln_g), (ln_b, m_ln_b, v_ln_b), (gla_b_gate, m_gla_b_gate, v_gla_b_gate),
                  (gla_norm_w, m_gla_norm_w, v_gla_norm_w), (attn_sinks, m_attn_sinks, v_attn_sinks)):
        small_params += list(group)
    small_params += [gla_w_gate_up[0], m_gla_w_gate_up[0], v_gla_w_gate_up[0]]
    fin_in, fin_out, tot = _reduce_grads_call(g_win, g_wout_by_chip, (g_lng, g_lnb, g_bg, g_nw, g_sink, g_wgu, loss))
    fin_in, d_in, nm_in, nv_in, fin_out, d_out, nm_out, nv_out, loss_sum, *small_out = _adamw_call(
        fin_in, lin3(w_in), lin3(m_w_in), lin3(v_w_in), fin_out, w_out[0], m_w_out[0], v_w_out[0], tot, small_params)

    def unpack(kind, big_in, big_out):
        lng_, lnb_, bg_, nw_, sink_, gu_ = small_out[kind * N_SMALL:(kind + 1) * N_SMALL]
        return (unlin(big_in), gu_[None], bg_, sink_, nw_, big_out[None], lng_, lnb_)

    loss_total = loss_sum[0, 0]
    g_s, d_s, nm_s, nv_s = 0, 1, 2, 3
    return (loss_total, gx[None], *unpack(g_s, fin_in, fin_out), *unpack(d_s, d_in, d_out),
            *unpack(nm_s, nm_in, nm_out), *unpack(nv_s, nv_in, nv_out))
```

```python
import functools

import jax
import jax.numpy as jnp
import numpy as np
from jax import lax
from jax.experimental import pallas as pl
from jax.experimental.pallas import tpu as pltpu

F32 = jnp.float32
BF16 = jnp.bfloat16
MESH = pl.DeviceIdType.MESH

D_MODEL = 1024
N_CHIPS = 4
W_QA, W_KA, W_VA, W_GA, W_QB, W_KB, W_VB, W_GB, W_R = 512, 128, 128, 512, 256, 256, 512, 512, 16
O_QA = 0
O_KA = O_QA + W_QA
O_VA = O_KA + W_KA
O_GA = O_VA + W_VA
O_QB = O_GA + W_GA
O_KB = O_QB + W_QB
O_VB = O_KB + W_KB
O_GB = O_VB + W_VB
O_R = O_GB + W_GB
D_PROJ = O_R + W_R
SHARD_IN = D_PROJ // N_CHIPS
SHARD_OUT = D_MODEL // N_CHIPS
SHARD_PAD = 720
ACC_ROWS = -(-((N_CHIPS - 1) * SHARD_IN + SHARD_PAD) // 8) * 8
HALF = D_MODEL // 2
GATHER_SPLIT = 368

HEAD_A = 64
Q_HEADS = 8
KV_HEADS = 2
GROUP = 4
BLOCK = 128
GLA_HEADS = 4
GLA_DK = 64
GLA_DV = 128
CHUNK = 64
GLA_TAU = 16.0
EPS = 1e-5
ALPHA = 2.0 ** 0.25
ATT_SCALE = HEAD_A ** -0.5
GLA_SCALE = GLA_DK ** -0.5

ADAM_LR = 0.001
ADAM_B1 = 0.9
ADAM_B2 = 0.999
ADAM_EPS = 1e-08
ADAM_WD = 0.01
ADAM_STEP = 10

TM = 256
TRI_SLAB = 128
VMEM_LIMIT = 56 * 1024 * 1024

P_LNG, P_LNB, P_BG, P_NW, P_SINK, P_LOSS, P_GU = 0, 8, 16, 18, 19, 20, 24
PACK_ROWS = P_GU + N_CHIPS * 16
PACK_OWN_ROWS = P_GU + 16


def _mm(a, b):
    return jnp.dot(a, b, preferred_element_type=F32)


def _mm_nt(a, b):
    return lax.dot_general(a, b, (((1,), (1,)), ((), ())), preferred_element_type=F32)


def _mm_tn(a, b):
    return lax.dot_general(a, b, (((0,), (0,)), ((), ())), preferred_element_type=F32)


def _split3(a):
    hi = a.astype(BF16)
    r1 = a - hi.astype(F32)
    mid = r1.astype(BF16)
    lo = (r1 - mid.astype(F32)).astype(BF16)
    return hi, mid, lo


def _tri_mm(tri, a):
    slab = tri.shape[0]
    hi, mid, lo = _split3(a)
    return jnp.concatenate(
        [_mm(tri, hi[s:s + slab]) + _mm(tri, mid[s:s + slab]) + _mm(tri, lo[s:s + slab])
         for s in range(0, a.shape[0], slab)], axis=0)


def _chunk_tri(n, upper):
    r = lax.broadcasted_iota(jnp.int32, (n, n), 0)
    c = lax.broadcasted_iota(jnp.int32, (n, n), 1)
    same = (r >> 6) == (c >> 6)
    order = (c >= r) if upper else (c <= r)
    return jnp.where(same & order, 1.0, 0.0).astype(BF16)


def _rope(t, cos, sa, sb):
    w = t.shape[1]
    return t * cos + pltpu.roll(t, w - 8, 1) * sa + pltpu.roll(t, 8, 1) * sb


def _rope_tile(cs):
    row = lax.broadcasted_iota(jnp.int32, (16, 128), 0)
    d = lax.broadcasted_iota(jnp.int32, (16, 128), 1) & (HEAD_A - 1)
    hit = (d & 7) == (row & 7)
    is_cos = row < 8
    lo = d < 8
    hi = (d >= 8) & (d < 16)
    pick_cos = jnp.where(hit & is_cos & (lo | hi), 1.0, 0.0).astype(BF16)
    pick_sa = jnp.where(hit & ~is_cos & lo, -1.0, 0.0).astype(BF16)
    pick_sb = jnp.where(hit & ~is_cos & hi, 1.0, 0.0).astype(BF16)
    pieces = _split3(cs)

    def spread(pick):
        return _mm_tn(pieces[0], pick) + _mm_tn(pieces[1], pick) + _mm_tn(pieces[2], pick)

    d1 = lax.broadcasted_iota(jnp.int32, (1, 128), 1) & (HEAD_A - 1)
    return spread(pick_cos) + jnp.where(d1 < 16, 0.0, 1.0), spread(pick_sa), spread(pick_sb)


def _rope_bwd(d, cos, sa, sb):
    w = d.shape[1]
    return d * cos + pltpu.roll(d * sa, 8, 1) + pltpu.roll(d * sb, w - 8, 1)


def _log_sigmoid(z):
    return jnp.minimum(z, 0.0) - jnp.log1p(jnp.exp(-jnp.abs(z)))


def _sigmoid(z):
    return 1.0 / (1.0 + jnp.exp(-z))


def _attn_bias(has_prev):
    r = lax.broadcasted_iota(jnp.int32, (GROUP * BLOCK, 2 * BLOCK), 0) & (BLOCK - 1)
    k = lax.broadcasted_iota(jnp.int32, (GROUP * BLOCK, 2 * BLOCK), 1)
    first_key = jnp.where(has_prev, 0, BLOCK)
    return jnp.where((k > r) & (k <= r + BLOCK) & (k >= first_key), 0.0, -jnp.inf)


def _sink_col(sinks_ref, j):
    r = lax.broadcasted_iota(jnp.int32, (GROUP * BLOCK, 1), 0) >> 7
    col = jnp.full((GROUP * BLOCK, 1), sinks_ref[GROUP * j], F32)
    for g in range(1, GROUP):
        col = jnp.where(r == g, sinks_ref[GROUP * j + g], col)
    return col


def _stack_heads(t, j):
    return jnp.concatenate([t[:, (GROUP * j + g) * HEAD_A:(GROUP * j + g + 1) * HEAD_A] for g in range(GROUP)], axis=0)


def _unstack_heads(parts):
    return jnp.concatenate([parts[j][g * BLOCK:(g + 1) * BLOCK] for j in range(KV_HEADS) for g in range(GROUP)], axis=1)


def _softmax_block(qs, kc, bias, sink):
    s = _mm_nt(qs, kc) * ATT_SCALE + bias
    m = jnp.maximum(jnp.max(s, axis=1, keepdims=True), sink)
    p = jnp.exp(s - m)
    e_sink = jnp.exp(sink - m)
    inv = 1.0 / (jnp.sum(p, axis=1, keepdims=True) + e_sink)
    return p * inv, e_sink * inv


def _fwd_call(x, tgt, cs, win, wout, wgu, bg, sinks, nw, lng, lnb):
    s_len = x.shape[0]
    nt = s_len // TM
    nblk = TM // BLOCK
    nch = TM // CHUNK

    def body(x_ref, t_ref, cs_ref, win_ref, wout_ref, wgu_ref, bg_ref, sinks_ref, nw_ref,
             lng_ref, lnb_ref,
             qa_ref, ka_ref, va_ref, qb_ref, kb_ref, vb_ref, r_ref, dattn_ref, dga_ref, dob_ref, dgb_ref, dh_ref,
             st_ref, dwout_ref, glng_ref, glnb_ref, gnw_ref, loss_ref,
             kprev, vprev, state, attn_s, ga_s, ob_s, gb_s, cat_s):
        i = pl.program_id(0)

        @pl.when(i == 0)
        def _():
            kprev[...] = jnp.zeros_like(kprev)
            vprev[...] = jnp.zeros_like(vprev)
            state[...] = jnp.zeros_like(state)
            dwout_ref[...] = jnp.zeros_like(dwout_ref)
            glng_ref[...] = jnp.zeros_like(glng_ref)
            glnb_ref[...] = jnp.zeros_like(glnb_ref)
            gnw_ref[...] = jnp.zeros_like(gnw_ref)
            loss_ref[...] = jnp.zeros_like(loss_ref)

        x = x_ref[...]
        xb = x.astype(BF16)

        def proj(off, width):
            return _mm_nt(xb, win_ref[off:off + width, :])

        cos, sa, sb = _rope_tile(cs_ref[...])
        cos4, sa4, sb4 = (jnp.concatenate([t] * 4, axis=1) for t in (cos, sa, sb))
        qa = _rope(proj(O_QA, W_QA), cos4, sa4, sb4).astype(BF16)
        ka = _rope(proj(O_KA, W_KA), cos, sa, sb).astype(BF16)
        va = proj(O_VA, W_VA).astype(BF16)
        qa_ref[...] = qa
        ka_ref[...] = ka
        va_ref[...] = va
        ga_s[...] = proj(O_GA, W_GA)
        gb_s[...] = proj(O_GB, W_GB)

        bias_inner = _attn_bias(True)
        sink_cols = [_sink_col(sinks_ref, j) for j in range(KV_HEADS)]
        for b in range(nblk):
            rows = slice(b * BLOCK, (b + 1) * BLOCK)
            mask = _attn_bias(i > 0) if b == 0 else bias_inner
            k_cur = ka[rows]
            v_cur = va[rows]
            k_old = kprev[...] if b == 0 else ka[(b - 1) * BLOCK:b * BLOCK]
            v_old = vprev[...] if b == 0 else va[(b - 1) * BLOCK:b * BLOCK]
            outs = []
            for j in range(KV_HEADS):
                hs = slice(j * HEAD_A, (j + 1) * HEAD_A)
                kc = jnp.concatenate([k_old[:, hs], k_cur[:, hs]], axis=0)
                vc = jnp.concatenate([v_old[:, hs], v_cur[:, hs]], axis=0)
                probs, _ = _softmax_block(_stack_heads(qa[rows], j), kc, mask, sink_cols[j])
                outs.append(_mm(probs.astype(BF16), vc))
            attn_s[rows, :] = _unstack_heads(outs)
        kprev[...] = ka[(nblk - 1) * BLOCK:]
        vprev[...] = va[(nblk - 1) * BLOCK:]

        r = proj(O_R, W_R)
        r_ref[...] = r
        z = _mm(r.astype(BF16), wgu_ref[...].astype(BF16)) + bg_ref[...]
        log_a = _log_sigmoid(z) / GLA_TAU
        bcum = _tri_mm(_chunk_tri(TRI_SLAB, False), log_a)
        qb = proj(O_QB, W_QB)
        kb = proj(O_KB, W_KB)
        vb = proj(O_VB, W_VB).astype(BF16)
        qb_ref[...] = qb
        kb_ref[...] = kb
        vb_ref[...] = vb
        qd_all = (qb * GLA_SCALE * jnp.exp(bcum)).astype(BF16)
        ki_all = (kb * jnp.exp(-bcum)).astype(BF16)
        tril = lax.broadcasted_iota(jnp.int32, (CHUNK, CHUNK), 0) >= lax.broadcasted_iota(jnp.int32, (CHUNK, CHUNK), 1)
        for c in range(nch):
            rows = slice(c * CHUNK, (c + 1) * CHUNK)
            b_c = bcum[rows]
            b_last = b_c[CHUNK - 1:CHUNK]
            ke = (kb[rows] * jnp.exp(b_last - b_c)).astype(BF16)
            st = state[...]
            st_ref[c] = st
            st16 = st.astype(BF16)
            o_parts, u_parts = [], []
            for h in range(GLA_HEADS):
                ks = slice(h * GLA_DK, (h + 1) * GLA_DK)
                vs = slice(h * GLA_DV, (h + 1) * GLA_DV)
                qd = qd_all[rows, ks]
                v_h = vb[rows, vs]
                a = jnp.where(tril, _mm_nt(qd, ki_all[rows, ks]), 0.0)
                o_parts.append(_mm(a.astype(BF16), v_h) + _mm_nt(qd, st16[:, ks]))
                u_parts.append(_mm_tn(v_h, ke[:, ks]))
            ob_s[rows, :] = jnp.concatenate(o_parts, axis=1)
            state[...] = st * jnp.exp(b_last) + jnp.concatenate(u_parts, axis=1)

        ga = ga_s[...]
        sg_a = _sigmoid(ga)
        silu_a = ga * sg_a
        attn = attn_s[...]
        cat_s[:, :W_GA] = (attn * silu_a).astype(BF16)
        gb = gb_s[...]
        sg_b = _sigmoid(gb)
        silu_b = gb * sg_b
        nw = nw_ref[...]
        on_parts = []
        for h in range(GLA_HEADS):
            vs = slice(h * GLA_DV, (h + 1) * GLA_DV)
            o_h = ob_s[:, vs]
            rs = lax.rsqrt(jnp.mean(o_h * o_h, axis=1, keepdims=True) + EPS)
            on_parts.append(o_h * rs * nw)
        on = jnp.concatenate(on_parts, axis=1)
        cat_s[:, W_GA:] = (on * silu_b).astype(BF16)
        cat = cat_s[...]
        hres = ALPHA * x + _mm(cat, wout_ref[...])
        mu = jnp.mean(hres, axis=1, keepdims=True)
        hc = hres - mu
        rstd = lax.rsqrt(jnp.mean(hc * hc, axis=1, keepdims=True) + EPS)
        xhat = hc * rstd
        g_ln = lng_ref[...]
        err = xhat * g_ln + lnb_ref[...] - t_ref[...]
        loss_ref[...] += jnp.sum(err * err) * (0.5 / D_MODEL)
        dy = err * (1.0 / D_MODEL)
        glng_ref[...] += jnp.sum(dy * xhat, axis=0, keepdims=True)
        glnb_ref[...] += jnp.sum(dy, axis=0, keepdims=True)
        dxh = dy * g_ln
        dh = rstd * (dxh - jnp.mean(dxh, axis=1, keepdims=True) - xhat * jnp.mean(dxh * xhat, axis=1, keepdims=True))
        dh_ref[...] = dh
        dh16 = dh.astype(BF16)
        for h in range(2):
            dwout_ref[h] += _mm_tn(cat, dh16[:, h * HALF:(h + 1) * HALF])
        dcat = _mm_nt(dh16, wout_ref[...])

        d_a = dcat[:, :W_GA]
        dattn_ref[...] = (d_a * silu_a).astype(BF16)
        dga_ref[...] = (d_a * attn * (sg_a * (1.0 + ga * (1.0 - sg_a)))).astype(BF16)
        d_b = dcat[:, W_GA:]
        dgb_ref[...] = (d_b * on * (sg_b * (1.0 + gb * (1.0 - sg_b)))).astype(BF16)
        d_on = d_b * silu_b
        gnw = jnp.zeros((1, GLA_DV), F32)
        do_parts = []
        for h in range(GLA_HEADS):
            vs = slice(h * GLA_DV, (h + 1) * GLA_DV)
            o_h = ob_s[:, vs]
            rs = lax.rsqrt(jnp.mean(o_h * o_h, axis=1, keepdims=True) + EPS)
            d_on_h = d_on[:, vs]
            gnw = gnw + jnp.sum(d_on_h * o_h * rs, axis=0, keepdims=True)
            gg = d_on_h * nw
            do_parts.append(rs * gg - o_h * (rs * rs * rs) * jnp.mean(gg * o_h, axis=1, keepdims=True))
        gnw_ref[...] += gnw
        dob_ref[...] = jnp.concatenate(do_parts, axis=1).astype(BF16)

    tile = lambda w: pl.BlockSpec((TM, w), lambda i: (i, 0))
    whole = lambda shape: pl.BlockSpec(shape, lambda i: tuple(0 for _ in shape), pipeline_mode=pl.Buffered(1))
    out_shape = (
        jax.ShapeDtypeStruct((s_len, W_QA), BF16),
        jax.ShapeDtypeStruct((s_len, W_KA), BF16),
        jax.ShapeDtypeStruct((s_len, W_VA), BF16),
        jax.ShapeDtypeStruct((s_len, W_QB), F32),
        jax.ShapeDtypeStruct((s_len, W_KB), F32),
        jax.ShapeDtypeStruct((s_len, W_VB), BF16),
        jax.ShapeDtypeStruct((s_len, W_R), F32),
        jax.ShapeDtypeStruct((s_len, W_GA), BF16),
        jax.ShapeDtypeStruct((s_len, W_GA), BF16),
        jax.ShapeDtypeStruct((s_len, W_GB), BF16),
        jax.ShapeDtypeStruct((s_len, W_GB), BF16),
        jax.ShapeDtypeStruct((s_len, D_MODEL), F32),
        jax.ShapeDtypeStruct((s_len // CHUNK, GLA_DV, GLA_HEADS * GLA_DK), F32),
        jax.ShapeDtypeStruct((2, D_MODEL, HALF), F32),
        jax.ShapeDtypeStruct((1, D_MODEL), F32),
        jax.ShapeDtypeStruct((1, D_MODEL), F32),
        jax.ShapeDtypeStruct((1, GLA_DV), F32),
        jax.ShapeDtypeStruct((1, 128), F32),
    )
    out_specs = (
        tile(W_QA), tile(W_KA), tile(W_VA), tile(W_QB), tile(W_KB), tile(W_VB), tile(W_R),
        tile(W_GA), tile(W_GA), tile(W_GB), tile(W_GB), tile(D_MODEL),
        pl.BlockSpec((nch, GLA_DV, GLA_HEADS * GLA_DK), lambda i: (i, 0, 0)),
        whole((2, D_MODEL, HALF)), whole((1, D_MODEL)), whole((1, D_MODEL)), whole((1, GLA_DV)), whole((1, 128)),
    )
    in_specs = [
        tile(D_MODEL), tile(D_MODEL), pl.BlockSpec((16, TM), lambda i: (0, i)),
        whole((D_PROJ, D_MODEL)), whole((D_MODEL, D_MODEL)), whole((W_R, W_KB)), whole((1, W_KB)),
        pl.BlockSpec(memory_space=pltpu.SMEM), whole((1, GLA_DV)), whole((1, D_MODEL)), whole((1, D_MODEL)),
    ]
    scratch = [
        pltpu.VMEM((BLOCK, W_KA), BF16), pltpu.VMEM((BLOCK, W_VA), BF16),
        pltpu.VMEM((GLA_DV, GLA_HEADS * GLA_DK), F32),
        pltpu.VMEM((TM, W_GA), F32), pltpu.VMEM((TM, W_GA), F32), pltpu.VMEM((TM, W_GB), F32),
        pltpu.VMEM((TM, W_GB), F32), pltpu.VMEM((TM, D_MODEL), BF16),
    ]
    return pl.pallas_call(
        body, name="fwd_head", grid=(nt,), in_specs=in_specs, out_specs=out_specs, out_shape=out_shape,
        scratch_shapes=scratch,
        compiler_params=pltpu.CompilerParams(dimension_semantics=("arbitrary",), vmem_limit_bytes=VMEM_LIMIT),
    )(x, tgt, cs, win, wout, wgu, bg, sinks, nw, lng, lnb)


def _bwd_call(x, dh, qa, ka, va, dattn, dga, dob, dgb, qb, kb, vb, r, st, cs, win, wgu, bg, sinks):
    s_len = x.shape[0]
    nt = s_len // TM
    nblk = TM // BLOCK
    nch = TM // CHUNK

    def body(x_ref, dh_ref, qa_ref, ka_ref, va_ref, kap_ref, vap_ref, dattn_ref, dga_ref, dob_ref, dgb_ref,
             qb_ref, kb_ref, vb_ref, r_ref, st_ref, cs_ref, win_ref, wgu_ref, bg_ref, sinks_ref,
             gx_ref, dwin_ref, gsink_ref, gbg_ref, gwgu_ref,
             dproj, dk_carry, dv_carry, ds_carry, db_s):
        i = pl.program_id(0)
        t = nt - 1 - i

        @pl.when(i == 0)
        def _():
            dk_carry[...] = jnp.zeros_like(dk_carry)
            dv_carry[...] = jnp.zeros_like(dv_carry)
            ds_carry[...] = jnp.zeros_like(ds_carry)
            dwin_ref[...] = jnp.zeros_like(dwin_ref)
            gsink_ref[...] = jnp.zeros_like(gsink_ref)
            gbg_ref[...] = jnp.zeros_like(gbg_ref)
            gwgu_ref[...] = jnp.zeros_like(gwgu_ref)

        cos, sa, sb = _rope_tile(cs_ref[...])
        cos4, sa4, sb4 = (jnp.concatenate([v] * 4, axis=1) for v in (cos, sa, sb))

        qa = qa_ref[...]
        ka = ka_ref[...]
        va = va_ref[...]
        dattn = dattn_ref[...]
        gsink_rows = [jnp.zeros((1, 1), F32) for _ in range(Q_HEADS)]
        bias_inner = _attn_bias(True)
        sink_cols = [_sink_col(sinks_ref, j) for j in range(KV_HEADS)]
        for b in reversed(range(nblk)):
            rows = slice(b * BLOCK, (b + 1) * BLOCK)
            mask = _attn_bias(t > 0) if b == 0 else bias_inner
            k_cur = ka[rows]
            v_cur = va[rows]
            k_old = kap_ref[...] if b == 0 else ka[(b - 1) * BLOCK:b * BLOCK]
            v_old = vap_ref[...] if b == 0 else va[(b - 1) * BLOCK:b * BLOCK]
            dq_parts, dk_parts, dv_parts = [], [], []
            for j in range(KV_HEADS):
                hs = slice(j * HEAD_A, (j + 1) * HEAD_A)
                kc = jnp.concatenate([k_old[:, hs], k_cur[:, hs]], axis=0)
                vc = jnp.concatenate([v_old[:, hs], v_cur[:, hs]], axis=0)
                qs = _stack_heads(qa[rows], j)
                do_s = _stack_heads(dattn[rows], j)
                probs, p_sink = _softmax_block(qs, kc, mask, sink_cols[j])
                dp = _mm_nt(do_s, vc)
                d_row = jnp.sum(probs * dp, axis=1, keepdims=True)
                ds16 = (probs * (dp - d_row) * ATT_SCALE).astype(BF16)
                dq_parts.append(_mm(ds16, kc))
                dk_parts.append(_mm_tn(ds16, qs))
                dv_parts.append(_mm_tn(probs.astype(BF16), do_s))
                t_sink = d_row * p_sink
                for g in range(GROUP):
                    gsink_rows[GROUP * j + g] = gsink_rows[GROUP * j + g] - jnp.sum(
                        t_sink[g * BLOCK:(g + 1) * BLOCK], axis=0, keepdims=True)
            dq = _rope_bwd(_unstack_heads(dq_parts), cos4[rows], sa4[rows], sb4[rows])
            dproj[rows, O_QA:O_QA + W_QA] = dq.astype(BF16)
            dk_cur = dk_carry[...] + jnp.concatenate([p[BLOCK:] for p in dk_parts], axis=1)
            dv_cur = dv_carry[...] + jnp.concatenate([p[BLOCK:] for p in dv_parts], axis=1)
            dproj[rows, O_KA:O_KA + W_KA] = _rope_bwd(dk_cur, cos[rows], sa[rows], sb[rows]).astype(BF16)
            dproj[rows, O_VA:O_VA + W_VA] = dv_cur.astype(BF16)
            dk_carry[...] = jnp.concatenate([p[:BLOCK] for p in dk_parts], axis=1)
            dv_carry[...] = jnp.concatenate([p[:BLOCK] for p in dv_parts], axis=1)
        for hq in range(Q_HEADS):
            gsink_ref[hq:hq + 1, :] += jnp.broadcast_to(gsink_rows[hq], (1, 128))

        dproj[:, O_GA:O_GA + W_GA] = dga_ref[...]
        dproj[:, O_GB:O_GB + W_GB] = dgb_ref[...]

        r16 = r_ref[...].astype(BF16)
        wgu16 = wgu_ref[...].astype(BF16)
        z = _mm(r16, wgu16) + bg_ref[...]
        log_a = _log_sigmoid(z) / GLA_TAU
        bcum = _tri_mm(_chunk_tri(TRI_SLAB, False), log_a)
        qb = qb_ref[...]
        kb = kb_ref[...]
        vb = vb_ref[...]
        dob = dob_ref[...]
        e_b = jnp.exp(bcum)
        e_nb = jnp.exp(-bcum)
        qd_f = qb * GLA_SCALE * e_b
        ki_f = kb * e_nb
        qd_all = qd_f.astype(BF16)
        ki_all = ki_f.astype(BF16)
        tril = lax.broadcasted_iota(jnp.int32, (CHUNK, CHUNK), 0) >= lax.broadcasted_iota(jnp.int32, (CHUNK, CHUNK), 1)
        last_row = lax.broadcasted_iota(jnp.int32, (CHUNK, 1), 0) == CHUNK - 1
        for c in reversed(range(nch)):
            rows = slice(c * CHUNK, (c + 1) * CHUNK)
            b_c = bcum[rows]
            b_last = b_c[CHUNK - 1:CHUNK]
            e_e = jnp.exp(b_last - b_c)
            dec = jnp.exp(b_last)
            ke_f = kb[rows] * e_e
            ke = ke_f.astype(BF16)
            sp = st_ref[c]
            sp16 = sp.astype(BF16)
            dsn = ds_carry[...]
            dsn16 = dsn.astype(BF16)
            dqd_p, dki_p, dke_p, dv_p, dsp_p = [], [], [], [], []
            for h in range(GLA_HEADS):
                ks = slice(h * GLA_DK, (h + 1) * GLA_DK)
                vs = slice(h * GLA_DV, (h + 1) * GLA_DV)
                qd = qd_all[rows, ks]
                ki = ki_all[rows, ks]
                v_h = vb[rows, vs]
                do_h = dob[rows, vs]
                a16 = jnp.where(tril, _mm_nt(qd, ki), 0.0).astype(BF16)
                da16 = jnp.where(tril, _mm_nt(do_h, v_h), 0.0).astype(BF16)
                dv_p.append(_mm_tn(a16, do_h) + _mm_nt(ke[:, ks], dsn16[:, ks]))
                dqd_p.append(_mm(da16, ki) + _mm(do_h, sp16[:, ks]))
                dki_p.append(_mm_tn(da16, qd))
                dke_p.append(_mm(v_h, dsn16[:, ks]))
                dsp_p.append(_mm_tn(do_h, qd))
            dqd = jnp.concatenate(dqd_p, axis=1)
            dki = jnp.concatenate(dki_p, axis=1)
            dke = jnp.concatenate(dke_p, axis=1)
            ddec = jnp.sum(dsn * sp, axis=0, keepdims=True)
            ds_carry[...] = dsn * dec + jnp.concatenate(dsp_p, axis=1)
            dproj[rows, O_QB:O_QB + W_QB] = (dqd * e_b[rows] * GLA_SCALE).astype(BF16)
            dproj[rows, O_KB:O_KB + W_KB] = (dki * e_nb[rows] + dke * e_e).astype(BF16)
            dproj[rows, O_VB:O_VB + W_VB] = jnp.concatenate(dv_p, axis=1).astype(BF16)
            dke_ke = dke * ke_f
            d_b = dqd * qd_f[rows] - dki * ki_f[rows] - dke_ke
            d_bl = jnp.sum(dke_ke, axis=0, keepdims=True) + ddec * dec
            db_s[rows, :] = d_b + jnp.where(last_row, d_bl, 0.0)
        dlog_a = _tri_mm(_chunk_tri(TRI_SLAB, True), db_s[...])
        dz = dlog_a * (1.0 / GLA_TAU) * _sigmoid(-z)
        dz16 = dz.astype(BF16)
        gbg_ref[...] += jnp.sum(dz, axis=0, keepdims=True)
        gwgu_ref[...] += _mm_tn(r16, dz16)
        dproj[:, O_R:O_R + W_R] = _mm_nt(dz16, wgu16).astype(BF16)

        dp16 = dproj[...]
        gx_ref[...] = ALPHA * dh_ref[...] + _mm(dp16, win_ref[...])
        x16 = x_ref[...].astype(BF16)
        for h in range(2):
            dwin_ref[h, 0:D_PROJ, :] += _mm_tn(dp16, x16[:, h * HALF:(h + 1) * HALF])

    tile = lambda w: pl.BlockSpec((TM, w), lambda i: (nt - 1 - i, 0))
    whole = lambda shape: pl.BlockSpec(shape, lambda i: tuple(0 for _ in shape), pipeline_mode=pl.Buffered(1))
    prev_blk = pl.BlockSpec((BLOCK, W_KA), lambda i: (jnp.maximum((nt - 1 - i) * nblk - 1, 0), 0))
    in_specs = [
        tile(D_MODEL), tile(D_MODEL), tile(W_QA), tile(W_KA), tile(W_VA), prev_blk, prev_blk,
        tile(W_GA), tile(W_GA), tile(W_GB), tile(W_GB), tile(W_QB), tile(W_KB), tile(W_VB), tile(W_R),
        pl.BlockSpec((nch, GLA_DV, GLA_HEADS * GLA_DK), lambda i: (nt - 1 - i, 0, 0)),
        pl.BlockSpec((16, TM), lambda i: (0, nt - 1 - i)),
        whole((D_PROJ, D_MODEL)), whole((W_R, W_KB)), whole((1, W_KB)), pl.BlockSpec(memory_space=pltpu.SMEM),
    ]
    out_shape = (
        jax.ShapeDtypeStruct((s_len, D_MODEL), F32),
        jax.ShapeDtypeStruct((2, ACC_ROWS, HALF), F32),
        jax.ShapeDtypeStruct((Q_HEADS, 128), F32),
        jax.ShapeDtypeStruct((1, W_KB), F32),
        jax.ShapeDtypeStruct((W_R, W_KB), F32),
    )
    out_specs = (tile(D_MODEL), whole((2, ACC_ROWS, HALF)), whole((Q_HEADS, 128)), whole((1, W_KB)),
                 whole((W_R, W_KB)))
    scratch = [
        pltpu.VMEM((TM, D_PROJ), BF16), pltpu.VMEM((BLOCK, W_KA), F32), pltpu.VMEM((BLOCK, W_VA), F32),
        pltpu.VMEM((GLA_DV, GLA_HEADS * GLA_DK), F32), pltpu.VMEM((TM, W_KB), F32),
    ]
    return pl.pallas_call(
        body, name="bwd_mix", grid=(nt,), in_specs=in_specs, out_specs=out_specs, out_shape=out_shape,
        scratch_shapes=scratch,
        compiler_params=pltpu.CompilerParams(dimension_semantics=("arbitrary",), vmem_limit_bytes=VMEM_LIMIT),
    )(x, dh, qa, ka, va, ka, va, dattn, dga, dob, dgb, qb, kb, vb, r, st, cs, win, wgu, bg, sinks)


def _mesh_place():
    x, y, c = lax.axis_index("x"), lax.axis_index("y"), lax.axis_index("c")
    chips = [(1 - x, y), (x, 1 - y), (1 - x, 1 - y)]
    return x, y, c, chips


def _gather_weights_call(w_lin, w_out, wgu):
    def body(wlin_ref, wout_ref, wgu_ref, wt_ref, wout_full, wgu_all, blk, oblk, asm, send_sems, recv_sems):
        x, y, c, chips = _mesh_place()
        k_me = 2 * x + y
        asm[SHARD_IN - 4:SHARD_PAD, :] = jnp.zeros((SHARD_PAD - SHARD_IN + 4, D_MODEL), F32)
        asm[0:SHARD_IN, :] = wlin_ref[:, 0, :]
        for h in range(2):
            blk[k_me, h] = asm[0:SHARD_PAD, h * HALF:(h + 1) * HALF].astype(BF16)
            oblk[k_me, h] = wout_ref[:, h * HALF:(h + 1) * HALF].astype(BF16)
        wgu_all[k_me] = wgu_ref[...]

        parts = ((0, GATHER_SPLIT, 0, SHARD_OUT // 2), (GATHER_SPLIT, SHARD_PAD - GATHER_SPLIT, SHARD_OUT // 2, SHARD_OUT // 2))
        me_id, sib_id = (x, y, c), (x, y, 1 - c)
        nbr_x, nbr_y, diag = ((*chip, c) for chip in chips)
        k_x, k_y, k_d = (2 * chip[0] + chip[1] for chip in chips)

        def copies(k, hc, p, sem0, to):
            if p is None:
                refs = (blk.at[k, hc], oblk.at[k, hc])
            else:
                r0, rn, o0, on = parts[p]
                refs = (blk.at[k, hc, pl.ds(r0, rn), :], oblk.at[k, hc, pl.ds(o0, on), :])
            return [pltpu.make_async_remote_copy(src_ref=ref, dst_ref=ref, send_sem=send_sems.at[sem0 + n],
                                                 recv_sem=recv_sems.at[sem0 + n], device_id=to, device_id_type=MESH)
                    for n, ref in enumerate(refs)]

        def gu_copy(k, r, to):
            return pltpu.make_async_remote_copy(src_ref=wgu_all.at[k], dst_ref=wgu_all.at[k], send_sem=send_sems.at[18 + r],
                                                recv_sem=recv_sems.at[18 + r], device_id=to, device_id_type=MESH)

        def start(cps):
            for cp in cps:
                cp.start()
            return cps

        def landed(cps):
            for cp in cps:
                cp.wait_recv()

        started = start(copies(k_me, c, 0, 0, nbr_x) + copies(k_me, c, 1, 6, nbr_y)
                        + copies(k_me, c, 1, 2, nbr_x) + copies(k_me, c, 0, 4, nbr_y)
                        + [gu_copy(k_me, r, to) for r, to in enumerate((nbr_x, nbr_y, diag))])
        landed(copies(k_x, c, 0, 0, me_id))
        started += start(copies(k_x, c, 0, 8, nbr_y))
        landed(copies(k_y, c, 1, 6, me_id))
        started += start(copies(k_y, c, 1, 10, nbr_x))
        landed(copies(k_x, c, 1, 2, me_id))
        started += start(copies(k_x, c, None, 12, sib_id))
        landed(copies(k_y, c, 0, 4, me_id))
        started += start(copies(k_y, c, None, 14, sib_id))
        landed(copies(k_d, c, 0, 8, me_id) + copies(k_d, c, 1, 10, me_id))
        started += start(copies(k_d, c, None, 16, sib_id))
        for r, k_r in enumerate((k_x, k_y, k_d)):
            landed(copies(k_r, 1 - c, None, 12 + 2 * r, me_id))
            gu_copy(k_r, r, me_id).wait_recv()
        for cp in started:
            cp.wait_send()

        for k in range(N_CHIPS):
            for h in range(2):
                asm[k * SHARD_IN:k * SHARD_IN + SHARD_PAD, h * HALF:(h + 1) * HALF] = blk[k, h].astype(F32)
                wout_full[k * SHARD_OUT:(k + 1) * SHARD_OUT, h * HALF:(h + 1) * HALF] = oblk[k, h]
        wt_ref[...] = asm[0:D_PROJ, :].astype(BF16)

    vmem = pl.BlockSpec(memory_space=pltpu.VMEM)
    return pl.pallas_call(
        body, name="gather_weights",
        out_shape=(jax.ShapeDtypeStruct((D_PROJ, D_MODEL), BF16),
                   jax.ShapeDtypeStruct((D_MODEL, D_MODEL), BF16),
                   jax.ShapeDtypeStruct((N_CHIPS, W_R, W_KB // N_CHIPS), F32)),
        in_specs=[vmem, vmem, vmem], out_specs=(vmem, vmem, vmem),
        scratch_shapes=[pltpu.VMEM((N_CHIPS, 2, SHARD_PAD, HALF), BF16), pltpu.VMEM((N_CHIPS, 2, SHARD_OUT, HALF), BF16),
                        pltpu.VMEM((ACC_ROWS, D_MODEL), F32),
                        pltpu.SemaphoreType.DMA((21,)), pltpu.SemaphoreType.DMA((21,))],
        compiler_params=pltpu.CompilerParams(vmem_limit_bytes=VMEM_LIMIT),
    )(w_lin, w_out, wgu)


def _adamw(w, g, m, v):
    m = ADAM_B1 * m + (1.0 - ADAM_B1) * g
    v = ADAM_B2 * v + (1.0 - ADAM_B2) * (g * g)
    m_hat = m / (1.0 - ADAM_B1 ** ADAM_STEP)
    v_hat = v / (1.0 - ADAM_B2 ** ADAM_STEP)
    delta = -ADAM_LR * (m_hat / (jnp.sqrt(v_hat) + ADAM_EPS) + ADAM_WD * w)
    return delta, m, v


N_SMALL = 6


def _reduce_grads_call(g_in, g_out, small_grads):
    def body(gin_hbm, gout_hbm, g_lng, g_lnb, g_bg, g_nw, g_sink, g_wgu, loss_in, lin_in, fin_out, tot_out,
             a_in, a_out, b_in, b_out, c_in, s_in, s_out, r_in, r_out, f_in, f_out, pack_ref, tot_ref, pack_all,
             send_sems, recv_sems, local_sems):
        x, y, c, chips = _mesh_place()
        k_me = 2 * x + y
        me = 4 * x + 2 * y + c
        sibling = (x, y, 1 - c)

        pack_ref[...] = jnp.zeros_like(pack_ref)
        for a in range(8):
            pack_ref[P_LNG + a:P_LNG + a + 1, :] = g_lng[:, a * 128:(a + 1) * 128]
            pack_ref[P_LNB + a:P_LNB + a + 1, :] = g_lnb[:, a * 128:(a + 1) * 128]
        for a in range(2):
            pack_ref[P_BG + a:P_BG + a + 1, :] = g_bg[:, a * 128:(a + 1) * 128]
        pack_ref[P_NW:P_NW + 1, :] = g_nw[...]
        lane = lax.broadcasted_iota(jnp.int32, (1, 128), 1)
        sink_row = jnp.zeros((1, 128), F32)
        for hq in range(Q_HEADS):
            sink_row = jnp.where(lane == hq, g_sink[hq:hq + 1, :], sink_row)
        pack_ref[P_SINK:P_SINK + 1, :] = sink_row
        pack_ref[P_LOSS:P_LOSS + 1, :] = loss_in[...]
        gu_w = W_KB // N_CHIPS
        for k in range(N_CHIPS):
            pack_ref[P_GU + W_R * k:P_GU + W_R * (k + 1), 0:gu_w] = g_wgu[:, k * gu_w:(k + 1) * gu_w]
        pack_all[me] = pack_ref[...]
        small = []
        for mask in range(1, 8):
            peer = (x ^ (mask >> 2), y ^ ((mask >> 1) & 1), c ^ (mask & 1))
            small.append(pltpu.make_async_remote_copy(
                src_ref=pack_ref, dst_ref=pack_all.at[me], send_sem=send_sems.at[mask], recv_sem=recv_sems.at[mask],
                device_id=peer, device_id_type=MESH))
        for cp in small:
            cp.start()

        mine = [pltpu.make_async_copy(gin_hbm.at[c], a_in, local_sems.at[0]),
                pltpu.make_async_copy(gout_hbm.at[c], a_out, local_sems.at[1])]
        to_sib = [pltpu.make_async_remote_copy(
                      src_ref=gin_hbm.at[1 - c], dst_ref=b_in,
                      send_sem=send_sems.at[8], recv_sem=recv_sems.at[8], device_id=sibling, device_id_type=MESH),
                  pltpu.make_async_remote_copy(
                      src_ref=gout_hbm.at[1 - c], dst_ref=b_out,
                      send_sem=send_sems.at[9], recv_sem=recv_sems.at[9], device_id=sibling, device_id_type=MESH)]
        for cp in mine + to_sib:
            cp.start()
        for cp in mine:
            cp.wait()
        for cp in to_sib:
            cp.wait_recv()
        for k in range(N_CHIPS):
            rows = slice(k * SHARD_IN, k * SHARD_IN + SHARD_PAD)
            c_in[k] = a_in[rows, :] + b_in[rows, :]

        sent = []
        for r, chip in enumerate(chips):
            k_r = 2 * chip[0] + chip[1]
            s_in[r] = c_in[k_r].astype(BF16)
            s_out[r] = (a_out[k_r] + b_out[k_r]).astype(BF16)
            sent.append(pltpu.make_async_remote_copy(
                src_ref=s_in.at[r], dst_ref=r_in.at[r], send_sem=send_sems.at[10 + 2 * r],
                recv_sem=recv_sems.at[10 + 2 * r], device_id=(*chip, c), device_id_type=MESH))
            sent.append(pltpu.make_async_remote_copy(
                src_ref=s_out.at[r], dst_ref=r_out.at[r], send_sem=send_sems.at[11 + 2 * r],
                recv_sem=recv_sems.at[11 + 2 * r], device_id=(*chip, c), device_id_type=MESH))
            sent[-2].start()
            sent[-1].start()
        own_out = a_out[k_me] + b_out[k_me]
        for cp in sent:
            cp.wait_recv()
        f_in[c] = c_in[k_me] + r_in[0].astype(F32) + r_in[1].astype(F32) + r_in[2].astype(F32)
        f_out[c] = own_out + r_out[0].astype(F32) + r_out[1].astype(F32) + r_out[2].astype(F32)

        swap = [pltpu.make_async_remote_copy(
                    src_ref=f_in.at[c], dst_ref=f_in.at[c],
                    send_sem=send_sems.at[16], recv_sem=recv_sems.at[16], device_id=sibling, device_id_type=MESH),
                pltpu.make_async_remote_copy(
                    src_ref=f_out.at[c], dst_ref=f_out.at[c],
                    send_sem=send_sems.at[17], recv_sem=recv_sems.at[17], device_id=sibling, device_id_type=MESH)]
        for cp in swap:
            cp.start()

        for cp in small:
            cp.wait_recv()
        total = pack_all[0]
        for d in range(1, 8):
            total = total + pack_all[d]
        tot_ref[...] = total
        tot_out[0:P_GU, :] = total[0:P_GU]
        tot_out[P_GU:PACK_OWN_ROWS, :] = tot_ref[pl.ds(pl.multiple_of(P_GU + W_R * k_me, 8), W_R), :]

        other_in = pltpu.make_async_remote_copy(
            src_ref=f_in.at[1 - c], dst_ref=f_in.at[1 - c],
            send_sem=send_sems.at[16], recv_sem=recv_sems.at[16], device_id=sibling, device_id_type=MESH)
        other_out = pltpu.make_async_remote_copy(
            src_ref=f_out.at[1 - c], dst_ref=f_out.at[1 - c],
            send_sem=send_sems.at[17], recv_sem=recv_sems.at[17], device_id=sibling, device_id_type=MESH)
        other_in.wait_recv()
        other_out.wait_recv()
        for cp in small + to_sib + sent + swap:
            cp.wait_send()

        for h in range(2):
            lin_in[:, h * HALF:(h + 1) * HALF] = f_in[h, 0:SHARD_IN, :]
            fin_out[:, h * HALF:(h + 1) * HALF] = f_out[h]

    vmem = pl.BlockSpec(memory_space=pltpu.VMEM)
    hbm = pl.BlockSpec(memory_space=pl.ANY)
    return pl.pallas_call(
        body, name="reduce_grads",
        out_shape=(jax.ShapeDtypeStruct((SHARD_IN, D_MODEL), F32), jax.ShapeDtypeStruct((SHARD_OUT, D_MODEL), F32),
                   jax.ShapeDtypeStruct((PACK_OWN_ROWS, 128), F32)),
        in_specs=[hbm, hbm] + [vmem] * 7, out_specs=(vmem,) * 3,
        scratch_shapes=[
            pltpu.VMEM((ACC_ROWS, HALF), F32), pltpu.VMEM((N_CHIPS, SHARD_OUT, HALF), F32),
            pltpu.VMEM((ACC_ROWS, HALF), F32), pltpu.VMEM((N_CHIPS, SHARD_OUT, HALF), F32),
            pltpu.VMEM((N_CHIPS, SHARD_PAD, HALF), F32),
            pltpu.VMEM((3, SHARD_PAD, HALF), BF16), pltpu.VMEM((3, SHARD_OUT, HALF), BF16),
            pltpu.VMEM((3, SHARD_PAD, HALF), BF16), pltpu.VMEM((3, SHARD_OUT, HALF), BF16),
            pltpu.VMEM((2, SHARD_PAD, HALF), F32), pltpu.VMEM((2, SHARD_OUT, HALF), F32),
            pltpu.VMEM((PACK_ROWS, 128), F32), pltpu.VMEM((PACK_ROWS, 128), F32), pltpu.VMEM((8, PACK_ROWS, 128), F32),
            pltpu.SemaphoreType.DMA((18,)), pltpu.SemaphoreType.DMA((18,)), pltpu.SemaphoreType.DMA((2,)),
        ],
        compiler_params=pltpu.CompilerParams(vmem_limit_bytes=VMEM_LIMIT),
    )(g_in, g_out, *small_grads)


def _adamw_call(g_in, w_in, m_in, v_in, g_out, w_out, m_out, v_out, tot, small_params):
    steps = 4
    rows_out = SHARD_OUT // steps
    cols = D_MODEL // steps
    gu_w = W_KB // N_CHIPS

    def body(gi, wi, mi, vi, go, wo, mo, vo, tot, *rest):
        params = rest[:3 * N_SMALL]
        gi_o, di, nmi, nvi, go_o, do, nmo, nvo, loss_out = rest[3 * N_SMALL:3 * N_SMALL + 9]
        small_out = rest[3 * N_SMALL + 9:]

        @pl.when(pl.program_id(0) == 0)
        def _():
            loss_out[...] = tot[P_LOSS:P_LOSS + 1, :]
            g_outs = small_out[0:N_SMALL]
            for a in range(8):
                g_outs[0][:, a * 128:(a + 1) * 128] = tot[P_LNG + a:P_LNG + a + 1, :]
                g_outs[1][:, a * 128:(a + 1) * 128] = tot[P_LNB + a:P_LNB + a + 1, :]
            for a in range(2):
                g_outs[2][:, a * 128:(a + 1) * 128] = tot[P_BG + a:P_BG + a + 1, :]
            g_outs[3][...] = tot[P_NW:P_NW + 1, :]
            g_outs[4][...] = tot[P_SINK:P_SINK + 1, 0:Q_HEADS]
            g_outs[5][...] = tot[P_GU:PACK_OWN_ROWS, 0:gu_w]
            for n in range(N_SMALL):
                w_ref, m_ref, v_ref = params[3 * n:3 * n + 3]
                delta, new_m, new_v = _adamw(w_ref[...], g_outs[n][...], m_ref[...], v_ref[...])
                small_out[N_SMALL + n][...] = delta
                small_out[2 * N_SMALL + n][...] = new_m
                small_out[3 * N_SMALL + n][...] = new_v

        g = gi[...]
        delta, new_m, new_v = _adamw(wi[:, 0, :], g, mi[:, 0, :], vi[:, 0, :])
        gi_o[:, 0, :] = g
        di[:, 0, :] = delta
        nmi[:, 0, :] = new_m
        nvi[:, 0, :] = new_v
        g = go[...]
        go_o[...] = g
        do[...], nmo[...], nvo[...] = _adamw(wo[...], g, mo[...], vo[...])

    t_g = pl.BlockSpec((SHARD_IN, cols), lambda i: (0, i))
    t_in = pl.BlockSpec((SHARD_IN, 1, cols), lambda i: (0, 0, i))
    t_out = pl.BlockSpec((rows_out, D_MODEL), lambda i: (i, 0))
    s_in = jax.ShapeDtypeStruct((SHARD_IN, 1, D_MODEL), F32)
    s_out = jax.ShapeDtypeStruct((SHARD_OUT, D_MODEL), F32)
    whole = lambda shape: pl.BlockSpec(shape, lambda i: tuple(0 for _ in shape))
    small_specs = [whole(p.shape) for p in small_params]
    small_shapes = [jax.ShapeDtypeStruct(p.shape, F32) for p in small_params[0::3]] * 4
    return pl.pallas_call(
        body, name="adamw", grid=(steps,),
        in_specs=[t_g] + [t_in] * 3 + [t_out] * 4 + [whole(tot.shape)] + small_specs,
        out_specs=(t_in,) * 4 + (t_out,) * 4 + (whole((1, 128)),) + tuple(small_specs[0::3] * 4),
        out_shape=(s_in,) * 4 + (s_out,) * 4 + (jax.ShapeDtypeStruct((1, 128), F32),) + tuple(small_shapes),
        compiler_params=pltpu.CompilerParams(dimension_semantics=("arbitrary",)),
    )(g_in, w_in, m_in, v_in, g_out, w_out, m_out, v_out, tot, *small_params)


def _rope_tables(positions):
    half = 8
    inv_freq = 500000.0 ** (-jnp.arange(half, dtype=F32) / half)
    ang = inv_freq[:, None] * positions.astype(F32)[None, :]
    return jnp.concatenate([jnp.cos(ang), jnp.sin(ang)], axis=0)


def kernel(x, positions, w_in, gla_w_gate_up, gla_b_gate, attn_sinks, gla_norm_w, w_out, ln_g, ln_b, loss_target, m_w_in, m_gla_w_gate_up, m_gla_b_gate, m_attn_sinks, m_gla_norm_w, m_w_out, m_ln_g, m_ln_b, v_w_in, v_gla_w_gate_up, v_gla_b_gate, v_attn_sinks, v_gla_norm_w, v_w_out, v_ln_g, v_ln_b):
    def lin3(w):
        return jnp.transpose(w, (2, 0, 1))

    def unlin(w):
        return jnp.transpose(w, (1, 2, 0))

    win, wout, wgu_all = _gather_weights_call(lin3(w_in), w_out[0], gla_w_gate_up[0])
    wgu = jnp.transpose(wgu_all, (1, 0, 2)).reshape(W_R, W_KB)
    cs = _rope_tables(positions[0])
    sinks = attn_sinks[0]

    (qa, ka, va, qb, kb, vb, r, dattn, dga, dob, dgb, dh, st, g_wout, g_lng, g_lnb, g_nw, loss) = _fwd_call(
        x[0], loss_target[0], cs, win, wout, wgu, gla_b_gate, sinks, gla_norm_w, ln_g, ln_b)
    gx, g_win, g_sink, g_bg, g_wgu = _bwd_call(
        x[0], dh, qa, ka, va, dattn, dga, dob, dgb, qb, kb, vb, r, st, cs, win, wgu, gla_b_gate, sinks)

    g_wout_by_chip = g_wout.reshape(2, N_CHIPS, SHARD_OUT, HALF)
    small_params = []
    for group in ((ln_g, m_ln_g, v_ln_g), (ln_b, m_ln_b, v_ln_b), (gla_b_gate, m_gla_b_gate, v_gla_b_gate),
                  (gla_norm_w, m_gla_norm_w, v_gla_norm_w), (attn_sinks, m_attn_sinks, v_attn_sinks)):
        small_params += list(group)
    small_params += [gla_w_gate_up[0], m_gla_w_gate_up[0], v_gla_w_gate_up[0]]
    fin_in, fin_out, tot = _reduce_grads_call(g_win, g_wout_by_chip, (g_lng, g_lnb, g_bg, g_nw, g_sink, g_wgu, loss))
    fin_in, d_in, nm_in, nv_in, fin_out, d_out, nm_out, nv_out, loss_sum, *small_out = _adamw_call(
        fin_in, lin3(w_in), lin3(m_w_in), lin3(v_w_in), fin_out, w_out[0], m_w_out[0], v_w_out[0], tot, small_params)

    def unpack(kind, big_in, big_out):
        lng_, lnb_, bg_, nw_, sink_, gu_ = small_out[kind * N_SMALL:(kind + 1) * N_SMALL]
        return (unlin(big_in), gu_[None], bg_, sink_, nw_, big_out[None], lng_, lnb_)

    loss_total = loss_sum[0, 0]
    g_s, d_s, nm_s, nv_s = 0, 1, 2, 3
    return (loss_total, gx[None], *unpack(g_s, fin_in, fin_out), *unpack(d_s, d_in, d_out),
            *unpack(nm_s, nm_in, nm_out), *unpack(nv_s, nv_in, nv_out))
```

```python
import functools

import jax
import jax.numpy as jnp
import numpy as np
from jax import lax
from jax.experimental import pallas as pl
from jax.experimental.pallas import tpu as pltpu

F32 = jnp.float32
BF16 = jnp.bfloat16
MESH = pl.DeviceIdType.MESH

D_MODEL = 1024
N_CHIPS = 4
W_QA, W_KA, W_VA, W_GA, W_QB, W_KB, W_VB, W_GB, W_R = 512, 128, 128, 512, 256, 256, 512, 512, 16
O_QA = 0
O_KA = O_QA + W_QA
O_VA = O_KA + W_KA
O_GA = O_VA + W_VA
O_QB = O_GA + W_GA
O_KB = O_QB + W_QB
O_VB = O_KB + W_KB
O_GB = O_VB + W_VB
O_R = O_GB + W_GB
D_PROJ = O_R + W_R
SHARD_IN = D_PROJ // N_CHIPS
SHARD_OUT = D_MODEL // N_CHIPS
SHARD_PAD = 720
ACC_ROWS = -(-((N_CHIPS - 1) * SHARD_IN + SHARD_PAD) // 8) * 8
HALF = D_MODEL // 2
GATHER_SPLIT = 368

HEAD_A = 64
Q_HEADS = 8
KV_HEADS = 2
GROUP = 4
BLOCK = 128
GLA_HEADS = 4
GLA_DK = 64
GLA_DV = 128
CHUNK = 64
GLA_TAU = 16.0
EPS = 1e-5
ALPHA = 2.0 ** 0.25
ATT_SCALE = HEAD_A ** -0.5
GLA_SCALE = GLA_DK ** -0.5

ADAM_LR = 0.001
ADAM_B1 = 0.9
ADAM_B2 = 0.999
ADAM_EPS = 1e-08
ADAM_WD = 0.01
ADAM_STEP = 10

TM = 256
TRI_SLAB = 128
VMEM_LIMIT = 56 * 1024 * 1024

P_LNG, P_LNB, P_BG, P_NW, P_SINK, P_LOSS, P_GU = 0, 8, 16, 18, 19, 20, 24
PACK_ROWS = P_GU + N_CHIPS * 16
PACK_OWN_ROWS = P_GU + 16


def _mm(a, b):
    return jnp.dot(a, b, preferred_element_type=F32)


def _mm_nt(a, b):
    return lax.dot_general(a, b, (((1,), (1,)), ((), ())), preferred_element_type=F32)


def _mm_tn(a, b):
    return lax.dot_general(a, b, (((0,), (0,)), ((), ())), preferred_element_type=F32)


def _split3(a):
    hi = a.astype(BF16)
    r1 = a - hi.astype(F32)
    mid = r1.astype(BF16)
    lo = (r1 - mid.astype(F32)).astype(BF16)
    return hi, mid, lo


def _tri_mm(tri, a):
    slab = tri.shape[0]
    hi, mid, lo = _split3(a)
    return jnp.concatenate(
        [_mm(tri, hi[s:s + slab]) + _mm(tri, mid[s:s + slab]) + _mm(tri, lo[s:s + slab])
         for s in range(0, a.shape[0], slab)], axis=0)


def _chunk_tri(n, upper):
    r = lax.broadcasted_iota(jnp.int32, (n, n), 0)
    c = lax.broadcasted_iota(jnp.int32, (n, n), 1)
    same = (r >> 6) == (c >> 6)
    order = (c >= r) if upper else (c <= r)
    return jnp.where(same & order, 1.0, 0.0).astype(BF16)


def _rope(t, cos, sa, sb):
    w = t.shape[1]
    return t * cos + pltpu.roll(t, w - 8, 1) * sa + pltpu.roll(t, 8, 1) * sb


def _rope_tile(cs):
    row = lax.broadcasted_iota(jnp.int32, (16, 128), 0)
    d = lax.broadcasted_iota(jnp.int32, (16, 128), 1) & (HEAD_A - 1)
    hit = (d & 7) == (row & 7)
    is_cos = row < 8
    lo = d < 8
    hi = (d >= 8) & (d < 16)
    pick_cos = jnp.where(hit & is_cos & (lo | hi), 1.0, 0.0).astype(BF16)
    pick_sa = jnp.where(hit & ~is_cos & lo, -1.0, 0.0).astype(BF16)
    pick_sb = jnp.where(hit & ~is_cos & hi, 1.0, 0.0).astype(BF16)
    pieces = _split3(cs)

    def spread(pick):
        return _mm_tn(pieces[0], pick) + _mm_tn(pieces[1], pick) + _mm_tn(pieces[2], pick)

    d1 = lax.broadcasted_iota(jnp.int32, (1, 128), 1) & (HEAD_A - 1)
    return spread(pick_cos) + jnp.where(d1 < 16, 0.0, 1.0), spread(pick_sa), spread(pick_sb)


def _rope_bwd(d, cos, sa, sb):
    w = d.shape[1]
    return d * cos + pltpu.roll(d * sa, 8, 1) + pltpu.roll(d * sb, w - 8, 1)


def _log_sigmoid(z):
    return jnp.minimum(z, 0.0) - jnp.log1p(jnp.exp(-jnp.abs(z)))


def _sigmoid(z):
    return 1.0 / (1.0 + jnp.exp(-z))


def _attn_bias(has_prev):
    r = lax.broadcasted_iota(jnp.int32, (GROUP * BLOCK, 2 * BLOCK), 0) & (BLOCK - 1)
    k = lax.broadcasted_iota(jnp.int32, (GROUP * BLOCK, 2 * BLOCK), 1)
    first_key = jnp.where(has_prev, 0, BLOCK)
    return jnp.where((k > r) & (k <= r + BLOCK) & (k >= first_key), 0.0, -jnp.inf)


def _sink_col(sinks_ref, j):
    r = lax.broadcasted_iota(jnp.int32, (GROUP * BLOCK, 1), 0) >> 7
    col = jnp.full((GROUP * BLOCK, 1), sinks_ref[GROUP * j], F32)
    for g in range(1, GROUP):
        col = jnp.where(r == g, sinks_ref[GROUP * j + g], col)
    return col


def _stack_heads(t, j):
    return jnp.concatenate([t[:, (GROUP * j + g) * HEAD_A:(GROUP * j + g + 1) * HEAD_A] for g in range(GROUP)], axis=0)


def _unstack_heads(parts):
    return jnp.concatenate([parts[j][g * BLOCK:(g + 1) * BLOCK] for j in range(KV_HEADS) for g in range(GROUP)], axis=1)


def _softmax_block(qs, kc, bias, sink):
    s = _mm_nt(qs, kc) * ATT_SCALE + bias
    m = jnp.maximum(jnp.max(s, axis=1, keepdims=True), sink)
    p = jnp.exp(s - m)
    e_sink = jnp.exp(sink - m)
    inv = 1.0 / (jnp.sum(p, axis=1, keepdims=True) + e_sink)
    return p * inv, e_sink * inv


def _fwd_call(x, tgt, cs, win, wout, wgu, bg, sinks, nw, lng, lnb):
    s_len = x.shape[0]
    nt = s_len // TM
    nblk = TM // BLOCK
    nch = TM // CHUNK

    def body(x_ref, t_ref, cs_ref, win_ref, wout_ref, wgu_ref, bg_ref, sinks_ref, nw_ref,
             lng_ref, lnb_ref,
             qa_ref, ka_ref, va_ref, qb_ref, kb_ref, vb_ref, r_ref, dattn_ref, dga_ref, dob_ref, dgb_ref, dh_ref,
             st_ref, dwout_ref, glng_ref, glnb_ref, gnw_ref, loss_ref,
             kprev, vprev, state, attn_s, ga_s, ob_s, gb_s, cat_s):
        i = pl.program_id(0)

        @pl.when(i == 0)
        def _():
            kprev[...] = jnp.zeros_like(kprev)
            vprev[...] = jnp.zeros_like(vprev)
            state[...] = jnp.zeros_like(state)
            dwout_ref[...] = jnp.zeros_like(dwout_ref)
            glng_ref[...] = jnp.zeros_like(glng_ref)
            glnb_ref[...] = jnp.zeros_like(glnb_ref)
            gnw_ref[...] = jnp.zeros_like(gnw_ref)
            loss_ref[...] = jnp.zeros_like(loss_ref)

        x = x_ref[...]
        xb = x.astype(BF16)

        def proj(off, width):
            return _mm_nt(xb, win_ref[off:off + width, :])

        cos, sa, sb = _rope_tile(cs_ref[...])
        cos4, sa4, sb4 = (jnp.concatenate([t] * 4, axis=1) for t in (cos, sa, sb))
        qa = _rope(proj(O_QA, W_QA), cos4, sa4, sb4).astype(BF16)
        ka = _rope(proj(O_KA, W_KA), cos, sa, sb).astype(BF16)
        va = proj(O_VA, W_VA).astype(BF16)
        qa_ref[...] = qa
        ka_ref[...] = ka
        va_ref[...] = va
        ga_s[...] = proj(O_GA, W_GA)
        gb_s[...] = proj(O_GB, W_GB)

        bias_inner = _attn_bias(True)
        sink_cols = [_sink_col(sinks_ref, j) for j in range(KV_HEADS)]
        for b in range(nblk):
            rows = slice(b * BLOCK, (b + 1) * BLOCK)
            mask = _attn_bias(i > 0) if b == 0 else bias_inner
            k_cur = ka[rows]
            v_cur = va[rows]
            k_old = kprev[...] if b == 0 else ka[(b - 1) * BLOCK:b * BLOCK]
            v_old = vprev[...] if b == 0 else va[(b - 1) * BLOCK:b * BLOCK]
            outs = []
            for j in range(KV_HEADS):
                hs = slice(j * HEAD_A, (j + 1) * HEAD_A)
                kc = jnp.concatenate([k_old[:, hs], k_cur[:, hs]], axis=0)
                vc = jnp.concatenate([v_old[:, hs], v_cur[:, hs]], axis=0)
                probs, _ = _softmax_block(_stack_heads(qa[rows], j), kc, mask, sink_cols[j])
                outs.append(_mm(probs.astype(BF16), vc))
            attn_s[rows, :] = _unstack_heads(outs)
        kprev[...] = ka[(nblk - 1) * BLOCK:]
        vprev[...] = va[(nblk - 1) * BLOCK:]

        r = proj(O_R, W_R)
        r_ref[...] = r
        z = _mm(r.astype(BF16), wgu_ref[...].astype(BF16)) + bg_ref[...]
        log_a = _log_sigmoid(z) / GLA_TAU
        bcum = _tri_mm(_chunk_tri(TRI_SLAB, False), log_a)
        qb = proj(O_QB, W_QB)
        kb = proj(O_KB, W_KB)
        vb = proj(O_VB, W_VB).astype(BF16)
        qb_ref[...] = qb
        kb_ref[...] = kb
        vb_ref[...] = vb
        qd_all = (qb * GLA_SCALE * jnp.exp(bcum)).astype(BF16)
        ki_all = (kb * jnp.exp(-bcum)).astype(BF16)
        tril = lax.broadcasted_iota(jnp.int32, (CHUNK, CHUNK), 0) >= lax.broadcasted_iota(jnp.int32, (CHUNK, CHUNK), 1)
        for c in range(nch):
            rows = slice(c * CHUNK, (c + 1) * CHUNK)
            b_c = bcum[rows]
            b_last = b_c[CHUNK - 1:CHUNK]
            ke = (kb[rows] * jnp.exp(b_last - b_c)).astype(BF16)
            st = state[...]
            st_ref[c] = st
            st16 = st.astype(BF16)
            o_parts, u_parts = [], []
            for h in range(GLA_HEADS):
                ks = slice(h * GLA_DK, (h + 1) * GLA_DK)
                vs = slice(h * GLA_DV, (h + 1) * GLA_DV)
                qd = qd_all[rows, ks]
                v_h = vb[rows, vs]
                a = jnp.where(tril, _mm_nt(qd, ki_all[rows, ks]), 0.0)
                o_parts.append(_mm(a.astype(BF16), v_h) + _mm_nt(qd, st16[:, ks]))
                u_parts.append(_mm_tn(v_h, ke[:, ks]))
            ob_s[rows, :] = jnp.concatenate(o_parts, axis=1)
            state[...] = st * jnp.exp(b_last) + jnp.concatenate(u_parts, axis=1)

        ga = ga_s[...]
        sg_a = _sigmoid(ga)
        silu_a = ga * sg_a
        attn = attn_s[...]
        cat_s[:, :W_GA] = (attn * silu_a).astype(BF16)
        gb = gb_s[...]
        sg_b = _sigmoid(gb)
        silu_b = gb * sg_b
        nw = nw_ref[...]
        on_parts = []
        for h in range(GLA_HEADS):
            vs = slice(h * GLA_DV, (h + 1) * GLA_DV)
            o_h = ob_s[:, vs]
            rs = lax.rsqrt(jnp.mean(o_h * o_h, axis=1, keepdims=True) + EPS)
            on_parts.append(o_h * rs * nw)
        on = jnp.concatenate(on_parts, axis=1)
        cat_s[:, W_GA:] = (on * silu_b).astype(BF16)
        cat = cat_s[...]
        hres = ALPHA * x + _mm(cat, wout_ref[...])
        mu = jnp.mean(hres, axis=1, keepdims=True)
        hc = hres - mu
        rstd = lax.rsqrt(jnp.mean(hc * hc, axis=1, keepdims=True) + EPS)
        xhat = hc * rstd
        g_ln = lng_ref[...]
        err = xhat * g_ln + lnb_ref[...] - t_ref[...]
        loss_ref[...] += jnp.sum(err * err) * (0.5 / D_MODEL)
        dy = err * (1.0 / D_MODEL)
        glng_ref[...] += jnp.sum(dy * xhat, axis=0, keepdims=True)
        glnb_ref[...] += jnp.sum(dy, axis=0, keepdims=True)
        dxh = dy * g_ln
        dh = rstd * (dxh - jnp.mean(dxh, axis=1, keepdims=True) - xhat * jnp.mean(dxh * xhat, axis=1, keepdims=True))
        dh_ref[...] = dh
        dh16 = dh.astype(BF16)
        for h in range(2):
            dwout_ref[h] += _mm_tn(cat, dh16[:, h * HALF:(h + 1) * HALF])
        dcat = _mm_nt(dh16, wout_ref[...])

        d_a = dcat[:, :W_GA]
        dattn_ref[...] = (d_a * silu_a).astype(BF16)
        dga_ref[...] = (d_a * attn * (sg_a * (1.0 + ga * (1.0 - sg_a)))).astype(BF16)
        d_b = dcat[:, W_GA:]
        dgb_ref[...] = (d_b * on * (sg_b * (1.0 + gb * (1.0 - sg_b)))).astype(BF16)
        d_on = d_b * silu_b
        gnw = jnp.zeros((1, GLA_DV), F32)
        do_parts = []
        for h in range(GLA_HEADS):
            vs = slice(h * GLA_DV, (h + 1) * GLA_DV)
            o_h = ob_s[:, vs]
            rs = lax.rsqrt(jnp.mean(o_h * o_h, axis=1, keepdims=True) + EPS)
            d_on_h = d_on[:, vs]
            gnw = gnw + jnp.sum(d_on_h * o_h * rs, axis=0, keepdims=True)
            gg = d_on_h * nw
            do_parts.append(rs * gg - o_h * (rs * rs * rs) * jnp.mean(gg * o_h, axis=1, keepdims=True))
        gnw_ref[...] += gnw
        dob_ref[...] = jnp.concatenate(do_parts, axis=1).astype(BF16)

    tile = lambda w: pl.BlockSpec((TM, w), lambda i: (i, 0))
    whole = lambda shape: pl.BlockSpec(shape, lambda i: tuple(0 for _ in shape), pipeline_mode=pl.Buffered(1))
    out_shape = (
        jax.ShapeDtypeStruct((s_len, W_QA), BF16),
        jax.ShapeDtypeStruct((s_len, W_KA), BF16),
        jax.ShapeDtypeStruct((s_len, W_VA), BF16),
        jax.ShapeDtypeStruct((s_len, W_QB), F32),
        jax.ShapeDtypeStruct((s_len, W_KB), F32),
        jax.ShapeDtypeStruct((s_len, W_VB), BF16),
        jax.ShapeDtypeStruct((s_len, W_R), F32),
        jax.ShapeDtypeStruct((s_len, W_GA), BF16),
        jax.ShapeDtypeStruct((s_len, W_GA), BF16),
        jax.ShapeDtypeStruct((s_len, W_GB), BF16),
        jax.ShapeDtypeStruct((s_len, W_GB), BF16),
        jax.ShapeDtypeStruct((s_len, D_MODEL), F32),
        jax.ShapeDtypeStruct((s_len // CHUNK, GLA_DV, GLA_HEADS * GLA_DK), F32),
        jax.ShapeDtypeStruct((2, D_MODEL, HALF), F32),
        jax.ShapeDtypeStruct((1, D_MODEL), F32),
        jax.ShapeDtypeStruct((1, D_MODEL), F32),
        jax.ShapeDtypeStruct((1, GLA_DV), F32),
        jax.ShapeDtypeStruct((1, 128), F32),
    )
    out_specs = (
        tile(W_QA), tile(W_KA), tile(W_VA), tile(W_QB), tile(W_KB), tile(W_VB), tile(W_R),
        tile(W_GA), tile(W_GA), tile(W_GB), tile(W_GB), tile(D_MODEL),
        pl.BlockSpec((nch, GLA_DV, GLA_HEADS * GLA_DK), lambda i: (i, 0, 0)),
        whole((2, D_MODEL, HALF)), whole((1, D_MODEL)), whole((1, D_MODEL)), whole((1, GLA_DV)), whole((1, 128)),
    )
    in_specs = [
        tile(D_MODEL), tile(D_MODEL), pl.BlockSpec((16, TM), lambda i: (0, i)),
        whole((D_PROJ, D_MODEL)), whole((D_MODEL, D_MODEL)), whole((W_R, W_KB)), whole((1, W_KB)),
        pl.BlockSpec(memory_space=pltpu.SMEM), whole((1, GLA_DV)), whole((1, D_MODEL)), whole((1, D_MODEL)),
    ]
    scratch = [
        pltpu.VMEM((BLOCK, W_KA), BF16), pltpu.VMEM((BLOCK, W_VA), BF16),
        pltpu.VMEM((GLA_DV, GLA_HEADS * GLA_DK), F32),
        pltpu.VMEM((TM, W_GA), F32), pltpu.VMEM((TM, W_GA), F32), pltpu.VMEM((TM, W_GB), F32),
        pltpu.VMEM((TM, W_GB), F32), pltpu.VMEM((TM, D_MODEL), BF16),
    ]
    return pl.pallas_call(
        body, name="fwd_head", grid=(nt,), in_specs=in_specs, out_specs=out_specs, out_shape=out_shape,
        scratch_shapes=scratch,
        compiler_params=pltpu.CompilerParams(dimension_semantics=("arbitrary",), vmem_limit_bytes=VMEM_LIMIT),
    )(x, tgt, cs, win, wout, wgu, bg, sinks, nw, lng, lnb)


def _bwd_call(x, dh, qa, ka, va, dattn, dga, dob, dgb, qb, kb, vb, r, st, cs, win, wgu, bg, sinks):
    s_len = x.shape[0]
    nt = s_len // TM
    nblk = TM // BLOCK
    nch = TM // CHUNK

    def body(x_ref, dh_ref, qa_ref, ka_ref, va_ref, kap_ref, vap_ref, dattn_ref, dga_ref, dob_ref, dgb_ref,
             qb_ref, kb_ref, vb_ref, r_ref, st_ref, cs_ref, win_ref, wgu_ref, bg_ref, sinks_ref,
             gx_ref, dwin_ref, gsink_ref, gbg_ref, gwgu_ref,
             dproj, dk_carry, dv_carry, ds_carry, db_s):
        i = pl.program_id(0)
        t = nt - 1 - i

        @pl.when(i == 0)
        def _():
            dk_carry[...] = jnp.zeros_like(dk_carry)
            dv_carry[...] = jnp.zeros_like(dv_carry)
            ds_carry[...] = jnp.zeros_like(ds_carry)
            dwin_ref[...] = jnp.zeros_like(dwin_ref)
            gsink_ref[...] = jnp.zeros_like(gsink_ref)
            gbg_ref[...] = jnp.zeros_like(gbg_ref)
            gwgu_ref[...] = jnp.zeros_like(gwgu_ref)

        cos, sa, sb = _rope_tile(cs_ref[...])
        cos4, sa4, sb4 = (jnp.concatenate([v] * 4, axis=1) for v in (cos, sa, sb))

        qa = qa_ref[...]
        ka = ka_ref[...]
        va = va_ref[...]
        dattn = dattn_ref[...]
        gsink_rows = [jnp.zeros((1, 1), F32) for _ in range(Q_HEADS)]
        bias_inner = _attn_bias(True)
        sink_cols = [_sink_col(sinks_ref, j) for j in range(KV_HEADS)]
        for b in reversed(range(nblk)):
            rows = slice(b * BLOCK, (b + 1) * BLOCK)
            mask = _attn_bias(t > 0) if b == 0 else bias_inner
            k_cur = ka[rows]
            v_cur = va[rows]
            k_old = kap_ref[...] if b == 0 else ka[(b - 1) * BLOCK:b * BLOCK]
            v_old = vap_ref[...] if b == 0 else va[(b - 1) * BLOCK:b * BLOCK]
            dq_parts, dk_parts, dv_parts = [], [], []
            for j in range(KV_HEADS):
                hs = slice(j * HEAD_A, (j + 1) * HEAD_A)
                kc = jnp.concatenate([k_old[:, hs], k_cur[:, hs]], axis=0)
                vc = jnp.concatenate([v_old[:, hs], v_cur[:, hs]], axis=0)
                qs = _stack_heads(qa[rows], j)
                do_s = _stack_heads(dattn[rows], j)
                probs, p_sink = _softmax_block(qs, kc, mask, sink_cols[j])
                dp = _mm_nt(do_s, vc)
                d_row = jnp.sum(probs * dp, axis=1, keepdims=True)
                ds16 = (probs * (dp - d_row) * ATT_SCALE).astype(BF16)
                dq_parts.append(_mm(ds16, kc))
                dk_parts.append(_mm_tn(ds16, qs))
                dv_parts.append(_mm_tn(probs.astype(BF16), do_s))
                t_sink = d_row * p_sink
                for g in range(GROUP):
                    gsink_rows[GROUP * j + g] = gsink_rows[GROUP * j + g] - jnp.sum(
                        t_sink[g * BLOCK:(g + 1) * BLOCK], axis=0, keepdims=True)
            dq = _rope_bwd(_unstack_heads(dq_parts), cos4[rows], sa4[rows], sb4[rows])
            dproj[rows, O_QA:O_QA + W_QA] = dq.astype(BF16)
            dk_cur = dk_carry[...] + jnp.concatenate([p[BLOCK:] for p in dk_parts], axis=1)
            dv_cur = dv_carry[...] + jnp.concatenate([p[BLOCK:] for p in dv_parts], axis=1)
            dproj[rows, O_KA:O_KA + W_KA] = _rope_bwd(dk_cur, cos[rows], sa[rows], sb[rows]).astype(BF16)
            dproj[rows, O_VA:O_VA + W_VA] = dv_cur.astype(BF16)
            dk_carry[...] = jnp.concatenate([p[:BLOCK] for p in dk_parts], axis=1)
            dv_carry[...] = jnp.concatenate([p[:BLOCK] for p in dv_parts], axis=1)
        for hq in range(Q_HEADS):
            gsink_ref[hq:hq + 1, :] += jnp.broadcast_to(gsink_rows[hq], (1, 128))

        dproj[:, O_GA:O_GA + W_GA] = dga_ref[...]
        dproj[:, O_GB:O_GB + W_GB] = dgb_ref[...]

        r16 = r_ref[...].astype(BF16)
        wgu16 = wgu_ref[...].astype(BF16)
        z = _mm(r16, wgu16) + bg_ref[...]
        log_a = _log_sigmoid(z) / GLA_TAU
        bcum = _tri_mm(_chunk_tri(TRI_SLAB, False), log_a)
        qb = qb_ref[...]
        kb = kb_ref[...]
        vb = vb_ref[...]
        dob = dob_ref[...]
        e_b = jnp.exp(bcum)
        e_nb = jnp.exp(-bcum)
        qd_f = qb * GLA_SCALE * e_b
        ki_f = kb * e_nb
        qd_all = qd_f.astype(BF16)
        ki_all = ki_f.astype(BF16)
        tril = lax.broadcasted_iota(jnp.int32, (CHUNK, CHUNK), 0) >= lax.broadcasted_iota(jnp.int32, (CHUNK, CHUNK), 1)
        last_row = lax.broadcasted_iota(jnp.int32, (CHUNK, 1), 0) == CHUNK - 1
        for c in reversed(range(nch)):
            rows = slice(c * CHUNK, (c + 1) * CHUNK)
            b_c = bcum[rows]
            b_last = b_c[CHUNK - 1:CHUNK]
            e_e = jnp.exp(b_last - b_c)
            dec = jnp.exp(b_last)
            ke_f = kb[rows] * e_e
            ke = ke_f.astype(BF16)
            sp = st_ref[c]
            sp16 = sp.astype(BF16)
            dsn = ds_carry[...]
            dsn16 = dsn.astype(BF16)
            dqd_p, dki_p, dke_p, dv_p, dsp_p = [], [], [], [], []
            for h in range(GLA_HEADS):
                ks = slice(h * GLA_DK, (h + 1) * GLA_DK)
                vs = slice(h * GLA_DV, (h + 1) * GLA_DV)
                qd = qd_all[rows, ks]
                ki = ki_all[rows, ks]
                v_h = vb[rows, vs]
                do_h = dob[rows, vs]
                a16 = jnp.where(tril, _mm_nt(qd, ki), 0.0).astype(BF16)
                da16 = jnp.where(tril, _mm_nt(do_h, v_h), 0.0).astype(BF16)
                dv_p.append(_mm_tn(a16, do_h) + _mm_nt(ke[:, ks], dsn16[:, ks]))
                dqd_p.append(_mm(da16, ki) + _mm(do_h, sp16[:, ks]))
                dki_p.append(_mm_tn(da16, qd))
                dke_p.append(_mm(v_h, dsn16[:, ks]))
                dsp_p.append(_mm_tn(do_h, qd))
            dqd = jnp.concatenate(dqd_p, axis=1)
            dki = jnp.concatenate(dki_p, axis=1)
            dke = jnp.concatenate(dke_p, axis=1)
            ddec = jnp.sum(dsn * sp, axis=0, keepdims=True)
            ds_carry[...] = dsn * dec + jnp.concatenate(dsp_p, axis=1)
            dproj[rows, O_QB:O_QB + W_QB] = (dqd * e_b[rows] * GLA_SCALE).astype(BF16)
            dproj[rows, O_KB:O_KB + W_KB] = (dki * e_nb[rows] + dke * e_e).astype(BF16)
            dproj[rows, O_VB:O_VB + W_VB] = jnp.concatenate(dv_p, axis=1).astype(BF16)
            dke_ke = dke * ke_f
            d_b = dqd * qd_f[rows] - dki * ki_f[rows] - dke_ke
            d_bl = jnp.sum(dke_ke, axis=0, keepdims=True) + ddec * dec
            db_s[rows, :] = d_b + jnp.where(last_row, d_bl, 0.0)
        dlog_a = _tri_mm(_chunk_tri(TRI_SLAB, True), db_s[...])
        dz = dlog_a * (1.0 / GLA_TAU) * _sigmoid(-z)
        dz16 = dz.astype(BF16)
        gbg_ref[...] += jnp.sum(dz, axis=0, keepdims=True)
        gwgu_ref[...] += _mm_tn(r16, dz16)
        dproj[:, O_R:O_R + W_R] = _mm_nt(dz16, wgu16).astype(BF16)

        dp16 = dproj[...]
        gx_ref[...] = ALPHA * dh_ref[...] + _mm(dp16, win_ref[...])
        x16 = x_ref[...].astype(BF16)
        for h in range(2):
            dwin_ref[h, 0:D_PROJ, :] += _mm_tn(dp16, x16[:, h * HALF:(h + 1) * HALF])

    tile = lambda w: pl.BlockSpec((TM, w), lambda i: (nt - 1 - i, 0))
    whole = lambda shape: pl.BlockSpec(shape, lambda i: tuple(0 for _ in shape), pipeline_mode=pl.Buffered(1))
    prev_blk = pl.BlockSpec((BLOCK, W_KA), lambda i: (jnp.maximum((nt - 1 - i) * nblk - 1, 0), 0))
    in_specs = [
        tile(D_MODEL), tile(D_MODEL), tile(W_QA), tile(W_KA), tile(W_VA), prev_blk, prev_blk,
        tile(W_GA), tile(W_GA), tile(W_GB), tile(W_GB), tile(W_QB), tile(W_KB), tile(W_VB), tile(W_R),
        pl.BlockSpec((nch, GLA_DV, GLA_HEADS * GLA_DK), lambda i: (nt - 1 - i, 0, 0)),
        pl.BlockSpec((16, TM), lambda i: (0, nt - 1 - i)),
        whole((D_PROJ, D_MODEL)), whole((W_R, W_KB)), whole((1, W_KB)), pl.BlockSpec(memory_space=pltpu.SMEM),
    ]
    out_shape = (
        jax.ShapeDtypeStruct((s_len, D_MODEL), F32),
        jax.ShapeDtypeStruct((2, ACC_ROWS, HALF), F32),
        jax.ShapeDtypeStruct((Q_HEADS, 128), F32),
        jax.ShapeDtypeStruct((1, W_KB), F32),
        jax.ShapeDtypeStruct((W_R, W_KB), F32),
    )
    out_specs = (tile(D_MODEL), whole((2, ACC_ROWS, HALF)), whole((Q_HEADS, 128)), whole((1, W_KB)),
                 whole((W_R, W_KB)))
    scratch = [
        pltpu.VMEM((TM, D_PROJ), BF16), pltpu.VMEM((BLOCK, W_KA), F32), pltpu.VMEM((BLOCK, W_VA), F32),
        pltpu.VMEM((GLA_DV, GLA_HEADS * GLA_DK), F32), pltpu.VMEM((TM, W_KB), F32),
    ]
    return pl.pallas_call(
        body, name="bwd_mix", grid=(nt,), in_specs=in_specs, out_specs=out_specs, out_shape=out_shape,
        scratch_shapes=scratch,
        compiler_params=pltpu.CompilerParams(dimension_semantics=("arbitrary",), vmem_limit_bytes=VMEM_LIMIT),
    )(x, dh, qa, ka, va, ka, va, dattn, dga, dob, dgb, qb, kb, vb, r, st, cs, win, wgu, bg, sinks)


def _mesh_place():
    x, y, c = lax.axis_index("x"), lax.axis_index("y"), lax.axis_index("c")
    chips = [(1 - x, y), (x, 1 - y), (1 - x, 1 - y)]
    return x, y, c, chips


def _gather_weights_call(w_lin, w_out, wgu):
    def body(wlin_ref, wout_ref, wgu_ref, wt_ref, wout_full, wgu_all, blk, oblk, asm, send_sems, recv_sems):
        x, y, c, chips = _mesh_place()
        k_me = 2 * x + y
        asm[SHARD_IN - 4:SHARD_PAD, :] = jnp.zeros((SHARD_PAD - SHARD_IN + 4, D_MODEL), F32)
        asm[0:SHARD_IN, :] = wlin_ref[:, 0, :]
        for h in range(2):
            blk[k_me, h] = asm[0:SHARD_PAD, h * HALF:(h + 1) * HALF].astype(BF16)
            oblk[k_me, h] = wout_ref[:, h * HALF:(h + 1) * HALF].astype(BF16)
        wgu_all[k_me] = wgu_ref[...]

        parts = ((0, GATHER_SPLIT, 0, SHARD_OUT // 2), (GATHER_SPLIT, SHARD_PAD - GATHER_SPLIT, SHARD_OUT // 2, SHARD_OUT // 2))
        me_id, sib_id = (x, y, c), (x, y, 1 - c)
        nbr_x, nbr_y, diag = ((*chip, c) for chip in chips)
        k_x, k_y, k_d = (2 * chip[0] + chip[1] for chip in chips)

        def copies(k, hc, p, sem0, to):
            if p is None:
                refs = (blk.at[k, hc], oblk.at[k, hc])
            else:
                r0, rn, o0, on = parts[p]
                refs = (blk.at[k, hc, pl.ds(r0, rn), :], oblk.at[k, hc, pl.ds(o0, on), :])
            return [pltpu.make_async_remote_copy(src_ref=ref, dst_ref=ref, send_sem=send_sems.at[sem0 + n],
                                                 recv_sem=recv_sems.at[sem0 + n], device_id=to, device_id_type=MESH)
                    for n, ref in enumerate(refs)]

        def gu_copy(k, r, to):
            return pltpu.make_async_remote_copy(src_ref=wgu_all.at[k], dst_ref=wgu_all.at[k], send_sem=send_sems.at[18 + r],
                                                recv_sem=recv_sems.at[18 + r], device_id=to, device_id_type=MESH)

        def start(cps):
            for cp in cps:
                cp.start()
            return cps

        def landed(cps):
            for cp in cps:
                cp.wait_recv()

        started = start(copies(k_me, c, 0, 0, nbr_x) + copies(k_me, c, 1, 6, nbr_y)
                        + copies(k_me, c, 1, 2, nbr_x) + copies(k_me, c, 0, 4, nbr_y)
                        + [gu_copy(k_me, r, to) for r, to in enumerate((nbr_x, nbr_y, diag))])
        landed(copies(k_x, c, 0, 0, me_id))
        started += start(copies(k_x, c, 0, 8, nbr_y))
        landed(copies(k_y, c, 1, 6, me_id))
        started += start(copies(k_y, c, 1, 10, nbr_x))
        landed(copies(k_x, c, 1, 2, me_id))
        started += start(copies(k_x, c, None, 12, sib_id))
        landed(copies(k_y, c, 0, 4, me_id))
        started += start(copies(k_y, c, None, 14, sib_id))
        landed(copies(k_d, c, 0, 8, me_id) + copies(k_d, c, 1, 10, me_id))
        started += start(copies(k_d, c, None, 16, sib_id))
        for r, k_r in enumerate((k_x, k_y, k_d)):
            landed(copies(k_r, 1 - c, None, 12 + 2 * r, me_id))
            gu_copy(k_r, r, me_id).wait_recv()
        for cp in started:
            cp.wait_send()

        for k in range(N_CHIPS):
            for h in range(2):
                asm[k * SHARD_IN:k * SHARD_IN + SHARD_PAD, h * HALF:(h + 1) * HALF] = blk[k, h].astype(F32)
                wout_full[k * SHARD_OUT:(k + 1) * SHARD_OUT, h * HALF:(h + 1) * HALF] = oblk[k, h]
        wt_ref[...] = asm[0:D_PROJ, :].astype(BF16)

    vmem = pl.BlockSpec(memory_space=pltpu.VMEM)
    return pl.pallas_call(
        body, name="gather_weights",
        out_shape=(jax.ShapeDtypeStruct((D_PROJ, D_MODEL), BF16),
                   jax.ShapeDtypeStruct((D_MODEL, D_MODEL), BF16),
                   jax.ShapeDtypeStruct((N_CHIPS, W_R, W_KB // N_CHIPS), F32)),
        in_specs=[vmem, vmem, vmem], out_specs=(vmem, vmem, vmem),
        scratch_shapes=[pltpu.VMEM((N_CHIPS, 2, SHARD_PAD, HALF), BF16), pltpu.VMEM((N_CHIPS, 2, SHARD_OUT, HALF), BF16),
                        pltpu.VMEM((ACC_ROWS, D_MODEL), F32),
                        pltpu.SemaphoreType.DMA((21,)), pltpu.SemaphoreType.DMA((21,))],
        compiler_params=pltpu.CompilerParams(vmem_limit_bytes=VMEM_LIMIT),
    )(w_lin, w_out, wgu)


def _adamw(w, g, m, v):
    m = ADAM_B1 * m + (1.0 - ADAM_B1) * g
    v = ADAM_B2 * v + (1.0 - ADAM_B2) * (g * g)
    m_hat = m / (1.0 - ADAM_B1 ** ADAM_STEP)
    v_hat = v / (1.0 - ADAM_B2 ** ADAM_STEP)
    delta = -ADAM_LR * (m_hat / (jnp.sqrt(v_hat) + ADAM_EPS) + ADAM_WD * w)
    return delta, m, v


N_SMALL = 6


def _reduce_grads_call(g_in, g_out, small_grads):
    def body(gin_hbm, gout_hbm, g_lng, g_lnb, g_bg, g_nw, g_sink, g_wgu, loss_in, lin_in, fin_out, tot_out,
             a_in, a_out, b_in, b_out, c_in, s_in, s_out, r_in, r_out, f_in, f_out, pack_ref, tot_ref, pack_all,
             send_sems, recv_sems, local_sems):
        x, y, c, chips = _mesh_place()
        k_me = 2 * x + y
        me = 4 * x + 2 * y + c
        sibling = (x, y, 1 - c)

        pack_ref[...] = jnp.zeros_like(pack_ref)
        for a in range(8):
            pack_ref[P_LNG + a:P_LNG + a + 1, :] = g_lng[:, a * 128:(a + 1) * 128]
            pack_ref[P_LNB + a:P_LNB + a + 1, :] = g_lnb[:, a * 128:(a + 1) * 128]
        for a in range(2):
            pack_ref[P_BG + a:P_BG + a + 1, :] = g_bg[:, a * 128:(a + 1) * 128]
        pack_ref[P_NW:P_NW + 1, :] = g_nw[...]
        lane = lax.broadcasted_iota(jnp.int32, (1, 128), 1)
        sink_row = jnp.zeros((1, 128), F32)
        for hq in range(Q_HEADS):
            sink_row = jnp.where(lane == hq, g_sink[hq:hq + 1, :], sink_row)
        pack_ref[P_SINK:P_SINK + 1, :] = sink_row
        pack_ref[P_LOSS:P_LOSS + 1, :] = loss_in[...]
        gu_w = W_KB // N_CHIPS
        for k in range(N_CHIPS):
            pack_ref[P_GU + W_R * k:P_GU + W_R * (k + 1), 0:gu_w] = g_wgu[:, k * gu_w:(k + 1) * gu_w]
        pack_all[me] = pack_ref[...]
        small = []
        for mask in range(1, 8):
            peer = (x ^ (mask >> 2), y ^ ((mask >> 1) & 1), c ^ (mask & 1))
            small.append(pltpu.make_async_remote_copy(
                src_ref=pack_ref, dst_ref=pack_all.at[me], send_sem=send_sems.at[mask], recv_sem=recv_sems.at[mask],
                device_id=peer, device_id_type=MESH))
        for cp in small:
            cp.start()

        mine = [pltpu.make_async_copy(gin_hbm.at[c], a_in, local_sems.at[0]),
                pltpu.make_async_copy(gout_hbm.at[c], a_out, local_sems.at[1])]
        to_sib = [pltpu.make_async_remote_copy(
                      src_ref=gin_hbm.at[1 - c], dst_ref=b_in,
                      send_sem=send_sems.at[8], recv_sem=recv_sems.at[8], device_id=sibling, device_id_type=MESH),
                  pltpu.make_async_remote_copy(
                      src_ref=gout_hbm.at[1 - c], dst_ref=b_out,
                      send_sem=send_sems.at[9], recv_sem=recv_sems.at[9], device_id=sibling, device_id_type=MESH)]
        for cp in mine + to_sib:
            cp.start()
        for cp in mine:
            cp.wait()
        for cp in to_sib:
            cp.wait_recv()
        for k in range(N_CHIPS):
            rows = slice(k * SHARD_IN, k * SHARD_IN + SHARD_PAD)
            c_in[k] = a_in[rows, :] + b_in[rows, :]

        parts = ((0, GATHER_SPLIT, 0, SHARD_OUT // 2), (GATHER_SPLIT, SHARD_PAD - GATHER_SPLIT, SHARD_OUT // 2, SHARD_OUT // 2))
        nbr_x, nbr_y, _ = ((*chip, c) for chip in chips)
        k_x, k_y, k_d = (2 * chip[0] + chip[1] for chip in chips)

        def mine(k, p):
            r0, rn, o0, on = parts[p]
            return (c_in[k, pl.ds(r0, rn), :], a_out[k, pl.ds(o0, on), :] + b_out[k, pl.ds(o0, on), :])

        def message(m, p, to):
            _, rn, _, on = parts[p]
            return [pltpu.make_async_remote_copy(
                        src_ref=s_in.at[m, pl.ds(0, rn), :], dst_ref=r_in.at[m, pl.ds(0, rn), :],
                        send_sem=send_sems.at[10 + 2 * m], recv_sem=recv_sems.at[10 + 2 * m], device_id=to, device_id_type=MESH),
                    pltpu.make_async_remote_copy(
                        src_ref=s_out.at[m, pl.ds(0, on), :], dst_ref=r_out.at[m, pl.ds(0, on), :],
                        send_sem=send_sems.at[11 + 2 * m], recv_sem=recv_sems.at[11 + 2 * m], device_id=to, device_id_type=MESH)]

        def post(m, p, vals, to):
            _, rn, _, on = parts[p]
            s_in[m, 0:rn, :] = vals[0].astype(BF16)
            s_out[m, 0:on, :] = vals[1].astype(BF16)
            cps = message(m, p, to)
            for cp in cps:
                cp.start()
            return cps

        def take(m, p):
            _, rn, _, on = parts[p]
            for cp in message(m, p, (x, y, c)):
                cp.wait_recv()
            return (r_in[m, 0:rn, :].astype(F32), r_out[m, 0:on, :].astype(F32))

        def add(u, v):
            return (u[0] + v[0], u[1] + v[1])

        sent = post(1, 0, mine(k_d, 0), nbr_x) + post(4, 1, mine(k_d, 1), nbr_y)
        sent += post(0, 0, mine(k_x, 0), nbr_x) + post(3, 1, mine(k_y, 1), nbr_y)
        sent += post(5, 0, add(take(1, 0), mine(k_y, 0)), nbr_y)
        sent += post(2, 1, add(take(4, 1), mine(k_x, 1)), nbr_x)
        for p, direct, summed in ((0, 0, 5), (1, 3, 2)):
            r0, rn, o0, on = parts[p]
            total = add(add(mine(k_me, p), take(direct, p)), take(summed, p))
            f_in[c, r0:r0 + rn, :] = total[0]
            f_out[c, o0:o0 + on, :] = total[1]

        swap = [pltpu.make_async_remote_copy(
                    src_ref=f_in.at[c], dst_ref=f_in.at[c],
                    send_sem=send_sems.at[22], recv_sem=recv_sems.at[22], device_id=sibling, device_id_type=MESH),
                pltpu.make_async_remote_copy(
                    src_ref=f_out.at[c], dst_ref=f_out.at[c],
                    send_sem=send_sems.at[23], recv_sem=recv_sems.at[23], device_id=sibling, device_id_type=MESH)]
        for cp in swap:
            cp.start()

        for cp in small:
            cp.wait_recv()
        total = pack_all[0]
        for d in range(1, 8):
            total = total + pack_all[d]
        tot_ref[...] = total
        tot_out[0:P_GU, :] = total[0:P_GU]
        tot_out[P_GU:PACK_OWN_ROWS, :] = tot_ref[pl.ds(pl.multiple_of(P_GU + W_R * k_me, 8), W_R), :]

        other_in = pltpu.make_async_remote_copy(
            src_ref=f_in.at[1 - c], dst_ref=f_in.at[1 - c],
            send_sem=send_sems.at[22], recv_sem=recv_sems.at[22], device_id=sibling, device_id_type=MESH)
        other_out = pltpu.make_async_remote_copy(
            src_ref=f_out.at[1 - c], dst_ref=f_out.at[1 - c],
            send_sem=send_sems.at[23], recv_sem=recv_sems.at[23], device_id=sibling, device_id_type=MESH)
        other_in.wait_recv()
        other_out.wait_recv()
        for cp in small + to_sib + sent + swap:
            cp.wait_send()

        for h in range(2):
            lin_in[:, h * HALF:(h + 1) * HALF] = f_in[h, 0:SHARD_IN, :]
            fin_out[:, h * HALF:(h + 1) * HALF] = f_out[h]

    vmem = pl.BlockSpec(memory_space=pltpu.VMEM)
    hbm = pl.BlockSpec(memory_space=pl.ANY)
    return pl.pallas_call(
        body, name="reduce_grads",
        out_shape=(jax.ShapeDtypeStruct((SHARD_IN, D_MODEL), F32), jax.ShapeDtypeStruct((SHARD_OUT, D_MODEL), F32),
                   jax.ShapeDtypeStruct((PACK_OWN_ROWS, 128), F32)),
        in_specs=[hbm, hbm] + [vmem] * 7, out_specs=(vmem,) * 3,
        scratch_shapes=[
            pltpu.VMEM((ACC_ROWS, HALF), F32), pltpu.VMEM((N_CHIPS, SHARD_OUT, HALF), F32),
            pltpu.VMEM((ACC_ROWS, HALF), F32), pltpu.VMEM((N_CHIPS, SHARD_OUT, HALF), F32),
            pltpu.VMEM((N_CHIPS, SHARD_PAD, HALF), F32),
            pltpu.VMEM((6, GATHER_SPLIT, HALF), BF16), pltpu.VMEM((6, SHARD_OUT // 2, HALF), BF16),
            pltpu.VMEM((6, GATHER_SPLIT, HALF), BF16), pltpu.VMEM((6, SHARD_OUT // 2, HALF), BF16),
            pltpu.VMEM((2, SHARD_PAD, HALF), F32), pltpu.VMEM((2, SHARD_OUT, HALF), F32),
            pltpu.VMEM((PACK_ROWS, 128), F32), pltpu.VMEM((PACK_ROWS, 128), F32), pltpu.VMEM((8, PACK_ROWS, 128), F32),
            pltpu.SemaphoreType.DMA((24,)), pltpu.SemaphoreType.DMA((24,)), pltpu.SemaphoreType.DMA((2,)),
        ],
        compiler_params=pltpu.CompilerParams(vmem_limit_bytes=VMEM_LIMIT),
    )(g_in, g_out, *small_grads)


def _adamw_call(g_in, w_in, m_in, v_in, g_out, w_out, m_out, v_out, tot, small_params):
    steps = 4
    rows_out = SHARD_OUT // steps
    cols = D_MODEL // steps
    gu_w = W_KB // N_CHIPS

    def body(gi, wi, mi, vi, go, wo, mo, vo, tot, *rest):
        params = rest[:3 * N_SMALL]
        gi_o, di, nmi, nvi, go_o, do, nmo, nvo, loss_out = rest[3 * N_SMALL:3 * N_SMALL + 9]
        small_out = rest[3 * N_SMALL + 9:]

        @pl.when(pl.program_id(0) == 0)
        def _():
            loss_out[...] = tot[P_LOSS:P_LOSS + 1, :]
            g_outs = small_out[0:N_SMALL]
            for a in range(8):
                g_outs[0][:, a * 128:(a + 1) * 128] = tot[P_LNG + a:P_LNG + a + 1, :]
                g_outs[1][:, a * 128:(a + 1) * 128] = tot[P_LNB + a:P_LNB + a + 1, :]
            for a in range(2):
                g_outs[2][:, a * 128:(a + 1) * 128] = tot[P_BG + a:P_BG + a + 1, :]
            g_outs[3][...] = tot[P_NW:P_NW + 1, :]
            g_outs[4][...] = tot[P_SINK:P_SINK + 1, 0:Q_HEADS]
            g_outs[5][...] = tot[P_GU:PACK_OWN_ROWS, 0:gu_w]
            for n in range(N_SMALL):
                w_ref, m_ref, v_ref = params[3 * n:3 * n + 3]
                delta, new_m, new_v = _adamw(w_ref[...], g_outs[n][...], m_ref[...], v_ref[...])
                small_out[N_SMALL + n][...] = delta
                small_out[2 * N_SMALL + n][...] = new_m
                small_out[3 * N_SMALL + n][...] = new_v

        g = gi[...]
        delta, new_m, new_v = _adamw(wi[:, 0, :], g, mi[:, 0, :], vi[:, 0, :])
        gi_o[:, 0, :] = g
        di[:, 0, :] = delta
        nmi[:, 0, :] = new_m
        nvi[:, 0, :] = new_v
        g = go[...]
        go_o[...] = g
        do[...], nmo[...], nvo[...] = _adamw(wo[...], g, mo[...], vo[...])

    t_g = pl.BlockSpec((SHARD_IN, cols), lambda i: (0, i))
    t_in = pl.BlockSpec((SHARD_IN, 1, cols), lambda i: (0, 0, i))
    t_out = pl.BlockSpec((rows_out, D_MODEL), lambda i: (i, 0))
    s_in = jax.ShapeDtypeStruct((SHARD_IN, 1, D_MODEL), F32)
    s_out = jax.ShapeDtypeStruct((SHARD_OUT, D_MODEL), F32)
    whole = lambda shape: pl.BlockSpec(shape, lambda i: tuple(0 for _ in shape))
    small_specs = [whole(p.shape) for p in small_params]
    small_shapes = [jax.ShapeDtypeStruct(p.shape, F32) for p in small_params[0::3]] * 4
    return pl.pallas_call(
        body, name="adamw", grid=(steps,),
        in_specs=[t_g] + [t_in] * 3 + [t_out] * 4 + [whole(tot.shape)] + small_specs,
        out_specs=(t_in,) * 4 + (t_out,) * 4 + (whole((1, 128)),) + tuple(small_specs[0::3] * 4),
        out_shape=(s_in,) * 4 + (s_out,) * 4 + (jax.ShapeDtypeStruct((1, 128), F32),) + tuple(small_shapes),
        compiler_params=pltpu.CompilerParams(dimension_semantics=("arbitrary",)),
    )(g_in, w_in, m_in, v_in, g_out, w_out, m_out, v_out, tot, *small_params)


def _rope_tables(positions):
    half = 8
    inv_freq = 500000.0 ** (-jnp.arange(half, dtype=F32) / half)
    ang = inv_freq[:, None] * positions.astype(F32)[None, :]
    return jnp.concatenate([jnp.cos(ang), jnp.sin(ang)], axis=0)


def kernel(x, positions, w_in, gla_w_gate_up, gla_b_gate, attn_sinks, gla_norm_w, w_out, ln_g, ln_b, loss_target, m_w_in, m_gla_w_gate_up, m_gla_b_gate, m_attn_sinks, m_gla_norm_w, m_w_out, m_ln_g, m_ln_b, v_w_in, v_gla_w_gate_up, v_gla_b_gate, v_attn_sinks, v_gla_norm_w, v_w_out, v_ln_g, v_ln_b):
    def lin3(w):
        return jnp.transpose(w, (2, 0, 1))

    def unlin(w):
        return jnp.transpose(w, (1, 2, 0))

    win, wout, wgu_all = _gather_weights_call(lin3(w_in), w_out[0], gla_w_gate_up[0])
    wgu = jnp.transpose(wgu_all, (1, 0, 2)).reshape(W_R, W_KB)
    cs = _rope_tables(positions[0])
    sinks = attn_sinks[0]

    (qa, ka, va, qb, kb, vb, r, dattn, dga, dob, dgb, dh, st, g_wout, g_lng, g_lnb, g_nw, loss) = _fwd_call(
        x[0], loss_target[0], cs, win, wout, wgu, gla_b_gate, sinks, gla_norm_w, ln_g, ln_b)
    gx, g_win, g_sink, g_bg, g_wgu = _bwd_call(
        x[0], dh, qa, ka, va, dattn, dga, dob, dgb, qb, kb, vb, r, st, cs, win, wgu, gla_b_gate, sinks)

    g_wout_by_chip = g_wout.reshape(2, N_CHIPS, SHARD_OUT, HALF)
    small_params = []
    for group in ((ln_g, m_ln_g, v_ln_g), (ln_b, m_ln_b, v_ln_b), (gla_b_gate, m_gla_b_gate, v_gla_b_gate),
                  (gla_norm_w, m_gla_norm_w, v_gla_norm_w), (attn_sinks, m_attn_sinks, v_attn_sinks)):
        small_params += list(group)
    small_params += [gla_w_gate_up[0], m_gla_w_gate_up[0], v_gla_w_gate_up[0]]
    fin_in, fin_out, tot = _reduce_grads_call(g_win, g_wout_by_chip, (g_lng, g_lnb, g_bg, g_nw, g_sink, g_wgu, loss))
    fin_in, d_in, nm_in, nv_in, fin_out, d_out, nm_out, nv_out, loss_sum, *small_out = _adamw_call(
        fin_in, lin3(w_in), lin3(m_w_in), lin3(v_w_in), fin_out, w_out[0], m_w_out[0], v_w_out[0], tot, small_params)

    def unpack(kind, big_in, big_out):
        lng_, lnb_, bg_, nw_, sink_, gu_ = small_out[kind * N_SMALL:(kind + 1) * N_SMALL]
        return (unlin(big_in), gu_[None], bg_, sink_, nw_, big_out[None], lng_, lnb_)

    loss_total = loss_sum[0, 0]
    g_s, d_s, nm_s, nv_s = 0, 1, 2, 3
    return (loss_total, gx[None], *unpack(g_s, fin_in, fin_out), *unpack(d_s, d_in, d_out),
            *unpack(nm_s, nm_in, nm_out), *unpack(nv_s, nv_in, nv_out))
```

```python
import functools

import jax
import jax.numpy as jnp
import numpy as np
from jax import lax
from jax.experimental import pallas as pl
from jax.experimental.pallas import tpu as pltpu

F32 = jnp.float32
BF16 = jnp.bfloat16
MESH = pl.DeviceIdType.MESH

D_MODEL = 1024
N_CHIPS = 4
W_QA, W_KA, W_VA, W_GA, W_QB, W_KB, W_VB, W_GB, W_R = 512, 128, 128, 512, 256, 256, 512, 512, 16
O_QA = 0
O_KA = O_QA + W_QA
O_VA = O_KA + W_KA
O_GA = O_VA + W_VA
O_QB = O_GA + W_GA
O_KB = O_QB + W_QB
O_VB = O_KB + W_KB
O_GB = O_VB + W_VB
O_R = O_GB + W_GB
D_PROJ = O_R + W_R
SHARD_IN = D_PROJ // N_CHIPS
SHARD_OUT = D_MODEL // N_CHIPS
SHARD_PAD = 720
ACC_ROWS = -(-((N_CHIPS - 1) * SHARD_IN + SHARD_PAD) // 8) * 8
HALF = D_MODEL // 2
GATHER_SPLIT = 368

HEAD_A = 64
Q_HEADS = 8
KV_HEADS = 2
GROUP = 4
BLOCK = 128
GLA_HEADS = 4
GLA_DK = 64
GLA_DV = 128
CHUNK = 64
GLA_TAU = 16.0
EPS = 1e-5
ALPHA = 2.0 ** 0.25
ATT_SCALE = HEAD_A ** -0.5
GLA_SCALE = GLA_DK ** -0.5

ADAM_LR = 0.001
ADAM_B1 = 0.9
ADAM_B2 = 0.999
ADAM_EPS = 1e-08
ADAM_WD = 0.01
ADAM_STEP = 10

TM = 256
TRI_SLAB = 128
VMEM_LIMIT = 56 * 1024 * 1024

P_LNG, P_LNB, P_BG, P_NW, P_SINK, P_LOSS, P_GU = 0, 8, 16, 18, 19, 20, 24
PACK_ROWS = P_GU + N_CHIPS * 16
PACK_OWN_ROWS = P_GU + 16


def _mm(a, b):
    return jnp.dot(a, b, preferred_element_type=F32)


def _mm_nt(a, b):
    return lax.dot_general(a, b, (((1,), (1,)), ((), ())), preferred_element_type=F32)


def _mm_tn(a, b):
    return lax.dot_general(a, b, (((0,), (0,)), ((), ())), preferred_element_type=F32)


def _split3(a):
    hi = a.astype(BF16)
    r1 = a - hi.astype(F32)
    mid = r1.astype(BF16)
    lo = (r1 - mid.astype(F32)).astype(BF16)
    return hi, mid, lo


def _tri_mm(tri, a):
    slab = tri.shape[0]
    hi, mid, lo = _split3(a)
    return jnp.concatenate(
        [_mm(tri, hi[s:s + slab]) + _mm(tri, mid[s:s + slab]) + _mm(tri, lo[s:s + slab])
         for s in range(0, a.shape[0], slab)], axis=0)


def _chunk_tri(n, upper):
    r = lax.broadcasted_iota(jnp.int32, (n, n), 0)
    c = lax.broadcasted_iota(jnp.int32, (n, n), 1)
    same = (r >> 6) == (c >> 6)
    order = (c >= r) if upper else (c <= r)
    return jnp.where(same & order, 1.0, 0.0).astype(BF16)


def _rope(t, cos, sa, sb):
    w = t.shape[1]
    return t * cos + pltpu.roll(t, w - 8, 1) * sa + pltpu.roll(t, 8, 1) * sb


def _rope_tile(cs):
    row = lax.broadcasted_iota(jnp.int32, (16, 128), 0)
    d = lax.broadcasted_iota(jnp.int32, (16, 128), 1) & (HEAD_A - 1)
    hit = (d & 7) == (row & 7)
    is_cos = row < 8
    lo = d < 8
    hi = (d >= 8) & (d < 16)
    pick_cos = jnp.where(hit & is_cos & (lo | hi), 1.0, 0.0).astype(BF16)
    pick_sa = jnp.where(hit & ~is_cos & lo, -1.0, 0.0).astype(BF16)
    pick_sb = jnp.where(hit & ~is_cos & hi, 1.0, 0.0).astype(BF16)
    pieces = _split3(cs)

    def spread(pick):
        return _mm_tn(pieces[0], pick) + _mm_tn(pieces[1], pick) + _mm_tn(pieces[2], pick)

    d1 = lax.broadcasted_iota(jnp.int32, (1, 128), 1) & (HEAD_A - 1)
    return spread(pick_cos) + jnp.where(d1 < 16, 0.0, 1.0), spread(pick_sa), spread(pick_sb)


def _rope_bwd(d, cos, sa, sb):
    w = d.shape[1]
    return d * cos + pltpu.roll(d * sa, 8, 1) + pltpu.roll(d * sb, w - 8, 1)


def _log_sigmoid(z):
    return jnp.minimum(z, 0.0) - jnp.log1p(jnp.exp(-jnp.abs(z)))


def _sigmoid(z):
    return 1.0 / (1.0 + jnp.exp(-z))


def _attn_bias(has_prev):
    r = lax.broadcasted_iota(jnp.int32, (GROUP * BLOCK, 2 * BLOCK), 0) & (BLOCK - 1)
    k = lax.broadcasted_iota(jnp.int32, (GROUP * BLOCK, 2 * BLOCK), 1)
    first_key = jnp.where(has_prev, 0, BLOCK)
    return jnp.where((k > r) & (k <= r + BLOCK) & (k >= first_key), 0.0, -jnp.inf)


def _sink_col(sinks_ref, j):
    r = lax.broadcasted_iota(jnp.int32, (GROUP * BLOCK, 1), 0) >> 7
    col = jnp.full((GROUP * BLOCK, 1), sinks_ref[GROUP * j], F32)
    for g in range(1, GROUP):
        col = jnp.where(r == g, sinks_ref[GROUP * j + g], col)
    return col


def _stack_heads(t, j):
    return jnp.concatenate([t[:, (GROUP * j + g) * HEAD_A:(GROUP * j + g + 1) * HEAD_A] for g in range(GROUP)], axis=0)


def _unstack_heads(parts):
    return jnp.concatenate([parts[j][g * BLOCK:(g + 1) * BLOCK] for j in range(KV_HEADS) for g in range(GROUP)], axis=1)


def _softmax_block(qs, kc, bias, sink):
    s = _mm_nt(qs, kc) * ATT_SCALE + bias
    m = jnp.maximum(jnp.max(s, axis=1, keepdims=True), sink)
    p = jnp.exp(s - m)
    e_sink = jnp.exp(sink - m)
    inv = 1.0 / (jnp.sum(p, axis=1, keepdims=True) + e_sink)
    return p * inv, e_sink * inv


def _fwd_call(x, tgt, cs, win, wout, wgu, bg, sinks, nw, lng, lnb):
    s_len = x.shape[0]
    nt = s_len // TM
    nblk = TM // BLOCK
    nch = TM // CHUNK

    def body(x_ref, t_ref, cs_ref, win_ref, wout_ref, wgu_ref, bg_ref, sinks_ref, nw_ref,
             lng_ref, lnb_ref,
             qa_ref, ka_ref, va_ref, qb_ref, kb_ref, vb_ref, r_ref, dattn_ref, dga_ref, dob_ref, dgb_ref, dh_ref,
             st_ref, dwout_ref, glng_ref, glnb_ref, gnw_ref, loss_ref,
             kprev, vprev, state, attn_s, ga_s, ob_s, gb_s, cat_s):
        i = pl.program_id(0)

        @pl.when(i == 0)
        def _():
            kprev[...] = jnp.zeros_like(kprev)
            vprev[...] = jnp.zeros_like(vprev)
            state[...] = jnp.zeros_like(state)
            dwout_ref[...] = jnp.zeros_like(dwout_ref)
            glng_ref[...] = jnp.zeros_like(glng_ref)
            glnb_ref[...] = jnp.zeros_like(glnb_ref)
            gnw_ref[...] = jnp.zeros_like(gnw_ref)
            loss_ref[...] = jnp.zeros_like(loss_ref)

        x = x_ref[...]
        xb = x.astype(BF16)

        def proj(off, width):
            return _mm_nt(xb, win_ref[off:off + width, :])

        cos, sa, sb = _rope_tile(cs_ref[...])
        cos4, sa4, sb4 = (jnp.concatenate([t] * 4, axis=1) for t in (cos, sa, sb))
        qa = _rope(proj(O_QA, W_QA), cos4, sa4, sb4).astype(BF16)
        ka = _rope(proj(O_KA, W_KA), cos, sa, sb).astype(BF16)
        va = proj(O_VA, W_VA).astype(BF16)
        qa_ref[...] = qa
        ka_ref[...] = ka
        va_ref[...] = va
        ga_s[...] = proj(O_GA, W_GA)
        gb_s[...] = proj(O_GB, W_GB)

        bias_inner = _attn_bias(True)
        sink_cols = [_sink_col(sinks_ref, j) for j in range(KV_HEADS)]
        for b in range(nblk):
            rows = slice(b * BLOCK, (b + 1) * BLOCK)
            mask = _attn_bias(i > 0) if b == 0 else bias_inner
            k_cur = ka[rows]
            v_cur = va[rows]
            k_old = kprev[...] if b == 0 else ka[(b - 1) * BLOCK:b * BLOCK]
            v_old = vprev[...] if b == 0 else va[(b - 1) * BLOCK:b * BLOCK]
            outs = []
            for j in range(KV_HEADS):
                hs = slice(j * HEAD_A, (j + 1) * HEAD_A)
                kc = jnp.concatenate([k_old[:, hs], k_cur[:, hs]], axis=0)
                vc = jnp.concatenate([v_old[:, hs], v_cur[:, hs]], axis=0)
                probs, _ = _softmax_block(_stack_heads(qa[rows], j), kc, mask, sink_cols[j])
                outs.append(_mm(probs.astype(BF16), vc))
            attn_s[rows, :] = _unstack_heads(outs)
        kprev[...] = ka[(nblk - 1) * BLOCK:]
        vprev[...] = va[(nblk - 1) * BLOCK:]

        r = proj(O_R, W_R)
        r_ref[...] = r
        z = _mm(r.astype(BF16), wgu_ref[...].astype(BF16)) + bg_ref[...]
        log_a = _log_sigmoid(z) / GLA_TAU
        bcum = _tri_mm(_chunk_tri(TRI_SLAB, False), log_a)
        qb = proj(O_QB, W_QB)
        kb = proj(O_KB, W_KB)
        vb = proj(O_VB, W_VB).astype(BF16)
        qb_ref[...] = qb
        kb_ref[...] = kb
        vb_ref[...] = vb
        qd_all = (qb * GLA_SCALE * jnp.exp(bcum)).astype(BF16)
        ki_all = (kb * jnp.exp(-bcum)).astype(BF16)
        tril = lax.broadcasted_iota(jnp.int32, (CHUNK, CHUNK), 0) >= lax.broadcasted_iota(jnp.int32, (CHUNK, CHUNK), 1)
        for c in range(nch):
            rows = slice(c * CHUNK, (c + 1) * CHUNK)
            b_c = bcum[rows]
            b_last = b_c[CHUNK - 1:CHUNK]
            ke = (kb[rows] * jnp.exp(b_last - b_c)).astype(BF16)
            st = state[...]
            st_ref[c] = st
            st16 = st.astype(BF16)
            o_parts, u_parts = [], []
            for h in range(GLA_HEADS):
                ks = slice(h * GLA_DK, (h + 1) * GLA_DK)
                vs = slice(h * GLA_DV, (h + 1) * GLA_DV)
                qd = qd_all[rows, ks]
                v_h = vb[rows, vs]
                a = jnp.where(tril, _mm_nt(qd, ki_all[rows, ks]), 0.0)
                o_parts.append(_mm(a.astype(BF16), v_h) + _mm_nt(qd, st16[:, ks]))
                u_parts.append(_mm_tn(v_h, ke[:, ks]))
            ob_s[rows, :] = jnp.concatenate(o_parts, axis=1)
            state[...] = st * jnp.exp(b_last) + jnp.concatenate(u_parts, axis=1)

        ga = ga_s[...]
        sg_a = _sigmoid(ga)
        silu_a = ga * sg_a
        attn = attn_s[...]
        cat_s[:, :W_GA] = (attn * silu_a).astype(BF16)
        gb = gb_s[...]
        sg_b = _sigmoid(gb)
        silu_b = gb * sg_b
        nw = nw_ref[...]
        on_parts = []
        for h in range(GLA_HEADS):
            vs = slice(h * GLA_DV, (h + 1) * GLA_DV)
            o_h = ob_s[:, vs]
            rs = lax.rsqrt(jnp.mean(o_h * o_h, axis=1, keepdims=True) + EPS)
            on_parts.append(o_h * rs * nw)
        on = jnp.concatenate(on_parts, axis=1)
        cat_s[:, W_GA:] = (on * silu_b).astype(BF16)
        cat = cat_s[...]
        hres = ALPHA * x + _mm(cat, wout_ref[...])
        mu = jnp.mean(hres, axis=1, keepdims=True)
        hc = hres - mu
        rstd = lax.rsqrt(jnp.mean(hc * hc, axis=1, keepdims=True) + EPS)
        xhat = hc * rstd
        g_ln = lng_ref[...]
        err = xhat * g_ln + lnb_ref[...] - t_ref[...]
        loss_ref[...] += jnp.sum(err * err) * (0.5 / D_MODEL)
        dy = err * (1.0 / D_MODEL)
        glng_ref[...] += jnp.sum(dy * xhat, axis=0, keepdims=True)
        glnb_ref[...] += jnp.sum(dy, axis=0, keepdims=True)
        dxh = dy * g_ln
        dh = rstd * (dxh - jnp.mean(dxh, axis=1, keepdims=True) - xhat * jnp.mean(dxh * xhat, axis=1, keepdims=True))
        dh_ref[...] = dh
        dh16 = dh.astype(BF16)
        for h in range(2):
            dwout_ref[h] += _mm_tn(cat, dh16[:, h * HALF:(h + 1) * HALF])
        dcat = _mm_nt(dh16, wout_ref[...])

        d_a = dcat[:, :W_GA]
        dattn_ref[...] = (d_a * silu_a).astype(BF16)
        dga_ref[...] = (d_a * attn * (sg_a * (1.0 + ga * (1.0 - sg_a)))).astype(BF16)
        d_b = dcat[:, W_GA:]
        dgb_ref[...] = (d_b * on * (sg_b * (1.0 + gb * (1.0 - sg_b)))).astype(BF16)
        d_on = d_b * silu_b
        gnw = jnp.zeros((1, GLA_DV), F32)
        do_parts = []
        for h in range(GLA_HEADS):
            vs = slice(h * GLA_DV, (h + 1) * GLA_DV)
            o_h = ob_s[:, vs]
            rs = lax.rsqrt(jnp.mean(o_h * o_h, axis=1, keepdims=True) + EPS)
            d_on_h = d_on[:, vs]
            gnw = gnw + jnp.sum(d_on_h * o_h * rs, axis=0, keepdims=True)
            gg = d_on_h * nw
            do_parts.append(rs * gg - o_h * (rs * rs * rs) * jnp.mean(gg * o_h, axis=1, keepdims=True))
        gnw_ref[...] += gnw
        dob_ref[...] = jnp.concatenate(do_parts, axis=1).astype(BF16)

    tile = lambda w: pl.BlockSpec((TM, w), lambda i: (i, 0))
    whole = lambda shape: pl.BlockSpec(shape, lambda i: tuple(0 for _ in shape), pipeline_mode=pl.Buffered(1))
    out_shape = (
        jax.ShapeDtypeStruct((s_len, W_QA), BF16),
        jax.ShapeDtypeStruct((s_len, W_KA), BF16),
        jax.ShapeDtypeStruct((s_len, W_VA), BF16),
        jax.ShapeDtypeStruct((s_len, W_QB), F32),
        jax.ShapeDtypeStruct((s_len, W_KB), F32),
        jax.ShapeDtypeStruct((s_len, W_VB), BF16),
        jax.ShapeDtypeStruct((s_len, W_R), F32),
        jax.ShapeDtypeStruct((s_len, W_GA), BF16),
        jax.ShapeDtypeStruct((s_len, W_GA), BF16),
        jax.ShapeDtypeStruct((s_len, W_GB), BF16),
        jax.ShapeDtypeStruct((s_len, W_GB), BF16),
        jax.ShapeDtypeStruct((s_len, D_MODEL), F32),
        jax.ShapeDtypeStruct((s_len // CHUNK, GLA_DV, GLA_HEADS * GLA_DK), F32),
        jax.ShapeDtypeStruct((2, D_MODEL, HALF), F32),
        jax.ShapeDtypeStruct((1, D_MODEL), F32),
        jax.ShapeDtypeStruct((1, D_MODEL), F32),
        jax.ShapeDtypeStruct((1, GLA_DV), F32),
        jax.ShapeDtypeStruct((1, 128), F32),
    )
    out_specs = (
        tile(W_QA), tile(W_KA), tile(W_VA), tile(W_QB), tile(W_KB), tile(W_VB), tile(W_R),
        tile(W_GA), tile(W_GA), tile(W_GB), tile(W_GB), tile(D_MODEL),
        pl.BlockSpec((nch, GLA_DV, GLA_HEADS * GLA_DK), lambda i: (i, 0, 0)),
        whole((2, D_MODEL, HALF)), whole((1, D_MODEL)), whole((1, D_MODEL)), whole((1, GLA_DV)), whole((1, 128)),
    )
    in_specs = [
        tile(D_MODEL), tile(D_MODEL), pl.BlockSpec((16, TM), lambda i: (0, i)),
        whole((D_PROJ, D_MODEL)), whole((D_MODEL, D_MODEL)), whole((W_R, W_KB)), whole((1, W_KB)),
        pl.BlockSpec(memory_space=pltpu.SMEM), whole((1, GLA_DV)), whole((1, D_MODEL)), whole((1, D_MODEL)),
    ]
    scratch = [
        pltpu.VMEM((BLOCK, W_KA), BF16), pltpu.VMEM((BLOCK, W_VA), BF16),
        pltpu.VMEM((GLA_DV, GLA_HEADS * GLA_DK), F32),
        pltpu.VMEM((TM, W_GA), F32), pltpu.VMEM((TM, W_GA), F32), pltpu.VMEM((TM, W_GB), F32),
        pltpu.VMEM((TM, W_GB), F32), pltpu.VMEM((TM, D_MODEL), BF16),
    ]
    return pl.pallas_call(
        body, name="fwd_head", grid=(nt,), in_specs=in_specs, out_specs=out_specs, out_shape=out_shape,
        scratch_shapes=scratch,
        compiler_params=pltpu.CompilerParams(dimension_semantics=("arbitrary",), vmem_limit_bytes=VMEM_LIMIT),
    )(x, tgt, cs, win, wout, wgu, bg, sinks, nw, lng, lnb)


def _bwd_call(x, dh, qa, ka, va, dattn, dga, dob, dgb, qb, kb, vb, r, st, cs, win, wgu, bg, sinks):
    s_len = x.shape[0]
    nt = s_len // TM
    nblk = TM // BLOCK
    nch = TM // CHUNK

    def body(x_ref, dh_ref, qa_ref, ka_ref, va_ref, kap_ref, vap_ref, dattn_ref, dga_ref, dob_ref, dgb_ref,
             qb_ref, kb_ref, vb_ref, r_ref, st_ref, cs_ref, win_ref, wgu_ref, bg_ref, sinks_ref,
             gx_ref, dwin_ref, gsink_ref, gbg_ref, gwgu_ref,
             dproj, dk_carry, dv_carry, ds_carry, db_s):
        i = pl.program_id(0)
        t = nt - 1 - i

        @pl.when(i == 0)
        def _():
            dk_carry[...] = jnp.zeros_like(dk_carry)
            dv_carry[...] = jnp.zeros_like(dv_carry)
            ds_carry[...] = jnp.zeros_like(ds_carry)
            dwin_ref[...] = jnp.zeros_like(dwin_ref)
            gsink_ref[...] = jnp.zeros_like(gsink_ref)
            gbg_ref[...] = jnp.zeros_like(gbg_ref)
            gwgu_ref[...] = jnp.zeros_like(gwgu_ref)

        cos, sa, sb = _rope_tile(cs_ref[...])
        cos4, sa4, sb4 = (jnp.concatenate([v] * 4, axis=1) for v in (cos, sa, sb))

        qa = qa_ref[...]
        ka = ka_ref[...]
        va = va_ref[...]
        dattn = dattn_ref[...]
        gsink_rows = [jnp.zeros((1, 1), F32) for _ in range(Q_HEADS)]
        bias_inner = _attn_bias(True)
        sink_cols = [_sink_col(sinks_ref, j) for j in range(KV_HEADS)]
        for b in reversed(range(nblk)):
            rows = slice(b * BLOCK, (b + 1) * BLOCK)
            mask = _attn_bias(t > 0) if b == 0 else bias_inner
            k_cur = ka[rows]
            v_cur = va[rows]
            k_old = kap_ref[...] if b == 0 else ka[(b - 1) * BLOCK:b * BLOCK]
            v_old = vap_ref[...] if b == 0 else va[(b - 1) * BLOCK:b * BLOCK]
            dq_parts, dk_parts, dv_parts = [], [], []
            for j in range(KV_HEADS):
                hs = slice(j * HEAD_A, (j + 1) * HEAD_A)
                kc = jnp.concatenate([k_old[:, hs], k_cur[:, hs]], axis=0)
                vc = jnp.concatenate([v_old[:, hs], v_cur[:, hs]], axis=0)
                qs = _stack_heads(qa[rows], j)
                do_s = _stack_heads(dattn[rows], j)
                probs, p_sink = _softmax_block(qs, kc, mask, sink_cols[j])
                dp = _mm_nt(do_s, vc)
                d_row = jnp.sum(probs * dp, axis=1, keepdims=True)
                ds16 = (probs * (dp - d_row) * ATT_SCALE).astype(BF16)
                dq_parts.append(_mm(ds16, kc))
                dk_parts.append(_mm_tn(ds16, qs))
                dv_parts.append(_mm_tn(probs.astype(BF16), do_s))
                t_sink = d_row * p_sink
                for g in range(GROUP):
                    gsink_rows[GROUP * j + g] = gsink_rows[GROUP * j + g] - jnp.sum(
                        t_sink[g * BLOCK:(g + 1) * BLOCK], axis=0, keepdims=True)
            dq = _rope_bwd(_unstack_heads(dq_parts), cos4[rows], sa4[rows], sb4[rows])
            dproj[rows, O_QA:O_QA + W_QA] = dq.astype(BF16)
            dk_cur = dk_carry[...] + jnp.concatenate([p[BLOCK:] for p in dk_parts], axis=1)
            dv_cur = dv_carry[...] + jnp.concatenate([p[BLOCK:] for p in dv_parts], axis=1)
            dproj[rows, O_KA:O_KA + W_KA] = _rope_bwd(dk_cur, cos[rows], sa[rows], sb[rows]).astype(BF16)
            dproj[rows, O_VA:O_VA + W_VA] = dv_cur.astype(BF16)
            dk_carry[...] = jnp.concatenate([p[:BLOCK] for p in dk_parts], axis=1)
            dv_carry[...] = jnp.concatenate([p[:BLOCK] for p in dv_parts], axis=1)
        for hq in range(Q_HEADS):
            gsink_ref[hq:hq + 1, :] += jnp.broadcast_to(gsink_rows[hq], (1, 128))

        dproj[:, O_GA:O_GA + W_GA] = dga_ref[...]
        dproj[:, O_GB:O_GB + W_GB] = dgb_ref[...]

        r16 = r_ref[...].astype(BF16)
        wgu16 = wgu_ref[...].astype(BF16)
        z = _mm(r16, wgu16) + bg_ref[...]
        log_a = _log_sigmoid(z) / GLA_TAU
        bcum = _tri_mm(_chunk_tri(TRI_SLAB, False), log_a)
        qb = qb_ref[...]
        kb = kb_ref[...]
        vb = vb_ref[...]
        dob = dob_ref[...]
        e_b = jnp.exp(bcum)
        e_nb = jnp.exp(-bcum)
        qd_f = qb * GLA_SCALE * e_b
        ki_f = kb * e_nb
        qd_all = qd_f.astype(BF16)
        ki_all = ki_f.astype(BF16)
        tril = lax.broadcasted_iota(jnp.int32, (CHUNK, CHUNK), 0) >= lax.broadcasted_iota(jnp.int32, (CHUNK, CHUNK), 1)
        last_row = lax.broadcasted_iota(jnp.int32, (CHUNK, 1), 0) == CHUNK - 1
        for c in reversed(range(nch)):
            rows = slice(c * CHUNK, (c + 1) * CHUNK)
            b_c = bcum[rows]
            b_last = b_c[CHUNK - 1:CHUNK]
            e_e = jnp.exp(b_last - b_c)
            dec = jnp.exp(b_last)
            ke_f = kb[rows] * e_e
            ke = ke_f.astype(BF16)
            sp = st_ref[c]
            sp16 = sp.astype(BF16)
            dsn = ds_carry[...]
            dsn16 = dsn.astype(BF16)
            dqd_p, dki_p, dke_p, dv_p, dsp_p = [], [], [], [], []
            for h in range(GLA_HEADS):
                ks = slice(h * GLA_DK, (h + 1) * GLA_DK)
                vs = slice(h * GLA_DV, (h + 1) * GLA_DV)
                qd = qd_all[rows, ks]
                ki = ki_all[rows, ks]
                v_h = vb[rows, vs]
                do_h = dob[rows, vs]
                a16 = jnp.where(tril, _mm_nt(qd, ki), 0.0).astype(BF16)
                da16 = jnp.where(tril, _mm_nt(do_h, v_h), 0.0).astype(BF16)
                dv_p.append(_mm_tn(a16, do_h) + _mm_nt(ke[:, ks], dsn16[:, ks]))
                dqd_p.append(_mm(da16, ki) + _mm(do_h, sp16[:, ks]))
                dki_p.append(_mm_tn(da16, qd))
                dke_p.append(_mm(v_h, dsn16[:, ks]))
                dsp_p.append(_mm_tn(do_h, qd))
            dqd = jnp.concatenate(dqd_p, axis=1)
            dki = jnp.concatenate(dki_p, axis=1)
            dke = jnp.concatenate(dke_p, axis=1)
            ddec = jnp.sum(dsn * sp, axis=0, keepdims=True)
            ds_carry[...] = dsn * dec + jnp.concatenate(dsp_p, axis=1)
            dproj[rows, O_QB:O_QB + W_QB] = (dqd * e_b[rows] * GLA_SCALE).astype(BF16)
            dproj[rows, O_KB:O_KB + W_KB] = (dki * e_nb[rows] + dke * e_e).astype(BF16)
            dproj[rows, O_VB:O_VB + W_VB] = jnp.concatenate(dv_p, axis=1).astype(BF16)
            dke_ke = dke * ke_f
            d_b = dqd * qd_f[rows] - dki * ki_f[rows] - dke_ke
            d_bl = jnp.sum(dke_ke, axis=0, keepdims=True) + ddec * dec
            db_s[rows, :] = d_b + jnp.where(last_row, d_bl, 0.0)
        dlog_a = _tri_mm(_chunk_tri(TRI_SLAB, True), db_s[...])
        dz = dlog_a * (1.0 / GLA_TAU) * _sigmoid(-z)
        dz16 = dz.astype(BF16)
        gbg_ref[...] += jnp.sum(dz, axis=0, keepdims=True)
        gwgu_ref[...] += _mm_tn(r16, dz16)
        dproj[:, O_R:O_R + W_R] = _mm_nt(dz16, wgu16).astype(BF16)

        dp16 = dproj[...]
        gx_ref[...] = ALPHA * dh_ref[...] + _mm(dp16, win_ref[...])
        x16 = x_ref[...].astype(BF16)
        for h in range(2):
            dwin_ref[h, 0:D_PROJ, :] += _mm_tn(dp16, x16[:, h * HALF:(h + 1) * HALF])

    tile = lambda w: pl.BlockSpec((TM, w), lambda i: (nt - 1 - i, 0))
    whole = lambda shape: pl.BlockSpec(shape, lambda i: tuple(0 for _ in shape), pipeline_mode=pl.Buffered(1))
    prev_blk = pl.BlockSpec((BLOCK, W_KA), lambda i: (jnp.maximum((nt - 1 - i) * nblk - 1, 0), 0))
    in_specs = [
        tile(D_MODEL), tile(D_MODEL), tile(W_QA), tile(W_KA), tile(W_VA), prev_blk, prev_blk,
        tile(W_GA), tile(W_GA), tile(W_GB), tile(W_GB), tile(W_QB), tile(W_KB), tile(W_VB), tile(W_R),
        pl.BlockSpec((nch, GLA_DV, GLA_HEADS * GLA_DK), lambda i: (nt - 1 - i, 0, 0)),
        pl.BlockSpec((16, TM), lambda i: (0, nt - 1 - i)),
        whole((D_PROJ, D_MODEL)), whole((W_R, W_KB)), whole((1, W_KB)), pl.BlockSpec(memory_space=pltpu.SMEM),
    ]
    out_shape = (
        jax.ShapeDtypeStruct((s_len, D_MODEL), F32),
        jax.ShapeDtypeStruct((2, ACC_ROWS, HALF), F32),
        jax.ShapeDtypeStruct((Q_HEADS, 128), F32),
        jax.ShapeDtypeStruct((1, W_KB), F32),
        jax.ShapeDtypeStruct((W_R, W_KB), F32),
    )
    out_specs = (tile(D_MODEL), whole((2, ACC_ROWS, HALF)), whole((Q_HEADS, 128)), whole((1, W_KB)),
                 whole((W_R, W_KB)))
    scratch = [
        pltpu.VMEM((TM, D_PROJ), BF16), pltpu.VMEM((BLOCK, W_KA), F32), pltpu.VMEM((BLOCK, W_VA), F32),
        pltpu.VMEM((GLA_DV, GLA_HEADS * GLA_DK), F32), pltpu.VMEM((TM, W_KB), F32),
    ]
    return pl.pallas_call(
        body, name="bwd_mix", grid=(nt,), in_specs=in_specs, out_specs=out_specs, out_shape=out_shape,
        scratch_shapes=scratch,
        compiler_params=pltpu.CompilerParams(dimension_semantics=("arbitrary",), vmem_limit_bytes=VMEM_LIMIT),
    )(x, dh, qa, ka, va, ka, va, dattn, dga, dob, dgb, qb, kb, vb, r, st, cs, win, wgu, bg, sinks)


def _mesh_place():
    x, y, c = lax.axis_index("x"), lax.axis_index("y"), lax.axis_index("c")
    chips = [(1 - x, y), (x, 1 - y), (1 - x, 1 - y)]
    return x, y, c, chips


def _gather_weights_call(w_lin, w_out, wgu):
    def body(wlin_ref, wout_ref, wgu_ref, wt_ref, wout_full, wgu_all, blk, oblk, asm, send_sems, recv_sems):
        x, y, c, chips = _mesh_place()
        k_me = 2 * x + y
        asm[SHARD_IN - 4:SHARD_PAD, :] = jnp.zeros((SHARD_PAD - SHARD_IN + 4, D_MODEL), F32)
        asm[0:SHARD_IN, :] = wlin_ref[:, 0, :]
        for h in range(2):
            blk[k_me, h] = asm[0:SHARD_PAD, h * HALF:(h + 1) * HALF].astype(BF16)
            oblk[k_me, h] = wout_ref[:, h * HALF:(h + 1) * HALF].astype(BF16)
        wgu_all[k_me] = wgu_ref[...]

        parts = ((0, GATHER_SPLIT, 0, SHARD_OUT // 2), (GATHER_SPLIT, SHARD_PAD - GATHER_SPLIT, SHARD_OUT // 2, SHARD_OUT // 2))
        me_id, sib_id = (x, y, c), (x, y, 1 - c)
        nbr_x, nbr_y, diag = ((*chip, c) for chip in chips)
        k_x, k_y, k_d = (2 * chip[0] + chip[1] for chip in chips)

        def copies(k, hc, p, sem0, to):
            if p is None:
                refs = (blk.at[k, hc], oblk.at[k, hc])
            else:
                r0, rn, o0, on = parts[p]
                refs = (blk.at[k, hc, pl.ds(r0, rn), :], oblk.at[k, hc, pl.ds(o0, on), :])
            return [pltpu.make_async_remote_copy(src_ref=ref, dst_ref=ref, send_sem=send_sems.at[sem0 + n],
                                                 recv_sem=recv_sems.at[sem0 + n], device_id=to, device_id_type=MESH)
                    for n, ref in enumerate(refs)]

        def gu_copy(k, r, to):
            return pltpu.make_async_remote_copy(src_ref=wgu_all.at[k], dst_ref=wgu_all.at[k], send_sem=send_sems.at[18 + r],
                                                recv_sem=recv_sems.at[18 + r], device_id=to, device_id_type=MESH)

        def start(cps):
            for cp in cps:
                cp.start()
            return cps

        def landed(cps):
            for cp in cps:
                cp.wait_recv()

        started = start(copies(k_me, c, 0, 0, nbr_x) + copies(k_me, c, 1, 6, nbr_y)
                        + copies(k_me, c, 1, 2, nbr_x) + copies(k_me, c, 0, 4, nbr_y)
                        + [gu_copy(k_me, r, to) for r, to in enumerate((nbr_x, nbr_y, diag))])
        landed(copies(k_x, c, 0, 0, me_id))
        started += start(copies(k_x, c, 0, 8, nbr_y))
        landed(copies(k_y, c, 1, 6, me_id))
        started += start(copies(k_y, c, 1, 10, nbr_x))
        landed(copies(k_x, c, 1, 2, me_id))
        started += start(copies(k_x, c, None, 12, sib_id))
        landed(copies(k_y, c, 0, 4, me_id))
        started += start(copies(k_y, c, None, 14, sib_id))
        landed(copies(k_d, c, 0, 8, me_id) + copies(k_d, c, 1, 10, me_id))
        started += start(copies(k_d, c, None, 16, sib_id))
        for r, k_r in enumerate((k_x, k_y, k_d)):
            landed(copies(k_r, 1 - c, None, 12 + 2 * r, me_id))
            gu_copy(k_r, r, me_id).wait_recv()
        for cp in started:
            cp.wait_send()

        for k in range(N_CHIPS):
            for h in range(2):
                asm[k * SHARD_IN:k * SHARD_IN + SHARD_PAD, h * HALF:(h + 1) * HALF] = blk[k, h].astype(F32)
                wout_full[k * SHARD_OUT:(k + 1) * SHARD_OUT, h * HALF:(h + 1) * HALF] = oblk[k, h]
        wt_ref[...] = asm[0:D_PROJ, :].astype(BF16)

    vmem = pl.BlockSpec(memory_space=pltpu.VMEM)
    return pl.pallas_call(
        body, name="gather_weights",
        out_shape=(jax.ShapeDtypeStruct((D_PROJ, D_MODEL), BF16),
                   jax.ShapeDtypeStruct((D_MODEL, D_MODEL), BF16),
                   jax.ShapeDtypeStruct((N_CHIPS, W_R, W_KB // N_CHIPS), F32)),
        in_specs=[vmem, vmem, vmem], out_specs=(vmem, vmem, vmem),
        scratch_shapes=[pltpu.VMEM((N_CHIPS, 2, SHARD_PAD, HALF), BF16), pltpu.VMEM((N_CHIPS, 2, SHARD_OUT, HALF), BF16),
                        pltpu.VMEM((ACC_ROWS, D_MODEL), F32),
                        pltpu.SemaphoreType.DMA((21,)), pltpu.SemaphoreType.DMA((21,))],
        compiler_params=pltpu.CompilerParams(vmem_limit_bytes=VMEM_LIMIT),
    )(w_lin, w_out, wgu)


def _adamw(w, g, m, v):
    m = ADAM_B1 * m + (1.0 - ADAM_B1) * g
    v = ADAM_B2 * v + (1.0 - ADAM_B2) * (g * g)
    m_hat = m / (1.0 - ADAM_B1 ** ADAM_STEP)
    v_hat = v / (1.0 - ADAM_B2 ** ADAM_STEP)
    delta = -ADAM_LR * (m_hat / (jnp.sqrt(v_hat) + ADAM_EPS) + ADAM_WD * w)
    return delta, m, v


N_SMALL = 6


def _reduce_grads_call(g_in, g_out, small_grads):
    def body(gin_hbm, gout_hbm, g_lng, g_lnb, g_bg, g_nw, g_sink, g_wgu, loss_in, lin_in, fin_out, tot_out,
             a_in, a_out, b_in, b_out, c_in, s_in, s_out, r_in, r_out, f_in, f_out, pack_ref, tot_ref, pack_all,
             send_sems, recv_sems, local_sems):
        x, y, c, chips = _mesh_place()
        k_me = 2 * x + y
        me = 4 * x + 2 * y + c
        sibling = (x, y, 1 - c)

        pack_ref[...] = jnp.zeros_like(pack_ref)
        for a in range(8):
            pack_ref[P_LNG + a:P_LNG + a + 1, :] = g_lng[:, a * 128:(a + 1) * 128]
            pack_ref[P_LNB + a:P_LNB + a + 1, :] = g_lnb[:, a * 128:(a + 1) * 128]
        for a in range(2):
            pack_ref[P_BG + a:P_BG + a + 1, :] = g_bg[:, a * 128:(a + 1) * 128]
        pack_ref[P_NW:P_NW + 1, :] = g_nw[...]
        lane = lax.broadcasted_iota(jnp.int32, (1, 128), 1)
        sink_row = jnp.zeros((1, 128), F32)
        for hq in range(Q_HEADS):
            sink_row = jnp.where(lane == hq, g_sink[hq:hq + 1, :], sink_row)
        pack_ref[P_SINK:P_SINK + 1, :] = sink_row
        pack_ref[P_LOSS:P_LOSS + 1, :] = loss_in[...]
        gu_w = W_KB // N_CHIPS
        for k in range(N_CHIPS):
            pack_ref[P_GU + W_R * k:P_GU + W_R * (k + 1), 0:gu_w] = g_wgu[:, k * gu_w:(k + 1) * gu_w]
        pack_all[me] = pack_ref[...]
        small = []
        for mask in range(1, 8):
            peer = (x ^ (mask >> 2), y ^ ((mask >> 1) & 1), c ^ (mask & 1))
            small.append(pltpu.make_async_remote_copy(
                src_ref=pack_ref, dst_ref=pack_all.at[me], send_sem=send_sems.at[mask], recv_sem=recv_sems.at[mask],
                device_id=peer, device_id_type=MESH))
        for cp in small:
            cp.start()

        loads = [pltpu.make_async_copy(gin_hbm.at[c], a_in, local_sems.at[0]),
                 pltpu.make_async_copy(gout_hbm.at[c], a_out, local_sems.at[1])]
        to_sib = [pltpu.make_async_remote_copy(
                      src_ref=gin_hbm.at[1 - c], dst_ref=b_in,
                      send_sem=send_sems.at[8], recv_sem=recv_sems.at[8], device_id=sibling, device_id_type=MESH),
                  pltpu.make_async_remote_copy(
                      src_ref=gout_hbm.at[1 - c], dst_ref=b_out,
                      send_sem=send_sems.at[9], recv_sem=recv_sems.at[9], device_id=sibling, device_id_type=MESH)]
        for cp in (loads[1], to_sib[1], loads[0], to_sib[0]):
            cp.start()

        parts = ((0, GATHER_SPLIT, 0, SHARD_OUT // 2), (GATHER_SPLIT, SHARD_PAD - GATHER_SPLIT, SHARD_OUT // 2, SHARD_OUT // 2))
        nbr_x, nbr_y, _ = ((*chip, c) for chip in chips)
        k_x, k_y, k_d = (2 * chip[0] + chip[1] for chip in chips)

        def rows_of(p, w):
            r0, rn, o0, on = parts[p]
            return (o0, on) if w else (r0, rn)

        def mine(k, p, w):
            r0, rn = rows_of(p, w)
            if w:
                return a_out[k, pl.ds(r0, rn), :] + b_out[k, pl.ds(r0, rn), :]
            return c_in[k, pl.ds(r0, rn), :]

        def message(m, p, w, to):
            rn = rows_of(p, w)[1]
            stage, land = (s_out, r_out) if w else (s_in, r_in)
            return pltpu.make_async_remote_copy(
                src_ref=stage.at[m, pl.ds(0, rn), :], dst_ref=land.at[m, pl.ds(0, rn), :],
                send_sem=send_sems.at[10 + 2 * m + w], recv_sem=recv_sems.at[10 + 2 * m + w],
                device_id=to, device_id_type=MESH)

        def post(m, p, w, val, to):
            rn = rows_of(p, w)[1]
            stage = s_out if w else s_in
            stage[m, 0:rn, :] = val.astype(BF16)
            cp = message(m, p, w, to)
            cp.start()
            return [cp]

        def take(m, p, w):
            rn = rows_of(p, w)[1]
            message(m, p, w, (x, y, c)).wait_recv()
            land = r_out if w else r_in
            return land[m, 0:rn, :].astype(F32)

        sent = []
        for w in (1, 0):
            loads[w].wait()
            to_sib[w].wait_recv()
            if w == 0:
                for k in range(N_CHIPS):
                    rows = slice(k * SHARD_IN, k * SHARD_IN + SHARD_PAD)
                    c_in[k] = a_in[rows, :] + b_in[rows, :]
            sent += post(1, 0, w, mine(k_d, 0, w), nbr_x) + post(4, 1, w, mine(k_d, 1, w), nbr_y)
            sent += post(0, 0, w, mine(k_x, 0, w), nbr_x) + post(3, 1, w, mine(k_y, 1, w), nbr_y)
        for w in (1, 0):
            sent += post(5, 0, w, take(1, 0, w) + mine(k_y, 0, w), nbr_y)
            sent += post(2, 1, w, take(4, 1, w) + mine(k_x, 1, w), nbr_x)
        for w in (1, 0):
            for p, direct, summed in ((0, 0, 5), (1, 3, 2)):
                r0, rn = rows_of(p, w)
                total = mine(k_me, p, w) + take(direct, p, w) + take(summed, p, w)
                if w:
                    f_out[c, r0:r0 + rn, :] = total
                else:
                    f_in[c, r0:r0 + rn, :] = total

        swap = [pltpu.make_async_remote_copy(
                    src_ref=f_in.at[c], dst_ref=f_in.at[c],
                    send_sem=send_sems.at[22], recv_sem=recv_sems.at[22], device_id=sibling, device_id_type=MESH),
                pltpu.make_async_remote_copy(
                    src_ref=f_out.at[c], dst_ref=f_out.at[c],
                    send_sem=send_sems.at[23], recv_sem=recv_sems.at[23], device_id=sibling, device_id_type=MESH)]
        for cp in swap:
            cp.start()

        for cp in small:
            cp.wait_recv()
        total = pack_all[0]
        for d in range(1, 8):
            total = total + pack_all[d]
        tot_ref[...] = total
        tot_out[0:P_GU, :] = total[0:P_GU]
        tot_out[P_GU:PACK_OWN_ROWS, :] = tot_ref[pl.ds(pl.multiple_of(P_GU + W_R * k_me, 8), W_R), :]

        other_in = pltpu.make_async_remote_copy(
            src_ref=f_in.at[1 - c], dst_ref=f_in.at[1 - c],
            send_sem=send_sems.at[22], recv_sem=recv_sems.at[22], device_id=sibling, device_id_type=MESH)
        other_out = pltpu.make_async_remote_copy(
            src_ref=f_out.at[1 - c], dst_ref=f_out.at[1 - c],
            send_sem=send_sems.at[23], recv_sem=recv_sems.at[23], device_id=sibling, device_id_type=MESH)
        other_in.wait_recv()
        other_out.wait_recv()
        for cp in small + to_sib + sent + swap:
            cp.wait_send()

        for h in range(2):
            lin_in[:, h * HALF:(h + 1) * HALF] = f_in[h, 0:SHARD_IN, :]
            fin_out[:, h * HALF:(h + 1) * HALF] = f_out[h]

    vmem = pl.BlockSpec(memory_space=pltpu.VMEM)
    hbm = pl.BlockSpec(memory_space=pl.ANY)
    return pl.pallas_call(
        body, name="reduce_grads",
        out_shape=(jax.ShapeDtypeStruct((SHARD_IN, D_MODEL), F32), jax.ShapeDtypeStruct((SHARD_OUT, D_MODEL), F32),
                   jax.ShapeDtypeStruct((PACK_OWN_ROWS, 128), F32)),
        in_specs=[hbm, hbm] + [vmem] * 7, out_specs=(vmem,) * 3,
        scratch_shapes=[
            pltpu.VMEM((ACC_ROWS, HALF), F32), pltpu.VMEM((N_CHIPS, SHARD_OUT, HALF), F32),
            pltpu.VMEM((ACC_ROWS, HALF), F32), pltpu.VMEM((N_CHIPS, SHARD_OUT, HALF), F32),
            pltpu.VMEM((N_CHIPS, SHARD_PAD, HALF), F32),
            pltpu.VMEM((6, GATHER_SPLIT, HALF), BF16), pltpu.VMEM((6, SHARD_OUT // 2, HALF), BF16),
            pltpu.VMEM((6, GATHER_SPLIT, HALF), BF16), pltpu.VMEM((6, SHARD_OUT // 2, HALF), BF16),
            pltpu.VMEM((2, SHARD_PAD, HALF), F32), pltpu.VMEM((2, SHARD_OUT, HALF), F32),
            pltpu.VMEM((PACK_ROWS, 128), F32), pltpu.VMEM((PACK_ROWS, 128), F32), pltpu.VMEM((8, PACK_ROWS, 128), F32),
            pltpu.SemaphoreType.DMA((24,)), pltpu.SemaphoreType.DMA((24,)), pltpu.SemaphoreType.DMA((2,)),
        ],
        compiler_params=pltpu.CompilerParams(vmem_limit_bytes=VMEM_LIMIT),
    )(g_in, g_out, *small_grads)


def _adamw_call(g_in, w_in, m_in, v_in, g_out, w_out, m_out, v_out, tot, small_params):
    steps = 4
    rows_out = SHARD_OUT // steps
    cols = D_MODEL // steps
    gu_w = W_KB // N_CHIPS

    def body(gi, wi, mi, vi, go, wo, mo, vo, tot, *rest):
        params = rest[:3 * N_SMALL]
        gi_o, di, nmi, nvi, go_o, do, nmo, nvo, loss_out = rest[3 * N_SMALL:3 * N_SMALL + 9]
        small_out = rest[3 * N_SMALL + 9:]

        @pl.when(pl.program_id(0) == 0)
        def _():
            loss_out[...] = tot[P_LOSS:P_LOSS + 1, :]
            g_outs = small_out[0:N_SMALL]
            for a in range(8):
                g_outs[0][:, a * 128:(a + 1) * 128] = tot[P_LNG + a:P_LNG + a + 1, :]
                g_outs[1][:, a * 128:(a + 1) * 128] = tot[P_LNB + a:P_LNB + a + 1, :]
            for a in range(2):
                g_outs[2][:, a * 128:(a + 1) * 128] = tot[P_BG + a:P_BG + a + 1, :]
            g_outs[3][...] = tot[P_NW:P_NW + 1, :]
            g_outs[4][...] = tot[P_SINK:P_SINK + 1, 0:Q_HEADS]
            g_outs[5][...] = tot[P_GU:PACK_OWN_ROWS, 0:gu_w]
            for n in range(N_SMALL):
                w_ref, m_ref, v_ref = params[3 * n:3 * n + 3]
                delta, new_m, new_v = _adamw(w_ref[...], g_outs[n][...], m_ref[...], v_ref[...])
                small_out[N_SMALL + n][...] = delta
                small_out[2 * N_SMALL + n][...] = new_m
                small_out[3 * N_SMALL + n][...] = new_v

        g = gi[...]
        delta, new_m, new_v = _adamw(wi[:, 0, :], g, mi[:, 0, :], vi[:, 0, :])
        gi_o[:, 0, :] = g
        di[:, 0, :] = delta
        nmi[:, 0, :] = new_m
        nvi[:, 0, :] = new_v
        g = go[...]
        go_o[...] = g
        do[...], nmo[...], nvo[...] = _adamw(wo[...], g, mo[...], vo[...])

    t_g = pl.BlockSpec((SHARD_IN, cols), lambda i: (0, i))
    t_in = pl.BlockSpec((SHARD_IN, 1, cols), lambda i: (0, 0, i))
    t_out = pl.BlockSpec((rows_out, D_MODEL), lambda i: (i, 0))
    s_in = jax.ShapeDtypeStruct((SHARD_IN, 1, D_MODEL), F32)
    s_out = jax.ShapeDtypeStruct((SHARD_OUT, D_MODEL), F32)
    whole = lambda shape: pl.BlockSpec(shape, lambda i: tuple(0 for _ in shape))
    small_specs = [whole(p.shape) for p in small_params]
    small_shapes = [jax.ShapeDtypeStruct(p.shape, F32) for p in small_params[0::3]] * 4
    return pl.pallas_call(
        body, name="adamw", grid=(steps,),
        in_specs=[t_g] + [t_in] * 3 + [t_out] * 4 + [whole(tot.shape)] + small_specs,
        out_specs=(t_in,) * 4 + (t_out,) * 4 + (whole((1, 128)),) + tuple(small_specs[0::3] * 4),
        out_shape=(s_in,) * 4 + (s_out,) * 4 + (jax.ShapeDtypeStruct((1, 128), F32),) + tuple(small_shapes),
        compiler_params=pltpu.CompilerParams(dimension_semantics=("arbitrary",)),
    )(g_in, w_in, m_in, v_in, g_out, w_out, m_out, v_out, tot, *small_params)


def _rope_tables(positions):
    half = 8
    inv_freq = 500000.0 ** (-jnp.arange(half, dtype=F32) / half)
    ang = inv_freq[:, None] * positions.astype(F32)[None, :]
    return jnp.concatenate([jnp.cos(ang), jnp.sin(ang)], axis=0)


def kernel(x, positions, w_in, gla_w_gate_up, gla_b_gate, attn_sinks, gla_norm_w, w_out, ln_g, ln_b, loss_target, m_w_in, m_gla_w_gate_up, m_gla_b_gate, m_attn_sinks, m_gla_norm_w, m_w_out, m_ln_g, m_ln_b, v_w_in, v_gla_w_gate_up, v_gla_b_gate, v_attn_sinks, v_gla_norm_w, v_w_out, v_ln_g, v_ln_b):
    def lin3(w):
        return jnp.transpose(w, (2, 0, 1))

    def unlin(w):
        return jnp.transpose(w, (1, 2, 0))

    win, wout, wgu_all = _gather_weights_call(lin3(w_in), w_out[0], gla_w_gate_up[0])
    wgu = jnp.transpose(wgu_all, (1, 0, 2)).reshape(W_R, W_KB)
    cs = _rope_tables(positions[0])
    sinks = attn_sinks[0]

    (qa, ka, va, qb, kb, vb, r, dattn, dga, dob, dgb, dh, st, g_wout, g_lng, g_lnb, g_nw, loss) = _fwd_call(
        x[0], loss_target[0], cs, win, wout, wgu, gla_b_gate, sinks, gla_norm_w, ln_g, ln_b)
    gx, g_win, g_sink, g_bg, g_wgu = _bwd_call(
        x[0], dh, qa, ka, va, dattn, dga, dob, dgb, qb, kb, vb, r, st, cs, win, wgu, gla_b_gate, sinks)

    g_wout_by_chip = g_wout.reshape(2, N_CHIPS, SHARD_OUT, HALF)
    small_params = []
    for group in ((ln_g, m_ln_g, v_ln_g), (ln_b, m_ln_b, v_ln_b), (gla_b_gate, m_gla_b_gate, v_gla_b_gate),
                  (gla_norm_w, m_gla_norm_w, v_gla_norm_w), (attn_sinks, m_attn_sinks, v_attn_sinks)):
        small_params += list(group)
    small_params += [gla_w_gate_up[0], m_gla_w_gate_up[0], v_gla_w_gate_up[0]]
    fin_in, fin_out, tot = _reduce_grads_call(g_win, g_wout_by_chip, (g_lng, g_lnb, g_bg, g_nw, g_sink, g_wgu, loss))
    fin_in, d_in, nm_in, nv_in, fin_out, d_out, nm_out, nv_out, loss_sum, *small_out = _adamw_call(
        fin_in, lin3(w_in), lin3(m_w_in), lin3(v_w_in), fin_out, w_out[0], m_w_out[0], v_w_out[0], tot, small_params)

    def unpack(kind, big_in, big_out):
        lng_, lnb_, bg_, nw_, sink_, gu_ = small_out[kind * N_SMALL:(kind + 1) * N_SMALL]
        return (unlin(big_in), gu_[None], bg_, sink_, nw_, big_out[None], lng_, lnb_)

    loss_total = loss_sum[0, 0]
    g_s, d_s, nm_s, nv_s = 0, 1, 2, 3
    return (loss_total, gx[None], *unpack(g_s, fin_in, fin_out), *unpack(d_s, d_in, d_out),
            *unpack(nm_s, nm_in, nm_out), *unpack(nv_s, nv_in, nv_out))
```

```python
import functools

import jax
import jax.numpy as jnp
import numpy as np
from jax import lax
from jax.experimental import pallas as pl
from jax.experimental.pallas import tpu as pltpu

F32 = jnp.float32
BF16 = jnp.bfloat16
MESH = pl.DeviceIdType.MESH

D_MODEL = 1024
N_CHIPS = 4
W_QA, W_KA, W_VA, W_GA, W_QB, W_KB, W_VB, W_GB, W_R = 512, 128, 128, 512, 256, 256, 512, 512, 16
O_QA = 0
O_KA = O_QA + W_QA
O_VA = O_KA + W_KA
O_GA = O_VA + W_VA
O_QB = O_GA + W_GA
O_KB = O_QB + W_QB
O_VB = O_KB + W_KB
O_GB = O_VB + W_VB
O_R = O_GB + W_GB
D_PROJ = O_R + W_R
SHARD_IN = D_PROJ // N_CHIPS
SHARD_OUT = D_MODEL // N_CHIPS
SHARD_PAD = 720
ACC_ROWS = -(-((N_CHIPS - 1) * SHARD_IN + SHARD_PAD) // 8) * 8
HALF = D_MODEL // 2
GATHER_SPLIT = 368

HEAD_A = 64
Q_HEADS = 8
KV_HEADS = 2
GROUP = 4
BLOCK = 128
GLA_HEADS = 4
GLA_DK = 64
GLA_DV = 128
CHUNK = 64
GLA_TAU = 16.0
EPS = 1e-5
ALPHA = 2.0 ** 0.25
ATT_SCALE = HEAD_A ** -0.5
GLA_SCALE = GLA_DK ** -0.5

ADAM_LR = 0.001
ADAM_B1 = 0.9
ADAM_B2 = 0.999
ADAM_EPS = 1e-08
ADAM_WD = 0.01
ADAM_STEP = 10

TM = 256
TRI_SLAB = 128
VMEM_LIMIT = 56 * 1024 * 1024

P_LNG, P_LNB, P_BG, P_NW, P_SINK, P_LOSS, P_GU = 0, 8, 16, 18, 19, 20, 24
PACK_ROWS = P_GU + N_CHIPS * 16
PACK_OWN_ROWS = P_GU + 16


def _mm(a, b):
    return jnp.dot(a, b, preferred_element_type=F32)


def _mm_nt(a, b):
    return lax.dot_general(a, b, (((1,), (1,)), ((), ())), preferred_element_type=F32)


def _mm_tn(a, b):
    return lax.dot_general(a, b, (((0,), (0,)), ((), ())), preferred_element_type=F32)


def _split3(a):
    hi = a.astype(BF16)
    r1 = a - hi.astype(F32)
    mid = r1.astype(BF16)
    lo = (r1 - mid.astype(F32)).astype(BF16)
    return hi, mid, lo


def _tri_mm(tri, a):
    slab = tri.shape[0]
    hi, mid, lo = _split3(a)
    return jnp.concatenate(
        [_mm(tri, hi[s:s + slab]) + _mm(tri, mid[s:s + slab]) + _mm(tri, lo[s:s + slab])
         for s in range(0, a.shape[0], slab)], axis=0)


def _chunk_tri(n, upper):
    r = lax.broadcasted_iota(jnp.int32, (n, n), 0)
    c = lax.broadcasted_iota(jnp.int32, (n, n), 1)
    same = (r >> 6) == (c >> 6)
    order = (c >= r) if upper else (c <= r)
    return jnp.where(same & order, 1.0, 0.0).astype(BF16)


def _rope(t, cos, sa, sb):
    w = t.shape[1]
    return t * cos + pltpu.roll(t, w - 8, 1) * sa + pltpu.roll(t, 8, 1) * sb


def _rope_tile(cs):
    row = lax.broadcasted_iota(jnp.int32, (16, 128), 0)
    d = lax.broadcasted_iota(jnp.int32, (16, 128), 1) & (HEAD_A - 1)
    hit = (d & 7) == (row & 7)
    is_cos = row < 8
    lo = d < 8
    hi = (d >= 8) & (d < 16)
    pick_cos = jnp.where(hit & is_cos & (lo | hi), 1.0, 0.0).astype(BF16)
    pick_sa = jnp.where(hit & ~is_cos & lo, -1.0, 0.0).astype(BF16)
    pick_sb = jnp.where(hit & ~is_cos & hi, 1.0, 0.0).astype(BF16)
    pick = jnp.concatenate([pick_cos, pick_sa, pick_sb], axis=1)
    hi, mid, lo = _split3(cs)
    tables = _mm_tn(hi, pick) + _mm_tn(mid, pick) + _mm_tn(lo, pick)
    d1 = lax.broadcasted_iota(jnp.int32, (1, 128), 1) & (HEAD_A - 1)
    return tables[:, 0:128] + jnp.where(d1 < 16, 0.0, 1.0), tables[:, 128:256], tables[:, 256:384]


def _rope_bwd(d, cos, sa, sb):
    w = d.shape[1]
    return d * cos + pltpu.roll(d * sa, 8, 1) + pltpu.roll(d * sb, w - 8, 1)


def _log_sigmoid(z):
    return jnp.minimum(z, 0.0) - jnp.log1p(jnp.exp(-jnp.abs(z)))


def _sigmoid(z):
    return 1.0 / (1.0 + jnp.exp(-z))


def _attn_bias(has_prev):
    r = lax.broadcasted_iota(jnp.int32, (GROUP * BLOCK, 2 * BLOCK), 0) & (BLOCK - 1)
    k = lax.broadcasted_iota(jnp.int32, (GROUP * BLOCK, 2 * BLOCK), 1)
    first_key = jnp.where(has_prev, 0, BLOCK)
    return jnp.where((k > r) & (k <= r + BLOCK) & (k >= first_key), 0.0, -jnp.inf)


def _sink_col(sinks_ref, j):
    r = lax.broadcasted_iota(jnp.int32, (GROUP * BLOCK, 1), 0) >> 7
    col = jnp.full((GROUP * BLOCK, 1), sinks_ref[GROUP * j], F32)
    for g in range(1, GROUP):
        col = jnp.where(r == g, sinks_ref[GROUP * j + g], col)
    return col


def _stack_heads(t, j):
    return jnp.concatenate([t[:, (GROUP * j + g) * HEAD_A:(GROUP * j + g + 1) * HEAD_A] for g in range(GROUP)], axis=0)


def _unstack_heads(parts):
    return jnp.concatenate([parts[j][g * BLOCK:(g + 1) * BLOCK] for j in range(KV_HEADS) for g in range(GROUP)], axis=1)


def _softmax_block(qs, kc, bias, sink):
    s = _mm_nt(qs, kc) * ATT_SCALE + bias
    m = jnp.maximum(jnp.max(s, axis=1, keepdims=True), sink)
    p = jnp.exp(s - m)
    e_sink = jnp.exp(sink - m)
    inv = 1.0 / (jnp.sum(p, axis=1, keepdims=True) + e_sink)
    return p * inv, e_sink * inv


def _fwd_call(x, tgt, cs, win, wout, wgu, bg, sinks, nw, lng, lnb):
    s_len = x.shape[0]
    nt = s_len // TM
    nblk = TM // BLOCK
    nch = TM // CHUNK

    def body(x_ref, t_ref, cs_ref, win_ref, wout_ref, wgu_ref, bg_ref, sinks_ref, nw_ref,
             lng_ref, lnb_ref,
             qa_ref, ka_ref, va_ref, qb_ref, kb_ref, vb_ref, r_ref, dattn_ref, dga_ref, dob_ref, dgb_ref, dh_ref,
             st_ref, dwout_ref, dwout16_ref, glng_ref, glnb_ref, gnw_ref, loss_ref,
             kprev, vprev, state, attn_s, ga_s, ob_s, gb_s, cat_s):
        i = pl.program_id(0)

        @pl.when(i == 0)
        def _():
            kprev[...] = jnp.zeros_like(kprev)
            vprev[...] = jnp.zeros_like(vprev)
            state[...] = jnp.zeros_like(state)
            dwout_ref[...] = jnp.zeros_like(dwout_ref)
            glng_ref[...] = jnp.zeros_like(glng_ref)
            glnb_ref[...] = jnp.zeros_like(glnb_ref)
            gnw_ref[...] = jnp.zeros_like(gnw_ref)
            loss_ref[...] = jnp.zeros_like(loss_ref)

        x = x_ref[...]
        xb = x.astype(BF16)

        def proj(off, width):
            return _mm_nt(xb, win_ref[off:off + width, :])

        cos, sa, sb = _rope_tile(cs_ref[...])
        cos4, sa4, sb4 = (jnp.concatenate([t] * 4, axis=1) for t in (cos, sa, sb))
        qa = _rope(proj(O_QA, W_QA), cos4, sa4, sb4).astype(BF16)
        ka = _rope(proj(O_KA, W_KA), cos, sa, sb).astype(BF16)
        va = proj(O_VA, W_VA).astype(BF16)
        qa_ref[...] = qa
        ka_ref[...] = ka
        va_ref[...] = va
        ga_s[...] = proj(O_GA, W_GA)
        gb_s[...] = proj(O_GB, W_GB)

        bias_inner = _attn_bias(True)
        sink_cols = [_sink_col(sinks_ref, j) for j in range(KV_HEADS)]
        for b in range(nblk):
            rows = slice(b * BLOCK, (b + 1) * BLOCK)
            mask = _attn_bias(i > 0) if b == 0 else bias_inner
            k_cur = ka[rows]
            v_cur = va[rows]
            k_old = kprev[...] if b == 0 else ka[(b - 1) * BLOCK:b * BLOCK]
            v_old = vprev[...] if b == 0 else va[(b - 1) * BLOCK:b * BLOCK]
            outs = []
            for j in range(KV_HEADS):
                hs = slice(j * HEAD_A, (j + 1) * HEAD_A)
                kc = jnp.concatenate([k_old[:, hs], k_cur[:, hs]], axis=0)
                vc = jnp.concatenate([v_old[:, hs], v_cur[:, hs]], axis=0)
                probs, _ = _softmax_block(_stack_heads(qa[rows], j), kc, mask, sink_cols[j])
                outs.append(_mm(probs.astype(BF16), vc))
            attn_s[rows, :] = _unstack_heads(outs)
        kprev[...] = ka[(nblk - 1) * BLOCK:]
        vprev[...] = va[(nblk - 1) * BLOCK:]

        r = proj(O_R, W_R)
        r_ref[...] = r
        z = _mm(r.astype(BF16), wgu_ref[...].astype(BF16)) + bg_ref[...]
        log_a = _log_sigmoid(z) / GLA_TAU
        bcum = _tri_mm(_chunk_tri(TRI_SLAB, False), log_a)
        qb = proj(O_QB, W_QB)
        kb = proj(O_KB, W_KB)
        vb = proj(O_VB, W_VB).astype(BF16)
        qb_ref[...] = qb
        kb_ref[...] = kb
        vb_ref[...] = vb
        qd_all = (qb * GLA_SCALE * jnp.exp(bcum)).astype(BF16)
        ki_all = (kb * jnp.exp(-bcum)).astype(BF16)
        tril = lax.broadcasted_iota(jnp.int32, (CHUNK, CHUNK), 0) >= lax.broadcasted_iota(jnp.int32, (CHUNK, CHUNK), 1)
        for c in range(nch):
            rows = slice(c * CHUNK, (c + 1) * CHUNK)
            b_c = bcum[rows]
            b_last = b_c[CHUNK - 1:CHUNK]
            ke = (kb[rows] * jnp.exp(b_last - b_c)).astype(BF16)
            st = state[...]
            st_ref[c] = st
            st16 = st.astype(BF16)
            o_parts, u_parts = [], []
            for h in range(GLA_HEADS):
                ks = slice(h * GLA_DK, (h + 1) * GLA_DK)
                vs = slice(h * GLA_DV, (h + 1) * GLA_DV)
                qd = qd_all[rows, ks]
                v_h = vb[rows, vs]
                a = jnp.where(tril, _mm_nt(qd, ki_all[rows, ks]), 0.0)
                o_parts.append(_mm(a.astype(BF16), v_h) + _mm_nt(qd, st16[:, ks]))
                u_parts.append(_mm_tn(v_h, ke[:, ks]))
            ob_s[rows, :] = jnp.concatenate(o_parts, axis=1)
            state[...] = st * jnp.exp(b_last) + jnp.concatenate(u_parts, axis=1)

        ga = ga_s[...]
        sg_a = _sigmoid(ga)
        silu_a = ga * sg_a
        attn = attn_s[...]
        cat_s[:, :W_GA] = (attn * silu_a).astype(BF16)
        gb = gb_s[...]
        sg_b = _sigmoid(gb)
        silu_b = gb * sg_b
        nw = nw_ref[...]
        on_parts = []
        for h in range(GLA_HEADS):
            vs = slice(h * GLA_DV, (h + 1) * GLA_DV)
            o_h = ob_s[:, vs]
            rs = lax.rsqrt(jnp.mean(o_h * o_h, axis=1, keepdims=True) + EPS)
            on_parts.append(o_h * rs * nw)
        on = jnp.concatenate(on_parts, axis=1)
        cat_s[:, W_GA:] = (on * silu_b).astype(BF16)
        cat = cat_s[...]
        hres = ALPHA * x + _mm(cat, wout_ref[...])
        mu = jnp.mean(hres, axis=1, keepdims=True)
        hc = hres - mu
        rstd = lax.rsqrt(jnp.mean(hc * hc, axis=1, keepdims=True) + EPS)
        xhat = hc * rstd
        g_ln = lng_ref[...]
        err = xhat * g_ln + lnb_ref[...] - t_ref[...]
        loss_ref[...] += jnp.sum(err * err) * (0.5 / D_MODEL)
        dy = err * (1.0 / D_MODEL)
        glng_ref[...] += jnp.sum(dy * xhat, axis=0, keepdims=True)
        glnb_ref[...] += jnp.sum(dy, axis=0, keepdims=True)
        dxh = dy * g_ln
        dh = rstd * (dxh - jnp.mean(dxh, axis=1, keepdims=True) - xhat * jnp.mean(dxh * xhat, axis=1, keepdims=True))
        dh_ref[...] = dh
        dh16 = dh.astype(BF16)
        for h in range(2):
            dwout_ref[h] += _mm_tn(cat, dh16[:, h * HALF:(h + 1) * HALF])

        @pl.when(i == nt - 1)
        def _():
            dwout16_ref[...] = dwout_ref[...].astype(BF16)
        dcat = _mm_nt(dh16, wout_ref[...])

        d_a = dcat[:, :W_GA]
        dattn_ref[...] = (d_a * silu_a).astype(BF16)
        dga_ref[...] = (d_a * attn * (sg_a * (1.0 + ga * (1.0 - sg_a)))).astype(BF16)
        d_b = dcat[:, W_GA:]
        dgb_ref[...] = (d_b * on * (sg_b * (1.0 + gb * (1.0 - sg_b)))).astype(BF16)
        d_on = d_b * silu_b
        gnw = jnp.zeros((1, GLA_DV), F32)
        do_parts = []
        for h in range(GLA_HEADS):
            vs = slice(h * GLA_DV, (h + 1) * GLA_DV)
            o_h = ob_s[:, vs]
            rs = lax.rsqrt(jnp.mean(o_h * o_h, axis=1, keepdims=True) + EPS)
            d_on_h = d_on[:, vs]
            gnw = gnw + jnp.sum(d_on_h * o_h * rs, axis=0, keepdims=True)
            gg = d_on_h * nw
            do_parts.append(rs * gg - o_h * (rs * rs * rs) * jnp.mean(gg * o_h, axis=1, keepdims=True))
        gnw_ref[...] += gnw
        dob_ref[...] = jnp.concatenate(do_parts, axis=1).astype(BF16)

    tile = lambda w: pl.BlockSpec((TM, w), lambda i: (i, 0))
    whole = lambda shape: pl.BlockSpec(shape, lambda i: tuple(0 for _ in shape), pipeline_mode=pl.Buffered(1))
    out_shape = (
        jax.ShapeDtypeStruct((s_len, W_QA), BF16),
        jax.ShapeDtypeStruct((s_len, W_KA), BF16),
        jax.ShapeDtypeStruct((s_len, W_VA), BF16),
        jax.ShapeDtypeStruct((s_len, W_QB), F32),
        jax.ShapeDtypeStruct((s_len, W_KB), F32),
        jax.ShapeDtypeStruct((s_len, W_VB), BF16),
        jax.ShapeDtypeStruct((s_len, W_R), F32),
        jax.ShapeDtypeStruct((s_len, W_GA), BF16),
        jax.ShapeDtypeStruct((s_len, W_GA), BF16),
        jax.ShapeDtypeStruct((s_len, W_GB), BF16),
        jax.ShapeDtypeStruct((s_len, W_GB), BF16),
        jax.ShapeDtypeStruct((s_len, D_MODEL), F32),
        jax.ShapeDtypeStruct((s_len // CHUNK, GLA_DV, GLA_HEADS * GLA_DK), F32),
        jax.ShapeDtypeStruct((2, D_MODEL, HALF), F32),
        jax.ShapeDtypeStruct((2, D_MODEL, HALF), BF16),
        jax.ShapeDtypeStruct((1, D_MODEL), F32),
        jax.ShapeDtypeStruct((1, D_MODEL), F32),
        jax.ShapeDtypeStruct((1, GLA_DV), F32),
        jax.ShapeDtypeStruct((1, 128), F32),
    )
    out_specs = (
        tile(W_QA), tile(W_KA), tile(W_VA), tile(W_QB), tile(W_KB), tile(W_VB), tile(W_R),
        tile(W_GA), tile(W_GA), tile(W_GB), tile(W_GB), tile(D_MODEL),
        pl.BlockSpec((nch, GLA_DV, GLA_HEADS * GLA_DK), lambda i: (i, 0, 0)),
        whole((2, D_MODEL, HALF)), whole((2, D_MODEL, HALF)), whole((1, D_MODEL)), whole((1, D_MODEL)),
        whole((1, GLA_DV)), whole((1, 128)),
    )
    in_specs = [
        tile(D_MODEL), tile(D_MODEL), pl.BlockSpec((16, TM), lambda i: (0, i)),
        whole((D_PROJ, D_MODEL)), whole((D_MODEL, D_MODEL)), whole((W_R, W_KB)), whole((1, W_KB)),
        pl.BlockSpec(memory_space=pltpu.SMEM), whole((1, GLA_DV)), whole((1, D_MODEL)), whole((1, D_MODEL)),
    ]
    scratch = [
        pltpu.VMEM((BLOCK, W_KA), BF16), pltpu.VMEM((BLOCK, W_VA), BF16),
        pltpu.VMEM((GLA_DV, GLA_HEADS * GLA_DK), F32),
        pltpu.VMEM((TM, W_GA), F32), pltpu.VMEM((TM, W_GA), F32), pltpu.VMEM((TM, W_GB), F32),
        pltpu.VMEM((TM, W_GB), F32), pltpu.VMEM((TM, D_MODEL), BF16),
    ]
    return pl.pallas_call(
        body, name="fwd_head", grid=(nt,), in_specs=in_specs, out_specs=out_specs, out_shape=out_shape,
        scratch_shapes=scratch,
        compiler_params=pltpu.CompilerParams(dimension_semantics=("arbitrary",), vmem_limit_bytes=VMEM_LIMIT),
    )(x, tgt, cs, win, wout, wgu, bg, sinks, nw, lng, lnb)


def _bwd_call(x, dh, qa, ka, va, dattn, dga, dob, dgb, qb, kb, vb, r, st, cs, win, wgu, bg, sinks):
    s_len = x.shape[0]
    nt = s_len // TM
    nblk = TM // BLOCK
    nch = TM // CHUNK

    def body(x_ref, dh_ref, qa_ref, ka_ref, va_ref, kap_ref, vap_ref, dattn_ref, dga_ref, dob_ref, dgb_ref,
             qb_ref, kb_ref, vb_ref, r_ref, st_ref, cs_ref, win_ref, wgu_ref, bg_ref, sinks_ref,
             gx_ref, dwin_ref, dwin16_ref, gsink_ref, gbg_ref, gwgu_ref,
             dproj, dk_carry, dv_carry, ds_carry, db_s):
        i = pl.program_id(0)
        t = nt - 1 - i

        @pl.when(i == 0)
        def _():
            dk_carry[...] = jnp.zeros_like(dk_carry)
            dv_carry[...] = jnp.zeros_like(dv_carry)
            ds_carry[...] = jnp.zeros_like(ds_carry)
            dwin_ref[...] = jnp.zeros_like(dwin_ref)
            gsink_ref[...] = jnp.zeros_like(gsink_ref)
            gbg_ref[...] = jnp.zeros_like(gbg_ref)
            gwgu_ref[...] = jnp.zeros_like(gwgu_ref)

        cos, sa, sb = _rope_tile(cs_ref[...])
        cos4, sa4, sb4 = (jnp.concatenate([v] * 4, axis=1) for v in (cos, sa, sb))

        qa = qa_ref[...]
        ka = ka_ref[...]
        va = va_ref[...]
        dattn = dattn_ref[...]
        gsink_rows = [jnp.zeros((1, 1), F32) for _ in range(Q_HEADS)]
        bias_inner = _attn_bias(True)
        sink_cols = [_sink_col(sinks_ref, j) for j in range(KV_HEADS)]
        for b in reversed(range(nblk)):
            rows = slice(b * BLOCK, (b + 1) * BLOCK)
            mask = _attn_bias(t > 0) if b == 0 else bias_inner
            k_cur = ka[rows]
            v_cur = va[rows]
            k_old = kap_ref[...] if b == 0 else ka[(b - 1) * BLOCK:b * BLOCK]
            v_old = vap_ref[...] if b == 0 else va[(b - 1) * BLOCK:b * BLOCK]
            dq_parts, dk_parts, dv_parts = [], [], []
            for j in range(KV_HEADS):
                hs = slice(j * HEAD_A, (j + 1) * HEAD_A)
                kc = jnp.concatenate([k_old[:, hs], k_cur[:, hs]], axis=0)
                vc = jnp.concatenate([v_old[:, hs], v_cur[:, hs]], axis=0)
                qs = _stack_heads(qa[rows], j)
                do_s = _stack_heads(dattn[rows], j)
                probs, p_sink = _softmax_block(qs, kc, mask, sink_cols[j])
                dp = _mm_nt(do_s, vc)
                d_row = jnp.sum(probs * dp, axis=1, keepdims=True)
                ds16 = (probs * (dp - d_row) * ATT_SCALE).astype(BF16)
                dq_parts.append(_mm(ds16, kc))
                dk_parts.append(_mm_tn(ds16, qs))
                dv_parts.append(_mm_tn(probs.astype(BF16), do_s))
                t_sink = d_row * p_sink
                for g in range(GROUP):
                    gsink_rows[GROUP * j + g] = gsink_rows[GROUP * j + g] - jnp.sum(
                        t_sink[g * BLOCK:(g + 1) * BLOCK], axis=0, keepdims=True)
            dq = _rope_bwd(_unstack_heads(dq_parts), cos4[rows], sa4[rows], sb4[rows])
            dproj[rows, O_QA:O_QA + W_QA] = dq.astype(BF16)
            dk_cur = dk_carry[...] + jnp.concatenate([p[BLOCK:] for p in dk_parts], axis=1)
            dv_cur = dv_carry[...] + jnp.concatenate([p[BLOCK:] for p in dv_parts], axis=1)
            dproj[rows, O_KA:O_KA + W_KA] = _rope_bwd(dk_cur, cos[rows], sa[rows], sb[rows]).astype(BF16)
            dproj[rows, O_VA:O_VA + W_VA] = dv_cur.astype(BF16)
            dk_carry[...] = jnp.concatenate([p[:BLOCK] for p in dk_parts], axis=1)
            dv_carry[...] = jnp.concatenate([p[:BLOCK] for p in dv_parts], axis=1)
        for hq in range(Q_HEADS):
            gsink_ref[hq:hq + 1, :] += jnp.broadcast_to(gsink_rows[hq], (1, 128))

        dproj[:, O_GA:O_GA + W_GA] = dga_ref[...]
        dproj[:, O_GB:O_GB + W_GB] = dgb_ref[...]

        r16 = r_ref[...].astype(BF16)
        wgu16 = wgu_ref[...].astype(BF16)
        z = _mm(r16, wgu16) + bg_ref[...]
        log_a = _log_sigmoid(z) / GLA_TAU
        bcum = _tri_mm(_chunk_tri(TRI_SLAB, False), log_a)
        qb = qb_ref[...]
        kb = kb_ref[...]
        vb = vb_ref[...]
        dob = dob_ref[...]
        e_b = jnp.exp(bcum)
        e_nb = jnp.exp(-bcum)
        qd_f = qb * GLA_SCALE * e_b
        ki_f = kb * e_nb
        qd_all = qd_f.astype(BF16)
        ki_all = ki_f.astype(BF16)
        tril = lax.broadcasted_iota(jnp.int32, (CHUNK, CHUNK), 0) >= lax.broadcasted_iota(jnp.int32, (CHUNK, CHUNK), 1)
        last_row = lax.broadcasted_iota(jnp.int32, (CHUNK, 1), 0) == CHUNK - 1
        for c in reversed(range(nch)):
            rows = slice(c * CHUNK, (c + 1) * CHUNK)
            b_c = bcum[rows]
            b_last = b_c[CHUNK - 1:CHUNK]
            e_e = jnp.exp(b_last - b_c)
            dec = jnp.exp(b_last)
            ke_f = kb[rows] * e_e
            ke = ke_f.astype(BF16)
            sp = st_ref[c]
            sp16 = sp.astype(BF16)
            dsn = ds_carry[...]
            dsn16 = dsn.astype(BF16)
            dqd_p, dki_p, dke_p, dv_p, dsp_p = [], [], [], [], []
            for h in range(GLA_HEADS):
                ks = slice(h * GLA_DK, (h + 1) * GLA_DK)
                vs = slice(h * GLA_DV, (h + 1) * GLA_DV)
                qd = qd_all[rows, ks]
                ki = ki_all[rows, ks]
                v_h = vb[rows, vs]
                do_h = dob[rows, vs]
                a16 = jnp.where(tril, _mm_nt(qd, ki), 0.0).astype(BF16)
                da16 = jnp.where(tril, _mm_nt(do_h, v_h), 0.0).astype(BF16)
                dv_p.append(_mm_tn(a16, do_h) + _mm_nt(ke[:, ks], dsn16[:, ks]))
                dqd_p.append(_mm(da16, ki) + _mm(do_h, sp16[:, ks]))
                dki_p.append(_mm_tn(da16, qd))
                dke_p.append(_mm(v_h, dsn16[:, ks]))
                dsp_p.append(_mm_tn(do_h, qd))
            dqd = jnp.concatenate(dqd_p, axis=1)
            dki = jnp.concatenate(dki_p, axis=1)
            dke = jnp.concatenate(dke_p, axis=1)
            ddec = jnp.sum(dsn * sp, axis=0, keepdims=True)
            ds_carry[...] = dsn * dec + jnp.concatenate(dsp_p, axis=1)
            dproj[rows, O_QB:O_QB + W_QB] = (dqd * e_b[rows] * GLA_SCALE).astype(BF16)
            dproj[rows, O_KB:O_KB + W_KB] = (dki * e_nb[rows] + dke * e_e).astype(BF16)
            dproj[rows, O_VB:O_VB + W_VB] = jnp.concatenate(dv_p, axis=1).astype(BF16)
            dke_ke = dke * ke_f
            d_b = dqd * qd_f[rows] - dki * ki_f[rows] - dke_ke
            d_bl = jnp.sum(dke_ke, axis=0, keepdims=True) + ddec * dec
            db_s[rows, :] = d_b + jnp.where(last_row, d_bl, 0.0)
        dlog_a = _tri_mm(_chunk_tri(TRI_SLAB, True), db_s[...])
        dz = dlog_a * (1.0 / GLA_TAU) * _sigmoid(-z)
        dz16 = dz.astype(BF16)
        gbg_ref[...] += jnp.sum(dz, axis=0, keepdims=True)
        gwgu_ref[...] += _mm_tn(r16, dz16)
        dproj[:, O_R:O_R + W_R] = _mm_nt(dz16, wgu16).astype(BF16)

        dp16 = dproj[...]
        gx_ref[...] = ALPHA * dh_ref[...] + _mm(dp16, win_ref[...])
        x16 = x_ref[...].astype(BF16)
        for h in range(2):
            dwin_ref[h, 0:D_PROJ, :] += _mm_tn(dp16, x16[:, h * HALF:(h + 1) * HALF])

        @pl.when(i == nt - 1)
        def _():
            dwin16_ref[...] = dwin_ref[...].astype(BF16)

    tile = lambda w: pl.BlockSpec((TM, w), lambda i: (nt - 1 - i, 0))
    whole = lambda shape: pl.BlockSpec(shape, lambda i: tuple(0 for _ in shape), pipeline_mode=pl.Buffered(1))
    prev_blk = pl.BlockSpec((BLOCK, W_KA), lambda i: (jnp.maximum((nt - 1 - i) * nblk - 1, 0), 0))
    in_specs = [
        tile(D_MODEL), tile(D_MODEL), tile(W_QA), tile(W_KA), tile(W_VA), prev_blk, prev_blk,
        tile(W_GA), tile(W_GA), tile(W_GB), tile(W_GB), tile(W_QB), tile(W_KB), tile(W_VB), tile(W_R),
        pl.BlockSpec((nch, GLA_DV, GLA_HEADS * GLA_DK), lambda i: (nt - 1 - i, 0, 0)),
        pl.BlockSpec((16, TM), lambda i: (0, nt - 1 - i)),
        whole((D_PROJ, D_MODEL)), whole((W_R, W_KB)), whole((1, W_KB)), pl.BlockSpec(memory_space=pltpu.SMEM),
    ]
    out_shape = (
        jax.ShapeDtypeStruct((s_len, D_MODEL), F32),
        jax.ShapeDtypeStruct((2, ACC_ROWS, HALF), F32),
        jax.ShapeDtypeStruct((2, ACC_ROWS, HALF), BF16),
        jax.ShapeDtypeStruct((Q_HEADS, 128), F32),
        jax.ShapeDtypeStruct((1, W_KB), F32),
        jax.ShapeDtypeStruct((W_R, W_KB), F32),
    )
    out_specs = (tile(D_MODEL), whole((2, ACC_ROWS, HALF)), whole((2, ACC_ROWS, HALF)), whole((Q_HEADS, 128)), whole((1, W_KB)),
                 whole((W_R, W_KB)))
    scratch = [
        pltpu.VMEM((TM, D_PROJ), BF16), pltpu.VMEM((BLOCK, W_KA), F32), pltpu.VMEM((BLOCK, W_VA), F32),
        pltpu.VMEM((GLA_DV, GLA_HEADS * GLA_DK), F32), pltpu.VMEM((TM, W_KB), F32),
    ]
    return pl.pallas_call(
        body, name="bwd_mix", grid=(nt,), in_specs=in_specs, out_specs=out_specs, out_shape=out_shape,
        scratch_shapes=scratch,
        compiler_params=pltpu.CompilerParams(dimension_semantics=("arbitrary",), vmem_limit_bytes=VMEM_LIMIT),
    )(x, dh, qa, ka, va, ka, va, dattn, dga, dob, dgb, qb, kb, vb, r, st, cs, win, wgu, bg, sinks)


def _mesh_place():
    x, y, c = lax.axis_index("x"), lax.axis_index("y"), lax.axis_index("c")
    chips = [(1 - x, y), (x, 1 - y), (1 - x, 1 - y)]
    return x, y, c, chips


def _gather_weights_call(w_lin, w_out, wgu):
    def body(wlin_ref, wout_ref, wgu_ref, wt_ref, wout_full, wgu_all, blk, oblk, asm, send_sems, recv_sems):
        x, y, c, chips = _mesh_place()
        k_me = 2 * x + y
        asm[SHARD_IN - 4:SHARD_PAD, :] = jnp.zeros((SHARD_PAD - SHARD_IN + 4, D_MODEL), F32)
        asm[0:SHARD_IN, :] = wlin_ref[:, 0, :]
        for h in range(2):
            blk[k_me, h] = asm[0:SHARD_PAD, h * HALF:(h + 1) * HALF].astype(BF16)
            oblk[k_me, h] = wout_ref[:, h * HALF:(h + 1) * HALF].astype(BF16)
        wgu_all[k_me] = wgu_ref[...]

        parts = ((0, GATHER_SPLIT, 0, SHARD_OUT // 2), (GATHER_SPLIT, SHARD_PAD - GATHER_SPLIT, SHARD_OUT // 2, SHARD_OUT // 2))
        me_id, sib_id = (x, y, c), (x, y, 1 - c)
        nbr_x, nbr_y, diag = ((*chip, c) for chip in chips)
        k_x, k_y, k_d = (2 * chip[0] + chip[1] for chip in chips)

        def copies(k, hc, p, sem0, to):
            if p is None:
                refs = (blk.at[k, hc], oblk.at[k, hc])
            else:
                r0, rn, o0, on = parts[p]
                refs = (blk.at[k, hc, pl.ds(r0, rn), :], oblk.at[k, hc, pl.ds(o0, on), :])
            return [pltpu.make_async_remote_copy(src_ref=ref, dst_ref=ref, send_sem=send_sems.at[sem0 + n],
                                                 recv_sem=recv_sems.at[sem0 + n], device_id=to, device_id_type=MESH)
                    for n, ref in enumerate(refs)]

        def gu_copy(k, r, to):
            return pltpu.make_async_remote_copy(src_ref=wgu_all.at[k], dst_ref=wgu_all.at[k], send_sem=send_sems.at[18 + r],
                                                recv_sem=recv_sems.at[18 + r], device_id=to, device_id_type=MESH)

        def start(cps):
            for cp in cps:
                cp.start()
            return cps

        def landed(cps):
            for cp in cps:
                cp.wait_recv()

        started = start(copies(k_me, c, 0, 0, nbr_x) + copies(k_me, c, 1, 6, nbr_y)
                        + copies(k_me, c, 1, 2, nbr_x) + copies(k_me, c, 0, 4, nbr_y)
                        + [gu_copy(k_me, r, to) for r, to in enumerate((nbr_x, nbr_y, diag))])
        landed(copies(k_x, c, 0, 0, me_id))
        started += start(copies(k_x, c, 0, 8, nbr_y))
        landed(copies(k_y, c, 1, 6, me_id))
        started += start(copies(k_y, c, 1, 10, nbr_x))
        landed(copies(k_x, c, 1, 2, me_id))
        started += start(copies(k_x, c, None, 12, sib_id))
        landed(copies(k_y, c, 0, 4, me_id))
        started += start(copies(k_y, c, None, 14, sib_id))
        landed(copies(k_d, c, 0, 8, me_id) + copies(k_d, c, 1, 10, me_id))
        started += start(copies(k_d, c, None, 16, sib_id))
        for r, k_r in enumerate((k_x, k_y, k_d)):
            landed(copies(k_r, 1 - c, None, 12 + 2 * r, me_id))
            gu_copy(k_r, r, me_id).wait_recv()
        for cp in started:
            cp.wait_send()

        for k in range(N_CHIPS):
            for h in range(2):
                asm[k * SHARD_IN:k * SHARD_IN + SHARD_PAD, h * HALF:(h + 1) * HALF] = blk[k, h].astype(F32)
                wout_full[k * SHARD_OUT:(k + 1) * SHARD_OUT, h * HALF:(h + 1) * HALF] = oblk[k, h]
        wt_ref[...] = asm[0:D_PROJ, :].astype(BF16)

    vmem = pl.BlockSpec(memory_space=pltpu.VMEM)
    return pl.pallas_call(
        body, name="gather_weights",
        out_shape=(jax.ShapeDtypeStruct((D_PROJ, D_MODEL), BF16),
                   jax.ShapeDtypeStruct((D_MODEL, D_MODEL), BF16),
                   jax.ShapeDtypeStruct((N_CHIPS, W_R, W_KB // N_CHIPS), F32)),
        in_specs=[vmem, vmem, vmem], out_specs=(vmem, vmem, vmem),
        scratch_shapes=[pltpu.VMEM((N_CHIPS, 2, SHARD_PAD, HALF), BF16), pltpu.VMEM((N_CHIPS, 2, SHARD_OUT, HALF), BF16),
                        pltpu.VMEM((ACC_ROWS, D_MODEL), F32),
                        pltpu.SemaphoreType.DMA((21,)), pltpu.SemaphoreType.DMA((21,))],
        compiler_params=pltpu.CompilerParams(vmem_limit_bytes=VMEM_LIMIT),
    )(w_lin, w_out, wgu)


def _adamw(w, g, m, v):
    m = ADAM_B1 * m + (1.0 - ADAM_B1) * g
    v = ADAM_B2 * v + (1.0 - ADAM_B2) * (g * g)
    m_hat = m / (1.0 - ADAM_B1 ** ADAM_STEP)
    v_hat = v / (1.0 - ADAM_B2 ** ADAM_STEP)
    delta = -ADAM_LR * (m_hat / (jnp.sqrt(v_hat) + ADAM_EPS) + ADAM_WD * w)
    return delta, m, v


N_SMALL = 6


def _reduce_grads_call(g_in, g_out, g_in16, g_out16, small_grads):
    def body(gin_hbm, gout_hbm, gin16_hbm, gout16_hbm, g_lng, g_lnb, g_bg, g_nw, g_sink, g_wgu, loss_in,
             lin_in, fin_out, tot_out,
             a_in, a_out, b_in, b_out, b32, c_in, s_in, s_out, r_in, r_out, f_in, f_out, pack_ref, tot_ref, pack_all,
             send_sems, recv_sems, local_sems):
        x, y, c, chips = _mesh_place()
        k_me = 2 * x + y
        me = 4 * x + 2 * y + c
        sibling = (x, y, 1 - c)

        pack_ref[...] = jnp.zeros_like(pack_ref)
        for a in range(8):
            pack_ref[P_LNG + a:P_LNG + a + 1, :] = g_lng[:, a * 128:(a + 1) * 128]
            pack_ref[P_LNB + a:P_LNB + a + 1, :] = g_lnb[:, a * 128:(a + 1) * 128]
        for a in range(2):
            pack_ref[P_BG + a:P_BG + a + 1, :] = g_bg[:, a * 128:(a + 1) * 128]
        pack_ref[P_NW:P_NW + 1, :] = g_nw[...]
        lane = lax.broadcasted_iota(jnp.int32, (1, 128), 1)
        sink_row = jnp.zeros((1, 128), F32)
        for hq in range(Q_HEADS):
            sink_row = jnp.where(lane == hq, g_sink[hq:hq + 1, :], sink_row)
        pack_ref[P_SINK:P_SINK + 1, :] = sink_row
        pack_ref[P_LOSS:P_LOSS + 1, :] = loss_in[...]
        gu_w = W_KB // N_CHIPS
        for k in range(N_CHIPS):
            pack_ref[P_GU + W_R * k:P_GU + W_R * (k + 1), 0:gu_w] = g_wgu[:, k * gu_w:(k + 1) * gu_w]
        pack_all[me] = pack_ref[...]
        small = []
        for mask in range(1, 8):
            peer = (x ^ (mask >> 2), y ^ ((mask >> 1) & 1), c ^ (mask & 1))
            small.append(pltpu.make_async_remote_copy(
                src_ref=pack_ref, dst_ref=pack_all.at[me], send_sem=send_sems.at[mask], recv_sem=recv_sems.at[mask],
                device_id=peer, device_id_type=MESH))
        for cp in small:
            cp.start()

        loads = [pltpu.make_async_copy(gin_hbm.at[c], a_in, local_sems.at[0]),
                 pltpu.make_async_copy(gout_hbm.at[c], a_out, local_sems.at[1])]
        to_sib = [pltpu.make_async_remote_copy(
                      src_ref=gin16_hbm.at[1 - c], dst_ref=b_in,
                      send_sem=send_sems.at[8], recv_sem=recv_sems.at[8], device_id=sibling, device_id_type=MESH),
                  pltpu.make_async_remote_copy(
                      src_ref=gout16_hbm.at[1 - c], dst_ref=b_out,
                      send_sem=send_sems.at[9], recv_sem=recv_sems.at[9], device_id=sibling, device_id_type=MESH)]
        for cp in (loads[1], to_sib[1], loads[0], to_sib[0]):
            cp.start()

        parts = ((0, GATHER_SPLIT, 0, SHARD_OUT // 2), (GATHER_SPLIT, SHARD_PAD - GATHER_SPLIT, SHARD_OUT // 2, SHARD_OUT // 2))
        nbr_x, nbr_y, _ = ((*chip, c) for chip in chips)
        k_x, k_y, k_d = (2 * chip[0] + chip[1] for chip in chips)

        def rows_of(p, w):
            r0, rn, o0, on = parts[p]
            return (o0, on) if w else (r0, rn)

        def mine(k, p, w):
            r0, rn = rows_of(p, w)
            if w:
                return a_out[k, pl.ds(r0, rn), :] + b_out[k, pl.ds(r0, rn), :].astype(F32)
            return c_in[k, pl.ds(r0, rn), :]

        def message(m, p, w, to):
            rn = rows_of(p, w)[1]
            stage, land = (s_out, r_out) if w else (s_in, r_in)
            return pltpu.make_async_remote_copy(
                src_ref=stage.at[m, pl.ds(0, rn), :], dst_ref=land.at[m, pl.ds(0, rn), :],
                send_sem=send_sems.at[10 + 2 * m + w], recv_sem=recv_sems.at[10 + 2 * m + w],
                device_id=to, device_id_type=MESH)

        def post(m, p, w, val, to):
            rn = rows_of(p, w)[1]
            stage = s_out if w else s_in
            stage[m, 0:rn, :] = val.astype(BF16)
            cp = message(m, p, w, to)
            cp.start()
            return [cp]

        def take(m, p, w):
            rn = rows_of(p, w)[1]
            message(m, p, w, (x, y, c)).wait_recv()
            land = r_out if w else r_in
            return land[m, 0:rn, :].astype(F32)

        sent = []
        for w in (1, 0):
            loads[w].wait()
            to_sib[w].wait_recv()
            if w == 0:
                b32[...] = b_in[...].astype(F32)
                for k in range(N_CHIPS):
                    rows = slice(k * SHARD_IN, k * SHARD_IN + SHARD_PAD)
                    c_in[k] = a_in[rows, :] + b32[rows, :]
            sent += post(1, 0, w, mine(k_d, 0, w), nbr_x) + post(4, 1, w, mine(k_d, 1, w), nbr_y)
            sent += post(0, 0, w, mine(k_x, 0, w), nbr_x) + post(3, 1, w, mine(k_y, 1, w), nbr_y)
        for w in (1, 0):
            sent += post(5, 0, w, take(1, 0, w) + mine(k_y, 0, w), nbr_y)
            sent += post(2, 1, w, take(4, 1, w) + mine(k_x, 1, w), nbr_x)
        for w in (1, 0):
            for p, direct, summed in ((0, 0, 5), (1, 3, 2)):
                r0, rn = rows_of(p, w)
                total = mine(k_me, p, w) + take(direct, p, w) + take(summed, p, w)
                if w:
                    f_out[c, r0:r0 + rn, :] = total
                else:
                    f_in[c, r0:r0 + rn, :] = total

        swap = [pltpu.make_async_remote_copy(
                    src_ref=f_in.at[c], dst_ref=f_in.at[c],
                    send_sem=send_sems.at[22], recv_sem=recv_sems.at[22], device_id=sibling, device_id_type=MESH),
                pltpu.make_async_remote_copy(
                    src_ref=f_out.at[c], dst_ref=f_out.at[c],
                    send_sem=send_sems.at[23], recv_sem=recv_sems.at[23], device_id=sibling, device_id_type=MESH)]
        for cp in swap:
            cp.start()

        for cp in small:
            cp.wait_recv()
        total = pack_all[0]
        for d in range(1, 8):
            total = total + pack_all[d]
        tot_ref[...] = total
        tot_out[0:P_GU, :] = total[0:P_GU]
        tot_out[P_GU:PACK_OWN_ROWS, :] = tot_ref[pl.ds(pl.multiple_of(P_GU + W_R * k_me, 8), W_R), :]

        other_in = pltpu.make_async_remote_copy(
            src_ref=f_in.at[1 - c], dst_ref=f_in.at[1 - c],
            send_sem=send_sems.at[22], recv_sem=recv_sems.at[22], device_id=sibling, device_id_type=MESH)
        other_out = pltpu.make_async_remote_copy(
            src_ref=f_out.at[1 - c], dst_ref=f_out.at[1 - c],
            send_sem=send_sems.at[23], recv_sem=recv_sems.at[23], device_id=sibling, device_id_type=MESH)
        other_in.wait_recv()
        other_out.wait_recv()
        for cp in small + to_sib + sent + swap:
            cp.wait_send()

        for h in range(2):
            lin_in[:, h * HALF:(h + 1) * HALF] = f_in[h, 0:SHARD_IN, :]
            fin_out[:, h * HALF:(h + 1) * HALF] = f_out[h]

    vmem = pl.BlockSpec(memory_space=pltpu.VMEM)
    hbm = pl.BlockSpec(memory_space=pl.ANY)
    return pl.pallas_call(
        body, name="reduce_grads",
        out_shape=(jax.ShapeDtypeStruct((SHARD_IN, D_MODEL), F32), jax.ShapeDtypeStruct((SHARD_OUT, D_MODEL), F32),
                   jax.ShapeDtypeStruct((PACK_OWN_ROWS, 128), F32)),
        in_specs=[hbm] * 4 + [vmem] * 7, out_specs=(vmem,) * 3,
        scratch_shapes=[
            pltpu.VMEM((ACC_ROWS, HALF), F32), pltpu.VMEM((N_CHIPS, SHARD_OUT, HALF), F32),
            pltpu.VMEM((ACC_ROWS, HALF), BF16), pltpu.VMEM((N_CHIPS, SHARD_OUT, HALF), BF16),
            pltpu.VMEM((ACC_ROWS, HALF), F32),
            pltpu.VMEM((N_CHIPS, SHARD_PAD, HALF), F32),
            pltpu.VMEM((6, GATHER_SPLIT, HALF), BF16), pltpu.VMEM((6, SHARD_OUT // 2, HALF), BF16),
            pltpu.VMEM((6, GATHER_SPLIT, HALF), BF16), pltpu.VMEM((6, SHARD_OUT // 2, HALF), BF16),
            pltpu.VMEM((2, SHARD_PAD, HALF), F32), pltpu.VMEM((2, SHARD_OUT, HALF), F32),
            pltpu.VMEM((PACK_ROWS, 128), F32), pltpu.VMEM((PACK_ROWS, 128), F32), pltpu.VMEM((8, PACK_ROWS, 128), F32),
            pltpu.SemaphoreType.DMA((24,)), pltpu.SemaphoreType.DMA((24,)), pltpu.SemaphoreType.DMA((2,)),
        ],
        compiler_params=pltpu.CompilerParams(vmem_limit_bytes=VMEM_LIMIT),
    )(g_in, g_out, g_in16, g_out16, *small_grads)


def _adamw_call(g_in, w_in, m_in, v_in, g_out, w_out, m_out, v_out, tot, small_params):
    steps = 4
    rows_out = SHARD_OUT // steps
    cols = D_MODEL // steps
    gu_w = W_KB // N_CHIPS

    def body(gi, wi, mi, vi, go, wo, mo, vo, tot, *rest):
        params = rest[:3 * N_SMALL]
        gi_o, di, nmi, nvi, go_o, do, nmo, nvo, loss_out = rest[3 * N_SMALL:3 * N_SMALL + 9]
        small_out = rest[3 * N_SMALL + 9:]

        @pl.when(pl.program_id(0) == 0)
        def _():
            loss_out[...] = tot[P_LOSS:P_LOSS + 1, :]
            g_outs = small_out[0:N_SMALL]
            for a in range(8):
                g_outs[0][:, a * 128:(a + 1) * 128] = tot[P_LNG + a:P_LNG + a + 1, :]
                g_outs[1][:, a * 128:(a + 1) * 128] = tot[P_LNB + a:P_LNB + a + 1, :]
            for a in range(2):
                g_outs[2][:, a * 128:(a + 1) * 128] = tot[P_BG + a:P_BG + a + 1, :]
            g_outs[3][...] = tot[P_NW:P_NW + 1, :]
            g_outs[4][...] = tot[P_SINK:P_SINK + 1, 0:Q_HEADS]
            g_outs[5][...] = tot[P_GU:PACK_OWN_ROWS, 0:gu_w]
            for n in range(N_SMALL):
                w_ref, m_ref, v_ref = params[3 * n:3 * n + 3]
                delta, new_m, new_v = _adamw(w_ref[...], g_outs[n][...], m_ref[...], v_ref[...])
                small_out[N_SMALL + n][...] = delta
                small_out[2 * N_SMALL + n][...] = new_m
                small_out[3 * N_SMALL + n][...] = new_v

        g = gi[...]
        delta, new_m, new_v = _adamw(wi[:, 0, :], g, mi[:, 0, :], vi[:, 0, :])
        gi_o[:, 0, :] = g
        di[:, 0, :] = delta
        nmi[:, 0, :] = new_m
        nvi[:, 0, :] = new_v
        g = go[...]
        go_o[...] = g
        do[...], nmo[...], nvo[...] = _adamw(wo[...], g, mo[...], vo[...])

    t_g = pl.BlockSpec((SHARD_IN, cols), lambda i: (0, i))
    t_in = pl.BlockSpec((SHARD_IN, 1, cols), lambda i: (0, 0, i))
    t_out = pl.BlockSpec((rows_out, D_MODEL), lambda i: (i, 0))
    s_in = jax.ShapeDtypeStruct((SHARD_IN, 1, D_MODEL), F32)
    s_out = jax.ShapeDtypeStruct((SHARD_OUT, D_MODEL), F32)
    whole = lambda shape: pl.BlockSpec(shape, lambda i: tuple(0 for _ in shape))
    small_specs = [whole(p.shape) for p in small_params]
    small_shapes = [jax.ShapeDtypeStruct(p.shape, F32) for p in small_params[0::3]] * 4
    return pl.pallas_call(
        body, name="adamw", grid=(steps,),
        in_specs=[t_g] + [t_in] * 3 + [t_out] * 4 + [whole(tot.shape)] + small_specs,
        out_specs=(t_in,) * 4 + (t_out,) * 4 + (whole((1, 128)),) + tuple(small_specs[0::3] * 4),
        out_shape=(s_in,) * 4 + (s_out,) * 4 + (jax.ShapeDtypeStruct((1, 128), F32),) + tuple(small_shapes),
        compiler_params=pltpu.CompilerParams(dimension_semantics=("arbitrary",)),
    )(g_in, w_in, m_in, v_in, g_out, w_out, m_out, v_out, tot, *small_params)


def _rope_tables(positions):
    half = 8
    inv_freq = 500000.0 ** (-jnp.arange(half, dtype=F32) / half)
    ang = inv_freq[:, None] * positions.astype(F32)[None, :]
    return jnp.concatenate([jnp.cos(ang), jnp.sin(ang)], axis=0)


def kernel(x, positions, w_in, gla_w_gate_up, gla_b_gate, attn_sinks, gla_norm_w, w_out, ln_g, ln_b, loss_target, m_w_in, m_gla_w_gate_up, m_gla_b_gate, m_attn_sinks, m_gla_norm_w, m_w_out, m_ln_g, m_ln_b, v_w_in, v_gla_w_gate_up, v_gla_b_gate, v_attn_sinks, v_gla_norm_w, v_w_out, v_ln_g, v_ln_b):
    def lin3(w):
        return jnp.transpose(w, (2, 0, 1))

    def unlin(w):
        return jnp.transpose(w, (1, 2, 0))

    win, wout, wgu_all = _gather_weights_call(lin3(w_in), w_out[0], gla_w_gate_up[0])
    wgu = jnp.transpose(wgu_all, (1, 0, 2)).reshape(W_R, W_KB)
    cs = _rope_tables(positions[0])
    sinks = attn_sinks[0]

    (qa, ka, va, qb, kb, vb, r, dattn, dga, dob, dgb, dh, st, g_wout, g_wout16, g_lng, g_lnb, g_nw, loss) = _fwd_call(
        x[0], loss_target[0], cs, win, wout, wgu, gla_b_gate, sinks, gla_norm_w, ln_g, ln_b)
    gx, g_win, g_win16, g_sink, g_bg, g_wgu = _bwd_call(
        x[0], dh, qa, ka, va, dattn, dga, dob, dgb, qb, kb, vb, r, st, cs, win, wgu, gla_b_gate, sinks)

    g_wout_by_chip = g_wout.reshape(2, N_CHIPS, SHARD_OUT, HALF)
    g_wout16_by_chip = g_wout16.reshape(2, N_CHIPS, SHARD_OUT, HALF)
    small_params = []
    for group in ((ln_g, m_ln_g, v_ln_g), (ln_b, m_ln_b, v_ln_b), (gla_b_gate, m_gla_b_gate, v_gla_b_gate),
                  (gla_norm_w, m_gla_norm_w, v_gla_norm_w), (attn_sinks, m_attn_sinks, v_attn_sinks)):
        small_params += list(group)
    small_params += [gla_w_gate_up[0], m_gla_w_gate_up[0], v_gla_w_gate_up[0]]
    fin_in, fin_out, tot = _reduce_grads_call(g_win, g_wout_by_chip, g_win16, g_wout16_by_chip,
                                              (g_lng, g_lnb, g_bg, g_nw, g_sink, g_wgu, loss))
    fin_in, d_in, nm_in, nv_in, fin_out, d_out, nm_out, nv_out, loss_sum, *small_out = _adamw_call(
        fin_in, lin3(w_in), lin3(m_w_in), lin3(v_w_in), fin_out, w_out[0], m_w_out[0], v_w_out[0], tot, small_params)

    def unpack(kind, big_in, big_out):
        lng_, lnb_, bg_, nw_, sink_, gu_ = small_out[kind * N_SMALL:(kind + 1) * N_SMALL]
        return (unlin(big_in), gu_[None], bg_, sink_, nw_, big_out[None], lng_, lnb_)

    loss_total = loss_sum[0, 0]
    g_s, d_s, nm_s, nv_s = 0, 1, 2, 3
    return (loss_total, gx[None], *unpack(g_s, fin_in, fin_out), *unpack(d_s, d_in, d_out),
            *unpack(nm_s, nm_in, nm_out), *unpack(nv_s, nv_in, nv_out))
```

```python
import functools

import jax
import jax.numpy as jnp
import numpy as np
from jax import lax
from jax.experimental import pallas as pl
from jax.experimental.pallas import tpu as pltpu

F32 = jnp.float32
BF16 = jnp.bfloat16
MESH = pl.DeviceIdType.MESH

D_MODEL = 1024
N_CHIPS = 4
W_QA, W_KA, W_VA, W_GA, W_QB, W_KB, W_VB, W_GB, W_R = 512, 128, 128, 512, 256, 256, 512, 512, 16
O_QA = 0
O_KA = O_QA + W_QA
O_VA = O_KA + W_KA
O_GA = O_VA + W_VA
O_QB = O_GA + W_GA
O_KB = O_QB + W_QB
O_VB = O_KB + W_KB
O_GB = O_VB + W_VB
O_R = O_GB + W_GB
D_PROJ = O_R + W_R
SHARD_IN = D_PROJ // N_CHIPS
SHARD_OUT = D_MODEL // N_CHIPS
SHARD_PAD = 720
ACC_ROWS = -(-((N_CHIPS - 1) * SHARD_IN + SHARD_PAD) // 8) * 8
HALF = D_MODEL // 2
GATHER_SPLIT = 368

HEAD_A = 64
Q_HEADS = 8
KV_HEADS = 2
GROUP = 4
BLOCK = 128
GLA_HEADS = 4
GLA_DK = 64
GLA_DV = 128
CHUNK = 64
GLA_TAU = 16.0
EPS = 1e-5
ALPHA = 2.0 ** 0.25
ATT_SCALE = HEAD_A ** -0.5
GLA_SCALE = GLA_DK ** -0.5

ADAM_LR = 0.001
ADAM_B1 = 0.9
ADAM_B2 = 0.999
ADAM_EPS = 1e-08
ADAM_WD = 0.01
ADAM_STEP = 10

TM = 256
TRI_SLAB = 128
VMEM_LIMIT = 56 * 1024 * 1024

P_LNG, P_LNB, P_BG, P_NW, P_SINK, P_LOSS, P_GU = 0, 8, 16, 18, 19, 20, 24
PACK_ROWS = P_GU + N_CHIPS * 16
PACK_OWN_ROWS = P_GU + 16


def _mm(a, b):
    return jnp.dot(a, b, preferred_element_type=F32)


def _mm_nt(a, b):
    return lax.dot_general(a, b, (((1,), (1,)), ((), ())), preferred_element_type=F32)


def _mm_tn(a, b):
    return lax.dot_general(a, b, (((0,), (0,)), ((), ())), preferred_element_type=F32)


def _split3(a):
    hi = a.astype(BF16)
    r1 = a - hi.astype(F32)
    mid = r1.astype(BF16)
    lo = (r1 - mid.astype(F32)).astype(BF16)
    return hi, mid, lo


def _tri_mm(tri, a):
    slab = tri.shape[0]
    hi, mid, lo = _split3(a)
    return jnp.concatenate(
        [_mm(tri, hi[s:s + slab]) + _mm(tri, mid[s:s + slab]) + _mm(tri, lo[s:s + slab])
         for s in range(0, a.shape[0], slab)], axis=0)


def _chunk_tri(n, upper):
    r = lax.broadcasted_iota(jnp.int32, (n, n), 0)
    c = lax.broadcasted_iota(jnp.int32, (n, n), 1)
    same = (r >> 6) == (c >> 6)
    order = (c >= r) if upper else (c <= r)
    return jnp.where(same & order, 1.0, 0.0).astype(BF16)


def _rope(t, cos, sa, sb):
    w = t.shape[1]
    return t * cos + pltpu.roll(t, w - 8, 1) * sa + pltpu.roll(t, 8, 1) * sb


def _rope_tile(cs):
    row = lax.broadcasted_iota(jnp.int32, (16, 128), 0)
    d = lax.broadcasted_iota(jnp.int32, (16, 128), 1) & (HEAD_A - 1)
    hit = (d & 7) == (row & 7)
    is_cos = row < 8
    lo = d < 8
    hi = (d >= 8) & (d < 16)
    pick_cos = jnp.where(hit & is_cos & (lo | hi), 1.0, 0.0).astype(BF16)
    pick_sa = jnp.where(hit & ~is_cos & lo, -1.0, 0.0).astype(BF16)
    pick_sb = jnp.where(hit & ~is_cos & hi, 1.0, 0.0).astype(BF16)
    pieces = _split3(cs)

    def spread(pick):
        return _mm_tn(pieces[0], pick) + _mm_tn(pieces[1], pick) + _mm_tn(pieces[2], pick)

    d1 = lax.broadcasted_iota(jnp.int32, (1, 128), 1) & (HEAD_A - 1)
    return spread(pick_cos) + jnp.where(d1 < 16, 0.0, 1.0), spread(pick_sa), spread(pick_sb)


def _rope_bwd(d, cos, sa, sb):
    w = d.shape[1]
    return d * cos + pltpu.roll(d * sa, 8, 1) + pltpu.roll(d * sb, w - 8, 1)


def _log_sigmoid(z):
    return jnp.minimum(z, 0.0) - jnp.log1p(jnp.exp(-jnp.abs(z)))


def _sigmoid(z):
    return 1.0 / (1.0 + jnp.exp(-z))


def _attn_bias(has_prev):
    r = lax.broadcasted_iota(jnp.int32, (GROUP * BLOCK, 2 * BLOCK), 0) & (BLOCK - 1)
    k = lax.broadcasted_iota(jnp.int32, (GROUP * BLOCK, 2 * BLOCK), 1)
    first_key = jnp.where(has_prev, 0, BLOCK)
    return jnp.where((k > r) & (k <= r + BLOCK) & (k >= first_key), 0.0, -jnp.inf)


def _sink_col(sinks_ref, j):
    r = lax.broadcasted_iota(jnp.int32, (GROUP * BLOCK, 1), 0) >> 7
    col = jnp.full((GROUP * BLOCK, 1), sinks_ref[GROUP * j], F32)
    for g in range(1, GROUP):
        col = jnp.where(r == g, sinks_ref[GROUP * j + g], col)
    return col


def _stack_heads(t, j):
    return jnp.concatenate([t[:, (GROUP * j + g) * HEAD_A:(GROUP * j + g + 1) * HEAD_A] for g in range(GROUP)], axis=0)


def _unstack_heads(parts):
    return jnp.concatenate([parts[j][g * BLOCK:(g + 1) * BLOCK] for j in range(KV_HEADS) for g in range(GROUP)], axis=1)


def _scores(qs, kc, bias):
    return _mm_nt(qs, kc) * ATT_SCALE + bias


def _softmax_block(qs, kc, bias, sink):
    return _softmax(_scores(qs, kc, bias), sink)


def _softmax(s, sink):
    m = jnp.maximum(jnp.max(s, axis=1, keepdims=True), sink)
    p = jnp.exp(s - m)
    e_sink = jnp.exp(sink - m)
    inv = 1.0 / (jnp.sum(p, axis=1, keepdims=True) + e_sink)
    return p * inv, e_sink * inv


def _fwd_call(x, tgt, cs, win, wout, wgu, bg, sinks, nw, lng, lnb):
    s_len = x.shape[0]
    nt = s_len // TM
    nblk = TM // BLOCK
    nch = TM // CHUNK

    def body(x_ref, t_ref, cs_ref, win_ref, wout_ref, wgu_ref, bg_ref, sinks_ref, nw_ref,
             lng_ref, lnb_ref,
             qa_ref, ka_ref, va_ref, qb_ref, kb_ref, vb_ref, r_ref, dattn_ref, dga_ref, dob_ref, dgb_ref, dh_ref,
             st_ref, dwout_ref, glng_ref, glnb_ref, gnw_ref, loss_ref,
             kprev, vprev, state, attn_s, ga_s, ob_s, gb_s, cat_s):
        i = pl.program_id(0)

        @pl.when(i == 0)
        def _():
            kprev[...] = jnp.zeros_like(kprev)
            vprev[...] = jnp.zeros_like(vprev)
            state[...] = jnp.zeros_like(state)
            dwout_ref[...] = jnp.zeros_like(dwout_ref)
            glng_ref[...] = jnp.zeros_like(glng_ref)
            glnb_ref[...] = jnp.zeros_like(glnb_ref)
            gnw_ref[...] = jnp.zeros_like(gnw_ref)
            loss_ref[...] = jnp.zeros_like(loss_ref)

        x = x_ref[...]
        xb = x.astype(BF16)

        def proj(off, width):
            return _mm_nt(xb, win_ref[off:off + width, :])

        cos, sa, sb = _rope_tile(cs_ref[...])
        cos4, sa4, sb4 = (jnp.concatenate([t] * 4, axis=1) for t in (cos, sa, sb))
        qa = _rope(proj(O_QA, W_QA), cos4, sa4, sb4).astype(BF16)
        ka = _rope(proj(O_KA, W_KA), cos, sa, sb).astype(BF16)
        va = proj(O_VA, W_VA).astype(BF16)
        qa_ref[...] = qa
        ka_ref[...] = ka
        va_ref[...] = va

        later = {}

        def fill_ga():
            ga_s[...] = proj(O_GA, W_GA)

        def fill_gb():
            gb_s[...] = proj(O_GB, W_GB)

        def fill_qk():
            later["qb"] = proj(O_QB, W_QB)
            later["kb"] = proj(O_KB, W_KB)

        def fill_vr():
            later["vb"] = proj(O_VB, W_VB).astype(BF16)
            later["r"] = proj(O_R, W_R)

        fillers = [fill_ga, fill_gb, fill_qk, fill_vr]

        bias_inner = _attn_bias(True)
        sink_cols = [_sink_col(sinks_ref, j) for j in range(KV_HEADS)]
        for b in range(nblk):
            rows = slice(b * BLOCK, (b + 1) * BLOCK)
            mask = _attn_bias(i > 0) if b == 0 else bias_inner
            k_cur = ka[rows]
            v_cur = va[rows]
            k_old = kprev[...] if b == 0 else ka[(b - 1) * BLOCK:b * BLOCK]
            v_old = vprev[...] if b == 0 else va[(b - 1) * BLOCK:b * BLOCK]
            outs = []
            for j in range(KV_HEADS):
                hs = slice(j * HEAD_A, (j + 1) * HEAD_A)
                kc = jnp.concatenate([k_old[:, hs], k_cur[:, hs]], axis=0)
                vc = jnp.concatenate([v_old[:, hs], v_cur[:, hs]], axis=0)
                s = _scores(_stack_heads(qa[rows], j), kc, mask)
                if fillers:
                    fillers.pop(0)()
                probs, _ = _softmax(s, sink_cols[j])
                outs.append(_mm(probs.astype(BF16), vc))
            attn_s[rows, :] = _unstack_heads(outs)
        kprev[...] = ka[(nblk - 1) * BLOCK:]
        vprev[...] = va[(nblk - 1) * BLOCK:]
        while fillers:
            fillers.pop(0)()

        r = later["r"]
        r_ref[...] = r
        z = _mm(r.astype(BF16), wgu_ref[...].astype(BF16)) + bg_ref[...]
        log_a = _log_sigmoid(z) / GLA_TAU
        bcum = _tri_mm(_chunk_tri(TRI_SLAB, False), log_a)
        qb = later["qb"]
        kb = later["kb"]
        vb = later["vb"]
        qb_ref[...] = qb
        kb_ref[...] = kb
        vb_ref[...] = vb
        qd_all = (qb * GLA_SCALE * jnp.exp(bcum)).astype(BF16)
        ki_all = (kb * jnp.exp(-bcum)).astype(BF16)
        tril = lax.broadcasted_iota(jnp.int32, (CHUNK, CHUNK), 0) >= lax.broadcasted_iota(jnp.int32, (CHUNK, CHUNK), 1)
        st = state[...]
        for c in range(nch):
            rows = slice(c * CHUNK, (c + 1) * CHUNK)
            b_c = bcum[rows]
            b_last = b_c[CHUNK - 1:CHUNK]
            ke = (kb[rows] * jnp.exp(b_last - b_c)).astype(BF16)
            st_ref[c] = st
            st16 = st.astype(BF16)
            o_parts, u_parts = [], []
            for h in range(GLA_HEADS):
                ks = slice(h * GLA_DK, (h + 1) * GLA_DK)
                vs = slice(h * GLA_DV, (h + 1) * GLA_DV)
                qd = qd_all[rows, ks]
                v_h = vb[rows, vs]
                a = jnp.where(tril, _mm_nt(qd, ki_all[rows, ks]), 0.0)
                o_parts.append(_mm(a.astype(BF16), v_h) + _mm_nt(qd, st16[:, ks]))
                u_parts.append(_mm_tn(v_h, ke[:, ks]))
            ob_s[rows, :] = jnp.concatenate(o_parts, axis=1)
            st = st * jnp.exp(b_last) + jnp.concatenate(u_parts, axis=1)
        state[...] = st

        ga = ga_s[...]
        sg_a = _sigmoid(ga)
        silu_a = ga * sg_a
        attn = attn_s[...]
        cat_s[:, :W_GA] = (attn * silu_a).astype(BF16)
        gb = gb_s[...]
        sg_b = _sigmoid(gb)
        silu_b = gb * sg_b
        nw = nw_ref[...]
        on_parts = []
        for h in range(GLA_HEADS):
            vs = slice(h * GLA_DV, (h + 1) * GLA_DV)
            o_h = ob_s[:, vs]
            rs = lax.rsqrt(jnp.mean(o_h * o_h, axis=1, keepdims=True) + EPS)
            on_parts.append(o_h * rs * nw)
        on = jnp.concatenate(on_parts, axis=1)
        cat_s[:, W_GA:] = (on * silu_b).astype(BF16)
        cat = cat_s[...]
        hres = ALPHA * x + _mm(cat, wout_ref[...])
        mu = jnp.mean(hres, axis=1, keepdims=True)
        hc = hres - mu
        rstd = lax.rsqrt(jnp.mean(hc * hc, axis=1, keepdims=True) + EPS)
        xhat = hc * rstd
        g_ln = lng_ref[...]
        err = xhat * g_ln + lnb_ref[...] - t_ref[...]
        loss_ref[...] += jnp.sum(err * err) * (0.5 / D_MODEL)
        dy = err * (1.0 / D_MODEL)
        glng_ref[...] += jnp.sum(dy * xhat, axis=0, keepdims=True)
        glnb_ref[...] += jnp.sum(dy, axis=0, keepdims=True)
        dxh = dy * g_ln
        dh = rstd * (dxh - jnp.mean(dxh, axis=1, keepdims=True) - xhat * jnp.mean(dxh * xhat, axis=1, keepdims=True))
        dh_ref[...] = dh
        dh16 = dh.astype(BF16)
        for h in range(2):
            dwout_ref[h] += _mm_tn(cat, dh16[:, h * HALF:(h + 1) * HALF])
        dcat = _mm_nt(dh16, wout_ref[...])

        d_a = dcat[:, :W_GA]
        dattn_ref[...] = (d_a * silu_a).astype(BF16)
        dga_ref[...] = (d_a * attn * (sg_a * (1.0 + ga * (1.0 - sg_a)))).astype(BF16)
        d_b = dcat[:, W_GA:]
        dgb_ref[...] = (d_b * on * (sg_b * (1.0 + gb * (1.0 - sg_b)))).astype(BF16)
        d_on = d_b * silu_b
        gnw = jnp.zeros((1, GLA_DV), F32)
        do_parts = []
        for h in range(GLA_HEADS):
            vs = slice(h * GLA_DV, (h + 1) * GLA_DV)
            o_h = ob_s[:, vs]
            rs = lax.rsqrt(jnp.mean(o_h * o_h, axis=1, keepdims=True) + EPS)
            d_on_h = d_on[:, vs]
            gnw = gnw + jnp.sum(d_on_h * o_h * rs, axis=0, keepdims=True)
            gg = d_on_h * nw
            do_parts.append(rs * gg - o_h * (rs * rs * rs) * jnp.mean(gg * o_h, axis=1, keepdims=True))
        gnw_ref[...] += gnw
        dob_ref[...] = jnp.concatenate(do_parts, axis=1).astype(BF16)

    tile = lambda w: pl.BlockSpec((TM, w), lambda i: (i, 0))
    whole = lambda shape: pl.BlockSpec(shape, lambda i: tuple(0 for _ in shape), pipeline_mode=pl.Buffered(1))
    out_shape = (
        jax.ShapeDtypeStruct((s_len, W_QA), BF16),
        jax.ShapeDtypeStruct((s_len, W_KA), BF16),
        jax.ShapeDtypeStruct((s_len, W_VA), BF16),
        jax.ShapeDtypeStruct((s_len, W_QB), F32),
        jax.ShapeDtypeStruct((s_len, W_KB), F32),
        jax.ShapeDtypeStruct((s_len, W_VB), BF16),
        jax.ShapeDtypeStruct((s_len, W_R), F32),
        jax.ShapeDtypeStruct((s_len, W_GA), BF16),
        jax.ShapeDtypeStruct((s_len, W_GA), BF16),
        jax.ShapeDtypeStruct((s_len, W_GB), BF16),
        jax.ShapeDtypeStruct((s_len, W_GB), BF16),
        jax.ShapeDtypeStruct((s_len, D_MODEL), F32),
        jax.ShapeDtypeStruct((s_len // CHUNK, GLA_DV, GLA_HEADS * GLA_DK), F32),
        jax.ShapeDtypeStruct((2, D_MODEL, HALF), F32),
        jax.ShapeDtypeStruct((1, D_MODEL), F32),
        jax.ShapeDtypeStruct((1, D_MODEL), F32),
        jax.ShapeDtypeStruct((1, GLA_DV), F32),
        jax.ShapeDtypeStruct((1, 128), F32),
    )
    out_specs = (
        tile(W_QA), tile(W_KA), tile(W_VA), tile(W_QB), tile(W_KB), tile(W_VB), tile(W_R),
        tile(W_GA), tile(W_GA), tile(W_GB), tile(W_GB), tile(D_MODEL),
        pl.BlockSpec((nch, GLA_DV, GLA_HEADS * GLA_DK), lambda i: (i, 0, 0)),
        whole((2, D_MODEL, HALF)), whole((1, D_MODEL)), whole((1, D_MODEL)), whole((1, GLA_DV)), whole((1, 128)),
    )
    in_specs = [
        tile(D_MODEL), tile(D_MODEL), pl.BlockSpec((16, TM), lambda i: (0, i)),
        whole((D_PROJ, D_MODEL)), whole((D_MODEL, D_MODEL)), whole((W_R, W_KB)), whole((1, W_KB)),
        pl.BlockSpec(memory_space=pltpu.SMEM), whole((1, GLA_DV)), whole((1, D_MODEL)), whole((1, D_MODEL)),
    ]
    scratch = [
        pltpu.VMEM((BLOCK, W_KA), BF16), pltpu.VMEM((BLOCK, W_VA), BF16),
        pltpu.VMEM((GLA_DV, GLA_HEADS * GLA_DK), F32),
        pltpu.VMEM((TM, W_GA), F32), pltpu.VMEM((TM, W_GA), F32), pltpu.VMEM((TM, W_GB), F32),
        pltpu.VMEM((TM, W_GB), F32), pltpu.VMEM((TM, D_MODEL), BF16),
    ]
    return pl.pallas_call(
        body, name="fwd_head", grid=(nt,), in_specs=in_specs, out_specs=out_specs, out_shape=out_shape,
        scratch_shapes=scratch,
        compiler_params=pltpu.CompilerParams(dimension_semantics=("arbitrary",), vmem_limit_bytes=VMEM_LIMIT),
    )(x, tgt, cs, win, wout, wgu, bg, sinks, nw, lng, lnb)


def _bwd_call(x, dh, qa, ka, va, dattn, dga, dob, dgb, qb, kb, vb, r, st, cs, win, wgu, bg, sinks):
    s_len = x.shape[0]
    nt = s_len // TM
    nblk = TM // BLOCK
    nch = TM // CHUNK

    def body(x_ref, dh_ref, qa_ref, ka_ref, va_ref, kap_ref, vap_ref, dattn_ref, dga_ref, dob_ref, dgb_ref,
             qb_ref, kb_ref, vb_ref, r_ref, st_ref, cs_ref, win_ref, wgu_ref, bg_ref, sinks_ref,
             gx_ref, dwin_ref, gsink_ref, gbg_ref, gwgu_ref,
             dproj, dk_carry, dv_carry, ds_carry, db_s):
        i = pl.program_id(0)
        t = nt - 1 - i

        @pl.when(i == 0)
        def _():
            dk_carry[...] = jnp.zeros_like(dk_carry)
            dv_carry[...] = jnp.zeros_like(dv_carry)
            ds_carry[...] = jnp.zeros_like(ds_carry)
            dwin_ref[...] = jnp.zeros_like(dwin_ref)
            gsink_ref[...] = jnp.zeros_like(gsink_ref)
            gbg_ref[...] = jnp.zeros_like(gbg_ref)
            gwgu_ref[...] = jnp.zeros_like(gwgu_ref)

        cos, sa, sb = _rope_tile(cs_ref[...])
        cos4, sa4, sb4 = (jnp.concatenate([v] * 4, axis=1) for v in (cos, sa, sb))

        qa = qa_ref[...]
        ka = ka_ref[...]
        va = va_ref[...]
        dattn = dattn_ref[...]
        gsink_rows = [jnp.zeros((1, 1), F32) for _ in range(Q_HEADS)]
        bias_inner = _attn_bias(True)
        sink_cols = [_sink_col(sinks_ref, j) for j in range(KV_HEADS)]
        for b in reversed(range(nblk)):
            rows = slice(b * BLOCK, (b + 1) * BLOCK)
            mask = _attn_bias(t > 0) if b == 0 else bias_inner
            k_cur = ka[rows]
            v_cur = va[rows]
            k_old = kap_ref[...] if b == 0 else ka[(b - 1) * BLOCK:b * BLOCK]
            v_old = vap_ref[...] if b == 0 else va[(b - 1) * BLOCK:b * BLOCK]
            dq_parts, dk_parts, dv_parts = [], [], []
            for j in range(KV_HEADS):
                hs = slice(j * HEAD_A, (j + 1) * HEAD_A)
                kc = jnp.concatenate([k_old[:, hs], k_cur[:, hs]], axis=0)
                vc = jnp.concatenate([v_old[:, hs], v_cur[:, hs]], axis=0)
                qs = _stack_heads(qa[rows], j)
                do_s = _stack_heads(dattn[rows], j)
                probs, p_sink = _softmax_block(qs, kc, mask, sink_cols[j])
                dp = _mm_nt(do_s, vc)
                d_row = jnp.sum(probs * dp, axis=1, keepdims=True)
                ds16 = (probs * (dp - d_row) * ATT_SCALE).astype(BF16)
                dq_parts.append(_mm(ds16, kc))
                dk_parts.append(_mm_tn(ds16, qs))
                dv_parts.append(_mm_tn(probs.astype(BF16), do_s))
                t_sink = d_row * p_sink
                for g in range(GROUP):
                    gsink_rows[GROUP * j + g] = gsink_rows[GROUP * j + g] - jnp.sum(
                        t_sink[g * BLOCK:(g + 1) * BLOCK], axis=0, keepdims=True)
            dq = _rope_bwd(_unstack_heads(dq_parts), cos4[rows], sa4[rows], sb4[rows])
            dproj[rows, O_QA:O_QA + W_QA] = dq.astype(BF16)
            dk_cur = dk_carry[...] + jnp.concatenate([p[BLOCK:] for p in dk_parts], axis=1)
            dv_cur = dv_carry[...] + jnp.concatenate([p[BLOCK:] for p in dv_parts], axis=1)
            dproj[rows, O_KA:O_KA + W_KA] = _rope_bwd(dk_cur, cos[rows], sa[rows], sb[rows]).astype(BF16)
            dproj[rows, O_VA:O_VA + W_VA] = dv_cur.astype(BF16)
            dk_carry[...] = jnp.concatenate([p[:BLOCK] for p in dk_parts], axis=1)
            dv_carry[...] = jnp.concatenate([p[:BLOCK] for p in dv_parts], axis=1)
        for hq in range(Q_HEADS):
            gsink_ref[hq:hq + 1, :] += jnp.broadcast_to(gsink_rows[hq], (1, 128))

        dproj[:, O_GA:O_GA + W_GA] = dga_ref[...]
        dproj[:, O_GB:O_GB + W_GB] = dgb_ref[...]

        r16 = r_ref[...].astype(BF16)
        wgu16 = wgu_ref[...].astype(BF16)
        z = _mm(r16, wgu16) + bg_ref[...]
        log_a = _log_sigmoid(z) / GLA_TAU
        bcum = _tri_mm(_chunk_tri(TRI_SLAB, False), log_a)
        qb = qb_ref[...]
        kb = kb_ref[...]
        vb = vb_ref[...]
        dob = dob_ref[...]
        e_b = jnp.exp(bcum)
        e_nb = jnp.exp(-bcum)
        qd_f = qb * GLA_SCALE * e_b
        ki_f = kb * e_nb
        qd_all = qd_f.astype(BF16)
        ki_all = ki_f.astype(BF16)
        tril = lax.broadcasted_iota(jnp.int32, (CHUNK, CHUNK), 0) >= lax.broadcasted_iota(jnp.int32, (CHUNK, CHUNK), 1)
        last_row = lax.broadcasted_iota(jnp.int32, (CHUNK, 1), 0) == CHUNK - 1

        x16 = x_ref[...].astype(BF16)
        dp_a = dproj[:, 0:O_QB]
        early = {}

        def fill_dx():
            early["gx"] = _mm(dp_a, win_ref[0:O_QB, :])

        def fill_dw(h):
            dwin_ref[h, 0:O_QB, :] += _mm_tn(dp_a, x16[:, h * HALF:(h + 1) * HALF])

        fillers = [fill_dx, functools.partial(fill_dw, 0), functools.partial(fill_dw, 1)]
        dsn = ds_carry[...]
        for c in reversed(range(nch)):
            if fillers:
                fillers.pop(0)()
            rows = slice(c * CHUNK, (c + 1) * CHUNK)
            b_c = bcum[rows]
            b_last = b_c[CHUNK - 1:CHUNK]
            e_e = jnp.exp(b_last - b_c)
            dec = jnp.exp(b_last)
            ke_f = kb[rows] * e_e
            ke = ke_f.astype(BF16)
            sp = st_ref[c]
            sp16 = sp.astype(BF16)
            dsn16 = dsn.astype(BF16)
            dqd_p, dki_p, dke_p, dv_p, dsp_p = [], [], [], [], []
            for h in range(GLA_HEADS):
                ks = slice(h * GLA_DK, (h + 1) * GLA_DK)
                vs = slice(h * GLA_DV, (h + 1) * GLA_DV)
                qd = qd_all[rows, ks]
                ki = ki_all[rows, ks]
                v_h = vb[rows, vs]
                do_h = dob[rows, vs]
                a16 = jnp.where(tril, _mm_nt(qd, ki), 0.0).astype(BF16)
                da16 = jnp.where(tril, _mm_nt(do_h, v_h), 0.0).astype(BF16)
                dv_p.append(_mm_tn(a16, do_h) + _mm_nt(ke[:, ks], dsn16[:, ks]))
                dqd_p.append(_mm(da16, ki) + _mm(do_h, sp16[:, ks]))
                dki_p.append(_mm_tn(da16, qd))
                dke_p.append(_mm(v_h, dsn16[:, ks]))
                dsp_p.append(_mm_tn(do_h, qd))
            dqd = jnp.concatenate(dqd_p, axis=1)
            dki = jnp.concatenate(dki_p, axis=1)
            dke = jnp.concatenate(dke_p, axis=1)
            ddec = jnp.sum(dsn * sp, axis=0, keepdims=True)
            dsn_next = dsn * dec + jnp.concatenate(dsp_p, axis=1)
            dproj[rows, O_QB:O_QB + W_QB] = (dqd * e_b[rows] * GLA_SCALE).astype(BF16)
            dproj[rows, O_KB:O_KB + W_KB] = (dki * e_nb[rows] + dke * e_e).astype(BF16)
            dproj[rows, O_VB:O_VB + W_VB] = jnp.concatenate(dv_p, axis=1).astype(BF16)
            dke_ke = dke * ke_f
            d_b = dqd * qd_f[rows] - dki * ki_f[rows] - dke_ke
            d_bl = jnp.sum(dke_ke, axis=0, keepdims=True) + ddec * dec
            db_s[rows, :] = d_b + jnp.where(last_row, d_bl, 0.0)
            dsn = dsn_next
        ds_carry[...] = dsn
        while fillers:
            fillers.pop(0)()
        dlog_a = _tri_mm(_chunk_tri(TRI_SLAB, True), db_s[...])
        dz = dlog_a * (1.0 / GLA_TAU) * _sigmoid(-z)
        dz16 = dz.astype(BF16)
        gbg_ref[...] += jnp.sum(dz, axis=0, keepdims=True)
        gwgu_ref[...] += _mm_tn(r16, dz16)
        dproj[:, O_R:O_R + W_R] = _mm_nt(dz16, wgu16).astype(BF16)

        dp_b = dproj[:, O_QB:D_PROJ]
        gx_ref[...] = ALPHA * dh_ref[...] + early["gx"] + _mm(dp_b, win_ref[O_QB:D_PROJ, :])
        for h in range(2):
            dwin_ref[h, O_QB:D_PROJ, :] += _mm_tn(dp_b, x16[:, h * HALF:(h + 1) * HALF])

    tile = lambda w: pl.BlockSpec((TM, w), lambda i: (nt - 1 - i, 0))
    whole = lambda shape: pl.BlockSpec(shape, lambda i: tuple(0 for _ in shape), pipeline_mode=pl.Buffered(1))
    prev_blk = pl.BlockSpec((BLOCK, W_KA), lambda i: (jnp.maximum((nt - 1 - i) * nblk - 1, 0), 0))
    in_specs = [
        tile(D_MODEL), tile(D_MODEL), tile(W_QA), tile(W_KA), tile(W_VA), prev_blk, prev_blk,
        tile(W_GA), tile(W_GA), tile(W_GB), tile(W_GB), tile(W_QB), tile(W_KB), tile(W_VB), tile(W_R),
        pl.BlockSpec((nch, GLA_DV, GLA_HEADS * GLA_DK), lambda i: (nt - 1 - i, 0, 0)),
        pl.BlockSpec((16, TM), lambda i: (0, nt - 1 - i)),
        whole((D_PROJ, D_MODEL)), whole((W_R, W_KB)), whole((1, W_KB)), pl.BlockSpec(memory_space=pltpu.SMEM),
    ]
    out_shape = (
        jax.ShapeDtypeStruct((s_len, D_MODEL), F32),
        jax.ShapeDtypeStruct((2, ACC_ROWS, HALF), F32),
        jax.ShapeDtypeStruct((Q_HEADS, 128), F32),
        jax.ShapeDtypeStruct((1, W_KB), F32),
        jax.ShapeDtypeStruct((W_R, W_KB), F32),
    )
    out_specs = (tile(D_MODEL), whole((2, ACC_ROWS, HALF)), whole((Q_HEADS, 128)), whole((1, W_KB)),
                 whole((W_R, W_KB)))
    scratch = [
        pltpu.VMEM((TM, D_PROJ), BF16), pltpu.VMEM((BLOCK, W_KA), F32), pltpu.VMEM((BLOCK, W_VA), F32),
        pltpu.VMEM((GLA_DV, GLA_HEADS * GLA_DK), F32), pltpu.VMEM((TM, W_KB), F32),
    ]
    return pl.pallas_call(
        body, name="bwd_mix", grid=(nt,), in_specs=in_specs, out_specs=out_specs, out_shape=out_shape,
        scratch_shapes=scratch,
        compiler_params=pltpu.CompilerParams(dimension_semantics=("arbitrary",), vmem_limit_bytes=VMEM_LIMIT),
    )(x, dh, qa, ka, va, ka, va, dattn, dga, dob, dgb, qb, kb, vb, r, st, cs, win, wgu, bg, sinks)


def _mesh_place():
    x, y, c = lax.axis_index("x"), lax.axis_index("y"), lax.axis_index("c")
    chips = [(1 - x, y), (x, 1 - y), (1 - x, 1 - y)]
    return x, y, c, chips


def _gather_weights_call(w_lin, w_out, wgu):
    def body(wlin_ref, wout_ref, wgu_ref, wt_ref, wout_full, wgu_all, blk, oblk, asm, send_sems, recv_sems):
        x, y, c, chips = _mesh_place()
        k_me = 2 * x + y
        asm[SHARD_IN - 4:SHARD_PAD, :] = jnp.zeros((SHARD_PAD - SHARD_IN + 4, D_MODEL), F32)
        asm[0:SHARD_IN, :] = wlin_ref[:, 0, :]
        for h in range(2):
            blk[k_me, h] = asm[0:SHARD_PAD, h * HALF:(h + 1) * HALF].astype(BF16)
            oblk[k_me, h] = wout_ref[:, h * HALF:(h + 1) * HALF].astype(BF16)
        wgu_all[k_me] = wgu_ref[...]

        parts = ((0, GATHER_SPLIT, 0, SHARD_OUT // 2), (GATHER_SPLIT, SHARD_PAD - GATHER_SPLIT, SHARD_OUT // 2, SHARD_OUT // 2))
        me_id, sib_id = (x, y, c), (x, y, 1 - c)
        nbr_x, nbr_y, diag = ((*chip, c) for chip in chips)
        k_x, k_y, k_d = (2 * chip[0] + chip[1] for chip in chips)

        def copies(k, hc, p, sem0, to):
            if p is None:
                refs = (blk.at[k, hc], oblk.at[k, hc])
            else:
                r0, rn, o0, on = parts[p]
                refs = (blk.at[k, hc, pl.ds(r0, rn), :], oblk.at[k, hc, pl.ds(o0, on), :])
            return [pltpu.make_async_remote_copy(src_ref=ref, dst_ref=ref, send_sem=send_sems.at[sem0 + n],
                                                 recv_sem=recv_sems.at[sem0 + n], device_id=to, device_id_type=MESH)
                    for n, ref in enumerate(refs)]

        def gu_copy(k, r, to):
            return pltpu.make_async_remote_copy(src_ref=wgu_all.at[k], dst_ref=wgu_all.at[k], send_sem=send_sems.at[18 + r],
                                                recv_sem=recv_sems.at[18 + r], device_id=to, device_id_type=MESH)

        def start(cps):
            for cp in cps:
                cp.start()
            return cps

        def landed(cps):
            for cp in cps:
                cp.wait_recv()

        started = start(copies(k_me, c, 0, 0, nbr_x) + copies(k_me, c, 1, 6, nbr_y)
                        + copies(k_me, c, 1, 2, nbr_x) + copies(k_me, c, 0, 4, nbr_y)
                        + [gu_copy(k_me, r, to) for r, to in enumerate((nbr_x, nbr_y, diag))])
        landed(copies(k_x, c, 0, 0, me_id))
        started += start(copies(k_x, c, 0, 8, nbr_y))
        landed(copies(k_y, c, 1, 6, me_id))
        started += start(copies(k_y, c, 1, 10, nbr_x))
        landed(copies(k_x, c, 1, 2, me_id))
        started += start(copies(k_x, c, None, 12, sib_id))
        landed(copies(k_y, c, 0, 4, me_id))
        started += start(copies(k_y, c, None, 14, sib_id))
        landed(copies(k_d, c, 0, 8, me_id) + copies(k_d, c, 1, 10, me_id))
        started += start(copies(k_d, c, None, 16, sib_id))
        for r, k_r in enumerate((k_x, k_y, k_d)):
            landed(copies(k_r, 1 - c, None, 12 + 2 * r, me_id))
            gu_copy(k_r, r, me_id).wait_recv()
        for cp in started:
            cp.wait_send()

        for k in range(N_CHIPS):
            for h in range(2):
                asm[k * SHARD_IN:k * SHARD_IN + SHARD_PAD, h * HALF:(h + 1) * HALF] = blk[k, h].astype(F32)
                wout_full[k * SHARD_OUT:(k + 1) * SHARD_OUT, h * HALF:(h + 1) * HALF] = oblk[k, h]
        wt_ref[...] = asm[0:D_PROJ, :].astype(BF16)

    vmem = pl.BlockSpec(memory_space=pltpu.VMEM)
    return pl.pallas_call(
        body, name="gather_weights",
        out_shape=(jax.ShapeDtypeStruct((D_PROJ, D_MODEL), BF16),
                   jax.ShapeDtypeStruct((D_MODEL, D_MODEL), BF16),
                   jax.ShapeDtypeStruct((N_CHIPS, W_R, W_KB // N_CHIPS), F32)),
        in_specs=[vmem, vmem, vmem], out_specs=(vmem, vmem, vmem),
        scratch_shapes=[pltpu.VMEM((N_CHIPS, 2, SHARD_PAD, HALF), BF16), pltpu.VMEM((N_CHIPS, 2, SHARD_OUT, HALF), BF16),
                        pltpu.VMEM((ACC_ROWS, D_MODEL), F32),
                        pltpu.SemaphoreType.DMA((21,)), pltpu.SemaphoreType.DMA((21,))],
        compiler_params=pltpu.CompilerParams(vmem_limit_bytes=VMEM_LIMIT),
    )(w_lin, w_out, wgu)


def _adamw(w, g, m, v):
    m = ADAM_B1 * m + (1.0 - ADAM_B1) * g
    v = ADAM_B2 * v + (1.0 - ADAM_B2) * (g * g)
    m_hat = m / (1.0 - ADAM_B1 ** ADAM_STEP)
    v_hat = v / (1.0 - ADAM_B2 ** ADAM_STEP)
    delta = -ADAM_LR * (m_hat / (jnp.sqrt(v_hat) + ADAM_EPS) + ADAM_WD * w)
    return delta, m, v


N_SMALL = 6


def _reduce_grads_call(g_in, g_out, small_grads):
    def body(gin_hbm, gout_hbm, g_lng, g_lnb, g_bg, g_nw, g_sink, g_wgu, loss_in, lin_in, fin_out, tot_out,
             a_in, a_out, b_in, b_out, c_in, s_in, s_out, r_in, r_out, f_in, f_out, pack_ref, tot_ref, pack_all,
             send_sems, recv_sems, local_sems):
        x, y, c, chips = _mesh_place()
        k_me = 2 * x + y
        me = 4 * x + 2 * y + c
        sibling = (x, y, 1 - c)

        pack_ref[...] = jnp.zeros_like(pack_ref)
        for a in range(8):
            pack_ref[P_LNG + a:P_LNG + a + 1, :] = g_lng[:, a * 128:(a + 1) * 128]
            pack_ref[P_LNB + a:P_LNB + a + 1, :] = g_lnb[:, a * 128:(a + 1) * 128]
        for a in range(2):
            pack_ref[P_BG + a:P_BG + a + 1, :] = g_bg[:, a * 128:(a + 1) * 128]
        pack_ref[P_NW:P_NW + 1, :] = g_nw[...]
        lane = lax.broadcasted_iota(jnp.int32, (1, 128), 1)
        sink_row = jnp.zeros((1, 128), F32)
        for hq in range(Q_HEADS):
            sink_row = jnp.where(lane == hq, g_sink[hq:hq + 1, :], sink_row)
        pack_ref[P_SINK:P_SINK + 1, :] = sink_row
        pack_ref[P_LOSS:P_LOSS + 1, :] = loss_in[...]
        gu_w = W_KB // N_CHIPS
        for k in range(N_CHIPS):
            pack_ref[P_GU + W_R * k:P_GU + W_R * (k + 1), 0:gu_w] = g_wgu[:, k * gu_w:(k + 1) * gu_w]
        pack_all[me] = pack_ref[...]
        small = []
        for mask in range(1, 8):
            peer = (x ^ (mask >> 2), y ^ ((mask >> 1) & 1), c ^ (mask & 1))
            small.append(pltpu.make_async_remote_copy(
                src_ref=pack_ref, dst_ref=pack_all.at[me], send_sem=send_sems.at[mask], recv_sem=recv_sems.at[mask],
                device_id=peer, device_id_type=MESH))
        for cp in small:
            cp.start()

        loads = [pltpu.make_async_copy(gin_hbm.at[c], a_in, local_sems.at[0]),
                 pltpu.make_async_copy(gout_hbm.at[c], a_out, local_sems.at[1])]
        to_sib = [pltpu.make_async_remote_copy(
                      src_ref=gin_hbm.at[1 - c], dst_ref=b_in,
                      send_sem=send_sems.at[8], recv_sem=recv_sems.at[8], device_id=sibling, device_id_type=MESH),
                  pltpu.make_async_remote_copy(
                      src_ref=gout_hbm.at[1 - c], dst_ref=b_out,
                      send_sem=send_sems.at[9], recv_sem=recv_sems.at[9], device_id=sibling, device_id_type=MESH)]
        for cp in (loads[1], to_sib[1], loads[0], to_sib[0]):
            cp.start()

        parts = ((0, GATHER_SPLIT, 0, SHARD_OUT // 2), (GATHER_SPLIT, SHARD_PAD - GATHER_SPLIT, SHARD_OUT // 2, SHARD_OUT // 2))
        nbr_x, nbr_y, _ = ((*chip, c) for chip in chips)
        k_x, k_y, k_d = (2 * chip[0] + chip[1] for chip in chips)

        def rows_of(p, w):
            r0, rn, o0, on = parts[p]
            return (o0, on) if w else (r0, rn)

        def mine(k, p, w):
            r0, rn = rows_of(p, w)
            if w:
                return a_out[k, pl.ds(r0, rn), :] + b_out[k, pl.ds(r0, rn), :]
            return c_in[k, pl.ds(r0, rn), :]

        def message(m, p, w, to):
            rn = rows_of(p, w)[1]
            stage, land = (s_out, r_out) if w else (s_in, r_in)
            return pltpu.make_async_remote_copy(
                src_ref=stage.at[m, pl.ds(0, rn), :], dst_ref=land.at[m, pl.ds(0, rn), :],
                send_sem=send_sems.at[10 + 2 * m + w], recv_sem=recv_sems.at[10 + 2 * m + w],
                device_id=to, device_id_type=MESH)

        def post(m, p, w, val, to):
            rn = rows_of(p, w)[1]
            stage = s_out if w else s_in
            stage[m, 0:rn, :] = val.astype(BF16)
            cp = message(m, p, w, to)
            cp.start()
            return [cp]

        def take(m, p, w):
            rn = rows_of(p, w)[1]
            message(m, p, w, (x, y, c)).wait_recv()
            land = r_out if w else r_in
            return land[m, 0:rn, :].astype(F32)

        sent = []
        for w in (1, 0):
            loads[w].wait()
            to_sib[w].wait_recv()
            if w == 0:
                for k in range(N_CHIPS):
                    rows = slice(k * SHARD_IN, k * SHARD_IN + SHARD_PAD)
                    c_in[k] = a_in[rows, :] + b_in[rows, :]
            sent += post(1, 0, w, mine(k_d, 0, w), nbr_x) + post(4, 1, w, mine(k_d, 1, w), nbr_y)
            sent += post(0, 0, w, mine(k_x, 0, w), nbr_x) + post(3, 1, w, mine(k_y, 1, w), nbr_y)
        for w in (1, 0):
            sent += post(5, 0, w, take(1, 0, w) + mine(k_y, 0, w), nbr_y)
            sent += post(2, 1, w, take(4, 1, w) + mine(k_x, 1, w), nbr_x)
        for w in (1, 0):
            for p, direct, summed in ((0, 0, 5), (1, 3, 2)):
                r0, rn = rows_of(p, w)
                total = mine(k_me, p, w) + take(direct, p, w) + take(summed, p, w)
                if w:
                    f_out[c, r0:r0 + rn, :] = total
                else:
                    f_in[c, r0:r0 + rn, :] = total

        swap = [pltpu.make_async_remote_copy(
                    src_ref=f_in.at[c], dst_ref=f_in.at[c],
                    send_sem=send_sems.at[22], recv_sem=recv_sems.at[22], device_id=sibling, device_id_type=MESH),
                pltpu.make_async_remote_copy(
                    src_ref=f_out.at[c], dst_ref=f_out.at[c],
                    send_sem=send_sems.at[23], recv_sem=recv_sems.at[23], device_id=sibling, device_id_type=MESH)]
        for cp in swap:
            cp.start()

        for cp in small:
            cp.wait_recv()
        total = pack_all[0]
        for d in range(1, 8):
            total = total + pack_all[d]
        tot_ref[...] = total
        tot_out[0:P_GU, :] = total[0:P_GU]
        tot_out[P_GU:PACK_OWN_ROWS, :] = tot_ref[pl.ds(pl.multiple_of(P_GU + W_R * k_me, 8), W_R), :]

        other_in = pltpu.make_async_remote_copy(
            src_ref=f_in.at[1 - c], dst_ref=f_in.at[1 - c],
            send_sem=send_sems.at[22], recv_sem=recv_sems.at[22], device_id=sibling, device_id_type=MESH)
        other_out = pltpu.make_async_remote_copy(
            src_ref=f_out.at[1 - c], dst_ref=f_out.at[1 - c],
            send_sem=send_sems.at[23], recv_sem=recv_sems.at[23], device_id=sibling, device_id_type=MESH)
        other_in.wait_recv()
        other_out.wait_recv()
        for cp in small + to_sib + sent + swap:
            cp.wait_send()

        for h in range(2):
            lin_in[:, h * HALF:(h + 1) * HALF] = f_in[h, 0:SHARD_IN, :]
            fin_out[:, h * HALF:(h + 1) * HALF] = f_out[h]

    vmem = pl.BlockSpec(memory_space=pltpu.VMEM)
    hbm = pl.BlockSpec(memory_space=pl.ANY)
    return pl.pallas_call(
        body, name="reduce_grads",
        out_shape=(jax.ShapeDtypeStruct((SHARD_IN, D_MODEL), F32), jax.ShapeDtypeStruct((SHARD_OUT, D_MODEL), F32),
                   jax.ShapeDtypeStruct((PACK_OWN_ROWS, 128), F32)),
        in_specs=[hbm, hbm] + [vmem] * 7, out_specs=(vmem,) * 3,
        scratch_shapes=[
            pltpu.VMEM((ACC_ROWS, HALF), F32), pltpu.VMEM((N_CHIPS, SHARD_OUT, HALF), F32),
            pltpu.VMEM((ACC_ROWS, HALF), F32), pltpu.VMEM((N_CHIPS, SHARD_OUT, HALF), F32),
            pltpu.VMEM((N_CHIPS, SHARD_PAD, HALF), F32),
            pltpu.VMEM((6, GATHER_SPLIT, HALF), BF16), pltpu.VMEM((6, SHARD_OUT // 2, HALF), BF16),
            pltpu.VMEM((6, GATHER_SPLIT, HALF), BF16), pltpu.VMEM((6, SHARD_OUT // 2, HALF), BF16),
            pltpu.VMEM((2, SHARD_PAD, HALF), F32), pltpu.VMEM((2, SHARD_OUT, HALF), F32),
            pltpu.VMEM((PACK_ROWS, 128), F32), pltpu.VMEM((PACK_ROWS, 128), F32), pltpu.VMEM((8, PACK_ROWS, 128), F32),
            pltpu.SemaphoreType.DMA((24,)), pltpu.SemaphoreType.DMA((24,)), pltpu.SemaphoreType.DMA((2,)),
        ],
        compiler_params=pltpu.CompilerParams(vmem_limit_bytes=VMEM_LIMIT),
    )(g_in, g_out, *small_grads)


def _adamw_call(g_in, w_in, m_in, v_in, g_out, w_out, m_out, v_out, tot, small_params):
    steps = 4
    rows_out = SHARD_OUT // steps
    cols = D_MODEL // steps
    gu_w = W_KB // N_CHIPS

    def body(gi, wi, mi, vi, go, wo, mo, vo, tot, *rest):
        params = rest[:3 * N_SMALL]
        gi_o, di, nmi, nvi, go_o, do, nmo, nvo, loss_out = rest[3 * N_SMALL:3 * N_SMALL + 9]
        small_out = rest[3 * N_SMALL + 9:]

        @pl.when(pl.program_id(0) == 0)
        def _():
            loss_out[...] = tot[P_LOSS:P_LOSS + 1, :]
            g_outs = small_out[0:N_SMALL]
            for a in range(8):
                g_outs[0][:, a * 128:(a + 1) * 128] = tot[P_LNG + a:P_LNG + a + 1, :]
                g_outs[1][:, a * 128:(a + 1) * 128] = tot[P_LNB + a:P_LNB + a + 1, :]
            for a in range(2):
                g_outs[2][:, a * 128:(a + 1) * 128] = tot[P_BG + a:P_BG + a + 1, :]
            g_outs[3][...] = tot[P_NW:P_NW + 1, :]
            g_outs[4][...] = tot[P_SINK:P_SINK + 1, 0:Q_HEADS]
            g_outs[5][...] = tot[P_GU:PACK_OWN_ROWS, 0:gu_w]
            for n in range(N_SMALL):
                w_ref, m_ref, v_ref = params[3 * n:3 * n + 3]
                delta, new_m, new_v = _adamw(w_ref[...], g_outs[n][...], m_ref[...], v_ref[...])
                small_out[N_SMALL + n][...] = delta
                small_out[2 * N_SMALL + n][...] = new_m
                small_out[3 * N_SMALL + n][...] = new_v

        g = gi[...]
        delta, new_m, new_v = _adamw(wi[:, 0, :], g, mi[:, 0, :], vi[:, 0, :])
        gi_o[:, 0, :] = g
        di[:, 0, :] = delta
        nmi[:, 0, :] = new_m
        nvi[:, 0, :] = new_v
        g = go[...]
        go_o[...] = g
        do[...], nmo[...], nvo[...] = _adamw(wo[...], g, mo[...], vo[...])

    t_g = pl.BlockSpec((SHARD_IN, cols), lambda i: (0, i))
    t_in = pl.BlockSpec((SHARD_IN, 1, cols), lambda i: (0, 0, i))
    t_out = pl.BlockSpec((rows_out, D_MODEL), lambda i: (i, 0))
    s_in = jax.ShapeDtypeStruct((SHARD_IN, 1, D_MODEL), F32)
    s_out = jax.ShapeDtypeStruct((SHARD_OUT, D_MODEL), F32)
    whole = lambda shape: pl.BlockSpec(shape, lambda i: tuple(0 for _ in shape))
    small_specs = [whole(p.shape) for p in small_params]
    small_shapes = [jax.ShapeDtypeStruct(p.shape, F32) for p in small_params[0::3]] * 4
    return pl.pallas_call(
        body, name="adamw", grid=(steps,),
        in_specs=[t_g] + [t_in] * 3 + [t_out] * 4 + [whole(tot.shape)] + small_specs,
        out_specs=(t_in,) * 4 + (t_out,) * 4 + (whole((1, 128)),) + tuple(small_specs[0::3] * 4),
        out_shape=(s_in,) * 4 + (s_out,) * 4 + (jax.ShapeDtypeStruct((1, 128), F32),) + tuple(small_shapes),
        compiler_params=pltpu.CompilerParams(dimension_semantics=("arbitrary",)),
    )(g_in, w_in, m_in, v_in, g_out, w_out, m_out, v_out, tot, *small_params)


def _rope_tables(positions):
    half = 8
    inv_freq = 500000.0 ** (-jnp.arange(half, dtype=F32) / half)
    ang = inv_freq[:, None] * positions.astype(F32)[None, :]
    return jnp.concatenate([jnp.cos(ang), jnp.sin(ang)], axis=0)


def kernel(x, positions, w_in, gla_w_gate_up, gla_b_gate, attn_sinks, gla_norm_w, w_out, ln_g, ln_b, loss_target, m_w_in, m_gla_w_gate_up, m_gla_b_gate, m_attn_sinks, m_gla_norm_w, m_w_out, m_ln_g, m_ln_b, v_w_in, v_gla_w_gate_up, v_gla_b_gate, v_attn_sinks, v_gla_norm_w, v_w_out, v_ln_g, v_ln_b):
    def lin3(w):
        return jnp.transpose(w, (2, 0, 1))

    def unlin(w):
        return jnp.transpose(w, (1, 2, 0))

    win, wout, wgu_all = _gather_weights_call(lin3(w_in), w_out[0], gla_w_gate_up[0])
    wgu = jnp.transpose(wgu_all, (1, 0, 2)).reshape(W_R, W_KB)
    cs = _rope_tables(positions[0])
    sinks = attn_sinks[0]

    (qa, ka, va, qb, kb, vb, r, dattn, dga, dob, dgb, dh, st, g_wout, g_lng, g_lnb, g_nw, loss) = _fwd_call(
        x[0], loss_target[0], cs, win, wout, wgu, gla_b_gate, sinks, gla_norm_w, ln_g, ln_b)
    gx, g_win, g_sink, g_bg, g_wgu = _bwd_call(
        x[0], dh, qa, ka, va, dattn, dga, dob, dgb, qb, kb, vb, r, st, cs, win, wgu, gla_b_gate, sinks)

    g_wout_by_chip = g_wout.reshape(2, N_CHIPS, SHARD_OUT, HALF)
    small_params = []
    for group in ((ln_g, m_ln_g, v_ln_g), (ln_b, m_ln_b, v_ln_b), (gla_b_gate, m_gla_b_gate, v_gla_b_gate),
                  (gla_norm_w, m_gla_norm_w, v_gla_norm_w), (attn_sinks, m_attn_sinks, v_attn_sinks)):
        small_params += list(group)
    small_params += [gla_w_gate_up[0], m_gla_w_gate_up[0], v_gla_w_gate_up[0]]
    fin_in, fin_out, tot = _reduce_grads_call(g_win, g_wout_by_chip, (g_lng, g_lnb, g_bg, g_nw, g_sink, g_wgu, loss))
    fin_in, d_in, nm_in, nv_in, fin_out, d_out, nm_out, nv_out, loss_sum, *small_out = _adamw_call(
        fin_in, lin3(w_in), lin3(m_w_in), lin3(v_w_in), fin_out, w_out[0], m_w_out[0], v_w_out[0], tot, small_params)

    def unpack(kind, big_in, big_out):
        lng_, lnb_, bg_, nw_, sink_, gu_ = small_out[kind * N_SMALL:(kind + 1) * N_SMALL]
        return (unlin(big_in), gu_[None], bg_, sink_, nw_, big_out[None], lng_, lnb_)

    loss_total = loss_sum[0, 0]
    g_s, d_s, nm_s, nv_s = 0, 1, 2, 3
    return (loss_total, gx[None], *unpack(g_s, fin_in, fin_out), *unpack(d_s, d_in, d_out),
            *unpack(nm_s, nm_in, nm_out), *unpack(nv_s, nv_in, nv_out))
```

```python
import functools

import jax
import jax.numpy as jnp
import numpy as np
from jax import lax
from jax.experimental import pallas as pl
from jax.experimental.pallas import tpu as pltpu

F32 = jnp.float32
BF16 = jnp.bfloat16
MESH = pl.DeviceIdType.MESH

D_MODEL = 1024
N_CHIPS = 4
W_QA, W_KA, W_VA, W_GA, W_QB, W_KB, W_VB, W_GB, W_R = 512, 128, 128, 512, 256, 256, 512, 512, 16
O_QA = 0
O_KA = O_QA + W_QA
O_VA = O_KA + W_KA
O_GA = O_VA + W_VA
O_QB = O_GA + W_GA
O_KB = O_QB + W_QB
O_VB = O_KB + W_KB
O_GB = O_VB + W_VB
O_R = O_GB + W_GB
D_PROJ = O_R + W_R
SHARD_IN = D_PROJ // N_CHIPS
SHARD_OUT = D_MODEL // N_CHIPS
SHARD_PAD = 720
ACC_ROWS = -(-((N_CHIPS - 1) * SHARD_IN + SHARD_PAD) // 8) * 8
HALF = D_MODEL // 2
GATHER_SPLIT = 368

HEAD_A = 64
Q_HEADS = 8
KV_HEADS = 2
GROUP = 4
BLOCK = 128
GLA_HEADS = 4
GLA_DK = 64
GLA_DV = 128
CHUNK = 64
GLA_TAU = 16.0
EPS = 1e-5
ALPHA = 2.0 ** 0.25
ATT_SCALE = HEAD_A ** -0.5
GLA_SCALE = GLA_DK ** -0.5

ADAM_LR = 0.001
ADAM_B1 = 0.9
ADAM_B2 = 0.999
ADAM_EPS = 1e-08
ADAM_WD = 0.01
ADAM_STEP = 10

TM = 256
TRI_SLAB = 128
VMEM_LIMIT = 56 * 1024 * 1024

P_LNG, P_LNB, P_BG, P_NW, P_SINK, P_LOSS, P_GU = 0, 8, 16, 18, 19, 20, 24
PACK_ROWS = P_GU + N_CHIPS * 16
PACK_OWN_ROWS = P_GU + 16


def _mm(a, b):
    return jnp.dot(a, b, preferred_element_type=F32)


def _mm_nt(a, b):
    return lax.dot_general(a, b, (((1,), (1,)), ((), ())), preferred_element_type=F32)


def _mm_tn(a, b):
    return lax.dot_general(a, b, (((0,), (0,)), ((), ())), preferred_element_type=F32)


def _split3(a):
    hi = a.astype(BF16)
    r1 = a - hi.astype(F32)
    mid = r1.astype(BF16)
    lo = (r1 - mid.astype(F32)).astype(BF16)
    return hi, mid, lo


def _tri_mm(tri, a):
    slab = tri.shape[0]
    hi, mid, lo = _split3(a)
    return jnp.concatenate(
        [_mm(tri, hi[s:s + slab]) + _mm(tri, mid[s:s + slab]) + _mm(tri, lo[s:s + slab])
         for s in range(0, a.shape[0], slab)], axis=0)


def _chunk_tri(n, upper):
    r = lax.broadcasted_iota(jnp.int32, (n, n), 0)
    c = lax.broadcasted_iota(jnp.int32, (n, n), 1)
    same = (r >> 6) == (c >> 6)
    order = (c >= r) if upper else (c <= r)
    return jnp.where(same & order, 1.0, 0.0).astype(BF16)


def _rope(t, cos, sa, sb):
    w = t.shape[1]
    return t * cos + pltpu.roll(t, w - 8, 1) * sa + pltpu.roll(t, 8, 1) * sb


def _rope_tile(cs):
    row = lax.broadcasted_iota(jnp.int32, (16, 128), 0)
    d = lax.broadcasted_iota(jnp.int32, (16, 128), 1) & (HEAD_A - 1)
    hit = (d & 7) == (row & 7)
    is_cos = row < 8
    lo = d < 8
    hi = (d >= 8) & (d < 16)
    pick_cos = jnp.where(hit & is_cos & (lo | hi), 1.0, 0.0).astype(BF16)
    pick_sa = jnp.where(hit & ~is_cos & lo, -1.0, 0.0).astype(BF16)
    pick_sb = jnp.where(hit & ~is_cos & hi, 1.0, 0.0).astype(BF16)
    pieces = _split3(cs)

    def spread(pick):
        return _mm_tn(pieces[0], pick) + _mm_tn(pieces[1], pick) + _mm_tn(pieces[2], pick)

    d1 = lax.broadcasted_iota(jnp.int32, (1, 128), 1) & (HEAD_A - 1)
    return spread(pick_cos) + jnp.where(d1 < 16, 0.0, 1.0), spread(pick_sa), spread(pick_sb)


def _rope_bwd(d, cos, sa, sb):
    w = d.shape[1]
    return d * cos + pltpu.roll(d * sa, 8, 1) + pltpu.roll(d * sb, w - 8, 1)


def _log_sigmoid(z):
    return jnp.minimum(z, 0.0) - jnp.log1p(jnp.exp(-jnp.abs(z)))


def _sigmoid(z):
    return 1.0 / (1.0 + jnp.exp(-z))


def _attn_bias(has_prev):
    r = lax.broadcasted_iota(jnp.int32, (GROUP * BLOCK, 2 * BLOCK), 0) & (BLOCK - 1)
    k = lax.broadcasted_iota(jnp.int32, (GROUP * BLOCK, 2 * BLOCK), 1)
    first_key = jnp.where(has_prev, 0, BLOCK)
    return jnp.where((k > r) & (k <= r + BLOCK) & (k >= first_key), 0.0, -jnp.inf)


def _sink_col(sinks_ref, j):
    r = lax.broadcasted_iota(jnp.int32, (GROUP * BLOCK, 1), 0) >> 7
    col = jnp.full((GROUP * BLOCK, 1), sinks_ref[GROUP * j], F32)
    for g in range(1, GROUP):
        col = jnp.where(r == g, sinks_ref[GROUP * j + g], col)
    return col


def _stack_heads(t, j):
    return jnp.concatenate([t[:, (GROUP * j + g) * HEAD_A:(GROUP * j + g + 1) * HEAD_A] for g in range(GROUP)], axis=0)


def _unstack_heads(parts):
    return jnp.concatenate([parts[j][g * BLOCK:(g + 1) * BLOCK] for j in range(KV_HEADS) for g in range(GROUP)], axis=1)


def _scores(qs, kc, bias):
    return _mm_nt(qs, kc) * ATT_SCALE + bias


def _softmax_block(qs, kc, bias, sink):
    return _softmax(_scores(qs, kc, bias), sink)


def _softmax(s, sink):
    m = jnp.maximum(jnp.max(s, axis=1, keepdims=True), sink)
    p = jnp.exp(s - m)
    e_sink = jnp.exp(sink - m)
    inv = 1.0 / (jnp.sum(p, axis=1, keepdims=True) + e_sink)
    return p * inv, e_sink * inv


def _fwd_call(x, tgt, cs, win, wout, wgu, bg, sinks, nw, lng, lnb):
    s_len = x.shape[0]
    nt = s_len // TM
    nblk = TM // BLOCK
    nch = TM // CHUNK

    def body(x_ref, t_ref, cs_ref, win_ref, wout_ref, wgu_ref, bg_ref, sinks_ref, nw_ref,
             lng_ref, lnb_ref,
             qa_ref, ka_ref, va_ref, qb_ref, kb_ref, vb_ref, r_ref, dattn_ref, dga_ref, dob_ref, dgb_ref, dh_ref,
             st_ref, dwout_ref, glng_ref, glnb_ref, gnw_ref, loss_ref,
             kprev, vprev, state, attn_s, ga_s, ob_s, gb_s, cat_s, dwout_acc):
        i = pl.program_id(0)

        @pl.when(i == 0)
        def _():
            kprev[...] = jnp.zeros_like(kprev)
            vprev[...] = jnp.zeros_like(vprev)
            state[...] = jnp.zeros_like(state)
            dwout_acc[...] = jnp.zeros_like(dwout_acc)
            glng_ref[...] = jnp.zeros_like(glng_ref)
            glnb_ref[...] = jnp.zeros_like(glnb_ref)
            gnw_ref[...] = jnp.zeros_like(gnw_ref)
            loss_ref[...] = jnp.zeros_like(loss_ref)

        x = x_ref[...]
        xb = x.astype(BF16)

        def proj(off, width):
            return _mm_nt(xb, win_ref[off:off + width, :])

        cos, sa, sb = _rope_tile(cs_ref[...])
        cos4, sa4, sb4 = (jnp.concatenate([t] * 4, axis=1) for t in (cos, sa, sb))
        qa = _rope(proj(O_QA, W_QA), cos4, sa4, sb4).astype(BF16)
        ka = _rope(proj(O_KA, W_KA), cos, sa, sb).astype(BF16)
        va = proj(O_VA, W_VA).astype(BF16)
        qa_ref[...] = qa
        ka_ref[...] = ka
        va_ref[...] = va

        later = {}

        def fill_ga():
            ga_s[...] = proj(O_GA, W_GA)

        def fill_gb():
            gb_s[...] = proj(O_GB, W_GB)

        def fill_qk():
            later["qb"] = proj(O_QB, W_QB)
            later["kb"] = proj(O_KB, W_KB)

        def fill_vr():
            later["vb"] = proj(O_VB, W_VB).astype(BF16)
            later["r"] = proj(O_R, W_R)

        fillers = [fill_ga, fill_gb, fill_qk, fill_vr]

        bias_inner = _attn_bias(True)
        sink_cols = [_sink_col(sinks_ref, j) for j in range(KV_HEADS)]
        for b in range(nblk):
            rows = slice(b * BLOCK, (b + 1) * BLOCK)
            mask = _attn_bias(i > 0) if b == 0 else bias_inner
            k_cur = ka[rows]
            v_cur = va[rows]
            k_old = kprev[...] if b == 0 else ka[(b - 1) * BLOCK:b * BLOCK]
            v_old = vprev[...] if b == 0 else va[(b - 1) * BLOCK:b * BLOCK]
            outs = []
            for j in range(KV_HEADS):
                hs = slice(j * HEAD_A, (j + 1) * HEAD_A)
                kc = jnp.concatenate([k_old[:, hs], k_cur[:, hs]], axis=0)
                vc = jnp.concatenate([v_old[:, hs], v_cur[:, hs]], axis=0)
                s = _scores(_stack_heads(qa[rows], j), kc, mask)
                if fillers:
                    fillers.pop(0)()
                probs, _ = _softmax(s, sink_cols[j])
                outs.append(_mm(probs.astype(BF16), vc))
            attn_s[rows, :] = _unstack_heads(outs)
        kprev[...] = ka[(nblk - 1) * BLOCK:]
        vprev[...] = va[(nblk - 1) * BLOCK:]
        while fillers:
            fillers.pop(0)()

        r = later["r"]
        r_ref[...] = r
        z = _mm(r.astype(BF16), wgu_ref[...].astype(BF16)) + bg_ref[...]
        log_a = _log_sigmoid(z) / GLA_TAU
        bcum = _tri_mm(_chunk_tri(TRI_SLAB, False), log_a)
        qb = later["qb"]
        kb = later["kb"]
        vb = later["vb"]
        qb_ref[...] = qb
        kb_ref[...] = kb
        vb_ref[...] = vb
        qd_all = (qb * GLA_SCALE * jnp.exp(bcum)).astype(BF16)
        ki_all = (kb * jnp.exp(-bcum)).astype(BF16)
        tril = lax.broadcasted_iota(jnp.int32, (CHUNK, CHUNK), 0) >= lax.broadcasted_iota(jnp.int32, (CHUNK, CHUNK), 1)
        st = state[...]
        for c in range(nch):
            rows = slice(c * CHUNK, (c + 1) * CHUNK)
            b_c = bcum[rows]
            b_last = b_c[CHUNK - 1:CHUNK]
            ke = (kb[rows] * jnp.exp(b_last - b_c)).astype(BF16)
            st_ref[c] = st
            st16 = st.astype(BF16)
            o_parts, u_parts = [], []
            for h in range(GLA_HEADS):
                ks = slice(h * GLA_DK, (h + 1) * GLA_DK)
                vs = slice(h * GLA_DV, (h + 1) * GLA_DV)
                qd = qd_all[rows, ks]
                v_h = vb[rows, vs]
                a = jnp.where(tril, _mm_nt(qd, ki_all[rows, ks]), 0.0)
                o_parts.append(_mm(a.astype(BF16), v_h) + _mm_nt(qd, st16[:, ks]))
                u_parts.append(_mm_tn(v_h, ke[:, ks]))
            ob_s[rows, :] = jnp.concatenate(o_parts, axis=1)
            st = st * jnp.exp(b_last) + jnp.concatenate(u_parts, axis=1)
        state[...] = st

        ga = ga_s[...]
        sg_a = _sigmoid(ga)
        silu_a = ga * sg_a
        attn = attn_s[...]
        cat_s[:, :W_GA] = (attn * silu_a).astype(BF16)
        gb = gb_s[...]
        sg_b = _sigmoid(gb)
        silu_b = gb * sg_b
        nw = nw_ref[...]
        on_parts = []
        for h in range(GLA_HEADS):
            vs = slice(h * GLA_DV, (h + 1) * GLA_DV)
            o_h = ob_s[:, vs]
            rs = lax.rsqrt(jnp.mean(o_h * o_h, axis=1, keepdims=True) + EPS)
            on_parts.append(o_h * rs * nw)
        on = jnp.concatenate(on_parts, axis=1)
        cat_s[:, W_GA:] = (on * silu_b).astype(BF16)
        cat = cat_s[...]
        hres = ALPHA * x + _mm(cat, wout_ref[...])
        mu = jnp.mean(hres, axis=1, keepdims=True)
        hc = hres - mu
        rstd = lax.rsqrt(jnp.mean(hc * hc, axis=1, keepdims=True) + EPS)
        xhat = hc * rstd
        g_ln = lng_ref[...]
        err = xhat * g_ln + lnb_ref[...] - t_ref[...]
        loss_ref[...] += jnp.sum(err * err) * (0.5 / D_MODEL)
        dy = err * (1.0 / D_MODEL)
        glng_ref[...] += jnp.sum(dy * xhat, axis=0, keepdims=True)
        glnb_ref[...] += jnp.sum(dy, axis=0, keepdims=True)
        dxh = dy * g_ln
        dh = rstd * (dxh - jnp.mean(dxh, axis=1, keepdims=True) - xhat * jnp.mean(dxh * xhat, axis=1, keepdims=True))
        dh_ref[...] = dh
        dh16 = dh.astype(BF16)
        for h in range(2):
            dwout_acc[h] += _mm_tn(cat, dh16[:, h * HALF:(h + 1) * HALF])

        @pl.when(i == nt - 1)
        def _():
            dwout_ref[...] = dwout_acc[...].astype(BF16)
        dcat = _mm_nt(dh16, wout_ref[...])

        d_a = dcat[:, :W_GA]
        dattn_ref[...] = (d_a * silu_a).astype(BF16)
        dga_ref[...] = (d_a * attn * (sg_a * (1.0 + ga * (1.0 - sg_a)))).astype(BF16)
        d_b = dcat[:, W_GA:]
        dgb_ref[...] = (d_b * on * (sg_b * (1.0 + gb * (1.0 - sg_b)))).astype(BF16)
        d_on = d_b * silu_b
        gnw = jnp.zeros((1, GLA_DV), F32)
        do_parts = []
        for h in range(GLA_HEADS):
            vs = slice(h * GLA_DV, (h + 1) * GLA_DV)
            o_h = ob_s[:, vs]
            rs = lax.rsqrt(jnp.mean(o_h * o_h, axis=1, keepdims=True) + EPS)
            d_on_h = d_on[:, vs]
            gnw = gnw + jnp.sum(d_on_h * o_h * rs, axis=0, keepdims=True)
            gg = d_on_h * nw
            do_parts.append(rs * gg - o_h * (rs * rs * rs) * jnp.mean(gg * o_h, axis=1, keepdims=True))
        gnw_ref[...] += gnw
        dob_ref[...] = jnp.concatenate(do_parts, axis=1).astype(BF16)

    tile = lambda w: pl.BlockSpec((TM, w), lambda i: (i, 0))
    whole = lambda shape: pl.BlockSpec(shape, lambda i: tuple(0 for _ in shape), pipeline_mode=pl.Buffered(1))
    out_shape = (
        jax.ShapeDtypeStruct((s_len, W_QA), BF16),
        jax.ShapeDtypeStruct((s_len, W_KA), BF16),
        jax.ShapeDtypeStruct((s_len, W_VA), BF16),
        jax.ShapeDtypeStruct((s_len, W_QB), F32),
        jax.ShapeDtypeStruct((s_len, W_KB), F32),
        jax.ShapeDtypeStruct((s_len, W_VB), BF16),
        jax.ShapeDtypeStruct((s_len, W_R), F32),
        jax.ShapeDtypeStruct((s_len, W_GA), BF16),
        jax.ShapeDtypeStruct((s_len, W_GA), BF16),
        jax.ShapeDtypeStruct((s_len, W_GB), BF16),
        jax.ShapeDtypeStruct((s_len, W_GB), BF16),
        jax.ShapeDtypeStruct((s_len, D_MODEL), F32),
        jax.ShapeDtypeStruct((s_len // CHUNK, GLA_DV, GLA_HEADS * GLA_DK), F32),
        jax.ShapeDtypeStruct((2, D_MODEL, HALF), BF16),
        jax.ShapeDtypeStruct((1, D_MODEL), F32),
        jax.ShapeDtypeStruct((1, D_MODEL), F32),
        jax.ShapeDtypeStruct((1, GLA_DV), F32),
        jax.ShapeDtypeStruct((1, 128), F32),
    )
    out_specs = (
        tile(W_QA), tile(W_KA), tile(W_VA), tile(W_QB), tile(W_KB), tile(W_VB), tile(W_R),
        tile(W_GA), tile(W_GA), tile(W_GB), tile(W_GB), tile(D_MODEL),
        pl.BlockSpec((nch, GLA_DV, GLA_HEADS * GLA_DK), lambda i: (i, 0, 0)),
        whole((2, D_MODEL, HALF)), whole((1, D_MODEL)), whole((1, D_MODEL)), whole((1, GLA_DV)), whole((1, 128)),
    )
    in_specs = [
        tile(D_MODEL), tile(D_MODEL), pl.BlockSpec((16, TM), lambda i: (0, i)),
        whole((D_PROJ, D_MODEL)), whole((D_MODEL, D_MODEL)), whole((W_R, W_KB)), whole((1, W_KB)),
        pl.BlockSpec(memory_space=pltpu.SMEM), whole((1, GLA_DV)), whole((1, D_MODEL)), whole((1, D_MODEL)),
    ]
    scratch = [
        pltpu.VMEM((BLOCK, W_KA), BF16), pltpu.VMEM((BLOCK, W_VA), BF16),
        pltpu.VMEM((GLA_DV, GLA_HEADS * GLA_DK), F32),
        pltpu.VMEM((TM, W_GA), F32), pltpu.VMEM((TM, W_GA), F32), pltpu.VMEM((TM, W_GB), F32),
        pltpu.VMEM((TM, W_GB), F32), pltpu.VMEM((TM, D_MODEL), BF16), pltpu.VMEM((2, D_MODEL, HALF), F32),
    ]
    return pl.pallas_call(
        body, name="fwd_head", grid=(nt,), in_specs=in_specs, out_specs=out_specs, out_shape=out_shape,
        scratch_shapes=scratch,
        compiler_params=pltpu.CompilerParams(dimension_semantics=("arbitrary",), vmem_limit_bytes=VMEM_LIMIT),
    )(x, tgt, cs, win, wout, wgu, bg, sinks, nw, lng, lnb)


def _bwd_call(x, dh, qa, ka, va, dattn, dga, dob, dgb, qb, kb, vb, r, st, cs, win, wgu, bg, sinks):
    s_len = x.shape[0]
    nt = s_len // TM
    nblk = TM // BLOCK
    nch = TM // CHUNK

    def body(x_ref, dh_ref, qa_ref, ka_ref, va_ref, kap_ref, vap_ref, dattn_ref, dga_ref, dob_ref, dgb_ref,
             qb_ref, kb_ref, vb_ref, r_ref, st_ref, cs_ref, win_ref, wgu_ref, bg_ref, sinks_ref,
             gx_ref, dwin_ref, gsink_ref, gbg_ref, gwgu_ref,
             dproj, dk_carry, dv_carry, ds_carry, db_s, dwin_acc):
        i = pl.program_id(0)
        t = nt - 1 - i

        @pl.when(i == 0)
        def _():
            dk_carry[...] = jnp.zeros_like(dk_carry)
            dv_carry[...] = jnp.zeros_like(dv_carry)
            ds_carry[...] = jnp.zeros_like(ds_carry)
            dwin_acc[...] = jnp.zeros_like(dwin_acc)
            gsink_ref[...] = jnp.zeros_like(gsink_ref)
            gbg_ref[...] = jnp.zeros_like(gbg_ref)
            gwgu_ref[...] = jnp.zeros_like(gwgu_ref)

        cos, sa, sb = _rope_tile(cs_ref[...])
        cos4, sa4, sb4 = (jnp.concatenate([v] * 4, axis=1) for v in (cos, sa, sb))

        qa = qa_ref[...]
        ka = ka_ref[...]
        va = va_ref[...]
        dattn = dattn_ref[...]
        gsink_rows = [jnp.zeros((1, 1), F32) for _ in range(Q_HEADS)]
        bias_inner = _attn_bias(True)
        sink_cols = [_sink_col(sinks_ref, j) for j in range(KV_HEADS)]
        for b in reversed(range(nblk)):
            rows = slice(b * BLOCK, (b + 1) * BLOCK)
            mask = _attn_bias(t > 0) if b == 0 else bias_inner
            k_cur = ka[rows]
            v_cur = va[rows]
            k_old = kap_ref[...] if b == 0 else ka[(b - 1) * BLOCK:b * BLOCK]
            v_old = vap_ref[...] if b == 0 else va[(b - 1) * BLOCK:b * BLOCK]
            dq_parts, dk_parts, dv_parts = [], [], []
            for j in range(KV_HEADS):
                hs = slice(j * HEAD_A, (j + 1) * HEAD_A)
                kc = jnp.concatenate([k_old[:, hs], k_cur[:, hs]], axis=0)
                vc = jnp.concatenate([v_old[:, hs], v_cur[:, hs]], axis=0)
                qs = _stack_heads(qa[rows], j)
                do_s = _stack_heads(dattn[rows], j)
                probs, p_sink = _softmax_block(qs, kc, mask, sink_cols[j])
                dp = _mm_nt(do_s, vc)
                d_row = jnp.sum(probs * dp, axis=1, keepdims=True)
                ds16 = (probs * (dp - d_row) * ATT_SCALE).astype(BF16)
                dq_parts.append(_mm(ds16, kc))
                dk_parts.append(_mm_tn(ds16, qs))
                dv_parts.append(_mm_tn(probs.astype(BF16), do_s))
                t_sink = d_row * p_sink
                for g in range(GROUP):
                    gsink_rows[GROUP * j + g] = gsink_rows[GROUP * j + g] - jnp.sum(
                        t_sink[g * BLOCK:(g + 1) * BLOCK], axis=0, keepdims=True)
            dq = _rope_bwd(_unstack_heads(dq_parts), cos4[rows], sa4[rows], sb4[rows])
            dproj[rows, O_QA:O_QA + W_QA] = dq.astype(BF16)
            dk_cur = dk_carry[...] + jnp.concatenate([p[BLOCK:] for p in dk_parts], axis=1)
            dv_cur = dv_carry[...] + jnp.concatenate([p[BLOCK:] for p in dv_parts], axis=1)
            dproj[rows, O_KA:O_KA + W_KA] = _rope_bwd(dk_cur, cos[rows], sa[rows], sb[rows]).astype(BF16)
            dproj[rows, O_VA:O_VA + W_VA] = dv_cur.astype(BF16)
            dk_carry[...] = jnp.concatenate([p[:BLOCK] for p in dk_parts], axis=1)
            dv_carry[...] = jnp.concatenate([p[:BLOCK] for p in dv_parts], axis=1)
        for hq in range(Q_HEADS):
            gsink_ref[hq:hq + 1, :] += jnp.broadcast_to(gsink_rows[hq], (1, 128))

        dproj[:, O_GA:O_GA + W_GA] = dga_ref[...]
        dproj[:, O_GB:O_GB + W_GB] = dgb_ref[...]

        r16 = r_ref[...].astype(BF16)
        wgu16 = wgu_ref[...].astype(BF16)
        z = _mm(r16, wgu16) + bg_ref[...]
        log_a = _log_sigmoid(z) / GLA_TAU
        bcum = _tri_mm(_chunk_tri(TRI_SLAB, False), log_a)
        qb = qb_ref[...]
        kb = kb_ref[...]
        vb = vb_ref[...]
        dob = dob_ref[...]
        e_b = jnp.exp(bcum)
        e_nb = jnp.exp(-bcum)
        qd_f = qb * GLA_SCALE * e_b
        ki_f = kb * e_nb
        qd_all = qd_f.astype(BF16)
        ki_all = ki_f.astype(BF16)
        tril = lax.broadcasted_iota(jnp.int32, (CHUNK, CHUNK), 0) >= lax.broadcasted_iota(jnp.int32, (CHUNK, CHUNK), 1)
        last_row = lax.broadcasted_iota(jnp.int32, (CHUNK, 1), 0) == CHUNK - 1

        x16 = x_ref[...].astype(BF16)
        dp_a = dproj[:, 0:O_QB]
        early = {}

        def fill_dx():
            early["gx"] = _mm(dp_a, win_ref[0:O_QB, :])

        def fill_dw(h):
            dwin_acc[h, 0:O_QB, :] += _mm_tn(dp_a, x16[:, h * HALF:(h + 1) * HALF])

        fillers = [fill_dx, functools.partial(fill_dw, 0), functools.partial(fill_dw, 1)]
        dsn = ds_carry[...]
        for c in reversed(range(nch)):
            if fillers:
                fillers.pop(0)()
            rows = slice(c * CHUNK, (c + 1) * CHUNK)
            b_c = bcum[rows]
            b_last = b_c[CHUNK - 1:CHUNK]
            e_e = jnp.exp(b_last - b_c)
            dec = jnp.exp(b_last)
            ke_f = kb[rows] * e_e
            ke = ke_f.astype(BF16)
            sp = st_ref[c]
            sp16 = sp.astype(BF16)
            dsn16 = dsn.astype(BF16)
            dqd_p, dki_p, dke_p, dv_p, dsp_p = [], [], [], [], []
            for h in range(GLA_HEADS):
                ks = slice(h * GLA_DK, (h + 1) * GLA_DK)
                vs = slice(h * GLA_DV, (h + 1) * GLA_DV)
                qd = qd_all[rows, ks]
                ki = ki_all[rows, ks]
                v_h = vb[rows, vs]
                do_h = dob[rows, vs]
                a16 = jnp.where(tril, _mm_nt(qd, ki), 0.0).astype(BF16)
                da16 = jnp.where(tril, _mm_nt(do_h, v_h), 0.0).astype(BF16)
                dv_p.append(_mm_tn(a16, do_h) + _mm_nt(ke[:, ks], dsn16[:, ks]))
                dqd_p.append(_mm(da16, ki) + _mm(do_h, sp16[:, ks]))
                dki_p.append(_mm_tn(da16, qd))
                dke_p.append(_mm(v_h, dsn16[:, ks]))
                dsp_p.append(_mm_tn(do_h, qd))
            dqd = jnp.concatenate(dqd_p, axis=1)
            dki = jnp.concatenate(dki_p, axis=1)
            dke = jnp.concatenate(dke_p, axis=1)
            ddec = jnp.sum(dsn * sp, axis=0, keepdims=True)
            dsn_next = dsn * dec + jnp.concatenate(dsp_p, axis=1)
            dproj[rows, O_QB:O_QB + W_QB] = (dqd * e_b[rows] * GLA_SCALE).astype(BF16)
            dproj[rows, O_KB:O_KB + W_KB] = (dki * e_nb[rows] + dke * e_e).astype(BF16)
            dproj[rows, O_VB:O_VB + W_VB] = jnp.concatenate(dv_p, axis=1).astype(BF16)
            dke_ke = dke * ke_f
            d_b = dqd * qd_f[rows] - dki * ki_f[rows] - dke_ke
            d_bl = jnp.sum(dke_ke, axis=0, keepdims=True) + ddec * dec
            db_s[rows, :] = d_b + jnp.where(last_row, d_bl, 0.0)
            dsn = dsn_next
        ds_carry[...] = dsn
        while fillers:
            fillers.pop(0)()
        dlog_a = _tri_mm(_chunk_tri(TRI_SLAB, True), db_s[...])
        dz = dlog_a * (1.0 / GLA_TAU) * _sigmoid(-z)
        dz16 = dz.astype(BF16)
        gbg_ref[...] += jnp.sum(dz, axis=0, keepdims=True)
        gwgu_ref[...] += _mm_tn(r16, dz16)
        dproj[:, O_R:O_R + W_R] = _mm_nt(dz16, wgu16).astype(BF16)

        dp_b = dproj[:, O_QB:D_PROJ]
        gx_ref[...] = ALPHA * dh_ref[...] + early["gx"] + _mm(dp_b, win_ref[O_QB:D_PROJ, :])
        for h in range(2):
            dwin_acc[h, O_QB:D_PROJ, :] += _mm_tn(dp_b, x16[:, h * HALF:(h + 1) * HALF])

        @pl.when(i == nt - 1)
        def _():
            dwin_ref[...] = dwin_acc[...].astype(BF16)

    tile = lambda w: pl.BlockSpec((TM, w), lambda i: (nt - 1 - i, 0))
    whole = lambda shape: pl.BlockSpec(shape, lambda i: tuple(0 for _ in shape), pipeline_mode=pl.Buffered(1))
    prev_blk = pl.BlockSpec((BLOCK, W_KA), lambda i: (jnp.maximum((nt - 1 - i) * nblk - 1, 0), 0))
    in_specs = [
        tile(D_MODEL), tile(D_MODEL), tile(W_QA), tile(W_KA), tile(W_VA), prev_blk, prev_blk,
        tile(W_GA), tile(W_GA), tile(W_GB), tile(W_GB), tile(W_QB), tile(W_KB), tile(W_VB), tile(W_R),
        pl.BlockSpec((nch, GLA_DV, GLA_HEADS * GLA_DK), lambda i: (nt - 1 - i, 0, 0)),
        pl.BlockSpec((16, TM), lambda i: (0, nt - 1 - i)),
        whole((D_PROJ, D_MODEL)), whole((W_R, W_KB)), whole((1, W_KB)), pl.BlockSpec(memory_space=pltpu.SMEM),
    ]
    out_shape = (
        jax.ShapeDtypeStruct((s_len, D_MODEL), F32),
        jax.ShapeDtypeStruct((2, ACC_ROWS, HALF), BF16),
        jax.ShapeDtypeStruct((Q_HEADS, 128), F32),
        jax.ShapeDtypeStruct((1, W_KB), F32),
        jax.ShapeDtypeStruct((W_R, W_KB), F32),
    )
    out_specs = (tile(D_MODEL), whole((2, ACC_ROWS, HALF)), whole((Q_HEADS, 128)), whole((1, W_KB)),
                 whole((W_R, W_KB)))
    scratch = [
        pltpu.VMEM((TM, D_PROJ), BF16), pltpu.VMEM((BLOCK, W_KA), F32), pltpu.VMEM((BLOCK, W_VA), F32),
        pltpu.VMEM((GLA_DV, GLA_HEADS * GLA_DK), F32), pltpu.VMEM((TM, W_KB), F32),
        pltpu.VMEM((2, ACC_ROWS, HALF), F32),
    ]
    return pl.pallas_call(
        body, name="bwd_mix", grid=(nt,), in_specs=in_specs, out_specs=out_specs, out_shape=out_shape,
        scratch_shapes=scratch,
        compiler_params=pltpu.CompilerParams(dimension_semantics=("arbitrary",), vmem_limit_bytes=VMEM_LIMIT),
    )(x, dh, qa, ka, va, ka, va, dattn, dga, dob, dgb, qb, kb, vb, r, st, cs, win, wgu, bg, sinks)


def _mesh_place():
    x, y, c = lax.axis_index("x"), lax.axis_index("y"), lax.axis_index("c")
    chips = [(1 - x, y), (x, 1 - y), (1 - x, 1 - y)]
    return x, y, c, chips


def _gather_weights_call(w_lin, w_out, wgu):
    def body(wlin_ref, wout_ref, wgu_ref, wt_ref, wout_full, wgu_all, blk, oblk, asm, send_sems, recv_sems):
        x, y, c, chips = _mesh_place()
        k_me = 2 * x + y
        asm[SHARD_IN - 4:SHARD_PAD, :] = jnp.zeros((SHARD_PAD - SHARD_IN + 4, D_MODEL), F32)
        asm[0:SHARD_IN, :] = wlin_ref[:, 0, :]
        for h in range(2):
            blk[k_me, h] = asm[0:SHARD_PAD, h * HALF:(h + 1) * HALF].astype(BF16)
            oblk[k_me, h] = wout_ref[:, h * HALF:(h + 1) * HALF].astype(BF16)
        wgu_all[k_me] = wgu_ref[...]

        parts = ((0, GATHER_SPLIT, 0, SHARD_OUT // 2), (GATHER_SPLIT, SHARD_PAD - GATHER_SPLIT, SHARD_OUT // 2, SHARD_OUT // 2))
        me_id, sib_id = (x, y, c), (x, y, 1 - c)
        nbr_x, nbr_y, diag = ((*chip, c) for chip in chips)
        k_x, k_y, k_d = (2 * chip[0] + chip[1] for chip in chips)

        def copies(k, hc, p, sem0, to):
            if p is None:
                refs = (blk.at[k, hc], oblk.at[k, hc])
            else:
                r0, rn, o0, on = parts[p]
                refs = (blk.at[k, hc, pl.ds(r0, rn), :], oblk.at[k, hc, pl.ds(o0, on), :])
            return [pltpu.make_async_remote_copy(src_ref=ref, dst_ref=ref, send_sem=send_sems.at[sem0 + n],
                                                 recv_sem=recv_sems.at[sem0 + n], device_id=to, device_id_type=MESH)
                    for n, ref in enumerate(refs)]

        def gu_copy(k, r, to):
            return pltpu.make_async_remote_copy(src_ref=wgu_all.at[k], dst_ref=wgu_all.at[k], send_sem=send_sems.at[18 + r],
                                                recv_sem=recv_sems.at[18 + r], device_id=to, device_id_type=MESH)

        def start(cps):
            for cp in cps:
                cp.start()
            return cps

        def landed(cps):
            for cp in cps:
                cp.wait_recv()

        started = start(copies(k_me, c, 0, 0, nbr_x) + copies(k_me, c, 1, 6, nbr_y)
                        + copies(k_me, c, 1, 2, nbr_x) + copies(k_me, c, 0, 4, nbr_y)
                        + [gu_copy(k_me, r, to) for r, to in enumerate((nbr_x, nbr_y, diag))])
        landed(copies(k_x, c, 0, 0, me_id))
        started += start(copies(k_x, c, 0, 8, nbr_y))
        landed(copies(k_y, c, 1, 6, me_id))
        started += start(copies(k_y, c, 1, 10, nbr_x))
        landed(copies(k_x, c, 1, 2, me_id))
        started += start(copies(k_x, c, None, 12, sib_id))
        landed(copies(k_y, c, 0, 4, me_id))
        started += start(copies(k_y, c, None, 14, sib_id))
        landed(copies(k_d, c, 0, 8, me_id) + copies(k_d, c, 1, 10, me_id))
        started += start(copies(k_d, c, None, 16, sib_id))
        for r, k_r in enumerate((k_x, k_y, k_d)):
            landed(copies(k_r, 1 - c, None, 12 + 2 * r, me_id))
            gu_copy(k_r, r, me_id).wait_recv()
        for cp in started:
            cp.wait_send()

        for k in range(N_CHIPS):
            for h in range(2):
                asm[k * SHARD_IN:k * SHARD_IN + SHARD_PAD, h * HALF:(h + 1) * HALF] = blk[k, h].astype(F32)
                wout_full[k * SHARD_OUT:(k + 1) * SHARD_OUT, h * HALF:(h + 1) * HALF] = oblk[k, h]
        wt_ref[...] = asm[0:D_PROJ, :].astype(BF16)

    vmem = pl.BlockSpec(memory_space=pltpu.VMEM)
    return pl.pallas_call(
        body, name="gather_weights",
        out_shape=(jax.ShapeDtypeStruct((D_PROJ, D_MODEL), BF16),
                   jax.ShapeDtypeStruct((D_MODEL, D_MODEL), BF16),
                   jax.ShapeDtypeStruct((N_CHIPS, W_R, W_KB // N_CHIPS), F32)),
        in_specs=[vmem, vmem, vmem], out_specs=(vmem, vmem, vmem),
        scratch_shapes=[pltpu.VMEM((N_CHIPS, 2, SHARD_PAD, HALF), BF16), pltpu.VMEM((N_CHIPS, 2, SHARD_OUT, HALF), BF16),
                        pltpu.VMEM((ACC_ROWS, D_MODEL), F32),
                        pltpu.SemaphoreType.DMA((21,)), pltpu.SemaphoreType.DMA((21,))],
        compiler_params=pltpu.CompilerParams(vmem_limit_bytes=VMEM_LIMIT),
    )(w_lin, w_out, wgu)


def _adamw(w, g, m, v):
    m = ADAM_B1 * m + (1.0 - ADAM_B1) * g
    v = ADAM_B2 * v + (1.0 - ADAM_B2) * (g * g)
    m_hat = m / (1.0 - ADAM_B1 ** ADAM_STEP)
    v_hat = v / (1.0 - ADAM_B2 ** ADAM_STEP)
    delta = -ADAM_LR * (m_hat / (jnp.sqrt(v_hat) + ADAM_EPS) + ADAM_WD * w)
    return delta, m, v


N_SMALL = 6


def _reduce_grads_call(g_in, g_out, small_grads):
    def body(gin_hbm, gout_hbm, g_lng, g_lnb, g_bg, g_nw, g_sink, g_wgu, loss_in, lin_in, fin_out, tot_out,
             a_in, a_out, b_in, b_out, ab_in, c_in, s_in, s_out, r_in, r_out, f_in, f_out, pack_ref, tot_ref, pack_all,
             send_sems, recv_sems, local_sems):
        x, y, c, chips = _mesh_place()
        k_me = 2 * x + y
        me = 4 * x + 2 * y + c
        sibling = (x, y, 1 - c)

        pack_ref[...] = jnp.zeros_like(pack_ref)
        for a in range(8):
            pack_ref[P_LNG + a:P_LNG + a + 1, :] = g_lng[:, a * 128:(a + 1) * 128]
            pack_ref[P_LNB + a:P_LNB + a + 1, :] = g_lnb[:, a * 128:(a + 1) * 128]
        for a in range(2):
            pack_ref[P_BG + a:P_BG + a + 1, :] = g_bg[:, a * 128:(a + 1) * 128]
        pack_ref[P_NW:P_NW + 1, :] = g_nw[...]
        lane = lax.broadcasted_iota(jnp.int32, (1, 128), 1)
        sink_row = jnp.zeros((1, 128), F32)
        for hq in range(Q_HEADS):
            sink_row = jnp.where(lane == hq, g_sink[hq:hq + 1, :], sink_row)
        pack_ref[P_SINK:P_SINK + 1, :] = sink_row
        pack_ref[P_LOSS:P_LOSS + 1, :] = loss_in[...]
        gu_w = W_KB // N_CHIPS
        for k in range(N_CHIPS):
            pack_ref[P_GU + W_R * k:P_GU + W_R * (k + 1), 0:gu_w] = g_wgu[:, k * gu_w:(k + 1) * gu_w]
        pack_all[me] = pack_ref[...]
        small = []
        for mask in range(1, 8):
            peer = (x ^ (mask >> 2), y ^ ((mask >> 1) & 1), c ^ (mask & 1))
            small.append(pltpu.make_async_remote_copy(
                src_ref=pack_ref, dst_ref=pack_all.at[me], send_sem=send_sems.at[mask], recv_sem=recv_sems.at[mask],
                device_id=peer, device_id_type=MESH))
        for cp in small:
            cp.start()

        loads = [pltpu.make_async_copy(gin_hbm.at[c], a_in, local_sems.at[0]),
                 pltpu.make_async_copy(gout_hbm.at[c], a_out, local_sems.at[1])]
        to_sib = [pltpu.make_async_remote_copy(
                      src_ref=gin_hbm.at[1 - c], dst_ref=b_in,
                      send_sem=send_sems.at[8], recv_sem=recv_sems.at[8], device_id=sibling, device_id_type=MESH),
                  pltpu.make_async_remote_copy(
                      src_ref=gout_hbm.at[1 - c], dst_ref=b_out,
                      send_sem=send_sems.at[9], recv_sem=recv_sems.at[9], device_id=sibling, device_id_type=MESH)]
        for cp in (loads[1], to_sib[1], loads[0], to_sib[0]):
            cp.start()

        parts = ((0, GATHER_SPLIT, 0, SHARD_OUT // 2), (GATHER_SPLIT, SHARD_PAD - GATHER_SPLIT, SHARD_OUT // 2, SHARD_OUT // 2))
        nbr_x, nbr_y, _ = ((*chip, c) for chip in chips)
        k_x, k_y, k_d = (2 * chip[0] + chip[1] for chip in chips)

        def rows_of(p, w):
            r0, rn, o0, on = parts[p]
            return (o0, on) if w else (r0, rn)

        def mine(k, p, w):
            r0, rn = rows_of(p, w)
            if w:
                return a_out[k, pl.ds(r0, rn), :].astype(F32) + b_out[k, pl.ds(r0, rn), :].astype(F32)
            return c_in[k, pl.ds(r0, rn), :]

        def message(m, p, w, to):
            rn = rows_of(p, w)[1]
            stage, land = (s_out, r_out) if w else (s_in, r_in)
            return pltpu.make_async_remote_copy(
                src_ref=stage.at[m, pl.ds(0, rn), :], dst_ref=land.at[m, pl.ds(0, rn), :],
                send_sem=send_sems.at[10 + 2 * m + w], recv_sem=recv_sems.at[10 + 2 * m + w],
                device_id=to, device_id_type=MESH)

        def post(m, p, w, val, to):
            rn = rows_of(p, w)[1]
            stage = s_out if w else s_in
            stage[m, 0:rn, :] = val.astype(BF16)
            cp = message(m, p, w, to)
            cp.start()
            return [cp]

        def take(m, p, w):
            rn = rows_of(p, w)[1]
            message(m, p, w, (x, y, c)).wait_recv()
            land = r_out if w else r_in
            return land[m, 0:rn, :].astype(F32)

        sent = []
        for w in (1, 0):
            loads[w].wait()
            to_sib[w].wait_recv()
            if w == 0:
                ab_in[...] = a_in[...].astype(F32) + b_in[...].astype(F32)
                for k in range(N_CHIPS):
                    c_in[k] = ab_in[k * SHARD_IN:k * SHARD_IN + SHARD_PAD, :]
            sent += post(1, 0, w, mine(k_d, 0, w), nbr_x) + post(4, 1, w, mine(k_d, 1, w), nbr_y)
            sent += post(0, 0, w, mine(k_x, 0, w), nbr_x) + post(3, 1, w, mine(k_y, 1, w), nbr_y)
        for w in (1, 0):
            sent += post(5, 0, w, take(1, 0, w) + mine(k_y, 0, w), nbr_y)
            sent += post(2, 1, w, take(4, 1, w) + mine(k_x, 1, w), nbr_x)
        for w in (1, 0):
            for p, direct, summed in ((0, 0, 5), (1, 3, 2)):
                r0, rn = rows_of(p, w)
                total = mine(k_me, p, w) + take(direct, p, w) + take(summed, p, w)
                if w:
                    f_out[c, r0:r0 + rn, :] = total
                else:
                    f_in[c, r0:r0 + rn, :] = total

        swap = [pltpu.make_async_remote_copy(
                    src_ref=f_in.at[c], dst_ref=f_in.at[c],
                    send_sem=send_sems.at[22], recv_sem=recv_sems.at[22], device_id=sibling, device_id_type=MESH),
                pltpu.make_async_remote_copy(
                    src_ref=f_out.at[c], dst_ref=f_out.at[c],
                    send_sem=send_sems.at[23], recv_sem=recv_sems.at[23], device_id=sibling, device_id_type=MESH)]
        for cp in swap:
            cp.start()

        for cp in small:
            cp.wait_recv()
        total = pack_all[0]
        for d in range(1, 8):
            total = total + pack_all[d]
        tot_ref[...] = total
        tot_out[0:P_GU, :] = total[0:P_GU]
        tot_out[P_GU:PACK_OWN_ROWS, :] = tot_ref[pl.ds(pl.multiple_of(P_GU + W_R * k_me, 8), W_R), :]

        other_in = pltpu.make_async_remote_copy(
            src_ref=f_in.at[1 - c], dst_ref=f_in.at[1 - c],
            send_sem=send_sems.at[22], recv_sem=recv_sems.at[22], device_id=sibling, device_id_type=MESH)
        other_out = pltpu.make_async_remote_copy(
            src_ref=f_out.at[1 - c], dst_ref=f_out.at[1 - c],
            send_sem=send_sems.at[23], recv_sem=recv_sems.at[23], device_id=sibling, device_id_type=MESH)
        other_in.wait_recv()
        other_out.wait_recv()
        for cp in small + to_sib + sent + swap:
            cp.wait_send()

        for h in range(2):
            lin_in[:, h * HALF:(h + 1) * HALF] = f_in[h, 0:SHARD_IN, :]
            fin_out[:, h * HALF:(h + 1) * HALF] = f_out[h]

    vmem = pl.BlockSpec(memory_space=pltpu.VMEM)
    hbm = pl.BlockSpec(memory_space=pl.ANY)
    return pl.pallas_call(
        body, name="reduce_grads",
        out_shape=(jax.ShapeDtypeStruct((SHARD_IN, D_MODEL), F32), jax.ShapeDtypeStruct((SHARD_OUT, D_MODEL), F32),
                   jax.ShapeDtypeStruct((PACK_OWN_ROWS, 128), F32)),
        in_specs=[hbm, hbm] + [vmem] * 7, out_specs=(vmem,) * 3,
        scratch_shapes=[
            pltpu.VMEM((ACC_ROWS, HALF), BF16), pltpu.VMEM((N_CHIPS, SHARD_OUT, HALF), BF16),
            pltpu.VMEM((ACC_ROWS, HALF), BF16), pltpu.VMEM((N_CHIPS, SHARD_OUT, HALF), BF16),
            pltpu.VMEM((ACC_ROWS, HALF), F32),
            pltpu.VMEM((N_CHIPS, SHARD_PAD, HALF), F32),
            pltpu.VMEM((6, GATHER_SPLIT, HALF), BF16), pltpu.VMEM((6, SHARD_OUT // 2, HALF), BF16),
            pltpu.VMEM((6, GATHER_SPLIT, HALF), BF16), pltpu.VMEM((6, SHARD_OUT // 2, HALF), BF16),
            pltpu.VMEM((2, SHARD_PAD, HALF), F32), pltpu.VMEM((2, SHARD_OUT, HALF), F32),
            pltpu.VMEM((PACK_ROWS, 128), F32), pltpu.VMEM((PACK_ROWS, 128), F32), pltpu.VMEM((8, PACK_ROWS, 128), F32),
            pltpu.SemaphoreType.DMA((24,)), pltpu.SemaphoreType.DMA((24,)), pltpu.SemaphoreType.DMA((2,)),
        ],
        compiler_params=pltpu.CompilerParams(vmem_limit_bytes=VMEM_LIMIT),
    )(g_in, g_out, *small_grads)


def _adamw_call(g_in, w_in, m_in, v_in, g_out, w_out, m_out, v_out, tot, small_params):
    steps = 8
    rows_out = SHARD_OUT // steps
    cols = D_MODEL // steps
    gu_w = W_KB // N_CHIPS

    def body(gi, wi, mi, vi, go, wo, mo, vo, tot, *rest):
        params = rest[:3 * N_SMALL]
        gi_o, di, nmi, nvi, go_o, do, nmo, nvo, loss_out = rest[3 * N_SMALL:3 * N_SMALL + 9]
        small_out = rest[3 * N_SMALL + 9:]

        @pl.when(pl.program_id(0) == 0)
        def _():
            loss_out[...] = tot[P_LOSS:P_LOSS + 1, :]
            g_outs = small_out[0:N_SMALL]
            for a in range(8):
                g_outs[0][:, a * 128:(a + 1) * 128] = tot[P_LNG + a:P_LNG + a + 1, :]
                g_outs[1][:, a * 128:(a + 1) * 128] = tot[P_LNB + a:P_LNB + a + 1, :]
            for a in range(2):
                g_outs[2][:, a * 128:(a + 1) * 128] = tot[P_BG + a:P_BG + a + 1, :]
            g_outs[3][...] = tot[P_NW:P_NW + 1, :]
            g_outs[4][...] = tot[P_SINK:P_SINK + 1, 0:Q_HEADS]
            g_outs[5][...] = tot[P_GU:PACK_OWN_ROWS, 0:gu_w]
            for n in range(N_SMALL):
                w_ref, m_ref, v_ref = params[3 * n:3 * n + 3]
                delta, new_m, new_v = _adamw(w_ref[...], g_outs[n][...], m_ref[...], v_ref[...])
                small_out[N_SMALL + n][...] = delta
                small_out[2 * N_SMALL + n][...] = new_m
                small_out[3 * N_SMALL + n][...] = new_v

        g = gi[...]
        delta, new_m, new_v = _adamw(wi[:, 0, :], g, mi[:, 0, :], vi[:, 0, :])
        gi_o[:, 0, :] = g
        di[:, 0, :] = delta
        nmi[:, 0, :] = new_m
        nvi[:, 0, :] = new_v
        g = go[...]
        go_o[...] = g
        do[...], nmo[...], nvo[...] = _adamw(wo[...], g, mo[...], vo[...])

    t_g = pl.BlockSpec((SHARD_IN, cols), lambda i: (0, i))
    t_in = pl.BlockSpec((SHARD_IN, 1, cols), lambda i: (0, 0, i))
    t_out = pl.BlockSpec((rows_out, D_MODEL), lambda i: (i, 0))
    s_in = jax.ShapeDtypeStruct((SHARD_IN, 1, D_MODEL), F32)
    s_out = jax.ShapeDtypeStruct((SHARD_OUT, D_MODEL), F32)
    whole = lambda shape: pl.BlockSpec(shape, lambda i: tuple(0 for _ in shape))
    small_specs = [whole(p.shape) for p in small_params]
    small_shapes = [jax.ShapeDtypeStruct(p.shape, F32) for p in small_params[0::3]] * 4
    return pl.pallas_call(
        body, name="adamw", grid=(steps,),
        in_specs=[t_g] + [t_in] * 3 + [t_out] * 4 + [whole(tot.shape)] + small_specs,
        out_specs=(t_in,) * 4 + (t_out,) * 4 + (whole((1, 128)),) + tuple(small_specs[0::3] * 4),
        out_shape=(s_in,) * 4 + (s_out,) * 4 + (jax.ShapeDtypeStruct((1, 128), F32),) + tuple(small_shapes),
        compiler_params=pltpu.CompilerParams(dimension_semantics=("arbitrary",)),
    )(g_in, w_in, m_in, v_in, g_out, w_out, m_out, v_out, tot, *small_params)


def _rope_tables(positions):
    half = 8
    inv_freq = 500000.0 ** (-jnp.arange(half, dtype=F32) / half)
    ang = inv_freq[:, None] * positions.astype(F32)[None, :]
    return jnp.concatenate([jnp.cos(ang), jnp.sin(ang)], axis=0)


def kernel(x, positions, w_in, gla_w_gate_up, gla_b_gate, attn_sinks, gla_norm_w, w_out, ln_g, ln_b, loss_target, m_w_in, m_gla_w_gate_up, m_gla_b_gate, m_attn_sinks, m_gla_norm_w, m_w_out, m_ln_g, m_ln_b, v_w_in, v_gla_w_gate_up, v_gla_b_gate, v_attn_sinks, v_gla_norm_w, v_w_out, v_ln_g, v_ln_b):
    def lin3(w):
        return jnp.transpose(w, (2, 0, 1))

    def unlin(w):
        return jnp.transpose(w, (1, 2, 0))

    win, wout, wgu_all = _gather_weights_call(lin3(w_in), w_out[0], gla_w_gate_up[0])
    wgu = jnp.transpose(wgu_all, (1, 0, 2)).reshape(W_R, W_KB)
    cs = _rope_tables(positions[0])
    sinks = attn_sinks[0]

    (qa, ka, va, qb, kb, vb, r, dattn, dga, dob, dgb, dh, st, g_wout, g_lng, g_lnb, g_nw, loss) = _fwd_call(
        x[0], loss_target[0], cs, win, wout, wgu, gla_b_gate, sinks, gla_norm_w, ln_g, ln_b)
    gx, g_win, g_sink, g_bg, g_wgu = _bwd_call(
        x[0], dh, qa, ka, va, dattn, dga, dob, dgb, qb, kb, vb, r, st, cs, win, wgu, gla_b_gate, sinks)

    g_wout_by_chip = g_wout.reshape(2, N_CHIPS, SHARD_OUT, HALF)
    small_params = []
    for group in ((ln_g, m_ln_g, v_ln_g), (ln_b, m_ln_b, v_ln_b), (gla_b_gate, m_gla_b_gate, v_gla_b_gate),
                  (gla_norm_w, m_gla_norm_w, v_gla_norm_w), (attn_sinks, m_attn_sinks, v_attn_sinks)):
        small_params += list(group)
    small_params += [gla_w_gate_up[0], m_gla_w_gate_up[0], v_gla_w_gate_up[0]]
    fin_in, fin_out, tot = _reduce_grads_call(g_win, g_wout_by_chip, (g_lng, g_lnb, g_bg, g_nw, g_sink, g_wgu, loss))
    fin_in, d_in, nm_in, nv_in, fin_out, d_out, nm_out, nv_out, loss_sum, *small_out = _adamw_call(
        fin_in, lin3(w_in), lin3(m_w_in), lin3(v_w_in), fin_out, w_out[0], m_w_out[0], v_w_out[0], tot, small_params)

    def unpack(kind, big_in, big_out):
        lng_, lnb_, bg_, nw_, sink_, gu_ = small_out[kind * N_SMALL:(kind + 1) * N_SMALL]
        return (unlin(big_in), gu_[None], bg_, sink_, nw_, big_out[None], lng_, lnb_)

    loss_total = loss_sum[0, 0]
    g_s, d_s, nm_s, nv_s = 0, 1, 2, 3
    return (loss_total, gx[None], *unpack(g_s, fin_in, fin_out), *unpack(d_s, d_in, d_out),
            *unpack(nm_s, nm_in, nm_out), *unpack(nv_s, nv_in, nv_out))
```

```python
import functools

import jax
import jax.numpy as jnp
import numpy as np
from jax import lax
from jax.experimental import pallas as pl
from jax.experimental.pallas import tpu as pltpu

F32 = jnp.float32
BF16 = jnp.bfloat16
MESH = pl.DeviceIdType.MESH

D_MODEL = 1024
N_CHIPS = 4
W_QA, W_KA, W_VA, W_GA, W_QB, W_KB, W_VB, W_GB, W_R = 512, 128, 128, 512, 256, 256, 512, 512, 16
O_QA = 0
O_KA = O_QA + W_QA
O_VA = O_KA + W_KA
O_GA = O_VA + W_VA
O_QB = O_GA + W_GA
O_KB = O_QB + W_QB
O_VB = O_KB + W_KB
O_GB = O_VB + W_VB
O_R = O_GB + W_GB
D_PROJ = O_R + W_R
SHARD_IN = D_PROJ // N_CHIPS
SHARD_OUT = D_MODEL // N_CHIPS
SHARD_PAD = 720
ACC_ROWS = -(-((N_CHIPS - 1) * SHARD_IN + SHARD_PAD) // 8) * 8
HALF = D_MODEL // 2
GATHER_SPLIT = 368

HEAD_A = 64
Q_HEADS = 8
KV_HEADS = 2
GROUP = 4
BLOCK = 128
GLA_HEADS = 4
GLA_DK = 64
GLA_DV = 128
CHUNK = 64
GLA_TAU = 16.0
EPS = 1e-5
ALPHA = 2.0 ** 0.25
ATT_SCALE = HEAD_A ** -0.5
GLA_SCALE = GLA_DK ** -0.5

ADAM_LR = 0.001
ADAM_B1 = 0.9
ADAM_B2 = 0.999
ADAM_EPS = 1e-08
ADAM_WD = 0.01
ADAM_STEP = 10

TM = 256
TRI_SLAB = 128
VMEM_LIMIT = 56 * 1024 * 1024

P_LNG, P_LNB, P_BG, P_NW, P_SINK, P_LOSS, P_GU = 0, 8, 16, 18, 19, 20, 24
PACK_ROWS = P_GU + N_CHIPS * 16
PACK_OWN_ROWS = P_GU + 16


def _mm(a, b):
    return jnp.dot(a, b, preferred_element_type=F32)


def _mm_nt(a, b):
    return lax.dot_general(a, b, (((1,), (1,)), ((), ())), preferred_element_type=F32)


def _mm_tn(a, b):
    return lax.dot_general(a, b, (((0,), (0,)), ((), ())), preferred_element_type=F32)


def _split3(a):
    hi = a.astype(BF16)
    r1 = a - hi.astype(F32)
    mid = r1.astype(BF16)
    lo = (r1 - mid.astype(F32)).astype(BF16)
    return hi, mid, lo


def _tri_mm(tri, a):
    slab = tri.shape[0]
    hi, mid, lo = _split3(a)
    return jnp.concatenate(
        [_mm(tri, hi[s:s + slab]) + _mm(tri, mid[s:s + slab]) + _mm(tri, lo[s:s + slab])
         for s in range(0, a.shape[0], slab)], axis=0)


def _chunk_tri(n, upper):
    r = lax.broadcasted_iota(jnp.int32, (n, n), 0)
    c = lax.broadcasted_iota(jnp.int32, (n, n), 1)
    same = (r >> 6) == (c >> 6)
    order = (c >= r) if upper else (c <= r)
    return jnp.where(same & order, 1.0, 0.0).astype(BF16)


def _rope(t, cos, sa, sb):
    w = t.shape[1]
    return t * cos + pltpu.roll(t, w - 8, 1) * sa + pltpu.roll(t, 8, 1) * sb


def _rope_tile(cs):
    row = lax.broadcasted_iota(jnp.int32, (16, 128), 0)
    d = lax.broadcasted_iota(jnp.int32, (16, 128), 1) & (HEAD_A - 1)
    hit = (d & 7) == (row & 7)
    is_cos = row < 8
    lo = d < 8
    hi = (d >= 8) & (d < 16)
    pick_cos = jnp.where(hit & is_cos & (lo | hi), 1.0, 0.0).astype(BF16)
    pick_sa = jnp.where(hit & ~is_cos & lo, -1.0, 0.0).astype(BF16)
    pick_sb = jnp.where(hit & ~is_cos & hi, 1.0, 0.0).astype(BF16)
    pieces = _split3(cs)

    def spread(pick):
        return _mm_tn(pieces[0], pick) + _mm_tn(pieces[1], pick) + _mm_tn(pieces[2], pick)

    d1 = lax.broadcasted_iota(jnp.int32, (1, 128), 1) & (HEAD_A - 1)
    return spread(pick_cos) + jnp.where(d1 < 16, 0.0, 1.0), spread(pick_sa), spread(pick_sb)


def _rope_bwd(d, cos, sa, sb):
    w = d.shape[1]
    return d * cos + pltpu.roll(d * sa, 8, 1) + pltpu.roll(d * sb, w - 8, 1)


def _log_sigmoid(z):
    return jnp.minimum(z, 0.0) - jnp.log1p(jnp.exp(-jnp.abs(z)))


def _sigmoid(z):
    return 1.0 / (1.0 + jnp.exp(-z))


def _attn_bias(has_prev):
    r = lax.broadcasted_iota(jnp.int32, (GROUP * BLOCK, 2 * BLOCK), 0) & (BLOCK - 1)
    k = lax.broadcasted_iota(jnp.int32, (GROUP * BLOCK, 2 * BLOCK), 1)
    first_key = jnp.where(has_prev, 0, BLOCK)
    return jnp.where((k > r) & (k <= r + BLOCK) & (k >= first_key), 0.0, -jnp.inf)


def _sink_col(sinks_ref, j):
    r = lax.broadcasted_iota(jnp.int32, (GROUP * BLOCK, 1), 0) >> 7
    col = jnp.full((GROUP * BLOCK, 1), sinks_ref[GROUP * j], F32)
    for g in range(1, GROUP):
        col = jnp.where(r == g, sinks_ref[GROUP * j + g], col)
    return col


def _stack_heads(t, j):
    return jnp.concatenate([t[:, (GROUP * j + g) * HEAD_A:(GROUP * j + g + 1) * HEAD_A] for g in range(GROUP)], axis=0)


def _unstack_heads(parts):
    return jnp.concatenate([parts[j][g * BLOCK:(g + 1) * BLOCK] for j in range(KV_HEADS) for g in range(GROUP)], axis=1)


def _scores(qs, kc, bias):
    return _mm_nt(qs, kc) * ATT_SCALE + bias


def _softmax_block(qs, kc, bias, sink):
    return _softmax(_scores(qs, kc, bias), sink)


def _softmax(s, sink):
    m = jnp.maximum(jnp.max(s, axis=1, keepdims=True), sink)
    p = jnp.exp(s - m)
    e_sink = jnp.exp(sink - m)
    inv = 1.0 / (jnp.sum(p, axis=1, keepdims=True) + e_sink)
    return p * inv, e_sink * inv


def _fwd_call(x, tgt, cs, win, wout, wgu, bg, sinks, nw, lng, lnb):
    s_len = x.shape[0]
    nt = s_len // TM
    nblk = TM // BLOCK
    nch = TM // CHUNK

    def body(x_ref, t_ref, cs_ref, win_ref, wout_ref, wgu_ref, bg_ref, sinks_ref, nw_ref,
             lng_ref, lnb_ref,
             qa_ref, ka_ref, va_ref, qb_ref, kb_ref, vb_ref, r_ref, dattn_ref, dga_ref, dob_ref, dgb_ref, dh_ref,
             st_ref, dwout_ref, glng_ref, glnb_ref, gnw_ref, loss_ref,
             kprev, vprev, state, attn_s, ga_s, ob_s, gb_s, cat_s):
        i = pl.program_id(0)

        @pl.when(i == 0)
        def _():
            kprev[...] = jnp.zeros_like(kprev)
            vprev[...] = jnp.zeros_like(vprev)
            state[...] = jnp.zeros_like(state)
            dwout_ref[...] = jnp.zeros_like(dwout_ref)
            glng_ref[...] = jnp.zeros_like(glng_ref)
            glnb_ref[...] = jnp.zeros_like(glnb_ref)
            gnw_ref[...] = jnp.zeros_like(gnw_ref)
            loss_ref[...] = jnp.zeros_like(loss_ref)

        x = x_ref[...]
        xb = x.astype(BF16)

        def proj(off, width):
            return _mm_nt(xb, win_ref[off:off + width, :])

        cos, sa, sb = _rope_tile(cs_ref[...])
        cos4, sa4, sb4 = (jnp.concatenate([t] * 4, axis=1) for t in (cos, sa, sb))
        qa = _rope(proj(O_QA, W_QA), cos4, sa4, sb4).astype(BF16)
        ka = _rope(proj(O_KA, W_KA), cos, sa, sb).astype(BF16)
        va = proj(O_VA, W_VA).astype(BF16)
        qa_ref[...] = qa
        ka_ref[...] = ka
        va_ref[...] = va

        later = {}

        def fill_ga():
            ga_s[...] = proj(O_GA, W_GA)

        def fill_gb():
            gb_s[...] = proj(O_GB, W_GB)

        def fill_qk():
            later["qb"] = proj(O_QB, W_QB)
            later["kb"] = proj(O_KB, W_KB)

        def fill_vr():
            later["vb"] = proj(O_VB, W_VB).astype(BF16)
            later["r"] = proj(O_R, W_R)

        fillers = [fill_ga, fill_gb, fill_qk, fill_vr]

        bias_inner = _attn_bias(True)
        sink_cols = [_sink_col(sinks_ref, j) for j in range(KV_HEADS)]
        for b in range(nblk):
            rows = slice(b * BLOCK, (b + 1) * BLOCK)
            mask = _attn_bias(i > 0) if b == 0 else bias_inner
            k_cur = ka[rows]
            v_cur = va[rows]
            k_old = kprev[...] if b == 0 else ka[(b - 1) * BLOCK:b * BLOCK]
            v_old = vprev[...] if b == 0 else va[(b - 1) * BLOCK:b * BLOCK]
            outs = []
            for j in range(KV_HEADS):
                hs = slice(j * HEAD_A, (j + 1) * HEAD_A)
                kc = jnp.concatenate([k_old[:, hs], k_cur[:, hs]], axis=0)
                vc = jnp.concatenate([v_old[:, hs], v_cur[:, hs]], axis=0)
                s = _scores(_stack_heads(qa[rows], j), kc, mask)
                if fillers:
                    fillers.pop(0)()
                probs, _ = _softmax(s, sink_cols[j])
                outs.append(_mm(probs.astype(BF16), vc))
            attn_s[rows, :] = _unstack_heads(outs)
        kprev[...] = ka[(nblk - 1) * BLOCK:]
        vprev[...] = va[(nblk - 1) * BLOCK:]
        while fillers:
            fillers.pop(0)()

        r = later["r"]
        r_ref[...] = r
        z = _mm(r.astype(BF16), wgu_ref[...].astype(BF16)) + bg_ref[...]
        log_a = _log_sigmoid(z) / GLA_TAU
        bcum = _tri_mm(_chunk_tri(TRI_SLAB, False), log_a)
        qb = later["qb"]
        kb = later["kb"]
        vb = later["vb"]
        qb_ref[...] = qb
        kb_ref[...] = kb
        vb_ref[...] = vb
        qd_all = (qb * GLA_SCALE * jnp.exp(bcum)).astype(BF16)
        ki_all = (kb * jnp.exp(-bcum)).astype(BF16)
        tril = lax.broadcasted_iota(jnp.int32, (CHUNK, CHUNK), 0) >= lax.broadcasted_iota(jnp.int32, (CHUNK, CHUNK), 1)
        st = state[...]
        for c in range(nch):
            rows = slice(c * CHUNK, (c + 1) * CHUNK)
            b_c = bcum[rows]
            b_last = b_c[CHUNK - 1:CHUNK]
            ke = (kb[rows] * jnp.exp(b_last - b_c)).astype(BF16)
            st_ref[c] = st
            st16 = st.astype(BF16)
            o_parts, u_parts = [], []
            for h in range(GLA_HEADS):
                ks = slice(h * GLA_DK, (h + 1) * GLA_DK)
                vs = slice(h * GLA_DV, (h + 1) * GLA_DV)
                qd = qd_all[rows, ks]
                v_h = vb[rows, vs]
                a = jnp.where(tril, _mm_nt(qd, ki_all[rows, ks]), 0.0)
                o_parts.append(_mm(a.astype(BF16), v_h) + _mm_nt(qd, st16[:, ks]))
                u_parts.append(_mm_tn(v_h, ke[:, ks]))
            ob_s[rows, :] = jnp.concatenate(o_parts, axis=1)
            st = st * jnp.exp(b_last) + jnp.concatenate(u_parts, axis=1)
        state[...] = st

        ga = ga_s[...]
        sg_a = _sigmoid(ga)
        silu_a = ga * sg_a
        attn = attn_s[...]
        cat_s[:, :W_GA] = (attn * silu_a).astype(BF16)
        gb = gb_s[...]
        sg_b = _sigmoid(gb)
        silu_b = gb * sg_b
        nw = nw_ref[...]
        on_parts = []
        for h in range(GLA_HEADS):
            vs = slice(h * GLA_DV, (h + 1) * GLA_DV)
            o_h = ob_s[:, vs]
            rs = lax.rsqrt(jnp.mean(o_h * o_h, axis=1, keepdims=True) + EPS)
            on_parts.append(o_h * rs * nw)
        on = jnp.concatenate(on_parts, axis=1)
        cat_s[:, W_GA:] = (on * silu_b).astype(BF16)
        cat = cat_s[...]
        hres = ALPHA * x + _mm(cat, wout_ref[...])
        mu = jnp.mean(hres, axis=1, keepdims=True)
        hc = hres - mu
        rstd = lax.rsqrt(jnp.mean(hc * hc, axis=1, keepdims=True) + EPS)
        xhat = hc * rstd
        g_ln = lng_ref[...]
        err = xhat * g_ln + lnb_ref[...] - t_ref[...]
        loss_ref[...] += jnp.sum(err * err) * (0.5 / D_MODEL)
        dy = err * (1.0 / D_MODEL)
        glng_ref[...] += jnp.sum(dy * xhat, axis=0, keepdims=True)
        glnb_ref[...] += jnp.sum(dy, axis=0, keepdims=True)
        dxh = dy * g_ln
        dh = rstd * (dxh - jnp.mean(dxh, axis=1, keepdims=True) - xhat * jnp.mean(dxh * xhat, axis=1, keepdims=True))
        dh_ref[...] = dh
        dh16 = dh.astype(BF16)
        for h in range(2):
            dwout_ref[h] += _mm_tn(cat, dh16[:, h * HALF:(h + 1) * HALF])
        dcat = _mm_nt(dh16, wout_ref[...])

        d_a = dcat[:, :W_GA]
        dattn_ref[...] = (d_a * silu_a).astype(BF16)
        dga_ref[...] = (d_a * attn * (sg_a * (1.0 + ga * (1.0 - sg_a)))).astype(BF16)
        d_b = dcat[:, W_GA:]
        dgb_ref[...] = (d_b * on * (sg_b * (1.0 + gb * (1.0 - sg_b)))).astype(BF16)
        d_on = d_b * silu_b
        gnw = jnp.zeros((1, GLA_DV), F32)
        do_parts = []
        for h in range(GLA_HEADS):
            vs = slice(h * GLA_DV, (h + 1) * GLA_DV)
            o_h = ob_s[:, vs]
            rs = lax.rsqrt(jnp.mean(o_h * o_h, axis=1, keepdims=True) + EPS)
            d_on_h = d_on[:, vs]
            gnw = gnw + jnp.sum(d_on_h * o_h * rs, axis=0, keepdims=True)
            gg = d_on_h * nw
            do_parts.append(rs * gg - o_h * (rs * rs * rs) * jnp.mean(gg * o_h, axis=1, keepdims=True))
        gnw_ref[...] += gnw
        dob_ref[...] = jnp.concatenate(do_parts, axis=1).astype(BF16)

    tile = lambda w: pl.BlockSpec((TM, w), lambda i: (i, 0))
    whole = lambda shape: pl.BlockSpec(shape, lambda i: tuple(0 for _ in shape), pipeline_mode=pl.Buffered(1))
    out_shape = (
        jax.ShapeDtypeStruct((s_len, W_QA), BF16),
        jax.ShapeDtypeStruct((s_len, W_KA), BF16),
        jax.ShapeDtypeStruct((s_len, W_VA), BF16),
        jax.ShapeDtypeStruct((s_len, W_QB), F32),
        jax.ShapeDtypeStruct((s_len, W_KB), F32),
        jax.ShapeDtypeStruct((s_len, W_VB), BF16),
        jax.ShapeDtypeStruct((s_len, W_R), F32),
        jax.ShapeDtypeStruct((s_len, W_GA), BF16),
        jax.ShapeDtypeStruct((s_len, W_GA), BF16),
        jax.ShapeDtypeStruct((s_len, W_GB), BF16),
        jax.ShapeDtypeStruct((s_len, W_GB), BF16),
        jax.ShapeDtypeStruct((s_len, D_MODEL), F32),
        jax.ShapeDtypeStruct((s_len // CHUNK, GLA_DV, GLA_HEADS * GLA_DK), F32),
        jax.ShapeDtypeStruct((2, D_MODEL, HALF), F32),
        jax.ShapeDtypeStruct((1, D_MODEL), F32),
        jax.ShapeDtypeStruct((1, D_MODEL), F32),
        jax.ShapeDtypeStruct((1, GLA_DV), F32),
        jax.ShapeDtypeStruct((1, 128), F32),
    )
    out_specs = (
        tile(W_QA), tile(W_KA), tile(W_VA), tile(W_QB), tile(W_KB), tile(W_VB), tile(W_R),
        tile(W_GA), tile(W_GA), tile(W_GB), tile(W_GB), tile(D_MODEL),
        pl.BlockSpec((nch, GLA_DV, GLA_HEADS * GLA_DK), lambda i: (i, 0, 0)),
        whole((2, D_MODEL, HALF)), whole((1, D_MODEL)), whole((1, D_MODEL)), whole((1, GLA_DV)), whole((1, 128)),
    )
    in_specs = [
        tile(D_MODEL), tile(D_MODEL), pl.BlockSpec((16, TM), lambda i: (0, i)),
        whole((D_PROJ, D_MODEL)), whole((D_MODEL, D_MODEL)), whole((W_R, W_KB)), whole((1, W_KB)),
        pl.BlockSpec(memory_space=pltpu.SMEM), whole((1, GLA_DV)), whole((1, D_MODEL)), whole((1, D_MODEL)),
    ]
    scratch = [
        pltpu.VMEM((BLOCK, W_KA), BF16), pltpu.VMEM((BLOCK, W_VA), BF16),
        pltpu.VMEM((GLA_DV, GLA_HEADS * GLA_DK), F32),
        pltpu.VMEM((TM, W_GA), F32), pltpu.VMEM((TM, W_GA), F32), pltpu.VMEM((TM, W_GB), F32),
        pltpu.VMEM((TM, W_GB), F32), pltpu.VMEM((TM, D_MODEL), BF16),
    ]
    return pl.pallas_call(
        body, name="fwd_head", grid=(nt,), in_specs=in_specs, out_specs=out_specs, out_shape=out_shape,
        scratch_shapes=scratch,
        compiler_params=pltpu.CompilerParams(dimension_semantics=("arbitrary",), vmem_limit_bytes=VMEM_LIMIT),
    )(x, tgt, cs, win, wout, wgu, bg, sinks, nw, lng, lnb)


def _bwd_call(x, dh, qa, ka, va, dattn, dga, dob, dgb, qb, kb, vb, r, st, cs, win, wgu, bg, sinks):
    s_len = x.shape[0]
    nt = s_len // TM
    nblk = TM // BLOCK
    nch = TM // CHUNK

    def body(x_ref, dh_ref, qa_ref, ka_ref, va_ref, kap_ref, vap_ref, dattn_ref, dga_ref, dob_ref, dgb_ref,
             qb_ref, kb_ref, vb_ref, r_ref, st_ref, cs_ref, win_ref, wgu_ref, bg_ref, sinks_ref,
             gx_ref, dwin_ref, gsink_ref, gbg_ref, gwgu_ref,
             dproj, dk_carry, dv_carry, ds_carry, db_s, dwin_acc):
        i = pl.program_id(0)
        t = nt - 1 - i

        @pl.when(i == 0)
        def _():
            dk_carry[...] = jnp.zeros_like(dk_carry)
            dv_carry[...] = jnp.zeros_like(dv_carry)
            ds_carry[...] = jnp.zeros_like(ds_carry)
            dwin_acc[...] = jnp.zeros_like(dwin_acc)
            gsink_ref[...] = jnp.zeros_like(gsink_ref)
            gbg_ref[...] = jnp.zeros_like(gbg_ref)
            gwgu_ref[...] = jnp.zeros_like(gwgu_ref)

        cos, sa, sb = _rope_tile(cs_ref[...])
        cos4, sa4, sb4 = (jnp.concatenate([v] * 4, axis=1) for v in (cos, sa, sb))

        qa = qa_ref[...]
        ka = ka_ref[...]
        va = va_ref[...]
        dattn = dattn_ref[...]
        gsink_rows = [jnp.zeros((1, 1), F32) for _ in range(Q_HEADS)]
        bias_inner = _attn_bias(True)
        sink_cols = [_sink_col(sinks_ref, j) for j in range(KV_HEADS)]
        for b in reversed(range(nblk)):
            rows = slice(b * BLOCK, (b + 1) * BLOCK)
            mask = _attn_bias(t > 0) if b == 0 else bias_inner
            k_cur = ka[rows]
            v_cur = va[rows]
            k_old = kap_ref[...] if b == 0 else ka[(b - 1) * BLOCK:b * BLOCK]
            v_old = vap_ref[...] if b == 0 else va[(b - 1) * BLOCK:b * BLOCK]
            dq_parts, dk_parts, dv_parts = [], [], []
            for j in range(KV_HEADS):
                hs = slice(j * HEAD_A, (j + 1) * HEAD_A)
                kc = jnp.concatenate([k_old[:, hs], k_cur[:, hs]], axis=0)
                vc = jnp.concatenate([v_old[:, hs], v_cur[:, hs]], axis=0)
                qs = _stack_heads(qa[rows], j)
                do_s = _stack_heads(dattn[rows], j)
                probs, p_sink = _softmax_block(qs, kc, mask, sink_cols[j])
                dp = _mm_nt(do_s, vc)
                d_row = jnp.sum(probs * dp, axis=1, keepdims=True)
                ds16 = (probs * (dp - d_row) * ATT_SCALE).astype(BF16)
                dq_parts.append(_mm(ds16, kc))
                dk_parts.append(_mm_tn(ds16, qs))
                dv_parts.append(_mm_tn(probs.astype(BF16), do_s))
                t_sink = d_row * p_sink
                for g in range(GROUP):
                    gsink_rows[GROUP * j + g] = gsink_rows[GROUP * j + g] - jnp.sum(
                        t_sink[g * BLOCK:(g + 1) * BLOCK], axis=0, keepdims=True)
            dq = _rope_bwd(_unstack_heads(dq_parts), cos4[rows], sa4[rows], sb4[rows])
            dproj[rows, O_QA:O_QA + W_QA] = dq.astype(BF16)
            dk_cur = dk_carry[...] + jnp.concatenate([p[BLOCK:] for p in dk_parts], axis=1)
            dv_cur = dv_carry[...] + jnp.concatenate([p[BLOCK:] for p in dv_parts], axis=1)
            dproj[rows, O_KA:O_KA + W_KA] = _rope_bwd(dk_cur, cos[rows], sa[rows], sb[rows]).astype(BF16)
            dproj[rows, O_VA:O_VA + W_VA] = dv_cur.astype(BF16)
            dk_carry[...] = jnp.concatenate([p[:BLOCK] for p in dk_parts], axis=1)
            dv_carry[...] = jnp.concatenate([p[:BLOCK] for p in dv_parts], axis=1)
        for hq in range(Q_HEADS):
            gsink_ref[hq:hq + 1, :] += jnp.broadcast_to(gsink_rows[hq], (1, 128))

        dproj[:, O_GA:O_GA + W_GA] = dga_ref[...]
        dproj[:, O_GB:O_GB + W_GB] = dgb_ref[...]

        r16 = r_ref[...].astype(BF16)
        wgu16 = wgu_ref[...].astype(BF16)
        z = _mm(r16, wgu16) + bg_ref[...]
        log_a = _log_sigmoid(z) / GLA_TAU
        bcum = _tri_mm(_chunk_tri(TRI_SLAB, False), log_a)
        qb = qb_ref[...]
        kb = kb_ref[...]
        vb = vb_ref[...]
        dob = dob_ref[...]
        e_b = jnp.exp(bcum)
        e_nb = jnp.exp(-bcum)
        qd_f = qb * GLA_SCALE * e_b
        ki_f = kb * e_nb
        qd_all = qd_f.astype(BF16)
        ki_all = ki_f.astype(BF16)
        tril = lax.broadcasted_iota(jnp.int32, (CHUNK, CHUNK), 0) >= lax.broadcasted_iota(jnp.int32, (CHUNK, CHUNK), 1)
        last_row = lax.broadcasted_iota(jnp.int32, (CHUNK, 1), 0) == CHUNK - 1

        x16 = x_ref[...].astype(BF16)
        dp_a = dproj[:, 0:O_QB]
        early = {}

        def fill_dx():
            early["gx"] = _mm(dp_a, win_ref[0:O_QB, :])

        def fill_dw(h):
            dwin_acc[h, 0:O_QB, :] += _mm_tn(dp_a, x16[:, h * HALF:(h + 1) * HALF])

        fillers = [fill_dx, functools.partial(fill_dw, 0), functools.partial(fill_dw, 1)]
        dsn = ds_carry[...]
        for c in reversed(range(nch)):
            if fillers:
                fillers.pop(0)()
            rows = slice(c * CHUNK, (c + 1) * CHUNK)
            b_c = bcum[rows]
            b_last = b_c[CHUNK - 1:CHUNK]
            e_e = jnp.exp(b_last - b_c)
            dec = jnp.exp(b_last)
            ke_f = kb[rows] * e_e
            ke = ke_f.astype(BF16)
            sp = st_ref[c]
            sp16 = sp.astype(BF16)
            dsn16 = dsn.astype(BF16)
            dqd_p, dki_p, dke_p, dv_p, dsp_p = [], [], [], [], []
            for h in range(GLA_HEADS):
                ks = slice(h * GLA_DK, (h + 1) * GLA_DK)
                vs = slice(h * GLA_DV, (h + 1) * GLA_DV)
                qd = qd_all[rows, ks]
                ki = ki_all[rows, ks]
                v_h = vb[rows, vs]
                do_h = dob[rows, vs]
                a16 = jnp.where(tril, _mm_nt(qd, ki), 0.0).astype(BF16)
                da16 = jnp.where(tril, _mm_nt(do_h, v_h), 0.0).astype(BF16)
                dv_p.append(_mm_tn(a16, do_h) + _mm_nt(ke[:, ks], dsn16[:, ks]))
                dqd_p.append(_mm(da16, ki) + _mm(do_h, sp16[:, ks]))
                dki_p.append(_mm_tn(da16, qd))
                dke_p.append(_mm(v_h, dsn16[:, ks]))
                dsp_p.append(_mm_tn(do_h, qd))
            dqd = jnp.concatenate(dqd_p, axis=1)
            dki = jnp.concatenate(dki_p, axis=1)
            dke = jnp.concatenate(dke_p, axis=1)
            ddec = jnp.sum(dsn * sp, axis=0, keepdims=True)
            dsn_next = dsn * dec + jnp.concatenate(dsp_p, axis=1)
            dproj[rows, O_QB:O_QB + W_QB] = (dqd * e_b[rows] * GLA_SCALE).astype(BF16)
            dproj[rows, O_KB:O_KB + W_KB] = (dki * e_nb[rows] + dke * e_e).astype(BF16)
            dproj[rows, O_VB:O_VB + W_VB] = jnp.concatenate(dv_p, axis=1).astype(BF16)
            dke_ke = dke * ke_f
            d_b = dqd * qd_f[rows] - dki * ki_f[rows] - dke_ke
            d_bl = jnp.sum(dke_ke, axis=0, keepdims=True) + ddec * dec
            db_s[rows, :] = d_b + jnp.where(last_row, d_bl, 0.0)
            dsn = dsn_next
        ds_carry[...] = dsn
        while fillers:
            fillers.pop(0)()
        dlog_a = _tri_mm(_chunk_tri(TRI_SLAB, True), db_s[...])
        dz = dlog_a * (1.0 / GLA_TAU) * _sigmoid(-z)
        dz16 = dz.astype(BF16)
        gbg_ref[...] += jnp.sum(dz, axis=0, keepdims=True)
        gwgu_ref[...] += _mm_tn(r16, dz16)
        dproj[:, O_R:O_R + W_R] = _mm_nt(dz16, wgu16).astype(BF16)

        dp_b = dproj[:, O_QB:D_PROJ]
        gx_ref[...] = ALPHA * dh_ref[...] + early["gx"] + _mm(dp_b, win_ref[O_QB:D_PROJ, :])
        for h in range(2):
            dwin_acc[h, O_QB:D_PROJ, :] += _mm_tn(dp_b, x16[:, h * HALF:(h + 1) * HALF])

        @pl.when(i == nt - 1)
        def _():
            dwin_ref[...] = dwin_acc[...].astype(BF16)

    tile = lambda w: pl.BlockSpec((TM, w), lambda i: (nt - 1 - i, 0))
    whole = lambda shape: pl.BlockSpec(shape, lambda i: tuple(0 for _ in shape), pipeline_mode=pl.Buffered(1))
    prev_blk = pl.BlockSpec((BLOCK, W_KA), lambda i: (jnp.maximum((nt - 1 - i) * nblk - 1, 0), 0))
    in_specs = [
        tile(D_MODEL), tile(D_MODEL), tile(W_QA), tile(W_KA), tile(W_VA), prev_blk, prev_blk,
        tile(W_GA), tile(W_GA), tile(W_GB), tile(W_GB), tile(W_QB), tile(W_KB), tile(W_VB), tile(W_R),
        pl.BlockSpec((nch, GLA_DV, GLA_HEADS * GLA_DK), lambda i: (nt - 1 - i, 0, 0)),
        pl.BlockSpec((16, TM), lambda i: (0, nt - 1 - i)),
        whole((D_PROJ, D_MODEL)), whole((W_R, W_KB)), whole((1, W_KB)), pl.BlockSpec(memory_space=pltpu.SMEM),
    ]
    out_shape = (
        jax.ShapeDtypeStruct((s_len, D_MODEL), F32),
        jax.ShapeDtypeStruct((2, ACC_ROWS, HALF), BF16),
        jax.ShapeDtypeStruct((Q_HEADS, 128), F32),
        jax.ShapeDtypeStruct((1, W_KB), F32),
        jax.ShapeDtypeStruct((W_R, W_KB), F32),
    )
    out_specs = (tile(D_MODEL), whole((2, ACC_ROWS, HALF)), whole((Q_HEADS, 128)), whole((1, W_KB)),
                 whole((W_R, W_KB)))
    scratch = [
        pltpu.VMEM((TM, D_PROJ), BF16), pltpu.VMEM((BLOCK, W_KA), F32), pltpu.VMEM((BLOCK, W_VA), F32),
        pltpu.VMEM((GLA_DV, GLA_HEADS * GLA_DK), F32), pltpu.VMEM((TM, W_KB), F32),
        pltpu.VMEM((2, ACC_ROWS, HALF), F32),
    ]
    return pl.pallas_call(
        body, name="bwd_mix", grid=(nt,), in_specs=in_specs, out_specs=out_specs, out_shape=out_shape,
        scratch_shapes=scratch,
        compiler_params=pltpu.CompilerParams(dimension_semantics=("arbitrary",), vmem_limit_bytes=VMEM_LIMIT),
    )(x, dh, qa, ka, va, ka, va, dattn, dga, dob, dgb, qb, kb, vb, r, st, cs, win, wgu, bg, sinks)


def _mesh_place():
    x, y, c = lax.axis_index("x"), lax.axis_index("y"), lax.axis_index("c")
    chips = [(1 - x, y), (x, 1 - y), (1 - x, 1 - y)]
    return x, y, c, chips


def _gather_weights_call(w_lin, w_out, wgu):
    def body(wlin_ref, wout_ref, wgu_ref, wt_ref, wout_full, wgu_all, blk, oblk, asm, send_sems, recv_sems):
        x, y, c, chips = _mesh_place()
        k_me = 2 * x + y
        asm[SHARD_IN - 4:SHARD_PAD, :] = jnp.zeros((SHARD_PAD - SHARD_IN + 4, D_MODEL), F32)
        asm[0:SHARD_IN, :] = wlin_ref[:, 0, :]
        for h in range(2):
            blk[k_me, h] = asm[0:SHARD_PAD, h * HALF:(h + 1) * HALF].astype(BF16)
            oblk[k_me, h] = wout_ref[:, h * HALF:(h + 1) * HALF].astype(BF16)
        wgu_all[k_me] = wgu_ref[...]

        parts = ((0, GATHER_SPLIT, 0, SHARD_OUT // 2), (GATHER_SPLIT, SHARD_PAD - GATHER_SPLIT, SHARD_OUT // 2, SHARD_OUT // 2))
        me_id, sib_id = (x, y, c), (x, y, 1 - c)
        nbr_x, nbr_y, diag = ((*chip, c) for chip in chips)
        k_x, k_y, k_d = (2 * chip[0] + chip[1] for chip in chips)

        def copies(k, hc, p, sem0, to):
            if p is None:
                refs = (blk.at[k, hc], oblk.at[k, hc])
            else:
                r0, rn, o0, on = parts[p]
                refs = (blk.at[k, hc, pl.ds(r0, rn), :], oblk.at[k, hc, pl.ds(o0, on), :])
            return [pltpu.make_async_remote_copy(src_ref=ref, dst_ref=ref, send_sem=send_sems.at[sem0 + n],
                                                 recv_sem=recv_sems.at[sem0 + n], device_id=to, device_id_type=MESH)
                    for n, ref in enumerate(refs)]

        def gu_copy(k, r, to):
            return pltpu.make_async_remote_copy(src_ref=wgu_all.at[k], dst_ref=wgu_all.at[k], send_sem=send_sems.at[18 + r],
                                                recv_sem=recv_sems.at[18 + r], device_id=to, device_id_type=MESH)

        def start(cps):
            for cp in cps:
                cp.start()
            return cps

        def landed(cps):
            for cp in cps:
                cp.wait_recv()

        started = start(copies(k_me, c, 0, 0, nbr_x) + copies(k_me, c, 1, 6, nbr_y)
                        + copies(k_me, c, 1, 2, nbr_x) + copies(k_me, c, 0, 4, nbr_y)
                        + [gu_copy(k_me, r, to) for r, to in enumerate((nbr_x, nbr_y, diag))])
        landed(copies(k_x, c, 0, 0, me_id))
        started += start(copies(k_x, c, 0, 8, nbr_y))
        landed(copies(k_y, c, 1, 6, me_id))
        started += start(copies(k_y, c, 1, 10, nbr_x))
        landed(copies(k_x, c, 1, 2, me_id))
        started += start(copies(k_x, c, None, 12, sib_id))
        landed(copies(k_y, c, 0, 4, me_id))
        started += start(copies(k_y, c, None, 14, sib_id))
        landed(copies(k_d, c, 0, 8, me_id) + copies(k_d, c, 1, 10, me_id))
        started += start(copies(k_d, c, None, 16, sib_id))
        for r, k_r in enumerate((k_x, k_y, k_d)):
            landed(copies(k_r, 1 - c, None, 12 + 2 * r, me_id))
            gu_copy(k_r, r, me_id).wait_recv()
        for cp in started:
            cp.wait_send()

        for k in range(N_CHIPS):
            for h in range(2):
                asm[k * SHARD_IN:k * SHARD_IN + SHARD_PAD, h * HALF:(h + 1) * HALF] = blk[k, h].astype(F32)
                wout_full[k * SHARD_OUT:(k + 1) * SHARD_OUT, h * HALF:(h + 1) * HALF] = oblk[k, h]
        wt_ref[...] = asm[0:D_PROJ, :].astype(BF16)

    vmem = pl.BlockSpec(memory_space=pltpu.VMEM)
    return pl.pallas_call(
        body, name="gather_weights",
        out_shape=(jax.ShapeDtypeStruct((D_PROJ, D_MODEL), BF16),
                   jax.ShapeDtypeStruct((D_MODEL, D_MODEL), BF16),
                   jax.ShapeDtypeStruct((N_CHIPS, W_R, W_KB // N_CHIPS), F32)),
        in_specs=[vmem, vmem, vmem], out_specs=(vmem, vmem, vmem),
        scratch_shapes=[pltpu.VMEM((N_CHIPS, 2, SHARD_PAD, HALF), BF16), pltpu.VMEM((N_CHIPS, 2, SHARD_OUT, HALF), BF16),
                        pltpu.VMEM((ACC_ROWS, D_MODEL), F32),
                        pltpu.SemaphoreType.DMA((21,)), pltpu.SemaphoreType.DMA((21,))],
        compiler_params=pltpu.CompilerParams(vmem_limit_bytes=VMEM_LIMIT),
    )(w_lin, w_out, wgu)


def _adamw(w, g, m, v):
    m = ADAM_B1 * m + (1.0 - ADAM_B1) * g
    v = ADAM_B2 * v + (1.0 - ADAM_B2) * (g * g)
    m_hat = m / (1.0 - ADAM_B1 ** ADAM_STEP)
    v_hat = v / (1.0 - ADAM_B2 ** ADAM_STEP)
    delta = -ADAM_LR * (m_hat / (jnp.sqrt(v_hat) + ADAM_EPS) + ADAM_WD * w)
    return delta, m, v


N_SMALL = 6


def _reduce_grads_call(g_in, g_out, small_grads):
    def body(gin_hbm, gout_hbm, g_lng, g_lnb, g_bg, g_nw, g_sink, g_wgu, loss_in, lin_in, fin_out, tot_out,
             a_in, a_out, b_in, b_out, ab_in, c_in, s_in, s_out, r_in, r_out, f_in, f_out, pack_ref, tot_ref, pack_all,
             send_sems, recv_sems, local_sems):
        x, y, c, chips = _mesh_place()
        k_me = 2 * x + y
        me = 4 * x + 2 * y + c
        sibling = (x, y, 1 - c)

        pack_ref[...] = jnp.zeros_like(pack_ref)
        for a in range(8):
            pack_ref[P_LNG + a:P_LNG + a + 1, :] = g_lng[:, a * 128:(a + 1) * 128]
            pack_ref[P_LNB + a:P_LNB + a + 1, :] = g_lnb[:, a * 128:(a + 1) * 128]
        for a in range(2):
            pack_ref[P_BG + a:P_BG + a + 1, :] = g_bg[:, a * 128:(a + 1) * 128]
        pack_ref[P_NW:P_NW + 1, :] = g_nw[...]
        lane = lax.broadcasted_iota(jnp.int32, (1, 128), 1)
        sink_row = jnp.zeros((1, 128), F32)
        for hq in range(Q_HEADS):
            sink_row = jnp.where(lane == hq, g_sink[hq:hq + 1, :], sink_row)
        pack_ref[P_SINK:P_SINK + 1, :] = sink_row
        pack_ref[P_LOSS:P_LOSS + 1, :] = loss_in[...]
        gu_w = W_KB // N_CHIPS
        for k in range(N_CHIPS):
            pack_ref[P_GU + W_R * k:P_GU + W_R * (k + 1), 0:gu_w] = g_wgu[:, k * gu_w:(k + 1) * gu_w]
        pack_all[me] = pack_ref[...]
        small = []
        for mask in range(1, 8):
            peer = (x ^ (mask >> 2), y ^ ((mask >> 1) & 1), c ^ (mask & 1))
            small.append(pltpu.make_async_remote_copy(
                src_ref=pack_ref, dst_ref=pack_all.at[me], send_sem=send_sems.at[mask], recv_sem=recv_sems.at[mask],
                device_id=peer, device_id_type=MESH))
        for cp in small:
            cp.start()

        loads = [pltpu.make_async_copy(gin_hbm.at[c], a_in, local_sems.at[0]),
                 pltpu.make_async_copy(gout_hbm.at[c], a_out, local_sems.at[1])]
        to_sib = [pltpu.make_async_remote_copy(
                      src_ref=gin_hbm.at[1 - c], dst_ref=b_in,
                      send_sem=send_sems.at[8], recv_sem=recv_sems.at[8], device_id=sibling, device_id_type=MESH),
                  pltpu.make_async_remote_copy(
                      src_ref=gout_hbm.at[1 - c], dst_ref=b_out,
                      send_sem=send_sems.at[9], recv_sem=recv_sems.at[9], device_id=sibling, device_id_type=MESH)]
        for cp in (loads[1], to_sib[1], loads[0], to_sib[0]):
            cp.start()

        parts = ((0, GATHER_SPLIT, 0, SHARD_OUT // 2), (GATHER_SPLIT, SHARD_PAD - GATHER_SPLIT, SHARD_OUT // 2, SHARD_OUT // 2))
        nbr_x, nbr_y, _ = ((*chip, c) for chip in chips)
        k_x, k_y, k_d = (2 * chip[0] + chip[1] for chip in chips)

        def rows_of(p, w):
            r0, rn, o0, on = parts[p]
            return (o0, on) if w else (r0, rn)

        def mine(k, p, w):
            r0, rn = rows_of(p, w)
            if w:
                return a_out[k, pl.ds(r0, rn), :] + b_out[k, pl.ds(r0, rn), :]
            return c_in[k, pl.ds(r0, rn), :]

        def message(m, p, w, to):
            rn = rows_of(p, w)[1]
            stage, land = (s_out, r_out) if w else (s_in, r_in)
            return pltpu.make_async_remote_copy(
                src_ref=stage.at[m, pl.ds(0, rn), :], dst_ref=land.at[m, pl.ds(0, rn), :],
                send_sem=send_sems.at[10 + 2 * m + w], recv_sem=recv_sems.at[10 + 2 * m + w],
                device_id=to, device_id_type=MESH)

        def post(m, p, w, val, to):
            rn = rows_of(p, w)[1]
            stage = s_out if w else s_in
            stage[m, 0:rn, :] = val.astype(BF16)
            cp = message(m, p, w, to)
            cp.start()
            return [cp]

        def take(m, p, w):
            rn = rows_of(p, w)[1]
            message(m, p, w, (x, y, c)).wait_recv()
            land = r_out if w else r_in
            return land[m, 0:rn, :].astype(F32)

        sent = []
        for w in (1, 0):
            loads[w].wait()
            to_sib[w].wait_recv()
            if w == 0:
                ab_in[...] = a_in[...].astype(F32) + b_in[...].astype(F32)
                for k in range(N_CHIPS):
                    c_in[k] = ab_in[k * SHARD_IN:k * SHARD_IN + SHARD_PAD, :]
            sent += post(1, 0, w, mine(k_d, 0, w), nbr_x) + post(4, 1, w, mine(k_d, 1, w), nbr_y)
            sent += post(0, 0, w, mine(k_x, 0, w), nbr_x) + post(3, 1, w, mine(k_y, 1, w), nbr_y)
        for w in (1, 0):
            sent += post(5, 0, w, take(1, 0, w) + mine(k_y, 0, w), nbr_y)
            sent += post(2, 1, w, take(4, 1, w) + mine(k_x, 1, w), nbr_x)
        for w in (1, 0):
            for p, direct, summed in ((0, 0, 5), (1, 3, 2)):
                r0, rn = rows_of(p, w)
                total = mine(k_me, p, w) + take(direct, p, w) + take(summed, p, w)
                if w:
                    f_out[c, r0:r0 + rn, :] = total
                else:
                    f_in[c, r0:r0 + rn, :] = total

        swap = [pltpu.make_async_remote_copy(
                    src_ref=f_in.at[c], dst_ref=f_in.at[c],
                    send_sem=send_sems.at[22], recv_sem=recv_sems.at[22], device_id=sibling, device_id_type=MESH),
                pltpu.make_async_remote_copy(
                    src_ref=f_out.at[c], dst_ref=f_out.at[c],
                    send_sem=send_sems.at[23], recv_sem=recv_sems.at[23], device_id=sibling, device_id_type=MESH)]
        for cp in swap:
            cp.start()

        for cp in small:
            cp.wait_recv()
        total = pack_all[0]
        for d in range(1, 8):
            total = total + pack_all[d]
        tot_ref[...] = total
        tot_out[0:P_GU, :] = total[0:P_GU]
        tot_out[P_GU:PACK_OWN_ROWS, :] = tot_ref[pl.ds(pl.multiple_of(P_GU + W_R * k_me, 8), W_R), :]

        other_in = pltpu.make_async_remote_copy(
            src_ref=f_in.at[1 - c], dst_ref=f_in.at[1 - c],
            send_sem=send_sems.at[22], recv_sem=recv_sems.at[22], device_id=sibling, device_id_type=MESH)
        other_out = pltpu.make_async_remote_copy(
            src_ref=f_out.at[1 - c], dst_ref=f_out.at[1 - c],
            send_sem=send_sems.at[23], recv_sem=recv_sems.at[23], device_id=sibling, device_id_type=MESH)
        other_in.wait_recv()
        other_out.wait_recv()
        for cp in small + to_sib + sent + swap:
            cp.wait_send()

        for h in range(2):
            lin_in[:, h * HALF:(h + 1) * HALF] = f_in[h, 0:SHARD_IN, :]
            fin_out[:, h * HALF:(h + 1) * HALF] = f_out[h]

    vmem = pl.BlockSpec(memory_space=pltpu.VMEM)
    hbm = pl.BlockSpec(memory_space=pl.ANY)
    return pl.pallas_call(
        body, name="reduce_grads",
        out_shape=(jax.ShapeDtypeStruct((SHARD_IN, D_MODEL), F32), jax.ShapeDtypeStruct((SHARD_OUT, D_MODEL), F32),
                   jax.ShapeDtypeStruct((PACK_OWN_ROWS, 128), F32)),
        in_specs=[hbm, hbm] + [vmem] * 7, out_specs=(vmem,) * 3,
        scratch_shapes=[
            pltpu.VMEM((ACC_ROWS, HALF), BF16), pltpu.VMEM((N_CHIPS, SHARD_OUT, HALF), F32),
            pltpu.VMEM((ACC_ROWS, HALF), BF16), pltpu.VMEM((N_CHIPS, SHARD_OUT, HALF), F32),
            pltpu.VMEM((ACC_ROWS, HALF), F32),
            pltpu.VMEM((N_CHIPS, SHARD_PAD, HALF), F32),
            pltpu.VMEM((6, GATHER_SPLIT, HALF), BF16), pltpu.VMEM((6, SHARD_OUT // 2, HALF), BF16),
            pltpu.VMEM((6, GATHER_SPLIT, HALF), BF16), pltpu.VMEM((6, SHARD_OUT // 2, HALF), BF16),
            pltpu.VMEM((2, SHARD_PAD, HALF), F32), pltpu.VMEM((2, SHARD_OUT, HALF), F32),
            pltpu.VMEM((PACK_ROWS, 128), F32), pltpu.VMEM((PACK_ROWS, 128), F32), pltpu.VMEM((8, PACK_ROWS, 128), F32),
            pltpu.SemaphoreType.DMA((24,)), pltpu.SemaphoreType.DMA((24,)), pltpu.SemaphoreType.DMA((2,)),
        ],
        compiler_params=pltpu.CompilerParams(vmem_limit_bytes=VMEM_LIMIT),
    )(g_in, g_out, *small_grads)


def _adamw_call(g_in, w_in, m_in, v_in, g_out, w_out, m_out, v_out, tot, small_params):
    steps = 4
    rows_out = SHARD_OUT // steps
    cols = D_MODEL // steps
    gu_w = W_KB // N_CHIPS

    def body(gi, wi, mi, vi, go, wo, mo, vo, tot, *rest):
        params = rest[:3 * N_SMALL]
        gi_o, di, nmi, nvi, go_o, do, nmo, nvo, loss_out = rest[3 * N_SMALL:3 * N_SMALL + 9]
        small_out = rest[3 * N_SMALL + 9:]

        @pl.when(pl.program_id(0) == 0)
        def _():
            loss_out[...] = tot[P_LOSS:P_LOSS + 1, :]
            g_outs = small_out[0:N_SMALL]
            for a in range(8):
                g_outs[0][:, a * 128:(a + 1) * 128] = tot[P_LNG + a:P_LNG + a + 1, :]
                g_outs[1][:, a * 128:(a + 1) * 128] = tot[P_LNB + a:P_LNB + a + 1, :]
            for a in range(2):
                g_outs[2][:, a * 128:(a + 1) * 128] = tot[P_BG + a:P_BG + a + 1, :]
            g_outs[3][...] = tot[P_NW:P_NW + 1, :]
            g_outs[4][...] = tot[P_SINK:P_SINK + 1, 0:Q_HEADS]
            g_outs[5][...] = tot[P_GU:PACK_OWN_ROWS, 0:gu_w]
            for n in range(N_SMALL):
                w_ref, m_ref, v_ref = params[3 * n:3 * n + 3]
                delta, new_m, new_v = _adamw(w_ref[...], g_outs[n][...], m_ref[...], v_ref[...])
                small_out[N_SMALL + n][...] = delta
                small_out[2 * N_SMALL + n][...] = new_m
                small_out[3 * N_SMALL + n][...] = new_v

        g = gi[...]
        delta, new_m, new_v = _adamw(wi[:, 0, :], g, mi[:, 0, :], vi[:, 0, :])
        gi_o[:, 0, :] = g
        di[:, 0, :] = delta
        nmi[:, 0, :] = new_m
        nvi[:, 0, :] = new_v
        g = go[...]
        go_o[...] = g
        do[...], nmo[...], nvo[...] = _adamw(wo[...], g, mo[...], vo[...])

    t_g = pl.BlockSpec((SHARD_IN, cols), lambda i: (0, i))
    t_in = pl.BlockSpec((SHARD_IN, 1, cols), lambda i: (0, 0, i))
    t_out = pl.BlockSpec((rows_out, D_MODEL), lambda i: (i, 0))
    s_in = jax.ShapeDtypeStruct((SHARD_IN, 1, D_MODEL), F32)
    s_out = jax.ShapeDtypeStruct((SHARD_OUT, D_MODEL), F32)
    whole = lambda shape: pl.BlockSpec(shape, lambda i: tuple(0 for _ in shape))
    small_specs = [whole(p.shape) for p in small_params]
    small_shapes = [jax.ShapeDtypeStruct(p.shape, F32) for p in small_params[0::3]] * 4
    return pl.pallas_call(
        body, name="adamw", grid=(steps,),
        in_specs=[t_g] + [t_in] * 3 + [t_out] * 4 + [whole(tot.shape)] + small_specs,
        out_specs=(t_in,) * 4 + (t_out,) * 4 + (whole((1, 128)),) + tuple(small_specs[0::3] * 4),
        out_shape=(s_in,) * 4 + (s_out,) * 4 + (jax.ShapeDtypeStruct((1, 128), F32),) + tuple(small_shapes),
        compiler_params=pltpu.CompilerParams(dimension_semantics=("arbitrary",)),
    )(g_in, w_in, m_in, v_in, g_out, w_out, m_out, v_out, tot, *small_params)


def _rope_tables(positions):
    half = 8
    inv_freq = 500000.0 ** (-jnp.arange(half, dtype=F32) / half)
    ang = inv_freq[:, None] * positions.astype(F32)[None, :]
    return jnp.concatenate([jnp.cos(ang), jnp.sin(ang)], axis=0)


def kernel(x, positions, w_in, gla_w_gate_up, gla_b_gate, attn_sinks, gla_norm_w, w_out, ln_g, ln_b, loss_target, m_w_in, m_gla_w_gate_up, m_gla_b_gate, m_attn_sinks, m_gla_norm_w, m_w_out, m_ln_g, m_ln_b, v_w_in, v_gla_w_gate_up, v_gla_b_gate, v_attn_sinks, v_gla_norm_w, v_w_out, v_ln_g, v_ln_b):
    def lin3(w):
        return jnp.transpose(w, (2, 0, 1))

    def unlin(w):
        return jnp.transpose(w, (1, 2, 0))

    win, wout, wgu_all = _gather_weights_call(lin3(w_in), w_out[0], gla_w_gate_up[0])
    wgu = jnp.transpose(wgu_all, (1, 0, 2)).reshape(W_R, W_KB)
    cs = _rope_tables(positions[0])
    sinks = attn_sinks[0]

    (qa, ka, va, qb, kb, vb, r, dattn, dga, dob, dgb, dh, st, g_wout, g_lng, g_lnb, g_nw, loss) = _fwd_call(
        x[0], loss_target[0], cs, win, wout, wgu, gla_b_gate, sinks, gla_norm_w, ln_g, ln_b)
    gx, g_win, g_sink, g_bg, g_wgu = _bwd_call(
        x[0], dh, qa, ka, va, dattn, dga, dob, dgb, qb, kb, vb, r, st, cs, win, wgu, gla_b_gate, sinks)

    g_wout_by_chip = g_wout.reshape(2, N_CHIPS, SHARD_OUT, HALF)
    small_params = []
    for group in ((ln_g, m_ln_g, v_ln_g), (ln_b, m_ln_b, v_ln_b), (gla_b_gate, m_gla_b_gate, v_gla_b_gate),
                  (gla_norm_w, m_gla_norm_w, v_gla_norm_w), (attn_sinks, m_attn_sinks, v_attn_sinks)):
        small_params += list(group)
    small_params += [gla_w_gate_up[0], m_gla_w_gate_up[0], v_gla_w_gate_up[0]]
    fin_in, fin_out, tot = _reduce_grads_call(g_win, g_wout_by_chip, (g_lng, g_lnb, g_bg, g_nw, g_sink, g_wgu, loss))
    fin_in, d_in, nm_in, nv_in, fin_out, d_out, nm_out, nv_out, loss_sum, *small_out = _adamw_call(
        fin_in, lin3(w_in), lin3(m_w_in), lin3(v_w_in), fin_out, w_out[0], m_w_out[0], v_w_out[0], tot, small_params)

    def unpack(kind, big_in, big_out):
        lng_, lnb_, bg_, nw_, sink_, gu_ = small_out[kind * N_SMALL:(kind + 1) * N_SMALL]
        return (unlin(big_in), gu_[None], bg_, sink_, nw_, big_out[None], lng_, lnb_)

    loss_total = loss_sum[0, 0]
    g_s, d_s, nm_s, nv_s = 0, 1, 2, 3
    return (loss_total, gx[None], *unpack(g_s, fin_in, fin_out), *unpack(d_s, d_in, d_out),
            *unpack(nm_s, nm_in, nm_out), *unpack(nv_s, nv_in, nv_out))
```

```python
import functools

import jax
import jax.numpy as jnp
import numpy as np
from jax import lax
from jax.experimental import pallas as pl
from jax.experimental.pallas import tpu as pltpu

F32 = jnp.float32
BF16 = jnp.bfloat16
MESH = pl.DeviceIdType.MESH

D_MODEL = 1024
N_CHIPS = 4
W_QA, W_KA, W_VA, W_GA, W_QB, W_KB, W_VB, W_GB, W_R = 512, 128, 128, 512, 256, 256, 512, 512, 16
O_QA = 0
O_KA = O_QA + W_QA
O_VA = O_KA + W_KA
O_GA = O_VA + W_VA
O_QB = O_GA + W_GA
O_KB = O_QB + W_QB
O_VB = O_KB + W_KB
O_GB = O_VB + W_VB
O_R = O_GB + W_GB
D_PROJ = O_R + W_R
SHARD_IN = D_PROJ // N_CHIPS
SHARD_OUT = D_MODEL // N_CHIPS
SHARD_PAD = 720
ACC_ROWS = -(-((N_CHIPS - 1) * SHARD_IN + SHARD_PAD) // 8) * 8
HALF = D_MODEL // 2
GATHER_SPLIT = 368

HEAD_A = 64
Q_HEADS = 8
KV_HEADS = 2
GROUP = 4
BLOCK = 128
GLA_HEADS = 4
GLA_DK = 64
GLA_DV = 128
CHUNK = 64
GLA_TAU = 16.0
EPS = 1e-5
ALPHA = 2.0 ** 0.25
ATT_SCALE = HEAD_A ** -0.5
GLA_SCALE = GLA_DK ** -0.5

ADAM_LR = 0.001
ADAM_B1 = 0.9
ADAM_B2 = 0.999
ADAM_EPS = 1e-08
ADAM_WD = 0.01
ADAM_STEP = 10

TM = 256
TRI_SLAB = 128
VMEM_LIMIT = 56 * 1024 * 1024

P_LNG, P_LNB, P_BG, P_NW, P_SINK, P_LOSS, P_GU = 0, 8, 16, 18, 19, 20, 24
PACK_ROWS = P_GU + N_CHIPS * 16
PACK_OWN_ROWS = P_GU + 16


def _mm(a, b):
    return jnp.dot(a, b, preferred_element_type=F32)


def _mm_nt(a, b):
    return lax.dot_general(a, b, (((1,), (1,)), ((), ())), preferred_element_type=F32)


def _mm_tn(a, b):
    return lax.dot_general(a, b, (((0,), (0,)), ((), ())), preferred_element_type=F32)


def _split3(a):
    hi = a.astype(BF16)
    r1 = a - hi.astype(F32)
    mid = r1.astype(BF16)
    lo = (r1 - mid.astype(F32)).astype(BF16)
    return hi, mid, lo


def _tri_mm(tri, a):
    slab = tri.shape[0]
    hi, mid, lo = _split3(a)
    return jnp.concatenate(
        [_mm(tri, hi[s:s + slab]) + _mm(tri, mid[s:s + slab]) + _mm(tri, lo[s:s + slab])
         for s in range(0, a.shape[0], slab)], axis=0)


def _chunk_tri(n, upper):
    r = lax.broadcasted_iota(jnp.int32, (n, n), 0)
    c = lax.broadcasted_iota(jnp.int32, (n, n), 1)
    same = (r >> 6) == (c >> 6)
    order = (c >= r) if upper else (c <= r)
    return jnp.where(same & order, 1.0, 0.0).astype(BF16)


def _rope(t, cos, sa, sb):
    w = t.shape[1]
    return t * cos + pltpu.roll(t, w - 8, 1) * sa + pltpu.roll(t, 8, 1) * sb


def _rope_tile(cs):
    row = lax.broadcasted_iota(jnp.int32, (16, 128), 0)
    d = lax.broadcasted_iota(jnp.int32, (16, 128), 1) & (HEAD_A - 1)
    hit = (d & 7) == (row & 7)
    is_cos = row < 8
    lo = d < 8
    hi = (d >= 8) & (d < 16)
    pick_cos = jnp.where(hit & is_cos & (lo | hi), 1.0, 0.0).astype(BF16)
    pick_sa = jnp.where(hit & ~is_cos & lo, -1.0, 0.0).astype(BF16)
    pick_sb = jnp.where(hit & ~is_cos & hi, 1.0, 0.0).astype(BF16)
    pieces = _split3(cs)

    def spread(pick):
        return _mm_tn(pieces[0], pick) + _mm_tn(pieces[1], pick) + _mm_tn(pieces[2], pick)

    d1 = lax.broadcasted_iota(jnp.int32, (1, 128), 1) & (HEAD_A - 1)
    return spread(pick_cos) + jnp.where(d1 < 16, 0.0, 1.0), spread(pick_sa), spread(pick_sb)


def _rope_bwd(d, cos, sa, sb):
    w = d.shape[1]
    return d * cos + pltpu.roll(d * sa, 8, 1) + pltpu.roll(d * sb, w - 8, 1)


def _log_sigmoid(z):
    return jnp.minimum(z, 0.0) - jnp.log1p(jnp.exp(-jnp.abs(z)))


def _sigmoid(z):
    return 1.0 / (1.0 + jnp.exp(-z))


def _attn_bias(has_prev):
    r = lax.broadcasted_iota(jnp.int32, (GROUP * BLOCK, 2 * BLOCK), 0) & (BLOCK - 1)
    k = lax.broadcasted_iota(jnp.int32, (GROUP * BLOCK, 2 * BLOCK), 1)
    first_key = jnp.where(has_prev, 0, BLOCK)
    return jnp.where((k > r) & (k <= r + BLOCK) & (k >= first_key), 0.0, -jnp.inf)


def _sink_col(sinks_ref, j):
    r = lax.broadcasted_iota(jnp.int32, (GROUP * BLOCK, 1), 0) >> 7
    col = jnp.full((GROUP * BLOCK, 1), sinks_ref[GROUP * j], F32)
    for g in range(1, GROUP):
        col = jnp.where(r == g, sinks_ref[GROUP * j + g], col)
    return col


def _stack_heads(t, j):
    return jnp.concatenate([t[:, (GROUP * j + g) * HEAD_A:(GROUP * j + g + 1) * HEAD_A] for g in range(GROUP)], axis=0)


def _unstack_heads(parts):
    return jnp.concatenate([parts[j][g * BLOCK:(g + 1) * BLOCK] for j in range(KV_HEADS) for g in range(GROUP)], axis=1)


def _scores(qs, kc, bias):
    return _mm_nt(qs, kc) * ATT_SCALE + bias


def _softmax_block(qs, kc, bias, sink):
    return _softmax(_scores(qs, kc, bias), sink)


def _softmax(s, sink):
    m = jnp.maximum(jnp.max(s, axis=1, keepdims=True), sink)
    p = jnp.exp(s - m)
    e_sink = jnp.exp(sink - m)
    inv = 1.0 / (jnp.sum(p, axis=1, keepdims=True) + e_sink)
    return p * inv, e_sink * inv


def _fwd_call(x, tgt, cs, win, wout, wgu, bg, sinks, nw, lng, lnb):
    s_len = x.shape[0]
    nt = s_len // TM
    nblk = TM // BLOCK
    nch = TM // CHUNK

    def body(x_ref, t_ref, cs_ref, win_ref, wout_ref, wgu_ref, bg_ref, sinks_ref, nw_ref,
             lng_ref, lnb_ref,
             qa_ref, ka_ref, va_ref, qb_ref, kb_ref, vb_ref, r_ref, dattn_ref, dga_ref, dob_ref, dgb_ref, dh_ref,
             st_ref, dwout_ref, glng_ref, glnb_ref, gnw_ref, loss_ref,
             kprev, vprev, state, attn_s, ga_s, ob_s, gb_s, cat_s):
        i = pl.program_id(0)

        @pl.when(i == 0)
        def _():
            kprev[...] = jnp.zeros_like(kprev)
            vprev[...] = jnp.zeros_like(vprev)
            state[...] = jnp.zeros_like(state)
            dwout_ref[...] = jnp.zeros_like(dwout_ref)
            glng_ref[...] = jnp.zeros_like(glng_ref)
            glnb_ref[...] = jnp.zeros_like(glnb_ref)
            gnw_ref[...] = jnp.zeros_like(gnw_ref)
            loss_ref[...] = jnp.zeros_like(loss_ref)

        x = x_ref[...]
        xb = x.astype(BF16)

        def proj(off, width):
            return _mm_nt(xb, win_ref[off:off + width, :])

        cos, sa, sb = _rope_tile(cs_ref[...])
        cos4, sa4, sb4 = (jnp.concatenate([t] * 4, axis=1) for t in (cos, sa, sb))
        qa = _rope(proj(O_QA, W_QA), cos4, sa4, sb4).astype(BF16)
        ka = _rope(proj(O_KA, W_KA), cos, sa, sb).astype(BF16)
        va = proj(O_VA, W_VA).astype(BF16)
        qa_ref[...] = qa
        ka_ref[...] = ka
        va_ref[...] = va

        later = {}

        def fill_ga():
            ga_s[...] = proj(O_GA, W_GA)

        def fill_gb():
            gb_s[...] = proj(O_GB, W_GB)

        def fill_qk():
            later["qb"] = proj(O_QB, W_QB)
            later["kb"] = proj(O_KB, W_KB)

        def fill_vr():
            later["vb"] = proj(O_VB, W_VB).astype(BF16)
            later["r"] = proj(O_R, W_R)

        fillers = [fill_ga, fill_gb, fill_qk, fill_vr]

        bias_inner = _attn_bias(True)
        sink_cols = [_sink_col(sinks_ref, j) for j in range(KV_HEADS)]
        for b in range(nblk):
            rows = slice(b * BLOCK, (b + 1) * BLOCK)
            mask = _attn_bias(i > 0) if b == 0 else bias_inner
            k_cur = ka[rows]
            v_cur = va[rows]
            k_old = kprev[...] if b == 0 else ka[(b - 1) * BLOCK:b * BLOCK]
            v_old = vprev[...] if b == 0 else va[(b - 1) * BLOCK:b * BLOCK]
            outs = []
            for j in range(KV_HEADS):
                hs = slice(j * HEAD_A, (j + 1) * HEAD_A)
                kc = jnp.concatenate([k_old[:, hs], k_cur[:, hs]], axis=0)
                vc = jnp.concatenate([v_old[:, hs], v_cur[:, hs]], axis=0)
                s = _scores(_stack_heads(qa[rows], j), kc, mask)
                if fillers:
                    fillers.pop(0)()
                probs, _ = _softmax(s, sink_cols[j])
                outs.append(_mm(probs.astype(BF16), vc))
            attn_s[rows, :] = _unstack_heads(outs)
        kprev[...] = ka[(nblk - 1) * BLOCK:]
        vprev[...] = va[(nblk - 1) * BLOCK:]
        while fillers:
            fillers.pop(0)()

        r = later["r"]
        r_ref[...] = r
        z = _mm(r.astype(BF16), wgu_ref[...].astype(BF16)) + bg_ref[...]
        log_a = _log_sigmoid(z) / GLA_TAU
        bcum = _tri_mm(_chunk_tri(TRI_SLAB, False), log_a)
        qb = later["qb"]
        kb = later["kb"]
        vb = later["vb"]
        qb_ref[...] = qb
        kb_ref[...] = kb
        vb_ref[...] = vb
        qd_all = (qb * GLA_SCALE * jnp.exp(bcum)).astype(BF16)
        ki_all = (kb * jnp.exp(-bcum)).astype(BF16)
        tril = lax.broadcasted_iota(jnp.int32, (CHUNK, CHUNK), 0) >= lax.broadcasted_iota(jnp.int32, (CHUNK, CHUNK), 1)
        st = state[...]
        for c in range(nch):
            rows = slice(c * CHUNK, (c + 1) * CHUNK)
            b_c = bcum[rows]
            b_last = b_c[CHUNK - 1:CHUNK]
            ke = (kb[rows] * jnp.exp(b_last - b_c)).astype(BF16)
            st_ref[c] = st
            st16 = st.astype(BF16)
            o_parts, u_parts = [], []
            for h in range(GLA_HEADS):
                ks = slice(h * GLA_DK, (h + 1) * GLA_DK)
                vs = slice(h * GLA_DV, (h + 1) * GLA_DV)
                qd = qd_all[rows, ks]
                v_h = vb[rows, vs]
                a = jnp.where(tril, _mm_nt(qd, ki_all[rows, ks]), 0.0)
                o_parts.append(_mm(a.astype(BF16), v_h) + _mm_nt(qd, st16[:, ks]))
                u_parts.append(_mm_tn(v_h, ke[:, ks]))
            ob_s[rows, :] = jnp.concatenate(o_parts, axis=1)
            st = st * jnp.exp(b_last) + jnp.concatenate(u_parts, axis=1)
        state[...] = st

        ga = ga_s[...]
        sg_a = _sigmoid(ga)
        silu_a = ga * sg_a
        attn = attn_s[...]
        cat_s[:, :W_GA] = (attn * silu_a).astype(BF16)
        gb = gb_s[...]
        sg_b = _sigmoid(gb)
        silu_b = gb * sg_b
        nw = nw_ref[...]
        on_parts = []
        for h in range(GLA_HEADS):
            vs = slice(h * GLA_DV, (h + 1) * GLA_DV)
            o_h = ob_s[:, vs]
            rs = lax.rsqrt(jnp.mean(o_h * o_h, axis=1, keepdims=True) + EPS)
            on_parts.append(o_h * rs * nw)
        on = jnp.concatenate(on_parts, axis=1)
        cat_s[:, W_GA:] = (on * silu_b).astype(BF16)
        cat = cat_s[...]
        hres = ALPHA * x + _mm(cat, wout_ref[...])
        mu = jnp.mean(hres, axis=1, keepdims=True)
        hc = hres - mu
        rstd = lax.rsqrt(jnp.mean(hc * hc, axis=1, keepdims=True) + EPS)
        xhat = hc * rstd
        g_ln = lng_ref[...]
        err = xhat * g_ln + lnb_ref[...] - t_ref[...]
        loss_ref[...] += jnp.sum(err * err) * (0.5 / D_MODEL)
        dy = err * (1.0 / D_MODEL)
        glng_ref[...] += jnp.sum(dy * xhat, axis=0, keepdims=True)
        glnb_ref[...] += jnp.sum(dy, axis=0, keepdims=True)
        dxh = dy * g_ln
        dh = rstd * (dxh - jnp.mean(dxh, axis=1, keepdims=True) - xhat * jnp.mean(dxh * xhat, axis=1, keepdims=True))
        dh_ref[...] = dh
        dh16 = dh.astype(BF16)
        for h in range(2):
            dwout_ref[h] += _mm_tn(cat, dh16[:, h * HALF:(h + 1) * HALF])
        dcat = _mm_nt(dh16, wout_ref[...])

        d_a = dcat[:, :W_GA]
        dattn_ref[...] = (d_a * silu_a).astype(BF16)
        dga_ref[...] = (d_a * attn * (sg_a * (1.0 + ga * (1.0 - sg_a)))).astype(BF16)
        d_b = dcat[:, W_GA:]
        dgb_ref[...] = (d_b * on * (sg_b * (1.0 + gb * (1.0 - sg_b)))).astype(BF16)
        d_on = d_b * silu_b
        gnw = jnp.zeros((1, GLA_DV), F32)
        do_parts = []
        for h in range(GLA_HEADS):
            vs = slice(h * GLA_DV, (h + 1) * GLA_DV)
            o_h = ob_s[:, vs]
            rs = lax.rsqrt(jnp.mean(o_h * o_h, axis=1, keepdims=True) + EPS)
            d_on_h = d_on[:, vs]
            gnw = gnw + jnp.sum(d_on_h * o_h * rs, axis=0, keepdims=True)
            gg = d_on_h * nw
            do_parts.append(rs * gg - o_h * (rs * rs * rs) * jnp.mean(gg * o_h, axis=1, keepdims=True))
        gnw_ref[...] += gnw
        dob_ref[...] = jnp.concatenate(do_parts, axis=1).astype(BF16)

    tile = lambda w: pl.BlockSpec((TM, w), lambda i: (i, 0))
    whole = lambda shape: pl.BlockSpec(shape, lambda i: tuple(0 for _ in shape), pipeline_mode=pl.Buffered(1))
    out_shape = (
        jax.ShapeDtypeStruct((s_len, W_QA), BF16),
        jax.ShapeDtypeStruct((s_len, W_KA), BF16),
        jax.ShapeDtypeStruct((s_len, W_VA), BF16),
        jax.ShapeDtypeStruct((s_len, W_QB), F32),
        jax.ShapeDtypeStruct((s_len, W_KB), F32),
        jax.ShapeDtypeStruct((s_len, W_VB), BF16),
        jax.ShapeDtypeStruct((s_len, W_R), F32),
        jax.ShapeDtypeStruct((s_len, W_GA), BF16),
        jax.ShapeDtypeStruct((s_len, W_GA), BF16),
        jax.ShapeDtypeStruct((s_len, W_GB), BF16),
        jax.ShapeDtypeStruct((s_len, W_GB), BF16),
        jax.ShapeDtypeStruct((s_len, D_MODEL), F32),
        jax.ShapeDtypeStruct((s_len // CHUNK, GLA_DV, GLA_HEADS * GLA_DK), F32),
        jax.ShapeDtypeStruct((2, D_MODEL, HALF), F32),
        jax.ShapeDtypeStruct((1, D_MODEL), F32),
        jax.ShapeDtypeStruct((1, D_MODEL), F32),
        jax.ShapeDtypeStruct((1, GLA_DV), F32),
        jax.ShapeDtypeStruct((1, 128), F32),
    )
    out_specs = (
        tile(W_QA), tile(W_KA), tile(W_VA), tile(W_QB), tile(W_KB), tile(W_VB), tile(W_R),
        tile(W_GA), tile(W_GA), tile(W_GB), tile(W_GB), tile(D_MODEL),
        pl.BlockSpec((nch, GLA_DV, GLA_HEADS * GLA_DK), lambda i: (i, 0, 0)),
        whole((2, D_MODEL, HALF)), whole((1, D_MODEL)), whole((1, D_MODEL)), whole((1, GLA_DV)), whole((1, 128)),
    )
    in_specs = [
        tile(D_MODEL), tile(D_MODEL), pl.BlockSpec((16, TM), lambda i: (0, i)),
        whole((D_PROJ, D_MODEL)), whole((D_MODEL, D_MODEL)), whole((W_R, W_KB)), whole((1, W_KB)),
        pl.BlockSpec(memory_space=pltpu.SMEM), whole((1, GLA_DV)), whole((1, D_MODEL)), whole((1, D_MODEL)),
    ]
    scratch = [
        pltpu.VMEM((BLOCK, W_KA), BF16), pltpu.VMEM((BLOCK, W_VA), BF16),
        pltpu.VMEM((GLA_DV, GLA_HEADS * GLA_DK), F32),
        pltpu.VMEM((TM, W_GA), F32), pltpu.VMEM((TM, W_GA), F32), pltpu.VMEM((TM, W_GB), F32),
        pltpu.VMEM((TM, W_GB), F32), pltpu.VMEM((TM, D_MODEL), BF16),
    ]
    return pl.pallas_call(
        body, name="fwd_head", grid=(nt,), in_specs=in_specs, out_specs=out_specs, out_shape=out_shape,
        scratch_shapes=scratch,
        compiler_params=pltpu.CompilerParams(dimension_semantics=("arbitrary",), vmem_limit_bytes=VMEM_LIMIT),
    )(x, tgt, cs, win, wout, wgu, bg, sinks, nw, lng, lnb)


def _bwd_call(x, dh, qa, ka, va, dattn, dga, dob, dgb, qb, kb, vb, r, st, cs, win, wgu, bg, sinks):
    s_len = x.shape[0]
    nt = s_len // TM
    nblk = TM // BLOCK
    nch = TM // CHUNK

    def body(x_ref, dh_ref, qa_ref, ka_ref, va_ref, kap_ref, vap_ref, dattn_ref, dga_ref, dob_ref, dgb_ref,
             qb_ref, kb_ref, vb_ref, r_ref, st_ref, cs_ref, win_ref, wgu_ref, bg_ref, sinks_ref,
             gx_ref, dwin_ref, gsink_ref, gbg_ref, gwgu_ref,
             dproj, dk_carry, dv_carry, ds_carry, db_s, dwin_acc):
        i = pl.program_id(0)
        t = nt - 1 - i

        @pl.when(i == 0)
        def _():
            dk_carry[...] = jnp.zeros_like(dk_carry)
            dv_carry[...] = jnp.zeros_like(dv_carry)
            ds_carry[...] = jnp.zeros_like(ds_carry)
            dwin_acc[...] = jnp.zeros_like(dwin_acc)
            gsink_ref[...] = jnp.zeros_like(gsink_ref)
            gbg_ref[...] = jnp.zeros_like(gbg_ref)
            gwgu_ref[...] = jnp.zeros_like(gwgu_ref)

        cos, sa, sb = _rope_tile(cs_ref[...])
        cos4, sa4, sb4 = (jnp.concatenate([v] * 4, axis=1) for v in (cos, sa, sb))

        qa = qa_ref[...]
        ka = ka_ref[...]
        va = va_ref[...]
        dattn = dattn_ref[...]
        gsink_rows = [jnp.zeros((1, 1), F32) for _ in range(Q_HEADS)]
        bias_inner = _attn_bias(True)
        sink_cols = [_sink_col(sinks_ref, j) for j in range(KV_HEADS)]
        for b in reversed(range(nblk)):
            rows = slice(b * BLOCK, (b + 1) * BLOCK)
            mask = _attn_bias(t > 0) if b == 0 else bias_inner
            k_cur = ka[rows]
            v_cur = va[rows]
            k_old = kap_ref[...] if b == 0 else ka[(b - 1) * BLOCK:b * BLOCK]
            v_old = vap_ref[...] if b == 0 else va[(b - 1) * BLOCK:b * BLOCK]
            dq_parts, dk_parts, dv_parts = [], [], []
            for j in range(KV_HEADS):
                hs = slice(j * HEAD_A, (j + 1) * HEAD_A)
                kc = jnp.concatenate([k_old[:, hs], k_cur[:, hs]], axis=0)
                vc = jnp.concatenate([v_old[:, hs], v_cur[:, hs]], axis=0)
                qs = _stack_heads(qa[rows], j)
                do_s = _stack_heads(dattn[rows], j)
                probs, p_sink = _softmax_block(qs, kc, mask, sink_cols[j])
                dp = _mm_nt(do_s, vc)
                d_row = jnp.sum(probs * dp, axis=1, keepdims=True)
                ds16 = (probs * (dp - d_row) * ATT_SCALE).astype(BF16)
                dq_parts.append(_mm(ds16, kc))
                dk_parts.append(_mm_tn(ds16, qs))
                dv_parts.append(_mm_tn(probs.astype(BF16), do_s))
                t_sink = d_row * p_sink
                for g in range(GROUP):
                    gsink_rows[GROUP * j + g] = gsink_rows[GROUP * j + g] - jnp.sum(
                        t_sink[g * BLOCK:(g + 1) * BLOCK], axis=0, keepdims=True)
            dq = _rope_bwd(_unstack_heads(dq_parts), cos4[rows], sa4[rows], sb4[rows])
            dproj[rows, O_QA:O_QA + W_QA] = dq.astype(BF16)
            dk_cur = dk_carry[...] + jnp.concatenate([p[BLOCK:] for p in dk_parts], axis=1)
            dv_cur = dv_carry[...] + jnp.concatenate([p[BLOCK:] for p in dv_parts], axis=1)
            dproj[rows, O_KA:O_KA + W_KA] = _rope_bwd(dk_cur, cos[rows], sa[rows], sb[rows]).astype(BF16)
            dproj[rows, O_VA:O_VA + W_VA] = dv_cur.astype(BF16)
            dk_carry[...] = jnp.concatenate([p[:BLOCK] for p in dk_parts], axis=1)
            dv_carry[...] = jnp.concatenate([p[:BLOCK] for p in dv_parts], axis=1)
        for hq in range(Q_HEADS):
            gsink_ref[hq:hq + 1, :] += jnp.broadcast_to(gsink_rows[hq], (1, 128))

        dproj[:, O_GA:O_GA + W_GA] = dga_ref[...]
        dproj[:, O_GB:O_GB + W_GB] = dgb_ref[...]

        r16 = r_ref[...].astype(BF16)
        wgu16 = wgu_ref[...].astype(BF16)
        z = _mm(r16, wgu16) + bg_ref[...]
        log_a = _log_sigmoid(z) / GLA_TAU
        bcum = _tri_mm(_chunk_tri(TRI_SLAB, False), log_a)
        qb = qb_ref[...]
        kb = kb_ref[...]
        vb = vb_ref[...]
        dob = dob_ref[...]
        e_b = jnp.exp(bcum)
        e_nb = jnp.exp(-bcum)
        qd_f = qb * GLA_SCALE * e_b
        ki_f = kb * e_nb
        qd_all = qd_f.astype(BF16)
        ki_all = ki_f.astype(BF16)
        tril = lax.broadcasted_iota(jnp.int32, (CHUNK, CHUNK), 0) >= lax.broadcasted_iota(jnp.int32, (CHUNK, CHUNK), 1)
        last_row = lax.broadcasted_iota(jnp.int32, (CHUNK, 1), 0) == CHUNK - 1

        x16 = x_ref[...].astype(BF16)
        dp_a = dproj[:, 0:O_QB]
        early = {}

        def fill_dx():
            early["gx"] = _mm(dp_a, win_ref[0:O_QB, :])

        def fill_dw(h):
            dwin_acc[h, 0:O_QB, :] += _mm_tn(dp_a, x16[:, h * HALF:(h + 1) * HALF])

        fillers = [fill_dx, functools.partial(fill_dw, 0), functools.partial(fill_dw, 1)]
        dsn = ds_carry[...]
        for c in reversed(range(nch)):
            if fillers:
                fillers.pop(0)()
            rows = slice(c * CHUNK, (c + 1) * CHUNK)
            b_c = bcum[rows]
            b_last = b_c[CHUNK - 1:CHUNK]
            e_e = jnp.exp(b_last - b_c)
            dec = jnp.exp(b_last)
            ke_f = kb[rows] * e_e
            ke = ke_f.astype(BF16)
            sp = st_ref[c]
            sp16 = sp.astype(BF16)
            dsn16 = dsn.astype(BF16)
            dqd_p, dki_p, dke_p, dv_p, dsp_p = [], [], [], [], []
            for h in range(GLA_HEADS):
                ks = slice(h * GLA_DK, (h + 1) * GLA_DK)
                vs = slice(h * GLA_DV, (h + 1) * GLA_DV)
                qd = qd_all[rows, ks]
                ki = ki_all[rows, ks]
                v_h = vb[rows, vs]
                do_h = dob[rows, vs]
                a16 = jnp.where(tril, _mm_nt(qd, ki), 0.0).astype(BF16)
                da16 = jnp.where(tril, _mm_nt(do_h, v_h), 0.0).astype(BF16)
                dv_p.append(_mm_tn(a16, do_h) + _mm_nt(ke[:, ks], dsn16[:, ks]))
                dqd_p.append(_mm(da16, ki) + _mm(do_h, sp16[:, ks]))
                dki_p.append(_mm_tn(da16, qd))
                dke_p.append(_mm(v_h, dsn16[:, ks]))
                dsp_p.append(_mm_tn(do_h, qd))
            dqd = jnp.concatenate(dqd_p, axis=1)
            dki = jnp.concatenate(dki_p, axis=1)
            dke = jnp.concatenate(dke_p, axis=1)
            ddec = jnp.sum(dsn * sp, axis=0, keepdims=True)
            dsn_next = dsn * dec + jnp.concatenate(dsp_p, axis=1)
            dproj[rows, O_QB:O_QB + W_QB] = (dqd * e_b[rows] * GLA_SCALE).astype(BF16)
            dproj[rows, O_KB:O_KB + W_KB] = (dki * e_nb[rows] + dke * e_e).astype(BF16)
            dproj[rows, O_VB:O_VB + W_VB] = jnp.concatenate(dv_p, axis=1).astype(BF16)
            dke_ke = dke * ke_f
            d_b = dqd * qd_f[rows] - dki * ki_f[rows] - dke_ke
            d_bl = jnp.sum(dke_ke, axis=0, keepdims=True) + ddec * dec
            db_s[rows, :] = d_b + jnp.where(last_row, d_bl, 0.0)
            dsn = dsn_next
        ds_carry[...] = dsn
        while fillers:
            fillers.pop(0)()
        dlog_a = _tri_mm(_chunk_tri(TRI_SLAB, True), db_s[...])
        dz = dlog_a * (1.0 / GLA_TAU) * _sigmoid(-z)
        dz16 = dz.astype(BF16)
        gbg_ref[...] += jnp.sum(dz, axis=0, keepdims=True)
        gwgu_ref[...] += _mm_tn(r16, dz16)
        dproj[:, O_R:O_R + W_R] = _mm_nt(dz16, wgu16).astype(BF16)

        dp_b = dproj[:, O_QB:D_PROJ]
        gx_ref[...] = ALPHA * dh_ref[...] + early["gx"] + _mm(dp_b, win_ref[O_QB:D_PROJ, :])
        for h in range(2):
            dwin_acc[h, O_QB:D_PROJ, :] += _mm_tn(dp_b, x16[:, h * HALF:(h + 1) * HALF])

        @pl.when(i == nt - 1)
        def _():
            dwin_ref[...] = dwin_acc[...].astype(BF16)

    tile = lambda w: pl.BlockSpec((TM, w), lambda i: (nt - 1 - i, 0))
    whole = lambda shape: pl.BlockSpec(shape, lambda i: tuple(0 for _ in shape), pipeline_mode=pl.Buffered(1))
    prev_blk = pl.BlockSpec((BLOCK, W_KA), lambda i: (jnp.maximum((nt - 1 - i) * nblk - 1, 0), 0))
    in_specs = [
        tile(D_MODEL), tile(D_MODEL), tile(W_QA), tile(W_KA), tile(W_VA), prev_blk, prev_blk,
        tile(W_GA), tile(W_GA), tile(W_GB), tile(W_GB), tile(W_QB), tile(W_KB), tile(W_VB), tile(W_R),
        pl.BlockSpec((nch, GLA_DV, GLA_HEADS * GLA_DK), lambda i: (nt - 1 - i, 0, 0)),
        pl.BlockSpec((16, TM), lambda i: (0, nt - 1 - i)),
        whole((D_PROJ, D_MODEL)), whole((W_R, W_KB)), whole((1, W_KB)), pl.BlockSpec(memory_space=pltpu.SMEM),
    ]
    out_shape = (
        jax.ShapeDtypeStruct((s_len, D_MODEL), F32),
        jax.ShapeDtypeStruct((2, ACC_ROWS, HALF), BF16),
        jax.ShapeDtypeStruct((Q_HEADS, 128), F32),
        jax.ShapeDtypeStruct((1, W_KB), F32),
        jax.ShapeDtypeStruct((W_R, W_KB), F32),
    )
    out_specs = (tile(D_MODEL), whole((2, ACC_ROWS, HALF)), whole((Q_HEADS, 128)), whole((1, W_KB)),
                 whole((W_R, W_KB)))
    scratch = [
        pltpu.VMEM((TM, D_PROJ), BF16), pltpu.VMEM((BLOCK, W_KA), F32), pltpu.VMEM((BLOCK, W_VA), F32),
        pltpu.VMEM((GLA_DV, GLA_HEADS * GLA_DK), F32), pltpu.VMEM((TM, W_KB), F32),
        pltpu.VMEM((2, ACC_ROWS, HALF), F32),
    ]
    return pl.pallas_call(
        body, name="bwd_mix", grid=(nt,), in_specs=in_specs, out_specs=out_specs, out_shape=out_shape,
        scratch_shapes=scratch,
        compiler_params=pltpu.CompilerParams(dimension_semantics=("arbitrary",), vmem_limit_bytes=VMEM_LIMIT),
    )(x, dh, qa, ka, va, ka, va, dattn, dga, dob, dgb, qb, kb, vb, r, st, cs, win, wgu, bg, sinks)


def _mesh_place():
    x, y, c = lax.axis_index("x"), lax.axis_index("y"), lax.axis_index("c")
    chips = [(1 - x, y), (x, 1 - y), (1 - x, 1 - y)]
    return x, y, c, chips


def _gather_weights_call(w_lin, w_out, wgu):
    def body(wlin_ref, wout_ref, wgu_ref, wt_ref, wout_full, wgu_all, blk, oblk, asm, send_sems, recv_sems):
        x, y, c, chips = _mesh_place()
        k_me = 2 * x + y
        asm[SHARD_IN - 4:SHARD_PAD, :] = jnp.zeros((SHARD_PAD - SHARD_IN + 4, D_MODEL), F32)
        asm[0:SHARD_IN, :] = wlin_ref[:, 0, :]
        for h in range(2):
            blk[k_me, h] = asm[0:SHARD_PAD, h * HALF:(h + 1) * HALF].astype(BF16)
            oblk[k_me, h] = wout_ref[:, h * HALF:(h + 1) * HALF].astype(BF16)
        wgu_all[k_me] = wgu_ref[...]

        parts = ((0, GATHER_SPLIT, 0, SHARD_OUT // 2), (GATHER_SPLIT, SHARD_PAD - GATHER_SPLIT, SHARD_OUT // 2, SHARD_OUT // 2))
        me_id, sib_id = (x, y, c), (x, y, 1 - c)
        nbr_x, nbr_y, diag = ((*chip, c) for chip in chips)
        k_x, k_y, k_d = (2 * chip[0] + chip[1] for chip in chips)

        def copies(k, hc, p, sem0, to):
            if p is None:
                refs = (blk.at[k, hc], oblk.at[k, hc])
            else:
                r0, rn, o0, on = parts[p]
                refs = (blk.at[k, hc, pl.ds(r0, rn), :], oblk.at[k, hc, pl.ds(o0, on), :])
            return [pltpu.make_async_remote_copy(src_ref=ref, dst_ref=ref, send_sem=send_sems.at[sem0 + n],
                                                 recv_sem=recv_sems.at[sem0 + n], device_id=to, device_id_type=MESH)
                    for n, ref in enumerate(refs)]

        def gu_copy(k, r, to):
            return pltpu.make_async_remote_copy(src_ref=wgu_all.at[k], dst_ref=wgu_all.at[k], send_sem=send_sems.at[18 + r],
                                                recv_sem=recv_sems.at[18 + r], device_id=to, device_id_type=MESH)

        def start(cps):
            for cp in cps:
                cp.start()
            return cps

        def landed(cps):
            for cp in cps:
                cp.wait_recv()

        started = start(copies(k_me, c, 0, 0, nbr_x) + copies(k_me, c, 1, 6, nbr_y)
                        + copies(k_me, c, 1, 2, nbr_x) + copies(k_me, c, 0, 4, nbr_y)
                        + [gu_copy(k_me, r, to) for r, to in enumerate((nbr_x, nbr_y, diag))])
        landed(copies(k_x, c, 0, 0, me_id))
        started += start(copies(k_x, c, 0, 8, nbr_y))
        landed(copies(k_y, c, 1, 6, me_id))
        started += start(copies(k_y, c, 1, 10, nbr_x))
        landed(copies(k_x, c, 1, 2, me_id))
        started += start(copies(k_x, c, None, 12, sib_id))
        landed(copies(k_y, c, 0, 4, me_id))
        started += start(copies(k_y, c, None, 14, sib_id))
        landed(copies(k_d, c, 0, 8, me_id) + copies(k_d, c, 1, 10, me_id))
        started += start(copies(k_d, c, None, 16, sib_id))
        for r, k_r in enumerate((k_x, k_y, k_d)):
            landed(copies(k_r, 1 - c, None, 12 + 2 * r, me_id))
            gu_copy(k_r, r, me_id).wait_recv()
        for cp in started:
            cp.wait_send()

        for k in range(N_CHIPS):
            for h in range(2):
                asm[k * SHARD_IN:k * SHARD_IN + SHARD_PAD, h * HALF:(h + 1) * HALF] = blk[k, h].astype(F32)
                wout_full[k * SHARD_OUT:(k + 1) * SHARD_OUT, h * HALF:(h + 1) * HALF] = oblk[k, h]
        wt_ref[...] = asm[0:D_PROJ, :].astype(BF16)

    vmem = pl.BlockSpec(memory_space=pltpu.VMEM)
    return pl.pallas_call(
        body, name="gather_weights",
        out_shape=(jax.ShapeDtypeStruct((D_PROJ, D_MODEL), BF16),
                   jax.ShapeDtypeStruct((D_MODEL, D_MODEL), BF16),
                   jax.ShapeDtypeStruct((N_CHIPS, W_R, W_KB // N_CHIPS), F32)),
        in_specs=[vmem, vmem, vmem], out_specs=(vmem, vmem, vmem),
        scratch_shapes=[pltpu.VMEM((N_CHIPS, 2, SHARD_PAD, HALF), BF16), pltpu.VMEM((N_CHIPS, 2, SHARD_OUT, HALF), BF16),
                        pltpu.VMEM((ACC_ROWS, D_MODEL), F32),
                        pltpu.SemaphoreType.DMA((21,)), pltpu.SemaphoreType.DMA((21,))],
        compiler_params=pltpu.CompilerParams(vmem_limit_bytes=VMEM_LIMIT),
    )(w_lin, w_out, wgu)


def _adamw(w, g, m, v):
    m = ADAM_B1 * m + (1.0 - ADAM_B1) * g
    v = ADAM_B2 * v + (1.0 - ADAM_B2) * (g * g)
    m_hat = m * (1.0 / (1.0 - ADAM_B1 ** ADAM_STEP))
    v_hat = v * (1.0 / (1.0 - ADAM_B2 ** ADAM_STEP))
    delta = -ADAM_LR * (m_hat / (jnp.sqrt(v_hat) + ADAM_EPS) + ADAM_WD * w)
    return delta, m, v


N_SMALL = 6


def _reduce_grads_call(g_in, g_out, small_grads):
    def body(gin_hbm, gout_hbm, g_lng, g_lnb, g_bg, g_nw, g_sink, g_wgu, loss_in, lin_in, fin_out, tot_out,
             a_in, a_out, b_in, b_out, ab_in, c_in, s_in, s_out, r_in, r_out, f_in, f_out, pack_ref, tot_ref, pack_all,
             send_sems, recv_sems, local_sems):
        x, y, c, chips = _mesh_place()
        k_me = 2 * x + y
        me = 4 * x + 2 * y + c
        sibling = (x, y, 1 - c)

        pack_ref[...] = jnp.zeros_like(pack_ref)
        for a in range(8):
            pack_ref[P_LNG + a:P_LNG + a + 1, :] = g_lng[:, a * 128:(a + 1) * 128]
            pack_ref[P_LNB + a:P_LNB + a + 1, :] = g_lnb[:, a * 128:(a + 1) * 128]
        for a in range(2):
            pack_ref[P_BG + a:P_BG + a + 1, :] = g_bg[:, a * 128:(a + 1) * 128]
        pack_ref[P_NW:P_NW + 1, :] = g_nw[...]
        lane = lax.broadcasted_iota(jnp.int32, (1, 128), 1)
        sink_row = jnp.zeros((1, 128), F32)
        for hq in range(Q_HEADS):
            sink_row = jnp.where(lane == hq, g_sink[hq:hq + 1, :], sink_row)
        pack_ref[P_SINK:P_SINK + 1, :] = sink_row
        pack_ref[P_LOSS:P_LOSS + 1, :] = loss_in[...]
        gu_w = W_KB // N_CHIPS
        for k in range(N_CHIPS):
            pack_ref[P_GU + W_R * k:P_GU + W_R * (k + 1), 0:gu_w] = g_wgu[:, k * gu_w:(k + 1) * gu_w]
        pack_all[me] = pack_ref[...]
        small = []
        for mask in range(1, 8):
            peer = (x ^ (mask >> 2), y ^ ((mask >> 1) & 1), c ^ (mask & 1))
            small.append(pltpu.make_async_remote_copy(
                src_ref=pack_ref, dst_ref=pack_all.at[me], send_sem=send_sems.at[mask], recv_sem=recv_sems.at[mask],
                device_id=peer, device_id_type=MESH))
        for cp in small:
            cp.start()

        loads = [pltpu.make_async_copy(gin_hbm.at[c], a_in, local_sems.at[0]),
                 pltpu.make_async_copy(gout_hbm.at[c], a_out, local_sems.at[1])]
        to_sib = [pltpu.make_async_remote_copy(
                      src_ref=gin_hbm.at[1 - c], dst_ref=b_in,
                      send_sem=send_sems.at[8], recv_sem=recv_sems.at[8], device_id=sibling, device_id_type=MESH),
                  pltpu.make_async_remote_copy(
                      src_ref=gout_hbm.at[1 - c], dst_ref=b_out,
                      send_sem=send_sems.at[9], recv_sem=recv_sems.at[9], device_id=sibling, device_id_type=MESH)]
        for cp in (loads[1], to_sib[1], loads[0], to_sib[0]):
            cp.start()

        parts = ((0, GATHER_SPLIT, 0, SHARD_OUT // 2), (GATHER_SPLIT, SHARD_PAD - GATHER_SPLIT, SHARD_OUT // 2, SHARD_OUT // 2))
        nbr_x, nbr_y, _ = ((*chip, c) for chip in chips)
        k_x, k_y, k_d = (2 * chip[0] + chip[1] for chip in chips)

        def rows_of(p, w):
            r0, rn, o0, on = parts[p]
            return (o0, on) if w else (r0, rn)

        def mine(k, p, w):
            r0, rn = rows_of(p, w)
            if w:
                return a_out[k, pl.ds(r0, rn), :] + b_out[k, pl.ds(r0, rn), :]
            return c_in[k, pl.ds(r0, rn), :]

        def message(m, p, w, to):
            rn = rows_of(p, w)[1]
            stage, land = (s_out, r_out) if w else (s_in, r_in)
            return pltpu.make_async_remote_copy(
                src_ref=stage.at[m, pl.ds(0, rn), :], dst_ref=land.at[m, pl.ds(0, rn), :],
                send_sem=send_sems.at[10 + 2 * m + w], recv_sem=recv_sems.at[10 + 2 * m + w],
                device_id=to, device_id_type=MESH)

        def post(m, p, w, val, to):
            rn = rows_of(p, w)[1]
            stage = s_out if w else s_in
            stage[m, 0:rn, :] = val.astype(BF16)
            cp = message(m, p, w, to)
            cp.start()
            return [cp]

        def take(m, p, w):
            rn = rows_of(p, w)[1]
            message(m, p, w, (x, y, c)).wait_recv()
            land = r_out if w else r_in
            return land[m, 0:rn, :].astype(F32)

        sent = []
        for w in (1, 0):
            loads[w].wait()
            to_sib[w].wait_recv()
            if w == 0:
                ab_in[...] = a_in[...].astype(F32) + b_in[...].astype(F32)
                for k in range(N_CHIPS):
                    c_in[k] = ab_in[k * SHARD_IN:k * SHARD_IN + SHARD_PAD, :]
            sent += post(1, 0, w, mine(k_d, 0, w), nbr_x) + post(4, 1, w, mine(k_d, 1, w), nbr_y)
            sent += post(0, 0, w, mine(k_x, 0, w), nbr_x) + post(3, 1, w, mine(k_y, 1, w), nbr_y)
        for w in (1, 0):
            sent += post(5, 0, w, take(1, 0, w) + mine(k_y, 0, w), nbr_y)
            sent += post(2, 1, w, take(4, 1, w) + mine(k_x, 1, w), nbr_x)
        for w in (1, 0):
            for p, direct, summed in ((0, 0, 5), (1, 3, 2)):
                r0, rn = rows_of(p, w)
                total = mine(k_me, p, w) + take(direct, p, w) + take(summed, p, w)
                if w:
                    f_out[c, r0:r0 + rn, :] = total
                else:
                    f_in[c, r0:r0 + rn, :] = total

        swap = [pltpu.make_async_remote_copy(
                    src_ref=f_in.at[c], dst_ref=f_in.at[c],
                    send_sem=send_sems.at[22], recv_sem=recv_sems.at[22], device_id=sibling, device_id_type=MESH),
                pltpu.make_async_remote_copy(
                    src_ref=f_out.at[c], dst_ref=f_out.at[c],
                    send_sem=send_sems.at[23], recv_sem=recv_sems.at[23], device_id=sibling, device_id_type=MESH)]
        for cp in swap:
            cp.start()

        for cp in small:
            cp.wait_recv()
        total = pack_all[0]
        for d in range(1, 8):
            total = total + pack_all[d]
        tot_ref[...] = total
        tot_out[0:P_GU, :] = total[0:P_GU]
        tot_out[P_GU:PACK_OWN_ROWS, :] = tot_ref[pl.ds(pl.multiple_of(P_GU + W_R * k_me, 8), W_R), :]

        other_in = pltpu.make_async_remote_copy(
            src_ref=f_in.at[1 - c], dst_ref=f_in.at[1 - c],
            send_sem=send_sems.at[22], recv_sem=recv_sems.at[22], device_id=sibling, device_id_type=MESH)
        other_out = pltpu.make_async_remote_copy(
            src_ref=f_out.at[1 - c], dst_ref=f_out.at[1 - c],
            send_sem=send_sems.at[23], recv_sem=recv_sems.at[23], device_id=sibling, device_id_type=MESH)
        other_in.wait_recv()
        other_out.wait_recv()
        for cp in small + to_sib + sent + swap:
            cp.wait_send()

        for h in range(2):
            lin_in[:, h * HALF:(h + 1) * HALF] = f_in[h, 0:SHARD_IN, :]
            fin_out[:, h * HALF:(h + 1) * HALF] = f_out[h]

    vmem = pl.BlockSpec(memory_space=pltpu.VMEM)
    hbm = pl.BlockSpec(memory_space=pl.ANY)
    return pl.pallas_call(
        body, name="reduce_grads",
        out_shape=(jax.ShapeDtypeStruct((SHARD_IN, D_MODEL), F32), jax.ShapeDtypeStruct((SHARD_OUT, D_MODEL), F32),
                   jax.ShapeDtypeStruct((PACK_OWN_ROWS, 128), F32)),
        in_specs=[hbm, hbm] + [vmem] * 7, out_specs=(vmem,) * 3,
        scratch_shapes=[
            pltpu.VMEM((ACC_ROWS, HALF), BF16), pltpu.VMEM((N_CHIPS, SHARD_OUT, HALF), F32),
            pltpu.VMEM((ACC_ROWS, HALF), BF16), pltpu.VMEM((N_CHIPS, SHARD_OUT, HALF), F32),
            pltpu.VMEM((ACC_ROWS, HALF), F32),
            pltpu.VMEM((N_CHIPS, SHARD_PAD, HALF), F32),
            pltpu.VMEM((6, GATHER_SPLIT, HALF), BF16), pltpu.VMEM((6, SHARD_OUT // 2, HALF), BF16),
            pltpu.VMEM((6, GATHER_SPLIT, HALF), BF16), pltpu.VMEM((6, SHARD_OUT // 2, HALF), BF16),
            pltpu.VMEM((2, SHARD_PAD, HALF), F32), pltpu.VMEM((2, SHARD_OUT, HALF), F32),
            pltpu.VMEM((PACK_ROWS, 128), F32), pltpu.VMEM((PACK_ROWS, 128), F32), pltpu.VMEM((8, PACK_ROWS, 128), F32),
            pltpu.SemaphoreType.DMA((24,)), pltpu.SemaphoreType.DMA((24,)), pltpu.SemaphoreType.DMA((2,)),
        ],
        compiler_params=pltpu.CompilerParams(vmem_limit_bytes=VMEM_LIMIT),
    )(g_in, g_out, *small_grads)


def _adamw_call(g_in, w_in, m_in, v_in, g_out, w_out, m_out, v_out, tot, small_params):
    steps = 4
    rows_out = SHARD_OUT // steps
    cols = D_MODEL // steps
    gu_w = W_KB // N_CHIPS

    def body(gi, wi, mi, vi, go, wo, mo, vo, tot, *rest):
        params = rest[:3 * N_SMALL]
        gi_o, di, nmi, nvi, go_o, do, nmo, nvo, loss_out = rest[3 * N_SMALL:3 * N_SMALL + 9]
        small_out = rest[3 * N_SMALL + 9:]

        @pl.when(pl.program_id(0) == 0)
        def _():
            loss_out[...] = tot[P_LOSS:P_LOSS + 1, :]
            g_outs = small_out[0:N_SMALL]
            for a in range(8):
                g_outs[0][:, a * 128:(a + 1) * 128] = tot[P_LNG + a:P_LNG + a + 1, :]
                g_outs[1][:, a * 128:(a + 1) * 128] = tot[P_LNB + a:P_LNB + a + 1, :]
            for a in range(2):
                g_outs[2][:, a * 128:(a + 1) * 128] = tot[P_BG + a:P_BG + a + 1, :]
            g_outs[3][...] = tot[P_NW:P_NW + 1, :]
            g_outs[4][...] = tot[P_SINK:P_SINK + 1, 0:Q_HEADS]
            g_outs[5][...] = tot[P_GU:PACK_OWN_ROWS, 0:gu_w]
            for n in range(N_SMALL):
                w_ref, m_ref, v_ref = params[3 * n:3 * n + 3]
                delta, new_m, new_v = _adamw(w_ref[...], g_outs[n][...], m_ref[...], v_ref[...])
                small_out[N_SMALL + n][...] = delta
                small_out[2 * N_SMALL + n][...] = new_m
                small_out[3 * N_SMALL + n][...] = new_v

        g = gi[...]
        delta, new_m, new_v = _adamw(wi[:, 0, :], g, mi[:, 0, :], vi[:, 0, :])
        gi_o[:, 0, :] = g
        di[:, 0, :] = delta
        nmi[:, 0, :] = new_m
        nvi[:, 0, :] = new_v
        g = go[...]
        go_o[...] = g
        do[...], nmo[...], nvo[...] = _adamw(wo[...], g, mo[...], vo[...])

    t_g = pl.BlockSpec((SHARD_IN, cols), lambda i: (0, i))
    t_in = pl.BlockSpec((SHARD_IN, 1, cols), lambda i: (0, 0, i))
    t_out = pl.BlockSpec((rows_out, D_MODEL), lambda i: (i, 0))
    s_in = jax.ShapeDtypeStruct((SHARD_IN, 1, D_MODEL), F32)
    s_out = jax.ShapeDtypeStruct((SHARD_OUT, D_MODEL), F32)
    whole = lambda shape: pl.BlockSpec(shape, lambda i: tuple(0 for _ in shape))
    small_specs = [whole(p.shape) for p in small_params]
    small_shapes = [jax.ShapeDtypeStruct(p.shape, F32) for p in small_params[0::3]] * 4
    return pl.pallas_call(
        body, name="adamw", grid=(steps,),
        in_specs=[t_g] + [t_in] * 3 + [t_out] * 4 + [whole(tot.shape)] + small_specs,
        out_specs=(t_in,) * 4 + (t_out,) * 4 + (whole((1, 128)),) + tuple(small_specs[0::3] * 4),
        out_shape=(s_in,) * 4 + (s_out,) * 4 + (jax.ShapeDtypeStruct((1, 128), F32),) + tuple(small_shapes),
        compiler_params=pltpu.CompilerParams(dimension_semantics=("arbitrary",)),
    )(g_in, w_in, m_in, v_in, g_out, w_out, m_out, v_out, tot, *small_params)


def _rope_tables(positions):
    half = 8
    inv_freq = 500000.0 ** (-jnp.arange(half, dtype=F32) / half)
    ang = inv_freq[:, None] * positions.astype(F32)[None, :]
    return jnp.concatenate([jnp.cos(ang), jnp.sin(ang)], axis=0)


def kernel(x, positions, w_in, gla_w_gate_up, gla_b_gate, attn_sinks, gla_norm_w, w_out, ln_g, ln_b, loss_target, m_w_in, m_gla_w_gate_up, m_gla_b_gate, m_attn_sinks, m_gla_norm_w, m_w_out, m_ln_g, m_ln_b, v_w_in, v_gla_w_gate_up, v_gla_b_gate, v_attn_sinks, v_gla_norm_w, v_w_out, v_ln_g, v_ln_b):
    def lin3(w):
        return jnp.transpose(w, (2, 0, 1))

    def unlin(w):
        return jnp.transpose(w, (1, 2, 0))

    win, wout, wgu_all = _gather_weights_call(lin3(w_in), w_out[0], gla_w_gate_up[0])
    wgu = jnp.transpose(wgu_all, (1, 0, 2)).reshape(W_R, W_KB)
    cs = _rope_tables(positions[0])
    sinks = attn_sinks[0]

    (qa, ka, va, qb, kb, vb, r, dattn, dga, dob, dgb, dh, st, g_wout, g_lng, g_lnb, g_nw, loss) = _fwd_call(
        x[0], loss_target[0], cs, win, wout, wgu, gla_b_gate, sinks, gla_norm_w, ln_g, ln_b)
    gx, g_win, g_sink, g_bg, g_wgu = _bwd_call(
        x[0], dh, qa, ka, va, dattn, dga, dob, dgb, qb, kb, vb, r, st, cs, win, wgu, gla_b_gate, sinks)

    g_wout_by_chip = g_wout.reshape(2, N_CHIPS, SHARD_OUT, HALF)
    small_params = []
    for group in ((ln_g, m_ln_g, v_ln_g), (ln_b, m_ln_b, v_ln_b), (gla_b_gate, m_gla_b_gate, v_gla_b_gate),
                  (gla_norm_w, m_gla_norm_w, v_gla_norm_w), (attn_sinks, m_attn_sinks, v_attn_sinks)):
        small_params += list(group)
    small_params += [gla_w_gate_up[0], m_gla_w_gate_up[0], v_gla_w_gate_up[0]]
    fin_in, fin_out, tot = _reduce_grads_call(g_win, g_wout_by_chip, (g_lng, g_lnb, g_bg, g_nw, g_sink, g_wgu, loss))
    fin_in, d_in, nm_in, nv_in, fin_out, d_out, nm_out, nv_out, loss_sum, *small_out = _adamw_call(
        fin_in, lin3(w_in), lin3(m_w_in), lin3(v_w_in), fin_out, w_out[0], m_w_out[0], v_w_out[0], tot, small_params)

    def unpack(kind, big_in, big_out):
        lng_, lnb_, bg_, nw_, sink_, gu_ = small_out[kind * N_SMALL:(kind + 1) * N_SMALL]
        return (unlin(big_in), gu_[None], bg_, sink_, nw_, big_out[None], lng_, lnb_)

    loss_total = loss_sum[0, 0]
    g_s, d_s, nm_s, nv_s = 0, 1, 2, 3
    return (loss_total, gx[None], *unpack(g_s, fin_in, fin_out), *unpack(d_s, d_in, d_out),
            *unpack(nm_s, nm_in, nm_out), *unpack(nv_s, nv_in, nv_out))
```

```python
import functools

import jax
import jax.numpy as jnp
import numpy as np
from jax import lax
from jax.experimental import pallas as pl
from jax.experimental.pallas import tpu as pltpu

F32 = jnp.float32
BF16 = jnp.bfloat16
MESH = pl.DeviceIdType.MESH

D_MODEL = 1024
N_CHIPS = 4
W_QA, W_KA, W_VA, W_GA, W_QB, W_KB, W_VB, W_GB, W_R = 512, 128, 128, 512, 256, 256, 512, 512, 16
O_QA = 0
O_KA = O_QA + W_QA
O_VA = O_KA + W_KA
O_GA = O_VA + W_VA
O_QB = O_GA + W_GA
O_KB = O_QB + W_QB
O_VB = O_KB + W_KB
O_GB = O_VB + W_VB
O_R = O_GB + W_GB
D_PROJ = O_R + W_R
SHARD_IN = D_PROJ // N_CHIPS
SHARD_OUT = D_MODEL // N_CHIPS
SHARD_PAD = 720
ACC_ROWS = -(-((N_CHIPS - 1) * SHARD_IN + SHARD_PAD) // 8) * 8
HALF = D_MODEL // 2
GATHER_SPLIT = 368

HEAD_A = 64
Q_HEADS = 8
KV_HEADS = 2
GROUP = 4
BLOCK = 128
GLA_HEADS = 4
GLA_DK = 64
GLA_DV = 128
CHUNK = 64
GLA_TAU = 16.0
EPS = 1e-5
ALPHA = 2.0 ** 0.25
ATT_SCALE = HEAD_A ** -0.5
GLA_SCALE = GLA_DK ** -0.5

ADAM_LR = 0.001
ADAM_B1 = 0.9
ADAM_B2 = 0.999
ADAM_EPS = 1e-08
ADAM_WD = 0.01
ADAM_STEP = 10

TM = 256
TRI_SLAB = 128
VMEM_LIMIT = 56 * 1024 * 1024

P_LNG, P_LNB, P_BG, P_NW, P_SINK, P_LOSS, P_GU = 0, 8, 16, 18, 19, 20, 24
PACK_ROWS = P_GU + N_CHIPS * 16
PACK_OWN_ROWS = P_GU + 16


def _mm(a, b):
    return jnp.dot(a, b, preferred_element_type=F32)


def _mm_nt(a, b):
    return lax.dot_general(a, b, (((1,), (1,)), ((), ())), preferred_element_type=F32)


def _mm_tn(a, b):
    return lax.dot_general(a, b, (((0,), (0,)), ((), ())), preferred_element_type=F32)


def _split3(a):
    hi = a.astype(BF16)
    r1 = a - hi.astype(F32)
    mid = r1.astype(BF16)
    lo = (r1 - mid.astype(F32)).astype(BF16)
    return hi, mid, lo


def _tri_mm(tri, a):
    slab = tri.shape[0]
    hi, mid, lo = _split3(a)
    return jnp.concatenate(
        [_mm(tri, hi[s:s + slab]) + _mm(tri, mid[s:s + slab]) + _mm(tri, lo[s:s + slab])
         for s in range(0, a.shape[0], slab)], axis=0)


def _chunk_tri(n, upper):
    r = lax.broadcasted_iota(jnp.int32, (n, n), 0)
    c = lax.broadcasted_iota(jnp.int32, (n, n), 1)
    same = (r >> 6) == (c >> 6)
    order = (c >= r) if upper else (c <= r)
    return jnp.where(same & order, 1.0, 0.0).astype(BF16)


def _rope(t, cos, sa, sb):
    w = t.shape[1]
    return t * cos + pltpu.roll(t, w - 8, 1) * sa + pltpu.roll(t, 8, 1) * sb


def _rope_tile(cs):
    row = lax.broadcasted_iota(jnp.int32, (16, 128), 0)
    d = lax.broadcasted_iota(jnp.int32, (16, 128), 1) & (HEAD_A - 1)
    hit = (d & 7) == (row & 7)
    is_cos = row < 8
    lo = d < 8
    hi = (d >= 8) & (d < 16)
    pick_cos = jnp.where(hit & is_cos & (lo | hi), 1.0, 0.0).astype(BF16)
    pick_sa = jnp.where(hit & ~is_cos & lo, -1.0, 0.0).astype(BF16)
    pick_sb = jnp.where(hit & ~is_cos & hi, 1.0, 0.0).astype(BF16)
    pieces = _split3(cs)

    def spread(pick):
        return _mm_tn(pieces[0], pick) + _mm_tn(pieces[1], pick) + _mm_tn(pieces[2], pick)

    d1 = lax.broadcasted_iota(jnp.int32, (1, 128), 1) & (HEAD_A - 1)
    return spread(pick_cos) + jnp.where(d1 < 16, 0.0, 1.0), spread(pick_sa), spread(pick_sb)


def _rope_bwd(d, cos, sa, sb):
    w = d.shape[1]
    return d * cos + pltpu.roll(d * sa, 8, 1) + pltpu.roll(d * sb, w - 8, 1)


def _log_sigmoid(z):
    return jnp.minimum(z, 0.0) - jnp.log1p(jnp.exp(-jnp.abs(z)))


def _sigmoid(z):
    return 1.0 / (1.0 + jnp.exp(-z))


def _attn_bias(has_prev):
    r = lax.broadcasted_iota(jnp.int32, (GROUP * BLOCK, 2 * BLOCK), 0) & (BLOCK - 1)
    k = lax.broadcasted_iota(jnp.int32, (GROUP * BLOCK, 2 * BLOCK), 1)
    first_key = jnp.where(has_prev, 0, BLOCK)
    return jnp.where((k > r) & (k <= r + BLOCK) & (k >= first_key), 0.0, -jnp.inf)


def _sink_col(sinks_ref, j):
    r = lax.broadcasted_iota(jnp.int32, (GROUP * BLOCK, 1), 0) >> 7
    col = jnp.full((GROUP * BLOCK, 1), sinks_ref[GROUP * j], F32)
    for g in range(1, GROUP):
        col = jnp.where(r == g, sinks_ref[GROUP * j + g], col)
    return col


def _stack_heads(t, j):
    return jnp.concatenate([t[:, (GROUP * j + g) * HEAD_A:(GROUP * j + g + 1) * HEAD_A] for g in range(GROUP)], axis=0)


def _unstack_heads(parts):
    return jnp.concatenate([parts[j][g * BLOCK:(g + 1) * BLOCK] for j in range(KV_HEADS) for g in range(GROUP)], axis=1)


def _scores(qs, kc, bias):
    return _mm_nt(qs, kc) * ATT_SCALE + bias


def _softmax_block(qs, kc, bias, sink):
    return _softmax(_scores(qs, kc, bias), sink)


def _softmax(s, sink):
    m = jnp.maximum(jnp.max(s, axis=1, keepdims=True), sink)
    p = jnp.exp(s - m)
    e_sink = jnp.exp(sink - m)
    inv = 1.0 / (jnp.sum(p, axis=1, keepdims=True) + e_sink)
    return p * inv, e_sink * inv


def _fwd_call(x, tgt, cs, win, wout, wgu, bg, sinks, nw, lng, lnb):
    s_len = x.shape[0]
    nt = s_len // TM
    nblk = TM // BLOCK
    nch = TM // CHUNK

    def body(x_ref, t_ref, cs_ref, win_ref, wout_ref, wgu_ref, bg_ref, sinks_ref, nw_ref,
             lng_ref, lnb_ref,
             qa_ref, ka_ref, va_ref, qb_ref, kb_ref, vb_ref, r_ref, dattn_ref, dga_ref, dob_ref, dgb_ref, dh_ref,
             st_ref, dwout_ref, glng_ref, glnb_ref, gnw_ref, loss_ref,
             kprev, vprev, state, attn_s, ga_s, ob_s, gb_s, cat_s):
        i = pl.program_id(0)

        @pl.when(i == 0)
        def _():
            kprev[...] = jnp.zeros_like(kprev)
            vprev[...] = jnp.zeros_like(vprev)
            state[...] = jnp.zeros_like(state)
            dwout_ref[...] = jnp.zeros_like(dwout_ref)
            glng_ref[...] = jnp.zeros_like(glng_ref)
            glnb_ref[...] = jnp.zeros_like(glnb_ref)
            gnw_ref[...] = jnp.zeros_like(gnw_ref)
            loss_ref[...] = jnp.zeros_like(loss_ref)

        x = x_ref[...]
        xb = x.astype(BF16)

        def proj(off, width):
            return _mm_nt(xb, win_ref[off:off + width, :])

        cos, sa, sb = _rope_tile(cs_ref[...])
        cos4, sa4, sb4 = (jnp.concatenate([t] * 4, axis=1) for t in (cos, sa, sb))
        qa = _rope(proj(O_QA, W_QA), cos4, sa4, sb4).astype(BF16)
        ka = _rope(proj(O_KA, W_KA), cos, sa, sb).astype(BF16)
        va = proj(O_VA, W_VA).astype(BF16)
        qa_ref[...] = qa
        ka_ref[...] = ka
        va_ref[...] = va

        r = proj(O_R, W_R)
        r_ref[...] = r
        z = _mm(r.astype(BF16), wgu_ref[...].astype(BF16)) + bg_ref[...]
        log_a = _log_sigmoid(z) / GLA_TAU
        bcum = _tri_mm(_chunk_tri(TRI_SLAB, False), log_a)
        qb = proj(O_QB, W_QB)
        kb = proj(O_KB, W_KB)
        vb = proj(O_VB, W_VB).astype(BF16)
        qb_ref[...] = qb
        kb_ref[...] = kb
        vb_ref[...] = vb
        qd_all = (qb * GLA_SCALE * jnp.exp(bcum)).astype(BF16)
        ki_all = (kb * jnp.exp(-bcum)).astype(BF16)
        tril = lax.broadcasted_iota(jnp.int32, (CHUNK, CHUNK), 0) >= lax.broadcasted_iota(jnp.int32, (CHUNK, CHUNK), 1)
        gla_state = [state[...]]

        def gla_chunk(c):
            rows = slice(c * CHUNK, (c + 1) * CHUNK)
            st = gla_state[0]
            b_c = bcum[rows]
            b_last = b_c[CHUNK - 1:CHUNK]
            ke = (kb[rows] * jnp.exp(b_last - b_c)).astype(BF16)
            st_ref[c] = st
            st16 = st.astype(BF16)
            o_parts, u_parts = [], []
            for h in range(GLA_HEADS):
                ks = slice(h * GLA_DK, (h + 1) * GLA_DK)
                vs = slice(h * GLA_DV, (h + 1) * GLA_DV)
                qd = qd_all[rows, ks]
                v_h = vb[rows, vs]
                a = jnp.where(tril, _mm_nt(qd, ki_all[rows, ks]), 0.0)
                o_parts.append(_mm(a.astype(BF16), v_h) + _mm_nt(qd, st16[:, ks]))
                u_parts.append(_mm_tn(v_h, ke[:, ks]))
            ob_s[rows, :] = jnp.concatenate(o_parts, axis=1)
            gla_state[0] = st * jnp.exp(b_last) + jnp.concatenate(u_parts, axis=1)

        def fill_ga():
            ga_s[...] = proj(O_GA, W_GA)

        def fill_gb():
            gb_s[...] = proj(O_GB, W_GB)

        fillers = [fill_ga, fill_gb]
        chunks = list(range(nch))

        bias_inner = _attn_bias(True)
        sink_cols = [_sink_col(sinks_ref, j) for j in range(KV_HEADS)]
        for b in range(nblk):
            rows = slice(b * BLOCK, (b + 1) * BLOCK)
            mask = _attn_bias(i > 0) if b == 0 else bias_inner
            k_cur = ka[rows]
            v_cur = va[rows]
            k_old = kprev[...] if b == 0 else ka[(b - 1) * BLOCK:b * BLOCK]
            v_old = vprev[...] if b == 0 else va[(b - 1) * BLOCK:b * BLOCK]
            outs = []
            for j in range(KV_HEADS):
                hs = slice(j * HEAD_A, (j + 1) * HEAD_A)
                kc = jnp.concatenate([k_old[:, hs], k_cur[:, hs]], axis=0)
                vc = jnp.concatenate([v_old[:, hs], v_cur[:, hs]], axis=0)
                s = _scores(_stack_heads(qa[rows], j), kc, mask)
                if fillers:
                    fillers.pop(0)()
                probs, _ = _softmax(s, sink_cols[j])
                if chunks:
                    gla_chunk(chunks.pop(0))
                outs.append(_mm(probs.astype(BF16), vc))
            attn_s[rows, :] = _unstack_heads(outs)
        kprev[...] = ka[(nblk - 1) * BLOCK:]
        vprev[...] = va[(nblk - 1) * BLOCK:]
        while fillers:
            fillers.pop(0)()
        while chunks:
            gla_chunk(chunks.pop(0))
        state[...] = gla_state[0]

        ga = ga_s[...]
        sg_a = _sigmoid(ga)
        silu_a = ga * sg_a
        attn = attn_s[...]
        cat_s[:, :W_GA] = (attn * silu_a).astype(BF16)
        gb = gb_s[...]
        sg_b = _sigmoid(gb)
        silu_b = gb * sg_b
        nw = nw_ref[...]
        on_parts = []
        for h in range(GLA_HEADS):
            vs = slice(h * GLA_DV, (h + 1) * GLA_DV)
            o_h = ob_s[:, vs]
            rs = lax.rsqrt(jnp.mean(o_h * o_h, axis=1, keepdims=True) + EPS)
            on_parts.append(o_h * rs * nw)
        on = jnp.concatenate(on_parts, axis=1)
        cat_s[:, W_GA:] = (on * silu_b).astype(BF16)
        cat = cat_s[...]
        hres = ALPHA * x + _mm(cat, wout_ref[...])
        mu = jnp.mean(hres, axis=1, keepdims=True)
        hc = hres - mu
        rstd = lax.rsqrt(jnp.mean(hc * hc, axis=1, keepdims=True) + EPS)
        xhat = hc * rstd
        g_ln = lng_ref[...]
        err = xhat * g_ln + lnb_ref[...] - t_ref[...]
        loss_ref[...] += jnp.sum(err * err) * (0.5 / D_MODEL)
        dy = err * (1.0 / D_MODEL)
        glng_ref[...] += jnp.sum(dy * xhat, axis=0, keepdims=True)
        glnb_ref[...] += jnp.sum(dy, axis=0, keepdims=True)
        dxh = dy * g_ln
        dh = rstd * (dxh - jnp.mean(dxh, axis=1, keepdims=True) - xhat * jnp.mean(dxh * xhat, axis=1, keepdims=True))
        dh_ref[...] = dh
        dh16 = dh.astype(BF16)
        for h in range(2):
            dwout_ref[h] += _mm_tn(cat, dh16[:, h * HALF:(h + 1) * HALF])
        dcat = _mm_nt(dh16, wout_ref[...])

        d_a = dcat[:, :W_GA]
        dattn_ref[...] = (d_a * silu_a).astype(BF16)
        dga_ref[...] = (d_a * attn * (sg_a * (1.0 + ga * (1.0 - sg_a)))).astype(BF16)
        d_b = dcat[:, W_GA:]
        dgb_ref[...] = (d_b * on * (sg_b * (1.0 + gb * (1.0 - sg_b)))).astype(BF16)
        d_on = d_b * silu_b
        gnw = jnp.zeros((1, GLA_DV), F32)
        do_parts = []
        for h in range(GLA_HEADS):
            vs = slice(h * GLA_DV, (h + 1) * GLA_DV)
            o_h = ob_s[:, vs]
            rs = lax.rsqrt(jnp.mean(o_h * o_h, axis=1, keepdims=True) + EPS)
            d_on_h = d_on[:, vs]
            gnw = gnw + jnp.sum(d_on_h * o_h * rs, axis=0, keepdims=True)
            gg = d_on_h * nw
            do_parts.append(rs * gg - o_h * (rs * rs * rs) * jnp.mean(gg * o_h, axis=1, keepdims=True))
        gnw_ref[...] += gnw
        dob_ref[...] = jnp.concatenate(do_parts, axis=1).astype(BF16)

    tile = lambda w: pl.BlockSpec((TM, w), lambda i: (i, 0))
    whole = lambda shape: pl.BlockSpec(shape, lambda i: tuple(0 for _ in shape), pipeline_mode=pl.Buffered(1))
    out_shape = (
        jax.ShapeDtypeStruct((s_len, W_QA), BF16),
        jax.ShapeDtypeStruct((s_len, W_KA), BF16),
        jax.ShapeDtypeStruct((s_len, W_VA), BF16),
        jax.ShapeDtypeStruct((s_len, W_QB), F32),
        jax.ShapeDtypeStruct((s_len, W_KB), F32),
        jax.ShapeDtypeStruct((s_len, W_VB), BF16),
        jax.ShapeDtypeStruct((s_len, W_R), F32),
        jax.ShapeDtypeStruct((s_len, W_GA), BF16),
        jax.ShapeDtypeStruct((s_len, W_GA), BF16),
        jax.ShapeDtypeStruct((s_len, W_GB), BF16),
        jax.ShapeDtypeStruct((s_len, W_GB), BF16),
        jax.ShapeDtypeStruct((s_len, D_MODEL), F32),
        jax.ShapeDtypeStruct((s_len // CHUNK, GLA_DV, GLA_HEADS * GLA_DK), F32),
        jax.ShapeDtypeStruct((2, D_MODEL, HALF), F32),
        jax.ShapeDtypeStruct((1, D_MODEL), F32),
        jax.ShapeDtypeStruct((1, D_MODEL), F32),
        jax.ShapeDtypeStruct((1, GLA_DV), F32),
        jax.ShapeDtypeStruct((1, 128), F32),
    )
    out_specs = (
        tile(W_QA), tile(W_KA), tile(W_VA), tile(W_QB), tile(W_KB), tile(W_VB), tile(W_R),
        tile(W_GA), tile(W_GA), tile(W_GB), tile(W_GB), tile(D_MODEL),
        pl.BlockSpec((nch, GLA_DV, GLA_HEADS * GLA_DK), lambda i: (i, 0, 0)),
        whole((2, D_MODEL, HALF)), whole((1, D_MODEL)), whole((1, D_MODEL)), whole((1, GLA_DV)), whole((1, 128)),
    )
    in_specs = [
        tile(D_MODEL), tile(D_MODEL), pl.BlockSpec((16, TM), lambda i: (0, i)),
        whole((D_PROJ, D_MODEL)), whole((D_MODEL, D_MODEL)), whole((W_R, W_KB)), whole((1, W_KB)),
        pl.BlockSpec(memory_space=pltpu.SMEM), whole((1, GLA_DV)), whole((1, D_MODEL)), whole((1, D_MODEL)),
    ]
    scratch = [
        pltpu.VMEM((BLOCK, W_KA), BF16), pltpu.VMEM((BLOCK, W_VA), BF16),
        pltpu.VMEM((GLA_DV, GLA_HEADS * GLA_DK), F32),
        pltpu.VMEM((TM, W_GA), F32), pltpu.VMEM((TM, W_GA), F32), pltpu.VMEM((TM, W_GB), F32),
        pltpu.VMEM((TM, W_GB), F32), pltpu.VMEM((TM, D_MODEL), BF16),
    ]
    return pl.pallas_call(
        body, name="fwd_head", grid=(nt,), in_specs=in_specs, out_specs=out_specs, out_shape=out_shape,
        scratch_shapes=scratch,
        compiler_params=pltpu.CompilerParams(dimension_semantics=("arbitrary",), vmem_limit_bytes=VMEM_LIMIT),
    )(x, tgt, cs, win, wout, wgu, bg, sinks, nw, lng, lnb)


def _bwd_call(x, dh, qa, ka, va, dattn, dga, dob, dgb, qb, kb, vb, r, st, cs, win, wgu, bg, sinks):
    s_len = x.shape[0]
    nt = s_len // TM
    nblk = TM // BLOCK
    nch = TM // CHUNK

    def body(x_ref, dh_ref, qa_ref, ka_ref, va_ref, kap_ref, vap_ref, dattn_ref, dga_ref, dob_ref, dgb_ref,
             qb_ref, kb_ref, vb_ref, r_ref, st_ref, cs_ref, win_ref, wgu_ref, bg_ref, sinks_ref,
             gx_ref, dwin_ref, gsink_ref, gbg_ref, gwgu_ref,
             dproj, dk_carry, dv_carry, ds_carry, db_s, dwin_acc):
        i = pl.program_id(0)
        t = nt - 1 - i

        @pl.when(i == 0)
        def _():
            dk_carry[...] = jnp.zeros_like(dk_carry)
            dv_carry[...] = jnp.zeros_like(dv_carry)
            ds_carry[...] = jnp.zeros_like(ds_carry)
            dwin_acc[...] = jnp.zeros_like(dwin_acc)
            gsink_ref[...] = jnp.zeros_like(gsink_ref)
            gbg_ref[...] = jnp.zeros_like(gbg_ref)
            gwgu_ref[...] = jnp.zeros_like(gwgu_ref)

        cos, sa, sb = _rope_tile(cs_ref[...])
        cos4, sa4, sb4 = (jnp.concatenate([v] * 4, axis=1) for v in (cos, sa, sb))

        qa = qa_ref[...]
        ka = ka_ref[...]
        va = va_ref[...]
        dattn = dattn_ref[...]
        def attention_backward(between):
            gsink_rows = [jnp.zeros((1, 1), F32) for _ in range(Q_HEADS)]
            bias_inner = _attn_bias(True)
            sink_cols = [_sink_col(sinks_ref, j) for j in range(KV_HEADS)]
            for b in reversed(range(nblk)):
                rows = slice(b * BLOCK, (b + 1) * BLOCK)
                mask = _attn_bias(t > 0) if b == 0 else bias_inner
                k_cur = ka[rows]
                v_cur = va[rows]
                k_old = kap_ref[...] if b == 0 else ka[(b - 1) * BLOCK:b * BLOCK]
                v_old = vap_ref[...] if b == 0 else va[(b - 1) * BLOCK:b * BLOCK]
                dq_parts, dk_parts, dv_parts = [], [], []
                for j in range(KV_HEADS):
                    hs = slice(j * HEAD_A, (j + 1) * HEAD_A)
                    kc = jnp.concatenate([k_old[:, hs], k_cur[:, hs]], axis=0)
                    vc = jnp.concatenate([v_old[:, hs], v_cur[:, hs]], axis=0)
                    qs = _stack_heads(qa[rows], j)
                    do_s = _stack_heads(dattn[rows], j)
                    probs, p_sink = _softmax_block(qs, kc, mask, sink_cols[j])
                    dp = _mm_nt(do_s, vc)
                    d_row = jnp.sum(probs * dp, axis=1, keepdims=True)
                    ds16 = (probs * (dp - d_row) * ATT_SCALE).astype(BF16)
                    between()
                    dq_parts.append(_mm(ds16, kc))
                    dk_parts.append(_mm_tn(ds16, qs))
                    dv_parts.append(_mm_tn(probs.astype(BF16), do_s))
                    t_sink = d_row * p_sink
                    for g in range(GROUP):
                        gsink_rows[GROUP * j + g] = gsink_rows[GROUP * j + g] - jnp.sum(
                            t_sink[g * BLOCK:(g + 1) * BLOCK], axis=0, keepdims=True)
                dq = _rope_bwd(_unstack_heads(dq_parts), cos4[rows], sa4[rows], sb4[rows])
                dproj[rows, O_QA:O_QA + W_QA] = dq.astype(BF16)
                dk_cur = dk_carry[...] + jnp.concatenate([p[BLOCK:] for p in dk_parts], axis=1)
                dv_cur = dv_carry[...] + jnp.concatenate([p[BLOCK:] for p in dv_parts], axis=1)
                dproj[rows, O_KA:O_KA + W_KA] = _rope_bwd(dk_cur, cos[rows], sa[rows], sb[rows]).astype(BF16)
                dproj[rows, O_VA:O_VA + W_VA] = dv_cur.astype(BF16)
                dk_carry[...] = jnp.concatenate([p[:BLOCK] for p in dk_parts], axis=1)
                dv_carry[...] = jnp.concatenate([p[:BLOCK] for p in dv_parts], axis=1)
            for hq in range(Q_HEADS):
                gsink_ref[hq:hq + 1, :] += jnp.broadcast_to(gsink_rows[hq], (1, 128))

        dproj[:, O_GA:O_GA + W_GA] = dga_ref[...]
        dproj[:, O_GB:O_GB + W_GB] = dgb_ref[...]

        r16 = r_ref[...].astype(BF16)
        wgu16 = wgu_ref[...].astype(BF16)
        z = _mm(r16, wgu16) + bg_ref[...]
        log_a = _log_sigmoid(z) / GLA_TAU
        bcum = _tri_mm(_chunk_tri(TRI_SLAB, False), log_a)
        qb = qb_ref[...]
        kb = kb_ref[...]
        vb = vb_ref[...]
        dob = dob_ref[...]
        e_b = jnp.exp(bcum)
        e_nb = jnp.exp(-bcum)
        qd_f = qb * GLA_SCALE * e_b
        ki_f = kb * e_nb
        qd_all = qd_f.astype(BF16)
        ki_all = ki_f.astype(BF16)
        tril = lax.broadcasted_iota(jnp.int32, (CHUNK, CHUNK), 0) >= lax.broadcasted_iota(jnp.int32, (CHUNK, CHUNK), 1)
        last_row = lax.broadcasted_iota(jnp.int32, (CHUNK, 1), 0) == CHUNK - 1

        dsn_state = [ds_carry[...]]

        def gla_chunk(c):
            dsn = dsn_state[0]
            rows = slice(c * CHUNK, (c + 1) * CHUNK)
            b_c = bcum[rows]
            b_last = b_c[CHUNK - 1:CHUNK]
            e_e = jnp.exp(b_last - b_c)
            dec = jnp.exp(b_last)
            ke_f = kb[rows] * e_e
            ke = ke_f.astype(BF16)
            sp = st_ref[c]
            sp16 = sp.astype(BF16)
            dsn16 = dsn.astype(BF16)
            dqd_p, dki_p, dke_p, dv_p, dsp_p = [], [], [], [], []
            for h in range(GLA_HEADS):
                ks = slice(h * GLA_DK, (h + 1) * GLA_DK)
                vs = slice(h * GLA_DV, (h + 1) * GLA_DV)
                qd = qd_all[rows, ks]
                ki = ki_all[rows, ks]
                v_h = vb[rows, vs]
                do_h = dob[rows, vs]
                a16 = jnp.where(tril, _mm_nt(qd, ki), 0.0).astype(BF16)
                da16 = jnp.where(tril, _mm_nt(do_h, v_h), 0.0).astype(BF16)
                dv_p.append(_mm_tn(a16, do_h) + _mm_nt(ke[:, ks], dsn16[:, ks]))
                dqd_p.append(_mm(da16, ki) + _mm(do_h, sp16[:, ks]))
                dki_p.append(_mm_tn(da16, qd))
                dke_p.append(_mm(v_h, dsn16[:, ks]))
                dsp_p.append(_mm_tn(do_h, qd))
            dqd = jnp.concatenate(dqd_p, axis=1)
            dki = jnp.concatenate(dki_p, axis=1)
            dke = jnp.concatenate(dke_p, axis=1)
            ddec = jnp.sum(dsn * sp, axis=0, keepdims=True)
            dsn_next = dsn * dec + jnp.concatenate(dsp_p, axis=1)
            dproj[rows, O_QB:O_QB + W_QB] = (dqd * e_b[rows] * GLA_SCALE).astype(BF16)
            dproj[rows, O_KB:O_KB + W_KB] = (dki * e_nb[rows] + dke * e_e).astype(BF16)
            dproj[rows, O_VB:O_VB + W_VB] = jnp.concatenate(dv_p, axis=1).astype(BF16)
            dke_ke = dke * ke_f
            d_b = dqd * qd_f[rows] - dki * ki_f[rows] - dke_ke
            d_bl = jnp.sum(dke_ke, axis=0, keepdims=True) + ddec * dec
            db_s[rows, :] = d_b + jnp.where(last_row, d_bl, 0.0)
            dsn_state[0] = dsn_next

        chunks = list(reversed(range(nch)))
        attention_backward(lambda: gla_chunk(chunks.pop(0)) if chunks else None)
        while chunks:
            gla_chunk(chunks.pop(0))
        ds_carry[...] = dsn_state[0]
        dlog_a = _tri_mm(_chunk_tri(TRI_SLAB, True), db_s[...])
        dz = dlog_a * (1.0 / GLA_TAU) * _sigmoid(-z)
        dz16 = dz.astype(BF16)
        gbg_ref[...] += jnp.sum(dz, axis=0, keepdims=True)
        gwgu_ref[...] += _mm_tn(r16, dz16)
        dproj[:, O_R:O_R + W_R] = _mm_nt(dz16, wgu16).astype(BF16)

        dp16 = dproj[...]
        gx_ref[...] = ALPHA * dh_ref[...] + _mm(dp16, win_ref[...])
        x16 = x_ref[...].astype(BF16)
        for h in range(2):
            dwin_acc[h, 0:D_PROJ, :] += _mm_tn(dp16, x16[:, h * HALF:(h + 1) * HALF])

        @pl.when(i == nt - 1)
        def _():
            dwin_ref[...] = dwin_acc[...].astype(BF16)

    tile = lambda w: pl.BlockSpec((TM, w), lambda i: (nt - 1 - i, 0))
    whole = lambda shape: pl.BlockSpec(shape, lambda i: tuple(0 for _ in shape), pipeline_mode=pl.Buffered(1))
    prev_blk = pl.BlockSpec((BLOCK, W_KA), lambda i: (jnp.maximum((nt - 1 - i) * nblk - 1, 0), 0))
    in_specs = [
        tile(D_MODEL), tile(D_MODEL), tile(W_QA), tile(W_KA), tile(W_VA), prev_blk, prev_blk,
        tile(W_GA), tile(W_GA), tile(W_GB), tile(W_GB), tile(W_QB), tile(W_KB), tile(W_VB), tile(W_R),
        pl.BlockSpec((nch, GLA_DV, GLA_HEADS * GLA_DK), lambda i: (nt - 1 - i, 0, 0)),
        pl.BlockSpec((16, TM), lambda i: (0, nt - 1 - i)),
        whole((D_PROJ, D_MODEL)), whole((W_R, W_KB)), whole((1, W_KB)), pl.BlockSpec(memory_space=pltpu.SMEM),
    ]
    out_shape = (
        jax.ShapeDtypeStruct((s_len, D_MODEL), F32),
        jax.ShapeDtypeStruct((2, ACC_ROWS, HALF), BF16),
        jax.ShapeDtypeStruct((Q_HEADS, 128), F32),
        jax.ShapeDtypeStruct((1, W_KB), F32),
        jax.ShapeDtypeStruct((W_R, W_KB), F32),
    )
    out_specs = (tile(D_MODEL), whole((2, ACC_ROWS, HALF)), whole((Q_HEADS, 128)), whole((1, W_KB)),
                 whole((W_R, W_KB)))
    scratch = [
        pltpu.VMEM((TM, D_PROJ), BF16), pltpu.VMEM((BLOCK, W_KA), F32), pltpu.VMEM((BLOCK, W_VA), F32),
        pltpu.VMEM((GLA_DV, GLA_HEADS * GLA_DK), F32), pltpu.VMEM((TM, W_KB), F32),
        pltpu.VMEM((2, ACC_ROWS, HALF), F32),
    ]
    return pl.pallas_call(
        body, name="bwd_mix", grid=(nt,), in_specs=in_specs, out_specs=out_specs, out_shape=out_shape,
        scratch_shapes=scratch,
        compiler_params=pltpu.CompilerParams(dimension_semantics=("arbitrary",), vmem_limit_bytes=VMEM_LIMIT),
    )(x, dh, qa, ka, va, ka, va, dattn, dga, dob, dgb, qb, kb, vb, r, st, cs, win, wgu, bg, sinks)


def _mesh_place():
    x, y, c = lax.axis_index("x"), lax.axis_index("y"), lax.axis_index("c")
    chips = [(1 - x, y), (x, 1 - y), (1 - x, 1 - y)]
    return x, y, c, chips


def _gather_weights_call(w_lin, w_out, wgu):
    def body(wlin_ref, wout_ref, wgu_ref, wt_ref, wout_full, wgu_all, blk, oblk, asm, send_sems, recv_sems):
        x, y, c, chips = _mesh_place()
        k_me = 2 * x + y
        asm[SHARD_IN - 4:SHARD_PAD, :] = jnp.zeros((SHARD_PAD - SHARD_IN + 4, D_MODEL), F32)
        asm[0:SHARD_IN, :] = wlin_ref[:, 0, :]
        for h in range(2):
            blk[k_me, h] = asm[0:SHARD_PAD, h * HALF:(h + 1) * HALF].astype(BF16)
            oblk[k_me, h] = wout_ref[:, h * HALF:(h + 1) * HALF].astype(BF16)
        wgu_all[k_me] = wgu_ref[...]

        parts = ((0, GATHER_SPLIT, 0, SHARD_OUT // 2), (GATHER_SPLIT, SHARD_PAD - GATHER_SPLIT, SHARD_OUT // 2, SHARD_OUT // 2))
        me_id, sib_id = (x, y, c), (x, y, 1 - c)
        nbr_x, nbr_y, diag = ((*chip, c) for chip in chips)
        k_x, k_y, k_d = (2 * chip[0] + chip[1] for chip in chips)

        def copies(k, hc, p, sem0, to):
            if p is None:
                refs = (blk.at[k, hc], oblk.at[k, hc])
            else:
                r0, rn, o0, on = parts[p]
                refs = (blk.at[k, hc, pl.ds(r0, rn), :], oblk.at[k, hc, pl.ds(o0, on), :])
            return [pltpu.make_async_remote_copy(src_ref=ref, dst_ref=ref, send_sem=send_sems.at[sem0 + n],
                                                 recv_sem=recv_sems.at[sem0 + n], device_id=to, device_id_type=MESH)
                    for n, ref in enumerate(refs)]

        def gu_copy(k, r, to):
            return pltpu.make_async_remote_copy(src_ref=wgu_all.at[k], dst_ref=wgu_all.at[k], send_sem=send_sems.at[18 + r],
                                                recv_sem=recv_sems.at[18 + r], device_id=to, device_id_type=MESH)

        def start(cps):
            for cp in cps:
                cp.start()
            return cps

        def landed(cps):
            for cp in cps:
                cp.wait_recv()

        started = start(copies(k_me, c, 0, 0, nbr_x) + copies(k_me, c, 1, 6, nbr_y)
                        + copies(k_me, c, 1, 2, nbr_x) + copies(k_me, c, 0, 4, nbr_y)
                        + [gu_copy(k_me, r, to) for r, to in enumerate((nbr_x, nbr_y, diag))])
        landed(copies(k_x, c, 0, 0, me_id))
        started += start(copies(k_x, c, 0, 8, nbr_y))
        landed(copies(k_y, c, 1, 6, me_id))
        started += start(copies(k_y, c, 1, 10, nbr_x))
        landed(copies(k_x, c, 1, 2, me_id))
        started += start(copies(k_x, c, None, 12, sib_id))
        landed(copies(k_y, c, 0, 4, me_id))
        started += start(copies(k_y, c, None, 14, sib_id))
        landed(copies(k_d, c, 0, 8, me_id) + copies(k_d, c, 1, 10, me_id))
        started += start(copies(k_d, c, None, 16, sib_id))
        for r, k_r in enumerate((k_x, k_y, k_d)):
            landed(copies(k_r, 1 - c, None, 12 + 2 * r, me_id))
            gu_copy(k_r, r, me_id).wait_recv()
        for cp in started:
            cp.wait_send()

        for k in range(N_CHIPS):
            for h in range(2):
                asm[k * SHARD_IN:k * SHARD_IN + SHARD_PAD, h * HALF:(h + 1) * HALF] = blk[k, h].astype(F32)
                wout_full[k * SHARD_OUT:(k + 1) * SHARD_OUT, h * HALF:(h + 1) * HALF] = oblk[k, h]
        wt_ref[...] = asm[0:D_PROJ, :].astype(BF16)

    vmem = pl.BlockSpec(memory_space=pltpu.VMEM)
    return pl.pallas_call(
        body, name="gather_weights",
        out_shape=(jax.ShapeDtypeStruct((D_PROJ, D_MODEL), BF16),
                   jax.ShapeDtypeStruct((D_MODEL, D_MODEL), BF16),
                   jax.ShapeDtypeStruct((N_CHIPS, W_R, W_KB // N_CHIPS), F32)),
        in_specs=[vmem, vmem, vmem], out_specs=(vmem, vmem, vmem),
        scratch_shapes=[pltpu.VMEM((N_CHIPS, 2, SHARD_PAD, HALF), BF16), pltpu.VMEM((N_CHIPS, 2, SHARD_OUT, HALF), BF16),
                        pltpu.VMEM((ACC_ROWS, D_MODEL), F32),
                        pltpu.SemaphoreType.DMA((21,)), pltpu.SemaphoreType.DMA((21,))],
        compiler_params=pltpu.CompilerParams(vmem_limit_bytes=VMEM_LIMIT),
    )(w_lin, w_out, wgu)


def _adamw(w, g, m, v):
    m = ADAM_B1 * m + (1.0 - ADAM_B1) * g
    v = ADAM_B2 * v + (1.0 - ADAM_B2) * (g * g)
    m_hat = m / (1.0 - ADAM_B1 ** ADAM_STEP)
    v_hat = v / (1.0 - ADAM_B2 ** ADAM_STEP)
    delta = -ADAM_LR * (m_hat / (jnp.sqrt(v_hat) + ADAM_EPS) + ADAM_WD * w)
    return delta, m, v


N_SMALL = 6


def _reduce_grads_call(g_in, g_out, small_grads):
    def body(gin_hbm, gout_hbm, g_lng, g_lnb, g_bg, g_nw, g_sink, g_wgu, loss_in, lin_in, fin_out, tot_out,
             a_in, a_out, b_in, b_out, ab_in, c_in, s_in, s_out, r_in, r_out, f_in, f_out, pack_ref, tot_ref, pack_all,
             send_sems, recv_sems, local_sems):
        x, y, c, chips = _mesh_place()
        k_me = 2 * x + y
        me = 4 * x + 2 * y + c
        sibling = (x, y, 1 - c)

        pack_ref[...] = jnp.zeros_like(pack_ref)
        for a in range(8):
            pack_ref[P_LNG + a:P_LNG + a + 1, :] = g_lng[:, a * 128:(a + 1) * 128]
            pack_ref[P_LNB + a:P_LNB + a + 1, :] = g_lnb[:, a * 128:(a + 1) * 128]
        for a in range(2):
            pack_ref[P_BG + a:P_BG + a + 1, :] = g_bg[:, a * 128:(a + 1) * 128]
        pack_ref[P_NW:P_NW + 1, :] = g_nw[...]
        lane = lax.broadcasted_iota(jnp.int32, (1, 128), 1)
        sink_row = jnp.zeros((1, 128), F32)
        for hq in range(Q_HEADS):
            sink_row = jnp.where(lane == hq, g_sink[hq:hq + 1, :], sink_row)
        pack_ref[P_SINK:P_SINK + 1, :] = sink_row
        pack_ref[P_LOSS:P_LOSS + 1, :] = loss_in[...]
        gu_w = W_KB // N_CHIPS
        for k in range(N_CHIPS):
            pack_ref[P_GU + W_R * k:P_GU + W_R * (k + 1), 0:gu_w] = g_wgu[:, k * gu_w:(k + 1) * gu_w]
        pack_all[me] = pack_ref[...]
        small = []
        for mask in range(1, 8):
            peer = (x ^ (mask >> 2), y ^ ((mask >> 1) & 1), c ^ (mask & 1))
            small.append(pltpu.make_async_remote_copy(
                src_ref=pack_ref, dst_ref=pack_all.at[me], send_sem=send_sems.at[mask], recv_sem=recv_sems.at[mask],
                device_id=peer, device_id_type=MESH))
        for cp in small:
            cp.start()

        loads = [pltpu.make_async_copy(gin_hbm.at[c], a_in, local_sems.at[0]),
                 pltpu.make_async_copy(gout_hbm.at[c], a_out, local_sems.at[1])]
        to_sib = [pltpu.make_async_remote_copy(
                      src_ref=gin_hbm.at[1 - c], dst_ref=b_in,
                      send_sem=send_sems.at[8], recv_sem=recv_sems.at[8], device_id=sibling, device_id_type=MESH),
                  pltpu.make_async_remote_copy(
                      src_ref=gout_hbm.at[1 - c], dst_ref=b_out,
                      send_sem=send_sems.at[9], recv_sem=recv_sems.at[9], device_id=sibling, device_id_type=MESH)]
        for cp in (loads[1], to_sib[1], loads[0], to_sib[0]):
            cp.start()

        parts = ((0, GATHER_SPLIT, 0, SHARD_OUT // 2), (GATHER_SPLIT, SHARD_PAD - GATHER_SPLIT, SHARD_OUT // 2, SHARD_OUT // 2))
        nbr_x, nbr_y, _ = ((*chip, c) for chip in chips)
        k_x, k_y, k_d = (2 * chip[0] + chip[1] for chip in chips)

        def rows_of(p, w):
            r0, rn, o0, on = parts[p]
            return (o0, on) if w else (r0, rn)

        def mine(k, p, w):
            r0, rn = rows_of(p, w)
            if w:
                return a_out[k, pl.ds(r0, rn), :] + b_out[k, pl.ds(r0, rn), :]
            return c_in[k, pl.ds(r0, rn), :]

        def message(m, p, w, to):
            rn = rows_of(p, w)[1]
            stage, land = (s_out, r_out) if w else (s_in, r_in)
            return pltpu.make_async_remote_copy(
                src_ref=stage.at[m, pl.ds(0, rn), :], dst_ref=land.at[m, pl.ds(0, rn), :],
                send_sem=send_sems.at[10 + 2 * m + w], recv_sem=recv_sems.at[10 + 2 * m + w],
                device_id=to, device_id_type=MESH)

        def post(m, p, w, val, to):
            rn = rows_of(p, w)[1]
            stage = s_out if w else s_in
            stage[m, 0:rn, :] = val.astype(BF16)
            cp = message(m, p, w, to)
            cp.start()
            return [cp]

        def take(m, p, w):
            rn = rows_of(p, w)[1]
            message(m, p, w, (x, y, c)).wait_recv()
            land = r_out if w else r_in
            return land[m, 0:rn, :].astype(F32)

        sent = []
        for w in (1, 0):
            loads[w].wait()
            to_sib[w].wait_recv()
            if w == 0:
                ab_in[...] = a_in[...].astype(F32) + b_in[...].astype(F32)
                for k in range(N_CHIPS):
                    c_in[k] = ab_in[k * SHARD_IN:k * SHARD_IN + SHARD_PAD, :]
            sent += post(1, 0, w, mine(k_d, 0, w), nbr_x) + post(4, 1, w, mine(k_d, 1, w), nbr_y)
            sent += post(0, 0, w, mine(k_x, 0, w), nbr_x) + post(3, 1, w, mine(k_y, 1, w), nbr_y)
        for w in (1, 0):
            sent += post(5, 0, w, take(1, 0, w) + mine(k_y, 0, w), nbr_y)
            sent += post(2, 1, w, take(4, 1, w) + mine(k_x, 1, w), nbr_x)
        for w in (1, 0):
            for p, direct, summed in ((0, 0, 5), (1, 3, 2)):
                r0, rn = rows_of(p, w)
                total = mine(k_me, p, w) + take(direct, p, w) + take(summed, p, w)
                if w:
                    f_out[c, r0:r0 + rn, :] = total
                else:
                    f_in[c, r0:r0 + rn, :] = total

        swap = [pltpu.make_async_remote_copy(
                    src_ref=f_in.at[c], dst_ref=f_in.at[c],
                    send_sem=send_sems.at[22], recv_sem=recv_sems.at[22], device_id=sibling, device_id_type=MESH),
                pltpu.make_async_remote_copy(
                    src_ref=f_out.at[c], dst_ref=f_out.at[c],
                    send_sem=send_sems.at[23], recv_sem=recv_sems.at[23], device_id=sibling, device_id_type=MESH)]
        for cp in swap:
            cp.start()

        for cp in small:
            cp.wait_recv()
        total = pack_all[0]
        for d in range(1, 8):
            total = total + pack_all[d]
        tot_ref[...] = total
        tot_out[0:P_GU, :] = total[0:P_GU]
        tot_out[P_GU:PACK_OWN_ROWS, :] = tot_ref[pl.ds(pl.multiple_of(P_GU + W_R * k_me, 8), W_R), :]

        other_in = pltpu.make_async_remote_copy(
            src_ref=f_in.at[1 - c], dst_ref=f_in.at[1 - c],
            send_sem=send_sems.at[22], recv_sem=recv_sems.at[22], device_id=sibling, device_id_type=MESH)
        other_out = pltpu.make_async_remote_copy(
            src_ref=f_out.at[1 - c], dst_ref=f_out.at[1 - c],
            send_sem=send_sems.at[23], recv_sem=recv_sems.at[23], device_id=sibling, device_id_type=MESH)
        other_in.wait_recv()
        other_out.wait_recv()
        for cp in small + to_sib + sent + swap:
            cp.wait_send()

        for h in range(2):
            lin_in[:, h * HALF:(h + 1) * HALF] = f_in[h, 0:SHARD_IN, :]
            fin_out[:, h * HALF:(h + 1) * HALF] = f_out[h]

    vmem = pl.BlockSpec(memory_space=pltpu.VMEM)
    hbm = pl.BlockSpec(memory_space=pl.ANY)
    return pl.pallas_call(
        body, name="reduce_grads",
        out_shape=(jax.ShapeDtypeStruct((SHARD_IN, D_MODEL), F32), jax.ShapeDtypeStruct((SHARD_OUT, D_MODEL), F32),
                   jax.ShapeDtypeStruct((PACK_OWN_ROWS, 128), F32)),
        in_specs=[hbm, hbm] + [vmem] * 7, out_specs=(vmem,) * 3,
        scratch_shapes=[
            pltpu.VMEM((ACC_ROWS, HALF), BF16), pltpu.VMEM((N_CHIPS, SHARD_OUT, HALF), F32),
            pltpu.VMEM((ACC_ROWS, HALF), BF16), pltpu.VMEM((N_CHIPS, SHARD_OUT, HALF), F32),
            pltpu.VMEM((ACC_ROWS, HALF), F32),
            pltpu.VMEM((N_CHIPS, SHARD_PAD, HALF), F32),
            pltpu.VMEM((6, GATHER_SPLIT, HALF), BF16), pltpu.VMEM((6, SHARD_OUT // 2, HALF), BF16),
            pltpu.VMEM((6, GATHER_SPLIT, HALF), BF16), pltpu.VMEM((6, SHARD_OUT // 2, HALF), BF16),
            pltpu.VMEM((2, SHARD_PAD, HALF), F32), pltpu.VMEM((2, SHARD_OUT, HALF), F32),
            pltpu.VMEM((PACK_ROWS, 128), F32), pltpu.VMEM((PACK_ROWS, 128), F32), pltpu.VMEM((8, PACK_ROWS, 128), F32),
            pltpu.SemaphoreType.DMA((24,)), pltpu.SemaphoreType.DMA((24,)), pltpu.SemaphoreType.DMA((2,)),
        ],
        compiler_params=pltpu.CompilerParams(vmem_limit_bytes=VMEM_LIMIT),
    )(g_in, g_out, *small_grads)


def _adamw_call(g_in, w_in, m_in, v_in, g_out, w_out, m_out, v_out, tot, small_params):
    steps = 4
    rows_out = SHARD_OUT // steps
    cols = D_MODEL // steps
    gu_w = W_KB // N_CHIPS

    def body(gi, wi, mi, vi, go, wo, mo, vo, tot, *rest):
        params = rest[:3 * N_SMALL]
        gi_o, di, nmi, nvi, go_o, do, nmo, nvo, loss_out = rest[3 * N_SMALL:3 * N_SMALL + 9]
        small_out = rest[3 * N_SMALL + 9:]

        @pl.when(pl.program_id(0) == 0)
        def _():
            loss_out[...] = tot[P_LOSS:P_LOSS + 1, :]
            g_outs = small_out[0:N_SMALL]
            for a in range(8):
                g_outs[0][:, a * 128:(a + 1) * 128] = tot[P_LNG + a:P_LNG + a + 1, :]
                g_outs[1][:, a * 128:(a + 1) * 128] = tot[P_LNB + a:P_LNB + a + 1, :]
            for a in range(2):
                g_outs[2][:, a * 128:(a + 1) * 128] = tot[P_BG + a:P_BG + a + 1, :]
            g_outs[3][...] = tot[P_NW:P_NW + 1, :]
            g_outs[4][...] = tot[P_SINK:P_SINK + 1, 0:Q_HEADS]
            g_outs[5][...] = tot[P_GU:PACK_OWN_ROWS, 0:gu_w]
            for n in range(N_SMALL):
                w_ref, m_ref, v_ref = params[3 * n:3 * n + 3]
                delta, new_m, new_v = _adamw(w_ref[...], g_outs[n][...], m_ref[...], v_ref[...])
                small_out[N_SMALL + n][...] = delta
                small_out[2 * N_SMALL + n][...] = new_m
                small_out[3 * N_SMALL + n][...] = new_v

        g = gi[...]
        delta, new_m, new_v = _adamw(wi[:, 0, :], g, mi[:, 0, :], vi[:, 0, :])
        gi_o[:, 0, :] = g
        di[:, 0, :] = delta
        nmi[:, 0, :] = new_m
        nvi[:, 0, :] = new_v
        g = go[...]
        go_o[...] = g
        do[...], nmo[...], nvo[...] = _adamw(wo[...], g, mo[...], vo[...])

    t_g = pl.BlockSpec((SHARD_IN, cols), lambda i: (0, i))
    t_in = pl.BlockSpec((SHARD_IN, 1, cols), lambda i: (0, 0, i))
    t_out = pl.BlockSpec((rows_out, D_MODEL), lambda i: (i, 0))
    s_in = jax.ShapeDtypeStruct((SHARD_IN, 1, D_MODEL), F32)
    s_out = jax.ShapeDtypeStruct((SHARD_OUT, D_MODEL), F32)
    whole = lambda shape: pl.BlockSpec(shape, lambda i: tuple(0 for _ in shape))
    small_specs = [whole(p.shape) for p in small_params]
    small_shapes = [jax.ShapeDtypeStruct(p.shape, F32) for p in small_params[0::3]] * 4
    return pl.pallas_call(
        body, name="adamw", grid=(steps,),
        in_specs=[t_g] + [t_in] * 3 + [t_out] * 4 + [whole(tot.shape)] + small_specs,
        out_specs=(t_in,) * 4 + (t_out,) * 4 + (whole((1, 128)),) + tuple(small_specs[0::3] * 4),
        out_shape=(s_in,) * 4 + (s_out,) * 4 + (jax.ShapeDtypeStruct((1, 128), F32),) + tuple(small_shapes),
        compiler_params=pltpu.CompilerParams(dimension_semantics=("arbitrary",)),
    )(g_in, w_in, m_in, v_in, g_out, w_out, m_out, v_out, tot, *small_params)


def _rope_tables(positions):
    half = 8
    inv_freq = 500000.0 ** (-jnp.arange(half, dtype=F32) / half)
    ang = inv_freq[:, None] * positions.astype(F32)[None, :]
    return jnp.concatenate([jnp.cos(ang), jnp.sin(ang)], axis=0)


def kernel(x, positions, w_in, gla_w_gate_up, gla_b_gate, attn_sinks, gla_norm_w, w_out, ln_g, ln_b, loss_target, m_w_in, m_gla_w_gate_up, m_gla_b_gate, m_attn_sinks, m_gla_norm_w, m_w_out, m_ln_g, m_ln_b, v_w_in, v_gla_w_gate_up, v_gla_b_gate, v_attn_sinks, v_gla_norm_w, v_w_out, v_ln_g, v_ln_b):
    def lin3(w):
        return jnp.transpose(w, (2, 0, 1))

    def unlin(w):
        return jnp.transpose(w, (1, 2, 0))

    win, wout, wgu_all = _gather_weights_call(lin3(w_in), w_out[0], gla_w_gate_up[0])
    wgu = jnp.transpose(wgu_all, (1, 0, 2)).reshape(W_R, W_KB)
    cs = _rope_tables(positions[0])
    sinks = attn_sinks[0]

    (qa, ka, va, qb, kb, vb, r, dattn, dga, dob, dgb, dh, st, g_wout, g_lng, g_lnb, g_nw, loss) = _fwd_call(
        x[0], loss_target[0], cs, win, wout, wgu, gla_b_gate, sinks, gla_norm_w, ln_g, ln_b)
    gx, g_win, g_sink, g_bg, g_wgu = _bwd_call(
        x[0], dh, qa, ka, va, dattn, dga, dob, dgb, qb, kb, vb, r, st, cs, win, wgu, gla_b_gate, sinks)

    g_wout_by_chip = g_wout.reshape(2, N_CHIPS, SHARD_OUT, HALF)
    small_params = []
    for group in ((ln_g, m_ln_g, v_ln_g), (ln_b, m_ln_b, v_ln_b), (gla_b_gate, m_gla_b_gate, v_gla_b_gate),
                  (gla_norm_w, m_gla_norm_w, v_gla_norm_w), (attn_sinks, m_attn_sinks, v_attn_sinks)):
        small_params += list(group)
    small_params += [gla_w_gate_up[0], m_gla_w_gate_up[0], v_gla_w_gate_up[0]]
    fin_in, fin_out, tot = _reduce_grads_call(g_win, g_wout_by_chip, (g_lng, g_lnb, g_bg, g_nw, g_sink, g_wgu, loss))
    fin_in, d_in, nm_in, nv_in, fin_out, d_out, nm_out, nv_out, loss_sum, *small_out = _adamw_call(
        fin_in, lin3(w_in), lin3(m_w_in), lin3(v_w_in), fin_out, w_out[0], m_w_out[0], v_w_out[0], tot, small_params)

    def unpack(kind, big_in, big_out):
        lng_, lnb_, bg_, nw_, sink_, gu_ = small_out[kind * N_SMALL:(kind + 1) * N_SMALL]
        return (unlin(big_in), gu_[None], bg_, sink_, nw_, big_out[None], lng_, lnb_)

    loss_total = loss_sum[0, 0]
    g_s, d_s, nm_s, nv_s = 0, 1, 2, 3
    return (loss_total, gx[None], *unpack(g_s, fin_in, fin_out), *unpack(d_s, d_in, d_out),
            *unpack(nm_s, nm_in, nm_out), *unpack(nv_s, nv_in, nv_out))
```

```python
import functools

import jax
import jax.numpy as jnp
import numpy as np
from jax import lax
from jax.experimental import pallas as pl
from jax.experimental.pallas import tpu as pltpu

F32 = jnp.float32
BF16 = jnp.bfloat16
MESH = pl.DeviceIdType.MESH

D_MODEL = 1024
N_CHIPS = 4
W_QA, W_KA, W_VA, W_GA, W_QB, W_KB, W_VB, W_GB, W_R = 512, 128, 128, 512, 256, 256, 512, 512, 16
O_QA = 0
O_KA = O_QA + W_QA
O_VA = O_KA + W_KA
O_GA = O_VA + W_VA
O_QB = O_GA + W_GA
O_KB = O_QB + W_QB
O_VB = O_KB + W_KB
O_GB = O_VB + W_VB
O_R = O_GB + W_GB
D_PROJ = O_R + W_R
SHARD_IN = D_PROJ // N_CHIPS
SHARD_OUT = D_MODEL // N_CHIPS
SHARD_PAD = 720
ACC_ROWS = -(-((N_CHIPS - 1) * SHARD_IN + SHARD_PAD) // 8) * 8
HALF = D_MODEL // 2
GATHER_SPLIT = 368

HEAD_A = 64
Q_HEADS = 8
KV_HEADS = 2
GROUP = 4
BLOCK = 128
GLA_HEADS = 4
GLA_DK = 64
GLA_DV = 128
CHUNK = 64
GLA_TAU = 16.0
EPS = 1e-5
ALPHA = 2.0 ** 0.25
ATT_SCALE = HEAD_A ** -0.5
GLA_SCALE = GLA_DK ** -0.5

ADAM_LR = 0.001
ADAM_B1 = 0.9
ADAM_B2 = 0.999
ADAM_EPS = 1e-08
ADAM_WD = 0.01
ADAM_STEP = 10

TM = 256
TM_FWD = 512
TRI_SLAB = 128
VMEM_LIMIT = 56 * 1024 * 1024

P_LNG, P_LNB, P_BG, P_NW, P_SINK, P_LOSS, P_GU = 0, 8, 16, 18, 19, 20, 24
PACK_ROWS = P_GU + N_CHIPS * 16
PACK_OWN_ROWS = P_GU + 16


def _mm(a, b):
    return jnp.dot(a, b, preferred_element_type=F32)


def _mm_nt(a, b):
    return lax.dot_general(a, b, (((1,), (1,)), ((), ())), preferred_element_type=F32)


def _mm_tn(a, b):
    return lax.dot_general(a, b, (((0,), (0,)), ((), ())), preferred_element_type=F32)


def _split3(a):
    hi = a.astype(BF16)
    r1 = a - hi.astype(F32)
    mid = r1.astype(BF16)
    lo = (r1 - mid.astype(F32)).astype(BF16)
    return hi, mid, lo


def _tri_mm(tri, a):
    slab = tri.shape[0]
    hi, mid, lo = _split3(a)
    return jnp.concatenate(
        [_mm(tri, hi[s:s + slab]) + _mm(tri, mid[s:s + slab]) + _mm(tri, lo[s:s + slab])
         for s in range(0, a.shape[0], slab)], axis=0)


def _chunk_tri(n, upper):
    r = lax.broadcasted_iota(jnp.int32, (n, n), 0)
    c = lax.broadcasted_iota(jnp.int32, (n, n), 1)
    same = (r >> 6) == (c >> 6)
    order = (c >= r) if upper else (c <= r)
    return jnp.where(same & order, 1.0, 0.0).astype(BF16)


def _rope(t, cos, sa, sb):
    w = t.shape[1]
    return t * cos + pltpu.roll(t, w - 8, 1) * sa + pltpu.roll(t, 8, 1) * sb


def _rope_tile(cs):
    row = lax.broadcasted_iota(jnp.int32, (16, 128), 0)
    d = lax.broadcasted_iota(jnp.int32, (16, 128), 1) & (HEAD_A - 1)
    hit = (d & 7) == (row & 7)
    is_cos = row < 8
    lo = d < 8
    hi = (d >= 8) & (d < 16)
    pick_cos = jnp.where(hit & is_cos & (lo | hi), 1.0, 0.0).astype(BF16)
    pick_sa = jnp.where(hit & ~is_cos & lo, -1.0, 0.0).astype(BF16)
    pick_sb = jnp.where(hit & ~is_cos & hi, 1.0, 0.0).astype(BF16)
    pieces = _split3(cs)

    def spread(pick):
        return _mm_tn(pieces[0], pick) + _mm_tn(pieces[1], pick) + _mm_tn(pieces[2], pick)

    d1 = lax.broadcasted_iota(jnp.int32, (1, 128), 1) & (HEAD_A - 1)
    return spread(pick_cos) + jnp.where(d1 < 16, 0.0, 1.0), spread(pick_sa), spread(pick_sb)


def _rope_bwd(d, cos, sa, sb):
    w = d.shape[1]
    return d * cos + pltpu.roll(d * sa, 8, 1) + pltpu.roll(d * sb, w - 8, 1)


def _log_sigmoid(z):
    return jnp.minimum(z, 0.0) - jnp.log1p(jnp.exp(-jnp.abs(z)))


def _sigmoid(z):
    return 1.0 / (1.0 + jnp.exp(-z))


def _attn_bias(has_prev):
    r = lax.broadcasted_iota(jnp.int32, (GROUP * BLOCK, 2 * BLOCK), 0) & (BLOCK - 1)
    k = lax.broadcasted_iota(jnp.int32, (GROUP * BLOCK, 2 * BLOCK), 1)
    first_key = jnp.where(has_prev, 0, BLOCK)
    return jnp.where((k > r) & (k <= r + BLOCK) & (k >= first_key), 0.0, -jnp.inf)


def _sink_col(sinks_ref, j):
    r = lax.broadcasted_iota(jnp.int32, (GROUP * BLOCK, 1), 0) >> 7
    col = jnp.full((GROUP * BLOCK, 1), sinks_ref[GROUP * j], F32)
    for g in range(1, GROUP):
        col = jnp.where(r == g, sinks_ref[GROUP * j + g], col)
    return col


def _stack_heads(t, j):
    return jnp.concatenate([t[:, (GROUP * j + g) * HEAD_A:(GROUP * j + g + 1) * HEAD_A] for g in range(GROUP)], axis=0)


def _unstack_heads(parts):
    return jnp.concatenate([parts[j][g * BLOCK:(g + 1) * BLOCK] for j in range(KV_HEADS) for g in range(GROUP)], axis=1)


def _scores(qs, kc, bias):
    return _mm_nt(qs, kc) * ATT_SCALE + bias


def _softmax_block(qs, kc, bias, sink):
    return _softmax(_scores(qs, kc, bias), sink)


def _softmax(s, sink):
    m = jnp.maximum(jnp.max(s, axis=1, keepdims=True), sink)
    p = jnp.exp(s - m)
    e_sink = jnp.exp(sink - m)
    inv = 1.0 / (jnp.sum(p, axis=1, keepdims=True) + e_sink)
    return p * inv, e_sink * inv


def _fwd_call(x, tgt, cs, win, wout, wgu, bg, sinks, nw, lng, lnb):
    s_len = x.shape[0]
    TM = min(TM_FWD, s_len)
    nt = s_len // TM
    nblk = TM // BLOCK
    nch = TM // CHUNK

    def body(x_ref, t_ref, cs_ref, win_ref, wout_ref, wgu_ref, bg_ref, sinks_ref, nw_ref,
             lng_ref, lnb_ref,
             qa_ref, ka_ref, va_ref, qb_ref, kb_ref, vb_ref, r_ref, dattn_ref, dga_ref, dob_ref, dgb_ref, dh_ref,
             st_ref, dwout_ref, glng_ref, glnb_ref, gnw_ref, loss_ref,
             kprev, vprev, state, attn_s, ga_s, ob_s, gb_s, cat_s):
        i = pl.program_id(0)

        @pl.when(i == 0)
        def _():
            kprev[...] = jnp.zeros_like(kprev)
            vprev[...] = jnp.zeros_like(vprev)
            state[...] = jnp.zeros_like(state)
            dwout_ref[...] = jnp.zeros_like(dwout_ref)
            glng_ref[...] = jnp.zeros_like(glng_ref)
            glnb_ref[...] = jnp.zeros_like(glnb_ref)
            gnw_ref[...] = jnp.zeros_like(gnw_ref)
            loss_ref[...] = jnp.zeros_like(loss_ref)

        x = x_ref[...]
        xb = x.astype(BF16)

        def proj(off, width):
            return _mm_nt(xb, win_ref[off:off + width, :])

        cos, sa, sb = _rope_tile(cs_ref[...])
        cos4, sa4, sb4 = (jnp.concatenate([t] * 4, axis=1) for t in (cos, sa, sb))
        qa = _rope(proj(O_QA, W_QA), cos4, sa4, sb4).astype(BF16)
        ka = _rope(proj(O_KA, W_KA), cos, sa, sb).astype(BF16)
        va = proj(O_VA, W_VA).astype(BF16)
        qa_ref[...] = qa
        ka_ref[...] = ka
        va_ref[...] = va

        r = proj(O_R, W_R)
        r_ref[...] = r
        z = _mm(r.astype(BF16), wgu_ref[...].astype(BF16)) + bg_ref[...]
        log_a = _log_sigmoid(z) / GLA_TAU
        bcum = _tri_mm(_chunk_tri(TRI_SLAB, False), log_a)
        qb = proj(O_QB, W_QB)
        kb = proj(O_KB, W_KB)
        vb = proj(O_VB, W_VB).astype(BF16)
        qb_ref[...] = qb
        kb_ref[...] = kb
        vb_ref[...] = vb
        qd_all = (qb * GLA_SCALE * jnp.exp(bcum)).astype(BF16)
        ki_all = (kb * jnp.exp(-bcum)).astype(BF16)
        tril = lax.broadcasted_iota(jnp.int32, (CHUNK, CHUNK), 0) >= lax.broadcasted_iota(jnp.int32, (CHUNK, CHUNK), 1)
        gla_state = [state[...]]

        def gla_chunk(c):
            rows = slice(c * CHUNK, (c + 1) * CHUNK)
            st = gla_state[0]
            b_c = bcum[rows]
            b_last = b_c[CHUNK - 1:CHUNK]
            ke = (kb[rows] * jnp.exp(b_last - b_c)).astype(BF16)
            st_ref[c] = st
            st16 = st.astype(BF16)
            o_parts, u_parts = [], []
            for h in range(GLA_HEADS):
                ks = slice(h * GLA_DK, (h + 1) * GLA_DK)
                vs = slice(h * GLA_DV, (h + 1) * GLA_DV)
                qd = qd_all[rows, ks]
                v_h = vb[rows, vs]
                a = jnp.where(tril, _mm_nt(qd, ki_all[rows, ks]), 0.0)
                o_parts.append(_mm(a.astype(BF16), v_h) + _mm_nt(qd, st16[:, ks]))
                u_parts.append(_mm_tn(v_h, ke[:, ks]))
            ob_s[rows, :] = jnp.concatenate(o_parts, axis=1)
            gla_state[0] = st * jnp.exp(b_last) + jnp.concatenate(u_parts, axis=1)

        def fill_ga():
            ga_s[...] = proj(O_GA, W_GA)

        def fill_gb():
            gb_s[...] = proj(O_GB, W_GB)

        fillers = [fill_ga, fill_gb]
        chunks = list(range(nch))

        bias_inner = _attn_bias(True)
        sink_cols = [_sink_col(sinks_ref, j) for j in range(KV_HEADS)]
        for b in range(nblk):
            rows = slice(b * BLOCK, (b + 1) * BLOCK)
            mask = _attn_bias(i > 0) if b == 0 else bias_inner
            k_cur = ka[rows]
            v_cur = va[rows]
            k_old = kprev[...] if b == 0 else ka[(b - 1) * BLOCK:b * BLOCK]
            v_old = vprev[...] if b == 0 else va[(b - 1) * BLOCK:b * BLOCK]
            outs = []
            for j in range(KV_HEADS):
                hs = slice(j * HEAD_A, (j + 1) * HEAD_A)
                kc = jnp.concatenate([k_old[:, hs], k_cur[:, hs]], axis=0)
                vc = jnp.concatenate([v_old[:, hs], v_cur[:, hs]], axis=0)
                s = _scores(_stack_heads(qa[rows], j), kc, mask)
                if fillers:
                    fillers.pop(0)()
                probs, _ = _softmax(s, sink_cols[j])
                if chunks:
                    gla_chunk(chunks.pop(0))
                outs.append(_mm(probs.astype(BF16), vc))
            attn_s[rows, :] = _unstack_heads(outs)
        kprev[...] = ka[(nblk - 1) * BLOCK:]
        vprev[...] = va[(nblk - 1) * BLOCK:]
        while fillers:
            fillers.pop(0)()
        while chunks:
            gla_chunk(chunks.pop(0))
        state[...] = gla_state[0]

        ga = ga_s[...]
        sg_a = _sigmoid(ga)
        silu_a = ga * sg_a
        attn = attn_s[...]
        cat_s[:, :W_GA] = (attn * silu_a).astype(BF16)
        gb = gb_s[...]
        sg_b = _sigmoid(gb)
        silu_b = gb * sg_b
        nw = nw_ref[...]
        on_parts = []
        for h in range(GLA_HEADS):
            vs = slice(h * GLA_DV, (h + 1) * GLA_DV)
            o_h = ob_s[:, vs]
            rs = lax.rsqrt(jnp.mean(o_h * o_h, axis=1, keepdims=True) + EPS)
            on_parts.append(o_h * rs * nw)
        on = jnp.concatenate(on_parts, axis=1)
        cat_s[:, W_GA:] = (on * silu_b).astype(BF16)
        cat = cat_s[...]
        hres = ALPHA * x + _mm(cat, wout_ref[...])
        mu = jnp.mean(hres, axis=1, keepdims=True)
        hc = hres - mu
        rstd = lax.rsqrt(jnp.mean(hc * hc, axis=1, keepdims=True) + EPS)
        xhat = hc * rstd
        g_ln = lng_ref[...]
        err = xhat * g_ln + lnb_ref[...] - t_ref[...]
        loss_ref[...] += jnp.sum(err * err) * (0.5 / D_MODEL)
        dy = err * (1.0 / D_MODEL)
        glng_ref[...] += jnp.sum(dy * xhat, axis=0, keepdims=True)
        glnb_ref[...] += jnp.sum(dy, axis=0, keepdims=True)
        dxh = dy * g_ln
        dh = rstd * (dxh - jnp.mean(dxh, axis=1, keepdims=True) - xhat * jnp.mean(dxh * xhat, axis=1, keepdims=True))
        dh_ref[...] = dh
        dh16 = dh.astype(BF16)
        for h in range(2):
            dwout_ref[h] += _mm_tn(cat, dh16[:, h * HALF:(h + 1) * HALF])
        dcat = _mm_nt(dh16, wout_ref[...])

        d_a = dcat[:, :W_GA]
        dattn_ref[...] = (d_a * silu_a).astype(BF16)
        dga_ref[...] = (d_a * attn * (sg_a * (1.0 + ga * (1.0 - sg_a)))).astype(BF16)
        d_b = dcat[:, W_GA:]
        dgb_ref[...] = (d_b * on * (sg_b * (1.0 + gb * (1.0 - sg_b)))).astype(BF16)
        d_on = d_b * silu_b
        gnw = jnp.zeros((1, GLA_DV), F32)
        do_parts = []
        for h in range(GLA_HEADS):
            vs = slice(h * GLA_DV, (h + 1) * GLA_DV)
            o_h = ob_s[:, vs]
            rs = lax.rsqrt(jnp.mean(o_h * o_h, axis=1, keepdims=True) + EPS)
            d_on_h = d_on[:, vs]
            gnw = gnw + jnp.sum(d_on_h * o_h * rs, axis=0, keepdims=True)
            gg = d_on_h * nw
            do_parts.append(rs * gg - o_h * (rs * rs * rs) * jnp.mean(gg * o_h, axis=1, keepdims=True))
        gnw_ref[...] += gnw
        dob_ref[...] = jnp.concatenate(do_parts, axis=1).astype(BF16)

    tile = lambda w: pl.BlockSpec((TM, w), lambda i: (i, 0))
    whole = lambda shape: pl.BlockSpec(shape, lambda i: tuple(0 for _ in shape), pipeline_mode=pl.Buffered(1))
    out_shape = (
        jax.ShapeDtypeStruct((s_len, W_QA), BF16),
        jax.ShapeDtypeStruct((s_len, W_KA), BF16),
        jax.ShapeDtypeStruct((s_len, W_VA), BF16),
        jax.ShapeDtypeStruct((s_len, W_QB), F32),
        jax.ShapeDtypeStruct((s_len, W_KB), F32),
        jax.ShapeDtypeStruct((s_len, W_VB), BF16),
        jax.ShapeDtypeStruct((s_len, W_R), F32),
        jax.ShapeDtypeStruct((s_len, W_GA), BF16),
        jax.ShapeDtypeStruct((s_len, W_GA), BF16),
        jax.ShapeDtypeStruct((s_len, W_GB), BF16),
        jax.ShapeDtypeStruct((s_len, W_GB), BF16),
        jax.ShapeDtypeStruct((s_len, D_MODEL), F32),
        jax.ShapeDtypeStruct((s_len // CHUNK, GLA_DV, GLA_HEADS * GLA_DK), F32),
        jax.ShapeDtypeStruct((2, D_MODEL, HALF), F32),
        jax.ShapeDtypeStruct((1, D_MODEL), F32),
        jax.ShapeDtypeStruct((1, D_MODEL), F32),
        jax.ShapeDtypeStruct((1, GLA_DV), F32),
        jax.ShapeDtypeStruct((1, 128), F32),
    )
    out_specs = (
        tile(W_QA), tile(W_KA), tile(W_VA), tile(W_QB), tile(W_KB), tile(W_VB), tile(W_R),
        tile(W_GA), tile(W_GA), tile(W_GB), tile(W_GB), tile(D_MODEL),
        pl.BlockSpec((nch, GLA_DV, GLA_HEADS * GLA_DK), lambda i: (i, 0, 0)),
        whole((2, D_MODEL, HALF)), whole((1, D_MODEL)), whole((1, D_MODEL)), whole((1, GLA_DV)), whole((1, 128)),
    )
    in_specs = [
        tile(D_MODEL), tile(D_MODEL), pl.BlockSpec((16, TM), lambda i: (0, i)),
        whole((D_PROJ, D_MODEL)), whole((D_MODEL, D_MODEL)), whole((W_R, W_KB)), whole((1, W_KB)),
        pl.BlockSpec(memory_space=pltpu.SMEM), whole((1, GLA_DV)), whole((1, D_MODEL)), whole((1, D_MODEL)),
    ]
    scratch = [
        pltpu.VMEM((BLOCK, W_KA), BF16), pltpu.VMEM((BLOCK, W_VA), BF16),
        pltpu.VMEM((GLA_DV, GLA_HEADS * GLA_DK), F32),
        pltpu.VMEM((TM, W_GA), F32), pltpu.VMEM((TM, W_GA), F32), pltpu.VMEM((TM, W_GB), F32),
        pltpu.VMEM((TM, W_GB), F32), pltpu.VMEM((TM, D_MODEL), BF16),
    ]
    return pl.pallas_call(
        body, name="fwd_head", grid=(nt,), in_specs=in_specs, out_specs=out_specs, out_shape=out_shape,
        scratch_shapes=scratch,
        compiler_params=pltpu.CompilerParams(dimension_semantics=("arbitrary",), vmem_limit_bytes=VMEM_LIMIT),
    )(x, tgt, cs, win, wout, wgu, bg, sinks, nw, lng, lnb)


def _bwd_call(x, dh, qa, ka, va, dattn, dga, dob, dgb, qb, kb, vb, r, st, cs, win, wgu, bg, sinks):
    s_len = x.shape[0]
    nt = s_len // TM
    nblk = TM // BLOCK
    nch = TM // CHUNK

    def body(x_ref, dh_ref, qa_ref, ka_ref, va_ref, kap_ref, vap_ref, dattn_ref, dga_ref, dob_ref, dgb_ref,
             qb_ref, kb_ref, vb_ref, r_ref, st_ref, cs_ref, win_ref, wgu_ref, bg_ref, sinks_ref,
             gx_ref, dwin_ref, gsink_ref, gbg_ref, gwgu_ref,
             dproj, dk_carry, dv_carry, ds_carry, db_s, dwin_acc):
        i = pl.program_id(0)
        t = nt - 1 - i

        @pl.when(i == 0)
        def _():
            dk_carry[...] = jnp.zeros_like(dk_carry)
            dv_carry[...] = jnp.zeros_like(dv_carry)
            ds_carry[...] = jnp.zeros_like(ds_carry)
            dwin_acc[...] = jnp.zeros_like(dwin_acc)
            gsink_ref[...] = jnp.zeros_like(gsink_ref)
            gbg_ref[...] = jnp.zeros_like(gbg_ref)
            gwgu_ref[...] = jnp.zeros_like(gwgu_ref)

        cos, sa, sb = _rope_tile(cs_ref[...])
        cos4, sa4, sb4 = (jnp.concatenate([v] * 4, axis=1) for v in (cos, sa, sb))

        qa = qa_ref[...]
        ka = ka_ref[...]
        va = va_ref[...]
        dattn = dattn_ref[...]
        def attention_backward(between):
            gsink_rows = [jnp.zeros((1, 1), F32) for _ in range(Q_HEADS)]
            bias_inner = _attn_bias(True)
            sink_cols = [_sink_col(sinks_ref, j) for j in range(KV_HEADS)]
            for b in reversed(range(nblk)):
                rows = slice(b * BLOCK, (b + 1) * BLOCK)
                mask = _attn_bias(t > 0) if b == 0 else bias_inner
                k_cur = ka[rows]
                v_cur = va[rows]
                k_old = kap_ref[...] if b == 0 else ka[(b - 1) * BLOCK:b * BLOCK]
                v_old = vap_ref[...] if b == 0 else va[(b - 1) * BLOCK:b * BLOCK]
                dq_parts, dk_parts, dv_parts = [], [], []
                for j in range(KV_HEADS):
                    hs = slice(j * HEAD_A, (j + 1) * HEAD_A)
                    kc = jnp.concatenate([k_old[:, hs], k_cur[:, hs]], axis=0)
                    vc = jnp.concatenate([v_old[:, hs], v_cur[:, hs]], axis=0)
                    qs = _stack_heads(qa[rows], j)
                    do_s = _stack_heads(dattn[rows], j)
                    probs, p_sink = _softmax_block(qs, kc, mask, sink_cols[j])
                    dp = _mm_nt(do_s, vc)
                    d_row = jnp.sum(probs * dp, axis=1, keepdims=True)
                    ds16 = (probs * (dp - d_row) * ATT_SCALE).astype(BF16)
                    between()
                    dq_parts.append(_mm(ds16, kc))
                    dk_parts.append(_mm_tn(ds16, qs))
                    dv_parts.append(_mm_tn(probs.astype(BF16), do_s))
                    t_sink = d_row * p_sink
                    for g in range(GROUP):
                        gsink_rows[GROUP * j + g] = gsink_rows[GROUP * j + g] - jnp.sum(
                            t_sink[g * BLOCK:(g + 1) * BLOCK], axis=0, keepdims=True)
                dq = _rope_bwd(_unstack_heads(dq_parts), cos4[rows], sa4[rows], sb4[rows])
                dproj[rows, O_QA:O_QA + W_QA] = dq.astype(BF16)
                dk_cur = dk_carry[...] + jnp.concatenate([p[BLOCK:] for p in dk_parts], axis=1)
                dv_cur = dv_carry[...] + jnp.concatenate([p[BLOCK:] for p in dv_parts], axis=1)
                dproj[rows, O_KA:O_KA + W_KA] = _rope_bwd(dk_cur, cos[rows], sa[rows], sb[rows]).astype(BF16)
                dproj[rows, O_VA:O_VA + W_VA] = dv_cur.astype(BF16)
                dk_carry[...] = jnp.concatenate([p[:BLOCK] for p in dk_parts], axis=1)
                dv_carry[...] = jnp.concatenate([p[:BLOCK] for p in dv_parts], axis=1)
            for hq in range(Q_HEADS):
                gsink_ref[hq:hq + 1, :] += jnp.broadcast_to(gsink_rows[hq], (1, 128))

        dproj[:, O_GA:O_GA + W_GA] = dga_ref[...]
        dproj[:, O_GB:O_GB + W_GB] = dgb_ref[...]

        r16 = r_ref[...].astype(BF16)
        wgu16 = wgu_ref[...].astype(BF16)
        z = _mm(r16, wgu16) + bg_ref[...]
        log_a = _log_sigmoid(z) / GLA_TAU
        bcum = _tri_mm(_chunk_tri(TRI_SLAB, False), log_a)
        qb = qb_ref[...]
        kb = kb_ref[...]
        vb = vb_ref[...]
        dob = dob_ref[...]
        e_b = jnp.exp(bcum)
        e_nb = jnp.exp(-bcum)
        qd_f = qb * GLA_SCALE * e_b
        ki_f = kb * e_nb
        qd_all = qd_f.astype(BF16)
        ki_all = ki_f.astype(BF16)
        tril = lax.broadcasted_iota(jnp.int32, (CHUNK, CHUNK), 0) >= lax.broadcasted_iota(jnp.int32, (CHUNK, CHUNK), 1)
        last_row = lax.broadcasted_iota(jnp.int32, (CHUNK, 1), 0) == CHUNK - 1

        dsn_state = [ds_carry[...]]

        def gla_chunk(c):
            dsn = dsn_state[0]
            rows = slice(c * CHUNK, (c + 1) * CHUNK)
            b_c = bcum[rows]
            b_last = b_c[CHUNK - 1:CHUNK]
            e_e = jnp.exp(b_last - b_c)
            dec = jnp.exp(b_last)
            ke_f = kb[rows] * e_e
            ke = ke_f.astype(BF16)
            sp = st_ref[c]
            sp16 = sp.astype(BF16)
            dsn16 = dsn.astype(BF16)
            dqd_p, dki_p, dke_p, dv_p, dsp_p = [], [], [], [], []
            for h in range(GLA_HEADS):
                ks = slice(h * GLA_DK, (h + 1) * GLA_DK)
                vs = slice(h * GLA_DV, (h + 1) * GLA_DV)
                qd = qd_all[rows, ks]
                ki = ki_all[rows, ks]
                v_h = vb[rows, vs]
                do_h = dob[rows, vs]
                a16 = jnp.where(tril, _mm_nt(qd, ki), 0.0).astype(BF16)
                da16 = jnp.where(tril, _mm_nt(do_h, v_h), 0.0).astype(BF16)
                dv_p.append(_mm_tn(a16, do_h) + _mm_nt(ke[:, ks], dsn16[:, ks]))
                dqd_p.append(_mm(da16, ki) + _mm(do_h, sp16[:, ks]))
                dki_p.append(_mm_tn(da16, qd))
                dke_p.append(_mm(v_h, dsn16[:, ks]))
                dsp_p.append(_mm_tn(do_h, qd))
            dqd = jnp.concatenate(dqd_p, axis=1)
            dki = jnp.concatenate(dki_p, axis=1)
            dke = jnp.concatenate(dke_p, axis=1)
            ddec = jnp.sum(dsn * sp, axis=0, keepdims=True)
            dsn_next = dsn * dec + jnp.concatenate(dsp_p, axis=1)
            dproj[rows, O_QB:O_QB + W_QB] = (dqd * e_b[rows] * GLA_SCALE).astype(BF16)
            dproj[rows, O_KB:O_KB + W_KB] = (dki * e_nb[rows] + dke * e_e).astype(BF16)
            dproj[rows, O_VB:O_VB + W_VB] = jnp.concatenate(dv_p, axis=1).astype(BF16)
            dke_ke = dke * ke_f
            d_b = dqd * qd_f[rows] - dki * ki_f[rows] - dke_ke
            d_bl = jnp.sum(dke_ke, axis=0, keepdims=True) + ddec * dec
            db_s[rows, :] = d_b + jnp.where(last_row, d_bl, 0.0)
            dsn_state[0] = dsn_next

        chunks = list(reversed(range(nch)))
        attention_backward(lambda: gla_chunk(chunks.pop(0)) if chunks else None)
        while chunks:
            gla_chunk(chunks.pop(0))
        ds_carry[...] = dsn_state[0]
        dlog_a = _tri_mm(_chunk_tri(TRI_SLAB, True), db_s[...])
        dz = dlog_a * (1.0 / GLA_TAU) * _sigmoid(-z)
        dz16 = dz.astype(BF16)
        gbg_ref[...] += jnp.sum(dz, axis=0, keepdims=True)
        gwgu_ref[...] += _mm_tn(r16, dz16)
        dproj[:, O_R:O_R + W_R] = _mm_nt(dz16, wgu16).astype(BF16)

        dp16 = dproj[...]
        gx_ref[...] = ALPHA * dh_ref[...] + _mm(dp16, win_ref[...])
        x16 = x_ref[...].astype(BF16)
        for h in range(2):
            dwin_acc[h, 0:D_PROJ, :] += _mm_tn(dp16, x16[:, h * HALF:(h + 1) * HALF])

        @pl.when(i == nt - 1)
        def _():
            dwin_ref[...] = dwin_acc[...].astype(BF16)

    tile = lambda w: pl.BlockSpec((TM, w), lambda i: (nt - 1 - i, 0))
    whole = lambda shape: pl.BlockSpec(shape, lambda i: tuple(0 for _ in shape), pipeline_mode=pl.Buffered(1))
    prev_blk = pl.BlockSpec((BLOCK, W_KA), lambda i: (jnp.maximum((nt - 1 - i) * nblk - 1, 0), 0))
    in_specs = [
        tile(D_MODEL), tile(D_MODEL), tile(W_QA), tile(W_KA), tile(W_VA), prev_blk, prev_blk,
        tile(W_GA), tile(W_GA), tile(W_GB), tile(W_GB), tile(W_QB), tile(W_KB), tile(W_VB), tile(W_R),
        pl.BlockSpec((nch, GLA_DV, GLA_HEADS * GLA_DK), lambda i: (nt - 1 - i, 0, 0)),
        pl.BlockSpec((16, TM), lambda i: (0, nt - 1 - i)),
        whole((D_PROJ, D_MODEL)), whole((W_R, W_KB)), whole((1, W_KB)), pl.BlockSpec(memory_space=pltpu.SMEM),
    ]
    out_shape = (
        jax.ShapeDtypeStruct((s_len, D_MODEL), F32),
        jax.ShapeDtypeStruct((2, ACC_ROWS, HALF), BF16),
        jax.ShapeDtypeStruct((Q_HEADS, 128), F32),
        jax.ShapeDtypeStruct((1, W_KB), F32),
        jax.ShapeDtypeStruct((W_R, W_KB), F32),
    )
    out_specs = (tile(D_MODEL), whole((2, ACC_ROWS, HALF)), whole((Q_HEADS, 128)), whole((1, W_KB)),
                 whole((W_R, W_KB)))
    scratch = [
        pltpu.VMEM((TM, D_PROJ), BF16), pltpu.VMEM((BLOCK, W_KA), F32), pltpu.VMEM((BLOCK, W_VA), F32),
        pltpu.VMEM((GLA_DV, GLA_HEADS * GLA_DK), F32), pltpu.VMEM((TM, W_KB), F32),
        pltpu.VMEM((2, ACC_ROWS, HALF), F32),
    ]
    return pl.pallas_call(
        body, name="bwd_mix", grid=(nt,), in_specs=in_specs, out_specs=out_specs, out_shape=out_shape,
        scratch_shapes=scratch,
        compiler_params=pltpu.CompilerParams(dimension_semantics=("arbitrary",), vmem_limit_bytes=VMEM_LIMIT),
    )(x, dh, qa, ka, va, ka, va, dattn, dga, dob, dgb, qb, kb, vb, r, st, cs, win, wgu, bg, sinks)


def _mesh_place():
    x, y, c = lax.axis_index("x"), lax.axis_index("y"), lax.axis_index("c")
    chips = [(1 - x, y), (x, 1 - y), (1 - x, 1 - y)]
    return x, y, c, chips


def _gather_weights_call(w_lin, w_out, wgu):
    def body(wlin_ref, wout_ref, wgu_ref, wt_ref, wout_full, wgu_all, blk, oblk, asm, send_sems, recv_sems):
        x, y, c, chips = _mesh_place()
        k_me = 2 * x + y
        asm[SHARD_IN - 4:SHARD_PAD, :] = jnp.zeros((SHARD_PAD - SHARD_IN + 4, D_MODEL), F32)
        asm[0:SHARD_IN, :] = wlin_ref[:, 0, :]
        for h in range(2):
            blk[k_me, h] = asm[0:SHARD_PAD, h * HALF:(h + 1) * HALF].astype(BF16)
            oblk[k_me, h] = wout_ref[:, h * HALF:(h + 1) * HALF].astype(BF16)
        wgu_all[k_me] = wgu_ref[...]

        parts = ((0, GATHER_SPLIT, 0, SHARD_OUT // 2), (GATHER_SPLIT, SHARD_PAD - GATHER_SPLIT, SHARD_OUT // 2, SHARD_OUT // 2))
        me_id, sib_id = (x, y, c), (x, y, 1 - c)
        nbr_x, nbr_y, diag = ((*chip, c) for chip in chips)
        k_x, k_y, k_d = (2 * chip[0] + chip[1] for chip in chips)

        def copies(k, hc, p, sem0, to):
            if p is None:
                refs = (blk.at[k, hc], oblk.at[k, hc])
            else:
                r0, rn, o0, on = parts[p]
                refs = (blk.at[k, hc, pl.ds(r0, rn), :], oblk.at[k, hc, pl.ds(o0, on), :])
            return [pltpu.make_async_remote_copy(src_ref=ref, dst_ref=ref, send_sem=send_sems.at[sem0 + n],
                                                 recv_sem=recv_sems.at[sem0 + n], device_id=to, device_id_type=MESH)
                    for n, ref in enumerate(refs)]

        def gu_copy(k, r, to):
            return pltpu.make_async_remote_copy(src_ref=wgu_all.at[k], dst_ref=wgu_all.at[k], send_sem=send_sems.at[18 + r],
                                                recv_sem=recv_sems.at[18 + r], device_id=to, device_id_type=MESH)

        def start(cps):
            for cp in cps:
                cp.start()
            return cps

        def landed(cps):
            for cp in cps:
                cp.wait_recv()

        started = start(copies(k_me, c, 0, 0, nbr_x) + copies(k_me, c, 1, 6, nbr_y)
                        + copies(k_me, c, 1, 2, nbr_x) + copies(k_me, c, 0, 4, nbr_y)
                        + [gu_copy(k_me, r, to) for r, to in enumerate((nbr_x, nbr_y, diag))])
        landed(copies(k_x, c, 0, 0, me_id))
        started += start(copies(k_x, c, 0, 8, nbr_y))
        landed(copies(k_y, c, 1, 6, me_id))
        started += start(copies(k_y, c, 1, 10, nbr_x))
        landed(copies(k_x, c, 1, 2, me_id))
        started += start(copies(k_x, c, None, 12, sib_id))
        landed(copies(k_y, c, 0, 4, me_id))
        started += start(copies(k_y, c, None, 14, sib_id))
        landed(copies(k_d, c, 0, 8, me_id) + copies(k_d, c, 1, 10, me_id))
        started += start(copies(k_d, c, None, 16, sib_id))
        for r, k_r in enumerate((k_x, k_y, k_d)):
            landed(copies(k_r, 1 - c, None, 12 + 2 * r, me_id))
            gu_copy(k_r, r, me_id).wait_recv()
        for cp in started:
            cp.wait_send()

        for k in range(N_CHIPS):
            for h in range(2):
                asm[k * SHARD_IN:k * SHARD_IN + SHARD_PAD, h * HALF:(h + 1) * HALF] = blk[k, h].astype(F32)
                wout_full[k * SHARD_OUT:(k + 1) * SHARD_OUT, h * HALF:(h + 1) * HALF] = oblk[k, h]
        wt_ref[...] = asm[0:D_PROJ, :].astype(BF16)

    vmem = pl.BlockSpec(memory_space=pltpu.VMEM)
    return pl.pallas_call(
        body, name="gather_weights",
        out_shape=(jax.ShapeDtypeStruct((D_PROJ, D_MODEL), BF16),
                   jax.ShapeDtypeStruct((D_MODEL, D_MODEL), BF16),
                   jax.ShapeDtypeStruct((N_CHIPS, W_R, W_KB // N_CHIPS), F32)),
        in_specs=[vmem, vmem, vmem], out_specs=(vmem, vmem, vmem),
        scratch_shapes=[pltpu.VMEM((N_CHIPS, 2, SHARD_PAD, HALF), BF16), pltpu.VMEM((N_CHIPS, 2, SHARD_OUT, HALF), BF16),
                        pltpu.VMEM((ACC_ROWS, D_MODEL), F32),
                        pltpu.SemaphoreType.DMA((21,)), pltpu.SemaphoreType.DMA((21,))],
        compiler_params=pltpu.CompilerParams(vmem_limit_bytes=VMEM_LIMIT),
    )(w_lin, w_out, wgu)


def _adamw(w, g, m, v):
    m = ADAM_B1 * m + (1.0 - ADAM_B1) * g
    v = ADAM_B2 * v + (1.0 - ADAM_B2) * (g * g)
    m_hat = m / (1.0 - ADAM_B1 ** ADAM_STEP)
    v_hat = v / (1.0 - ADAM_B2 ** ADAM_STEP)
    delta = -ADAM_LR * (m_hat / (jnp.sqrt(v_hat) + ADAM_EPS) + ADAM_WD * w)
    return delta, m, v


N_SMALL = 6


def _reduce_grads_call(g_in, g_out, small_grads):
    def body(gin_hbm, gout_hbm, g_lng, g_lnb, g_bg, g_nw, g_sink, g_wgu, loss_in, lin_in, fin_out, tot_out,
             a_in, a_out, b_in, b_out, ab_in, c_in, s_in, s_out, r_in, r_out, f_in, f_out, pack_ref, tot_ref, pack_all,
             send_sems, recv_sems, local_sems):
        x, y, c, chips = _mesh_place()
        k_me = 2 * x + y
        me = 4 * x + 2 * y + c
        sibling = (x, y, 1 - c)

        pack_ref[...] = jnp.zeros_like(pack_ref)
        for a in range(8):
            pack_ref[P_LNG + a:P_LNG + a + 1, :] = g_lng[:, a * 128:(a + 1) * 128]
            pack_ref[P_LNB + a:P_LNB + a + 1, :] = g_lnb[:, a * 128:(a + 1) * 128]
        for a in range(2):
            pack_ref[P_BG + a:P_BG + a + 1, :] = g_bg[:, a * 128:(a + 1) * 128]
        pack_ref[P_NW:P_NW + 1, :] = g_nw[...]
        lane = lax.broadcasted_iota(jnp.int32, (1, 128), 1)
        sink_row = jnp.zeros((1, 128), F32)
        for hq in range(Q_HEADS):
            sink_row = jnp.where(lane == hq, g_sink[hq:hq + 1, :], sink_row)
        pack_ref[P_SINK:P_SINK + 1, :] = sink_row
        pack_ref[P_LOSS:P_LOSS + 1, :] = loss_in[...]
        gu_w = W_KB // N_CHIPS
        for k in range(N_CHIPS):
            pack_ref[P_GU + W_R * k:P_GU + W_R * (k + 1), 0:gu_w] = g_wgu[:, k * gu_w:(k + 1) * gu_w]
        pack_all[me] = pack_ref[...]
        small = []
        for mask in range(1, 8):
            peer = (x ^ (mask >> 2), y ^ ((mask >> 1) & 1), c ^ (mask & 1))
            small.append(pltpu.make_async_remote_copy(
                src_ref=pack_ref, dst_ref=pack_all.at[me], send_sem=send_sems.at[mask], recv_sem=recv_sems.at[mask],
                device_id=peer, device_id_type=MESH))
        for cp in small:
            cp.start()

        loads = [pltpu.make_async_copy(gin_hbm.at[c], a_in, local_sems.at[0]),
                 pltpu.make_async_copy(gout_hbm.at[c], a_out, local_sems.at[1])]
        to_sib = [pltpu.make_async_remote_copy(
                      src_ref=gin_hbm.at[1 - c], dst_ref=b_in,
                      send_sem=send_sems.at[8], recv_sem=recv_sems.at[8], device_id=sibling, device_id_type=MESH),
                  pltpu.make_async_remote_copy(
                      src_ref=gout_hbm.at[1 - c], dst_ref=b_out,
                      send_sem=send_sems.at[9], recv_sem=recv_sems.at[9], device_id=sibling, device_id_type=MESH)]
        for cp in (loads[1], to_sib[1], loads[0], to_sib[0]):
            cp.start()

        parts = ((0, GATHER_SPLIT, 0, SHARD_OUT // 2), (GATHER_SPLIT, SHARD_PAD - GATHER_SPLIT, SHARD_OUT // 2, SHARD_OUT // 2))
        nbr_x, nbr_y, _ = ((*chip, c) for chip in chips)
        k_x, k_y, k_d = (2 * chip[0] + chip[1] for chip in chips)

        def rows_of(p, w):
            r0, rn, o0, on = parts[p]
            return (o0, on) if w else (r0, rn)

        def mine(k, p, w):
            r0, rn = rows_of(p, w)
            if w:
                return a_out[k, pl.ds(r0, rn), :] + b_out[k, pl.ds(r0, rn), :]
            return c_in[k, pl.ds(r0, rn), :]

        def message(m, p, w, to):
            rn = rows_of(p, w)[1]
            stage, land = (s_out, r_out) if w else (s_in, r_in)
            return pltpu.make_async_remote_copy(
                src_ref=stage.at[m, pl.ds(0, rn), :], dst_ref=land.at[m, pl.ds(0, rn), :],
                send_sem=send_sems.at[10 + 2 * m + w], recv_sem=recv_sems.at[10 + 2 * m + w],
                device_id=to, device_id_type=MESH)

        def post(m, p, w, val, to):
            rn = rows_of(p, w)[1]
            stage = s_out if w else s_in
            stage[m, 0:rn, :] = val.astype(BF16)
            cp = message(m, p, w, to)
            cp.start()
            return [cp]

        def take(m, p, w):
            rn = rows_of(p, w)[1]
            message(m, p, w, (x, y, c)).wait_recv()
            land = r_out if w else r_in
            return land[m, 0:rn, :].astype(F32)

        sent = []
        for w in (1, 0):
            loads[w].wait()
            to_sib[w].wait_recv()
            if w == 0:
                ab_in[...] = a_in[...].astype(F32) + b_in[...].astype(F32)
                for k in range(N_CHIPS):
                    c_in[k] = ab_in[k * SHARD_IN:k * SHARD_IN + SHARD_PAD, :]
            sent += post(1, 0, w, mine(k_d, 0, w), nbr_x) + post(4, 1, w, mine(k_d, 1, w), nbr_y)
            sent += post(0, 0, w, mine(k_x, 0, w), nbr_x) + post(3, 1, w, mine(k_y, 1, w), nbr_y)
        for w in (1, 0):
            sent += post(5, 0, w, take(1, 0, w) + mine(k_y, 0, w), nbr_y)
            sent += post(2, 1, w, take(4, 1, w) + mine(k_x, 1, w), nbr_x)
        for w in (1, 0):
            for p, direct, summed in ((0, 0, 5), (1, 3, 2)):
                r0, rn = rows_of(p, w)
                total = mine(k_me, p, w) + take(direct, p, w) + take(summed, p, w)
                if w:
                    f_out[c, r0:r0 + rn, :] = total
                else:
                    f_in[c, r0:r0 + rn, :] = total

        swap = [pltpu.make_async_remote_copy(
                    src_ref=f_in.at[c], dst_ref=f_in.at[c],
                    send_sem=send_sems.at[22], recv_sem=recv_sems.at[22], device_id=sibling, device_id_type=MESH),
                pltpu.make_async_remote_copy(
                    src_ref=f_out.at[c], dst_ref=f_out.at[c],
                    send_sem=send_sems.at[23], recv_sem=recv_sems.at[23], device_id=sibling, device_id_type=MESH)]
        for cp in swap:
            cp.start()

        for cp in small:
            cp.wait_recv()
        total = pack_all[0]
        for d in range(1, 8):
            total = total + pack_all[d]
        tot_ref[...] = total
        tot_out[0:P_GU, :] = total[0:P_GU]
        tot_out[P_GU:PACK_OWN_ROWS, :] = tot_ref[pl.ds(pl.multiple_of(P_GU + W_R * k_me, 8), W_R), :]

        other_in = pltpu.make_async_remote_copy(
            src_ref=f_in.at[1 - c], dst_ref=f_in.at[1 - c],
            send_sem=send_sems.at[22], recv_sem=recv_sems.at[22], device_id=sibling, device_id_type=MESH)
        other_out = pltpu.make_async_remote_copy(
            src_ref=f_out.at[1 - c], dst_ref=f_out.at[1 - c],
            send_sem=send_sems.at[23], recv_sem=recv_sems.at[23], device_id=sibling, device_id_type=MESH)
        other_in.wait_recv()
        other_out.wait_recv()
        for cp in small + to_sib + sent + swap:
            cp.wait_send()

        for h in range(2):
            lin_in[:, h * HALF:(h + 1) * HALF] = f_in[h, 0:SHARD_IN, :]
            fin_out[:, h * HALF:(h + 1) * HALF] = f_out[h]

    vmem = pl.BlockSpec(memory_space=pltpu.VMEM)
    hbm = pl.BlockSpec(memory_space=pl.ANY)
    return pl.pallas_call(
        body, name="reduce_grads",
        out_shape=(jax.ShapeDtypeStruct((SHARD_IN, D_MODEL), F32), jax.ShapeDtypeStruct((SHARD_OUT, D_MODEL), F32),
                   jax.ShapeDtypeStruct((PACK_OWN_ROWS, 128), F32)),
        in_specs=[hbm, hbm] + [vmem] * 7, out_specs=(vmem,) * 3,
        scratch_shapes=[
            pltpu.VMEM((ACC_ROWS, HALF), BF16), pltpu.VMEM((N_CHIPS, SHARD_OUT, HALF), F32),
            pltpu.VMEM((ACC_ROWS, HALF), BF16), pltpu.VMEM((N_CHIPS, SHARD_OUT, HALF), F32),
            pltpu.VMEM((ACC_ROWS, HALF), F32),
            pltpu.VMEM((N_CHIPS, SHARD_PAD, HALF), F32),
            pltpu.VMEM((6, GATHER_SPLIT, HALF), BF16), pltpu.VMEM((6, SHARD_OUT // 2, HALF), BF16),
            pltpu.VMEM((6, GATHER_SPLIT, HALF), BF16), pltpu.VMEM((6, SHARD_OUT // 2, HALF), BF16),
            pltpu.VMEM((2, SHARD_PAD, HALF), F32), pltpu.VMEM((2, SHARD_OUT, HALF), F32),
            pltpu.VMEM((PACK_ROWS, 128), F32), pltpu.VMEM((PACK_ROWS, 128), F32), pltpu.VMEM((8, PACK_ROWS, 128), F32),
            pltpu.SemaphoreType.DMA((24,)), pltpu.SemaphoreType.DMA((24,)), pltpu.SemaphoreType.DMA((2,)),
        ],
        compiler_params=pltpu.CompilerParams(vmem_limit_bytes=VMEM_LIMIT),
    )(g_in, g_out, *small_grads)


def _adamw_call(g_in, w_in, m_in, v_in, g_out, w_out, m_out, v_out, tot, small_params):
    steps = 4
    rows_out = SHARD_OUT // steps
    cols = D_MODEL // steps
    gu_w = W_KB // N_CHIPS

    def body(gi, wi, mi, vi, go, wo, mo, vo, tot, *rest):
        params = rest[:3 * N_SMALL]
        gi_o, di, nmi, nvi, go_o, do, nmo, nvo, loss_out = rest[3 * N_SMALL:3 * N_SMALL + 9]
        small_out = rest[3 * N_SMALL + 9:]

        @pl.when(pl.program_id(0) == 0)
        def _():
            loss_out[...] = tot[P_LOSS:P_LOSS + 1, :]
            g_outs = small_out[0:N_SMALL]
            for a in range(8):
                g_outs[0][:, a * 128:(a + 1) * 128] = tot[P_LNG + a:P_LNG + a + 1, :]
                g_outs[1][:, a * 128:(a + 1) * 128] = tot[P_LNB + a:P_LNB + a + 1, :]
            for a in range(2):
                g_outs[2][:, a * 128:(a + 1) * 128] = tot[P_BG + a:P_BG + a + 1, :]
            g_outs[3][...] = tot[P_NW:P_NW + 1, :]
            g_outs[4][...] = tot[P_SINK:P_SINK + 1, 0:Q_HEADS]
            g_outs[5][...] = tot[P_GU:PACK_OWN_ROWS, 0:gu_w]
            for n in range(N_SMALL):
                w_ref, m_ref, v_ref = params[3 * n:3 * n + 3]
                delta, new_m, new_v = _adamw(w_ref[...], g_outs[n][...], m_ref[...], v_ref[...])
                small_out[N_SMALL + n][...] = delta
                small_out[2 * N_SMALL + n][...] = new_m
                small_out[3 * N_SMALL + n][...] = new_v

        g = gi[...]
        delta, new_m, new_v = _adamw(wi[:, 0, :], g, mi[:, 0, :], vi[:, 0, :])
        gi_o[:, 0, :] = g
        di[:, 0, :] = delta
        nmi[:, 0, :] = new_m
        nvi[:, 0, :] = new_v
        g = go[...]
        go_o[...] = g
        do[...], nmo[...], nvo[...] = _adamw(wo[...], g, mo[...], vo[...])

    t_g = pl.BlockSpec((SHARD_IN, cols), lambda i: (0, i))
    t_in = pl.BlockSpec((SHARD_IN, 1, cols), lambda i: (0, 0, i))
    t_out = pl.BlockSpec((rows_out, D_MODEL), lambda i: (i, 0))
    s_in = jax.ShapeDtypeStruct((SHARD_IN, 1, D_MODEL), F32)
    s_out = jax.ShapeDtypeStruct((SHARD_OUT, D_MODEL), F32)
    whole = lambda shape: pl.BlockSpec(shape, lambda i: tuple(0 for _ in shape))
    small_specs = [whole(p.shape) for p in small_params]
    small_shapes = [jax.ShapeDtypeStruct(p.shape, F32) for p in small_params[0::3]] * 4
    return pl.pallas_call(
        body, name="adamw", grid=(steps,),
        in_specs=[t_g] + [t_in] * 3 + [t_out] * 4 + [whole(tot.shape)] + small_specs,
        out_specs=(t_in,) * 4 + (t_out,) * 4 + (whole((1, 128)),) + tuple(small_specs[0::3] * 4),
        out_shape=(s_in,) * 4 + (s_out,) * 4 + (jax.ShapeDtypeStruct((1, 128), F32),) + tuple(small_shapes),
        compiler_params=pltpu.CompilerParams(dimension_semantics=("arbitrary",)),
    )(g_in, w_in, m_in, v_in, g_out, w_out, m_out, v_out, tot, *small_params)


def _rope_tables(positions):
    half = 8
    inv_freq = 500000.0 ** (-jnp.arange(half, dtype=F32) / half)
    ang = inv_freq[:, None] * positions.astype(F32)[None, :]
    return jnp.concatenate([jnp.cos(ang), jnp.sin(ang)], axis=0)


def kernel(x, positions, w_in, gla_w_gate_up, gla_b_gate, attn_sinks, gla_norm_w, w_out, ln_g, ln_b, loss_target, m_w_in, m_gla_w_gate_up, m_gla_b_gate, m_attn_sinks, m_gla_norm_w, m_w_out, m_ln_g, m_ln_b, v_w_in, v_gla_w_gate_up, v_gla_b_gate, v_attn_sinks, v_gla_norm_w, v_w_out, v_ln_g, v_ln_b):
    def lin3(w):
        return jnp.transpose(w, (2, 0, 1))

    def unlin(w):
        return jnp.transpose(w, (1, 2, 0))

    win, wout, wgu_all = _gather_weights_call(lin3(w_in), w_out[0], gla_w_gate_up[0])
    wgu = jnp.transpose(wgu_all, (1, 0, 2)).reshape(W_R, W_KB)
    cs = _rope_tables(positions[0])
    sinks = attn_sinks[0]

    (qa, ka, va, qb, kb, vb, r, dattn, dga, dob, dgb, dh, st, g_wout, g_lng, g_lnb, g_nw, loss) = _fwd_call(
        x[0], loss_target[0], cs, win, wout, wgu, gla_b_gate, sinks, gla_norm_w, ln_g, ln_b)
    gx, g_win, g_sink, g_bg, g_wgu = _bwd_call(
        x[0], dh, qa, ka, va, dattn, dga, dob, dgb, qb, kb, vb, r, st, cs, win, wgu, gla_b_gate, sinks)

    g_wout_by_chip = g_wout.reshape(2, N_CHIPS, SHARD_OUT, HALF)
    small_params = []
    for group in ((ln_g, m_ln_g, v_ln_g), (ln_b, m_ln_b, v_ln_b), (gla_b_gate, m_gla_b_gate, v_gla_b_gate),
                  (gla_norm_w, m_gla_norm_w, v_gla_norm_w), (attn_sinks, m_attn_sinks, v_attn_sinks)):
        small_params += list(group)
    small_params += [gla_w_gate_up[0], m_gla_w_gate_up[0], v_gla_w_gate_up[0]]
    fin_in, fin_out, tot = _reduce_grads_call(g_win, g_wout_by_chip, (g_lng, g_lnb, g_bg, g_nw, g_sink, g_wgu, loss))
    fin_in, d_in, nm_in, nv_in, fin_out, d_out, nm_out, nv_out, loss_sum, *small_out = _adamw_call(
        fin_in, lin3(w_in), lin3(m_w_in), lin3(v_w_in), fin_out, w_out[0], m_w_out[0], v_w_out[0], tot, small_params)

    def unpack(kind, big_in, big_out):
        lng_, lnb_, bg_, nw_, sink_, gu_ = small_out[kind * N_SMALL:(kind + 1) * N_SMALL]
        return (unlin(big_in), gu_[None], bg_, sink_, nw_, big_out[None], lng_, lnb_)

    loss_total = loss_sum[0, 0]
    g_s, d_s, nm_s, nv_s = 0, 1, 2, 3
    return (loss_total, gx[None], *unpack(g_s, fin_in, fin_out), *unpack(d_s, d_in, d_out),
            *unpack(nm_s, nm_in, nm_out), *unpack(nv_s, nv_in, nv_out))
```

```python
import functools

import jax
import jax.numpy as jnp
import numpy as np
from jax import lax
from jax.experimental import pallas as pl
from jax.experimental.pallas import tpu as pltpu

F32 = jnp.float32
BF16 = jnp.bfloat16
MESH = pl.DeviceIdType.MESH

D_MODEL = 1024
N_CHIPS = 4
W_QA, W_KA, W_VA, W_GA, W_QB, W_KB, W_VB, W_GB, W_R = 512, 128, 128, 512, 256, 256, 512, 512, 16
O_QA = 0
O_KA = O_QA + W_QA
O_VA = O_KA + W_KA
O_GA = O_VA + W_VA
O_QB = O_GA + W_GA
O_KB = O_QB + W_QB
O_VB = O_KB + W_KB
O_GB = O_VB + W_VB
O_R = O_GB + W_GB
D_PROJ = O_R + W_R
SHARD_IN = D_PROJ // N_CHIPS
SHARD_OUT = D_MODEL // N_CHIPS
SHARD_PAD = 720
ACC_ROWS = -(-((N_CHIPS - 1) * SHARD_IN + SHARD_PAD) // 8) * 8
HALF = D_MODEL // 2
GATHER_SPLIT = 368

HEAD_A = 64
Q_HEADS = 8
KV_HEADS = 2
GROUP = 4
BLOCK = 128
GLA_HEADS = 4
GLA_DK = 64
GLA_DV = 128
CHUNK = 64
GLA_TAU = 16.0
EPS = 1e-5
ALPHA = 2.0 ** 0.25
ATT_SCALE = HEAD_A ** -0.5
GLA_SCALE = GLA_DK ** -0.5

ADAM_LR = 0.001
ADAM_B1 = 0.9
ADAM_B2 = 0.999
ADAM_EPS = 1e-08
ADAM_WD = 0.01
ADAM_STEP = 10

TM = 256
TM_FWD = 512
TRI_SLAB = 128
VMEM_LIMIT = 56 * 1024 * 1024

P_LNG, P_LNB, P_BG, P_NW, P_SINK, P_LOSS, P_GU = 0, 8, 16, 18, 19, 20, 24
PACK_ROWS = P_GU + N_CHIPS * 16
PACK_OWN_ROWS = P_GU + 16


def _mm(a, b):
    return jnp.dot(a, b, preferred_element_type=F32)


def _mm_nt(a, b):
    return lax.dot_general(a, b, (((1,), (1,)), ((), ())), preferred_element_type=F32)


def _mm_tn(a, b):
    return lax.dot_general(a, b, (((0,), (0,)), ((), ())), preferred_element_type=F32)


def _split3(a):
    hi = a.astype(BF16)
    r1 = a - hi.astype(F32)
    mid = r1.astype(BF16)
    lo = (r1 - mid.astype(F32)).astype(BF16)
    return hi, mid, lo


def _tri_mm(tri, a):
    slab = tri.shape[0]
    hi, mid, lo = _split3(a)
    return jnp.concatenate(
        [_mm(tri, hi[s:s + slab]) + _mm(tri, mid[s:s + slab]) + _mm(tri, lo[s:s + slab])
         for s in range(0, a.shape[0], slab)], axis=0)


def _chunk_tri(n, upper):
    r = lax.broadcasted_iota(jnp.int32, (n, n), 0)
    c = lax.broadcasted_iota(jnp.int32, (n, n), 1)
    same = (r >> 6) == (c >> 6)
    order = (c >= r) if upper else (c <= r)
    return jnp.where(same & order, 1.0, 0.0).astype(BF16)


def _rope(t, cos, sa, sb):
    w = t.shape[1]
    return t * cos + pltpu.roll(t, w - 8, 1) * sa + pltpu.roll(t, 8, 1) * sb


def _rope_tile(cs):
    row = lax.broadcasted_iota(jnp.int32, (16, 128), 0)
    d = lax.broadcasted_iota(jnp.int32, (16, 128), 1) & (HEAD_A - 1)
    hit = (d & 7) == (row & 7)
    is_cos = row < 8
    lo = d < 8
    hi = (d >= 8) & (d < 16)
    pick_cos = jnp.where(hit & is_cos & (lo | hi), 1.0, 0.0).astype(BF16)
    pick_sa = jnp.where(hit & ~is_cos & lo, -1.0, 0.0).astype(BF16)
    pick_sb = jnp.where(hit & ~is_cos & hi, 1.0, 0.0).astype(BF16)
    pieces = _split3(cs)

    def spread(pick):
        return _mm_tn(pieces[0], pick) + _mm_tn(pieces[1], pick) + _mm_tn(pieces[2], pick)

    d1 = lax.broadcasted_iota(jnp.int32, (1, 128), 1) & (HEAD_A - 1)
    return spread(pick_cos) + jnp.where(d1 < 16, 0.0, 1.0), spread(pick_sa), spread(pick_sb)


def _rope_bwd(d, cos, sa, sb):
    w = d.shape[1]
    return d * cos + pltpu.roll(d * sa, 8, 1) + pltpu.roll(d * sb, w - 8, 1)


def _log_sigmoid(z):
    return jnp.minimum(z, 0.0) - jnp.log1p(jnp.exp(-jnp.abs(z)))


def _sigmoid(z):
    return 1.0 / (1.0 + jnp.exp(-z))


def _attn_bias(has_prev):
    r = lax.broadcasted_iota(jnp.int32, (GROUP * BLOCK, 2 * BLOCK), 0) & (BLOCK - 1)
    k = lax.broadcasted_iota(jnp.int32, (GROUP * BLOCK, 2 * BLOCK), 1)
    first_key = jnp.where(has_prev, 0, BLOCK)
    return jnp.where((k > r) & (k <= r + BLOCK) & (k >= first_key), 0.0, -jnp.inf)


def _sink_col(sinks_ref, j):
    r = lax.broadcasted_iota(jnp.int32, (GROUP * BLOCK, 1), 0) >> 7
    col = jnp.full((GROUP * BLOCK, 1), sinks_ref[GROUP * j], F32)
    for g in range(1, GROUP):
        col = jnp.where(r == g, sinks_ref[GROUP * j + g], col)
    return col


def _stack_heads(t, j):
    return jnp.concatenate([t[:, (GROUP * j + g) * HEAD_A:(GROUP * j + g + 1) * HEAD_A] for g in range(GROUP)], axis=0)


def _unstack_heads(parts):
    return jnp.concatenate([parts[j][g * BLOCK:(g + 1) * BLOCK] for j in range(KV_HEADS) for g in range(GROUP)], axis=1)


def _scores(qs, kc, bias):
    return _mm_nt(qs, kc) * ATT_SCALE + bias


def _softmax_block(qs, kc, bias, sink):
    return _softmax(_scores(qs, kc, bias), sink)


def _softmax(s, sink):
    m = jnp.maximum(jnp.max(s, axis=1, keepdims=True), sink)
    p = jnp.exp(s - m)
    e_sink = jnp.exp(sink - m)
    inv = 1.0 / (jnp.sum(p, axis=1, keepdims=True) + e_sink)
    return p * inv, e_sink * inv


def _fwd_call(x, tgt, cs, win, wout, wgu, bg, sinks, nw, lng, lnb):
    s_len = x.shape[0]
    TM = min(TM_FWD, s_len)
    nt = s_len // TM
    nblk = TM // BLOCK
    nch = TM // CHUNK

    def body(x_ref, t_ref, cs_ref, win_ref, wout_ref, wgu_ref, bg_ref, sinks_ref, nw_ref,
             lng_ref, lnb_ref,
             qa_ref, ka_ref, va_ref, qb_ref, kb_ref, vb_ref, r_ref, dattn_ref, dga_ref, dob_ref, dgb_ref, dh_ref,
             st_ref, dwout_ref, glng_ref, glnb_ref, gnw_ref, loss_ref,
             kprev, vprev, state, attn_s, ga_s, ob_s, gb_s, cat_s):
        i = pl.program_id(0)

        @pl.when(i == 0)
        def _():
            kprev[...] = jnp.zeros_like(kprev)
            vprev[...] = jnp.zeros_like(vprev)
            state[...] = jnp.zeros_like(state)
            dwout_ref[...] = jnp.zeros_like(dwout_ref)
            glng_ref[...] = jnp.zeros_like(glng_ref)
            glnb_ref[...] = jnp.zeros_like(glnb_ref)
            gnw_ref[...] = jnp.zeros_like(gnw_ref)
            loss_ref[...] = jnp.zeros_like(loss_ref)

        x = x_ref[...]
        xb = x.astype(BF16)

        def proj(off, width):
            return _mm_nt(xb, win_ref[off:off + width, :])

        cos, sa, sb = _rope_tile(cs_ref[...])
        cos4, sa4, sb4 = (jnp.concatenate([t] * 4, axis=1) for t in (cos, sa, sb))
        qa = _rope(proj(O_QA, W_QA), cos4, sa4, sb4).astype(BF16)
        ka = _rope(proj(O_KA, W_KA), cos, sa, sb).astype(BF16)
        va = proj(O_VA, W_VA).astype(BF16)
        qa_ref[...] = qa
        ka_ref[...] = ka
        va_ref[...] = va

        r = proj(O_R, W_R)
        r_ref[...] = r
        z = _mm(r.astype(BF16), wgu_ref[...].astype(BF16)) + bg_ref[...]
        log_a = _log_sigmoid(z) / GLA_TAU
        bcum = _tri_mm(_chunk_tri(TRI_SLAB, False), log_a)
        qb = proj(O_QB, W_QB)
        kb = proj(O_KB, W_KB)
        vb = proj(O_VB, W_VB).astype(BF16)
        qb_ref[...] = qb
        kb_ref[...] = kb
        vb_ref[...] = vb
        qd_all = (qb * GLA_SCALE * jnp.exp(bcum)).astype(BF16)
        ki_all = (kb * jnp.exp(-bcum)).astype(BF16)
        tril = lax.broadcasted_iota(jnp.int32, (CHUNK, CHUNK), 0) >= lax.broadcasted_iota(jnp.int32, (CHUNK, CHUNK), 1)
        gla_state = [state[...]]

        def gla_chunk(c):
            rows = slice(c * CHUNK, (c + 1) * CHUNK)
            st = gla_state[0]
            b_c = bcum[rows]
            b_last = b_c[CHUNK - 1:CHUNK]
            ke = (kb[rows] * jnp.exp(b_last - b_c)).astype(BF16)
            st_ref[c] = st
            st16 = st.astype(BF16)
            o_parts, u_parts = [], []
            for h in range(GLA_HEADS):
                ks = slice(h * GLA_DK, (h + 1) * GLA_DK)
                vs = slice(h * GLA_DV, (h + 1) * GLA_DV)
                qd = qd_all[rows, ks]
                v_h = vb[rows, vs]
                a = jnp.where(tril, _mm_nt(qd, ki_all[rows, ks]), 0.0)
                o_parts.append(_mm(a.astype(BF16), v_h) + _mm_nt(qd, st16[:, ks]))
                u_parts.append(_mm_tn(v_h, ke[:, ks]))
            ob_s[rows, :] = jnp.concatenate(o_parts, axis=1)
            gla_state[0] = st * jnp.exp(b_last) + jnp.concatenate(u_parts, axis=1)

        def fill_ga():
            ga_s[...] = proj(O_GA, W_GA)

        def fill_gb():
            gb_s[...] = proj(O_GB, W_GB)

        fillers = [fill_ga, fill_gb]
        chunks = list(range(nch))

        bias_inner = _attn_bias(True)
        sink_cols = [_sink_col(sinks_ref, j) for j in range(KV_HEADS)]
        for b in range(nblk):
            rows = slice(b * BLOCK, (b + 1) * BLOCK)
            mask = _attn_bias(i > 0) if b == 0 else bias_inner
            k_cur = ka[rows]
            v_cur = va[rows]
            k_old = kprev[...] if b == 0 else ka[(b - 1) * BLOCK:b * BLOCK]
            v_old = vprev[...] if b == 0 else va[(b - 1) * BLOCK:b * BLOCK]
            outs = []
            for j in range(KV_HEADS):
                hs = slice(j * HEAD_A, (j + 1) * HEAD_A)
                kc = jnp.concatenate([k_old[:, hs], k_cur[:, hs]], axis=0)
                vc = jnp.concatenate([v_old[:, hs], v_cur[:, hs]], axis=0)
                s = _scores(_stack_heads(qa[rows], j), kc, mask)
                if fillers:
                    fillers.pop(0)()
                probs, _ = _softmax(s, sink_cols[j])
                if chunks:
                    gla_chunk(chunks.pop(0))
                outs.append(_mm(probs.astype(BF16), vc))
            attn_s[rows, :] = _unstack_heads(outs)
        kprev[...] = ka[(nblk - 1) * BLOCK:]
        vprev[...] = va[(nblk - 1) * BLOCK:]
        while fillers:
            fillers.pop(0)()
        while chunks:
            gla_chunk(chunks.pop(0))
        state[...] = gla_state[0]

        ga = ga_s[...]
        sg_a = _sigmoid(ga)
        silu_a = ga * sg_a
        attn = attn_s[...]
        cat_s[:, :W_GA] = (attn * silu_a).astype(BF16)
        gb = gb_s[...]
        sg_b = _sigmoid(gb)
        silu_b = gb * sg_b
        nw = nw_ref[...]
        on_parts = []
        for h in range(GLA_HEADS):
            vs = slice(h * GLA_DV, (h + 1) * GLA_DV)
            o_h = ob_s[:, vs]
            rs = lax.rsqrt(jnp.mean(o_h * o_h, axis=1, keepdims=True) + EPS)
            on_parts.append(o_h * rs * nw)
        on = jnp.concatenate(on_parts, axis=1)
        cat_s[:, W_GA:] = (on * silu_b).astype(BF16)
        cat = cat_s[...]
        hres = ALPHA * x + _mm(cat, wout_ref[...])
        mu = jnp.mean(hres, axis=1, keepdims=True)
        hc = hres - mu
        rstd = lax.rsqrt(jnp.mean(hc * hc, axis=1, keepdims=True) + EPS)
        xhat = hc * rstd
        g_ln = lng_ref[...]
        err = xhat * g_ln + lnb_ref[...] - t_ref[...]
        loss_ref[...] += jnp.sum(err * err) * (0.5 / D_MODEL)
        dy = err * (1.0 / D_MODEL)
        glng_ref[...] += jnp.sum(dy * xhat, axis=0, keepdims=True)
        glnb_ref[...] += jnp.sum(dy, axis=0, keepdims=True)
        dxh = dy * g_ln
        dh = rstd * (dxh - jnp.mean(dxh, axis=1, keepdims=True) - xhat * jnp.mean(dxh * xhat, axis=1, keepdims=True))
        dh_ref[...] = dh
        dh16 = dh.astype(BF16)
        for h in range(2):
            dwout_ref[h] += _mm_tn(cat, dh16[:, h * HALF:(h + 1) * HALF])
        dcat = _mm_nt(dh16, wout_ref[...])

        d_a = dcat[:, :W_GA]
        dattn_ref[...] = (d_a * silu_a).astype(BF16)
        dga_ref[...] = (d_a * attn * (sg_a * (1.0 + ga * (1.0 - sg_a)))).astype(BF16)
        d_b = dcat[:, W_GA:]
        dgb_ref[...] = (d_b * on * (sg_b * (1.0 + gb * (1.0 - sg_b)))).astype(BF16)
        d_on = d_b * silu_b
        gnw = jnp.zeros((1, GLA_DV), F32)
        do_parts = []
        for h in range(GLA_HEADS):
            vs = slice(h * GLA_DV, (h + 1) * GLA_DV)
            o_h = ob_s[:, vs]
            rs = lax.rsqrt(jnp.mean(o_h * o_h, axis=1, keepdims=True) + EPS)
            d_on_h = d_on[:, vs]
            gnw = gnw + jnp.sum(d_on_h * o_h * rs, axis=0, keepdims=True)
            gg = d_on_h * nw
            do_parts.append(rs * gg - o_h * (rs * rs * rs) * jnp.mean(gg * o_h, axis=1, keepdims=True))
        gnw_ref[...] += gnw
        dob_ref[...] = jnp.concatenate(do_parts, axis=1).astype(BF16)

    tile = lambda w: pl.BlockSpec((TM, w), lambda i: (i, 0))
    whole = lambda shape: pl.BlockSpec(shape, lambda i: tuple(0 for _ in shape), pipeline_mode=pl.Buffered(1))
    out_shape = (
        jax.ShapeDtypeStruct((s_len, W_QA), BF16),
        jax.ShapeDtypeStruct((s_len, W_KA), BF16),
        jax.ShapeDtypeStruct((s_len, W_VA), BF16),
        jax.ShapeDtypeStruct((s_len, W_QB), F32),
        jax.ShapeDtypeStruct((s_len, W_KB), F32),
        jax.ShapeDtypeStruct((s_len, W_VB), BF16),
        jax.ShapeDtypeStruct((s_len, W_R), F32),
        jax.ShapeDtypeStruct((s_len, W_GA), BF16),
        jax.ShapeDtypeStruct((s_len, W_GA), BF16),
        jax.ShapeDtypeStruct((s_len, W_GB), BF16),
        jax.ShapeDtypeStruct((s_len, W_GB), BF16),
        jax.ShapeDtypeStruct((s_len, D_MODEL), F32),
        jax.ShapeDtypeStruct((s_len // CHUNK, GLA_DV, GLA_HEADS * GLA_DK), F32),
        jax.ShapeDtypeStruct((2, D_MODEL, HALF), F32),
        jax.ShapeDtypeStruct((1, D_MODEL), F32),
        jax.ShapeDtypeStruct((1, D_MODEL), F32),
        jax.ShapeDtypeStruct((1, GLA_DV), F32),
        jax.ShapeDtypeStruct((1, 128), F32),
    )
    out_specs = (
        tile(W_QA), tile(W_KA), tile(W_VA), tile(W_QB), tile(W_KB), tile(W_VB), tile(W_R),
        tile(W_GA), tile(W_GA), tile(W_GB), tile(W_GB), tile(D_MODEL),
        pl.BlockSpec((nch, GLA_DV, GLA_HEADS * GLA_DK), lambda i: (i, 0, 0)),
        whole((2, D_MODEL, HALF)), whole((1, D_MODEL)), whole((1, D_MODEL)), whole((1, GLA_DV)), whole((1, 128)),
    )
    in_specs = [
        tile(D_MODEL), tile(D_MODEL), pl.BlockSpec((16, TM), lambda i: (0, i)),
        whole((D_PROJ, D_MODEL)), whole((D_MODEL, D_MODEL)), whole((W_R, W_KB)), whole((1, W_KB)),
        pl.BlockSpec(memory_space=pltpu.SMEM), whole((1, GLA_DV)), whole((1, D_MODEL)), whole((1, D_MODEL)),
    ]
    scratch = [
        pltpu.VMEM((BLOCK, W_KA), BF16), pltpu.VMEM((BLOCK, W_VA), BF16),
        pltpu.VMEM((GLA_DV, GLA_HEADS * GLA_DK), F32),
        pltpu.VMEM((TM, W_GA), F32), pltpu.VMEM((TM, W_GA), F32), pltpu.VMEM((TM, W_GB), F32),
        pltpu.VMEM((TM, W_GB), F32), pltpu.VMEM((TM, D_MODEL), BF16),
    ]
    return pl.pallas_call(
        body, name="fwd_head", grid=(nt,), in_specs=in_specs, out_specs=out_specs, out_shape=out_shape,
        scratch_shapes=scratch,
        compiler_params=pltpu.CompilerParams(dimension_semantics=("arbitrary",), vmem_limit_bytes=VMEM_LIMIT),
    )(x, tgt, cs, win, wout, wgu, bg, sinks, nw, lng, lnb)


def _bwd_call(x, dh, qa, ka, va, dattn, dga, dob, dgb, qb, kb, vb, r, st, cs, win, wgu, bg, sinks):
    s_len = x.shape[0]
    nt = s_len // TM
    nblk = TM // BLOCK
    nch = TM // CHUNK

    def body(x_ref, dh_ref, qa_ref, ka_ref, va_ref, kap_ref, vap_ref, dattn_ref, dga_ref, dob_ref, dgb_ref,
             qb_ref, kb_ref, vb_ref, r_ref, st_ref, cs_ref, win_ref, wgu_ref, bg_ref, sinks_ref,
             gx_ref, dwin_ref, gsink_ref, gbg_ref, gwgu_ref,
             dproj, dk_carry, dv_carry, ds_carry, db_s, dwin_acc):
        i = pl.program_id(0)
        t = nt - 1 - i

        @pl.when(i == 0)
        def _():
            dk_carry[...] = jnp.zeros_like(dk_carry)
            dv_carry[...] = jnp.zeros_like(dv_carry)
            ds_carry[...] = jnp.zeros_like(ds_carry)
            dwin_acc[...] = jnp.zeros_like(dwin_acc)
            gsink_ref[...] = jnp.zeros_like(gsink_ref)
            gbg_ref[...] = jnp.zeros_like(gbg_ref)
            gwgu_ref[...] = jnp.zeros_like(gwgu_ref)

        cos, sa, sb = _rope_tile(cs_ref[...])
        cos4, sa4, sb4 = (jnp.concatenate([v] * 4, axis=1) for v in (cos, sa, sb))

        def attention_backward(between):
            gsink_rows = [jnp.zeros((1, 1), F32) for _ in range(Q_HEADS)]
            bias_inner = _attn_bias(True)
            sink_cols = [_sink_col(sinks_ref, j) for j in range(KV_HEADS)]
            for b in reversed(range(nblk)):
                rows = slice(b * BLOCK, (b + 1) * BLOCK)
                mask = _attn_bias(t > 0) if b == 0 else bias_inner
                k_cur = ka_ref[rows, :]
                v_cur = va_ref[rows, :]
                k_old = kap_ref[...] if b == 0 else ka_ref[(b - 1) * BLOCK:b * BLOCK, :]
                v_old = vap_ref[...] if b == 0 else va_ref[(b - 1) * BLOCK:b * BLOCK, :]
                dq_parts, dk_parts, dv_parts = [], [], []
                for j in range(KV_HEADS):
                    hs = slice(j * HEAD_A, (j + 1) * HEAD_A)
                    kc = jnp.concatenate([k_old[:, hs], k_cur[:, hs]], axis=0)
                    vc = jnp.concatenate([v_old[:, hs], v_cur[:, hs]], axis=0)
                    qs = _stack_heads(qa_ref[rows, :], j)
                    do_s = _stack_heads(dattn_ref[rows, :], j)
                    probs, p_sink = _softmax_block(qs, kc, mask, sink_cols[j])
                    dp = _mm_nt(do_s, vc)
                    d_row = jnp.sum(probs * dp, axis=1, keepdims=True)
                    ds16 = (probs * (dp - d_row) * ATT_SCALE).astype(BF16)
                    between()
                    dq_parts.append(_mm(ds16, kc))
                    dk_parts.append(_mm_tn(ds16, qs))
                    dv_parts.append(_mm_tn(probs.astype(BF16), do_s))
                    t_sink = d_row * p_sink
                    for g in range(GROUP):
                        gsink_rows[GROUP * j + g] = gsink_rows[GROUP * j + g] - jnp.sum(
                            t_sink[g * BLOCK:(g + 1) * BLOCK], axis=0, keepdims=True)
                dq = _rope_bwd(_unstack_heads(dq_parts), cos4[rows], sa4[rows], sb4[rows])
                dproj[rows, O_QA:O_QA + W_QA] = dq.astype(BF16)
                dk_cur = dk_carry[...] + jnp.concatenate([p[BLOCK:] for p in dk_parts], axis=1)
                dv_cur = dv_carry[...] + jnp.concatenate([p[BLOCK:] for p in dv_parts], axis=1)
                dproj[rows, O_KA:O_KA + W_KA] = _rope_bwd(dk_cur, cos[rows], sa[rows], sb[rows]).astype(BF16)
                dproj[rows, O_VA:O_VA + W_VA] = dv_cur.astype(BF16)
                dk_carry[...] = jnp.concatenate([p[:BLOCK] for p in dk_parts], axis=1)
                dv_carry[...] = jnp.concatenate([p[:BLOCK] for p in dv_parts], axis=1)
            for hq in range(Q_HEADS):
                gsink_ref[hq:hq + 1, :] += jnp.broadcast_to(gsink_rows[hq], (1, 128))

        dproj[:, O_GA:O_GA + W_GA] = dga_ref[...]
        dproj[:, O_GB:O_GB + W_GB] = dgb_ref[...]

        r16 = r_ref[...].astype(BF16)
        wgu16 = wgu_ref[...].astype(BF16)
        z = _mm(r16, wgu16) + bg_ref[...]
        log_a = _log_sigmoid(z) / GLA_TAU
        bcum = _tri_mm(_chunk_tri(TRI_SLAB, False), log_a)
        qb = qb_ref[...]
        kb = kb_ref[...]
        e_b = jnp.exp(bcum)
        e_nb = jnp.exp(-bcum)
        qd_f = qb * GLA_SCALE * e_b
        ki_f = kb * e_nb
        qd_all = qd_f.astype(BF16)
        ki_all = ki_f.astype(BF16)
        tril = lax.broadcasted_iota(jnp.int32, (CHUNK, CHUNK), 0) >= lax.broadcasted_iota(jnp.int32, (CHUNK, CHUNK), 1)
        last_row = lax.broadcasted_iota(jnp.int32, (CHUNK, 1), 0) == CHUNK - 1

        dsn_state = [ds_carry[...]]

        def gla_chunk(c):
            dsn = dsn_state[0]
            rows = slice(c * CHUNK, (c + 1) * CHUNK)
            b_c = bcum[rows]
            b_last = b_c[CHUNK - 1:CHUNK]
            e_e = jnp.exp(b_last - b_c)
            dec = jnp.exp(b_last)
            ke_f = kb[rows] * e_e
            ke = ke_f.astype(BF16)
            sp = st_ref[c]
            sp16 = sp.astype(BF16)
            dsn16 = dsn.astype(BF16)
            dqd_p, dki_p, dke_p, dv_p, dsp_p = [], [], [], [], []
            for h in range(GLA_HEADS):
                ks = slice(h * GLA_DK, (h + 1) * GLA_DK)
                vs = slice(h * GLA_DV, (h + 1) * GLA_DV)
                qd = qd_all[rows, ks]
                ki = ki_all[rows, ks]
                v_h = vb_ref[rows, vs]
                do_h = dob_ref[rows, vs]
                a16 = jnp.where(tril, _mm_nt(qd, ki), 0.0).astype(BF16)
                da16 = jnp.where(tril, _mm_nt(do_h, v_h), 0.0).astype(BF16)
                dv_p.append(_mm_tn(a16, do_h) + _mm_nt(ke[:, ks], dsn16[:, ks]))
                dqd_p.append(_mm(da16, ki) + _mm(do_h, sp16[:, ks]))
                dki_p.append(_mm_tn(da16, qd))
                dke_p.append(_mm(v_h, dsn16[:, ks]))
                dsp_p.append(_mm_tn(do_h, qd))
            dqd = jnp.concatenate(dqd_p, axis=1)
            dki = jnp.concatenate(dki_p, axis=1)
            dke = jnp.concatenate(dke_p, axis=1)
            ddec = jnp.sum(dsn * sp, axis=0, keepdims=True)
            dsn_next = dsn * dec + jnp.concatenate(dsp_p, axis=1)
            dproj[rows, O_QB:O_QB + W_QB] = (dqd * e_b[rows] * GLA_SCALE).astype(BF16)
            dproj[rows, O_KB:O_KB + W_KB] = (dki * e_nb[rows] + dke * e_e).astype(BF16)
            dproj[rows, O_VB:O_VB + W_VB] = jnp.concatenate(dv_p, axis=1).astype(BF16)
            dke_ke = dke * ke_f
            d_b = dqd * qd_f[rows] - dki * ki_f[rows] - dke_ke
            d_bl = jnp.sum(dke_ke, axis=0, keepdims=True) + ddec * dec
            db_s[rows, :] = d_b + jnp.where(last_row, d_bl, 0.0)
            dsn_state[0] = dsn_next

        chunks = list(reversed(range(nch)))
        attention_backward(lambda: gla_chunk(chunks.pop(0)) if chunks else None)
        while chunks:
            gla_chunk(chunks.pop(0))
        ds_carry[...] = dsn_state[0]
        dlog_a = _tri_mm(_chunk_tri(TRI_SLAB, True), db_s[...])
        dz = dlog_a * (1.0 / GLA_TAU) * _sigmoid(-z)
        dz16 = dz.astype(BF16)
        gbg_ref[...] += jnp.sum(dz, axis=0, keepdims=True)
        gwgu_ref[...] += _mm_tn(r16, dz16)
        dproj[:, O_R:O_R + W_R] = _mm_nt(dz16, wgu16).astype(BF16)

        dp16 = dproj[...]
        gx_ref[...] = ALPHA * dh_ref[...] + _mm(dp16, win_ref[...])
        x16 = x_ref[...].astype(BF16)
        for h in range(2):
            dwin_acc[h, 0:D_PROJ, :] += _mm_tn(dp16, x16[:, h * HALF:(h + 1) * HALF])

        @pl.when(i == nt - 1)
        def _():
            dwin_ref[...] = dwin_acc[...].astype(BF16)

    tile = lambda w: pl.BlockSpec((TM, w), lambda i: (nt - 1 - i, 0))
    whole = lambda shape: pl.BlockSpec(shape, lambda i: tuple(0 for _ in shape), pipeline_mode=pl.Buffered(1))
    prev_blk = pl.BlockSpec((BLOCK, W_KA), lambda i: (jnp.maximum((nt - 1 - i) * nblk - 1, 0), 0))
    in_specs = [
        tile(D_MODEL), tile(D_MODEL), tile(W_QA), tile(W_KA), tile(W_VA), prev_blk, prev_blk,
        tile(W_GA), tile(W_GA), tile(W_GB), tile(W_GB), tile(W_QB), tile(W_KB), tile(W_VB), tile(W_R),
        pl.BlockSpec((nch, GLA_DV, GLA_HEADS * GLA_DK), lambda i: (nt - 1 - i, 0, 0)),
        pl.BlockSpec((16, TM), lambda i: (0, nt - 1 - i)),
        whole((D_PROJ, D_MODEL)), whole((W_R, W_KB)), whole((1, W_KB)), pl.BlockSpec(memory_space=pltpu.SMEM),
    ]
    out_shape = (
        jax.ShapeDtypeStruct((s_len, D_MODEL), F32),
        jax.ShapeDtypeStruct((2, ACC_ROWS, HALF), BF16),
        jax.ShapeDtypeStruct((Q_HEADS, 128), F32),
        jax.ShapeDtypeStruct((1, W_KB), F32),
        jax.ShapeDtypeStruct((W_R, W_KB), F32),
    )
    out_specs = (tile(D_MODEL), whole((2, ACC_ROWS, HALF)), whole((Q_HEADS, 128)), whole((1, W_KB)),
                 whole((W_R, W_KB)))
    scratch = [
        pltpu.VMEM((TM, D_PROJ), BF16), pltpu.VMEM((BLOCK, W_KA), F32), pltpu.VMEM((BLOCK, W_VA), F32),
        pltpu.VMEM((GLA_DV, GLA_HEADS * GLA_DK), F32), pltpu.VMEM((TM, W_KB), F32),
        pltpu.VMEM((2, ACC_ROWS, HALF), F32),
    ]
    return pl.pallas_call(
        body, name="bwd_mix", grid=(nt,), in_specs=in_specs, out_specs=out_specs, out_shape=out_shape,
        scratch_shapes=scratch,
        compiler_params=pltpu.CompilerParams(dimension_semantics=("arbitrary",), vmem_limit_bytes=VMEM_LIMIT),
    )(x, dh, qa, ka, va, ka, va, dattn, dga, dob, dgb, qb, kb, vb, r, st, cs, win, wgu, bg, sinks)


def _mesh_place():
    x, y, c = lax.axis_index("x"), lax.axis_index("y"), lax.axis_index("c")
    chips = [(1 - x, y), (x, 1 - y), (1 - x, 1 - y)]
    return x, y, c, chips


def _gather_weights_call(w_lin, w_out, wgu):
    def body(wlin_ref, wout_ref, wgu_ref, wt_ref, wout_full, wgu_all, blk, oblk, asm, send_sems, recv_sems):
        x, y, c, chips = _mesh_place()
        k_me = 2 * x + y
        asm[SHARD_IN - 4:SHARD_PAD, :] = jnp.zeros((SHARD_PAD - SHARD_IN + 4, D_MODEL), F32)
        asm[0:SHARD_IN, :] = wlin_ref[:, 0, :]
        for h in range(2):
            blk[k_me, h] = asm[0:SHARD_PAD, h * HALF:(h + 1) * HALF].astype(BF16)
            oblk[k_me, h] = wout_ref[:, h * HALF:(h + 1) * HALF].astype(BF16)
        wgu_all[k_me] = wgu_ref[...]

        parts = ((0, GATHER_SPLIT, 0, SHARD_OUT // 2), (GATHER_SPLIT, SHARD_PAD - GATHER_SPLIT, SHARD_OUT // 2, SHARD_OUT // 2))
        me_id, sib_id = (x, y, c), (x, y, 1 - c)
        nbr_x, nbr_y, diag = ((*chip, c) for chip in chips)
        k_x, k_y, k_d = (2 * chip[0] + chip[1] for chip in chips)

        def copies(k, hc, p, sem0, to):
            if p is None:
                refs = (blk.at[k, hc], oblk.at[k, hc])
            else:
                r0, rn, o0, on = parts[p]
                refs = (blk.at[k, hc, pl.ds(r0, rn), :], oblk.at[k, hc, pl.ds(o0, on), :])
            return [pltpu.make_async_remote_copy(src_ref=ref, dst_ref=ref, send_sem=send_sems.at[sem0 + n],
                                                 recv_sem=recv_sems.at[sem0 + n], device_id=to, device_id_type=MESH)
                    for n, ref in enumerate(refs)]

        def gu_copy(k, r, to):
            return pltpu.make_async_remote_copy(src_ref=wgu_all.at[k], dst_ref=wgu_all.at[k], send_sem=send_sems.at[18 + r],
                                                recv_sem=recv_sems.at[18 + r], device_id=to, device_id_type=MESH)

        def start(cps):
            for cp in cps:
                cp.start()
            return cps

        def landed(cps):
            for cp in cps:
                cp.wait_recv()

        started = start(copies(k_me, c, 0, 0, nbr_x) + copies(k_me, c, 1, 6, nbr_y)
                        + copies(k_me, c, 1, 2, nbr_x) + copies(k_me, c, 0, 4, nbr_y)
                        + [gu_copy(k_me, r, to) for r, to in enumerate((nbr_x, nbr_y, diag))])
        landed(copies(k_x, c, 0, 0, me_id))
        started += start(copies(k_x, c, 0, 8, nbr_y))
        landed(copies(k_y, c, 1, 6, me_id))
        started += start(copies(k_y, c, 1, 10, nbr_x))
        landed(copies(k_x, c, 1, 2, me_id))
        started += start(copies(k_x, c, None, 12, sib_id))
        landed(copies(k_y, c, 0, 4, me_id))
        started += start(copies(k_y, c, None, 14, sib_id))
        landed(copies(k_d, c, 0, 8, me_id) + copies(k_d, c, 1, 10, me_id))
        started += start(copies(k_d, c, None, 16, sib_id))
        for r, k_r in enumerate((k_x, k_y, k_d)):
            landed(copies(k_r, 1 - c, None, 12 + 2 * r, me_id))
            gu_copy(k_r, r, me_id).wait_recv()
        for cp in started:
            cp.wait_send()

        for k in range(N_CHIPS):
            for h in range(2):
                asm[k * SHARD_IN:k * SHARD_IN + SHARD_PAD, h * HALF:(h + 1) * HALF] = blk[k, h].astype(F32)
                wout_full[k * SHARD_OUT:(k + 1) * SHARD_OUT, h * HALF:(h + 1) * HALF] = oblk[k, h]
        wt_ref[...] = asm[0:D_PROJ, :].astype(BF16)

    vmem = pl.BlockSpec(memory_space=pltpu.VMEM)
    return pl.pallas_call(
        body, name="gather_weights",
        out_shape=(jax.ShapeDtypeStruct((D_PROJ, D_MODEL), BF16),
                   jax.ShapeDtypeStruct((D_MODEL, D_MODEL), BF16),
                   jax.ShapeDtypeStruct((N_CHIPS, W_R, W_KB // N_CHIPS), F32)),
        in_specs=[vmem, vmem, vmem], out_specs=(vmem, vmem, vmem),
        scratch_shapes=[pltpu.VMEM((N_CHIPS, 2, SHARD_PAD, HALF), BF16), pltpu.VMEM((N_CHIPS, 2, SHARD_OUT, HALF), BF16),
                        pltpu.VMEM((ACC_ROWS, D_MODEL), F32),
                        pltpu.SemaphoreType.DMA((21,)), pltpu.SemaphoreType.DMA((21,))],
        compiler_params=pltpu.CompilerParams(vmem_limit_bytes=VMEM_LIMIT),
    )(w_lin, w_out, wgu)


def _adamw(w, g, m, v):
    m = ADAM_B1 * m + (1.0 - ADAM_B1) * g
    v = ADAM_B2 * v + (1.0 - ADAM_B2) * (g * g)
    m_hat = m / (1.0 - ADAM_B1 ** ADAM_STEP)
    v_hat = v / (1.0 - ADAM_B2 ** ADAM_STEP)
    delta = -ADAM_LR * (m_hat / (jnp.sqrt(v_hat) + ADAM_EPS) + ADAM_WD * w)
    return delta, m, v


N_SMALL = 6


def _reduce_grads_call(g_in, g_out, small_grads):
    def body(gin_hbm, gout_hbm, g_lng, g_lnb, g_bg, g_nw, g_sink, g_wgu, loss_in, lin_in, fin_out, tot_out,
             a_in, a_out, b_in, b_out, ab_in, c_in, s_in, s_out, r_in, r_out, f_in, f_out, pack_ref, tot_ref, pack_all,
             send_sems, recv_sems, local_sems):
        x, y, c, chips = _mesh_place()
        k_me = 2 * x + y
        me = 4 * x + 2 * y + c
        sibling = (x, y, 1 - c)

        pack_ref[...] = jnp.zeros_like(pack_ref)
        for a in range(8):
            pack_ref[P_LNG + a:P_LNG + a + 1, :] = g_lng[:, a * 128:(a + 1) * 128]
            pack_ref[P_LNB + a:P_LNB + a + 1, :] = g_lnb[:, a * 128:(a + 1) * 128]
        for a in range(2):
            pack_ref[P_BG + a:P_BG + a + 1, :] = g_bg[:, a * 128:(a + 1) * 128]
        pack_ref[P_NW:P_NW + 1, :] = g_nw[...]
        lane = lax.broadcasted_iota(jnp.int32, (1, 128), 1)
        sink_row = jnp.zeros((1, 128), F32)
        for hq in range(Q_HEADS):
            sink_row = jnp.where(lane == hq, g_sink[hq:hq + 1, :], sink_row)
        pack_ref[P_SINK:P_SINK + 1, :] = sink_row
        pack_ref[P_LOSS:P_LOSS + 1, :] = loss_in[...]
        gu_w = W_KB // N_CHIPS
        for k in range(N_CHIPS):
            pack_ref[P_GU + W_R * k:P_GU + W_R * (k + 1), 0:gu_w] = g_wgu[:, k * gu_w:(k + 1) * gu_w]
        pack_all[me] = pack_ref[...]
        small = []
        for mask in range(1, 8):
            peer = (x ^ (mask >> 2), y ^ ((mask >> 1) & 1), c ^ (mask & 1))
            small.append(pltpu.make_async_remote_copy(
                src_ref=pack_ref, dst_ref=pack_all.at[me], send_sem=send_sems.at[mask], recv_sem=recv_sems.at[mask],
                device_id=peer, device_id_type=MESH))
        for cp in small:
            cp.start()

        loads = [pltpu.make_async_copy(gin_hbm.at[c], a_in, local_sems.at[0]),
                 pltpu.make_async_copy(gout_hbm.at[c], a_out, local_sems.at[1])]
        to_sib = [pltpu.make_async_remote_copy(
                      src_ref=gin_hbm.at[1 - c], dst_ref=b_in,
                      send_sem=send_sems.at[8], recv_sem=recv_sems.at[8], device_id=sibling, device_id_type=MESH),
                  pltpu.make_async_remote_copy(
                      src_ref=gout_hbm.at[1 - c], dst_ref=b_out,
                      send_sem=send_sems.at[9], recv_sem=recv_sems.at[9], device_id=sibling, device_id_type=MESH)]
        for cp in (loads[1], to_sib[1], loads[0], to_sib[0]):
            cp.start()

        parts = ((0, GATHER_SPLIT, 0, SHARD_OUT // 2), (GATHER_SPLIT, SHARD_PAD - GATHER_SPLIT, SHARD_OUT // 2, SHARD_OUT // 2))
        nbr_x, nbr_y, _ = ((*chip, c) for chip in chips)
        k_x, k_y, k_d = (2 * chip[0] + chip[1] for chip in chips)

        def rows_of(p, w):
            r0, rn, o0, on = parts[p]
            return (o0, on) if w else (r0, rn)

        def mine(k, p, w):
            r0, rn = rows_of(p, w)
            if w:
                return a_out[k, pl.ds(r0, rn), :] + b_out[k, pl.ds(r0, rn), :]
            return c_in[k, pl.ds(r0, rn), :]

        def message(m, p, w, to):
            rn = rows_of(p, w)[1]
            stage, land = (s_out, r_out) if w else (s_in, r_in)
            return pltpu.make_async_remote_copy(
                src_ref=stage.at[m, pl.ds(0, rn), :], dst_ref=land.at[m, pl.ds(0, rn), :],
                send_sem=send_sems.at[10 + 2 * m + w], recv_sem=recv_sems.at[10 + 2 * m + w],
                device_id=to, device_id_type=MESH)

        def post(m, p, w, val, to):
            rn = rows_of(p, w)[1]
            stage = s_out if w else s_in
            stage[m, 0:rn, :] = val.astype(BF16)
            cp = message(m, p, w, to)
            cp.start()
            return [cp]

        def take(m, p, w):
            rn = rows_of(p, w)[1]
            message(m, p, w, (x, y, c)).wait_recv()
            land = r_out if w else r_in
            return land[m, 0:rn, :].astype(F32)

        sent = []
        for w in (1, 0):
            loads[w].wait()
            to_sib[w].wait_recv()
            if w == 0:
                ab_in[...] = a_in[...].astype(F32) + b_in[...].astype(F32)
                for k in range(N_CHIPS):
                    c_in[k] = ab_in[k * SHARD_IN:k * SHARD_IN + SHARD_PAD, :]
            sent += post(1, 0, w, mine(k_d, 0, w), nbr_x) + post(4, 1, w, mine(k_d, 1, w), nbr_y)
            sent += post(0, 0, w, mine(k_x, 0, w), nbr_x) + post(3, 1, w, mine(k_y, 1, w), nbr_y)
        for w in (1, 0):
            sent += post(5, 0, w, take(1, 0, w) + mine(k_y, 0, w), nbr_y)
            sent += post(2, 1, w, take(4, 1, w) + mine(k_x, 1, w), nbr_x)
        for w in (1, 0):
            for p, direct, summed in ((0, 0, 5), (1, 3, 2)):
                r0, rn = rows_of(p, w)
                total = mine(k_me, p, w) + take(direct, p, w) + take(summed, p, w)
                if w:
                    f_out[c, r0:r0 + rn, :] = total
                else:
                    f_in[c, r0:r0 + rn, :] = total

        swap = [pltpu.make_async_remote_copy(
                    src_ref=f_in.at[c], dst_ref=f_in.at[c],
                    send_sem=send_sems.at[22], recv_sem=recv_sems.at[22], device_id=sibling, device_id_type=MESH),
                pltpu.make_async_remote_copy(
                    src_ref=f_out.at[c], dst_ref=f_out.at[c],
                    send_sem=send_sems.at[23], recv_sem=recv_sems.at[23], device_id=sibling, device_id_type=MESH)]
        for cp in swap:
            cp.start()

        for cp in small:
            cp.wait_recv()
        total = pack_all[0]
        for d in range(1, 8):
            total = total + pack_all[d]
        tot_ref[...] = total
        tot_out[0:P_GU, :] = total[0:P_GU]
        tot_out[P_GU:PACK_OWN_ROWS, :] = tot_ref[pl.ds(pl.multiple_of(P_GU + W_R * k_me, 8), W_R), :]

        other_in = pltpu.make_async_remote_copy(
            src_ref=f_in.at[1 - c], dst_ref=f_in.at[1 - c],
            send_sem=send_sems.at[22], recv_sem=recv_sems.at[22], device_id=sibling, device_id_type=MESH)
        other_out = pltpu.make_async_remote_copy(
            src_ref=f_out.at[1 - c], dst_ref=f_out.at[1 - c],
            send_sem=send_sems.at[23], recv_sem=recv_sems.at[23], device_id=sibling, device_id_type=MESH)
        other_in.wait_recv()
        other_out.wait_recv()
        for cp in small + to_sib + sent + swap:
            cp.wait_send()

        for h in range(2):
            lin_in[:, h * HALF:(h + 1) * HALF] = f_in[h, 0:SHARD_IN, :]
            fin_out[:, h * HALF:(h + 1) * HALF] = f_out[h]

    vmem = pl.BlockSpec(memory_space=pltpu.VMEM)
    hbm = pl.BlockSpec(memory_space=pl.ANY)
    return pl.pallas_call(
        body, name="reduce_grads",
        out_shape=(jax.ShapeDtypeStruct((SHARD_IN, D_MODEL), F32), jax.ShapeDtypeStruct((SHARD_OUT, D_MODEL), F32),
                   jax.ShapeDtypeStruct((PACK_OWN_ROWS, 128), F32)),
        in_specs=[hbm, hbm] + [vmem] * 7, out_specs=(vmem,) * 3,
        scratch_shapes=[
            pltpu.VMEM((ACC_ROWS, HALF), BF16), pltpu.VMEM((N_CHIPS, SHARD_OUT, HALF), F32),
            pltpu.VMEM((ACC_ROWS, HALF), BF16), pltpu.VMEM((N_CHIPS, SHARD_OUT, HALF), F32),
            pltpu.VMEM((ACC_ROWS, HALF), F32),
            pltpu.VMEM((N_CHIPS, SHARD_PAD, HALF), F32),
            pltpu.VMEM((6, GATHER_SPLIT, HALF), BF16), pltpu.VMEM((6, SHARD_OUT // 2, HALF), BF16),
            pltpu.VMEM((6, GATHER_SPLIT, HALF), BF16), pltpu.VMEM((6, SHARD_OUT // 2, HALF), BF16),
            pltpu.VMEM((2, SHARD_PAD, HALF), F32), pltpu.VMEM((2, SHARD_OUT, HALF), F32),
            pltpu.VMEM((PACK_ROWS, 128), F32), pltpu.VMEM((PACK_ROWS, 128), F32), pltpu.VMEM((8, PACK_ROWS, 128), F32),
            pltpu.SemaphoreType.DMA((24,)), pltpu.SemaphoreType.DMA((24,)), pltpu.SemaphoreType.DMA((2,)),
        ],
        compiler_params=pltpu.CompilerParams(vmem_limit_bytes=VMEM_LIMIT),
    )(g_in, g_out, *small_grads)


def _adamw_call(g_in, w_in, m_in, v_in, g_out, w_out, m_out, v_out, tot, small_params):
    steps = 4
    rows_out = SHARD_OUT // steps
    cols = D_MODEL // steps
    gu_w = W_KB // N_CHIPS

    def body(gi, wi, mi, vi, go, wo, mo, vo, tot, *rest):
        params = rest[:3 * N_SMALL]
        gi_o, di, nmi, nvi, go_o, do, nmo, nvo, loss_out = rest[3 * N_SMALL:3 * N_SMALL + 9]
        small_out = rest[3 * N_SMALL + 9:]

        @pl.when(pl.program_id(0) == 0)
        def _():
            loss_out[...] = tot[P_LOSS:P_LOSS + 1, :]
            g_outs = small_out[0:N_SMALL]
            for a in range(8):
                g_outs[0][:, a * 128:(a + 1) * 128] = tot[P_LNG + a:P_LNG + a + 1, :]
                g_outs[1][:, a * 128:(a + 1) * 128] = tot[P_LNB + a:P_LNB + a + 1, :]
            for a in range(2):
                g_outs[2][:, a * 128:(a + 1) * 128] = tot[P_BG + a:P_BG + a + 1, :]
            g_outs[3][...] = tot[P_NW:P_NW + 1, :]
            g_outs[4][...] = tot[P_SINK:P_SINK + 1, 0:Q_HEADS]
            g_outs[5][...] = tot[P_GU:PACK_OWN_ROWS, 0:gu_w]
            for n in range(N_SMALL):
                w_ref, m_ref, v_ref = params[3 * n:3 * n + 3]
                delta, new_m, new_v = _adamw(w_ref[...], g_outs[n][...], m_ref[...], v_ref[...])
                small_out[N_SMALL + n][...] = delta
                small_out[2 * N_SMALL + n][...] = new_m
                small_out[3 * N_SMALL + n][...] = new_v

        g = gi[...]
        delta, new_m, new_v = _adamw(wi[:, 0, :], g, mi[:, 0, :], vi[:, 0, :])
        gi_o[:, 0, :] = g
        di[:, 0, :] = delta
        nmi[:, 0, :] = new_m
        nvi[:, 0, :] = new_v
        g = go[...]
        go_o[...] = g
        do[...], nmo[...], nvo[...] = _adamw(wo[...], g, mo[...], vo[...])

    t_g = pl.BlockSpec((SHARD_IN, cols), lambda i: (0, i))
    t_in = pl.BlockSpec((SHARD_IN, 1, cols), lambda i: (0, 0, i))
    t_out = pl.BlockSpec((rows_out, D_MODEL), lambda i: (i, 0))
    s_in = jax.ShapeDtypeStruct((SHARD_IN, 1, D_MODEL), F32)
    s_out = jax.ShapeDtypeStruct((SHARD_OUT, D_MODEL), F32)
    whole = lambda shape: pl.BlockSpec(shape, lambda i: tuple(0 for _ in shape))
    small_specs = [whole(p.shape) for p in small_params]
    small_shapes = [jax.ShapeDtypeStruct(p.shape, F32) for p in small_params[0::3]] * 4
    return pl.pallas_call(
        body, name="adamw", grid=(steps,),
        in_specs=[t_g] + [t_in] * 3 + [t_out] * 4 + [whole(tot.shape)] + small_specs,
        out_specs=(t_in,) * 4 + (t_out,) * 4 + (whole((1, 128)),) + tuple(small_specs[0::3] * 4),
        out_shape=(s_in,) * 4 + (s_out,) * 4 + (jax.ShapeDtypeStruct((1, 128), F32),) + tuple(small_shapes),
        compiler_params=pltpu.CompilerParams(dimension_semantics=("arbitrary",)),
    )(g_in, w_in, m_in, v_in, g_out, w_out, m_out, v_out, tot, *small_params)


def _rope_tables(positions):
    half = 8
    inv_freq = 500000.0 ** (-jnp.arange(half, dtype=F32) / half)
    ang = inv_freq[:, None] * positions.astype(F32)[None, :]
    return jnp.concatenate([jnp.cos(ang), jnp.sin(ang)], axis=0)


def kernel(x, positions, w_in, gla_w_gate_up, gla_b_gate, attn_sinks, gla_norm_w, w_out, ln_g, ln_b, loss_target, m_w_in, m_gla_w_gate_up, m_gla_b_gate, m_attn_sinks, m_gla_norm_w, m_w_out, m_ln_g, m_ln_b, v_w_in, v_gla_w_gate_up, v_gla_b_gate, v_attn_sinks, v_gla_norm_w, v_w_out, v_ln_g, v_ln_b):
    def lin3(w):
        return jnp.transpose(w, (2, 0, 1))

    def unlin(w):
        return jnp.transpose(w, (1, 2, 0))

    win, wout, wgu_all = _gather_weights_call(lin3(w_in), w_out[0], gla_w_gate_up[0])
    wgu = jnp.transpose(wgu_all, (1, 0, 2)).reshape(W_R, W_KB)
    cs = _rope_tables(positions[0])
    sinks = attn_sinks[0]

    (qa, ka, va, qb, kb, vb, r, dattn, dga, dob, dgb, dh, st, g_wout, g_lng, g_lnb, g_nw, loss) = _fwd_call(
        x[0], loss_target[0], cs, win, wout, wgu, gla_b_gate, sinks, gla_norm_w, ln_g, ln_b)
    gx, g_win, g_sink, g_bg, g_wgu = _bwd_call(
        x[0], dh, qa, ka, va, dattn, dga, dob, dgb, qb, kb, vb, r, st, cs, win, wgu, gla_b_gate, sinks)

    g_wout_by_chip = g_wout.reshape(2, N_CHIPS, SHARD_OUT, HALF)
    small_params = []
    for group in ((ln_g, m_ln_g, v_ln_g), (ln_b, m_ln_b, v_ln_b), (gla_b_gate, m_gla_b_gate, v_gla_b_gate),
                  (gla_norm_w, m_gla_norm_w, v_gla_norm_w), (attn_sinks, m_attn_sinks, v_attn_sinks)):
        small_params += list(group)
    small_params += [gla_w_gate_up[0], m_gla_w_gate_up[0], v_gla_w_gate_up[0]]
    fin_in, fin_out, tot = _reduce_grads_call(g_win, g_wout_by_chip, (g_lng, g_lnb, g_bg, g_nw, g_sink, g_wgu, loss))
    fin_in, d_in, nm_in, nv_in, fin_out, d_out, nm_out, nv_out, loss_sum, *small_out = _adamw_call(
        fin_in, lin3(w_in), lin3(m_w_in), lin3(v_w_in), fin_out, w_out[0], m_w_out[0], v_w_out[0], tot, small_params)

    def unpack(kind, big_in, big_out):
        lng_, lnb_, bg_, nw_, sink_, gu_ = small_out[kind * N_SMALL:(kind + 1) * N_SMALL]
        return (unlin(big_in), gu_[None], bg_, sink_, nw_, big_out[None], lng_, lnb_)

    loss_total = loss_sum[0, 0]
    g_s, d_s, nm_s, nv_s = 0, 1, 2, 3
    return (loss_total, gx[None], *unpack(g_s, fin_in, fin_out), *unpack(d_s, d_in, d_out),
            *unpack(nm_s, nm_in, nm_out), *unpack(nv_s, nv_in, nv_out))
```
